```python
import jax
import jax.numpy as jnp
from jax import lax
import numpy as np

D_MODEL = 1024
BATCH = 4
SEQ = 8192
DEPTH = 4

HEAD_DIM = 64
N_HEADS = D_MODEL // HEAD_DIM
MIX_WIDTH = N_HEADS * HEAD_DIM
N_MIXERS = 3
Q_BLOCK = 128
ROPE_THETA = 10000.0
EPS = 1e-6
NEG_INF = -1e30
BIG = 1e30
SCALE = HEAD_DIM ** -0.5

NSA_KV_HEADS = 4
NSA_CMP_LEN = 32
NSA_CMP_STRIDE = 16
NSA_SLC_LEN = 64
NSA_TOPK = 16
NSA_WINDOW = 512
NSA_Q_BLOCK = 64
A_IN = MIX_WIDTH + 6 * NSA_KV_HEADS * HEAD_DIM + 3 * N_HEADS + MIX_WIDTH

SWA_KV_HEADS = 2
SWA_WINDOW = 128
B_IN = MIX_WIDTH + 2 * SWA_KV_HEADS * HEAD_DIM + MIX_WIDTH

C_IN = 3 * MIX_WIDTH + N_HEADS + MIX_WIDTH

kernel_name = 'hybrid_nsa_swasink_fox_trunk'


def layer_counts():
    return tuple(len(range(m, DEPTH, N_MIXERS)) for m in range(N_MIXERS))


def rms_norm(x, gain):
    xf = x.astype(jnp.float32)
    y = xf * lax.rsqrt(jnp.mean(xf * xf, axis=-1, keepdims=True) + EPS)
    return (y * gain.astype(jnp.float32)).astype(x.dtype)


def rope(x, positions):
    half = HEAD_DIM // 2
    inv_freq = ROPE_THETA ** (-jnp.arange(half, dtype=jnp.float32) * 2.0 / HEAD_DIM)
    ang = positions.astype(jnp.float32)[:, :, None, None] * inv_freq
    cos, sin = jnp.cos(ang), jnp.sin(ang)
    xf = x.astype(jnp.float32)
    x1, x2 = xf[..., :half], xf[..., half:]
    return jnp.concatenate([x1 * cos - x2 * sin, x2 * cos + x1 * sin], axis=-1).astype(x.dtype)


def split_heads(t, n):
    b, s, _ = t.shape
    return t.reshape(b, s, n, HEAD_DIM)


def q_groups(q, g):
    b, s, h, d = q.shape
    return q.reshape(b, s, g, h // g, d).transpose(0, 2, 3, 1, 4)


def kv_groups(k):
    return k.transpose(0, 2, 1, 3)


def merge_blocks(o):
    n, b, g, r, qb, d = o.shape
    return o.transpose(1, 0, 4, 2, 3, 5).reshape(b, n * qb, g * r * d)


def masked_softmax(s, mask):
    p = jax.nn.softmax(jnp.where(mask, s, NEG_INF), axis=-1)
    return jnp.where(mask, p, 0.0)


def sink_softmax(s, mask, sink):
    s = jnp.where(mask, s, NEG_INF)
    m = jnp.maximum(jnp.max(s, axis=-1, keepdims=True), sink)
    e = jnp.exp(s - m)
    return e / (jnp.sum(e, axis=-1, keepdims=True) + jnp.exp(sink - m))


def window_attend(qb, k_pad, v_pad, q0, t, window, sink=None):
    span = qb.shape[3] + window
    kb = lax.dynamic_slice_in_dim(k_pad, q0, span, axis=2)
    vb = lax.dynamic_slice_in_dim(v_pad, q0, span, axis=2)
    s_pos = q0 - window + jnp.arange(span)
    diff = t[:, None] - s_pos[None, :]
    mask = (diff >= 0) & (diff < window) & (s_pos[None, :] >= 0)
    s = jnp.einsum('bgrqd,bgkd->bgrqk', qb, kb).astype(jnp.float32) * SCALE
    p = masked_softmax(s, mask) if sink is None else sink_softmax(s, mask, sink)
    return jnp.einsum('bgrqk,bgkd->bgrqd', p.astype(vb.dtype), vb)


def nsa_mixer(h, positions, w_in, q_gain, k_gain, cmp_pos, cmp_w1, cmp_w2, w_out):
    b, s, _ = h.shape
    g = NSA_KV_HEADS
    kvd = g * HEAD_DIM
    sizes = [MIX_WIDTH] + [kvd] * 6 + [3 * N_HEADS]
    q, kc, vc, ks, vs, kw, vw, gl, z = jnp.split(h @ w_in, np.cumsum(sizes).tolist(), axis=-1)
    q = q_groups(rope(rms_norm(split_heads(q, N_HEADS), q_gain), positions), g)

    n_cmp = (s - NSA_CMP_LEN) // NSA_CMP_STRIDE + 1
    cmp_idx = np.arange(n_cmp)[:, None] * NSA_CMP_STRIDE + np.arange(NSA_CMP_LEN)[None, :]
    kc = kv_groups(rope(split_heads(kc, g), positions))
    vc = kv_groups(split_heads(vc, g))

    def compress(t, pos_emb, w1, w2):
        blocks = t[:, :, cmp_idx] + pos_emb
        flat = blocks.reshape(b, g, n_cmp, NSA_CMP_LEN * HEAD_DIM)
        return jax.nn.gelu(flat @ w1) @ w2

    k_cmp = rms_norm(compress(kc, cmp_pos[0], cmp_w1[0], cmp_w2[0]), k_gain[0])
    v_cmp = compress(vc, cmp_pos[1], cmp_w1[1], cmp_w2[1])
    cmp_end = jnp.asarray(cmp_idx[:, -1])

    n_blk = s // NSA_SLC_LEN
    top_n = min(NSA_TOPK, n_blk)
    slc_start = np.arange(n_blk) * NSA_SLC_LEN
    ov = np.minimum(cmp_idx[:, -1][:, None], slc_start[None, :] + NSA_SLC_LEN - 1) - np.maximum(cmp_idx[:, 0][:, None], slc_start[None, :]) + 1
    overlap = jnp.asarray(np.clip(ov, 0, None) / NSA_CMP_LEN, jnp.float32)
    k_slc = kv_groups(rope(rms_norm(split_heads(ks, g), k_gain[1]), positions)).reshape(b, g, n_blk, NSA_SLC_LEN, HEAD_DIM)
    v_slc = kv_groups(split_heads(vs, g)).reshape(b, g, n_blk, NSA_SLC_LEN, HEAD_DIM)
    blk_ids = jnp.arange(n_blk)
    gather = jax.vmap(jax.vmap(lambda blocks, ids: blocks[ids]))

    pad = ((0, 0), (0, 0), (NSA_WINDOW, 0), (0, 0))
    k_win = jnp.pad(kv_groups(rope(rms_norm(split_heads(kw, g), k_gain[2]), positions)), pad)
    v_win = jnp.pad(kv_groups(split_heads(vw, g)), pad)

    def block_fn(i):
        q0 = i * NSA_Q_BLOCK
        t = q0 + jnp.arange(NSA_Q_BLOCK)
        qb = lax.dynamic_slice_in_dim(q, q0, NSA_Q_BLOCK, axis=3)
        sc = jnp.einsum('bgrqd,bgcd->bgrqc', qb, k_cmp).astype(jnp.float32) * SCALE
        pc = masked_softmax(sc, cmp_end[None, :] <= t[:, None])
        o_cmp = jnp.einsum('bgrqc,bgcd->bgrqd', pc.astype(v_cmp.dtype), v_cmp)
        imp = jnp.einsum('bgrqc,cn->bgqn', pc, overlap)
        cur = t // NSA_SLC_LEN
        forced = (blk_ids[None, :] == 0) | (blk_ids[None, :] == cur[:, None]) | (blk_ids[None, :] == cur[:, None] - 1)
        imp = jnp.where(forced, BIG, jnp.where(blk_ids[None, :] > cur[:, None], NEG_INF, imp))
        _, sel = lax.top_k(imp, top_n)
        k_sel = gather(k_slc, sel).reshape(b, g, NSA_Q_BLOCK, top_n * NSA_SLC_LEN, HEAD_DIM)
        v_sel = gather(v_slc, sel).reshape(b, g, NSA_Q_BLOCK, top_n * NSA_SLC_LEN, HEAD_DIM)
        pos_sel = (sel[..., None] * NSA_SLC_LEN + jnp.arange(NSA_SLC_LEN)).reshape(b, g, NSA_Q_BLOCK, top_n * NSA_SLC_LEN)
        ss = jnp.einsum('bgrqd,bgqkd->bgrqk', qb, k_sel).astype(jnp.float32) * SCALE
        ps = masked_softmax(ss, (pos_sel <= t[:, None])[:, :, None])
        o_slc = jnp.einsum('bgrqk,bgqkd->bgrqd', ps.astype(v_sel.dtype), v_sel)
        o_win = window_attend(qb, k_win, v_win, q0, t, NSA_WINDOW)
        return o_cmp, o_slc, o_win

    o_cmp, o_slc, o_win = lax.map(block_fn, jnp.arange(s // NSA_Q_BLOCK))
    gates = jax.nn.sigmoid(gl.astype(jnp.float32)).reshape(b, s, 3, N_HEADS, 1).astype(h.dtype)
    o = (gates[:, :, 0] * merge_blocks(o_cmp).reshape(b, s, N_HEADS, HEAD_DIM)
         + gates[:, :, 1] * merge_blocks(o_slc).reshape(b, s, N_HEADS, HEAD_DIM)
         + gates[:, :, 2] * merge_blocks(o_win).reshape(b, s, N_HEADS, HEAD_DIM))
    return (o.reshape(b, s, MIX_WIDTH) * jax.nn.silu(z)) @ w_out


def swa_sink_mixer(h, positions, w_in, q_gain, k_gain, sinks, w_out):
    b, s, _ = h.shape
    g = SWA_KV_HEADS
    kvd = g * HEAD_DIM
    q, k, v, z = jnp.split(h @ w_in, np.cumsum([MIX_WIDTH, kvd, kvd]).tolist(), axis=-1)
    q = q_groups(rope(rms_norm(split_heads(q, N_HEADS), q_gain), positions), g)
    pad = ((0, 0), (0, 0), (SWA_WINDOW, 0), (0, 0))
    k = jnp.pad(kv_groups(rope(rms_norm(split_heads(k, g), k_gain), positions)), pad)
    v = jnp.pad(kv_groups(split_heads(v, g)), pad)
    sink = sinks.astype(jnp.float32).reshape(1, g, N_HEADS // g, 1, 1)

    def block_fn(i):
        q0 = i * Q_BLOCK
        t = q0 + jnp.arange(Q_BLOCK)
        qb = lax.dynamic_slice_in_dim(q, q0, Q_BLOCK, axis=3)
        return window_attend(qb, k, v, q0, t, SWA_WINDOW, sink)

    o = merge_blocks(lax.map(block_fn, jnp.arange(s // Q_BLOCK)))
    return (o * jax.nn.silu(z)) @ w_out


def fox_mixer(h, w_in, forget_bias, q_gain, k_gain, w_out):
    b, s, _ = h.shape
    q, k, v, fl, z = jnp.split(h @ w_in, np.cumsum([MIX_WIDTH] * 3 + [N_HEADS]).tolist(), axis=-1)
    q = q_groups(rms_norm(split_heads(q, N_HEADS), q_gain), N_HEADS)
    k = kv_groups(rms_norm(split_heads(k, N_HEADS), k_gain))
    v = kv_groups(split_heads(v, N_HEADS))
    log_f = jax.nn.log_sigmoid(fl.astype(jnp.float32) + forget_bias.astype(jnp.float32))
    cum = jnp.cumsum(log_f, axis=1).transpose(0, 2, 1)
    s_pos = jnp.arange(s)

    def block_fn(i):
        q0 = i * Q_BLOCK
        t = q0 + jnp.arange(Q_BLOCK)
        qb = lax.dynamic_slice_in_dim(q, q0, Q_BLOCK, axis=3)
        cq = lax.dynamic_slice_in_dim(cum, q0, Q_BLOCK, axis=2)
        decay = cq[:, :, None, :, None] - cum[:, :, None, None, :]
        sc = jnp.einsum('bgrqd,bgkd->bgrqk', qb, k).astype(jnp.float32) * SCALE + decay
        p = masked_softmax(sc, s_pos[None, :] <= t[:, None])
        return jnp.einsum('bgrqk,bgkd->bgrqd', p.astype(v.dtype), v)

    o = merge_blocks(lax.map(block_fn, jnp.arange(s // Q_BLOCK)))
    return (o * jax.nn.silu(z)) @ w_out


def setup_inputs(seed: int = 0) -> dict:
    key = jax.random.key(seed)
    keys = iter(jax.random.split(key, 32))
    n_a, n_b, n_c = layer_counts()

    def nrm(shape, scale):
        return scale * jax.random.normal(next(keys), shape, jnp.float32)

    def gain(shape):
        return 1.0 + nrm(shape, 0.02)

    cmp_flat = NSA_CMP_LEN * HEAD_DIM
    return {
        'x': nrm((BATCH, SEQ, D_MODEL), 1.0),
        'positions': jnp.broadcast_to(jnp.arange(SEQ, dtype=jnp.int32), (BATCH, SEQ)),
        'norm_gains': gain((DEPTH, D_MODEL)),
        'a_w_in': nrm((n_a, D_MODEL, A_IN), D_MODEL ** -0.5),
        'a_q_gain': gain((n_a, HEAD_DIM)),
        'a_k_gain': gain((n_a, 3, HEAD_DIM)),
        'a_cmp_pos': nrm((n_a, 2, NSA_CMP_LEN, HEAD_DIM), 0.1),
        'a_cmp_w1': nrm((n_a, 2, cmp_flat, HEAD_DIM), cmp_flat ** -0.5),
        'a_cmp_w2': nrm((n_a, 2, HEAD_DIM, HEAD_DIM), HEAD_DIM ** -0.5),
        'a_w_out': nrm((n_a, MIX_WIDTH, D_MODEL), MIX_WIDTH ** -0.5),
        'b_w_in': nrm((n_b, D_MODEL, B_IN), D_MODEL ** -0.5),
        'b_q_gain': gain((n_b, HEAD_DIM)),
        'b_k_gain': gain((n_b, HEAD_DIM)),
        'b_sinks': nrm((n_b, N_HEADS), 0.5),
        'b_w_out': nrm((n_b, MIX_WIDTH, D_MODEL), MIX_WIDTH ** -0.5),
        'c_w_in': nrm((n_c, D_MODEL, C_IN), D_MODEL ** -0.5),
        'c_forget_bias': 4.0 + nrm((n_c, N_HEADS), 0.5),
        'c_q_gain': gain((n_c, HEAD_DIM)),
        'c_k_gain': gain((n_c, HEAD_DIM)),
        'c_w_out': nrm((n_c, MIX_WIDTH, D_MODEL), MIX_WIDTH ** -0.5),
    }


def reference(x, positions, norm_gains, a_w_in, a_q_gain, a_k_gain, a_cmp_pos, a_cmp_w1, a_cmp_w2, a_w_out,
              b_w_in, b_q_gain, b_k_gain, b_sinks, b_w_out,
              c_w_in, c_forget_bias, c_q_gain, c_k_gain, c_w_out):
    for i in range(DEPTH):
        j = i // N_MIXERS
        h = rms_norm(x, norm_gains[i])
        mixer = i % N_MIXERS
        if mixer == 0:
            y = nsa_mixer(h, positions, a_w_in[j], a_q_gain[j], a_k_gain[j], a_cmp_pos[j],
                          a_cmp_w1[j], a_cmp_w2[j], a_w_out[j])
        elif mixer == 1:
            y = swa_sink_mixer(h, positions, b_w_in[j], b_q_gain[j], b_k_gain[j], b_sinks[j], b_w_out[j])
        else:
            y = fox_mixer(h, c_w_in[j], c_forget_bias[j], c_q_gain[j], c_k_gain[j], c_w_out[j])
        x = x + y
    return x
```

```python
import functools

import jax
import jax.numpy as jnp
import numpy as np
from jax import lax
from jax.experimental import pallas as pl
from jax.experimental.pallas import tpu as pltpu

D_MODEL = 1024
HEAD_DIM = 64
N_HEADS = 16
MIX_WIDTH = N_HEADS * HEAD_DIM
ROPE_THETA = 10000.0
EPS = 1e-6
SCALE = HEAD_DIM ** -0.5
NEG_INF = -1e30
BIG = 1e30
M_INIT = -1e29
SEL_OFF = -(2.0 ** 100)

NSA_KV_HEADS = 4
NSA_CMP_LEN = 32
NSA_CMP_STRIDE = 16
NSA_SLC_LEN = 64
NSA_TOPK = 16
NSA_WINDOW = 512
SWA_KV_HEADS = 2
SWA_WINDOW = 128

LANES = 128
KEY_PAD = 128
VMEM_LIMIT = 56 * 1024 * 1024

TOK_TILE = 512
NT_DIMS = (((1,), (1,)), ((), ()))

BF = jnp.bfloat16
F32 = jnp.float32


def _params(sem):
    return pltpu.CompilerParams(dimension_semantics=sem, vmem_limit_bytes=VMEM_LIMIT)


def _dot(a, b):
    return jnp.dot(a, b, preferred_element_type=F32)


def _dot_nt(a, b):
    return lax.dot_general(a, b, NT_DIMS, preferred_element_type=F32)


def _rope_tab_kernel(pos_ref, invf_ref, cos_ref, sin_ref):
    ang = invf_ref[...] * pos_ref[0].astype(F32)
    cos_ref[0] = jnp.cos(ang)
    sin_ref[0] = jnp.sin(ang)


def _rope_tables(positions):
    b, s = positions.shape
    half = HEAD_DIM // 2
    inv_freq = ROPE_THETA ** (-jnp.arange(half, dtype=F32) * 2.0 / HEAD_DIM)
    tm = min(TOK_TILE, s)
    out = jax.ShapeDtypeStruct((b, half, s), F32)
    return pl.pallas_call(
        _rope_tab_kernel,
        grid=(b, s // tm),
        in_specs=[pl.BlockSpec((1, 1, tm), lambda i, j: (i, 0, j)),
                  pl.BlockSpec((half, 1), lambda i, j: (0, 0))],
        out_specs=[pl.BlockSpec((1, half, tm), lambda i, j: (i, 0, j))] * 2,
        out_shape=[out, out],
        compiler_params=_params(("parallel", "parallel")),
        name="rope_tables",
    )(positions.reshape(b, 1, s), inv_freq.reshape(half, 1))


def _rms_rows(x, gain_row):
    y = x * lax.rsqrt(jnp.mean(x * x, axis=-1, keepdims=True) + EPS)
    return y * gain_row


def _prenorm_kernel(x_ref, g_ref, h_ref):
    h_ref[0] = _rms_rows(x_ref[0], g_ref[...]).astype(BF)


def _prenorm(x, gain):
    b, s, d = x.shape
    tm = min(TOK_TILE, s)
    return pl.pallas_call(
        _prenorm_kernel,
        grid=(b, s // tm),
        in_specs=[pl.BlockSpec((1, tm, d), lambda i, j: (i, j, 0)),
                  pl.BlockSpec((1, d), lambda i, j: (0, 0))],
        out_specs=pl.BlockSpec((1, tm, d), lambda i, j: (i, j, 0)),
        out_shape=jax.ShapeDtypeStruct((b, s, d), BF),
        compiler_params=_params(("parallel", "parallel")),
        name="prenorm",
    )(x, gain.reshape(1, d))


def _head_rms_t(y, gain_col):
    ms = jnp.mean(y * y, axis=0, keepdims=True)
    return (y * lax.rsqrt(ms + EPS)) * gain_col


def _rope_t(y, cos, sin):
    half = HEAD_DIM // 2
    x1, x2 = y[:half], y[half:]
    return jnp.concatenate([x1 * cos - x2 * sin, x2 * cos + x1 * sin], axis=0)


def _to_token_major(y):
    pad = jnp.zeros((KEY_PAD - y.shape[0], y.shape[1]), y.dtype)
    return jnp.concatenate([y, pad], axis=0).T


def _silu(z):
    return z * (1.0 / (1.0 + jnp.exp(-z)))


def _sigmoid(z):
    return 1.0 / (1.0 + jnp.exp(-z))


def _store_lane_tiles(out_ref, y, tile):
    for c in range(y.shape[1] // tile):
        out_ref[0, c] = y[:, c * tile:(c + 1) * tile]


def _nsa_proj_kernel(h_ref, cos_ref, sin_ref, wq_ref, wk_ref, wv_ref, wg_ref, wz_ref,
                     qg_ref, kg_ref,
                     q_out, kc_out, vc_out, ks_out, kw_out, vs_out, vw_out, g_out, z_out,
                     y_ref, *, slc_tile, win_tile):
    h = h_ref[0]
    cos, sin = cos_ref[0], sin_ref[0]
    g = NSA_KV_HEADS
    kvd = g * HEAD_DIM
    y_ref[...] = _dot_nt(wq_ref[...], h)
    for hd in range(N_HEADS):
        rows = slice(hd * HEAD_DIM, (hd + 1) * HEAD_DIM)
        y = _rope_t(_head_rms_t(y_ref[rows, :], qg_ref[...]), cos, sin) * SCALE
        q_out[0, rows, :] = y.astype(BF)
    y_ref[0:3 * kvd, :] = _dot_nt(wk_ref[...], h)
    for kind in range(3):
        for gi in range(g):
            r0 = (kind * g + gi) * HEAD_DIM
            y = y_ref[r0:r0 + HEAD_DIM, :]
            if kind > 0:
                y = _head_rms_t(y, kg_ref[:, kind:kind + 1])
            yt = _to_token_major(_rope_t(y, cos, sin))
            if kind == 0:
                kc_out[0, gi] = yt[:, :HEAD_DIM]
            elif kind == 1:
                ks_out[0, gi] = yt.astype(BF)
            else:
                kw_out[0, gi] = yt.astype(BF)
    y_ref[0:3 * kvd, :] = _dot_nt(wv_ref[...], h)
    for gi in range(g):
        r0 = gi * HEAD_DIM
        vc_out[0, gi] = _to_token_major(y_ref[r0:r0 + HEAD_DIM, :])[:, :HEAD_DIM]
    _store_lane_tiles(vs_out, y_ref[kvd:2 * kvd, :].astype(BF), slc_tile)
    _store_lane_tiles(vw_out, y_ref[2 * kvd:3 * kvd, :].astype(BF), win_tile)
    g_out[0] = _sigmoid(_dot_nt(wg_ref[...], h))
    z_out[0] = _silu(_dot(h, wz_ref[...])).astype(BF)


def _full(shape):
    nd = len(shape)
    return pl.BlockSpec(shape, lambda i, j, _n=nd: (0,) * _n)


def _nsa_proj(h, cos, sin, w_in, q_gain, k_gain, *, slc_tile, win_tile):
    b, s, d = h.shape
    g = NSA_KV_HEADS
    kvd = g * HEAD_DIM
    tm = min(TOK_TILE, s)
    sizes = [MIX_WIDTH] + [kvd] * 6 + [3 * N_HEADS]
    off = np.cumsum([0] + sizes)
    wt = w_in.T.astype(BF)
    wq = wt[off[0]:off[1]]
    wk = jnp.concatenate([wt[off[1]:off[2]], wt[off[3]:off[4]], wt[off[5]:off[6]]], axis=0)
    wv = jnp.concatenate([wt[off[2]:off[3]], wt[off[4]:off[5]], wt[off[6]:off[7]]], axis=0)
    wg = wt[off[7]:off[8]]
    wz = w_in[:, off[8]:].astype(BF)
    qg = q_gain.reshape(HEAD_DIM, 1)
    kg = k_gain.T
    n_t = s // tm
    out_shape = [
        jax.ShapeDtypeStruct((b, MIX_WIDTH, s), BF),
        jax.ShapeDtypeStruct((b, g, s, HEAD_DIM), F32),
        jax.ShapeDtypeStruct((b, g, s, HEAD_DIM), F32),
        jax.ShapeDtypeStruct((b, g, s, KEY_PAD), BF),
        jax.ShapeDtypeStruct((b, g, s, KEY_PAD), BF),
        jax.ShapeDtypeStruct((b, s // slc_tile, kvd, slc_tile), BF),
        jax.ShapeDtypeStruct((b, s // win_tile, kvd, win_tile), BF),
        jax.ShapeDtypeStruct((b, 3 * N_HEADS, s), F32),
        jax.ShapeDtypeStruct((b, s, MIX_WIDTH), BF),
    ]
    out_specs = [
        pl.BlockSpec((1, MIX_WIDTH, tm), lambda i, j: (i, 0, j)),
        pl.BlockSpec((1, g, tm, HEAD_DIM), lambda i, j: (i, 0, j, 0)),
        pl.BlockSpec((1, g, tm, HEAD_DIM), lambda i, j: (i, 0, j, 0)),
        pl.BlockSpec((1, g, tm, KEY_PAD), lambda i, j: (i, 0, j, 0)),
        pl.BlockSpec((1, g, tm, KEY_PAD), lambda i, j: (i, 0, j, 0)),
        pl.BlockSpec((1, tm // slc_tile, kvd, slc_tile), lambda i, j: (i, j, 0, 0)),
        pl.BlockSpec((1, tm // win_tile, kvd, win_tile), lambda i, j: (i, j, 0, 0)),
        pl.BlockSpec((1, 3 * N_HEADS, tm), lambda i, j: (i, 0, j)),
        pl.BlockSpec((1, tm, MIX_WIDTH), lambda i, j: (i, j, 0)),
    ]
    in_specs = [
        pl.BlockSpec((1, tm, d), lambda i, j: (i, j, 0)),
        pl.BlockSpec((1, HEAD_DIM // 2, tm), lambda i, j: (i, 0, j)),
        pl.BlockSpec((1, HEAD_DIM // 2, tm), lambda i, j: (i, 0, j)),
        _full(wq.shape), _full(wk.shape), _full(wv.shape), _full(wg.shape), _full(wz.shape),
        _full(qg.shape), _full(kg.shape),
    ]
    return pl.pallas_call(
        functools.partial(_nsa_proj_kernel, slc_tile=slc_tile, win_tile=win_tile),
        grid=(b, n_t), in_specs=in_specs, out_specs=out_specs, out_shape=out_shape,
        scratch_shapes=[pltpu.VMEM((MIX_WIDTH, tm), F32)],
        compiler_params=_params(("parallel", "parallel")),
        name="nsa_proj",
    )(h, cos, sin, wq, wk, wv, wg, wz, qg, kg)


def _gelu_tanh(x):
    c = np.float32(np.sqrt(2.0 / np.pi))
    return 0.5 * x * (1.0 + jnp.tanh(c * (x + 0.044715 * (x * x * x))))


def _compress_kernel(kc_ref, vc_ref, pos_ref, w1_ref, w2_ref, kg_ref, kcmp_out, vcmp_out):
    for which, (src, dst) in enumerate(((kc_ref, kcmp_out), (vc_ref, vcmp_out))):
        x = src[0, 0]
        n = x.shape[0]
        half = x.shape[1]
        xa = (x + pos_ref[which, 0:1, :]).astype(BF)
        xb = (x + pos_ref[which, 1:2, :]).astype(BF)
        ua = _dot(xa, w1_ref[which, :half, :])
        ub = _dot(xb, w1_ref[which, half:, :])
        row = lax.broadcasted_iota(jnp.int32, (n, 1), 0)
        ub_next = jnp.where(row == n - 1, 0.0, pltpu.roll(ub, n - 1, 0))
        hid = _gelu_tanh(ua + ub_next)
        y = _dot(hid.astype(BF), w2_ref[which])
        if which == 0:
            y = _rms_rows(y, kg_ref[...])
            dst[0, 0] = y.astype(BF)
        else:
            pad = jnp.zeros((n, KEY_PAD - HEAD_DIM), F32)
            dst[0, 0] = jnp.concatenate([y, pad], axis=1).T[:HEAD_DIM].astype(BF)


def _compress(kc_tok, vc_tok, cmp_pos, cmp_w1, cmp_w2, k_gain0):
    b, g, s, _ = kc_tok.shape
    n_chunk = s // NSA_CMP_STRIDE
    flat = NSA_CMP_STRIDE * HEAD_DIM
    kc = kc_tok.reshape(b, g, n_chunk, flat)
    vc = vc_tok.reshape(b, g, n_chunk, flat)
    pos = cmp_pos.reshape(2, 2, flat)
    w1 = cmp_w1.astype(BF)
    w2 = cmp_w2.astype(BF)
    blk = pl.BlockSpec((1, 1, n_chunk, flat), lambda i, j: (i, j, 0, 0))
    return pl.pallas_call(
        _compress_kernel,
        grid=(b, g),
        in_specs=[blk, blk, _full(pos.shape), _full(w1.shape), _full(w2.shape),
                  _full((1, HEAD_DIM))],
        out_specs=[pl.BlockSpec((1, 1, n_chunk, HEAD_DIM), lambda i, j: (i, j, 0, 0)),
                   pl.BlockSpec((1, 1, HEAD_DIM, n_chunk), lambda i, j: (i, j, 0, 0))],
        out_shape=[jax.ShapeDtypeStruct((b, g, n_chunk, HEAD_DIM), BF),
                   jax.ShapeDtypeStruct((b, g, HEAD_DIM, n_chunk), BF)],
        compiler_params=_params(("parallel", "parallel")),
        name="nsa_compress",
    )(kc, vc, pos, w1, w2, k_gain0.reshape(1, HEAD_DIM))


def _stack_heads(q_ref, n_heads, dk):
    return jnp.concatenate([q_ref[0, r * dk:(r + 1) * dk, :] for r in range(n_heads)], axis=1)


def _unstack_store(o, out_ref, n_heads, tq):
    rows = jnp.concatenate([o[:, r * tq:(r + 1) * tq] for r in range(n_heads)], axis=0)
    out_ref[0] = rows.T.astype(out_ref.dtype)


def _gate_row(gate_ref, n_heads):
    return jnp.concatenate([gate_ref[0, 0, 0, r:r + 1, :] for r in range(n_heads)], axis=1)


def _cmp_select_kernel(q_ref, kc_ref, vc_ref, ov_ref, gate_ref, o_out, sel_out, *, tq, n_blk):
    r_heads = N_HEADS // NSA_KV_HEADS
    nq = r_heads * tq
    q0 = pl.program_id(2) * tq
    q4 = _stack_heads(q_ref, r_heads, HEAD_DIM)
    kc = kc_ref[0, 0]
    n_chunk = kc.shape[0]
    s = _dot(kc, q4)
    t_row = q0 + (lax.broadcasted_iota(jnp.int32, (1, nq), 1) & (tq - 1))
    c_col = lax.broadcasted_iota(jnp.int32, (n_chunk, 1), 0)
    n_cmp = n_chunk - 1
    valid = (c_col * NSA_CMP_STRIDE + (NSA_CMP_LEN - 1) <= t_row) & (c_col < n_cmp)
    s = jnp.where(valid, s, NEG_INF)
    m = jnp.max(s, axis=0, keepdims=True)
    e = jnp.where(valid, jnp.exp(s - m), 0.0)
    l = jnp.sum(e, axis=0, keepdims=True)
    inv = jnp.where(l > 0.0, 1.0 / jnp.where(l > 0.0, l, 1.0), 0.0)
    p = e * inv
    o = _dot(vc_ref[0, 0], p.astype(BF))
    o = o * _gate_row(gate_ref, r_heads)
    _unstack_store(o, o_out, r_heads, tq)

    psum = p[:, 0:tq]
    for r in range(1, r_heads):
        psum = psum + p[:, r * tq:(r + 1) * tq]
    p_hi = psum.astype(BF)
    p_lo = (psum - p_hi.astype(F32)).astype(BF)
    ov = ov_ref[...]
    imp = _dot(ov, p_hi) + _dot(ov, p_lo)

    t1 = q0 + lax.broadcasted_iota(jnp.int32, (1, tq), 1)
    cur = lax.shift_right_logical(t1, int(np.log2(NSA_SLC_LEN)))
    blk = lax.broadcasted_iota(jnp.int32, (n_blk, tq), 0)
    forced = (blk == 0) | (blk == cur) | (blk == cur - 1)
    imp = jnp.where(forced, BIG, jnp.where(blk > cur, NEG_INF, imp))
    for _ in range(min(NSA_TOPK, n_blk)):
        best = jnp.max(imp, axis=0, keepdims=True)
        first = jnp.min(jnp.where(imp == best, blk, n_blk), axis=0, keepdims=True)
        imp = jnp.where(blk == first, -jnp.inf, imp)
    sel_out[0, 0] = jnp.where(imp == -jnp.inf, 0.0, SEL_OFF).astype(BF)


def _overlap_matrix(s):
    n_chunk = s // NSA_CMP_STRIDE
    n_blk = s // NSA_SLC_LEN
    c0 = np.arange(n_chunk) * NSA_CMP_STRIDE
    c1 = c0 + NSA_CMP_LEN - 1
    b0 = np.arange(n_blk) * NSA_SLC_LEN
    ov = np.minimum(c1[None, :], b0[:, None] + NSA_SLC_LEN - 1) - np.maximum(c0[None, :], b0[:, None]) + 1
    return jnp.asarray(np.clip(ov, 0, None) / NSA_CMP_LEN, BF)


def _cmp_select(qT, k_cmp, v_cmpT, gates5, *, tq):
    b, _, s = qT.shape
    g = NSA_KV_HEADS
    r_heads = N_HEADS // g
    n_chunk = k_cmp.shape[2]
    n_blk = s // NSA_SLC_LEN
    ov = _overlap_matrix(s)
    rows = r_heads * HEAD_DIM
    return pl.pallas_call(
        functools.partial(_cmp_select_kernel, tq=tq, n_blk=n_blk),
        grid=(b, g, s // tq),
        in_specs=[
            pl.BlockSpec((1, rows, tq), lambda i, j, k: (i, j, k)),
            pl.BlockSpec((1, 1, n_chunk, HEAD_DIM), lambda i, j, k: (i, j, 0, 0)),
            pl.BlockSpec((1, 1, HEAD_DIM, n_chunk), lambda i, j, k: (i, j, 0, 0)),
            pl.BlockSpec((n_blk, n_chunk), lambda i, j, k: (0, 0)),
            pl.BlockSpec((1, 1, 1, r_heads, tq), lambda i, j, k: (i, 0, j, 0, k)),
        ],
        out_specs=[pl.BlockSpec((1, tq, rows), lambda i, j, k: (i, k, j)),
                   pl.BlockSpec((1, 1, n_blk, tq), lambda i, j, k: (i, j, 0, k))],
        out_shape=[jax.ShapeDtypeStruct((b, s, MIX_WIDTH), BF),
                   jax.ShapeDtypeStruct((b, g, n_blk, s), BF)],
        compiler_params=_params(("parallel", "parallel", "parallel")),
        name="nsa_cmp_select",
    )(qT, k_cmp, v_cmpT, ov, gates5)


def _flash_kernel(*refs, mode, n_par, n_rep, dk, tq, tk, window, has_bias, has_sink, has_gate):
    it = iter(refs)
    q_ref, k_ref, v_ref = next(it), next(it), next(it)
    e_ref = next(it) if has_bias else None
    sb_ref = next(it) if has_bias else None
    sink_ref = next(it) if has_sink else None
    gate_ref = next(it) if has_gate else None
    out_ref = next(it)
    qs_ref, m_ref, l_ref, acc_ref = next(it), next(it), next(it), next(it)
    sb2_ref = next(it) if has_bias else None

    nq = n_rep * tq
    grp = pl.program_id(1)
    qi = pl.program_id(2)
    q0 = qi * tq
    for p in range(n_par):
        for r in range(n_rep):
            hd = p * n_rep + r
            qs_ref[p, :, r * tq:(r + 1) * tq] = q_ref[0, hd * dk:(hd + 1) * dk, :]
        if has_sink:
            m_ref[p] = jnp.concatenate(
                [jnp.full((1, tq), sink_ref[(grp * n_par + p) * n_rep + r], F32) for r in range(n_rep)], axis=1)
            l_ref[p] = jnp.ones((1, nq), F32)
        else:
            m_ref[p] = jnp.full((1, nq), M_INIT, F32)
            l_ref[p] = jnp.zeros((1, nq), F32)
        acc_ref[p] = jnp.zeros((HEAD_DIM, nq), F32)
    if has_bias:
        sb = sb_ref[0, 0]
        sb2_ref[...] = jnp.concatenate([sb] * n_rep, axis=1)

    t_row = q0 + (lax.broadcasted_iota(jnp.int32, (1, nq), 1) & (tq - 1))

    def step(j, masked):
        for p in range(n_par):
            k = k_ref[0, p, j]
            if dk < KEY_PAD:
                k = k[:, :dk]
            s = _dot(k, qs_ref[p])
            if has_bias:
                s = s + _dot(e_ref[j], sb2_ref[...])
            if masked:
                key = j * tk + lax.broadcasted_iota(jnp.int32, (tk, 1), 0)
                ok = key <= t_row
                if window is not None:
                    ok = ok & (key > t_row - window)
                s = jnp.where(ok, s, NEG_INF)
            m_old = m_ref[p]
            m_new = jnp.maximum(m_old, jnp.max(s, axis=0, keepdims=True))
            pr = jnp.exp(s - m_new)
            alpha = jnp.exp(m_old - m_new)
            l_ref[p] = alpha * l_ref[p] + jnp.sum(pr, axis=0, keepdims=True)
            v = v_ref[0, j, p * HEAD_DIM:(p + 1) * HEAD_DIM, :]
            acc_ref[p] = alpha * acc_ref[p] + _dot(v, pr.astype(BF))
            m_ref[p] = m_new

    if mode == "causal":
        n_full = q0 // tk

        def body(j, carry):
            step(j, False)
            return carry

        lax.fori_loop(0, n_full, body, 0)
        step(n_full, True)
    else:
        last = q0 // tk
        first = jnp.maximum(last - window // tk, 0)

        def body(j, carry):
            step(j, True)
            return carry

        lax.fori_loop(first, last + 1, body, 0)

    outs = []
    for p in range(n_par):
        o = acc_ref[p] * (1.0 / l_ref[p])
        if has_gate:
            o = o * _gate_row(gate_ref, n_rep)
        outs += [o[:, r * tq:(r + 1) * tq] for r in range(n_rep)]
    out_ref[0] = jnp.concatenate(outs, axis=0).T.astype(out_ref.dtype)


def _flash(qT, k_tok, vT_tiles, *, mode, n_par, n_rep, dk, tq, tk, window=None,
           sel_bias=None, sinks=None, gates=None, gate_branch=0, name="flash"):
    b, _, s = qT.shape
    kh = k_tok.shape[1]
    n_t = s // tk
    k5 = k_tok.reshape(b, kh, n_t, tk, KEY_PAD)
    n_grp = kh // n_par
    heads = n_par * n_rep
    nq = n_rep * tq
    has_bias, has_sink, has_gate = sel_bias is not None, sinks is not None, gates is not None
    args = [qT, k5, vT_tiles]
    in_specs = [
        pl.BlockSpec((1, heads * dk, tq), lambda i, j, k: (i, j, k)),
        pl.BlockSpec((1, n_par, n_t, tk, KEY_PAD), lambda i, j, k: (i, j, 0, 0, 0)),
        pl.BlockSpec((1, n_t, n_par * HEAD_DIM, tk), lambda i, j, k: (i, 0, j, 0)),
    ]
    scratch = [pltpu.VMEM((n_par, dk, nq), BF), pltpu.VMEM((n_par, 1, nq), F32),
               pltpu.VMEM((n_par, 1, nq), F32), pltpu.VMEM((n_par, HEAD_DIM, nq), F32)]
    if has_bias:
        n_blk = sel_bias.shape[2]
        key_blk = (np.arange(s) // NSA_SLC_LEN)[:, None] == np.arange(n_blk)[None, :]
        e = jnp.asarray(key_blk, BF).reshape(n_t, tk, n_blk)
        args += [e, sel_bias]
        in_specs += [pl.BlockSpec((n_t, tk, n_blk), lambda i, j, k: (0, 0, 0)),
                     pl.BlockSpec((1, 1, n_blk, tq), lambda i, j, k: (i, j, 0, k))]
        scratch.append(pltpu.VMEM((n_blk, nq), BF))
    if has_sink:
        args.append(sinks.astype(F32))
        in_specs.append(pl.BlockSpec(memory_space=pltpu.SMEM))
    if has_gate:
        args.append(gates)
        in_specs.append(pl.BlockSpec((1, 1, 1, n_rep, tq),
                                     lambda i, j, k, _br=gate_branch: (i, _br, j, 0, k)))
    kern = functools.partial(_flash_kernel, mode=mode, n_par=n_par, n_rep=n_rep, dk=dk, tq=tq, tk=tk,
                             window=window, has_bias=has_bias, has_sink=has_sink, has_gate=has_gate)
    return pl.pallas_call(
        kern,
        grid=(b, n_grp, s // tq),
        in_specs=in_specs,
        out_specs=pl.BlockSpec((1, tq, heads * HEAD_DIM), lambda i, j, k: (i, k, j)),
        out_shape=jax.ShapeDtypeStruct((b, s, MIX_WIDTH), BF),
        scratch_shapes=scratch,
        compiler_params=_params(("parallel", "parallel", "arbitrary")),
        name=name,
    )(*args)


def _swa_proj_kernel(h_ref, cos_ref, sin_ref, wq_ref, wk_ref, wv_ref, wz_ref, qg_ref, kg_ref,
                     q_out, k_out, v_out, z_out, y_ref, *, win_tile):
    h = h_ref[0]
    cos, sin = cos_ref[0], sin_ref[0]
    y_ref[...] = _dot_nt(wq_ref[...], h)
    for hd in range(N_HEADS):
        rows = slice(hd * HEAD_DIM, (hd + 1) * HEAD_DIM)
        q_out[0, rows, :] = (_rope_t(_head_rms_t(y_ref[rows, :], qg_ref[...]), cos, sin) * SCALE).astype(BF)
    kvd = SWA_KV_HEADS * HEAD_DIM
    y_ref[0:kvd, :] = _dot_nt(wk_ref[...], h)
    for gi in range(SWA_KV_HEADS):
        rows = slice(gi * HEAD_DIM, (gi + 1) * HEAD_DIM)
        k_out[0, gi] = _to_token_major(_rope_t(_head_rms_t(y_ref[rows, :], kg_ref[...]), cos, sin)).astype(BF)
    _store_lane_tiles(v_out, _dot_nt(wv_ref[...], h).astype(BF), win_tile)
    z_out[0] = _silu(_dot(h, wz_ref[...])).astype(BF)


def _swa_proj(h, cos, sin, w_in, q_gain, k_gain, *, win_tile):
    b, s, d = h.shape
    g = SWA_KV_HEADS
    kvd = g * HEAD_DIM
    tm = min(TOK_TILE, s)
    wt = w_in.T.astype(BF)
    wq, wk, wv = wt[:MIX_WIDTH], wt[MIX_WIDTH:MIX_WIDTH + kvd], wt[MIX_WIDTH + kvd:MIX_WIDTH + 2 * kvd]
    wz = w_in[:, MIX_WIDTH + 2 * kvd:].astype(BF)
    qg, kg = q_gain.reshape(HEAD_DIM, 1), k_gain.reshape(HEAD_DIM, 1)
    wpt = tm // win_tile
    half = HEAD_DIM // 2
    return pl.pallas_call(
        functools.partial(_swa_proj_kernel, win_tile=win_tile),
        grid=(b, s // tm),
        in_specs=[pl.BlockSpec((1, tm, d), lambda i, j: (i, j, 0)),
                  pl.BlockSpec((1, half, tm), lambda i, j: (i, 0, j)),
                  pl.BlockSpec((1, half, tm), lambda i, j: (i, 0, j)),
                  _full(wq.shape), _full(wk.shape), _full(wv.shape), _full(wz.shape),
                  _full(qg.shape), _full(kg.shape)],
        out_specs=[pl.BlockSpec((1, MIX_WIDTH, tm), lambda i, j: (i, 0, j)),
                   pl.BlockSpec((1, g, tm, KEY_PAD), lambda i, j: (i, 0, j, 0)),
                   pl.BlockSpec((1, wpt, kvd, win_tile), lambda i, j: (i, j, 0, 0)),
                   pl.BlockSpec((1, tm, MIX_WIDTH), lambda i, j: (i, j, 0))],
        out_shape=[jax.ShapeDtypeStruct((b, MIX_WIDTH, s), BF),
                   jax.ShapeDtypeStruct((b, g, s, KEY_PAD), BF),
                   jax.ShapeDtypeStruct((b, s // win_tile, kvd, win_tile), BF),
                   jax.ShapeDtypeStruct((b, s, MIX_WIDTH), BF)],
        scratch_shapes=[pltpu.VMEM((MIX_WIDTH, tm), F32)],
        compiler_params=_params(("parallel", "parallel")),
        name="swa_proj",
    )(h, cos, sin, wq, wk, wv, wz, qg, kg)


def _split3(x):
    hi = x.astype(BF)
    r1 = x - hi.astype(F32)
    mid = r1.astype(BF)
    lo = (r1 - mid.astype(F32)).astype(BF)
    return hi, mid, lo


def _forget_cum_kernel(h_ref, wf_ref, bias_ref, tri_ref, cum_out, carry_ref):
    @pl.when(pl.program_id(1) == 0)
    def _():
        carry_ref[...] = jnp.zeros_like(carry_ref)

    x = _dot_nt(wf_ref[...], h_ref[0]) + bias_ref[...]
    logf = jnp.minimum(x, 0.0) - jnp.log(1.0 + jnp.exp(-jnp.abs(x)))
    tri = tri_ref[...]
    hi, mid, lo = _split3(logf)
    cum = (_dot(hi, tri) + _dot(mid, tri)) + _dot(lo, tri) + carry_ref[:, 0:1]
    cum_out[0] = cum
    carry_ref[...] = jnp.broadcast_to(cum[:, -1:], carry_ref.shape)


def _forget_cum(h, wf, bias):
    b, s, d = h.shape
    tm = min(TOK_TILE, s)
    tri = jnp.asarray(np.arange(tm)[:, None] <= np.arange(tm)[None, :], BF)
    return pl.pallas_call(
        _forget_cum_kernel,
        grid=(b, s // tm),
        in_specs=[pl.BlockSpec((1, tm, d), lambda i, j: (i, j, 0)),
                  _full(wf.shape), _full((N_HEADS, 1)), _full(tri.shape)],
        out_specs=pl.BlockSpec((1, N_HEADS, tm), lambda i, j: (i, 0, j)),
        out_shape=jax.ShapeDtypeStruct((b, N_HEADS, s), F32),
        scratch_shapes=[pltpu.VMEM((N_HEADS, LANES), F32)],
        compiler_params=_params(("parallel", "arbitrary")),
        name="fox_forget_cum",
    )(h, wf, bias.reshape(N_HEADS, 1).astype(F32), tri)


def _fox_proj_kernel(h_ref, cum_ref, wq_ref, wk_ref, wv_ref, wz_ref, qg_ref, kg_ref,
                     q_out, k_out, v_out, z_out, yq_ref, yk_ref, *, tile):
    h = h_ref[0]
    tm = h.shape[0]
    yq_ref[...] = _dot_nt(wq_ref[...], h)
    yk_ref[...] = _dot_nt(wk_ref[...], h)
    row = lax.broadcasted_iota(jnp.int32, (8, tm), 0)
    zeros = jnp.zeros((KEY_PAD - HEAD_DIM - 16, tm), F32)
    for hd in range(N_HEADS):
        rows = slice(hd * HEAD_DIM, (hd + 1) * HEAD_DIM)
        c_hi, c_mid, c_lo = (c.astype(F32) for c in _split3(cum_ref[0, hd:hd + 1, :]))
        c3 = jnp.where(row == 0, c_hi, jnp.where(row == 1, c_mid, jnp.where(row == 2, c_lo, 0.0)))
        one3 = jnp.where(row < 3, 1.0, 0.0)
        q = _head_rms_t(yq_ref[rows, :], qg_ref[...]) * SCALE
        q_out[0, hd] = jnp.concatenate([q, one3, c3, zeros], axis=0).astype(BF)
        k = _head_rms_t(yk_ref[rows, :], kg_ref[...])
        k_out[0, hd] = jnp.concatenate([k, -c3, one3, zeros], axis=0).T.astype(BF)
    _store_lane_tiles(v_out, _dot_nt(wv_ref[...], h).astype(BF), tile)
    z_out[0] = _silu(_dot(h, wz_ref[...])).astype(BF)


def _fox_proj(h, cum, w_in, q_gain, k_gain, *, tile):
    b, s, d = h.shape
    tm = min(TOK_TILE, s)
    wt = w_in.T.astype(BF)
    wq, wk, wv = wt[:MIX_WIDTH], wt[MIX_WIDTH:2 * MIX_WIDTH], wt[2 * MIX_WIDTH:3 * MIX_WIDTH]
    wz = w_in[:, 3 * MIX_WIDTH + N_HEADS:].astype(BF)
    qg, kg = q_gain.reshape(HEAD_DIM, 1), k_gain.reshape(HEAD_DIM, 1)
    return pl.pallas_call(
        functools.partial(_fox_proj_kernel, tile=tile),
        grid=(b, s // tm),
        in_specs=[pl.BlockSpec((1, tm, d), lambda i, j: (i, j, 0)),
                  pl.BlockSpec((1, N_HEADS, tm), lambda i, j: (i, 0, j)),
                  _full(wq.shape), _full(wk.shape), _full(wv.shape), _full(wz.shape),
                  _full(qg.shape), _full(kg.shape)],
        out_specs=[pl.BlockSpec((1, N_HEADS, KEY_PAD, tm), lambda i, j: (i, 0, 0, j)),
                   pl.BlockSpec((1, N_HEADS, tm, KEY_PAD), lambda i, j: (i, 0, j, 0)),
                   pl.BlockSpec((1, tm // tile, MIX_WIDTH, tile), lambda i, j: (i, j, 0, 0)),
                   pl.BlockSpec((1, tm, MIX_WIDTH), lambda i, j: (i, j, 0))],
        out_shape=[jax.ShapeDtypeStruct((b, N_HEADS, KEY_PAD, s), BF),
                   jax.ShapeDtypeStruct((b, N_HEADS, s, KEY_PAD), BF),
                   jax.ShapeDtypeStruct((b, s // tile, MIX_WIDTH, tile), BF),
                   jax.ShapeDtypeStruct((b, s, MIX_WIDTH), BF)],
        scratch_shapes=[pltpu.VMEM((MIX_WIDTH, tm), F32), pltpu.VMEM((MIX_WIDTH, tm), F32)],
        compiler_params=_params(("parallel", "parallel")),
        name="fox_proj",
    )(h, cum, wq, wk, wv, wz, qg, kg)


def _out_proj_kernel(*refs, n_o, has_next):
    o_refs = refs[:n_o]
    z_ref, x_ref, w_ref = refs[n_o:n_o + 3]
    rest = refs[n_o + 3:]
    if has_next:
        g_ref, x_out, h_out = rest
    else:
        (x_out,) = rest
    o = o_refs[0][0].astype(F32)
    for r in o_refs[1:]:
        o = o + r[0].astype(F32)
    y = _dot((o * z_ref[0].astype(F32)).astype(BF), w_ref[...])
    x_new = x_ref[0] + y
    x_out[0] = x_new
    if has_next:
        h_out[0] = _rms_rows(x_new, g_ref[...]).astype(BF)


def _out_proj(o_list, zs, x, w_out, next_gain):
    b, s, d = x.shape
    tm = min(TOK_TILE, s)
    has_next = next_gain is not None
    blk = pl.BlockSpec((1, tm, d), lambda i, j: (i, j, 0))
    args = list(o_list) + [zs, x, w_out.astype(BF)]
    in_specs = [blk] * (len(o_list) + 2) + [_full(w_out.shape)]
    out_shape = [jax.ShapeDtypeStruct((b, s, d), F32)]
    out_specs = [blk]
    if has_next:
        args.append(next_gain.reshape(1, d))
        in_specs.append(_full((1, d)))
        out_shape.append(jax.ShapeDtypeStruct((b, s, d), BF))
        out_specs.append(blk)
    res = pl.pallas_call(
        functools.partial(_out_proj_kernel, n_o=len(o_list), has_next=has_next),
        grid=(b, s // tm), in_specs=in_specs, out_specs=out_specs, out_shape=out_shape,
        compiler_params=_params(("parallel", "parallel")),
        name="out_proj",
    )(*args)
    return (res[0], res[1]) if has_next else (res[0], None)


NSA_Q_TILE = 128
NSA_SLC_TILE = 256
NSA_WIN_TILE = 128
SWA_TILE = 128
FOX_Q_TILE = 256
FOX_K_TILE = 256


def _nsa_mixer(h, cos, sin, w_in, q_gain, k_gain, cmp_pos, cmp_w1, cmp_w2):
    b, s, _ = h.shape
    g = NSA_KV_HEADS
    r = N_HEADS // g
    slc_tile = min(NSA_SLC_TILE, s)
    qT, kc_tok, vc_tok, ks, kw, vsT, vwT, gates, zs = _nsa_proj(
        h, cos, sin, w_in, q_gain, k_gain, slc_tile=slc_tile, win_tile=NSA_WIN_TILE)
    k_cmp, v_cmpT = _compress(kc_tok, vc_tok, cmp_pos, cmp_w1, cmp_w2, k_gain[0])
    gates5 = gates.reshape(b, 3, g, r, s)
    o_cmp, sel_bias = _cmp_select(qT, k_cmp, v_cmpT, gates5, tq=NSA_Q_TILE)
    o_slc = _flash(qT, ks, vsT, mode="causal", n_par=1, n_rep=r, dk=HEAD_DIM, tq=NSA_Q_TILE,
                   tk=slc_tile, sel_bias=sel_bias, gates=gates5, gate_branch=1, name="nsa_selected")
    o_win = _flash(qT, kw, vwT, mode="window", n_par=1, n_rep=r, dk=HEAD_DIM, tq=NSA_Q_TILE,
                   tk=NSA_WIN_TILE, window=NSA_WINDOW, gates=gates5, gate_branch=2, name="nsa_window")
    return [o_cmp, o_slc, o_win], zs


def _swa_mixer(h, cos, sin, w_in, q_gain, k_gain, sinks):
    r = N_HEADS // SWA_KV_HEADS
    qT, k, vT, zs = _swa_proj(h, cos, sin, w_in, q_gain, k_gain, win_tile=SWA_TILE)
    o = _flash(qT, k, vT, mode="window", n_par=1, n_rep=r, dk=HEAD_DIM, tq=SWA_TILE, tk=SWA_TILE,
               window=SWA_WINDOW, sinks=sinks, name="swa_window")
    return [o], zs


def _fox_mixer(h, w_in, forget_bias, q_gain, k_gain):
    b, s, _ = h.shape
    wf = w_in[:, 3 * MIX_WIDTH:3 * MIX_WIDTH + N_HEADS].T.astype(BF)
    cum = _forget_cum(h, wf, forget_bias)
    tk = min(FOX_K_TILE, s)
    qT, k, vT, zs = _fox_proj(h, cum, w_in, q_gain, k_gain, tile=tk)
    o = _flash(qT.reshape(b, N_HEADS * KEY_PAD, s), k, vT, mode="causal", n_par=2, n_rep=1, dk=KEY_PAD,
               tq=min(FOX_Q_TILE, s), tk=tk, name="fox_attention")
    return [o], zs


def kernel(x, positions, norm_gains, a_w_in, a_q_gain, a_k_gain, a_cmp_pos, a_cmp_w1, a_cmp_w2, a_w_out,
           b_w_in, b_q_gain, b_k_gain, b_sinks, b_w_out,
           c_w_in, c_forget_bias, c_q_gain, c_k_gain, c_w_out):
    depth = norm_gains.shape[0]
    cos, sin = _rope_tables(positions)
    h = _prenorm(x, norm_gains[0])
    for i in range(depth):
        j, mixer = divmod(i, 3)
        if mixer == 0:
            o_list, zs = _nsa_mixer(h, cos, sin, a_w_in[j], a_q_gain[j], a_k_gain[j],
                                    a_cmp_pos[j], a_cmp_w1[j], a_cmp_w2[j])
            w_out = a_w_out[j]
        elif mixer == 1:
            o_list, zs = _swa_mixer(h, cos, sin, b_w_in[j], b_q_gain[j], b_k_gain[j], b_sinks[j])
            w_out = b_w_out[j]
        else:
            o_list, zs = _fox_mixer(h, c_w_in[j], c_forget_bias[j], c_q_gain[j], c_k_gain[j])
            w_out = c_w_out[j]
        next_gain = norm_gains[i + 1] if i + 1 < depth else None
        x, h = _out_proj(o_list, zs, x, w_out, next_gain)
    return x
```

```python
import functools

import jax
import jax.numpy as jnp
import numpy as np
from jax import lax
from jax.experimental import pallas as pl
from jax.experimental.pallas import tpu as pltpu

D_MODEL = 1024
HEAD_DIM = 64
N_HEADS = 16
MIX_WIDTH = N_HEADS * HEAD_DIM
ROPE_THETA = 10000.0
EPS = 1e-6
SCALE = HEAD_DIM ** -0.5
NEG_INF = -1e30
BIG = 1e30
M_INIT = -1e29
SEL_OFF = -(2.0 ** 100)

NSA_KV_HEADS = 4
NSA_CMP_LEN = 32
NSA_CMP_STRIDE = 16
NSA_SLC_LEN = 64
NSA_TOPK = 16
NSA_WINDOW = 512
SWA_KV_HEADS = 2
SWA_WINDOW = 128

LOG2E = float(np.log2(np.e))
Q_SCALE = SCALE * LOG2E

LANES = 128
KEY_PAD = 128
SLC_KEY_LANES = 256
V_ROWS = 80
VMEM_LIMIT = 56 * 1024 * 1024

TOK_TILE = 512
NT_DIMS = (((1,), (1,)), ((), ()))

BF = jnp.bfloat16
F32 = jnp.float32


def _params(sem):
    return pltpu.CompilerParams(dimension_semantics=sem, vmem_limit_bytes=VMEM_LIMIT)


def _dot(a, b):
    return jnp.dot(a, b, preferred_element_type=F32)


def _dot_nt(a, b):
    return lax.dot_general(a, b, NT_DIMS, preferred_element_type=F32)


def _rope_tab_kernel(pos_ref, invf_ref, cos_ref, sin_ref):
    ang = invf_ref[...] * pos_ref[0].astype(F32)
    cos_ref[0] = jnp.cos(ang)
    sin_ref[0] = jnp.sin(ang)


def _rope_tables(positions):
    b, s = positions.shape
    half = HEAD_DIM // 2
    inv_freq = ROPE_THETA ** (-jnp.arange(half, dtype=F32) * 2.0 / HEAD_DIM)
    tm = min(TOK_TILE, s)
    out = jax.ShapeDtypeStruct((b, half, s), F32)
    return pl.pallas_call(
        _rope_tab_kernel,
        grid=(b, s // tm),
        in_specs=[pl.BlockSpec((1, 1, tm), lambda i, j: (i, 0, j)),
                  pl.BlockSpec((half, 1), lambda i, j: (0, 0))],
        out_specs=[pl.BlockSpec((1, half, tm), lambda i, j: (i, 0, j))] * 2,
        out_shape=[out, out],
        compiler_params=_params(("parallel", "parallel")),
        name="rope_tables",
    )(positions.reshape(b, 1, s), inv_freq.reshape(half, 1))


def _rms_rows(x, gain_row):
    y = x * lax.rsqrt(jnp.mean(x * x, axis=-1, keepdims=True) + EPS)
    return y * gain_row


def _prenorm_kernel(x_ref, g_ref, h_ref):
    h_ref[0] = _rms_rows(x_ref[0], g_ref[...]).astype(BF)


def _prenorm(x, gain):
    b, s, d = x.shape
    tm = min(TOK_TILE, s)
    return pl.pallas_call(
        _prenorm_kernel,
        grid=(b, s // tm),
        in_specs=[pl.BlockSpec((1, tm, d), lambda i, j: (i, j, 0)),
                  pl.BlockSpec((1, d), lambda i, j: (0, 0))],
        out_specs=pl.BlockSpec((1, tm, d), lambda i, j: (i, j, 0)),
        out_shape=jax.ShapeDtypeStruct((b, s, d), BF),
        compiler_params=_params(("parallel", "parallel")),
        name="prenorm",
    )(x, gain.reshape(1, d))


def _head_rms_t(y, gain_col):
    ms = jnp.mean(y * y, axis=0, keepdims=True)
    return (y * lax.rsqrt(ms + EPS)) * gain_col


def _rope_t(y, cos, sin):
    half = HEAD_DIM // 2
    x1, x2 = y[:half], y[half:]
    return jnp.concatenate([x1 * cos - x2 * sin, x2 * cos + x1 * sin], axis=0)


def _to_token_major(y):
    pad = jnp.zeros((KEY_PAD - y.shape[0], y.shape[1]), y.dtype)
    return jnp.concatenate([y, pad], axis=0).T


def _silu(z):
    return z * (1.0 / (1.0 + jnp.exp(-z)))


def _sigmoid(z):
    return 1.0 / (1.0 + jnp.exp(-z))


def _store_lane_tiles(out_ref, y, tile):
    for c in range(y.shape[1] // tile):
        out_ref[0, c] = y[:, c * tile:(c + 1) * tile]


def _nsa_proj_kernel(h_ref, cos_ref, sin_ref, wq_ref, wk_ref, wv_ref, wg_ref, wz_ref,
                     qg_ref, kg_ref,
                     q_out, kc_out, vc_out, ks_out, kw_out, vs_out, vw_out, g_out, z_out,
                     y_ref, *, slc_tile, win_tile):
    h = h_ref[0]
    cos, sin = cos_ref[0], sin_ref[0]
    g = NSA_KV_HEADS
    kvd = g * HEAD_DIM
    y_ref[...] = _dot_nt(wq_ref[...], h)
    for hd in range(N_HEADS):
        rows = slice(hd * HEAD_DIM, (hd + 1) * HEAD_DIM)
        y = _rope_t(_head_rms_t(y_ref[rows, :], qg_ref[...]), cos, sin) * Q_SCALE
        q_out[0, rows, :] = y.astype(BF)
    y_ref[0:3 * kvd, :] = _dot_nt(wk_ref[...], h)
    tm = h.shape[0]
    tok = pl.program_id(1) * tm + lax.broadcasted_iota(jnp.int32, (tm, 1), 0)
    blk_lane = HEAD_DIM + lax.shift_right_logical(tok, int(np.log2(NSA_SLC_LEN)))
    blk_hot = lax.broadcasted_iota(jnp.int32, (1, SLC_KEY_LANES), 1) == blk_lane
    for kind in range(3):
        for gi in range(g):
            r0 = (kind * g + gi) * HEAD_DIM
            y = y_ref[r0:r0 + HEAD_DIM, :]
            if kind > 0:
                y = _head_rms_t(y, kg_ref[:, kind:kind + 1])
            yt = _to_token_major(_rope_t(y, cos, sin))
            if kind == 0:
                kc_out[0, gi] = yt[:, :HEAD_DIM]
            elif kind == 1:
                wide = jnp.concatenate([yt, jnp.zeros((tm, SLC_KEY_LANES - KEY_PAD), F32)], axis=1)
                ks_out[0, gi] = jnp.where(blk_hot, 1.0, wide).astype(BF)
            else:
                kw_out[0, gi] = yt.astype(BF)
    y_ref[0:3 * kvd, :] = _dot_nt(wv_ref[...], h)
    for gi in range(g):
        r0 = gi * HEAD_DIM
        vc_out[0, gi] = _to_token_major(y_ref[r0:r0 + HEAD_DIM, :])[:, :HEAD_DIM]
    _store_lane_tiles(vs_out, y_ref[kvd:2 * kvd, :].astype(BF), slc_tile)
    _store_lane_tiles(vw_out, y_ref[2 * kvd:3 * kvd, :].astype(BF), win_tile)
    g_out[0] = _sigmoid(_dot_nt(wg_ref[...], h))
    z_out[0] = _silu(_dot(h, wz_ref[...])).astype(BF)


def _full(shape):
    nd = len(shape)
    return pl.BlockSpec(shape, lambda i, j, _n=nd: (0,) * _n)


def _nsa_proj(h, cos, sin, w_in, q_gain, k_gain, *, slc_tile, win_tile):
    b, s, d = h.shape
    g = NSA_KV_HEADS
    kvd = g * HEAD_DIM
    tm = min(TOK_TILE, s)
    sizes = [MIX_WIDTH] + [kvd] * 6 + [3 * N_HEADS]
    off = np.cumsum([0] + sizes)
    wt = w_in.T.astype(BF)
    wq = wt[off[0]:off[1]]
    wk = jnp.concatenate([wt[off[1]:off[2]], wt[off[3]:off[4]], wt[off[5]:off[6]]], axis=0)
    wv = jnp.concatenate([wt[off[2]:off[3]], wt[off[4]:off[5]], wt[off[6]:off[7]]], axis=0)
    wg = wt[off[7]:off[8]]
    wz = w_in[:, off[8]:].astype(BF)
    qg = q_gain.reshape(HEAD_DIM, 1)
    kg = k_gain.T
    n_t = s // tm
    out_shape = [
        jax.ShapeDtypeStruct((b, MIX_WIDTH, s), BF),
        jax.ShapeDtypeStruct((b, g, s, HEAD_DIM), F32),
        jax.ShapeDtypeStruct((b, g, s, HEAD_DIM), F32),
        jax.ShapeDtypeStruct((b, g, s, SLC_KEY_LANES), BF),
        jax.ShapeDtypeStruct((b, g, s, KEY_PAD), BF),
        jax.ShapeDtypeStruct((b, s // slc_tile, kvd, slc_tile), BF),
        jax.ShapeDtypeStruct((b, s // win_tile, kvd, win_tile), BF),
        jax.ShapeDtypeStruct((b, 3 * N_HEADS, s), F32),
        jax.ShapeDtypeStruct((b, s, MIX_WIDTH), BF),
    ]
    out_specs = [
        pl.BlockSpec((1, MIX_WIDTH, tm), lambda i, j: (i, 0, j)),
        pl.BlockSpec((1, g, tm, HEAD_DIM), lambda i, j: (i, 0, j, 0)),
        pl.BlockSpec((1, g, tm, HEAD_DIM), lambda i, j: (i, 0, j, 0)),
        pl.BlockSpec((1, g, tm, SLC_KEY_LANES), lambda i, j: (i, 0, j, 0)),
        pl.BlockSpec((1, g, tm, KEY_PAD), lambda i, j: (i, 0, j, 0)),
        pl.BlockSpec((1, tm // slc_tile, kvd, slc_tile), lambda i, j: (i, j, 0, 0)),
        pl.BlockSpec((1, tm // win_tile, kvd, win_tile), lambda i, j: (i, j, 0, 0)),
        pl.BlockSpec((1, 3 * N_HEADS, tm), lambda i, j: (i, 0, j)),
        pl.BlockSpec((1, tm, MIX_WIDTH), lambda i, j: (i, j, 0)),
    ]
    in_specs = [
        pl.BlockSpec((1, tm, d), lambda i, j: (i, j, 0)),
        pl.BlockSpec((1, HEAD_DIM // 2, tm), lambda i, j: (i, 0, j)),
        pl.BlockSpec((1, HEAD_DIM // 2, tm), lambda i, j: (i, 0, j)),
        _full(wq.shape), _full(wk.shape), _full(wv.shape), _full(wg.shape), _full(wz.shape),
        _full(qg.shape), _full(kg.shape),
    ]
    return pl.pallas_call(
        functools.partial(_nsa_proj_kernel, slc_tile=slc_tile, win_tile=win_tile),
        grid=(b, n_t), in_specs=in_specs, out_specs=out_specs, out_shape=out_shape,
        scratch_shapes=[pltpu.VMEM((MIX_WIDTH, tm), F32)],
        compiler_params=_params(("parallel", "parallel")),
        name="nsa_proj",
    )(h, cos, sin, wq, wk, wv, wg, wz, qg, kg)


def _gelu_tanh(x):
    c = np.float32(np.sqrt(2.0 / np.pi))
    return 0.5 * x * (1.0 + jnp.tanh(c * (x + 0.044715 * (x * x * x))))


def _compress_kernel(kc_ref, vc_ref, pos_ref, w1_ref, w2_ref, kg_ref, kcmp_out, vcmp_out):
    for which, (src, dst) in enumerate(((kc_ref, kcmp_out), (vc_ref, vcmp_out))):
        x = src[0, 0]
        n = x.shape[0]
        half = x.shape[1]
        xa = (x + pos_ref[which, 0:1, :]).astype(BF)
        xb = (x + pos_ref[which, 1:2, :]).astype(BF)
        ua = _dot(xa, w1_ref[which, :half, :])
        ub = _dot(xb, w1_ref[which, half:, :])
        row = lax.broadcasted_iota(jnp.int32, (n, 1), 0)
        ub_next = jnp.where(row == n - 1, 0.0, pltpu.roll(ub, n - 1, 0))
        hid = _gelu_tanh(ua + ub_next)
        y = _dot(hid.astype(BF), w2_ref[which])
        if which == 0:
            y = _rms_rows(y, kg_ref[...])
            dst[0, 0] = y.astype(BF)
        else:
            pad = jnp.zeros((n, KEY_PAD - HEAD_DIM), F32)
            dst[0, 0] = jnp.concatenate([y, pad], axis=1).T[:HEAD_DIM].astype(BF)


def _compress(kc_tok, vc_tok, cmp_pos, cmp_w1, cmp_w2, k_gain0):
    b, g, s, _ = kc_tok.shape
    n_chunk = s // NSA_CMP_STRIDE
    flat = NSA_CMP_STRIDE * HEAD_DIM
    kc = kc_tok.reshape(b, g, n_chunk, flat)
    vc = vc_tok.reshape(b, g, n_chunk, flat)
    pos = cmp_pos.reshape(2, 2, flat)
    w1 = cmp_w1.astype(BF)
    w2 = cmp_w2.astype(BF)
    blk = pl.BlockSpec((1, 1, n_chunk, flat), lambda i, j: (i, j, 0, 0))
    return pl.pallas_call(
        _compress_kernel,
        grid=(b, g),
        in_specs=[blk, blk, _full(pos.shape), _full(w1.shape), _full(w2.shape),
                  _full((1, HEAD_DIM))],
        out_specs=[pl.BlockSpec((1, 1, n_chunk, HEAD_DIM), lambda i, j: (i, j, 0, 0)),
                   pl.BlockSpec((1, 1, HEAD_DIM, n_chunk), lambda i, j: (i, j, 0, 0))],
        out_shape=[jax.ShapeDtypeStruct((b, g, n_chunk, HEAD_DIM), BF),
                   jax.ShapeDtypeStruct((b, g, HEAD_DIM, n_chunk), BF)],
        compiler_params=_params(("parallel", "parallel")),
        name="nsa_compress",
    )(kc, vc, pos, w1, w2, k_gain0.reshape(1, HEAD_DIM))


def _stack_heads(q_ref, n_heads, dk):
    return jnp.concatenate([q_ref[0, r * dk:(r + 1) * dk, :] for r in range(n_heads)], axis=1)


def _unstack_store(o, out_ref, n_heads, tq):
    rows = jnp.concatenate([o[:, r * tq:(r + 1) * tq] for r in range(n_heads)], axis=0)
    out_ref[0] = rows.T.astype(out_ref.dtype)


def _gate_row(gate_ref, n_heads):
    return jnp.concatenate([gate_ref[0, 0, 0, r:r + 1, :] for r in range(n_heads)], axis=1)


def _cmp_select_kernel(q_ref, kc_ref, vc_ref, ov_ref, gate_ref, o_out, sel_out, *, tq, n_blk):
    r_heads = N_HEADS // NSA_KV_HEADS
    nq = r_heads * tq
    q0 = pl.program_id(2) * tq
    q4 = _stack_heads(q_ref, r_heads, HEAD_DIM)
    kc = kc_ref[0, 0]
    n_chunk = kc.shape[0]
    s = _dot(kc, q4)
    t_row = q0 + (lax.broadcasted_iota(jnp.int32, (1, nq), 1) & (tq - 1))
    c_col = lax.broadcasted_iota(jnp.int32, (n_chunk, 1), 0)
    n_cmp = n_chunk - 1
    valid = (c_col * NSA_CMP_STRIDE + (NSA_CMP_LEN - 1) <= t_row) & (c_col < n_cmp)
    s = jnp.where(valid, s, NEG_INF)
    m = jnp.max(s, axis=0, keepdims=True)
    e = jnp.where(valid, jnp.exp2(s - m), 0.0)
    l = jnp.sum(e, axis=0, keepdims=True)
    inv = jnp.where(l > 0.0, 1.0 / jnp.where(l > 0.0, l, 1.0), 0.0)
    p = e * inv
    o = _dot(vc_ref[0, 0], p.astype(BF))
    o = o * _gate_row(gate_ref, r_heads)
    _unstack_store(o, o_out, r_heads, tq)

    psum = p[:, 0:tq]
    for r in range(1, r_heads):
        psum = psum + p[:, r * tq:(r + 1) * tq]
    p_hi = psum.astype(BF)
    p_lo = (psum - p_hi.astype(F32)).astype(BF)
    ov = ov_ref[...]
    imp = _dot(ov, p_hi) + _dot(ov, p_lo)

    t1 = q0 + lax.broadcasted_iota(jnp.int32, (1, tq), 1)
    cur = lax.shift_right_logical(t1, int(np.log2(NSA_SLC_LEN)))
    blk = lax.broadcasted_iota(jnp.int32, (n_blk, tq), 0)
    forced = (blk == 0) | (blk == cur) | (blk == cur - 1)
    imp = jnp.where(forced, BIG, jnp.where(blk > cur, NEG_INF, imp))
    for _ in range(min(NSA_TOPK, n_blk)):
        best = jnp.max(imp, axis=0, keepdims=True)
        first = jnp.min(jnp.where(imp == best, blk, n_blk), axis=0, keepdims=True)
        imp = jnp.where(blk == first, -jnp.inf, imp)
    sel_out[0, 0] = jnp.where(imp == -jnp.inf, 0.0, SEL_OFF).astype(BF)


def _overlap_matrix(s):
    n_chunk = s // NSA_CMP_STRIDE
    n_blk = s // NSA_SLC_LEN
    c0 = np.arange(n_chunk) * NSA_CMP_STRIDE
    c1 = c0 + NSA_CMP_LEN - 1
    b0 = np.arange(n_blk) * NSA_SLC_LEN
    ov = np.minimum(c1[None, :], b0[:, None] + NSA_SLC_LEN - 1) - np.maximum(c0[None, :], b0[:, None]) + 1
    return jnp.asarray(np.clip(ov, 0, None) / NSA_CMP_LEN, BF)


def _cmp_select(qT, k_cmp, v_cmpT, gates5, *, tq):
    b, _, s = qT.shape
    g = NSA_KV_HEADS
    r_heads = N_HEADS // g
    n_chunk = k_cmp.shape[2]
    n_blk = s // NSA_SLC_LEN
    ov = _overlap_matrix(s)
    rows = r_heads * HEAD_DIM
    return pl.pallas_call(
        functools.partial(_cmp_select_kernel, tq=tq, n_blk=n_blk),
        grid=(b, g, s // tq),
        in_specs=[
            pl.BlockSpec((1, rows, tq), lambda i, j, k: (i, j, k)),
            pl.BlockSpec((1, 1, n_chunk, HEAD_DIM), lambda i, j, k: (i, j, 0, 0)),
            pl.BlockSpec((1, 1, HEAD_DIM, n_chunk), lambda i, j, k: (i, j, 0, 0)),
            pl.BlockSpec((n_blk, n_chunk), lambda i, j, k: (0, 0)),
            pl.BlockSpec((1, 1, 1, r_heads, tq), lambda i, j, k: (i, 0, j, 0, k)),
        ],
        out_specs=[pl.BlockSpec((1, tq, rows), lambda i, j, k: (i, k, j)),
                   pl.BlockSpec((1, 1, n_blk, tq), lambda i, j, k: (i, j, 0, k))],
        out_shape=[jax.ShapeDtypeStruct((b, s, MIX_WIDTH), BF),
                   jax.ShapeDtypeStruct((b, g, n_blk, s), BF)],
        compiler_params=_params(("parallel", "parallel", "parallel")),
        name="nsa_cmp_select",
    )(qT, k_cmp, v_cmpT, ov, gates5)


def _flash_kernel(*refs, mode, n_par, n_rep, dq, kc, tq, tk, cw, window, n_blk, has_sink, has_gate):
    it = iter(refs)
    q_ref, k_ref, v_ref = next(it), next(it), next(it)
    sel_ref = next(it) if n_blk else None
    sink_ref = next(it) if has_sink else None
    gate_ref = next(it) if has_gate else None
    out_ref = next(it)
    qs_ref, m_ref, acc_ref, mt_ref, s0_ref, s1_ref = (next(it) for _ in range(6))
    s_bufs = (s0_ref, s1_ref)

    nq = n_rep * tq
    items = [(p, c) for p in range(n_par) for c in range(nq // cw)]
    n_items = len(items)
    grp = pl.program_id(1)
    q0 = pl.program_id(2) * tq
    acc_row = lax.broadcasted_iota(jnp.int32, (V_ROWS, nq), 0)
    for p in range(n_par):
        for r in range(n_rep):
            hd = p * n_rep + r
            qs_ref[p, 0:dq, r * tq:(r + 1) * tq] = q_ref[0, hd * dq:(hd + 1) * dq, :]
        if n_blk:
            qs_ref[p, dq:dq + n_blk, :] = jnp.concatenate([sel_ref[0, 0]] * n_rep, axis=1)
            if dq + n_blk < kc:
                qs_ref[p, dq + n_blk:kc, :] = jnp.zeros((kc - dq - n_blk, nq), BF)
        if has_sink:
            m_ref[p] = jnp.concatenate(
                [jnp.full((1, tq), sink_ref[(grp * n_par + p) * n_rep + r] * LOG2E, F32) for r in range(n_rep)],
                axis=1)
            acc_ref[p] = jnp.where(acc_row == HEAD_DIM, 1.0, 0.0)
        else:
            m_ref[p] = jnp.full((1, nq), M_INIT, F32)
            acc_ref[p] = jnp.zeros((V_ROWS, nq), F32)

    t_row = q0 + (lax.broadcasted_iota(jnp.int32, (1, nq), 1) & (tq - 1))
    one_row = jnp.where(lax.broadcasted_iota(jnp.int32, (V_ROWS - HEAD_DIM, tk), 0) == 0, 1.0, 0.0).astype(BF)

    def stage_a(item, j, key0, slot, kind):
        p, c = item
        cols = slice(c * cw, (c + 1) * cw)
        s = _dot(k_ref[0, p, j][:, :kc], qs_ref[p, :, cols])
        if kind is not None:
            key = key0 + lax.broadcasted_iota(jnp.int32, (tk, 1), 0)
            ok = key <= t_row[:, cols] if kind == "causal" else key > t_row[:, cols] - window
            s = jnp.where(ok, s, NEG_INF)
        s_bufs[slot][...] = s
        mt_ref[slot] = jnp.max(s, axis=0, keepdims=True)

    def stage_b(item, j, slot):
        p, c = item
        cols = slice(c * cw, (c + 1) * cw)
        m_old = m_ref[p, :, cols]
        m_new = jnp.maximum(m_old, mt_ref[slot])
        pr = jnp.exp2(s_bufs[slot][...] - m_new).astype(BF)
        alpha = jnp.exp2(m_old - m_new)
        v = jnp.concatenate([v_ref[0, j, p * HEAD_DIM:(p + 1) * HEAD_DIM, :], one_row], axis=0)
        acc_ref[p, :, cols] = alpha * acc_ref[p, :, cols] + _dot(v, pr)
        m_ref[p, :, cols] = m_new

    if mode == "causal":
        assert n_items % 2 == 0
        n_full = q0 // tk

        def step(j, kind, next_kind, last):
            for idx, item in enumerate(items):
                slot = idx % 2
                if idx + 1 < n_items:
                    stage_a(items[idx + 1], j, j * tk, 1 - slot, kind)
                elif not last:
                    stage_a(items[0], j + 1, (j + 1) * tk, 1 - slot, next_kind)
                stage_b(item, j, slot)

        stage_a(items[0], 0, 0, 0, "causal")

        def body(j, carry):
            step(j, None, None, False)
            return carry

        lax.fori_loop(0, n_full - 1, body, 0)

        @pl.when(n_full >= 1)
        def _():
            step(n_full - 1, None, "causal", False)

        step(n_full, "causal", None, True)
    else:
        w_tiles, q_tiles = window // tk, tq // tk
        work = []
        for i in range(w_tiles + q_tiles):
            jv = q0 // tk - w_tiles + i
            key0 = jnp.where(jv < 0, -(1 << 30), jv * tk)
            kind = "causal" if i >= w_tiles else "lower"
            work += [(item, jnp.maximum(jv, 0), key0, kind) for item in items]
        stage_a(work[0][0], work[0][1], work[0][2], 0, work[0][3])
        for n, (item, j, _, _) in enumerate(work):
            if n + 1 < len(work):
                nxt = work[n + 1]
                stage_a(nxt[0], nxt[1], nxt[2], (n + 1) % 2, nxt[3])
            stage_b(item, j, n % 2)

    outs = []
    for p in range(n_par):
        acc = acc_ref[p]
        o = acc[:HEAD_DIM] * (1.0 / acc[HEAD_DIM:HEAD_DIM + 1])
        if has_gate:
            o = o * _gate_row(gate_ref, n_rep)
        outs += [o[:, r * tq:(r + 1) * tq] for r in range(n_rep)]
    out_ref[0] = jnp.concatenate(outs, axis=0).T.astype(out_ref.dtype)


def _flash(qT, k_tok, vT_tiles, *, mode, n_par, n_rep, dq, kc, tq, tk, cw, window=None,
           sel_bias=None, sinks=None, gates=None, gate_branch=0, name="flash"):
    b, _, s = qT.shape
    kh, k_lanes = k_tok.shape[1], k_tok.shape[3]
    n_t = s // tk
    k5 = k_tok.reshape(b, kh, n_t, tk, k_lanes)
    n_grp = kh // n_par
    heads = n_par * n_rep
    nq = n_rep * tq
    n_blk = sel_bias.shape[2] if sel_bias is not None else 0
    has_sink, has_gate = sinks is not None, gates is not None
    args = [qT, k5, vT_tiles]
    in_specs = [
        pl.BlockSpec((1, heads * dq, tq), lambda i, j, k: (i, j, k)),
        pl.BlockSpec((1, n_par, n_t, tk, k_lanes), lambda i, j, k: (i, j, 0, 0, 0)),
        pl.BlockSpec((1, n_t, n_par * HEAD_DIM, tk), lambda i, j, k: (i, 0, j, 0)),
    ]
    if n_blk:
        assert dq + n_blk <= kc
        args.append(sel_bias)
        in_specs.append(pl.BlockSpec((1, 1, n_blk, tq), lambda i, j, k: (i, j, 0, k)))
    if has_sink:
        args.append(sinks.astype(F32))
        in_specs.append(pl.BlockSpec(memory_space=pltpu.SMEM))
    if has_gate:
        args.append(gates)
        in_specs.append(pl.BlockSpec((1, 1, 1, n_rep, tq),
                                     lambda i, j, k, _br=gate_branch: (i, _br, j, 0, k)))
    scratch = [pltpu.VMEM((n_par, kc, nq), BF), pltpu.VMEM((n_par, 1, nq), F32),
               pltpu.VMEM((n_par, V_ROWS, nq), F32), pltpu.VMEM((2, 1, cw), F32),
               pltpu.VMEM((tk, cw), F32), pltpu.VMEM((tk, cw), F32)]
    kern = functools.partial(_flash_kernel, mode=mode, n_par=n_par, n_rep=n_rep, dq=dq, kc=kc, tq=tq, tk=tk,
                             cw=cw, window=window, n_blk=n_blk, has_sink=has_sink, has_gate=has_gate)
    return pl.pallas_call(
        kern,
        grid=(b, n_grp, s // tq),
        in_specs=in_specs,
        out_specs=pl.BlockSpec((1, tq, heads * HEAD_DIM), lambda i, j, k: (i, k, j)),
        out_shape=jax.ShapeDtypeStruct((b, s, MIX_WIDTH), BF),
        scratch_shapes=scratch,
        compiler_params=_params(("parallel", "parallel", "arbitrary")),
        name=name,
    )(*args)


def _swa_proj_kernel(h_ref, cos_ref, sin_ref, wq_ref, wk_ref, wv_ref, wz_ref, qg_ref, kg_ref,
                     q_out, k_out, v_out, z_out, y_ref, *, win_tile):
    h = h_ref[0]
    cos, sin = cos_ref[0], sin_ref[0]
    y_ref[...] = _dot_nt(wq_ref[...], h)
    for hd in range(N_HEADS):
        rows = slice(hd * HEAD_DIM, (hd + 1) * HEAD_DIM)
        q_out[0, rows, :] = (_rope_t(_head_rms_t(y_ref[rows, :], qg_ref[...]), cos, sin) * Q_SCALE).astype(BF)
    kvd = SWA_KV_HEADS * HEAD_DIM
    y_ref[0:kvd, :] = _dot_nt(wk_ref[...], h)
    for gi in range(SWA_KV_HEADS):
        rows = slice(gi * HEAD_DIM, (gi + 1) * HEAD_DIM)
        k_out[0, gi] = _to_token_major(_rope_t(_head_rms_t(y_ref[rows, :], kg_ref[...]), cos, sin)).astype(BF)
    _store_lane_tiles(v_out, _dot_nt(wv_ref[...], h).astype(BF), win_tile)
    z_out[0] = _silu(_dot(h, wz_ref[...])).astype(BF)


def _swa_proj(h, cos, sin, w_in, q_gain, k_gain, *, win_tile):
    b, s, d = h.shape
    g = SWA_KV_HEADS
    kvd = g * HEAD_DIM
    tm = min(TOK_TILE, s)
    wt = w_in.T.astype(BF)
    wq, wk, wv = wt[:MIX_WIDTH], wt[MIX_WIDTH:MIX_WIDTH + kvd], wt[MIX_WIDTH + kvd:MIX_WIDTH + 2 * kvd]
    wz = w_in[:, MIX_WIDTH + 2 * kvd:].astype(BF)
    qg, kg = q_gain.reshape(HEAD_DIM, 1), k_gain.reshape(HEAD_DIM, 1)
    wpt = tm // win_tile
    half = HEAD_DIM // 2
    return pl.pallas_call(
        functools.partial(_swa_proj_kernel, win_tile=win_tile),
        grid=(b, s // tm),
        in_specs=[pl.BlockSpec((1, tm, d), lambda i, j: (i, j, 0)),
                  pl.BlockSpec((1, half, tm), lambda i, j: (i, 0, j)),
                  pl.BlockSpec((1, half, tm), lambda i, j: (i, 0, j)),
                  _full(wq.shape), _full(wk.shape), _full(wv.shape), _full(wz.shape),
                  _full(qg.shape), _full(kg.shape)],
        out_specs=[pl.BlockSpec((1, MIX_WIDTH, tm), lambda i, j: (i, 0, j)),
                   pl.BlockSpec((1, g, tm, KEY_PAD), lambda i, j: (i, 0, j, 0)),
                   pl.BlockSpec((1, wpt, kvd, win_tile), lambda i, j: (i, j, 0, 0)),
                   pl.BlockSpec((1, tm, MIX_WIDTH), lambda i, j: (i, j, 0))],
        out_shape=[jax.ShapeDtypeStruct((b, MIX_WIDTH, s), BF),
                   jax.ShapeDtypeStruct((b, g, s, KEY_PAD), BF),
                   jax.ShapeDtypeStruct((b, s // win_tile, kvd, win_tile), BF),
                   jax.ShapeDtypeStruct((b, s, MIX_WIDTH), BF)],
        scratch_shapes=[pltpu.VMEM((MIX_WIDTH, tm), F32)],
        compiler_params=_params(("parallel", "parallel")),
        name="swa_proj",
    )(h, cos, sin, wq, wk, wv, wz, qg, kg)


def _split3(x):
    hi = x.astype(BF)
    r1 = x - hi.astype(F32)
    mid = r1.astype(BF)
    lo = (r1 - mid.astype(F32)).astype(BF)
    return hi, mid, lo


def _forget_cum_kernel(h_ref, wf_ref, bias_ref, tri_ref, cum_out, carry_ref):
    @pl.when(pl.program_id(1) == 0)
    def _():
        carry_ref[...] = jnp.zeros_like(carry_ref)

    x = _dot_nt(wf_ref[...], h_ref[0]) + bias_ref[...]
    logf = jnp.minimum(x, 0.0) - jnp.log(1.0 + jnp.exp(-jnp.abs(x)))
    tri = tri_ref[...]
    hi, mid, lo = _split3(logf)
    cum = (_dot(hi, tri) + _dot(mid, tri)) + _dot(lo, tri) + carry_ref[:, 0:1]
    cum_out[0] = cum
    carry_ref[...] = jnp.broadcast_to(cum[:, -1:], carry_ref.shape)


def _forget_cum(h, wf, bias):
    b, s, d = h.shape
    tm = min(TOK_TILE, s)
    tri = jnp.asarray(np.arange(tm)[:, None] <= np.arange(tm)[None, :], BF)
    return pl.pallas_call(
        _forget_cum_kernel,
        grid=(b, s // tm),
        in_specs=[pl.BlockSpec((1, tm, d), lambda i, j: (i, j, 0)),
                  _full(wf.shape), _full((N_HEADS, 1)), _full(tri.shape)],
        out_specs=pl.BlockSpec((1, N_HEADS, tm), lambda i, j: (i, 0, j)),
        out_shape=jax.ShapeDtypeStruct((b, N_HEADS, s), F32),
        scratch_shapes=[pltpu.VMEM((N_HEADS, LANES), F32)],
        compiler_params=_params(("parallel", "arbitrary")),
        name="fox_forget_cum",
    )(h, wf, bias.reshape(N_HEADS, 1).astype(F32), tri)


def _fox_proj_kernel(h_ref, cum_ref, wq_ref, wk_ref, wv_ref, wz_ref, qg_ref, kg_ref,
                     q_out, k_out, v_out, z_out, yq_ref, yk_ref, *, tile):
    h = h_ref[0]
    tm = h.shape[0]
    yq_ref[...] = _dot_nt(wq_ref[...], h)
    yk_ref[...] = _dot_nt(wk_ref[...], h)
    row = lax.broadcasted_iota(jnp.int32, (8, tm), 0)
    zeros = jnp.zeros((KEY_PAD - HEAD_DIM - 16, tm), F32)
    for hd in range(N_HEADS):
        rows = slice(hd * HEAD_DIM, (hd + 1) * HEAD_DIM)
        c_hi, c_mid, c_lo = (c.astype(F32) for c in _split3(cum_ref[0, hd:hd + 1, :] * LOG2E))
        c3 = jnp.where(row == 0, c_hi, jnp.where(row == 1, c_mid, jnp.where(row == 2, c_lo, 0.0)))
        one3 = jnp.where(row < 3, 1.0, 0.0)
        q = _head_rms_t(yq_ref[rows, :], qg_ref[...]) * Q_SCALE
        q_out[0, hd] = jnp.concatenate([q, one3, c3, zeros], axis=0).astype(BF)
        k = _head_rms_t(yk_ref[rows, :], kg_ref[...])
        k_out[0, hd] = jnp.concatenate([k, -c3, one3, zeros], axis=0).T.astype(BF)
    _store_lane_tiles(v_out, _dot_nt(wv_ref[...], h).astype(BF), tile)
    z_out[0] = _silu(_dot(h, wz_ref[...])).astype(BF)


def _fox_proj(h, cum, w_in, q_gain, k_gain, *, tile):
    b, s, d = h.shape
    tm = min(TOK_TILE, s)
    wt = w_in.T.astype(BF)
    wq, wk, wv = wt[:MIX_WIDTH], wt[MIX_WIDTH:2 * MIX_WIDTH], wt[2 * MIX_WIDTH:3 * MIX_WIDTH]
    wz = w_in[:, 3 * MIX_WIDTH + N_HEADS:].astype(BF)
    qg, kg = q_gain.reshape(HEAD_DIM, 1), k_gain.reshape(HEAD_DIM, 1)
    return pl.pallas_call(
        functools.partial(_fox_proj_kernel, tile=tile),
        grid=(b, s // tm),
        in_specs=[pl.BlockSpec((1, tm, d), lambda i, j: (i, j, 0)),
                  pl.BlockSpec((1, N_HEADS, tm), lambda i, j: (i, 0, j)),
                  _full(wq.shape), _full(wk.shape), _full(wv.shape), _full(wz.shape),
                  _full(qg.shape), _full(kg.shape)],
        out_specs=[pl.BlockSpec((1, N_HEADS, KEY_PAD, tm), lambda i, j: (i, 0, 0, j)),
                   pl.BlockSpec((1, N_HEADS, tm, KEY_PAD), lambda i, j: (i, 0, j, 0)),
                   pl.BlockSpec((1, tm // tile, MIX_WIDTH, tile), lambda i, j: (i, j, 0, 0)),
                   pl.BlockSpec((1, tm, MIX_WIDTH), lambda i, j: (i, j, 0))],
        out_shape=[jax.ShapeDtypeStruct((b, N_HEADS, KEY_PAD, s), BF),
                   jax.ShapeDtypeStruct((b, N_HEADS, s, KEY_PAD), BF),
                   jax.ShapeDtypeStruct((b, s // tile, MIX_WIDTH, tile), BF),
                   jax.ShapeDtypeStruct((b, s, MIX_WIDTH), BF)],
        scratch_shapes=[pltpu.VMEM((MIX_WIDTH, tm), F32), pltpu.VMEM((MIX_WIDTH, tm), F32)],
        compiler_params=_params(("parallel", "parallel")),
        name="fox_proj",
    )(h, cum, wq, wk, wv, wz, qg, kg)


def _out_proj_kernel(*refs, n_o, has_next):
    o_refs = refs[:n_o]
    z_ref, x_ref, w_ref = refs[n_o:n_o + 3]
    rest = refs[n_o + 3:]
    if has_next:
        g_ref, x_out, h_out = rest
    else:
        (x_out,) = rest
    o = o_refs[0][0].astype(F32)
    for r in o_refs[1:]:
        o = o + r[0].astype(F32)
    y = _dot((o * z_ref[0].astype(F32)).astype(BF), w_ref[...])
    x_new = x_ref[0] + y
    x_out[0] = x_new
    if has_next:
        h_out[0] = _rms_rows(x_new, g_ref[...]).astype(BF)


def _out_proj(o_list, zs, x, w_out, next_gain):
    b, s, d = x.shape
    tm = min(TOK_TILE, s)
    has_next = next_gain is not None
    blk = pl.BlockSpec((1, tm, d), lambda i, j: (i, j, 0))
    args = list(o_list) + [zs, x, w_out.astype(BF)]
    in_specs = [blk] * (len(o_list) + 2) + [_full(w_out.shape)]
    out_shape = [jax.ShapeDtypeStruct((b, s, d), F32)]
    out_specs = [blk]
    if has_next:
        args.append(next_gain.reshape(1, d))
        in_specs.append(_full((1, d)))
        out_shape.append(jax.ShapeDtypeStruct((b, s, d), BF))
        out_specs.append(blk)
    res = pl.pallas_call(
        functools.partial(_out_proj_kernel, n_o=len(o_list), has_next=has_next),
        grid=(b, s // tm), in_specs=in_specs, out_specs=out_specs, out_shape=out_shape,
        compiler_params=_params(("parallel", "parallel")),
        name="out_proj",
    )(*args)
    return (res[0], res[1]) if has_next else (res[0], None)


ITEM_LANES = 512
CMP_Q_TILE = 128
NSA_Q_TILE = 256
NSA_SLC_TILE = 512
NSA_WIN_TILE = 256
SWA_TILE = 128
FOX_Q_TILE = 512
FOX_K_TILE = 512


def _nsa_mixer(h, cos, sin, w_in, q_gain, k_gain, cmp_pos, cmp_w1, cmp_w2):
    b, s, _ = h.shape
    g = NSA_KV_HEADS
    r = N_HEADS // g
    qT, kc_tok, vc_tok, ks, kw, vsT, vwT, gates, zs = _nsa_proj(
        h, cos, sin, w_in, q_gain, k_gain, slc_tile=NSA_SLC_TILE, win_tile=NSA_WIN_TILE)
    k_cmp, v_cmpT = _compress(kc_tok, vc_tok, cmp_pos, cmp_w1, cmp_w2, k_gain[0])
    gates5 = gates.reshape(b, 3, g, r, s)
    o_cmp, sel_bias = _cmp_select(qT, k_cmp, v_cmpT, gates5, tq=CMP_Q_TILE)
    o_slc = _flash(qT, ks, vsT, mode="causal", n_par=1, n_rep=r, dq=HEAD_DIM, kc=SLC_KEY_LANES,
                   tq=NSA_Q_TILE, tk=NSA_SLC_TILE, cw=ITEM_LANES, sel_bias=sel_bias,
                   gates=gates5, gate_branch=1, name="nsa_selected")
    o_win = _flash(qT, kw, vwT, mode="window", n_par=1, n_rep=r, dq=HEAD_DIM, kc=HEAD_DIM,
                   tq=NSA_Q_TILE, tk=NSA_WIN_TILE, cw=ITEM_LANES, window=NSA_WINDOW,
                   gates=gates5, gate_branch=2, name="nsa_window")
    return [o_cmp, o_slc, o_win], zs


def _swa_mixer(h, cos, sin, w_in, q_gain, k_gain, sinks):
    r = N_HEADS // SWA_KV_HEADS
    qT, k, vT, zs = _swa_proj(h, cos, sin, w_in, q_gain, k_gain, win_tile=SWA_TILE)
    o = _flash(qT, k, vT, mode="window", n_par=1, n_rep=r, dq=HEAD_DIM, kc=HEAD_DIM, tq=SWA_TILE,
               tk=SWA_TILE, cw=ITEM_LANES, window=SWA_WINDOW, sinks=sinks, name="swa_window")
    return [o], zs


def _fox_mixer(h, w_in, forget_bias, q_gain, k_gain):
    b, s, _ = h.shape
    wf = w_in[:, 3 * MIX_WIDTH:3 * MIX_WIDTH + N_HEADS].T.astype(BF)
    cum = _forget_cum(h, wf, forget_bias)
    qT, k, vT, zs = _fox_proj(h, cum, w_in, q_gain, k_gain, tile=FOX_K_TILE)
    o = _flash(qT.reshape(b, N_HEADS * KEY_PAD, s), k, vT, mode="causal", n_par=2, n_rep=1, dq=KEY_PAD,
               kc=KEY_PAD, tq=FOX_Q_TILE, tk=FOX_K_TILE, cw=ITEM_LANES, name="fox_attention")
    return [o], zs


def kernel(x, positions, norm_gains, a_w_in, a_q_gain, a_k_gain, a_cmp_pos, a_cmp_w1, a_cmp_w2, a_w_out,
           b_w_in, b_q_gain, b_k_gain, b_sinks, b_w_out,
           c_w_in, c_forget_bias, c_q_gain, c_k_gain, c_w_out):
    depth = norm_gains.shape[0]
    cos, sin = _rope_tables(positions)
    h = _prenorm(x, norm_gains[0])
    for i in range(depth):
        j, mixer = divmod(i, 3)
        if mixer == 0:
            o_list, zs = _nsa_mixer(h, cos, sin, a_w_in[j], a_q_gain[j], a_k_gain[j],
                                    a_cmp_pos[j], a_cmp_w1[j], a_cmp_w2[j])
            w_out = a_w_out[j]
        elif mixer == 1:
            o_list, zs = _swa_mixer(h, cos, sin, b_w_in[j], b_q_gain[j], b_k_gain[j], b_sinks[j])
            w_out = b_w_out[j]
        else:
            o_list, zs = _fox_mixer(h, c_w_in[j], c_forget_bias[j], c_q_gain[j], c_k_gain[j])
            w_out = c_w_out[j]
        next_gain = norm_gains[i + 1] if i + 1 < depth else None
        x, h = _out_proj(o_list, zs, x, w_out, next_gain)
    return x
```

```python
import functools

import jax
import jax.numpy as jnp
import numpy as np
from jax import lax
from jax.experimental import pallas as pl
from jax.experimental.pallas import tpu as pltpu

D_MODEL = 1024
HEAD_DIM = 64
N_HEADS = 16
MIX_WIDTH = N_HEADS * HEAD_DIM
ROPE_THETA = 10000.0
EPS = 1e-6
SCALE = HEAD_DIM ** -0.5
NEG_INF = -1e30
BIG = 1e30
M_INIT = -1e29
SEL_OFF = -(2.0 ** 100)

NSA_KV_HEADS = 4
NSA_CMP_LEN = 32
NSA_CMP_STRIDE = 16
NSA_SLC_LEN = 64
NSA_TOPK = 16
NSA_WINDOW = 512
SWA_KV_HEADS = 2
SWA_WINDOW = 128

LOG2E = float(np.log2(np.e))
Q_SCALE = SCALE * LOG2E

LANES = 128
KEY_PAD = 128
SLC_KEY_LANES = 256
V_ROWS = 80
CMP_BLOCK = 128
CMP_MASK_ROWS = 2 * CMP_BLOCK
VMEM_LIMIT = 56 * 1024 * 1024

TOK_TILE = 512
NT_DIMS = (((1,), (1,)), ((), ()))

BF = jnp.bfloat16
F32 = jnp.float32


def _params(sem):
    return pltpu.CompilerParams(dimension_semantics=sem, vmem_limit_bytes=VMEM_LIMIT)


def _dot(a, b):
    return jnp.dot(a, b, preferred_element_type=F32)


def _dot_nt(a, b):
    return lax.dot_general(a, b, NT_DIMS, preferred_element_type=F32)


def _rope_tab_kernel(pos_ref, invf_ref, cos_ref, sin_ref):
    ang = invf_ref[...] * pos_ref[0].astype(F32)
    cos_ref[0] = jnp.cos(ang)
    sin_ref[0] = jnp.sin(ang)


def _rope_tables(positions):
    b, s = positions.shape
    half = HEAD_DIM // 2
    inv_freq = ROPE_THETA ** (-jnp.arange(half, dtype=F32) * 2.0 / HEAD_DIM)
    tm = min(TOK_TILE, s)
    out = jax.ShapeDtypeStruct((b, half, s), F32)
    return pl.pallas_call(
        _rope_tab_kernel,
        grid=(b, s // tm),
        in_specs=[pl.BlockSpec((1, 1, tm), lambda i, j: (i, 0, j)),
                  pl.BlockSpec((half, 1), lambda i, j: (0, 0))],
        out_specs=[pl.BlockSpec((1, half, tm), lambda i, j: (i, 0, j))] * 2,
        out_shape=[out, out],
        compiler_params=_params(("parallel", "parallel")),
        name="rope_tables",
    )(positions.reshape(b, 1, s), inv_freq.reshape(half, 1))


def _rms_rows(x, gain_row):
    y = x * lax.rsqrt(jnp.mean(x * x, axis=-1, keepdims=True) + EPS)
    return y * gain_row


def _prenorm_kernel(x_ref, g_ref, h_ref):
    h_ref[0] = _rms_rows(x_ref[0], g_ref[...]).astype(BF)


def _prenorm(x, gain):
    b, s, d = x.shape
    tm = min(TOK_TILE, s)
    return pl.pallas_call(
        _prenorm_kernel,
        grid=(b, s // tm),
        in_specs=[pl.BlockSpec((1, tm, d), lambda i, j: (i, j, 0)),
                  pl.BlockSpec((1, d), lambda i, j: (0, 0))],
        out_specs=pl.BlockSpec((1, tm, d), lambda i, j: (i, j, 0)),
        out_shape=jax.ShapeDtypeStruct((b, s, d), BF),
        compiler_params=_params(("parallel", "parallel")),
        name="prenorm",
    )(x, gain.reshape(1, d))


def _head_rms_t(y, gain_col):
    ms = jnp.mean(y * y, axis=0, keepdims=True)
    return (y * lax.rsqrt(ms + EPS)) * gain_col


def _rope_t(y, cos, sin):
    half = HEAD_DIM // 2
    x1, x2 = y[:half], y[half:]
    return jnp.concatenate([x1 * cos - x2 * sin, x2 * cos + x1 * sin], axis=0)


def _to_token_major(y):
    pad = jnp.zeros((KEY_PAD - y.shape[0], y.shape[1]), y.dtype)
    return jnp.concatenate([y, pad], axis=0).T


def _silu(z):
    return z * (1.0 / (1.0 + jnp.exp(-z)))


def _sigmoid(z):
    return 1.0 / (1.0 + jnp.exp(-z))


def _store_lane_tiles(out_ref, y, tile):
    for c in range(y.shape[1] // tile):
        out_ref[0, c] = y[:, c * tile:(c + 1) * tile]


def _nsa_proj_kernel(h_ref, cos_ref, sin_ref, wq_ref, wk_ref, wv_ref, wg_ref, wz_ref,
                     qg_ref, kg_ref,
                     q_out, kc_out, vc_out, ks_out, kw_out, vs_out, vw_out, g_out, z_out,
                     y_ref, *, slc_tile, win_tile):
    h = h_ref[0]
    cos, sin = cos_ref[0], sin_ref[0]
    g = NSA_KV_HEADS
    kvd = g * HEAD_DIM
    y_ref[...] = _dot_nt(wq_ref[...], h)
    for hd in range(N_HEADS):
        rows = slice(hd * HEAD_DIM, (hd + 1) * HEAD_DIM)
        y = _rope_t(_head_rms_t(y_ref[rows, :], qg_ref[...]), cos, sin) * Q_SCALE
        q_out[0, rows, :] = y.astype(BF)
    y_ref[0:3 * kvd, :] = _dot_nt(wk_ref[...], h)
    tm = h.shape[0]
    tok = pl.program_id(1) * tm + lax.broadcasted_iota(jnp.int32, (tm, 1), 0)
    blk_lane = HEAD_DIM + lax.shift_right_logical(tok, int(np.log2(NSA_SLC_LEN)))
    blk_hot = lax.broadcasted_iota(jnp.int32, (1, SLC_KEY_LANES), 1) == blk_lane
    for kind in range(3):
        for gi in range(g):
            r0 = (kind * g + gi) * HEAD_DIM
            y = y_ref[r0:r0 + HEAD_DIM, :]
            if kind > 0:
                y = _head_rms_t(y, kg_ref[:, kind:kind + 1])
            yt = _to_token_major(_rope_t(y, cos, sin))
            if kind == 0:
                kc_out[0, gi] = yt[:, :HEAD_DIM]
            elif kind == 1:
                wide = jnp.concatenate([yt, jnp.zeros((tm, SLC_KEY_LANES - KEY_PAD), F32)], axis=1)
                ks_out[0, gi] = jnp.where(blk_hot, 1.0, wide).astype(BF)
            else:
                kw_out[0, gi] = yt.astype(BF)
    y_ref[0:3 * kvd, :] = _dot_nt(wv_ref[...], h)
    for gi in range(g):
        r0 = gi * HEAD_DIM
        vc_out[0, gi] = _to_token_major(y_ref[r0:r0 + HEAD_DIM, :])[:, :HEAD_DIM]
    _store_lane_tiles(vs_out, y_ref[kvd:2 * kvd, :].astype(BF), slc_tile)
    _store_lane_tiles(vw_out, y_ref[2 * kvd:3 * kvd, :].astype(BF), win_tile)
    g_out[0] = _sigmoid(_dot_nt(wg_ref[...], h))
    z_out[0] = _silu(_dot(h, wz_ref[...])).astype(BF)


def _full(shape):
    nd = len(shape)
    return pl.BlockSpec(shape, lambda i, j, _n=nd: (0,) * _n)


def _nsa_proj(h, cos, sin, w_in, q_gain, k_gain, *, slc_tile, win_tile):
    b, s, d = h.shape
    g = NSA_KV_HEADS
    kvd = g * HEAD_DIM
    tm = min(TOK_TILE, s)
    sizes = [MIX_WIDTH] + [kvd] * 6 + [3 * N_HEADS]
    off = np.cumsum([0] + sizes)
    wt = w_in.T.astype(BF)
    wq = wt[off[0]:off[1]]
    wk = jnp.concatenate([wt[off[1]:off[2]], wt[off[3]:off[4]], wt[off[5]:off[6]]], axis=0)
    wv = jnp.concatenate([wt[off[2]:off[3]], wt[off[4]:off[5]], wt[off[6]:off[7]]], axis=0)
    wg = wt[off[7]:off[8]]
    wz = w_in[:, off[8]:].astype(BF)
    qg = q_gain.reshape(HEAD_DIM, 1)
    kg = k_gain.T
    n_t = s // tm
    out_shape = [
        jax.ShapeDtypeStruct((b, MIX_WIDTH, s), BF),
        jax.ShapeDtypeStruct((b, g, s, HEAD_DIM), F32),
        jax.ShapeDtypeStruct((b, g, s, HEAD_DIM), F32),
        jax.ShapeDtypeStruct((b, g, s, SLC_KEY_LANES), BF),
        jax.ShapeDtypeStruct((b, g, s, KEY_PAD), BF),
        jax.ShapeDtypeStruct((b, s // slc_tile, kvd, slc_tile), BF),
        jax.ShapeDtypeStruct((b, s // win_tile, kvd, win_tile), BF),
        jax.ShapeDtypeStruct((b, 3 * N_HEADS, s), F32),
        jax.ShapeDtypeStruct((b, s, MIX_WIDTH), BF),
    ]
    out_specs = [
        pl.BlockSpec((1, MIX_WIDTH, tm), lambda i, j: (i, 0, j)),
        pl.BlockSpec((1, g, tm, HEAD_DIM), lambda i, j: (i, 0, j, 0)),
        pl.BlockSpec((1, g, tm, HEAD_DIM), lambda i, j: (i, 0, j, 0)),
        pl.BlockSpec((1, g, tm, SLC_KEY_LANES), lambda i, j: (i, 0, j, 0)),
        pl.BlockSpec((1, g, tm, KEY_PAD), lambda i, j: (i, 0, j, 0)),
        pl.BlockSpec((1, tm // slc_tile, kvd, slc_tile), lambda i, j: (i, j, 0, 0)),
        pl.BlockSpec((1, tm // win_tile, kvd, win_tile), lambda i, j: (i, j, 0, 0)),
        pl.BlockSpec((1, 3 * N_HEADS, tm), lambda i, j: (i, 0, j)),
        pl.BlockSpec((1, tm, MIX_WIDTH), lambda i, j: (i, j, 0)),
    ]
    in_specs = [
        pl.BlockSpec((1, tm, d), lambda i, j: (i, j, 0)),
        pl.BlockSpec((1, HEAD_DIM // 2, tm), lambda i, j: (i, 0, j)),
        pl.BlockSpec((1, HEAD_DIM // 2, tm), lambda i, j: (i, 0, j)),
        _full(wq.shape), _full(wk.shape), _full(wv.shape), _full(wg.shape), _full(wz.shape),
        _full(qg.shape), _full(kg.shape),
    ]
    return pl.pallas_call(
        functools.partial(_nsa_proj_kernel, slc_tile=slc_tile, win_tile=win_tile),
        grid=(b, n_t), in_specs=in_specs, out_specs=out_specs, out_shape=out_shape,
        scratch_shapes=[pltpu.VMEM((MIX_WIDTH, tm), F32)],
        compiler_params=_params(("parallel", "parallel")),
        name="nsa_proj",
    )(h, cos, sin, wq, wk, wv, wg, wz, qg, kg)


def _gelu_tanh(x):
    c = np.float32(np.sqrt(2.0 / np.pi))
    return 0.5 * x * (1.0 + jnp.tanh(c * (x + 0.044715 * (x * x * x))))


def _compress_kernel(kc_ref, vc_ref, pos_ref, w1_ref, w2_ref, kg_ref, kcmp_out, vcmp_out):
    for which, (src, dst) in enumerate(((kc_ref, kcmp_out), (vc_ref, vcmp_out))):
        x = src[0, 0]
        n = x.shape[0]
        half = x.shape[1]
        xa = (x + pos_ref[which, 0:1, :]).astype(BF)
        xb = (x + pos_ref[which, 1:2, :]).astype(BF)
        ua = _dot(xa, w1_ref[which, :half, :])
        ub = _dot(xb, w1_ref[which, half:, :])
        row = lax.broadcasted_iota(jnp.int32, (n, 1), 0)
        ub_next = jnp.where(row == n - 1, 0.0, pltpu.roll(ub, n - 1, 0))
        hid = _gelu_tanh(ua + ub_next)
        y = _dot(hid.astype(BF), w2_ref[which])
        if which == 0:
            y = _rms_rows(y, kg_ref[...])
            dst[0, 0] = y.astype(BF)
        else:
            pad = jnp.zeros((n, KEY_PAD - HEAD_DIM), F32)
            dst[0, 0] = jnp.concatenate([y, pad], axis=1).T[:HEAD_DIM].astype(BF)


def _compress(kc_tok, vc_tok, cmp_pos, cmp_w1, cmp_w2, k_gain0):
    b, g, s, _ = kc_tok.shape
    n_chunk = s // NSA_CMP_STRIDE
    flat = NSA_CMP_STRIDE * HEAD_DIM
    kc = kc_tok.reshape(b, g, n_chunk, flat)
    vc = vc_tok.reshape(b, g, n_chunk, flat)
    pos = cmp_pos.reshape(2, 2, flat)
    w1 = cmp_w1.astype(BF)
    w2 = cmp_w2.astype(BF)
    blk = pl.BlockSpec((1, 1, n_chunk, flat), lambda i, j: (i, j, 0, 0))
    return pl.pallas_call(
        _compress_kernel,
        grid=(b, g),
        in_specs=[blk, blk, _full(pos.shape), _full(w1.shape), _full(w2.shape),
                  _full((1, HEAD_DIM))],
        out_specs=[pl.BlockSpec((1, 1, n_chunk, HEAD_DIM), lambda i, j: (i, j, 0, 0)),
                   pl.BlockSpec((1, 1, HEAD_DIM, n_chunk), lambda i, j: (i, j, 0, 0))],
        out_shape=[jax.ShapeDtypeStruct((b, g, n_chunk, HEAD_DIM), BF),
                   jax.ShapeDtypeStruct((b, g, HEAD_DIM, n_chunk), BF)],
        compiler_params=_params(("parallel", "parallel")),
        name="nsa_compress",
    )(kc, vc, pos, w1, w2, k_gain0.reshape(1, HEAD_DIM))


def _stack_heads(q_ref, n_heads, dk):
    return jnp.concatenate([q_ref[0, r * dk:(r + 1) * dk, :] for r in range(n_heads)], axis=1)


def _unstack_store(o, out_ref, n_heads, tq):
    rows = jnp.concatenate([o[:, r * tq:(r + 1) * tq] for r in range(n_heads)], axis=0)
    out_ref[0] = rows.T.astype(out_ref.dtype)


def _gate_row(gate_ref, n_heads):
    return jnp.concatenate([gate_ref[0, 0, 0, r:r + 1, :] for r in range(n_heads)], axis=1)


def _cmp_branch(rows, q_ref, kc_ref, vc_ref, ov_ref, gate_ref, o_out, imp_ref, *, tq, n_blk):
    r_heads = N_HEADS // NSA_KV_HEADS
    nq = r_heads * tq
    q0 = pl.program_id(2) * tq
    n_cmp = kc_ref.shape[2] - 1
    q4 = _stack_heads(q_ref, r_heads, HEAD_DIM)
    s = _dot(kc_ref[0, 0, 0:rows, :], q4)
    t_row = q0 + (lax.broadcasted_iota(jnp.int32, (1, nq), 1) & (tq - 1))
    lo = max(rows - CMP_MASK_ROWS, 0)
    c_col = lo + lax.broadcasted_iota(jnp.int32, (rows - lo, 1), 0)
    valid = (c_col * NSA_CMP_STRIDE + (NSA_CMP_LEN - 1) <= t_row) & (c_col < n_cmp)
    s_new = jnp.where(valid, s[lo:], NEG_INF)
    m = jnp.max(s_new, axis=0, keepdims=True)
    if lo:
        m = jnp.maximum(m, jnp.max(s[:lo], axis=0, keepdims=True))
    e = jnp.where(valid, jnp.exp2(s_new - m), 0.0).astype(BF)
    if lo:
        e = jnp.concatenate([jnp.exp2(s[:lo] - m).astype(BF), e], axis=0)
    one_row = jnp.where(lax.broadcasted_iota(jnp.int32, (V_ROWS - HEAD_DIM, rows), 0) == 0, 1.0, 0.0).astype(BF)
    lhs = jnp.concatenate([vc_ref[0, 0, :, 0:rows], one_row, ov_ref[:, 0:rows]], axis=0)
    res = _dot(lhs, e)
    l = res[HEAD_DIM:HEAD_DIM + 1]
    inv = jnp.where(l > 0.0, 1.0 / jnp.where(l > 0.0, l, 1.0), 0.0)
    o = res[:HEAD_DIM] * (inv * _gate_row(gate_ref, r_heads))
    _unstack_store(o, o_out, r_heads, tq)
    w = res[V_ROWS:V_ROWS + n_blk] * inv
    imp = w[:, 0:tq]
    for r in range(1, r_heads):
        imp = imp + w[:, r * tq:(r + 1) * tq]
    imp_ref[...] = imp


def _cmp_select_kernel(q_ref, kc_ref, vc_ref, ov_ref, gate_ref, o_out, sel_out, imp_ref, *, tq, n_blk):
    q0 = pl.program_id(2) * tq
    n_chunk = kc_ref.shape[2]
    n_need = jnp.minimum((q0 + tq - NSA_CMP_LEN) // NSA_CMP_STRIDE + 1, n_chunk - 1)
    n_steps = n_chunk // CMP_BLOCK
    need_steps = (n_need + CMP_BLOCK - 1) // CMP_BLOCK
    for k in range(1, n_steps + 1):
        @pl.when(need_steps == k)
        def _(k=k):
            _cmp_branch(k * CMP_BLOCK, q_ref, kc_ref, vc_ref, ov_ref, gate_ref, o_out, imp_ref, tq=tq, n_blk=n_blk)

    imp = imp_ref[...]
    t1 = q0 + lax.broadcasted_iota(jnp.int32, (1, tq), 1)
    cur = lax.shift_right_logical(t1, int(np.log2(NSA_SLC_LEN)))
    blk = lax.broadcasted_iota(jnp.int32, (n_blk, tq), 0)
    forced = (blk == 0) | (blk == cur) | (blk == cur - 1)
    imp = jnp.where(forced, BIG, jnp.where(blk > cur, NEG_INF, imp))
    for _ in range(min(NSA_TOPK, n_blk)):
        best = jnp.max(imp, axis=0, keepdims=True)
        first = jnp.min(jnp.where(imp == best, blk, n_blk), axis=0, keepdims=True)
        imp = jnp.where(blk == first, -jnp.inf, imp)
    sel_out[0, 0] = jnp.where(imp == -jnp.inf, 0.0, SEL_OFF).astype(BF)


def _overlap_matrix(s):
    n_chunk = s // NSA_CMP_STRIDE
    n_blk = s // NSA_SLC_LEN
    c0 = np.arange(n_chunk) * NSA_CMP_STRIDE
    c1 = c0 + NSA_CMP_LEN - 1
    b0 = np.arange(n_blk) * NSA_SLC_LEN
    ov = np.minimum(c1[None, :], b0[:, None] + NSA_SLC_LEN - 1) - np.maximum(c0[None, :], b0[:, None]) + 1
    return jnp.asarray(np.clip(ov, 0, None) / NSA_CMP_LEN, BF)


def _cmp_select(qT, k_cmp, v_cmpT, gates5, *, tq):
    b, _, s = qT.shape
    g = NSA_KV_HEADS
    r_heads = N_HEADS // g
    n_chunk = k_cmp.shape[2]
    n_blk = s // NSA_SLC_LEN
    ov = _overlap_matrix(s)
    rows = r_heads * HEAD_DIM
    return pl.pallas_call(
        functools.partial(_cmp_select_kernel, tq=tq, n_blk=n_blk),
        grid=(b, g, s // tq),
        in_specs=[
            pl.BlockSpec((1, rows, tq), lambda i, j, k: (i, j, k)),
            pl.BlockSpec((1, 1, n_chunk, HEAD_DIM), lambda i, j, k: (i, j, 0, 0)),
            pl.BlockSpec((1, 1, HEAD_DIM, n_chunk), lambda i, j, k: (i, j, 0, 0)),
            pl.BlockSpec((n_blk, n_chunk), lambda i, j, k: (0, 0)),
            pl.BlockSpec((1, 1, 1, r_heads, tq), lambda i, j, k: (i, 0, j, 0, k)),
        ],
        out_specs=[pl.BlockSpec((1, tq, rows), lambda i, j, k: (i, k, j)),
                   pl.BlockSpec((1, 1, n_blk, tq), lambda i, j, k: (i, j, 0, k))],
        out_shape=[jax.ShapeDtypeStruct((b, s, MIX_WIDTH), BF),
                   jax.ShapeDtypeStruct((b, g, n_blk, s), BF)],
        scratch_shapes=[pltpu.VMEM((n_blk, tq), F32)],
        compiler_params=_params(("parallel", "parallel", "parallel")),
        name="nsa_cmp_select",
    )(qT, k_cmp, v_cmpT, ov, gates5)


def _flash_kernel(*refs, mode, n_par, n_rep, dq, kc, tq, tk, cw, window, n_blk, has_sink, has_gate):
    it = iter(refs)
    q_ref, k_ref, v_ref = next(it), next(it), next(it)
    sel_ref = next(it) if n_blk else None
    sink_ref = next(it) if has_sink else None
    gate_ref = next(it) if has_gate else None
    out_ref = next(it)
    qs_ref, m_ref, acc_ref, mt_ref, s0_ref, s1_ref = (next(it) for _ in range(6))
    s_bufs = (s0_ref, s1_ref)

    nq = n_rep * tq
    items = [(p, c) for p in range(n_par) for c in range(nq // cw)]
    n_items = len(items)
    grp = pl.program_id(1)
    q0 = pl.program_id(2) * tq
    acc_row = lax.broadcasted_iota(jnp.int32, (V_ROWS, nq), 0)
    for p in range(n_par):
        for r in range(n_rep):
            hd = p * n_rep + r
            qs_ref[p, 0:dq, r * tq:(r + 1) * tq] = q_ref[0, hd * dq:(hd + 1) * dq, :]
        if n_blk:
            qs_ref[p, dq:dq + n_blk, :] = jnp.concatenate([sel_ref[0, 0]] * n_rep, axis=1)
            if dq + n_blk < kc:
                qs_ref[p, dq + n_blk:kc, :] = jnp.zeros((kc - dq - n_blk, nq), BF)
        if has_sink:
            m_ref[p] = jnp.concatenate(
                [jnp.full((1, tq), sink_ref[(grp * n_par + p) * n_rep + r] * LOG2E, F32) for r in range(n_rep)],
                axis=1)
            acc_ref[p] = jnp.where(acc_row == HEAD_DIM, 1.0, 0.0)
        else:
            m_ref[p] = jnp.full((1, nq), M_INIT, F32)
            acc_ref[p] = jnp.zeros((V_ROWS, nq), F32)

    t_row = q0 + (lax.broadcasted_iota(jnp.int32, (1, nq), 1) & (tq - 1))
    one_row = jnp.where(lax.broadcasted_iota(jnp.int32, (V_ROWS - HEAD_DIM, tk), 0) == 0, 1.0, 0.0).astype(BF)

    def stage_a(item, j, key0, slot, kind):
        p, c = item
        cols = slice(c * cw, (c + 1) * cw)
        s = _dot(k_ref[0, p, j][:, :kc], qs_ref[p, :, cols])
        if kind is not None:
            key = key0 + lax.broadcasted_iota(jnp.int32, (tk, 1), 0)
            ok = key <= t_row[:, cols] if kind == "causal" else key > t_row[:, cols] - window
            s = jnp.where(ok, s, NEG_INF)
        s_bufs[slot][...] = s
        mt_ref[slot] = jnp.max(s, axis=0, keepdims=True)

    def stage_b(item, j, slot):
        p, c = item
        cols = slice(c * cw, (c + 1) * cw)
        m_old = m_ref[p, :, cols]
        m_new = jnp.maximum(m_old, mt_ref[slot])
        pr = jnp.exp2(s_bufs[slot][...] - m_new).astype(BF)
        alpha = jnp.exp2(m_old - m_new)
        v = jnp.concatenate([v_ref[0, j, p * HEAD_DIM:(p + 1) * HEAD_DIM, :], one_row], axis=0)
        acc_ref[p, :, cols] = alpha * acc_ref[p, :, cols] + _dot(v, pr)
        m_ref[p, :, cols] = m_new

    if mode == "causal":
        assert n_items % 2 == 0
        n_full = q0 // tk

        def step(j, kind, next_kind, last):
            for idx, item in enumerate(items):
                slot = idx % 2
                if idx + 1 < n_items:
                    stage_a(items[idx + 1], j, j * tk, 1 - slot, kind)
                elif not last:
                    stage_a(items[0], j + 1, (j + 1) * tk, 1 - slot, next_kind)
                stage_b(item, j, slot)

        stage_a(items[0], 0, 0, 0, "causal")

        def body(j, carry):
            step(j, None, None, False)
            return carry

        def body_pair(i, carry):
            step(2 * i, None, None, False)
            step(2 * i + 1, None, None, False)
            return carry

        n_main = jnp.maximum(n_full - 1, 0)
        n_pairs = lax.shift_right_logical(n_main, 1)
        lax.fori_loop(0, n_pairs, body_pair, 0)
        lax.fori_loop(2 * n_pairs, n_main, body, 0)

        @pl.when(n_full >= 1)
        def _():
            step(n_full - 1, None, "causal", False)

        step(n_full, "causal", None, True)
    else:
        w_tiles, q_tiles = window // tk, tq // tk
        work = []
        for i in range(w_tiles + q_tiles):
            jv = q0 // tk - w_tiles + i
            key0 = jnp.where(jv < 0, -(1 << 30), jv * tk)
            kind = "causal" if i >= w_tiles else "lower"
            work += [(item, jnp.maximum(jv, 0), key0, kind) for item in items]
        stage_a(work[0][0], work[0][1], work[0][2], 0, work[0][3])
        for n, (item, j, _, _) in enumerate(work):
            if n + 1 < len(work):
                nxt = work[n + 1]
                stage_a(nxt[0], nxt[1], nxt[2], (n + 1) % 2, nxt[3])
            stage_b(item, j, n % 2)

    outs = []
    for p in range(n_par):
        acc = acc_ref[p]
        o = acc[:HEAD_DIM] * (1.0 / acc[HEAD_DIM:HEAD_DIM + 1])
        if has_gate:
            o = o * _gate_row(gate_ref, n_rep)
        outs += [o[:, r * tq:(r + 1) * tq] for r in range(n_rep)]
    out_ref[0] = jnp.concatenate(outs, axis=0).T.astype(out_ref.dtype)


def _flash(qT, k_tok, vT_tiles, *, mode, n_par, n_rep, dq, kc, tq, tk, cw, window=None,
           sel_bias=None, sinks=None, gates=None, gate_branch=0, name="flash"):
    b, _, s = qT.shape
    kh, k_lanes = k_tok.shape[1], k_tok.shape[3]
    n_t = s // tk
    k5 = k_tok.reshape(b, kh, n_t, tk, k_lanes)
    n_grp = kh // n_par
    heads = n_par * n_rep
    nq = n_rep * tq
    n_blk = sel_bias.shape[2] if sel_bias is not None else 0
    has_sink, has_gate = sinks is not None, gates is not None
    args = [qT, k5, vT_tiles]
    in_specs = [
        pl.BlockSpec((1, heads * dq, tq), lambda i, j, k: (i, j, k)),
        pl.BlockSpec((1, n_par, n_t, tk, k_lanes), lambda i, j, k: (i, j, 0, 0, 0)),
        pl.BlockSpec((1, n_t, n_par * HEAD_DIM, tk), lambda i, j, k: (i, 0, j, 0)),
    ]
    if n_blk:
        assert dq + n_blk <= kc
        args.append(sel_bias)
        in_specs.append(pl.BlockSpec((1, 1, n_blk, tq), lambda i, j, k: (i, j, 0, k)))
    if has_sink:
        args.append(sinks.astype(F32))
        in_specs.append(pl.BlockSpec(memory_space=pltpu.SMEM))
    if has_gate:
        args.append(gates)
        in_specs.append(pl.BlockSpec((1, 1, 1, n_rep, tq),
                                     lambda i, j, k, _br=gate_branch: (i, _br, j, 0, k)))
    scratch = [pltpu.VMEM((n_par, kc, nq), BF), pltpu.VMEM((n_par, 1, nq), F32),
               pltpu.VMEM((n_par, V_ROWS, nq), F32), pltpu.VMEM((2, 1, cw), F32),
               pltpu.VMEM((tk, cw), F32), pltpu.VMEM((tk, cw), F32)]
    kern = functools.partial(_flash_kernel, mode=mode, n_par=n_par, n_rep=n_rep, dq=dq, kc=kc, tq=tq, tk=tk,
                             cw=cw, window=window, n_blk=n_blk, has_sink=has_sink, has_gate=has_gate)
    return pl.pallas_call(
        kern,
        grid=(b, n_grp, s // tq),
        in_specs=in_specs,
        out_specs=pl.BlockSpec((1, tq, heads * HEAD_DIM), lambda i, j, k: (i, k, j)),
        out_shape=jax.ShapeDtypeStruct((b, s, MIX_WIDTH), BF),
        scratch_shapes=scratch,
        compiler_params=_params(("parallel", "parallel", "arbitrary")),
        name=name,
    )(*args)


def _swa_proj_kernel(h_ref, cos_ref, sin_ref, wq_ref, wk_ref, wv_ref, wz_ref, qg_ref, kg_ref,
                     q_out, k_out, v_out, z_out, y_ref, *, win_tile):
    h = h_ref[0]
    cos, sin = cos_ref[0], sin_ref[0]
    y_ref[...] = _dot_nt(wq_ref[...], h)
    for hd in range(N_HEADS):
        rows = slice(hd * HEAD_DIM, (hd + 1) * HEAD_DIM)
        q_out[0, rows, :] = (_rope_t(_head_rms_t(y_ref[rows, :], qg_ref[...]), cos, sin) * Q_SCALE).astype(BF)
    kvd = SWA_KV_HEADS * HEAD_DIM
    y_ref[0:kvd, :] = _dot_nt(wk_ref[...], h)
    for gi in range(SWA_KV_HEADS):
        rows = slice(gi * HEAD_DIM, (gi + 1) * HEAD_DIM)
        k_out[0, gi] = _to_token_major(_rope_t(_head_rms_t(y_ref[rows, :], kg_ref[...]), cos, sin)).astype(BF)
    _store_lane_tiles(v_out, _dot_nt(wv_ref[...], h).astype(BF), win_tile)
    z_out[0] = _silu(_dot(h, wz_ref[...])).astype(BF)


def _swa_proj(h, cos, sin, w_in, q_gain, k_gain, *, win_tile):
    b, s, d = h.shape
    g = SWA_KV_HEADS
    kvd = g * HEAD_DIM
    tm = min(TOK_TILE, s)
    wt = w_in.T.astype(BF)
    wq, wk, wv = wt[:MIX_WIDTH], wt[MIX_WIDTH:MIX_WIDTH + kvd], wt[MIX_WIDTH + kvd:MIX_WIDTH + 2 * kvd]
    wz = w_in[:, MIX_WIDTH + 2 * kvd:].astype(BF)
    qg, kg = q_gain.reshape(HEAD_DIM, 1), k_gain.reshape(HEAD_DIM, 1)
    wpt = tm // win_tile
    half = HEAD_DIM // 2
    return pl.pallas_call(
        functools.partial(_swa_proj_kernel, win_tile=win_tile),
        grid=(b, s // tm),
        in_specs=[pl.BlockSpec((1, tm, d), lambda i, j: (i, j, 0)),
                  pl.BlockSpec((1, half, tm), lambda i, j: (i, 0, j)),
                  pl.BlockSpec((1, half, tm), lambda i, j: (i, 0, j)),
                  _full(wq.shape), _full(wk.shape), _full(wv.shape), _full(wz.shape),
                  _full(qg.shape), _full(kg.shape)],
        out_specs=[pl.BlockSpec((1, MIX_WIDTH, tm), lambda i, j: (i, 0, j)),
                   pl.BlockSpec((1, g, tm, KEY_PAD), lambda i, j: (i, 0, j, 0)),
                   pl.BlockSpec((1, wpt, kvd, win_tile), lambda i, j: (i, j, 0, 0)),
                   pl.BlockSpec((1, tm, MIX_WIDTH), lambda i, j: (i, j, 0))],
        out_shape=[jax.ShapeDtypeStruct((b, MIX_WIDTH, s), BF),
                   jax.ShapeDtypeStruct((b, g, s, KEY_PAD), BF),
                   jax.ShapeDtypeStruct((b, s // win_tile, kvd, win_tile), BF),
                   jax.ShapeDtypeStruct((b, s, MIX_WIDTH), BF)],
        scratch_shapes=[pltpu.VMEM((MIX_WIDTH, tm), F32)],
        compiler_params=_params(("parallel", "parallel")),
        name="swa_proj",
    )(h, cos, sin, wq, wk, wv, wz, qg, kg)


def _split3(x):
    hi = x.astype(BF)
    r1 = x - hi.astype(F32)
    mid = r1.astype(BF)
    lo = (r1 - mid.astype(F32)).astype(BF)
    return hi, mid, lo


def _forget_cum_kernel(h_ref, wf_ref, bias_ref, tri_ref, cum_out, carry_ref):
    @pl.when(pl.program_id(1) == 0)
    def _():
        carry_ref[...] = jnp.zeros_like(carry_ref)

    x = _dot_nt(wf_ref[...], h_ref[0]) + bias_ref[...]
    logf = jnp.minimum(x, 0.0) - jnp.log(1.0 + jnp.exp(-jnp.abs(x)))
    tri = tri_ref[...]
    hi, mid, lo = _split3(logf)
    cum = (_dot(hi, tri) + _dot(mid, tri)) + _dot(lo, tri) + carry_ref[:, 0:1]
    cum_out[0] = cum
    carry_ref[...] = jnp.broadcast_to(cum[:, -1:], carry_ref.shape)


def _forget_cum(h, wf, bias):
    b, s, d = h.shape
    tm = min(TOK_TILE, s)
    tri = jnp.asarray(np.arange(tm)[:, None] <= np.arange(tm)[None, :], BF)
    return pl.pallas_call(
        _forget_cum_kernel,
        grid=(b, s // tm),
        in_specs=[pl.BlockSpec((1, tm, d), lambda i, j: (i, j, 0)),
                  _full(wf.shape), _full((N_HEADS, 1)), _full(tri.shape)],
        out_specs=pl.BlockSpec((1, N_HEADS, tm), lambda i, j: (i, 0, j)),
        out_shape=jax.ShapeDtypeStruct((b, N_HEADS, s), F32),
        scratch_shapes=[pltpu.VMEM((N_HEADS, LANES), F32)],
        compiler_params=_params(("parallel", "arbitrary")),
        name="fox_forget_cum",
    )(h, wf, bias.reshape(N_HEADS, 1).astype(F32), tri)


def _fox_proj_kernel(h_ref, cum_ref, wq_ref, wk_ref, wv_ref, wz_ref, qg_ref, kg_ref,
                     q_out, k_out, v_out, z_out, yq_ref, yk_ref, *, tile):
    h = h_ref[0]
    tm = h.shape[0]
    yq_ref[...] = _dot_nt(wq_ref[...], h)
    yk_ref[...] = _dot_nt(wk_ref[...], h)
    row = lax.broadcasted_iota(jnp.int32, (8, tm), 0)
    zeros = jnp.zeros((KEY_PAD - HEAD_DIM - 16, tm), F32)
    for hd in range(N_HEADS):
        rows = slice(hd * HEAD_DIM, (hd + 1) * HEAD_DIM)
        c_hi, c_mid, c_lo = (c.astype(F32) for c in _split3(cum_ref[0, hd:hd + 1, :] * LOG2E))
        c3 = jnp.where(row == 0, c_hi, jnp.where(row == 1, c_mid, jnp.where(row == 2, c_lo, 0.0)))
        one3 = jnp.where(row < 3, 1.0, 0.0)
        q = _head_rms_t(yq_ref[rows, :], qg_ref[...]) * Q_SCALE
        q_out[0, hd] = jnp.concatenate([q, one3, c3, zeros], axis=0).astype(BF)
        k = _head_rms_t(yk_ref[rows, :], kg_ref[...])
        k_out[0, hd] = jnp.concatenate([k, -c3, one3, zeros], axis=0).T.astype(BF)
    _store_lane_tiles(v_out, _dot_nt(wv_ref[...], h).astype(BF), tile)
    z_out[0] = _silu(_dot(h, wz_ref[...])).astype(BF)


def _fox_proj(h, cum, w_in, q_gain, k_gain, *, tile):
    b, s, d = h.shape
    tm = min(TOK_TILE, s)
    wt = w_in.T.astype(BF)
    wq, wk, wv = wt[:MIX_WIDTH], wt[MIX_WIDTH:2 * MIX_WIDTH], wt[2 * MIX_WIDTH:3 * MIX_WIDTH]
    wz = w_in[:, 3 * MIX_WIDTH + N_HEADS:].astype(BF)
    qg, kg = q_gain.reshape(HEAD_DIM, 1), k_gain.reshape(HEAD_DIM, 1)
    return pl.pallas_call(
        functools.partial(_fox_proj_kernel, tile=tile),
        grid=(b, s // tm),
        in_specs=[pl.BlockSpec((1, tm, d), lambda i, j: (i, j, 0)),
                  pl.BlockSpec((1, N_HEADS, tm), lambda i, j: (i, 0, j)),
                  _full(wq.shape), _full(wk.shape), _full(wv.shape), _full(wz.shape),
                  _full(qg.shape), _full(kg.shape)],
        out_specs=[pl.BlockSpec((1, N_HEADS, KEY_PAD, tm), lambda i, j: (i, 0, 0, j)),
                   pl.BlockSpec((1, N_HEADS, tm, KEY_PAD), lambda i, j: (i, 0, j, 0)),
                   pl.BlockSpec((1, tm // tile, MIX_WIDTH, tile), lambda i, j: (i, j, 0, 0)),
                   pl.BlockSpec((1, tm, MIX_WIDTH), lambda i, j: (i, j, 0))],
        out_shape=[jax.ShapeDtypeStruct((b, N_HEADS, KEY_PAD, s), BF),
                   jax.ShapeDtypeStruct((b, N_HEADS, s, KEY_PAD), BF),
                   jax.ShapeDtypeStruct((b, s // tile, MIX_WIDTH, tile), BF),
                   jax.ShapeDtypeStruct((b, s, MIX_WIDTH), BF)],
        scratch_shapes=[pltpu.VMEM((MIX_WIDTH, tm), F32), pltpu.VMEM((MIX_WIDTH, tm), F32)],
        compiler_params=_params(("parallel", "parallel")),
        name="fox_proj",
    )(h, cum, wq, wk, wv, wz, qg, kg)


def _out_proj_kernel(*refs, n_o, has_next):
    o_refs = refs[:n_o]
    z_ref, x_ref, w_ref = refs[n_o:n_o + 3]
    rest = refs[n_o + 3:]
    if has_next:
        g_ref, x_out, h_out = rest
    else:
        (x_out,) = rest
    o = o_refs[0][0].astype(F32)
    for r in o_refs[1:]:
        o = o + r[0].astype(F32)
    y = _dot((o * z_ref[0].astype(F32)).astype(BF), w_ref[...])
    x_new = x_ref[0] + y
    x_out[0] = x_new
    if has_next:
        h_out[0] = _rms_rows(x_new, g_ref[...]).astype(BF)


def _out_proj(o_list, zs, x, w_out, next_gain):
    b, s, d = x.shape
    tm = min(TOK_TILE, s)
    has_next = next_gain is not None
    blk = pl.BlockSpec((1, tm, d), lambda i, j: (i, j, 0))
    args = list(o_list) + [zs, x, w_out.astype(BF)]
    in_specs = [blk] * (len(o_list) + 2) + [_full(w_out.shape)]
    out_shape = [jax.ShapeDtypeStruct((b, s, d), F32)]
    out_specs = [blk]
    if has_next:
        args.append(next_gain.reshape(1, d))
        in_specs.append(_full((1, d)))
        out_shape.append(jax.ShapeDtypeStruct((b, s, d), BF))
        out_specs.append(blk)
    res = pl.pallas_call(
        functools.partial(_out_proj_kernel, n_o=len(o_list), has_next=has_next),
        grid=(b, s // tm), in_specs=in_specs, out_specs=out_specs, out_shape=out_shape,
        compiler_params=_params(("parallel", "parallel")),
        name="out_proj",
    )(*args)
    return (res[0], res[1]) if has_next else (res[0], None)


ITEM_LANES = 512
CMP_Q_TILE = 256
NSA_Q_TILE = 256
NSA_SLC_TILE = 512
NSA_WIN_TILE = 256
SWA_TILE = 128
FOX_Q_TILE = 512
FOX_K_TILE = 512


def _nsa_mixer(h, cos, sin, w_in, q_gain, k_gain, cmp_pos, cmp_w1, cmp_w2):
    b, s, _ = h.shape
    g = NSA_KV_HEADS
    r = N_HEADS // g
    qT, kc_tok, vc_tok, ks, kw, vsT, vwT, gates, zs = _nsa_proj(
        h, cos, sin, w_in, q_gain, k_gain, slc_tile=NSA_SLC_TILE, win_tile=NSA_WIN_TILE)
    k_cmp, v_cmpT = _compress(kc_tok, vc_tok, cmp_pos, cmp_w1, cmp_w2, k_gain[0])
    gates5 = gates.reshape(b, 3, g, r, s)
    o_cmp, sel_bias = _cmp_select(qT, k_cmp, v_cmpT, gates5, tq=CMP_Q_TILE)
    o_slc = _flash(qT, ks, vsT, mode="causal", n_par=1, n_rep=r, dq=HEAD_DIM, kc=SLC_KEY_LANES,
                   tq=NSA_Q_TILE, tk=NSA_SLC_TILE, cw=ITEM_LANES, sel_bias=sel_bias,
                   gates=gates5, gate_branch=1, name="nsa_selected")
    o_win = _flash(qT, kw, vwT, mode="window", n_par=1, n_rep=r, dq=HEAD_DIM, kc=HEAD_DIM,
                   tq=NSA_Q_TILE, tk=NSA_WIN_TILE, cw=ITEM_LANES, window=NSA_WINDOW,
                   gates=gates5, gate_branch=2, name="nsa_window")
    return [o_cmp, o_slc, o_win], zs


def _swa_mixer(h, cos, sin, w_in, q_gain, k_gain, sinks):
    r = N_HEADS // SWA_KV_HEADS
    qT, k, vT, zs = _swa_proj(h, cos, sin, w_in, q_gain, k_gain, win_tile=SWA_TILE)
    o = _flash(qT, k, vT, mode="window", n_par=1, n_rep=r, dq=HEAD_DIM, kc=HEAD_DIM, tq=SWA_TILE,
               tk=SWA_TILE, cw=ITEM_LANES, window=SWA_WINDOW, sinks=sinks, name="swa_window")
    return [o], zs


def _fox_mixer(h, w_in, forget_bias, q_gain, k_gain):
    b, s, _ = h.shape
    wf = w_in[:, 3 * MIX_WIDTH:3 * MIX_WIDTH + N_HEADS].T.astype(BF)
    cum = _forget_cum(h, wf, forget_bias)
    qT, k, vT, zs = _fox_proj(h, cum, w_in, q_gain, k_gain, tile=FOX_K_TILE)
    o = _flash(qT.reshape(b, N_HEADS * KEY_PAD, s), k, vT, mode="causal", n_par=2, n_rep=1, dq=KEY_PAD,
               kc=KEY_PAD, tq=FOX_Q_TILE, tk=FOX_K_TILE, cw=ITEM_LANES, name="fox_attention")
    return [o], zs


def kernel(x, positions, norm_gains, a_w_in, a_q_gain, a_k_gain, a_cmp_pos, a_cmp_w1, a_cmp_w2, a_w_out,
           b_w_in, b_q_gain, b_k_gain, b_sinks, b_w_out,
           c_w_in, c_forget_bias, c_q_gain, c_k_gain, c_w_out):
    depth = norm_gains.shape[0]
    cos, sin = _rope_tables(positions)
    h = _prenorm(x, norm_gains[0])
    for i in range(depth):
        j, mixer = divmod(i, 3)
        if mixer == 0:
            o_list, zs = _nsa_mixer(h, cos, sin, a_w_in[j], a_q_gain[j], a_k_gain[j],
                                    a_cmp_pos[j], a_cmp_w1[j], a_cmp_w2[j])
            w_out = a_w_out[j]
        elif mixer == 1:
            o_list, zs = _swa_mixer(h, cos, sin, b_w_in[j], b_q_gain[j], b_k_gain[j], b_sinks[j])
            w_out = b_w_out[j]
        else:
            o_list, zs = _fox_mixer(h, c_w_in[j], c_forget_bias[j], c_q_gain[j], c_k_gain[j])
            w_out = c_w_out[j]
        next_gain = norm_gains[i + 1] if i + 1 < depth else None
        x, h = _out_proj(o_list, zs, x, w_out, next_gain)
    return x
```

```python
import functools

import jax
import jax.numpy as jnp
import numpy as np
from jax import lax
from jax.experimental import pallas as pl
from jax.experimental.pallas import tpu as pltpu

D_MODEL = 1024
HEAD_DIM = 64
N_HEADS = 16
MIX_WIDTH = N_HEADS * HEAD_DIM
ROPE_THETA = 10000.0
EPS = 1e-6
SCALE = HEAD_DIM ** -0.5
NEG_INF = -1e30
BIG = 1e30
M_INIT = -1e29
SEL_OFF = -(2.0 ** 100)

NSA_KV_HEADS = 4
NSA_CMP_LEN = 32
NSA_CMP_STRIDE = 16
NSA_SLC_LEN = 64
NSA_TOPK = 16
NSA_WINDOW = 512
SWA_KV_HEADS = 2
SWA_WINDOW = 128

LOG2E = float(np.log2(np.e))
Q_SCALE = SCALE * LOG2E

LANES = 128
KEY_PAD = 128
SLC_KEY_LANES = 256
V_ROWS = 80
FLASH_UNROLL = 4
CMP_BLOCK = 128
CMP_MASK_ROWS = 2 * CMP_BLOCK
VMEM_LIMIT = 56 * 1024 * 1024

TOK_TILE = 512
NT_DIMS = (((1,), (1,)), ((), ()))

BF = jnp.bfloat16
F32 = jnp.float32


def _params(sem):
    return pltpu.CompilerParams(dimension_semantics=sem, vmem_limit_bytes=VMEM_LIMIT)


def _dot(a, b):
    return jnp.dot(a, b, preferred_element_type=F32)


def _dot_nt(a, b):
    return lax.dot_general(a, b, NT_DIMS, preferred_element_type=F32)


def _rope_tab_kernel(pos_ref, invf_ref, cos_ref, sin_ref):
    ang = invf_ref[...] * pos_ref[0].astype(F32)
    cos_ref[0] = jnp.cos(ang)
    sin_ref[0] = jnp.sin(ang)


def _rope_tables(positions):
    b, s = positions.shape
    half = HEAD_DIM // 2
    inv_freq = ROPE_THETA ** (-jnp.arange(half, dtype=F32) * 2.0 / HEAD_DIM)
    tm = min(TOK_TILE, s)
    out = jax.ShapeDtypeStruct((b, half, s), F32)
    return pl.pallas_call(
        _rope_tab_kernel,
        grid=(b, s // tm),
        in_specs=[pl.BlockSpec((1, 1, tm), lambda i, j: (i, 0, j)),
                  pl.BlockSpec((half, 1), lambda i, j: (0, 0))],
        out_specs=[pl.BlockSpec((1, half, tm), lambda i, j: (i, 0, j))] * 2,
        out_shape=[out, out],
        compiler_params=_params(("parallel", "parallel")),
        name="rope_tables",
    )(positions.reshape(b, 1, s), inv_freq.reshape(half, 1))


def _rms_rows(x, gain_row):
    y = x * lax.rsqrt(jnp.mean(x * x, axis=-1, keepdims=True) + EPS)
    return y * gain_row


def _prenorm_kernel(x_ref, g_ref, h_ref):
    h_ref[0] = _rms_rows(x_ref[0], g_ref[...]).astype(BF)


def _prenorm(x, gain):
    b, s, d = x.shape
    tm = min(TOK_TILE, s)
    return pl.pallas_call(
        _prenorm_kernel,
        grid=(b, s // tm),
        in_specs=[pl.BlockSpec((1, tm, d), lambda i, j: (i, j, 0)),
                  pl.BlockSpec((1, d), lambda i, j: (0, 0))],
        out_specs=pl.BlockSpec((1, tm, d), lambda i, j: (i, j, 0)),
        out_shape=jax.ShapeDtypeStruct((b, s, d), BF),
        compiler_params=_params(("parallel", "parallel")),
        name="prenorm",
    )(x, gain.reshape(1, d))


def _head_rms_t(y, gain_col):
    ms = jnp.mean(y * y, axis=0, keepdims=True)
    return (y * lax.rsqrt(ms + EPS)) * gain_col


def _rope_t(y, cos, sin):
    half = HEAD_DIM // 2
    x1, x2 = y[:half], y[half:]
    return jnp.concatenate([x1 * cos - x2 * sin, x2 * cos + x1 * sin], axis=0)


def _to_token_major(y):
    pad = jnp.zeros((KEY_PAD - y.shape[0], y.shape[1]), y.dtype)
    return jnp.concatenate([y, pad], axis=0).T


def _silu(z):
    return z * (1.0 / (1.0 + jnp.exp(-z)))


def _sigmoid(z):
    return 1.0 / (1.0 + jnp.exp(-z))


def _store_lane_tiles(out_ref, y, tile):
    for c in range(y.shape[1] // tile):
        out_ref[0, c] = y[:, c * tile:(c + 1) * tile]


def _nsa_proj_kernel(h_ref, cos_ref, sin_ref, wq_ref, wk_ref, wv_ref, wg_ref, wz_ref,
                     qg_ref, kg_ref,
                     q_out, kc_out, vc_out, ks_out, kw_out, vs_out, vw_out, g_out, z_out,
                     y_ref, *, slc_tile, win_tile):
    h = h_ref[0]
    cos, sin = cos_ref[0], sin_ref[0]
    g = NSA_KV_HEADS
    kvd = g * HEAD_DIM
    y_ref[...] = _dot_nt(wq_ref[...], h)
    for hd in range(N_HEADS):
        rows = slice(hd * HEAD_DIM, (hd + 1) * HEAD_DIM)
        y = _rope_t(_head_rms_t(y_ref[rows, :], qg_ref[...]), cos, sin) * Q_SCALE
        q_out[0, rows, :] = y.astype(BF)
    y_ref[0:3 * kvd, :] = _dot_nt(wk_ref[...], h)
    tm = h.shape[0]
    tok = pl.program_id(1) * tm + lax.broadcasted_iota(jnp.int32, (tm, 1), 0)
    blk_lane = HEAD_DIM + lax.shift_right_logical(tok, int(np.log2(NSA_SLC_LEN)))
    blk_hot = lax.broadcasted_iota(jnp.int32, (1, SLC_KEY_LANES), 1) == blk_lane
    for kind in range(3):
        for gi in range(g):
            r0 = (kind * g + gi) * HEAD_DIM
            y = y_ref[r0:r0 + HEAD_DIM, :]
            if kind > 0:
                y = _head_rms_t(y, kg_ref[:, kind:kind + 1])
            yt = _to_token_major(_rope_t(y, cos, sin))
            if kind == 0:
                kc_out[0, gi] = yt[:, :HEAD_DIM]
            elif kind == 1:
                wide = jnp.concatenate([yt, jnp.zeros((tm, SLC_KEY_LANES - KEY_PAD), F32)], axis=1)
                ks_out[0, gi] = jnp.where(blk_hot, 1.0, wide).astype(BF)
            else:
                kw_out[0, gi] = yt.astype(BF)
    y_ref[0:3 * kvd, :] = _dot_nt(wv_ref[...], h)
    for gi in range(g):
        r0 = gi * HEAD_DIM
        vc_out[0, gi] = _to_token_major(y_ref[r0:r0 + HEAD_DIM, :])[:, :HEAD_DIM]
    _store_lane_tiles(vs_out, y_ref[kvd:2 * kvd, :].astype(BF), slc_tile)
    _store_lane_tiles(vw_out, y_ref[2 * kvd:3 * kvd, :].astype(BF), win_tile)
    g_out[0] = _sigmoid(_dot_nt(wg_ref[...], h))
    z_out[0] = _silu(_dot(h, wz_ref[...])).astype(BF)


def _full(shape):
    nd = len(shape)
    return pl.BlockSpec(shape, lambda i, j, _n=nd: (0,) * _n)


def _nsa_proj(h, cos, sin, w_in, q_gain, k_gain, *, slc_tile, win_tile):
    b, s, d = h.shape
    g = NSA_KV_HEADS
    kvd = g * HEAD_DIM
    tm = min(TOK_TILE, s)
    sizes = [MIX_WIDTH] + [kvd] * 6 + [3 * N_HEADS]
    off = np.cumsum([0] + sizes)
    wt = w_in.T.astype(BF)
    wq = wt[off[0]:off[1]]
    wk = jnp.concatenate([wt[off[1]:off[2]], wt[off[3]:off[4]], wt[off[5]:off[6]]], axis=0)
    wv = jnp.concatenate([wt[off[2]:off[3]], wt[off[4]:off[5]], wt[off[6]:off[7]]], axis=0)
    wg = wt[off[7]:off[8]]
    wz = w_in[:, off[8]:].astype(BF)
    qg = q_gain.reshape(HEAD_DIM, 1)
    kg = k_gain.T
    n_t = s // tm
    out_shape = [
        jax.ShapeDtypeStruct((b, MIX_WIDTH, s), BF),
        jax.ShapeDtypeStruct((b, g, s, HEAD_DIM), F32),
        jax.ShapeDtypeStruct((b, g, s, HEAD_DIM), F32),
        jax.ShapeDtypeStruct((b, g, s, SLC_KEY_LANES), BF),
        jax.ShapeDtypeStruct((b, g, s, KEY_PAD), BF),
        jax.ShapeDtypeStruct((b, s // slc_tile, kvd, slc_tile), BF),
        jax.ShapeDtypeStruct((b, s // win_tile, kvd, win_tile), BF),
        jax.ShapeDtypeStruct((b, 3 * N_HEADS, s), F32),
        jax.ShapeDtypeStruct((b, s, MIX_WIDTH), BF),
    ]
    out_specs = [
        pl.BlockSpec((1, MIX_WIDTH, tm), lambda i, j: (i, 0, j)),
        pl.BlockSpec((1, g, tm, HEAD_DIM), lambda i, j: (i, 0, j, 0)),
        pl.BlockSpec((1, g, tm, HEAD_DIM), lambda i, j: (i, 0, j, 0)),
        pl.BlockSpec((1, g, tm, SLC_KEY_LANES), lambda i, j: (i, 0, j, 0)),
        pl.BlockSpec((1, g, tm, KEY_PAD), lambda i, j: (i, 0, j, 0)),
        pl.BlockSpec((1, tm // slc_tile, kvd, slc_tile), lambda i, j: (i, j, 0, 0)),
        pl.BlockSpec((1, tm // win_tile, kvd, win_tile), lambda i, j: (i, j, 0, 0)),
        pl.BlockSpec((1, 3 * N_HEADS, tm), lambda i, j: (i, 0, j)),
        pl.BlockSpec((1, tm, MIX_WIDTH), lambda i, j: (i, j, 0)),
    ]
    in_specs = [
        pl.BlockSpec((1, tm, d), lambda i, j: (i, j, 0)),
        pl.BlockSpec((1, HEAD_DIM // 2, tm), lambda i, j: (i, 0, j)),
        pl.BlockSpec((1, HEAD_DIM // 2, tm), lambda i, j: (i, 0, j)),
        _full(wq.shape), _full(wk.shape), _full(wv.shape), _full(wg.shape), _full(wz.shape),
        _full(qg.shape), _full(kg.shape),
    ]
    return pl.pallas_call(
        functools.partial(_nsa_proj_kernel, slc_tile=slc_tile, win_tile=win_tile),
        grid=(b, n_t), in_specs=in_specs, out_specs=out_specs, out_shape=out_shape,
        scratch_shapes=[pltpu.VMEM((MIX_WIDTH, tm), F32)],
        compiler_params=_params(("parallel", "parallel")),
        name="nsa_proj",
    )(h, cos, sin, wq, wk, wv, wg, wz, qg, kg)


def _gelu_tanh(x):
    c = np.float32(np.sqrt(2.0 / np.pi))
    return 0.5 * x * (1.0 + jnp.tanh(c * (x + 0.044715 * (x * x * x))))


def _compress_kernel(kc_ref, vc_ref, pos_ref, w1_ref, w2_ref, kg_ref, kcmp_out, vcmp_out):
    for which, (src, dst) in enumerate(((kc_ref, kcmp_out), (vc_ref, vcmp_out))):
        x = src[0, 0]
        n = x.shape[0]
        half = x.shape[1]
        xa = (x + pos_ref[which, 0:1, :]).astype(BF)
        xb = (x + pos_ref[which, 1:2, :]).astype(BF)
        ua = _dot(xa, w1_ref[which, :half, :])
        ub = _dot(xb, w1_ref[which, half:, :])
        row = lax.broadcasted_iota(jnp.int32, (n, 1), 0)
        ub_next = jnp.where(row == n - 1, 0.0, pltpu.roll(ub, n - 1, 0))
        hid = _gelu_tanh(ua + ub_next)
        y = _dot(hid.astype(BF), w2_ref[which])
        if which == 0:
            y = _rms_rows(y, kg_ref[...])
            dst[0, 0] = y.astype(BF)
        else:
            pad = jnp.zeros((n, KEY_PAD - HEAD_DIM), F32)
            dst[0, 0] = jnp.concatenate([y, pad], axis=1).T[:HEAD_DIM].astype(BF)


def _compress(kc_tok, vc_tok, cmp_pos, cmp_w1, cmp_w2, k_gain0):
    b, g, s, _ = kc_tok.shape
    n_chunk = s // NSA_CMP_STRIDE
    flat = NSA_CMP_STRIDE * HEAD_DIM
    kc = kc_tok.reshape(b, g, n_chunk, flat)
    vc = vc_tok.reshape(b, g, n_chunk, flat)
    pos = cmp_pos.reshape(2, 2, flat)
    w1 = cmp_w1.astype(BF)
    w2 = cmp_w2.astype(BF)
    blk = pl.BlockSpec((1, 1, n_chunk, flat), lambda i, j: (i, j, 0, 0))
    return pl.pallas_call(
        _compress_kernel,
        grid=(b, g),
        in_specs=[blk, blk, _full(pos.shape), _full(w1.shape), _full(w2.shape),
                  _full((1, HEAD_DIM))],
        out_specs=[pl.BlockSpec((1, 1, n_chunk, HEAD_DIM), lambda i, j: (i, j, 0, 0)),
                   pl.BlockSpec((1, 1, HEAD_DIM, n_chunk), lambda i, j: (i, j, 0, 0))],
        out_shape=[jax.ShapeDtypeStruct((b, g, n_chunk, HEAD_DIM), BF),
                   jax.ShapeDtypeStruct((b, g, HEAD_DIM, n_chunk), BF)],
        compiler_params=_params(("parallel", "parallel")),
        name="nsa_compress",
    )(kc, vc, pos, w1, w2, k_gain0.reshape(1, HEAD_DIM))


def _stack_heads(q_ref, n_heads, dk):
    return jnp.concatenate([q_ref[0, r * dk:(r + 1) * dk, :] for r in range(n_heads)], axis=1)


def _unstack_store(o, out_ref, n_heads, tq):
    rows = jnp.concatenate([o[:, r * tq:(r + 1) * tq] for r in range(n_heads)], axis=0)
    out_ref[0] = rows.T.astype(out_ref.dtype)


def _gate_row(gate_ref, n_heads):
    return jnp.concatenate([gate_ref[0, 0, 0, r:r + 1, :] for r in range(n_heads)], axis=1)


def _cmp_branch(rows, q_ref, kc_ref, vc_ref, ov_ref, gate_ref, o_out, sel_out, *, tq, n_blk):
    r_heads = N_HEADS // NSA_KV_HEADS
    nq = r_heads * tq
    q0 = pl.program_id(2) * tq
    n_cmp = kc_ref.shape[2] - 1
    q4 = _stack_heads(q_ref, r_heads, HEAD_DIM)
    s = _dot(kc_ref[0, 0, 0:rows, :], q4)
    t_row = q0 + (lax.broadcasted_iota(jnp.int32, (1, nq), 1) & (tq - 1))
    lo = max(rows - CMP_MASK_ROWS, 0)
    c_col = lo + lax.broadcasted_iota(jnp.int32, (rows - lo, 1), 0)
    valid = (c_col * NSA_CMP_STRIDE + (NSA_CMP_LEN - 1) <= t_row) & (c_col < n_cmp)
    s_new = jnp.where(valid, s[lo:], NEG_INF)
    m = jnp.max(s_new, axis=0, keepdims=True)
    if lo:
        m = jnp.maximum(m, jnp.max(s[:lo], axis=0, keepdims=True))
    e = jnp.where(valid, jnp.exp2(s_new - m), 0.0).astype(BF)
    if lo:
        e = jnp.concatenate([jnp.exp2(s[:lo] - m).astype(BF), e], axis=0)
    one_row = jnp.where(lax.broadcasted_iota(jnp.int32, (V_ROWS - HEAD_DIM, rows), 0) == 0, 1.0, 0.0).astype(BF)
    n_live = min(n_blk, rows * NSA_CMP_STRIDE // NSA_SLC_LEN + 8)
    lhs = jnp.concatenate([vc_ref[0, 0, :, 0:rows], one_row, ov_ref[0:n_live, 0:rows]], axis=0)
    res = _dot(lhs, e)
    l = res[HEAD_DIM:HEAD_DIM + 1]
    inv = jnp.where(l > 0.0, 1.0 / jnp.where(l > 0.0, l, 1.0), 0.0)
    o = res[:HEAD_DIM] * (inv * _gate_row(gate_ref, r_heads))
    _unstack_store(o, o_out, r_heads, tq)
    w = res[V_ROWS:V_ROWS + n_live] * inv
    imp = w[:, 0:tq]
    for r in range(1, r_heads):
        imp = imp + w[:, r * tq:(r + 1) * tq]
    t1 = q0 + lax.broadcasted_iota(jnp.int32, (1, tq), 1)
    cur = lax.shift_right_logical(t1, int(np.log2(NSA_SLC_LEN)))
    blk = lax.broadcasted_iota(jnp.int32, (n_live, tq), 0)
    forced = (blk == 0) | (blk == cur) | (blk == cur - 1)
    imp = jnp.where(forced, BIG, jnp.where(blk > cur, NEG_INF, imp))
    for _ in range(min(NSA_TOPK, n_blk)):
        best = jnp.max(imp, axis=0, keepdims=True)
        first = jnp.min(jnp.where(imp == best, blk, n_blk), axis=0, keepdims=True)
        imp = jnp.where(blk == first, -jnp.inf, imp)
    sel_out[0, 0, 0:n_live, :] = jnp.where(imp == -jnp.inf, 0.0, SEL_OFF).astype(BF)
    if n_live < n_blk:
        sel_out[0, 0, n_live:n_blk, :] = jnp.full((n_blk - n_live, tq), SEL_OFF, BF)


def _cmp_select_kernel(q_ref, kc_ref, vc_ref, ov_ref, gate_ref, o_out, sel_out, *, tq, n_blk):
    q0 = pl.program_id(2) * tq
    n_chunk = kc_ref.shape[2]
    n_need = jnp.minimum((q0 + tq - NSA_CMP_LEN) // NSA_CMP_STRIDE + 1, n_chunk - 1)
    n_steps = n_chunk // CMP_BLOCK
    need_steps = (n_need + CMP_BLOCK - 1) // CMP_BLOCK
    for k in range(1, n_steps + 1):
        @pl.when(need_steps == k)
        def _(k=k):
            _cmp_branch(k * CMP_BLOCK, q_ref, kc_ref, vc_ref, ov_ref, gate_ref, o_out, sel_out, tq=tq, n_blk=n_blk)


def _overlap_matrix(s):
    n_chunk = s // NSA_CMP_STRIDE
    n_blk = s // NSA_SLC_LEN
    c0 = np.arange(n_chunk) * NSA_CMP_STRIDE
    c1 = c0 + NSA_CMP_LEN - 1
    b0 = np.arange(n_blk) * NSA_SLC_LEN
    ov = np.minimum(c1[None, :], b0[:, None] + NSA_SLC_LEN - 1) - np.maximum(c0[None, :], b0[:, None]) + 1
    return jnp.asarray(np.clip(ov, 0, None) / NSA_CMP_LEN, BF)


def _cmp_select(qT, k_cmp, v_cmpT, gates5, *, tq):
    b, _, s = qT.shape
    g = NSA_KV_HEADS
    r_heads = N_HEADS // g
    n_chunk = k_cmp.shape[2]
    n_blk = s // NSA_SLC_LEN
    ov = _overlap_matrix(s)
    rows = r_heads * HEAD_DIM
    return pl.pallas_call(
        functools.partial(_cmp_select_kernel, tq=tq, n_blk=n_blk),
        grid=(b, g, s // tq),
        in_specs=[
            pl.BlockSpec((1, rows, tq), lambda i, j, k: (i, j, k)),
            pl.BlockSpec((1, 1, n_chunk, HEAD_DIM), lambda i, j, k: (i, j, 0, 0)),
            pl.BlockSpec((1, 1, HEAD_DIM, n_chunk), lambda i, j, k: (i, j, 0, 0)),
            pl.BlockSpec((n_blk, n_chunk), lambda i, j, k: (0, 0)),
            pl.BlockSpec((1, 1, 1, r_heads, tq), lambda i, j, k: (i, 0, j, 0, k)),
        ],
        out_specs=[pl.BlockSpec((1, tq, rows), lambda i, j, k: (i, k, j)),
                   pl.BlockSpec((1, 1, n_blk, tq), lambda i, j, k: (i, j, 0, k))],
        out_shape=[jax.ShapeDtypeStruct((b, s, MIX_WIDTH), BF),
                   jax.ShapeDtypeStruct((b, g, n_blk, s), BF)],
        compiler_params=_params(("parallel", "parallel", "parallel")),
        name="nsa_cmp_select",
    )(qT, k_cmp, v_cmpT, ov, gates5)


def _flash_kernel(*refs, mode, n_par, n_rep, dq, kc, tq, tk, cw, window, n_blk, has_sink, has_gate):
    it = iter(refs)
    q_ref, k_ref, v_ref = next(it), next(it), next(it)
    sel_ref = next(it) if n_blk else None
    sink_ref = next(it) if has_sink else None
    gate_ref = next(it) if has_gate else None
    out_ref = next(it)
    qs_ref, m_ref, acc_ref, mt_ref, s0_ref, s1_ref = (next(it) for _ in range(6))
    s_bufs = (s0_ref, s1_ref)

    nq = n_rep * tq
    items = [(p, c) for p in range(n_par) for c in range(nq // cw)]
    n_items = len(items)
    grp = pl.program_id(1)
    q0 = pl.program_id(2) * tq
    acc_row = lax.broadcasted_iota(jnp.int32, (V_ROWS, nq), 0)
    for p in range(n_par):
        for r in range(n_rep):
            hd = p * n_rep + r
            qs_ref[p, 0:dq, r * tq:(r + 1) * tq] = q_ref[0, hd * dq:(hd + 1) * dq, :]
        if n_blk:
            qs_ref[p, dq:dq + n_blk, :] = jnp.concatenate([sel_ref[0, 0]] * n_rep, axis=1)
            if dq + n_blk < kc:
                qs_ref[p, dq + n_blk:kc, :] = jnp.zeros((kc - dq - n_blk, nq), BF)
        if has_sink:
            m_ref[p] = jnp.concatenate(
                [jnp.full((1, tq), sink_ref[(grp * n_par + p) * n_rep + r] * LOG2E, F32) for r in range(n_rep)],
                axis=1)
            acc_ref[p] = jnp.where(acc_row == HEAD_DIM, 1.0, 0.0)
        else:
            m_ref[p] = jnp.full((1, nq), M_INIT, F32)
            acc_ref[p] = jnp.zeros((V_ROWS, nq), F32)

    t_row = q0 + (lax.broadcasted_iota(jnp.int32, (1, nq), 1) & (tq - 1))
    one_row = jnp.where(lax.broadcasted_iota(jnp.int32, (V_ROWS - HEAD_DIM, tk), 0) == 0, 1.0, 0.0).astype(BF)

    def stage_a(item, j, key0, slot, kind):
        p, c = item
        cols = slice(c * cw, (c + 1) * cw)
        s = _dot(k_ref[0, p, j][:, :kc], qs_ref[p, :, cols])
        if kind is not None:
            key = key0 + lax.broadcasted_iota(jnp.int32, (tk, 1), 0)
            ok = key <= t_row[:, cols] if kind == "causal" else key > t_row[:, cols] - window
            s = jnp.where(ok, s, NEG_INF)
        s_bufs[slot][...] = s
        mt_ref[slot] = jnp.max(s, axis=0, keepdims=True)

    def stage_b(item, j, slot):
        p, c = item
        cols = slice(c * cw, (c + 1) * cw)
        m_old = m_ref[p, :, cols]
        m_new = jnp.maximum(m_old, mt_ref[slot])
        pr = jnp.exp2(s_bufs[slot][...] - m_new).astype(BF)
        alpha = jnp.exp2(m_old - m_new)
        v = jnp.concatenate([v_ref[0, j, p * HEAD_DIM:(p + 1) * HEAD_DIM, :], one_row], axis=0)
        acc_ref[p, :, cols] = alpha * acc_ref[p, :, cols] + _dot(v, pr)
        m_ref[p, :, cols] = m_new

    if mode == "causal":
        assert n_items % 2 == 0
        n_full = q0 // tk

        def step(j, kind, next_kind, last):
            for idx, item in enumerate(items):
                slot = idx % 2
                if idx + 1 < n_items:
                    stage_a(items[idx + 1], j, j * tk, 1 - slot, kind)
                elif not last:
                    stage_a(items[0], j + 1, (j + 1) * tk, 1 - slot, next_kind)
                stage_b(item, j, slot)

        stage_a(items[0], 0, 0, 0, "causal")

        def body(j, carry):
            step(j, None, None, False)
            return carry

        def body_group(i, carry):
            for u in range(FLASH_UNROLL):
                step(FLASH_UNROLL * i + u, None, None, False)
            return carry

        n_main = jnp.maximum(n_full - 1, 0)
        n_groups = lax.shift_right_logical(n_main, int(np.log2(FLASH_UNROLL)))
        lax.fori_loop(0, n_groups, body_group, 0)
        lax.fori_loop(FLASH_UNROLL * n_groups, n_main, body, 0)

        @pl.when(n_full >= 1)
        def _():
            step(n_full - 1, None, "causal", False)

        step(n_full, "causal", None, True)
    else:
        w_tiles, q_tiles = window // tk, tq // tk
        work = []
        for i in range(w_tiles + q_tiles):
            jv = q0 // tk - w_tiles + i
            key0 = jnp.where(jv < 0, -(1 << 30), jv * tk)
            kind = "causal" if i >= w_tiles else "lower"
            work += [(item, jnp.maximum(jv, 0), key0, kind) for item in items]
        stage_a(work[0][0], work[0][1], work[0][2], 0, work[0][3])
        for n, (item, j, _, _) in enumerate(work):
            if n + 1 < len(work):
                nxt = work[n + 1]
                stage_a(nxt[0], nxt[1], nxt[2], (n + 1) % 2, nxt[3])
            stage_b(item, j, n % 2)

    outs = []
    for p in range(n_par):
        acc = acc_ref[p]
        o = acc[:HEAD_DIM] * (1.0 / acc[HEAD_DIM:HEAD_DIM + 1])
        if has_gate:
            o = o * _gate_row(gate_ref, n_rep)
        outs += [o[:, r * tq:(r + 1) * tq] for r in range(n_rep)]
    out_ref[0] = jnp.concatenate(outs, axis=0).T.astype(out_ref.dtype)


def _flash(qT, k_tok, vT_tiles, *, mode, n_par, n_rep, dq, kc, tq, tk, cw, window=None,
           sel_bias=None, sinks=None, gates=None, gate_branch=0, name="flash"):
    b, _, s = qT.shape
    kh, k_lanes = k_tok.shape[1], k_tok.shape[3]
    n_t = s // tk
    k5 = k_tok.reshape(b, kh, n_t, tk, k_lanes)
    n_grp = kh // n_par
    heads = n_par * n_rep
    nq = n_rep * tq
    n_blk = sel_bias.shape[2] if sel_bias is not None else 0
    has_sink, has_gate = sinks is not None, gates is not None
    args = [qT, k5, vT_tiles]
    in_specs = [
        pl.BlockSpec((1, heads * dq, tq), lambda i, j, k: (i, j, k)),
        pl.BlockSpec((1, n_par, n_t, tk, k_lanes), lambda i, j, k: (i, j, 0, 0, 0)),
        pl.BlockSpec((1, n_t, n_par * HEAD_DIM, tk), lambda i, j, k: (i, 0, j, 0)),
    ]
    if n_blk:
        assert dq + n_blk <= kc
        args.append(sel_bias)
        in_specs.append(pl.BlockSpec((1, 1, n_blk, tq), lambda i, j, k: (i, j, 0, k)))
    if has_sink:
        args.append(sinks.astype(F32))
        in_specs.append(pl.BlockSpec(memory_space=pltpu.SMEM))
    if has_gate:
        args.append(gates)
        in_specs.append(pl.BlockSpec((1, 1, 1, n_rep, tq),
                                     lambda i, j, k, _br=gate_branch: (i, _br, j, 0, k)))
    scratch = [pltpu.VMEM((n_par, kc, nq), BF), pltpu.VMEM((n_par, 1, nq), F32),
               pltpu.VMEM((n_par, V_ROWS, nq), F32), pltpu.VMEM((2, 1, cw), F32),
               pltpu.VMEM((tk, cw), F32), pltpu.VMEM((tk, cw), F32)]
    kern = functools.partial(_flash_kernel, mode=mode, n_par=n_par, n_rep=n_rep, dq=dq, kc=kc, tq=tq, tk=tk,
                             cw=cw, window=window, n_blk=n_blk, has_sink=has_sink, has_gate=has_gate)
    return pl.pallas_call(
        kern,
        grid=(b, n_grp, s // tq),
        in_specs=in_specs,
        out_specs=pl.BlockSpec((1, tq, heads * HEAD_DIM), lambda i, j, k: (i, k, j)),
        out_shape=jax.ShapeDtypeStruct((b, s, MIX_WIDTH), BF),
        scratch_shapes=scratch,
        compiler_params=_params(("parallel", "parallel", "arbitrary")),
        name=name,
    )(*args)


def _swa_proj_kernel(h_ref, cos_ref, sin_ref, wq_ref, wk_ref, wv_ref, wz_ref, qg_ref, kg_ref,
                     q_out, k_out, v_out, z_out, y_ref, *, win_tile):
    h = h_ref[0]
    cos, sin = cos_ref[0], sin_ref[0]
    y_ref[...] = _dot_nt(wq_ref[...], h)
    for hd in range(N_HEADS):
        rows = slice(hd * HEAD_DIM, (hd + 1) * HEAD_DIM)
        q_out[0, rows, :] = (_rope_t(_head_rms_t(y_ref[rows, :], qg_ref[...]), cos, sin) * Q_SCALE).astype(BF)
    kvd = SWA_KV_HEADS * HEAD_DIM
    y_ref[0:kvd, :] = _dot_nt(wk_ref[...], h)
    for gi in range(SWA_KV_HEADS):
        rows = slice(gi * HEAD_DIM, (gi + 1) * HEAD_DIM)
        k_out[0, gi] = _to_token_major(_rope_t(_head_rms_t(y_ref[rows, :], kg_ref[...]), cos, sin)).astype(BF)
    _store_lane_tiles(v_out, _dot_nt(wv_ref[...], h).astype(BF), win_tile)
    z_out[0] = _silu(_dot(h, wz_ref[...])).astype(BF)


def _swa_proj(h, cos, sin, w_in, q_gain, k_gain, *, win_tile):
    b, s, d = h.shape
    g = SWA_KV_HEADS
    kvd = g * HEAD_DIM
    tm = min(TOK_TILE, s)
    wt = w_in.T.astype(BF)
    wq, wk, wv = wt[:MIX_WIDTH], wt[MIX_WIDTH:MIX_WIDTH + kvd], wt[MIX_WIDTH + kvd:MIX_WIDTH + 2 * kvd]
    wz = w_in[:, MIX_WIDTH + 2 * kvd:].astype(BF)
    qg, kg = q_gain.reshape(HEAD_DIM, 1), k_gain.reshape(HEAD_DIM, 1)
    wpt = tm // win_tile
    half = HEAD_DIM // 2
    return pl.pallas_call(
        functools.partial(_swa_proj_kernel, win_tile=win_tile),
        grid=(b, s // tm),
        in_specs=[pl.BlockSpec((1, tm, d), lambda i, j: (i, j, 0)),
                  pl.BlockSpec((1, half, tm), lambda i, j: (i, 0, j)),
                  pl.BlockSpec((1, half, tm), lambda i, j: (i, 0, j)),
                  _full(wq.shape), _full(wk.shape), _full(wv.shape), _full(wz.shape),
                  _full(qg.shape), _full(kg.shape)],
        out_specs=[pl.BlockSpec((1, MIX_WIDTH, tm), lambda i, j: (i, 0, j)),
                   pl.BlockSpec((1, g, tm, KEY_PAD), lambda i, j: (i, 0, j, 0)),
                   pl.BlockSpec((1, wpt, kvd, win_tile), lambda i, j: (i, j, 0, 0)),
                   pl.BlockSpec((1, tm, MIX_WIDTH), lambda i, j: (i, j, 0))],
        out_shape=[jax.ShapeDtypeStruct((b, MIX_WIDTH, s), BF),
                   jax.ShapeDtypeStruct((b, g, s, KEY_PAD), BF),
                   jax.ShapeDtypeStruct((b, s // win_tile, kvd, win_tile), BF),
                   jax.ShapeDtypeStruct((b, s, MIX_WIDTH), BF)],
        scratch_shapes=[pltpu.VMEM((MIX_WIDTH, tm), F32)],
        compiler_params=_params(("parallel", "parallel")),
        name="swa_proj",
    )(h, cos, sin, wq, wk, wv, wz, qg, kg)


def _split3(x):
    hi = x.astype(BF)
    r1 = x - hi.astype(F32)
    mid = r1.astype(BF)
    lo = (r1 - mid.astype(F32)).astype(BF)
    return hi, mid, lo


def _forget_cum_kernel(h_ref, wf_ref, bias_ref, tri_ref, cum_out, carry_ref):
    @pl.when(pl.program_id(1) == 0)
    def _():
        carry_ref[...] = jnp.zeros_like(carry_ref)

    x = _dot_nt(wf_ref[...], h_ref[0]) + bias_ref[...]
    logf = jnp.minimum(x, 0.0) - jnp.log(1.0 + jnp.exp(-jnp.abs(x)))
    tri = tri_ref[...]
    hi, mid, lo = _split3(logf)
    cum = (_dot(hi, tri) + _dot(mid, tri)) + _dot(lo, tri) + carry_ref[:, 0:1]
    cum_out[0] = cum
    carry_ref[...] = jnp.broadcast_to(cum[:, -1:], carry_ref.shape)


def _forget_cum(h, wf, bias):
    b, s, d = h.shape
    tm = min(TOK_TILE, s)
    tri = jnp.asarray(np.arange(tm)[:, None] <= np.arange(tm)[None, :], BF)
    return pl.pallas_call(
        _forget_cum_kernel,
        grid=(b, s // tm),
        in_specs=[pl.BlockSpec((1, tm, d), lambda i, j: (i, j, 0)),
                  _full(wf.shape), _full((N_HEADS, 1)), _full(tri.shape)],
        out_specs=pl.BlockSpec((1, N_HEADS, tm), lambda i, j: (i, 0, j)),
        out_shape=jax.ShapeDtypeStruct((b, N_HEADS, s), F32),
        scratch_shapes=[pltpu.VMEM((N_HEADS, LANES), F32)],
        compiler_params=_params(("parallel", "arbitrary")),
        name="fox_forget_cum",
    )(h, wf, bias.reshape(N_HEADS, 1).astype(F32), tri)


def _fox_proj_kernel(h_ref, cum_ref, wq_ref, wk_ref, wv_ref, wz_ref, qg_ref, kg_ref,
                     q_out, k_out, v_out, z_out, yq_ref, yk_ref, *, tile):
    h = h_ref[0]
    tm = h.shape[0]
    yq_ref[...] = _dot_nt(wq_ref[...], h)
    yk_ref[...] = _dot_nt(wk_ref[...], h)
    row = lax.broadcasted_iota(jnp.int32, (8, tm), 0)
    zeros = jnp.zeros((KEY_PAD - HEAD_DIM - 16, tm), F32)
    for hd in range(N_HEADS):
        rows = slice(hd * HEAD_DIM, (hd + 1) * HEAD_DIM)
        c_hi, c_mid, c_lo = (c.astype(F32) for c in _split3(cum_ref[0, hd:hd + 1, :] * LOG2E))
        c3 = jnp.where(row == 0, c_hi, jnp.where(row == 1, c_mid, jnp.where(row == 2, c_lo, 0.0)))
        one3 = jnp.where(row < 3, 1.0, 0.0)
        q = _head_rms_t(yq_ref[rows, :], qg_ref[...]) * Q_SCALE
        q_out[0, hd] = jnp.concatenate([q, one3, c3, zeros], axis=0).astype(BF)
        k = _head_rms_t(yk_ref[rows, :], kg_ref[...])
        k_out[0, hd] = jnp.concatenate([k, -c3, one3, zeros], axis=0).T.astype(BF)
    _store_lane_tiles(v_out, _dot_nt(wv_ref[...], h).astype(BF), tile)
    z_out[0] = _silu(_dot(h, wz_ref[...])).astype(BF)


def _fox_proj(h, cum, w_in, q_gain, k_gain, *, tile):
    b, s, d = h.shape
    tm = min(TOK_TILE, s)
    wt = w_in.T.astype(BF)
    wq, wk, wv = wt[:MIX_WIDTH], wt[MIX_WIDTH:2 * MIX_WIDTH], wt[2 * MIX_WIDTH:3 * MIX_WIDTH]
    wz = w_in[:, 3 * MIX_WIDTH + N_HEADS:].astype(BF)
    qg, kg = q_gain.reshape(HEAD_DIM, 1), k_gain.reshape(HEAD_DIM, 1)
    return pl.pallas_call(
        functools.partial(_fox_proj_kernel, tile=tile),
        grid=(b, s // tm),
        in_specs=[pl.BlockSpec((1, tm, d), lambda i, j: (i, j, 0)),
                  pl.BlockSpec((1, N_HEADS, tm), lambda i, j: (i, 0, j)),
                  _full(wq.shape), _full(wk.shape), _full(wv.shape), _full(wz.shape),
                  _full(qg.shape), _full(kg.shape)],
        out_specs=[pl.BlockSpec((1, N_HEADS, KEY_PAD, tm), lambda i, j: (i, 0, 0, j)),
                   pl.BlockSpec((1, N_HEADS, tm, KEY_PAD), lambda i, j: (i, 0, j, 0)),
                   pl.BlockSpec((1, tm // tile, MIX_WIDTH, tile), lambda i, j: (i, j, 0, 0)),
                   pl.BlockSpec((1, tm, MIX_WIDTH), lambda i, j: (i, j, 0))],
        out_shape=[jax.ShapeDtypeStruct((b, N_HEADS, KEY_PAD, s), BF),
                   jax.ShapeDtypeStruct((b, N_HEADS, s, KEY_PAD), BF),
                   jax.ShapeDtypeStruct((b, s // tile, MIX_WIDTH, tile), BF),
                   jax.ShapeDtypeStruct((b, s, MIX_WIDTH), BF)],
        scratch_shapes=[pltpu.VMEM((MIX_WIDTH, tm), F32), pltpu.VMEM((MIX_WIDTH, tm), F32)],
        compiler_params=_params(("parallel", "parallel")),
        name="fox_proj",
    )(h, cum, wq, wk, wv, wz, qg, kg)


def _out_proj_kernel(*refs, n_o, has_next):
    o_refs = refs[:n_o]
    z_ref, x_ref, w_ref = refs[n_o:n_o + 3]
    rest = refs[n_o + 3:]
    if has_next:
        g_ref, x_out, h_out = rest
    else:
        (x_out,) = rest
    o = o_refs[0][0].astype(F32)
    for r in o_refs[1:]:
        o = o + r[0].astype(F32)
    y = _dot((o * z_ref[0].astype(F32)).astype(BF), w_ref[...])
    x_new = x_ref[0] + y
    x_out[0] = x_new
    if has_next:
        h_out[0] = _rms_rows(x_new, g_ref[...]).astype(BF)


def _out_proj(o_list, zs, x, w_out, next_gain):
    b, s, d = x.shape
    tm = min(TOK_TILE, s)
    has_next = next_gain is not None
    blk = pl.BlockSpec((1, tm, d), lambda i, j: (i, j, 0))
    args = list(o_list) + [zs, x, w_out.astype(BF)]
    in_specs = [blk] * (len(o_list) + 2) + [_full(w_out.shape)]
    out_shape = [jax.ShapeDtypeStruct((b, s, d), F32)]
    out_specs = [blk]
    if has_next:
        args.append(next_gain.reshape(1, d))
        in_specs.append(_full((1, d)))
        out_shape.append(jax.ShapeDtypeStruct((b, s, d), BF))
        out_specs.append(blk)
    res = pl.pallas_call(
        functools.partial(_out_proj_kernel, n_o=len(o_list), has_next=has_next),
        grid=(b, s // tm), in_specs=in_specs, out_specs=out_specs, out_shape=out_shape,
        compiler_params=_params(("parallel", "parallel")),
        name="out_proj",
    )(*args)
    return (res[0], res[1]) if has_next else (res[0], None)


ITEM_LANES = 512
CMP_Q_TILE = 256
NSA_Q_TILE = 256
NSA_SLC_TILE = 512
NSA_WIN_TILE = 256
SWA_TILE = 128
FOX_Q_TILE = 512
FOX_K_TILE = 512


def _nsa_mixer(h, cos, sin, w_in, q_gain, k_gain, cmp_pos, cmp_w1, cmp_w2):
    b, s, _ = h.shape
    g = NSA_KV_HEADS
    r = N_HEADS // g
    qT, kc_tok, vc_tok, ks, kw, vsT, vwT, gates, zs = _nsa_proj(
        h, cos, sin, w_in, q_gain, k_gain, slc_tile=NSA_SLC_TILE, win_tile=NSA_WIN_TILE)
    k_cmp, v_cmpT = _compress(kc_tok, vc_tok, cmp_pos, cmp_w1, cmp_w2, k_gain[0])
    gates5 = gates.reshape(b, 3, g, r, s)
    o_cmp, sel_bias = _cmp_select(qT, k_cmp, v_cmpT, gates5, tq=CMP_Q_TILE)
    o_slc = _flash(qT, ks, vsT, mode="causal", n_par=1, n_rep=r, dq=HEAD_DIM, kc=SLC_KEY_LANES,
                   tq=NSA_Q_TILE, tk=NSA_SLC_TILE, cw=ITEM_LANES, sel_bias=sel_bias,
                   gates=gates5, gate_branch=1, name="nsa_selected")
    o_win = _flash(qT, kw, vwT, mode="window", n_par=1, n_rep=r, dq=HEAD_DIM, kc=HEAD_DIM,
                   tq=NSA_Q_TILE, tk=NSA_WIN_TILE, cw=ITEM_LANES, window=NSA_WINDOW,
                   gates=gates5, gate_branch=2, name="nsa_window")
    return [o_cmp, o_slc, o_win], zs


def _swa_mixer(h, cos, sin, w_in, q_gain, k_gain, sinks):
    r = N_HEADS // SWA_KV_HEADS
    qT, k, vT, zs = _swa_proj(h, cos, sin, w_in, q_gain, k_gain, win_tile=SWA_TILE)
    o = _flash(qT, k, vT, mode="window", n_par=1, n_rep=r, dq=HEAD_DIM, kc=HEAD_DIM, tq=SWA_TILE,
               tk=SWA_TILE, cw=ITEM_LANES, window=SWA_WINDOW, sinks=sinks, name="swa_window")
    return [o], zs


def _fox_mixer(h, w_in, forget_bias, q_gain, k_gain):
    b, s, _ = h.shape
    wf = w_in[:, 3 * MIX_WIDTH:3 * MIX_WIDTH + N_HEADS].T.astype(BF)
    cum = _forget_cum(h, wf, forget_bias)
    qT, k, vT, zs = _fox_proj(h, cum, w_in, q_gain, k_gain, tile=FOX_K_TILE)
    o = _flash(qT.reshape(b, N_HEADS * KEY_PAD, s), k, vT, mode="causal", n_par=2, n_rep=1, dq=KEY_PAD,
               kc=KEY_PAD, tq=FOX_Q_TILE, tk=FOX_K_TILE, cw=ITEM_LANES, name="fox_attention")
    return [o], zs


def kernel(x, positions, norm_gains, a_w_in, a_q_gain, a_k_gain, a_cmp_pos, a_cmp_w1, a_cmp_w2, a_w_out,
           b_w_in, b_q_gain, b_k_gain, b_sinks, b_w_out,
           c_w_in, c_forget_bias, c_q_gain, c_k_gain, c_w_out):
    depth = norm_gains.shape[0]
    cos, sin = _rope_tables(positions)
    h = _prenorm(x, norm_gains[0])
    for i in range(depth):
        j, mixer = divmod(i, 3)
        if mixer == 0:
            o_list, zs = _nsa_mixer(h, cos, sin, a_w_in[j], a_q_gain[j], a_k_gain[j],
                                    a_cmp_pos[j], a_cmp_w1[j], a_cmp_w2[j])
            w_out = a_w_out[j]
        elif mixer == 1:
            o_list, zs = _swa_mixer(h, cos, sin, b_w_in[j], b_q_gain[j], b_k_gain[j], b_sinks[j])
            w_out = b_w_out[j]
        else:
            o_list, zs = _fox_mixer(h, c_w_in[j], c_forget_bias[j], c_q_gain[j], c_k_gain[j])
            w_out = c_w_out[j]
        next_gain = norm_gains[i + 1] if i + 1 < depth else None
        x, h = _out_proj(o_list, zs, x, w_out, next_gain)
    return x
```

```python
import functools

import jax
import jax.numpy as jnp
import numpy as np
from jax import lax
from jax.experimental import pallas as pl
from jax.experimental.pallas import tpu as pltpu

D_MODEL = 1024
HEAD_DIM = 64
N_HEADS = 16
MIX_WIDTH = N_HEADS * HEAD_DIM
ROPE_THETA = 10000.0
EPS = 1e-6
SCALE = HEAD_DIM ** -0.5
NEG_INF = -1e30
BIG = 1e30
M_INIT = -1e29
SEL_OFF = -(2.0 ** 100)

NSA_KV_HEADS = 4
NSA_CMP_LEN = 32
NSA_CMP_STRIDE = 16
NSA_SLC_LEN = 64
NSA_TOPK = 16
NSA_WINDOW = 512
SWA_KV_HEADS = 2
SWA_WINDOW = 128

LOG2E = float(np.log2(np.e))
Q_SCALE = SCALE * LOG2E

LANES = 128
KEY_PAD = 128
SLC_KEY_LANES = 256
V_ROWS = 80
FLASH_UNROLL = 4
CMP_BLOCK = 128
CMP_MASK_ROWS = 2 * CMP_BLOCK
VMEM_LIMIT = 56 * 1024 * 1024

TOK_TILE = 512
NT_DIMS = (((1,), (1,)), ((), ()))

BF = jnp.bfloat16
F32 = jnp.float32


def _params(sem):
    return pltpu.CompilerParams(dimension_semantics=sem, vmem_limit_bytes=VMEM_LIMIT)


def _dot(a, b):
    return jnp.dot(a, b, preferred_element_type=F32)


def _dot_nt(a, b):
    return lax.dot_general(a, b, NT_DIMS, preferred_element_type=F32)


def _rope_tab_kernel(pos_ref, invf_ref, cos_ref, sin_ref):
    ang = invf_ref[...] * pos_ref[0].astype(F32)
    cos_ref[0] = jnp.cos(ang)
    sin_ref[0] = jnp.sin(ang)


def _rope_tables(positions):
    b, s = positions.shape
    half = HEAD_DIM // 2
    inv_freq = ROPE_THETA ** (-jnp.arange(half, dtype=F32) * 2.0 / HEAD_DIM)
    tm = min(TOK_TILE, s)
    out = jax.ShapeDtypeStruct((b, half, s), F32)
    return pl.pallas_call(
        _rope_tab_kernel,
        grid=(b, s // tm),
        in_specs=[pl.BlockSpec((1, 1, tm), lambda i, j: (i, 0, j)),
                  pl.BlockSpec((half, 1), lambda i, j: (0, 0))],
        out_specs=[pl.BlockSpec((1, half, tm), lambda i, j: (i, 0, j))] * 2,
        out_shape=[out, out],
        compiler_params=_params(("parallel", "parallel")),
        name="rope_tables",
    )(positions.reshape(b, 1, s), inv_freq.reshape(half, 1))


def _rms_rows(x, gain_row):
    y = x * lax.rsqrt(jnp.mean(x * x, axis=-1, keepdims=True) + EPS)
    return y * gain_row


def _prenorm_kernel(x_ref, g_ref, h_ref):
    h_ref[0] = _rms_rows(x_ref[0], g_ref[...]).astype(BF)


def _prenorm(x, gain):
    b, s, d = x.shape
    tm = min(TOK_TILE, s)
    return pl.pallas_call(
        _prenorm_kernel,
        grid=(b, s // tm),
        in_specs=[pl.BlockSpec((1, tm, d), lambda i, j: (i, j, 0)),
                  pl.BlockSpec((1, d), lambda i, j: (0, 0))],
        out_specs=pl.BlockSpec((1, tm, d), lambda i, j: (i, j, 0)),
        out_shape=jax.ShapeDtypeStruct((b, s, d), BF),
        compiler_params=_params(("parallel", "parallel")),
        name="prenorm",
    )(x, gain.reshape(1, d))


def _head_rms_t(y, gain_col):
    ms = jnp.mean(y * y, axis=0, keepdims=True)
    return (y * lax.rsqrt(ms + EPS)) * gain_col


def _rope_t(y, cos, sin):
    half = HEAD_DIM // 2
    x1, x2 = y[:half], y[half:]
    return jnp.concatenate([x1 * cos - x2 * sin, x2 * cos + x1 * sin], axis=0)


def _to_token_major(y):
    pad = jnp.zeros((KEY_PAD - y.shape[0], y.shape[1]), y.dtype)
    return jnp.concatenate([y, pad], axis=0).T


def _silu(z):
    return z * (1.0 / (1.0 + jnp.exp(-z)))


def _sigmoid(z):
    return 1.0 / (1.0 + jnp.exp(-z))


def _store_lane_tiles(out_ref, y, tile):
    for c in range(y.shape[1] // tile):
        out_ref[0, c] = y[:, c * tile:(c + 1) * tile]


def _nsa_proj_kernel(h_ref, cos_ref, sin_ref, wq_ref, wk_ref, wv_ref, wg_ref, wz_ref,
                     qg_ref, kg_ref,
                     q_out, kc_out, vc_out, ks_out, kw_out, vs_out, vw_out, g_out, z_out,
                     y_ref, *, slc_tile, win_tile):
    h = h_ref[0]
    cos, sin = cos_ref[0], sin_ref[0]
    g = NSA_KV_HEADS
    kvd = g * HEAD_DIM
    y_ref[...] = _dot_nt(wq_ref[...], h)
    for hd in range(N_HEADS):
        rows = slice(hd * HEAD_DIM, (hd + 1) * HEAD_DIM)
        y = _rope_t(_head_rms_t(y_ref[rows, :], qg_ref[...]), cos, sin) * Q_SCALE
        q_out[0, rows, :] = y.astype(BF)
    y_ref[0:3 * kvd, :] = _dot_nt(wk_ref[...], h)
    tm = h.shape[0]
    tok = pl.program_id(1) * tm + lax.broadcasted_iota(jnp.int32, (tm, 1), 0)
    blk_lane = HEAD_DIM + lax.shift_right_logical(tok, int(np.log2(NSA_SLC_LEN)))
    blk_hot = lax.broadcasted_iota(jnp.int32, (1, SLC_KEY_LANES), 1) == blk_lane
    for kind in range(3):
        for gi in range(g):
            r0 = (kind * g + gi) * HEAD_DIM
            y = y_ref[r0:r0 + HEAD_DIM, :]
            if kind > 0:
                y = _head_rms_t(y, kg_ref[:, kind:kind + 1])
            yt = _to_token_major(_rope_t(y, cos, sin))
            if kind == 0:
                kc_out[0, gi] = yt[:, :HEAD_DIM]
            elif kind == 1:
                wide = jnp.concatenate([yt, jnp.zeros((tm, SLC_KEY_LANES - KEY_PAD), F32)], axis=1)
                ks_out[0, gi] = jnp.where(blk_hot, 1.0, wide).astype(BF)
            else:
                kw_out[0, gi] = yt.astype(BF)
    y_ref[0:3 * kvd, :] = _dot_nt(wv_ref[...], h)
    for gi in range(g):
        r0 = gi * HEAD_DIM
        vc_out[0, gi] = _to_token_major(y_ref[r0:r0 + HEAD_DIM, :])[:, :HEAD_DIM]
    _store_lane_tiles(vs_out, y_ref[kvd:2 * kvd, :].astype(BF), slc_tile)
    _store_lane_tiles(vw_out, y_ref[2 * kvd:3 * kvd, :].astype(BF), win_tile)
    g_out[0] = _sigmoid(_dot_nt(wg_ref[...], h))
    z_out[0] = _silu(_dot(h, wz_ref[...])).astype(BF)


def _full(shape):
    nd = len(shape)
    return pl.BlockSpec(shape, lambda i, j, _n=nd: (0,) * _n)


def _nsa_proj(h, cos, sin, w_in, q_gain, k_gain, *, slc_tile, win_tile):
    b, s, d = h.shape
    g = NSA_KV_HEADS
    kvd = g * HEAD_DIM
    tm = min(TOK_TILE, s)
    sizes = [MIX_WIDTH] + [kvd] * 6 + [3 * N_HEADS]
    off = np.cumsum([0] + sizes)
    wt = w_in.T.astype(BF)
    wq = wt[off[0]:off[1]]
    wk = jnp.concatenate([wt[off[1]:off[2]], wt[off[3]:off[4]], wt[off[5]:off[6]]], axis=0)
    wv = jnp.concatenate([wt[off[2]:off[3]], wt[off[4]:off[5]], wt[off[6]:off[7]]], axis=0)
    wg = wt[off[7]:off[8]]
    wz = w_in[:, off[8]:].astype(BF)
    qg = q_gain.reshape(HEAD_DIM, 1)
    kg = k_gain.T
    n_t = s // tm
    out_shape = [
        jax.ShapeDtypeStruct((b, MIX_WIDTH, s), BF),
        jax.ShapeDtypeStruct((b, g, s, HEAD_DIM), F32),
        jax.ShapeDtypeStruct((b, g, s, HEAD_DIM), F32),
        jax.ShapeDtypeStruct((b, g, s, SLC_KEY_LANES), BF),
        jax.ShapeDtypeStruct((b, g, s, KEY_PAD), BF),
        jax.ShapeDtypeStruct((b, s // slc_tile, kvd, slc_tile), BF),
        jax.ShapeDtypeStruct((b, s // win_tile, kvd, win_tile), BF),
        jax.ShapeDtypeStruct((b, 3 * N_HEADS, s), F32),
        jax.ShapeDtypeStruct((b, s, MIX_WIDTH), BF),
    ]
    out_specs = [
        pl.BlockSpec((1, MIX_WIDTH, tm), lambda i, j: (i, 0, j)),
        pl.BlockSpec((1, g, tm, HEAD_DIM), lambda i, j: (i, 0, j, 0)),
        pl.BlockSpec((1, g, tm, HEAD_DIM), lambda i, j: (i, 0, j, 0)),
        pl.BlockSpec((1, g, tm, SLC_KEY_LANES), lambda i, j: (i, 0, j, 0)),
        pl.BlockSpec((1, g, tm, KEY_PAD), lambda i, j: (i, 0, j, 0)),
        pl.BlockSpec((1, tm // slc_tile, kvd, slc_tile), lambda i, j: (i, j, 0, 0)),
        pl.BlockSpec((1, tm // win_tile, kvd, win_tile), lambda i, j: (i, j, 0, 0)),
        pl.BlockSpec((1, 3 * N_HEADS, tm), lambda i, j: (i, 0, j)),
        pl.BlockSpec((1, tm, MIX_WIDTH), lambda i, j: (i, j, 0)),
    ]
    in_specs = [
        pl.BlockSpec((1, tm, d), lambda i, j: (i, j, 0)),
        pl.BlockSpec((1, HEAD_DIM // 2, tm), lambda i, j: (i, 0, j)),
        pl.BlockSpec((1, HEAD_DIM // 2, tm), lambda i, j: (i, 0, j)),
        _full(wq.shape), _full(wk.shape), _full(wv.shape), _full(wg.shape), _full(wz.shape),
        _full(qg.shape), _full(kg.shape),
    ]
    return pl.pallas_call(
        functools.partial(_nsa_proj_kernel, slc_tile=slc_tile, win_tile=win_tile),
        grid=(b, n_t), in_specs=in_specs, out_specs=out_specs, out_shape=out_shape,
        scratch_shapes=[pltpu.VMEM((MIX_WIDTH, tm), F32)],
        compiler_params=_params(("parallel", "parallel")),
        name="nsa_proj",
    )(h, cos, sin, wq, wk, wv, wg, wz, qg, kg)


def _gelu_tanh(x):
    c = np.float32(np.sqrt(2.0 / np.pi))
    return 0.5 * x * (1.0 + jnp.tanh(c * (x + 0.044715 * (x * x * x))))


def _compress_kernel(kc_ref, vc_ref, pos_ref, w1_ref, w2_ref, kg_ref, kcmp_out, vcmp_out):
    for which, (src, dst) in enumerate(((kc_ref, kcmp_out), (vc_ref, vcmp_out))):
        x = src[0, 0]
        n = x.shape[0]
        half = x.shape[1]
        xa = (x + pos_ref[which, 0:1, :]).astype(BF)
        xb = (x + pos_ref[which, 1:2, :]).astype(BF)
        ua = _dot(xa, w1_ref[which, :half, :])
        ub = _dot(xb, w1_ref[which, half:, :])
        row = lax.broadcasted_iota(jnp.int32, (n, 1), 0)
        ub_next = jnp.where(row == n - 1, 0.0, pltpu.roll(ub, n - 1, 0))
        hid = _gelu_tanh(ua + ub_next)
        y = _dot(hid.astype(BF), w2_ref[which])
        if which == 0:
            y = _rms_rows(y, kg_ref[...])
            dst[0, 0] = y.astype(BF)
        else:
            pad = jnp.zeros((n, KEY_PAD - HEAD_DIM), F32)
            dst[0, 0] = jnp.concatenate([y, pad], axis=1).T[:HEAD_DIM].astype(BF)


def _compress(kc_tok, vc_tok, cmp_pos, cmp_w1, cmp_w2, k_gain0):
    b, g, s, _ = kc_tok.shape
    n_chunk = s // NSA_CMP_STRIDE
    flat = NSA_CMP_STRIDE * HEAD_DIM
    kc = kc_tok.reshape(b, g, n_chunk, flat)
    vc = vc_tok.reshape(b, g, n_chunk, flat)
    pos = cmp_pos.reshape(2, 2, flat)
    w1 = cmp_w1.astype(BF)
    w2 = cmp_w2.astype(BF)
    blk = pl.BlockSpec((1, 1, n_chunk, flat), lambda i, j: (i, j, 0, 0))
    return pl.pallas_call(
        _compress_kernel,
        grid=(b, g),
        in_specs=[blk, blk, _full(pos.shape), _full(w1.shape), _full(w2.shape),
                  _full((1, HEAD_DIM))],
        out_specs=[pl.BlockSpec((1, 1, n_chunk, HEAD_DIM), lambda i, j: (i, j, 0, 0)),
                   pl.BlockSpec((1, 1, HEAD_DIM, n_chunk), lambda i, j: (i, j, 0, 0))],
        out_shape=[jax.ShapeDtypeStruct((b, g, n_chunk, HEAD_DIM), BF),
                   jax.ShapeDtypeStruct((b, g, HEAD_DIM, n_chunk), BF)],
        compiler_params=_params(("parallel", "parallel")),
        name="nsa_compress",
    )(kc, vc, pos, w1, w2, k_gain0.reshape(1, HEAD_DIM))


def _stack_heads(q_ref, n_heads, dk):
    return jnp.concatenate([q_ref[0, r * dk:(r + 1) * dk, :] for r in range(n_heads)], axis=1)


def _unstack_store(o, out_ref, n_heads, tq):
    rows = jnp.concatenate([o[:, r * tq:(r + 1) * tq] for r in range(n_heads)], axis=0)
    out_ref[0] = rows.T.astype(out_ref.dtype)


def _gate_row(gate_ref, n_heads):
    return jnp.concatenate([gate_ref[0, 0, 0, r:r + 1, :] for r in range(n_heads)], axis=1)


def _cmp_branch(rows, q_ref, kc_ref, vc_ref, ov_ref, gate_ref, o_out, sel_out, *, tq, n_blk):
    r_heads = N_HEADS // NSA_KV_HEADS
    nq = r_heads * tq
    q0 = pl.program_id(2) * tq
    n_cmp = kc_ref.shape[2] - 1
    q4 = _stack_heads(q_ref, r_heads, HEAD_DIM)
    s = _dot(kc_ref[0, 0, 0:rows, :], q4)
    t_row = q0 + (lax.broadcasted_iota(jnp.int32, (1, nq), 1) & (tq - 1))
    lo = max(rows - CMP_MASK_ROWS, 0)
    c_col = lo + lax.broadcasted_iota(jnp.int32, (rows - lo, 1), 0)
    valid = (c_col * NSA_CMP_STRIDE + (NSA_CMP_LEN - 1) <= t_row) & (c_col < n_cmp)
    s_new = jnp.where(valid, s[lo:], NEG_INF)
    m = jnp.max(s_new, axis=0, keepdims=True)
    if lo:
        m = jnp.maximum(m, jnp.max(s[:lo], axis=0, keepdims=True))
    e = jnp.where(valid, jnp.exp2(s_new - m), 0.0).astype(BF)
    if lo:
        e = jnp.concatenate([jnp.exp2(s[:lo] - m).astype(BF), e], axis=0)
    one_row = jnp.where(lax.broadcasted_iota(jnp.int32, (V_ROWS - HEAD_DIM, rows), 0) == 0, 1.0, 0.0).astype(BF)
    n_live = min(n_blk, rows * NSA_CMP_STRIDE // NSA_SLC_LEN + 8)
    lhs = jnp.concatenate([vc_ref[0, 0, :, 0:rows], one_row, ov_ref[0:n_live, 0:rows]], axis=0)
    res = _dot(lhs, e)
    l = res[HEAD_DIM:HEAD_DIM + 1]
    inv = jnp.where(l > 0.0, 1.0 / jnp.where(l > 0.0, l, 1.0), 0.0)
    o = res[:HEAD_DIM] * (inv * _gate_row(gate_ref, r_heads))
    _unstack_store(o, o_out, r_heads, tq)
    w = res[V_ROWS:V_ROWS + n_live] * inv
    imp = w[:, 0:tq]
    for r in range(1, r_heads):
        imp = imp + w[:, r * tq:(r + 1) * tq]
    t1 = q0 + lax.broadcasted_iota(jnp.int32, (1, tq), 1)
    cur = lax.shift_right_logical(t1, int(np.log2(NSA_SLC_LEN)))
    blk = lax.broadcasted_iota(jnp.int32, (n_live, tq), 0)
    forced = (blk == 0) | (blk == cur) | (blk == cur - 1)
    imp = jnp.where(forced, BIG, jnp.where(blk > cur, NEG_INF, imp))
    for _ in range(min(NSA_TOPK, n_blk)):
        best = jnp.max(imp, axis=0, keepdims=True)
        first = jnp.min(jnp.where(imp == best, blk, n_blk), axis=0, keepdims=True)
        imp = jnp.where(blk == first, -jnp.inf, imp)
    sel_out[0, 0, 0:n_live, :] = jnp.where(imp == -jnp.inf, 0.0, SEL_OFF).astype(BF)
    if n_live < n_blk:
        sel_out[0, 0, n_live:n_blk, :] = jnp.full((n_blk - n_live, tq), SEL_OFF, BF)


def _cmp_select_kernel(q_ref, kc_ref, vc_ref, ov_ref, gate_ref, o_out, sel_out, *, tq, n_blk):
    q0 = pl.program_id(2) * tq
    n_chunk = kc_ref.shape[2]
    n_need = jnp.minimum((q0 + tq - NSA_CMP_LEN) // NSA_CMP_STRIDE + 1, n_chunk - 1)
    n_steps = n_chunk // CMP_BLOCK
    need_steps = (n_need + CMP_BLOCK - 1) // CMP_BLOCK
    for k in range(1, n_steps + 1):
        @pl.when(need_steps == k)
        def _(k=k):
            _cmp_branch(k * CMP_BLOCK, q_ref, kc_ref, vc_ref, ov_ref, gate_ref, o_out, sel_out, tq=tq, n_blk=n_blk)


def _overlap_matrix(s):
    n_chunk = s // NSA_CMP_STRIDE
    n_blk = s // NSA_SLC_LEN
    c0 = np.arange(n_chunk) * NSA_CMP_STRIDE
    c1 = c0 + NSA_CMP_LEN - 1
    b0 = np.arange(n_blk) * NSA_SLC_LEN
    ov = np.minimum(c1[None, :], b0[:, None] + NSA_SLC_LEN - 1) - np.maximum(c0[None, :], b0[:, None]) + 1
    return jnp.asarray(np.clip(ov, 0, None) / NSA_CMP_LEN, BF)


def _cmp_select(qT, k_cmp, v_cmpT, gates5, *, tq):
    b, _, s = qT.shape
    g = NSA_KV_HEADS
    r_heads = N_HEADS // g
    n_chunk = k_cmp.shape[2]
    n_blk = s // NSA_SLC_LEN
    ov = _overlap_matrix(s)
    rows = r_heads * HEAD_DIM
    return pl.pallas_call(
        functools.partial(_cmp_select_kernel, tq=tq, n_blk=n_blk),
        grid=(b, g, s // tq),
        in_specs=[
            pl.BlockSpec((1, rows, tq), lambda i, j, k: (i, j, k)),
            pl.BlockSpec((1, 1, n_chunk, HEAD_DIM), lambda i, j, k: (i, j, 0, 0)),
            pl.BlockSpec((1, 1, HEAD_DIM, n_chunk), lambda i, j, k: (i, j, 0, 0)),
            pl.BlockSpec((n_blk, n_chunk), lambda i, j, k: (0, 0)),
            pl.BlockSpec((1, 1, 1, r_heads, tq), lambda i, j, k: (i, 0, j, 0, k)),
        ],
        out_specs=[pl.BlockSpec((1, tq, rows), lambda i, j, k: (i, k, j)),
                   pl.BlockSpec((1, 1, n_blk, tq), lambda i, j, k: (i, j, 0, k))],
        out_shape=[jax.ShapeDtypeStruct((b, s, MIX_WIDTH), BF),
                   jax.ShapeDtypeStruct((b, g, n_blk, s), BF)],
        compiler_params=_params(("parallel", "parallel", "parallel")),
        name="nsa_cmp_select",
    )(qT, k_cmp, v_cmpT, ov, gates5)


def _flash_kernel(*refs, mode, n_par, n_rep, dq, kc, tq, tk, cw, window, n_blk, has_sink, has_gate):
    it = iter(refs)
    q_ref, k_ref, v_ref = next(it), next(it), next(it)
    sel_ref = next(it) if n_blk else None
    sink_ref = next(it) if has_sink else None
    gate_ref = next(it) if has_gate else None
    out_ref = next(it)
    qs_ref, m_ref, acc_ref, mt_ref, s0_ref, s1_ref = (next(it) for _ in range(6))
    s_bufs = (s0_ref, s1_ref)

    nq = n_rep * tq
    items = [(p, c) for p in range(n_par) for c in range(nq // cw)]
    n_items = len(items)
    grp = pl.program_id(1)
    q0 = pl.program_id(2) * tq
    acc_row = lax.broadcasted_iota(jnp.int32, (V_ROWS, nq), 0)
    for p in range(n_par):
        for r in range(n_rep):
            hd = p * n_rep + r
            qs_ref[p, 0:dq, r * tq:(r + 1) * tq] = q_ref[0, hd * dq:(hd + 1) * dq, :]
        if n_blk:
            qs_ref[p, dq:dq + n_blk, :] = jnp.concatenate([sel_ref[0, 0]] * n_rep, axis=1)
            if dq + n_blk < kc:
                qs_ref[p, dq + n_blk:kc, :] = jnp.zeros((kc - dq - n_blk, nq), BF)
        if has_sink:
            m_ref[p] = jnp.concatenate(
                [jnp.full((1, tq), sink_ref[(grp * n_par + p) * n_rep + r] * LOG2E, F32) for r in range(n_rep)],
                axis=1)
            acc_ref[p] = jnp.where(acc_row == HEAD_DIM, 1.0, 0.0)
        else:
            m_ref[p] = jnp.full((1, nq), M_INIT, F32)
            acc_ref[p] = jnp.zeros((V_ROWS, nq), F32)

    t_row = q0 + (lax.broadcasted_iota(jnp.int32, (1, nq), 1) & (tq - 1))
    one_row = jnp.where(lax.broadcasted_iota(jnp.int32, (V_ROWS - HEAD_DIM, tk), 0) == 0, 1.0, 0.0).astype(BF)

    def stage_a(item, j, key0, slot, kind):
        p, c = item
        cols = slice(c * cw, (c + 1) * cw)
        s = _dot(k_ref[0, p, j][:, :kc], qs_ref[p, :, cols])
        if kind is not None:
            key = key0 + lax.broadcasted_iota(jnp.int32, (tk, 1), 0)
            ok = key <= t_row[:, cols] if kind == "causal" else key > t_row[:, cols] - window
            s = jnp.where(ok, s, NEG_INF)
        s_bufs[slot][...] = s
        mt_ref[slot] = jnp.max(s, axis=0, keepdims=True)

    def stage_b(item, j, slot):
        p, c = item
        cols = slice(c * cw, (c + 1) * cw)
        m_old = m_ref[p, :, cols]
        m_new = jnp.maximum(m_old, mt_ref[slot])
        pr = jnp.exp2(s_bufs[slot][...] - m_new).astype(BF)
        alpha = jnp.exp2(m_old - m_new)
        v = jnp.concatenate([v_ref[0, j, p * HEAD_DIM:(p + 1) * HEAD_DIM, :], one_row], axis=0)
        acc_ref[p, :, cols] = alpha * acc_ref[p, :, cols] + _dot(v, pr)
        m_ref[p, :, cols] = m_new

    if mode == "causal":
        assert n_items % 2 == 0
        n_full = q0 // tk

        def step(j, kind, next_kind, last):
            for idx, item in enumerate(items):
                slot = idx % 2
                if idx + 1 < n_items:
                    stage_a(items[idx + 1], j, j * tk, 1 - slot, kind)
                elif not last:
                    stage_a(items[0], j + 1, (j + 1) * tk, 1 - slot, next_kind)
                stage_b(item, j, slot)

        stage_a(items[0], 0, 0, 0, "causal")

        def body(j, carry):
            step(j, None, None, False)
            return carry

        def body_group(i, carry):
            for u in range(FLASH_UNROLL):
                step(FLASH_UNROLL * i + u, None, None, False)
            return carry

        n_main = jnp.maximum(n_full - 1, 0)
        n_groups = lax.shift_right_logical(n_main, int(np.log2(FLASH_UNROLL)))
        lax.fori_loop(0, n_groups, body_group, 0)
        lax.fori_loop(FLASH_UNROLL * n_groups, n_main, body, 0)

        @pl.when(n_full >= 1)
        def _():
            step(n_full - 1, None, "causal", False)

        step(n_full, "causal", None, True)
    else:
        w_tiles, q_tiles = window // tk, tq // tk
        work = []
        for i in range(w_tiles + q_tiles):
            jv = q0 // tk - w_tiles + i
            key0 = jnp.where(jv < 0, -(1 << 30), jv * tk)
            kind = "causal" if i >= w_tiles else "lower"
            work += [(item, jnp.maximum(jv, 0), key0, kind) for item in items]
        stage_a(work[0][0], work[0][1], work[0][2], 0, work[0][3])
        for n, (item, j, _, _) in enumerate(work):
            if n + 1 < len(work):
                nxt = work[n + 1]
                stage_a(nxt[0], nxt[1], nxt[2], (n + 1) % 2, nxt[3])
            stage_b(item, j, n % 2)

    outs = []
    for p in range(n_par):
        acc = acc_ref[p]
        o = acc[:HEAD_DIM] * (1.0 / acc[HEAD_DIM:HEAD_DIM + 1])
        if has_gate:
            o = o * _gate_row(gate_ref, n_rep)
        outs += [o[:, r * tq:(r + 1) * tq] for r in range(n_rep)]
    out_ref[0] = jnp.concatenate(outs, axis=0).T.astype(out_ref.dtype)


def _flash(qT, k_tok, vT_tiles, *, mode, n_par, n_rep, dq, kc, tq, tk, cw, window=None,
           sel_bias=None, sinks=None, gates=None, gate_branch=0, name="flash"):
    b, _, s = qT.shape
    kh, k_lanes = k_tok.shape[1], k_tok.shape[3]
    n_t = s // tk
    k5 = k_tok.reshape(b, kh, n_t, tk, k_lanes)
    n_grp = kh // n_par
    heads = n_par * n_rep
    nq = n_rep * tq
    n_blk = sel_bias.shape[2] if sel_bias is not None else 0
    has_sink, has_gate = sinks is not None, gates is not None
    args = [qT, k5, vT_tiles]
    in_specs = [
        pl.BlockSpec((1, heads * dq, tq), lambda i, j, k: (i, j, k)),
        pl.BlockSpec((1, n_par, n_t, tk, k_lanes), lambda i, j, k: (i, j, 0, 0, 0)),
        pl.BlockSpec((1, n_t, n_par * HEAD_DIM, tk), lambda i, j, k: (i, 0, j, 0)),
    ]
    if n_blk:
        assert dq + n_blk <= kc
        args.append(sel_bias)
        in_specs.append(pl.BlockSpec((1, 1, n_blk, tq), lambda i, j, k: (i, j, 0, k)))
    if has_sink:
        args.append(sinks.astype(F32))
        in_specs.append(pl.BlockSpec(memory_space=pltpu.SMEM))
    if has_gate:
        args.append(gates)
        in_specs.append(pl.BlockSpec((1, 1, 1, n_rep, tq),
                                     lambda i, j, k, _br=gate_branch: (i, _br, j, 0, k)))
    scratch = [pltpu.VMEM((n_par, kc, nq), BF), pltpu.VMEM((n_par, 1, nq), F32),
               pltpu.VMEM((n_par, V_ROWS, nq), F32), pltpu.VMEM((2, 1, cw), F32),
               pltpu.VMEM((tk, cw), F32), pltpu.VMEM((tk, cw), F32)]
    kern = functools.partial(_flash_kernel, mode=mode, n_par=n_par, n_rep=n_rep, dq=dq, kc=kc, tq=tq, tk=tk,
                             cw=cw, window=window, n_blk=n_blk, has_sink=has_sink, has_gate=has_gate)
    return pl.pallas_call(
        kern,
        grid=(b, n_grp, s // tq),
        in_specs=in_specs,
        out_specs=pl.BlockSpec((1, tq, heads * HEAD_DIM), lambda i, j, k: (i, k, j)),
        out_shape=jax.ShapeDtypeStruct((b, s, MIX_WIDTH), BF),
        scratch_shapes=scratch,
        compiler_params=_params(("parallel", "parallel", "arbitrary")),
        name=name,
    )(*args)


def _swa_proj_kernel(h_ref, cos_ref, sin_ref, wq_ref, wk_ref, wv_ref, wz_ref, qg_ref, kg_ref,
                     q_out, k_out, v_out, z_out, y_ref, *, win_tile):
    h = h_ref[0]
    cos, sin = cos_ref[0], sin_ref[0]
    y_ref[...] = _dot_nt(wq_ref[...], h)
    for hd in range(N_HEADS):
        rows = slice(hd * HEAD_DIM, (hd + 1) * HEAD_DIM)
        q_out[0, rows, :] = (_rope_t(_head_rms_t(y_ref[rows, :], qg_ref[...]), cos, sin) * Q_SCALE).astype(BF)
    kvd = SWA_KV_HEADS * HEAD_DIM
    y_ref[0:kvd, :] = _dot_nt(wk_ref[...], h)
    for gi in range(SWA_KV_HEADS):
        rows = slice(gi * HEAD_DIM, (gi + 1) * HEAD_DIM)
        k_out[0, gi] = _to_token_major(_rope_t(_head_rms_t(y_ref[rows, :], kg_ref[...]), cos, sin)).astype(BF)
    _store_lane_tiles(v_out, _dot_nt(wv_ref[...], h).astype(BF), win_tile)
    z_out[0] = _silu(_dot(h, wz_ref[...])).astype(BF)


def _swa_proj(h, cos, sin, w_in, q_gain, k_gain, *, win_tile):
    b, s, d = h.shape
    g = SWA_KV_HEADS
    kvd = g * HEAD_DIM
    tm = min(TOK_TILE, s)
    wt = w_in.T.astype(BF)
    wq, wk, wv = wt[:MIX_WIDTH], wt[MIX_WIDTH:MIX_WIDTH + kvd], wt[MIX_WIDTH + kvd:MIX_WIDTH + 2 * kvd]
    wz = w_in[:, MIX_WIDTH + 2 * kvd:].astype(BF)
    qg, kg = q_gain.reshape(HEAD_DIM, 1), k_gain.reshape(HEAD_DIM, 1)
    wpt = tm // win_tile
    half = HEAD_DIM // 2
    return pl.pallas_call(
        functools.partial(_swa_proj_kernel, win_tile=win_tile),
        grid=(b, s // tm),
        in_specs=[pl.BlockSpec((1, tm, d), lambda i, j: (i, j, 0)),
                  pl.BlockSpec((1, half, tm), lambda i, j: (i, 0, j)),
                  pl.BlockSpec((1, half, tm), lambda i, j: (i, 0, j)),
                  _full(wq.shape), _full(wk.shape), _full(wv.shape), _full(wz.shape),
                  _full(qg.shape), _full(kg.shape)],
        out_specs=[pl.BlockSpec((1, MIX_WIDTH, tm), lambda i, j: (i, 0, j)),
                   pl.BlockSpec((1, g, tm, KEY_PAD), lambda i, j: (i, 0, j, 0)),
                   pl.BlockSpec((1, wpt, kvd, win_tile), lambda i, j: (i, j, 0, 0)),
                   pl.BlockSpec((1, tm, MIX_WIDTH), lambda i, j: (i, j, 0))],
        out_shape=[jax.ShapeDtypeStruct((b, MIX_WIDTH, s), BF),
                   jax.ShapeDtypeStruct((b, g, s, KEY_PAD), BF),
                   jax.ShapeDtypeStruct((b, s // win_tile, kvd, win_tile), BF),
                   jax.ShapeDtypeStruct((b, s, MIX_WIDTH), BF)],
        scratch_shapes=[pltpu.VMEM((MIX_WIDTH, tm), F32)],
        compiler_params=_params(("parallel", "parallel")),
        name="swa_proj",
    )(h, cos, sin, wq, wk, wv, wz, qg, kg)


def _split3(x):
    hi = x.astype(BF)
    r1 = x - hi.astype(F32)
    mid = r1.astype(BF)
    lo = (r1 - mid.astype(F32)).astype(BF)
    return hi, mid, lo


def _forget_cum_kernel(h_ref, wf_ref, bias_ref, tri_ref, cum_out, carry_ref):
    @pl.when(pl.program_id(1) == 0)
    def _():
        carry_ref[...] = jnp.zeros_like(carry_ref)

    x = _dot_nt(wf_ref[...], h_ref[0]) + bias_ref[...]
    logf = jnp.minimum(x, 0.0) - jnp.log(1.0 + jnp.exp(-jnp.abs(x)))
    tri = tri_ref[...]
    hi, mid, lo = _split3(logf)
    cum = (_dot(hi, tri) + _dot(mid, tri)) + _dot(lo, tri) + carry_ref[:, 0:1]
    cum_out[0] = cum
    carry_ref[...] = jnp.broadcast_to(cum[:, -1:], carry_ref.shape)


def _forget_cum(h, wf, bias):
    b, s, d = h.shape
    tm = min(TOK_TILE, s)
    tri = jnp.asarray(np.arange(tm)[:, None] <= np.arange(tm)[None, :], BF)
    return pl.pallas_call(
        _forget_cum_kernel,
        grid=(b, s // tm),
        in_specs=[pl.BlockSpec((1, tm, d), lambda i, j: (i, j, 0)),
                  _full(wf.shape), _full((N_HEADS, 1)), _full(tri.shape)],
        out_specs=pl.BlockSpec((1, N_HEADS, tm), lambda i, j: (i, 0, j)),
        out_shape=jax.ShapeDtypeStruct((b, N_HEADS, s), F32),
        scratch_shapes=[pltpu.VMEM((N_HEADS, LANES), F32)],
        compiler_params=_params(("parallel", "arbitrary")),
        name="fox_forget_cum",
    )(h, wf, bias.reshape(N_HEADS, 1).astype(F32), tri)


def _fox_proj_kernel(h_ref, cum_ref, wq_ref, wk_ref, wv_ref, wz_ref, qg_ref, kg_ref,
                     q_out, k_out, v_out, z_out, yq_ref, yk_ref, *, tile):
    h = h_ref[0]
    tm = h.shape[0]
    yq_ref[...] = _dot_nt(wq_ref[...], h)
    yk_ref[...] = _dot_nt(wk_ref[...], h)
    row = lax.broadcasted_iota(jnp.int32, (8, tm), 0)
    zeros = jnp.zeros((KEY_PAD - HEAD_DIM - 16, tm), F32)
    for hd in range(N_HEADS):
        rows = slice(hd * HEAD_DIM, (hd + 1) * HEAD_DIM)
        c_hi, c_mid, c_lo = (c.astype(F32) for c in _split3(cum_ref[0, hd:hd + 1, :] * LOG2E))
        c3 = jnp.where(row == 0, c_hi, jnp.where(row == 1, c_mid, jnp.where(row == 2, c_lo, 0.0)))
        one3 = jnp.where(row < 3, 1.0, 0.0)
        q = _head_rms_t(yq_ref[rows, :], qg_ref[...]) * Q_SCALE
        q_out[0, hd] = jnp.concatenate([q, one3, c3, zeros], axis=0).astype(BF)
        k = _head_rms_t(yk_ref[rows, :], kg_ref[...])
        k_out[0, hd] = jnp.concatenate([k, -c3, one3, zeros], axis=0).T.astype(BF)
    _store_lane_tiles(v_out, _dot_nt(wv_ref[...], h).astype(BF), tile)
    z_out[0] = _silu(_dot(h, wz_ref[...])).astype(BF)


def _fox_proj(h, cum, w_in, q_gain, k_gain, *, tile):
    b, s, d = h.shape
    tm = min(TOK_TILE, s)
    wt = w_in.T.astype(BF)
    wq, wk, wv = wt[:MIX_WIDTH], wt[MIX_WIDTH:2 * MIX_WIDTH], wt[2 * MIX_WIDTH:3 * MIX_WIDTH]
    wz = w_in[:, 3 * MIX_WIDTH + N_HEADS:].astype(BF)
    qg, kg = q_gain.reshape(HEAD_DIM, 1), k_gain.reshape(HEAD_DIM, 1)
    return pl.pallas_call(
        functools.partial(_fox_proj_kernel, tile=tile),
        grid=(b, s // tm),
        in_specs=[pl.BlockSpec((1, tm, d), lambda i, j: (i, j, 0)),
                  pl.BlockSpec((1, N_HEADS, tm), lambda i, j: (i, 0, j)),
                  _full(wq.shape), _full(wk.shape), _full(wv.shape), _full(wz.shape),
                  _full(qg.shape), _full(kg.shape)],
        out_specs=[pl.BlockSpec((1, N_HEADS, KEY_PAD, tm), lambda i, j: (i, 0, 0, j)),
                   pl.BlockSpec((1, N_HEADS, tm, KEY_PAD), lambda i, j: (i, 0, j, 0)),
                   pl.BlockSpec((1, tm // tile, MIX_WIDTH, tile), lambda i, j: (i, j, 0, 0)),
                   pl.BlockSpec((1, tm, MIX_WIDTH), lambda i, j: (i, j, 0))],
        out_shape=[jax.ShapeDtypeStruct((b, N_HEADS, KEY_PAD, s), BF),
                   jax.ShapeDtypeStruct((b, N_HEADS, s, KEY_PAD), BF),
                   jax.ShapeDtypeStruct((b, s // tile, MIX_WIDTH, tile), BF),
                   jax.ShapeDtypeStruct((b, s, MIX_WIDTH), BF)],
        scratch_shapes=[pltpu.VMEM((MIX_WIDTH, tm), F32), pltpu.VMEM((MIX_WIDTH, tm), F32)],
        compiler_params=_params(("parallel", "parallel")),
        name="fox_proj",
    )(h, cum, wq, wk, wv, wz, qg, kg)


def _out_proj_kernel(*refs, n_o, has_next):
    o_refs = refs[:n_o]
    z_ref, x_ref, w_ref = refs[n_o:n_o + 3]
    rest = refs[n_o + 3:]
    if has_next:
        g_ref, x_out, h_out = rest
    else:
        (x_out,) = rest
    o = o_refs[0][0].astype(F32)
    for r in o_refs[1:]:
        o = o + r[0].astype(F32)
    y = _dot((o * z_ref[0].astype(F32)).astype(BF), w_ref[...])
    x_new = x_ref[0] + y
    x_out[0] = x_new
    if has_next:
        h_out[0] = _rms_rows(x_new, g_ref[...]).astype(BF)


def _out_proj(o_list, zs, x, w_out, next_gain):
    b, s, d = x.shape
    tm = min(TOK_TILE, s)
    has_next = next_gain is not None
    blk = pl.BlockSpec((1, tm, d), lambda i, j: (i, j, 0))
    args = list(o_list) + [zs, x, w_out.astype(BF)]
    in_specs = [blk] * (len(o_list) + 2) + [_full(w_out.shape)]
    out_shape = [jax.ShapeDtypeStruct((b, s, d), F32)]
    out_specs = [blk]
    if has_next:
        args.append(next_gain.reshape(1, d))
        in_specs.append(_full((1, d)))
        out_shape.append(jax.ShapeDtypeStruct((b, s, d), BF))
        out_specs.append(blk)
    res = pl.pallas_call(
        functools.partial(_out_proj_kernel, n_o=len(o_list), has_next=has_next),
        grid=(b, s // tm), in_specs=in_specs, out_specs=out_specs, out_shape=out_shape,
        compiler_params=_params(("parallel", "parallel")),
        name="out_proj",
    )(*args)
    return (res[0], res[1]) if has_next else (res[0], None)


ITEM_LANES = 512
CMP_Q_TILE = 256
NSA_SLC_Q_TILE = 512
NSA_SLC_TILE = 512
NSA_WIN_Q_TILE = 256
NSA_WIN_TILE = 256
SWA_TILE = 128
FOX_Q_TILE = 512
FOX_K_TILE = 512
FOX_HEADS_PER_STEP = 4


def _nsa_mixer(h, cos, sin, w_in, q_gain, k_gain, cmp_pos, cmp_w1, cmp_w2):
    b, s, _ = h.shape
    g = NSA_KV_HEADS
    r = N_HEADS // g
    qT, kc_tok, vc_tok, ks, kw, vsT, vwT, gates, zs = _nsa_proj(
        h, cos, sin, w_in, q_gain, k_gain, slc_tile=NSA_SLC_TILE, win_tile=NSA_WIN_TILE)
    k_cmp, v_cmpT = _compress(kc_tok, vc_tok, cmp_pos, cmp_w1, cmp_w2, k_gain[0])
    gates5 = gates.reshape(b, 3, g, r, s)
    o_cmp, sel_bias = _cmp_select(qT, k_cmp, v_cmpT, gates5, tq=CMP_Q_TILE)
    o_slc = _flash(qT, ks, vsT, mode="causal", n_par=1, n_rep=r, dq=HEAD_DIM, kc=SLC_KEY_LANES,
                   tq=NSA_SLC_Q_TILE, tk=NSA_SLC_TILE, cw=ITEM_LANES, sel_bias=sel_bias,
                   gates=gates5, gate_branch=1, name="nsa_selected")
    o_win = _flash(qT, kw, vwT, mode="window", n_par=1, n_rep=r, dq=HEAD_DIM, kc=HEAD_DIM,
                   tq=NSA_WIN_Q_TILE, tk=NSA_WIN_TILE, cw=ITEM_LANES, window=NSA_WINDOW,
                   gates=gates5, gate_branch=2, name="nsa_window")
    return [o_cmp, o_slc, o_win], zs


def _swa_mixer(h, cos, sin, w_in, q_gain, k_gain, sinks):
    r = N_HEADS // SWA_KV_HEADS
    qT, k, vT, zs = _swa_proj(h, cos, sin, w_in, q_gain, k_gain, win_tile=SWA_TILE)
    o = _flash(qT, k, vT, mode="window", n_par=1, n_rep=r, dq=HEAD_DIM, kc=HEAD_DIM, tq=SWA_TILE,
               tk=SWA_TILE, cw=r * SWA_TILE, window=SWA_WINDOW, sinks=sinks, name="swa_window")
    return [o], zs


def _fox_mixer(h, w_in, forget_bias, q_gain, k_gain):
    b, s, _ = h.shape
    wf = w_in[:, 3 * MIX_WIDTH:3 * MIX_WIDTH + N_HEADS].T.astype(BF)
    cum = _forget_cum(h, wf, forget_bias)
    qT, k, vT, zs = _fox_proj(h, cum, w_in, q_gain, k_gain, tile=FOX_K_TILE)
    o = _flash(qT.reshape(b, N_HEADS * KEY_PAD, s), k, vT, mode="causal", n_par=FOX_HEADS_PER_STEP, n_rep=1, dq=KEY_PAD,
               kc=KEY_PAD, tq=FOX_Q_TILE, tk=FOX_K_TILE, cw=ITEM_LANES, name="fox_attention")
    return [o], zs


def kernel(x, positions, norm_gains, a_w_in, a_q_gain, a_k_gain, a_cmp_pos, a_cmp_w1, a_cmp_w2, a_w_out,
           b_w_in, b_q_gain, b_k_gain, b_sinks, b_w_out,
           c_w_in, c_forget_bias, c_q_gain, c_k_gain, c_w_out):
    depth = norm_gains.shape[0]
    cos, sin = _rope_tables(positions)
    h = _prenorm(x, norm_gains[0])
    for i in range(depth):
        j, mixer = divmod(i, 3)
        if mixer == 0:
            o_list, zs = _nsa_mixer(h, cos, sin, a_w_in[j], a_q_gain[j], a_k_gain[j],
                                    a_cmp_pos[j], a_cmp_w1[j], a_cmp_w2[j])
            w_out = a_w_out[j]
        elif mixer == 1:
            o_list, zs = _swa_mixer(h, cos, sin, b_w_in[j], b_q_gain[j], b_k_gain[j], b_sinks[j])
            w_out = b_w_out[j]
        else:
            o_list, zs = _fox_mixer(h, c_w_in[j], c_forget_bias[j], c_q_gain[j], c_k_gain[j])
            w_out = c_w_out[j]
        next_gain = norm_gains[i + 1] if i + 1 < depth else None
        x, h = _out_proj(o_list, zs, x, w_out, next_gain)
    return x
```

```python
import functools

import jax
import jax.numpy as jnp
import numpy as np
from jax import lax
from jax.experimental import pallas as pl
from jax.experimental.pallas import tpu as pltpu

D_MODEL = 1024
HEAD_DIM = 64
N_HEADS = 16
MIX_WIDTH = N_HEADS * HEAD_DIM
ROPE_THETA = 10000.0
EPS = 1e-6
SCALE = HEAD_DIM ** -0.5
NEG_INF = -1e30
BIG = 1e30
M_INIT = -1e29
SEL_OFF = -(2.0 ** 100)

NSA_KV_HEADS = 4
NSA_CMP_LEN = 32
NSA_CMP_STRIDE = 16
NSA_SLC_LEN = 64
NSA_TOPK = 16
NSA_WINDOW = 512
SWA_KV_HEADS = 2
SWA_WINDOW = 128

LOG2E = float(np.log2(np.e))
Q_SCALE = SCALE * LOG2E

LANES = 128
KEY_PAD = 128
SLC_KEY_LANES = 256
V_ROWS = 80
FLASH_UNROLL = 4
WINDOW_LOOKAHEAD = 2
WINDOW_SCORE_BUFS = 4
CMP_BLOCK = 128
CMP_MASK_ROWS = 2 * CMP_BLOCK
VMEM_LIMIT = 56 * 1024 * 1024

TOK_TILE = 512
NT_DIMS = (((1,), (1,)), ((), ()))

BF = jnp.bfloat16
F32 = jnp.float32


def _params(sem):
    return pltpu.CompilerParams(dimension_semantics=sem, vmem_limit_bytes=VMEM_LIMIT)


def _dot(a, b):
    return jnp.dot(a, b, preferred_element_type=F32)


def _dot_nt(a, b):
    return lax.dot_general(a, b, NT_DIMS, preferred_element_type=F32)


def _rope_tab_kernel(pos_ref, invf_ref, cos_ref, sin_ref):
    ang = invf_ref[...] * pos_ref[0].astype(F32)
    cos_ref[0] = jnp.cos(ang)
    sin_ref[0] = jnp.sin(ang)


def _rope_tables(positions):
    b, s = positions.shape
    half = HEAD_DIM // 2
    inv_freq = ROPE_THETA ** (-jnp.arange(half, dtype=F32) * 2.0 / HEAD_DIM)
    tm = min(TOK_TILE, s)
    out = jax.ShapeDtypeStruct((b, half, s), F32)
    return pl.pallas_call(
        _rope_tab_kernel,
        grid=(b, s // tm),
        in_specs=[pl.BlockSpec((1, 1, tm), lambda i, j: (i, 0, j)),
                  pl.BlockSpec((half, 1), lambda i, j: (0, 0))],
        out_specs=[pl.BlockSpec((1, half, tm), lambda i, j: (i, 0, j))] * 2,
        out_shape=[out, out],
        compiler_params=_params(("parallel", "parallel")),
        name="rope_tables",
    )(positions.reshape(b, 1, s), inv_freq.reshape(half, 1))


def _rms_rows(x, gain_row):
    y = x * lax.rsqrt(jnp.mean(x * x, axis=-1, keepdims=True) + EPS)
    return y * gain_row


def _prenorm_kernel(x_ref, g_ref, h_ref):
    h_ref[0] = _rms_rows(x_ref[0], g_ref[...]).astype(BF)


def _prenorm(x, gain):
    b, s, d = x.shape
    tm = min(TOK_TILE, s)
    return pl.pallas_call(
        _prenorm_kernel,
        grid=(b, s // tm),
        in_specs=[pl.BlockSpec((1, tm, d), lambda i, j: (i, j, 0)),
                  pl.BlockSpec((1, d), lambda i, j: (0, 0))],
        out_specs=pl.BlockSpec((1, tm, d), lambda i, j: (i, j, 0)),
        out_shape=jax.ShapeDtypeStruct((b, s, d), BF),
        compiler_params=_params(("parallel", "parallel")),
        name="prenorm",
    )(x, gain.reshape(1, d))


def _head_rms_t(y, gain_col):
    ms = jnp.mean(y * y, axis=0, keepdims=True)
    return (y * lax.rsqrt(ms + EPS)) * gain_col


def _rope_t(y, cos, sin):
    half = HEAD_DIM // 2
    x1, x2 = y[:half], y[half:]
    return jnp.concatenate([x1 * cos - x2 * sin, x2 * cos + x1 * sin], axis=0)


def _to_token_major(y):
    pad = jnp.zeros((KEY_PAD - y.shape[0], y.shape[1]), y.dtype)
    return jnp.concatenate([y, pad], axis=0).T


def _silu(z):
    return z * (1.0 / (1.0 + jnp.exp(-z)))


def _sigmoid(z):
    return 1.0 / (1.0 + jnp.exp(-z))


def _store_lane_tiles(out_ref, y, tile):
    for c in range(y.shape[1] // tile):
        out_ref[0, c] = y[:, c * tile:(c + 1) * tile]


def _nsa_proj_kernel(h_ref, cos_ref, sin_ref, wq_ref, wk_ref, wv_ref, wg_ref, wz_ref,
                     qg_ref, kg_ref,
                     q_out, kc_out, vc_out, ks_out, kw_out, vs_out, vw_out, g_out, z_out,
                     y_ref, *, slc_tile, win_tile):
    h = h_ref[0]
    cos, sin = cos_ref[0], sin_ref[0]
    g = NSA_KV_HEADS
    kvd = g * HEAD_DIM
    y_ref[...] = _dot_nt(wq_ref[...], h)
    for hd in range(N_HEADS):
        rows = slice(hd * HEAD_DIM, (hd + 1) * HEAD_DIM)
        y = _rope_t(_head_rms_t(y_ref[rows, :], qg_ref[...]), cos, sin) * Q_SCALE
        q_out[0, rows, :] = y.astype(BF)
    y_ref[0:3 * kvd, :] = _dot_nt(wk_ref[...], h)
    tm = h.shape[0]
    tok = pl.program_id(1) * tm + lax.broadcasted_iota(jnp.int32, (tm, 1), 0)
    blk_lane = HEAD_DIM + lax.shift_right_logical(tok, int(np.log2(NSA_SLC_LEN)))
    blk_hot = lax.broadcasted_iota(jnp.int32, (1, SLC_KEY_LANES), 1) == blk_lane
    for kind in range(3):
        for gi in range(g):
            r0 = (kind * g + gi) * HEAD_DIM
            y = y_ref[r0:r0 + HEAD_DIM, :]
            if kind > 0:
                y = _head_rms_t(y, kg_ref[:, kind:kind + 1])
            yt = _to_token_major(_rope_t(y, cos, sin))
            if kind == 0:
                kc_out[0, gi] = yt[:, :HEAD_DIM]
            elif kind == 1:
                wide = jnp.concatenate([yt, jnp.zeros((tm, SLC_KEY_LANES - KEY_PAD), F32)], axis=1)
                ks_out[0, gi] = jnp.where(blk_hot, 1.0, wide).astype(BF)
            else:
                kw_out[0, gi] = yt.astype(BF)
    y_ref[0:3 * kvd, :] = _dot_nt(wv_ref[...], h)
    for gi in range(g):
        r0 = gi * HEAD_DIM
        vc_out[0, gi] = _to_token_major(y_ref[r0:r0 + HEAD_DIM, :])[:, :HEAD_DIM]
    _store_lane_tiles(vs_out, y_ref[kvd:2 * kvd, :].astype(BF), slc_tile)
    _store_lane_tiles(vw_out, y_ref[2 * kvd:3 * kvd, :].astype(BF), win_tile)
    g_out[0] = _sigmoid(_dot_nt(wg_ref[...], h))
    z_out[0] = _silu(_dot(h, wz_ref[...])).astype(BF)


def _full(shape):
    nd = len(shape)
    return pl.BlockSpec(shape, lambda i, j, _n=nd: (0,) * _n)


def _nsa_proj(h, cos, sin, w_in, q_gain, k_gain, *, slc_tile, win_tile):
    b, s, d = h.shape
    g = NSA_KV_HEADS
    kvd = g * HEAD_DIM
    tm = min(TOK_TILE, s)
    sizes = [MIX_WIDTH] + [kvd] * 6 + [3 * N_HEADS]
    off = np.cumsum([0] + sizes)
    wt = w_in.T.astype(BF)
    wq = wt[off[0]:off[1]]
    wk = jnp.concatenate([wt[off[1]:off[2]], wt[off[3]:off[4]], wt[off[5]:off[6]]], axis=0)
    wv = jnp.concatenate([wt[off[2]:off[3]], wt[off[4]:off[5]], wt[off[6]:off[7]]], axis=0)
    wg = wt[off[7]:off[8]]
    wz = w_in[:, off[8]:].astype(BF)
    qg = q_gain.reshape(HEAD_DIM, 1)
    kg = k_gain.T
    n_t = s // tm
    out_shape = [
        jax.ShapeDtypeStruct((b, MIX_WIDTH, s), BF),
        jax.ShapeDtypeStruct((b, g, s, HEAD_DIM), F32),
        jax.ShapeDtypeStruct((b, g, s, HEAD_DIM), F32),
        jax.ShapeDtypeStruct((b, g, s, SLC_KEY_LANES), BF),
        jax.ShapeDtypeStruct((b, g, s, KEY_PAD), BF),
        jax.ShapeDtypeStruct((b, s // slc_tile, kvd, slc_tile), BF),
        jax.ShapeDtypeStruct((b, s // win_tile, kvd, win_tile), BF),
        jax.ShapeDtypeStruct((b, 3 * N_HEADS, s), F32),
        jax.ShapeDtypeStruct((b, s, MIX_WIDTH), BF),
    ]
    out_specs = [
        pl.BlockSpec((1, MIX_WIDTH, tm), lambda i, j: (i, 0, j)),
        pl.BlockSpec((1, g, tm, HEAD_DIM), lambda i, j: (i, 0, j, 0)),
        pl.BlockSpec((1, g, tm, HEAD_DIM), lambda i, j: (i, 0, j, 0)),
        pl.BlockSpec((1, g, tm, SLC_KEY_LANES), lambda i, j: (i, 0, j, 0)),
        pl.BlockSpec((1, g, tm, KEY_PAD), lambda i, j: (i, 0, j, 0)),
        pl.BlockSpec((1, tm // slc_tile, kvd, slc_tile), lambda i, j: (i, j, 0, 0)),
        pl.BlockSpec((1, tm // win_tile, kvd, win_tile), lambda i, j: (i, j, 0, 0)),
        pl.BlockSpec((1, 3 * N_HEADS, tm), lambda i, j: (i, 0, j)),
        pl.BlockSpec((1, tm, MIX_WIDTH), lambda i, j: (i, j, 0)),
    ]
    in_specs = [
        pl.BlockSpec((1, tm, d), lambda i, j: (i, j, 0)),
        pl.BlockSpec((1, HEAD_DIM // 2, tm), lambda i, j: (i, 0, j)),
        pl.BlockSpec((1, HEAD_DIM // 2, tm), lambda i, j: (i, 0, j)),
        _full(wq.shape), _full(wk.shape), _full(wv.shape), _full(wg.shape), _full(wz.shape),
        _full(qg.shape), _full(kg.shape),
    ]
    return pl.pallas_call(
        functools.partial(_nsa_proj_kernel, slc_tile=slc_tile, win_tile=win_tile),
        grid=(b, n_t), in_specs=in_specs, out_specs=out_specs, out_shape=out_shape,
        scratch_shapes=[pltpu.VMEM((MIX_WIDTH, tm), F32)],
        compiler_params=_params(("parallel", "parallel")),
        name="nsa_proj",
    )(h, cos, sin, wq, wk, wv, wg, wz, qg, kg)


def _gelu_tanh(x):
    c = np.float32(np.sqrt(2.0 / np.pi))
    return 0.5 * x * (1.0 + jnp.tanh(c * (x + 0.044715 * (x * x * x))))


def _compress_kernel(kc_ref, vc_ref, pos_ref, w1_ref, w2_ref, kg_ref, kcmp_out, vcmp_out):
    for which, (src, dst) in enumerate(((kc_ref, kcmp_out), (vc_ref, vcmp_out))):
        x = src[0, 0]
        n = x.shape[0]
        half = x.shape[1]
        xa = (x + pos_ref[which, 0:1, :]).astype(BF)
        xb = (x + pos_ref[which, 1:2, :]).astype(BF)
        ua = _dot(xa, w1_ref[which, :half, :])
        ub = _dot(xb, w1_ref[which, half:, :])
        row = lax.broadcasted_iota(jnp.int32, (n, 1), 0)
        ub_next = jnp.where(row == n - 1, 0.0, pltpu.roll(ub, n - 1, 0))
        hid = _gelu_tanh(ua + ub_next)
        y = _dot(hid.astype(BF), w2_ref[which])
        if which == 0:
            y = _rms_rows(y, kg_ref[...])
            dst[0, 0] = y.astype(BF)
        else:
            pad = jnp.zeros((n, KEY_PAD - HEAD_DIM), F32)
            dst[0, 0] = jnp.concatenate([y, pad], axis=1).T[:HEAD_DIM].astype(BF)


def _compress(kc_tok, vc_tok, cmp_pos, cmp_w1, cmp_w2, k_gain0):
    b, g, s, _ = kc_tok.shape
    n_chunk = s // NSA_CMP_STRIDE
    flat = NSA_CMP_STRIDE * HEAD_DIM
    kc = kc_tok.reshape(b, g, n_chunk, flat)
    vc = vc_tok.reshape(b, g, n_chunk, flat)
    pos = cmp_pos.reshape(2, 2, flat)
    w1 = cmp_w1.astype(BF)
    w2 = cmp_w2.astype(BF)
    blk = pl.BlockSpec((1, 1, n_chunk, flat), lambda i, j: (i, j, 0, 0))
    return pl.pallas_call(
        _compress_kernel,
        grid=(b, g),
        in_specs=[blk, blk, _full(pos.shape), _full(w1.shape), _full(w2.shape),
                  _full((1, HEAD_DIM))],
        out_specs=[pl.BlockSpec((1, 1, n_chunk, HEAD_DIM), lambda i, j: (i, j, 0, 0)),
                   pl.BlockSpec((1, 1, HEAD_DIM, n_chunk), lambda i, j: (i, j, 0, 0))],
        out_shape=[jax.ShapeDtypeStruct((b, g, n_chunk, HEAD_DIM), BF),
                   jax.ShapeDtypeStruct((b, g, HEAD_DIM, n_chunk), BF)],
        compiler_params=_params(("parallel", "parallel")),
        name="nsa_compress",
    )(kc, vc, pos, w1, w2, k_gain0.reshape(1, HEAD_DIM))


def _stack_heads(q_ref, n_heads, dk):
    return jnp.concatenate([q_ref[0, r * dk:(r + 1) * dk, :] for r in range(n_heads)], axis=1)


def _unstack_store(o, out_ref, n_heads, tq):
    rows = jnp.concatenate([o[:, r * tq:(r + 1) * tq] for r in range(n_heads)], axis=0)
    out_ref[0] = rows.T.astype(out_ref.dtype)


def _gate_row(gate_ref, n_heads):
    return jnp.concatenate([gate_ref[0, 0, 0, r:r + 1, :] for r in range(n_heads)], axis=1)


def _cmp_branch(rows, q_ref, kc_ref, vc_ref, ov_ref, gate_ref, o_out, sel_out, *, tq, n_blk):
    r_heads = N_HEADS // NSA_KV_HEADS
    nq = r_heads * tq
    q0 = pl.program_id(2) * tq
    n_cmp = kc_ref.shape[2] - 1
    q4 = _stack_heads(q_ref, r_heads, HEAD_DIM)
    s = _dot(kc_ref[0, 0, 0:rows, :], q4)
    t_row = q0 + (lax.broadcasted_iota(jnp.int32, (1, nq), 1) & (tq - 1))
    lo = max(rows - CMP_MASK_ROWS, 0)
    c_col = lo + lax.broadcasted_iota(jnp.int32, (rows - lo, 1), 0)
    valid = (c_col * NSA_CMP_STRIDE + (NSA_CMP_LEN - 1) <= t_row) & (c_col < n_cmp)
    s_new = jnp.where(valid, s[lo:], NEG_INF)
    m = jnp.max(s_new, axis=0, keepdims=True)
    if lo:
        m = jnp.maximum(m, jnp.max(s[:lo], axis=0, keepdims=True))
    e = jnp.where(valid, jnp.exp2(s_new - m), 0.0).astype(BF)
    if lo:
        e = jnp.concatenate([jnp.exp2(s[:lo] - m).astype(BF), e], axis=0)
    one_row = jnp.where(lax.broadcasted_iota(jnp.int32, (V_ROWS - HEAD_DIM, rows), 0) == 0, 1.0, 0.0).astype(BF)
    n_live = min(n_blk, rows * NSA_CMP_STRIDE // NSA_SLC_LEN + 8)
    lhs = jnp.concatenate([vc_ref[0, 0, :, 0:rows], one_row, ov_ref[0:n_live, 0:rows]], axis=0)
    res = _dot(lhs, e)
    l = res[HEAD_DIM:HEAD_DIM + 1]
    inv = jnp.where(l > 0.0, 1.0 / jnp.where(l > 0.0, l, 1.0), 0.0)
    o = res[:HEAD_DIM] * (inv * _gate_row(gate_ref, r_heads))
    _unstack_store(o, o_out, r_heads, tq)
    w = res[V_ROWS:V_ROWS + n_live] * inv
    imp = w[:, 0:tq]
    for r in range(1, r_heads):
        imp = imp + w[:, r * tq:(r + 1) * tq]
    t1 = q0 + lax.broadcasted_iota(jnp.int32, (1, tq), 1)
    cur = lax.shift_right_logical(t1, int(np.log2(NSA_SLC_LEN)))
    blk = lax.broadcasted_iota(jnp.int32, (n_live, tq), 0)
    forced = (blk == 0) | (blk == cur) | (blk == cur - 1)
    imp = jnp.where(forced, BIG, jnp.where(blk > cur, NEG_INF, imp))
    for _ in range(min(NSA_TOPK, n_blk)):
        best = jnp.max(imp, axis=0, keepdims=True)
        first = jnp.min(jnp.where(imp == best, blk, n_blk), axis=0, keepdims=True)
        imp = jnp.where(blk == first, -jnp.inf, imp)
    sel_out[0, 0, 0:n_live, :] = jnp.where(imp == -jnp.inf, 0.0, SEL_OFF).astype(BF)
    if n_live < n_blk:
        sel_out[0, 0, n_live:n_blk, :] = jnp.full((n_blk - n_live, tq), SEL_OFF, BF)


def _cmp_select_kernel(q_ref, kc_ref, vc_ref, ov_ref, gate_ref, o_out, sel_out, *, tq, n_blk):
    q0 = pl.program_id(2) * tq
    n_chunk = kc_ref.shape[2]
    n_need = jnp.minimum((q0 + tq - NSA_CMP_LEN) // NSA_CMP_STRIDE + 1, n_chunk - 1)
    n_steps = n_chunk // CMP_BLOCK
    need_steps = (n_need + CMP_BLOCK - 1) // CMP_BLOCK
    for k in range(1, n_steps + 1):
        @pl.when(need_steps == k)
        def _(k=k):
            _cmp_branch(k * CMP_BLOCK, q_ref, kc_ref, vc_ref, ov_ref, gate_ref, o_out, sel_out, tq=tq, n_blk=n_blk)


def _overlap_matrix(s):
    n_chunk = s // NSA_CMP_STRIDE
    n_blk = s // NSA_SLC_LEN
    c0 = np.arange(n_chunk) * NSA_CMP_STRIDE
    c1 = c0 + NSA_CMP_LEN - 1
    b0 = np.arange(n_blk) * NSA_SLC_LEN
    ov = np.minimum(c1[None, :], b0[:, None] + NSA_SLC_LEN - 1) - np.maximum(c0[None, :], b0[:, None]) + 1
    return jnp.asarray(np.clip(ov, 0, None) / NSA_CMP_LEN, BF)


def _cmp_select(qT, k_cmp, v_cmpT, gates5, *, tq):
    b, _, s = qT.shape
    g = NSA_KV_HEADS
    r_heads = N_HEADS // g
    n_chunk = k_cmp.shape[2]
    n_blk = s // NSA_SLC_LEN
    ov = _overlap_matrix(s)
    rows = r_heads * HEAD_DIM
    return pl.pallas_call(
        functools.partial(_cmp_select_kernel, tq=tq, n_blk=n_blk),
        grid=(b, g, s // tq),
        in_specs=[
            pl.BlockSpec((1, rows, tq), lambda i, j, k: (i, j, k)),
            pl.BlockSpec((1, 1, n_chunk, HEAD_DIM), lambda i, j, k: (i, j, 0, 0)),
            pl.BlockSpec((1, 1, HEAD_DIM, n_chunk), lambda i, j, k: (i, j, 0, 0)),
            pl.BlockSpec((n_blk, n_chunk), lambda i, j, k: (0, 0)),
            pl.BlockSpec((1, 1, 1, r_heads, tq), lambda i, j, k: (i, 0, j, 0, k)),
        ],
        out_specs=[pl.BlockSpec((1, tq, rows), lambda i, j, k: (i, k, j)),
                   pl.BlockSpec((1, 1, n_blk, tq), lambda i, j, k: (i, j, 0, k))],
        out_shape=[jax.ShapeDtypeStruct((b, s, MIX_WIDTH), BF),
                   jax.ShapeDtypeStruct((b, g, n_blk, s), BF)],
        compiler_params=_params(("parallel", "parallel", "parallel")),
        name="nsa_cmp_select",
    )(qT, k_cmp, v_cmpT, ov, gates5)


def _flash_kernel(*refs, mode, n_par, n_rep, dq, kc, tq, tk, cw, window, n_blk, has_sink, has_gate):
    it = iter(refs)
    q_ref, k_ref, v_ref = next(it), next(it), next(it)
    sel_ref = next(it) if n_blk else None
    sink_ref = next(it) if has_sink else None
    gate_ref = next(it) if has_gate else None
    out_ref = next(it)
    qs_ref, m_ref, acc_ref, mt_ref = (next(it) for _ in range(4))
    s_bufs = tuple(it)

    nq = n_rep * tq
    items = [(p, c) for p in range(n_par) for c in range(nq // cw)]
    n_items = len(items)
    grp = pl.program_id(1)
    q0 = pl.program_id(2) * tq
    acc_row = lax.broadcasted_iota(jnp.int32, (V_ROWS, nq), 0)
    for p in range(n_par):
        for r in range(n_rep):
            hd = p * n_rep + r
            qs_ref[p, 0:dq, r * tq:(r + 1) * tq] = q_ref[0, hd * dq:(hd + 1) * dq, :]
        if n_blk:
            qs_ref[p, dq:dq + n_blk, :] = jnp.concatenate([sel_ref[0, 0]] * n_rep, axis=1)
            if dq + n_blk < kc:
                qs_ref[p, dq + n_blk:kc, :] = jnp.zeros((kc - dq - n_blk, nq), BF)
        if has_sink:
            m_ref[p] = jnp.concatenate(
                [jnp.full((1, tq), sink_ref[(grp * n_par + p) * n_rep + r] * LOG2E, F32) for r in range(n_rep)],
                axis=1)
            acc_ref[p] = jnp.where(acc_row == HEAD_DIM, 1.0, 0.0)
        else:
            m_ref[p] = jnp.full((1, nq), M_INIT, F32)
            acc_ref[p] = jnp.zeros((V_ROWS, nq), F32)

    t_row = q0 + (lax.broadcasted_iota(jnp.int32, (1, nq), 1) & (tq - 1))
    one_row = jnp.where(lax.broadcasted_iota(jnp.int32, (V_ROWS - HEAD_DIM, tk), 0) == 0, 1.0, 0.0).astype(BF)

    def stage_a(item, j, key0, slot, kind):
        p, c = item
        cols = slice(c * cw, (c + 1) * cw)
        s = _dot(k_ref[0, p, j][:, :kc], qs_ref[p, :, cols])
        if kind is not None:
            key = key0 + lax.broadcasted_iota(jnp.int32, (tk, 1), 0)
            ok = key <= t_row[:, cols] if kind == "causal" else key > t_row[:, cols] - window
            s = jnp.where(ok, s, NEG_INF)
        s_bufs[slot][...] = s
        mt_ref[slot] = jnp.max(s, axis=0, keepdims=True)

    def stage_b(item, j, slot):
        p, c = item
        cols = slice(c * cw, (c + 1) * cw)
        m_old = m_ref[p, :, cols]
        m_new = jnp.maximum(m_old, mt_ref[slot])
        pr = jnp.exp2(s_bufs[slot][...] - m_new).astype(BF)
        alpha = jnp.exp2(m_old - m_new)
        v = jnp.concatenate([v_ref[0, j, p * HEAD_DIM:(p + 1) * HEAD_DIM, :], one_row], axis=0)
        acc_ref[p, :, cols] = alpha * acc_ref[p, :, cols] + _dot(v, pr)
        m_ref[p, :, cols] = m_new

    if mode == "causal":
        assert n_items % 2 == 0
        n_full = q0 // tk

        def step(j, kind, next_kind, last):
            for idx, item in enumerate(items):
                slot = idx % 2
                if idx + 1 < n_items:
                    stage_a(items[idx + 1], j, j * tk, 1 - slot, kind)
                elif not last:
                    stage_a(items[0], j + 1, (j + 1) * tk, 1 - slot, next_kind)
                stage_b(item, j, slot)

        stage_a(items[0], 0, 0, 0, "causal")

        def body(j, carry):
            step(j, None, None, False)
            return carry

        def body_group(i, carry):
            for u in range(FLASH_UNROLL):
                step(FLASH_UNROLL * i + u, None, None, False)
            return carry

        n_main = jnp.maximum(n_full - 1, 0)
        n_groups = lax.shift_right_logical(n_main, int(np.log2(FLASH_UNROLL)))
        lax.fori_loop(0, n_groups, body_group, 0)
        lax.fori_loop(FLASH_UNROLL * n_groups, n_main, body, 0)

        @pl.when(n_full >= 1)
        def _():
            step(n_full - 1, None, "causal", False)

        step(n_full, "causal", None, True)
    else:
        w_tiles, q_tiles = window // tk, tq // tk
        work = []
        for i in range(w_tiles + q_tiles):
            jv = q0 // tk - w_tiles + i
            key0 = jnp.where(jv < 0, -(1 << 30), jv * tk)
            kind = "causal" if i >= w_tiles else "lower"
            work += [(item, jnp.maximum(jv, 0), key0, kind) for item in items]
        n_buf = len(s_bufs)
        ahead = WINDOW_LOOKAHEAD
        for n in range(ahead):
            stage_a(work[n][0], work[n][1], work[n][2], n % n_buf, work[n][3])
        for n, (item, j, _, _) in enumerate(work):
            if n + ahead < len(work):
                nxt = work[n + ahead]
                stage_a(nxt[0], nxt[1], nxt[2], (n + ahead) % n_buf, nxt[3])
            stage_b(item, j, n % n_buf)

    outs = []
    for p in range(n_par):
        acc = acc_ref[p]
        o = acc[:HEAD_DIM] * (1.0 / acc[HEAD_DIM:HEAD_DIM + 1])
        if has_gate:
            o = o * _gate_row(gate_ref, n_rep)
        outs += [o[:, r * tq:(r + 1) * tq] for r in range(n_rep)]
    out_ref[0] = jnp.concatenate(outs, axis=0).T.astype(out_ref.dtype)


def _flash(qT, k_tok, vT_tiles, *, mode, n_par, n_rep, dq, kc, tq, tk, cw, window=None,
           sel_bias=None, sinks=None, gates=None, gate_branch=0, name="flash"):
    b, _, s = qT.shape
    kh, k_lanes = k_tok.shape[1], k_tok.shape[3]
    n_t = s // tk
    k5 = k_tok.reshape(b, kh, n_t, tk, k_lanes)
    n_grp = kh // n_par
    heads = n_par * n_rep
    nq = n_rep * tq
    n_blk = sel_bias.shape[2] if sel_bias is not None else 0
    has_sink, has_gate = sinks is not None, gates is not None
    args = [qT, k5, vT_tiles]
    in_specs = [
        pl.BlockSpec((1, heads * dq, tq), lambda i, j, k: (i, j, k)),
        pl.BlockSpec((1, n_par, n_t, tk, k_lanes), lambda i, j, k: (i, j, 0, 0, 0)),
        pl.BlockSpec((1, n_t, n_par * HEAD_DIM, tk), lambda i, j, k: (i, 0, j, 0)),
    ]
    if n_blk:
        assert dq + n_blk <= kc
        args.append(sel_bias)
        in_specs.append(pl.BlockSpec((1, 1, n_blk, tq), lambda i, j, k: (i, j, 0, k)))
    if has_sink:
        args.append(sinks.astype(F32))
        in_specs.append(pl.BlockSpec(memory_space=pltpu.SMEM))
    if has_gate:
        args.append(gates)
        in_specs.append(pl.BlockSpec((1, 1, 1, n_rep, tq),
                                     lambda i, j, k, _br=gate_branch: (i, _br, j, 0, k)))
    n_sbuf = 2 if mode == "causal" else WINDOW_SCORE_BUFS
    scratch = [pltpu.VMEM((n_par, kc, nq), BF), pltpu.VMEM((n_par, 1, nq), F32),
               pltpu.VMEM((n_par, V_ROWS, nq), F32), pltpu.VMEM((n_sbuf, 1, cw), F32)]
    scratch += [pltpu.VMEM((tk, cw), F32)] * n_sbuf
    kern = functools.partial(_flash_kernel, mode=mode, n_par=n_par, n_rep=n_rep, dq=dq, kc=kc, tq=tq, tk=tk,
                             cw=cw, window=window, n_blk=n_blk, has_sink=has_sink, has_gate=has_gate)
    return pl.pallas_call(
        kern,
        grid=(b, n_grp, s // tq),
        in_specs=in_specs,
        out_specs=pl.BlockSpec((1, tq, heads * HEAD_DIM), lambda i, j, k: (i, k, j)),
        out_shape=jax.ShapeDtypeStruct((b, s, MIX_WIDTH), BF),
        scratch_shapes=scratch,
        compiler_params=_params(("parallel", "parallel", "arbitrary")),
        name=name,
    )(*args)


def _swa_proj_kernel(h_ref, cos_ref, sin_ref, wq_ref, wk_ref, wv_ref, wz_ref, qg_ref, kg_ref,
                     q_out, k_out, v_out, z_out, y_ref, *, win_tile):
    h = h_ref[0]
    cos, sin = cos_ref[0], sin_ref[0]
    y_ref[...] = _dot_nt(wq_ref[...], h)
    for hd in range(N_HEADS):
        rows = slice(hd * HEAD_DIM, (hd + 1) * HEAD_DIM)
        q_out[0, rows, :] = (_rope_t(_head_rms_t(y_ref[rows, :], qg_ref[...]), cos, sin) * Q_SCALE).astype(BF)
    kvd = SWA_KV_HEADS * HEAD_DIM
    y_ref[0:kvd, :] = _dot_nt(wk_ref[...], h)
    for gi in range(SWA_KV_HEADS):
        rows = slice(gi * HEAD_DIM, (gi + 1) * HEAD_DIM)
        k_out[0, gi] = _to_token_major(_rope_t(_head_rms_t(y_ref[rows, :], kg_ref[...]), cos, sin)).astype(BF)
    _store_lane_tiles(v_out, _dot_nt(wv_ref[...], h).astype(BF), win_tile)
    z_out[0] = _silu(_dot(h, wz_ref[...])).astype(BF)


def _swa_proj(h, cos, sin, w_in, q_gain, k_gain, *, win_tile):
    b, s, d = h.shape
    g = SWA_KV_HEADS
    kvd = g * HEAD_DIM
    tm = min(TOK_TILE, s)
    wt = w_in.T.astype(BF)
    wq, wk, wv = wt[:MIX_WIDTH], wt[MIX_WIDTH:MIX_WIDTH + kvd], wt[MIX_WIDTH + kvd:MIX_WIDTH + 2 * kvd]
    wz = w_in[:, MIX_WIDTH + 2 * kvd:].astype(BF)
    qg, kg = q_gain.reshape(HEAD_DIM, 1), k_gain.reshape(HEAD_DIM, 1)
    wpt = tm // win_tile
    half = HEAD_DIM // 2
    return pl.pallas_call(
        functools.partial(_swa_proj_kernel, win_tile=win_tile),
        grid=(b, s // tm),
        in_specs=[pl.BlockSpec((1, tm, d), lambda i, j: (i, j, 0)),
                  pl.BlockSpec((1, half, tm), lambda i, j: (i, 0, j)),
                  pl.BlockSpec((1, half, tm), lambda i, j: (i, 0, j)),
                  _full(wq.shape), _full(wk.shape), _full(wv.shape), _full(wz.shape),
                  _full(qg.shape), _full(kg.shape)],
        out_specs=[pl.BlockSpec((1, MIX_WIDTH, tm), lambda i, j: (i, 0, j)),
                   pl.BlockSpec((1, g, tm, KEY_PAD), lambda i, j: (i, 0, j, 0)),
                   pl.BlockSpec((1, wpt, kvd, win_tile), lambda i, j: (i, j, 0, 0)),
                   pl.BlockSpec((1, tm, MIX_WIDTH), lambda i, j: (i, j, 0))],
        out_shape=[jax.ShapeDtypeStruct((b, MIX_WIDTH, s), BF),
                   jax.ShapeDtypeStruct((b, g, s, KEY_PAD), BF),
                   jax.ShapeDtypeStruct((b, s // win_tile, kvd, win_tile), BF),
                   jax.ShapeDtypeStruct((b, s, MIX_WIDTH), BF)],
        scratch_shapes=[pltpu.VMEM((MIX_WIDTH, tm), F32)],
        compiler_params=_params(("parallel", "parallel")),
        name="swa_proj",
    )(h, cos, sin, wq, wk, wv, wz, qg, kg)


def _split3(x):
    hi = x.astype(BF)
    r1 = x - hi.astype(F32)
    mid = r1.astype(BF)
    lo = (r1 - mid.astype(F32)).astype(BF)
    return hi, mid, lo


def _forget_cum_kernel(h_ref, wf_ref, bias_ref, tri_ref, cum_out, carry_ref):
    @pl.when(pl.program_id(1) == 0)
    def _():
        carry_ref[...] = jnp.zeros_like(carry_ref)

    x = _dot_nt(wf_ref[...], h_ref[0]) + bias_ref[...]
    logf = jnp.minimum(x, 0.0) - jnp.log(1.0 + jnp.exp(-jnp.abs(x)))
    tri = tri_ref[...]
    hi, mid, lo = _split3(logf)
    cum = (_dot(hi, tri) + _dot(mid, tri)) + _dot(lo, tri) + carry_ref[:, 0:1]
    cum_out[0] = cum
    carry_ref[...] = jnp.broadcast_to(cum[:, -1:], carry_ref.shape)


def _forget_cum(h, wf, bias):
    b, s, d = h.shape
    tm = min(TOK_TILE, s)
    tri = jnp.asarray(np.arange(tm)[:, None] <= np.arange(tm)[None, :], BF)
    return pl.pallas_call(
        _forget_cum_kernel,
        grid=(b, s // tm),
        in_specs=[pl.BlockSpec((1, tm, d), lambda i, j: (i, j, 0)),
                  _full(wf.shape), _full((N_HEADS, 1)), _full(tri.shape)],
        out_specs=pl.BlockSpec((1, N_HEADS, tm), lambda i, j: (i, 0, j)),
        out_shape=jax.ShapeDtypeStruct((b, N_HEADS, s), F32),
        scratch_shapes=[pltpu.VMEM((N_HEADS, LANES), F32)],
        compiler_params=_params(("parallel", "arbitrary")),
        name="fox_forget_cum",
    )(h, wf, bias.reshape(N_HEADS, 1).astype(F32), tri)


def _fox_proj_kernel(h_ref, cum_ref, wq_ref, wk_ref, wv_ref, wz_ref, qg_ref, kg_ref,
                     q_out, k_out, v_out, z_out, yq_ref, yk_ref, *, tile):
    h = h_ref[0]
    tm = h.shape[0]
    yq_ref[...] = _dot_nt(wq_ref[...], h)
    yk_ref[...] = _dot_nt(wk_ref[...], h)
    row = lax.broadcasted_iota(jnp.int32, (8, tm), 0)
    zeros = jnp.zeros((KEY_PAD - HEAD_DIM - 16, tm), F32)
    for hd in range(N_HEADS):
        rows = slice(hd * HEAD_DIM, (hd + 1) * HEAD_DIM)
        c_hi, c_mid, c_lo = (c.astype(F32) for c in _split3(cum_ref[0, hd:hd + 1, :] * LOG2E))
        c3 = jnp.where(row == 0, c_hi, jnp.where(row == 1, c_mid, jnp.where(row == 2, c_lo, 0.0)))
        one3 = jnp.where(row < 3, 1.0, 0.0)
        q = _head_rms_t(yq_ref[rows, :], qg_ref[...]) * Q_SCALE
        q_out[0, hd] = jnp.concatenate([q, one3, c3, zeros], axis=0).astype(BF)
        k = _head_rms_t(yk_ref[rows, :], kg_ref[...])
        k_out[0, hd] = jnp.concatenate([k, -c3, one3, zeros], axis=0).T.astype(BF)
    _store_lane_tiles(v_out, _dot_nt(wv_ref[...], h).astype(BF), tile)
    z_out[0] = _silu(_dot(h, wz_ref[...])).astype(BF)


def _fox_proj(h, cum, w_in, q_gain, k_gain, *, tile):
    b, s, d = h.shape
    tm = min(TOK_TILE, s)
    wt = w_in.T.astype(BF)
    wq, wk, wv = wt[:MIX_WIDTH], wt[MIX_WIDTH:2 * MIX_WIDTH], wt[2 * MIX_WIDTH:3 * MIX_WIDTH]
    wz = w_in[:, 3 * MIX_WIDTH + N_HEADS:].astype(BF)
    qg, kg = q_gain.reshape(HEAD_DIM, 1), k_gain.reshape(HEAD_DIM, 1)
    return pl.pallas_call(
        functools.partial(_fox_proj_kernel, tile=tile),
        grid=(b, s // tm),
        in_specs=[pl.BlockSpec((1, tm, d), lambda i, j: (i, j, 0)),
                  pl.BlockSpec((1, N_HEADS, tm), lambda i, j: (i, 0, j)),
                  _full(wq.shape), _full(wk.shape), _full(wv.shape), _full(wz.shape),
                  _full(qg.shape), _full(kg.shape)],
        out_specs=[pl.BlockSpec((1, N_HEADS, KEY_PAD, tm), lambda i, j: (i, 0, 0, j)),
                   pl.BlockSpec((1, N_HEADS, tm, KEY_PAD), lambda i, j: (i, 0, j, 0)),
                   pl.BlockSpec((1, tm // tile, MIX_WIDTH, tile), lambda i, j: (i, j, 0, 0)),
                   pl.BlockSpec((1, tm, MIX_WIDTH), lambda i, j: (i, j, 0))],
        out_shape=[jax.ShapeDtypeStruct((b, N_HEADS, KEY_PAD, s), BF),
                   jax.ShapeDtypeStruct((b, N_HEADS, s, KEY_PAD), BF),
                   jax.ShapeDtypeStruct((b, s // tile, MIX_WIDTH, tile), BF),
                   jax.ShapeDtypeStruct((b, s, MIX_WIDTH), BF)],
        scratch_shapes=[pltpu.VMEM((MIX_WIDTH, tm), F32), pltpu.VMEM((MIX_WIDTH, tm), F32)],
        compiler_params=_params(("parallel", "parallel")),
        name="fox_proj",
    )(h, cum, wq, wk, wv, wz, qg, kg)


def _out_proj_kernel(*refs, n_o, has_next):
    o_refs = refs[:n_o]
    z_ref, x_ref, w_ref = refs[n_o:n_o + 3]
    rest = refs[n_o + 3:]
    if has_next:
        g_ref, x_out, h_out = rest
    else:
        (x_out,) = rest
    o = o_refs[0][0].astype(F32)
    for r in o_refs[1:]:
        o = o + r[0].astype(F32)
    y = _dot((o * z_ref[0].astype(F32)).astype(BF), w_ref[...])
    x_new = x_ref[0] + y
    x_out[0] = x_new
    if has_next:
        h_out[0] = _rms_rows(x_new, g_ref[...]).astype(BF)


def _out_proj(o_list, zs, x, w_out, next_gain):
    b, s, d = x.shape
    tm = min(TOK_TILE, s)
    has_next = next_gain is not None
    blk = pl.BlockSpec((1, tm, d), lambda i, j: (i, j, 0))
    args = list(o_list) + [zs, x, w_out.astype(BF)]
    in_specs = [blk] * (len(o_list) + 2) + [_full(w_out.shape)]
    out_shape = [jax.ShapeDtypeStruct((b, s, d), F32)]
    out_specs = [blk]
    if has_next:
        args.append(next_gain.reshape(1, d))
        in_specs.append(_full((1, d)))
        out_shape.append(jax.ShapeDtypeStruct((b, s, d), BF))
        out_specs.append(blk)
    res = pl.pallas_call(
        functools.partial(_out_proj_kernel, n_o=len(o_list), has_next=has_next),
        grid=(b, s // tm), in_specs=in_specs, out_specs=out_specs, out_shape=out_shape,
        compiler_params=_params(("parallel", "parallel")),
        name="out_proj",
    )(*args)
    return (res[0], res[1]) if has_next else (res[0], None)


ITEM_LANES = 512
CMP_Q_TILE = 512
NSA_SLC_Q_TILE = 512
NSA_SLC_TILE = 512
NSA_WIN_Q_TILE = 256
NSA_WIN_TILE = 256
SWA_TILE = 128
FOX_Q_TILE = 512
FOX_K_TILE = 512
FOX_HEADS_PER_STEP = 4


def _nsa_mixer(h, cos, sin, w_in, q_gain, k_gain, cmp_pos, cmp_w1, cmp_w2):
    b, s, _ = h.shape
    g = NSA_KV_HEADS
    r = N_HEADS // g
    qT, kc_tok, vc_tok, ks, kw, vsT, vwT, gates, zs = _nsa_proj(
        h, cos, sin, w_in, q_gain, k_gain, slc_tile=NSA_SLC_TILE, win_tile=NSA_WIN_TILE)
    k_cmp, v_cmpT = _compress(kc_tok, vc_tok, cmp_pos, cmp_w1, cmp_w2, k_gain[0])
    gates5 = gates.reshape(b, 3, g, r, s)
    o_cmp, sel_bias = _cmp_select(qT, k_cmp, v_cmpT, gates5, tq=CMP_Q_TILE)
    o_slc = _flash(qT, ks, vsT, mode="causal", n_par=1, n_rep=r, dq=HEAD_DIM, kc=SLC_KEY_LANES,
                   tq=NSA_SLC_Q_TILE, tk=NSA_SLC_TILE, cw=ITEM_LANES, sel_bias=sel_bias,
                   gates=gates5, gate_branch=1, name="nsa_selected")
    o_win = _flash(qT, kw, vwT, mode="window", n_par=1, n_rep=r, dq=HEAD_DIM, kc=HEAD_DIM,
                   tq=NSA_WIN_Q_TILE, tk=NSA_WIN_TILE, cw=ITEM_LANES, window=NSA_WINDOW,
                   gates=gates5, gate_branch=2, name="nsa_window")
    return [o_cmp, o_slc, o_win], zs


def _swa_mixer(h, cos, sin, w_in, q_gain, k_gain, sinks):
    r = N_HEADS // SWA_KV_HEADS
    qT, k, vT, zs = _swa_proj(h, cos, sin, w_in, q_gain, k_gain, win_tile=SWA_TILE)
    o = _flash(qT, k, vT, mode="window", n_par=1, n_rep=r, dq=HEAD_DIM, kc=HEAD_DIM, tq=SWA_TILE,
               tk=SWA_TILE, cw=r * SWA_TILE, window=SWA_WINDOW, sinks=sinks, name="swa_window")
    return [o], zs


def _fox_mixer(h, w_in, forget_bias, q_gain, k_gain):
    b, s, _ = h.shape
    wf = w_in[:, 3 * MIX_WIDTH:3 * MIX_WIDTH + N_HEADS].T.astype(BF)
    cum = _forget_cum(h, wf, forget_bias)
    qT, k, vT, zs = _fox_proj(h, cum, w_in, q_gain, k_gain, tile=FOX_K_TILE)
    o = _flash(qT.reshape(b, N_HEADS * KEY_PAD, s), k, vT, mode="causal", n_par=FOX_HEADS_PER_STEP, n_rep=1, dq=KEY_PAD,
               kc=KEY_PAD, tq=FOX_Q_TILE, tk=FOX_K_TILE, cw=ITEM_LANES, name="fox_attention")
    return [o], zs


def kernel(x, positions, norm_gains, a_w_in, a_q_gain, a_k_gain, a_cmp_pos, a_cmp_w1, a_cmp_w2, a_w_out,
           b_w_in, b_q_gain, b_k_gain, b_sinks, b_w_out,
           c_w_in, c_forget_bias, c_q_gain, c_k_gain, c_w_out):
    depth = norm_gains.shape[0]
    cos, sin = _rope_tables(positions)
    h = _prenorm(x, norm_gains[0])
    for i in range(depth):
        j, mixer = divmod(i, 3)
        if mixer == 0:
            o_list, zs = _nsa_mixer(h, cos, sin, a_w_in[j], a_q_gain[j], a_k_gain[j],
                                    a_cmp_pos[j], a_cmp_w1[j], a_cmp_w2[j])
            w_out = a_w_out[j]
        elif mixer == 1:
            o_list, zs = _swa_mixer(h, cos, sin, b_w_in[j], b_q_gain[j], b_k_gain[j], b_sinks[j])
            w_out = b_w_out[j]
        else:
            o_list, zs = _fox_mixer(h, c_w_in[j], c_forget_bias[j], c_q_gain[j], c_k_gain[j])
            w_out = c_w_out[j]
        next_gain = norm_gains[i + 1] if i + 1 < depth else None
        x, h = _out_proj(o_list, zs, x, w_out, next_gain)
    return x
```

```python
import functools

import jax
import jax.numpy as jnp
import numpy as np
from jax import lax
from jax.experimental import pallas as pl
from jax.experimental.pallas import tpu as pltpu

D_MODEL = 1024
HEAD_DIM = 64
N_HEADS = 16
MIX_WIDTH = N_HEADS * HEAD_DIM
ROPE_THETA = 10000.0
EPS = 1e-6
SCALE = HEAD_DIM ** -0.5
NEG_INF = -1e30
BIG = 1e30
M_INIT = -1e29
SEL_OFF = -(2.0 ** 100)

NSA_KV_HEADS = 4
NSA_CMP_LEN = 32
NSA_CMP_STRIDE = 16
NSA_SLC_LEN = 64
NSA_TOPK = 16
NSA_WINDOW = 512
SWA_KV_HEADS = 2
SWA_WINDOW = 128

LOG2E = float(np.log2(np.e))
Q_SCALE = SCALE * LOG2E

LANES = 128
KEY_PAD = 128
SLC_KEY_LANES = 256
V_ROWS = 80
FLASH_UNROLL = 4
WINDOW_LOOKAHEAD = 2
WINDOW_SCORE_BUFS = 4
CMP_BLOCK = 128
CMP_MASK_ROWS = 2 * CMP_BLOCK
VMEM_LIMIT = 56 * 1024 * 1024

TOK_TILE = 512
NT_DIMS = (((1,), (1,)), ((), ()))

BF = jnp.bfloat16
F32 = jnp.float32


def _params(sem):
    return pltpu.CompilerParams(dimension_semantics=sem, vmem_limit_bytes=VMEM_LIMIT)


def _dot(a, b):
    return jnp.dot(a, b, preferred_element_type=F32)


def _dot_nt(a, b):
    return lax.dot_general(a, b, NT_DIMS, preferred_element_type=F32)


def _rope_tab_kernel(pos_ref, invf_ref, cos_ref, sin_ref):
    ang = invf_ref[...] * pos_ref[0].astype(F32)
    cos_ref[0] = jnp.cos(ang)
    sin_ref[0] = jnp.sin(ang)


def _rope_tables(positions):
    b, s = positions.shape
    half = HEAD_DIM // 2
    inv_freq = ROPE_THETA ** (-jnp.arange(half, dtype=F32) * 2.0 / HEAD_DIM)
    tm = min(TOK_TILE, s)
    out = jax.ShapeDtypeStruct((b, half, s), F32)
    return pl.pallas_call(
        _rope_tab_kernel,
        grid=(b, s // tm),
        in_specs=[pl.BlockSpec((1, 1, tm), lambda i, j: (i, 0, j)),
                  pl.BlockSpec((half, 1), lambda i, j: (0, 0))],
        out_specs=[pl.BlockSpec((1, half, tm), lambda i, j: (i, 0, j))] * 2,
        out_shape=[out, out],
        compiler_params=_params(("parallel", "parallel")),
        name="rope_tables",
    )(positions.reshape(b, 1, s), inv_freq.reshape(half, 1))


def _rms_rows(x, gain_row):
    y = x * lax.rsqrt(jnp.mean(x * x, axis=-1, keepdims=True) + EPS)
    return y * gain_row


def _prenorm_kernel(x_ref, g_ref, h_ref):
    h_ref[0] = _rms_rows(x_ref[0], g_ref[...]).astype(BF)


def _prenorm(x, gain):
    b, s, d = x.shape
    tm = min(TOK_TILE, s)
    return pl.pallas_call(
        _prenorm_kernel,
        grid=(b, s // tm),
        in_specs=[pl.BlockSpec((1, tm, d), lambda i, j: (i, j, 0)),
                  pl.BlockSpec((1, d), lambda i, j: (0, 0))],
        out_specs=pl.BlockSpec((1, tm, d), lambda i, j: (i, j, 0)),
        out_shape=jax.ShapeDtypeStruct((b, s, d), BF),
        compiler_params=_params(("parallel", "parallel")),
        name="prenorm",
    )(x, gain.reshape(1, d))


def _head_rms_t(y, gain_col):
    ms = jnp.mean(y * y, axis=0, keepdims=True)
    return (y * lax.rsqrt(ms + EPS)) * gain_col


def _rope_t(y, cos, sin):
    half = HEAD_DIM // 2
    x1, x2 = y[:half], y[half:]
    return jnp.concatenate([x1 * cos - x2 * sin, x2 * cos + x1 * sin], axis=0)


def _to_token_major(y):
    pad = jnp.zeros((KEY_PAD - y.shape[0], y.shape[1]), y.dtype)
    return jnp.concatenate([y, pad], axis=0).T


def _silu(z):
    return z * (1.0 / (1.0 + jnp.exp(-z)))


def _sigmoid(z):
    return 1.0 / (1.0 + jnp.exp(-z))


def _store_lane_tiles(out_ref, y, tile):
    for c in range(y.shape[1] // tile):
        out_ref[0, c] = y[:, c * tile:(c + 1) * tile]


def _nsa_proj_kernel(h_ref, cos_ref, sin_ref, wq_ref, wk_ref, wv_ref, wg_ref, wz_ref,
                     qg_ref, kg_ref,
                     q_out, kc_out, vc_out, ks_out, kw_out, vs_out, vw_out, g_out, z_out,
                     y_ref, *, slc_tile, win_tile):
    h = h_ref[0]
    cos, sin = cos_ref[0], sin_ref[0]
    g = NSA_KV_HEADS
    kvd = g * HEAD_DIM
    y_ref[...] = _dot_nt(wq_ref[...], h)
    for hd in range(N_HEADS):
        rows = slice(hd * HEAD_DIM, (hd + 1) * HEAD_DIM)
        y = _rope_t(_head_rms_t(y_ref[rows, :], qg_ref[...]), cos, sin) * Q_SCALE
        q_out[0, rows, :] = y.astype(BF)
    y_ref[0:3 * kvd, :] = _dot_nt(wk_ref[...], h)
    tm = h.shape[0]
    tok = pl.program_id(1) * tm + lax.broadcasted_iota(jnp.int32, (tm, 1), 0)
    blk_lane = HEAD_DIM + lax.shift_right_logical(tok, int(np.log2(NSA_SLC_LEN)))
    blk_hot = lax.broadcasted_iota(jnp.int32, (1, SLC_KEY_LANES), 1) == blk_lane
    for kind in range(3):
        for gi in range(g):
            r0 = (kind * g + gi) * HEAD_DIM
            y = y_ref[r0:r0 + HEAD_DIM, :]
            if kind > 0:
                y = _head_rms_t(y, kg_ref[:, kind:kind + 1])
            yt = _to_token_major(_rope_t(y, cos, sin))
            if kind == 0:
                kc_out[0, gi] = yt[:, :HEAD_DIM]
            elif kind == 1:
                wide = jnp.concatenate([yt, jnp.zeros((tm, SLC_KEY_LANES - KEY_PAD), F32)], axis=1)
                ks_out[0, gi] = jnp.where(blk_hot, 1.0, wide).astype(BF)
            else:
                kw_out[0, gi] = yt.astype(BF)
    y_ref[0:3 * kvd, :] = _dot_nt(wv_ref[...], h)
    for gi in range(g):
        r0 = gi * HEAD_DIM
        vc_out[0, gi] = _to_token_major(y_ref[r0:r0 + HEAD_DIM, :])[:, :HEAD_DIM]
    _store_lane_tiles(vs_out, y_ref[kvd:2 * kvd, :].astype(BF), slc_tile)
    _store_lane_tiles(vw_out, y_ref[2 * kvd:3 * kvd, :].astype(BF), win_tile)
    g_out[0] = _sigmoid(_dot_nt(wg_ref[...], h))
    z_out[0] = _silu(_dot(h, wz_ref[...])).astype(BF)


def _full(shape):
    nd = len(shape)
    return pl.BlockSpec(shape, lambda i, j, _n=nd: (0,) * _n)


def _nsa_proj(h, cos, sin, w_in, q_gain, k_gain, *, slc_tile, win_tile):
    b, s, d = h.shape
    g = NSA_KV_HEADS
    kvd = g * HEAD_DIM
    tm = min(TOK_TILE, s)
    sizes = [MIX_WIDTH] + [kvd] * 6 + [3 * N_HEADS]
    off = np.cumsum([0] + sizes)
    wt = w_in.T.astype(BF)
    wq = wt[off[0]:off[1]]
    wk = jnp.concatenate([wt[off[1]:off[2]], wt[off[3]:off[4]], wt[off[5]:off[6]]], axis=0)
    wv = jnp.concatenate([wt[off[2]:off[3]], wt[off[4]:off[5]], wt[off[6]:off[7]]], axis=0)
    wg = wt[off[7]:off[8]]
    wz = w_in[:, off[8]:].astype(BF)
    qg = q_gain.reshape(HEAD_DIM, 1)
    kg = k_gain.T
    n_t = s // tm
    out_shape = [
        jax.ShapeDtypeStruct((b, MIX_WIDTH, s), BF),
        jax.ShapeDtypeStruct((b, g, s, HEAD_DIM), F32),
        jax.ShapeDtypeStruct((b, g, s, HEAD_DIM), F32),
        jax.ShapeDtypeStruct((b, g, s, SLC_KEY_LANES), BF),
        jax.ShapeDtypeStruct((b, g, s, KEY_PAD), BF),
        jax.ShapeDtypeStruct((b, s // slc_tile, kvd, slc_tile), BF),
        jax.ShapeDtypeStruct((b, s // win_tile, kvd, win_tile), BF),
        jax.ShapeDtypeStruct((b, 3 * N_HEADS, s), F32),
        jax.ShapeDtypeStruct((b, s, MIX_WIDTH), BF),
    ]
    out_specs = [
        pl.BlockSpec((1, MIX_WIDTH, tm), lambda i, j: (i, 0, j)),
        pl.BlockSpec((1, g, tm, HEAD_DIM), lambda i, j: (i, 0, j, 0)),
        pl.BlockSpec((1, g, tm, HEAD_DIM), lambda i, j: (i, 0, j, 0)),
        pl.BlockSpec((1, g, tm, SLC_KEY_LANES), lambda i, j: (i, 0, j, 0)),
        pl.BlockSpec((1, g, tm, KEY_PAD), lambda i, j: (i, 0, j, 0)),
        pl.BlockSpec((1, tm // slc_tile, kvd, slc_tile), lambda i, j: (i, j, 0, 0)),
        pl.BlockSpec((1, tm // win_tile, kvd, win_tile), lambda i, j: (i, j, 0, 0)),
        pl.BlockSpec((1, 3 * N_HEADS, tm), lambda i, j: (i, 0, j)),
        pl.BlockSpec((1, tm, MIX_WIDTH), lambda i, j: (i, j, 0)),
    ]
    in_specs = [
        pl.BlockSpec((1, tm, d), lambda i, j: (i, j, 0)),
        pl.BlockSpec((1, HEAD_DIM // 2, tm), lambda i, j: (i, 0, j)),
        pl.BlockSpec((1, HEAD_DIM // 2, tm), lambda i, j: (i, 0, j)),
        _full(wq.shape), _full(wk.shape), _full(wv.shape), _full(wg.shape), _full(wz.shape),
        _full(qg.shape), _full(kg.shape),
    ]
    return pl.pallas_call(
        functools.partial(_nsa_proj_kernel, slc_tile=slc_tile, win_tile=win_tile),
        grid=(b, n_t), in_specs=in_specs, out_specs=out_specs, out_shape=out_shape,
        scratch_shapes=[pltpu.VMEM((MIX_WIDTH, tm), F32)],
        compiler_params=_params(("parallel", "parallel")),
        name="nsa_proj",
    )(h, cos, sin, wq, wk, wv, wg, wz, qg, kg)


def _gelu_tanh(x):
    c = np.float32(np.sqrt(2.0 / np.pi))
    return 0.5 * x * (1.0 + jnp.tanh(c * (x + 0.044715 * (x * x * x))))


def _compress_kernel(kc_ref, vc_ref, pos_ref, w1_ref, w2_ref, kg_ref, kcmp_out, vcmp_out):
    for which, (src, dst) in enumerate(((kc_ref, kcmp_out), (vc_ref, vcmp_out))):
        x = src[0, 0]
        n = x.shape[0]
        half = x.shape[1]
        xa = (x + pos_ref[which, 0:1, :]).astype(BF)
        xb = (x + pos_ref[which, 1:2, :]).astype(BF)
        ua = _dot(xa, w1_ref[which, :half, :])
        ub = _dot(xb, w1_ref[which, half:, :])
        row = lax.broadcasted_iota(jnp.int32, (n, 1), 0)
        ub_next = jnp.where(row == n - 1, 0.0, pltpu.roll(ub, n - 1, 0))
        hid = _gelu_tanh(ua + ub_next)
        y = _dot(hid.astype(BF), w2_ref[which])
        if which == 0:
            y = _rms_rows(y, kg_ref[...])
            dst[0, 0] = y.astype(BF)
        else:
            pad = jnp.zeros((n, KEY_PAD - HEAD_DIM), F32)
            dst[0, 0] = jnp.concatenate([y, pad], axis=1).T[:HEAD_DIM].astype(BF)


def _compress(kc_tok, vc_tok, cmp_pos, cmp_w1, cmp_w2, k_gain0):
    b, g, s, _ = kc_tok.shape
    n_chunk = s // NSA_CMP_STRIDE
    flat = NSA_CMP_STRIDE * HEAD_DIM
    kc = kc_tok.reshape(b, g, n_chunk, flat)
    vc = vc_tok.reshape(b, g, n_chunk, flat)
    pos = cmp_pos.reshape(2, 2, flat)
    w1 = cmp_w1.astype(BF)
    w2 = cmp_w2.astype(BF)
    blk = pl.BlockSpec((1, 1, n_chunk, flat), lambda i, j: (i, j, 0, 0))
    return pl.pallas_call(
        _compress_kernel,
        grid=(b, g),
        in_specs=[blk, blk, _full(pos.shape), _full(w1.shape), _full(w2.shape),
                  _full((1, HEAD_DIM))],
        out_specs=[pl.BlockSpec((1, 1, n_chunk, HEAD_DIM), lambda i, j: (i, j, 0, 0)),
                   pl.BlockSpec((1, 1, HEAD_DIM, n_chunk), lambda i, j: (i, j, 0, 0))],
        out_shape=[jax.ShapeDtypeStruct((b, g, n_chunk, HEAD_DIM), BF),
                   jax.ShapeDtypeStruct((b, g, HEAD_DIM, n_chunk), BF)],
        compiler_params=_params(("parallel", "parallel")),
        name="nsa_compress",
    )(kc, vc, pos, w1, w2, k_gain0.reshape(1, HEAD_DIM))


def _stack_heads(q_ref, n_heads, dk):
    return jnp.concatenate([q_ref[0, r * dk:(r + 1) * dk, :] for r in range(n_heads)], axis=1)


def _unstack_store(o, out_ref, n_heads, tq):
    rows = jnp.concatenate([o[:, r * tq:(r + 1) * tq] for r in range(n_heads)], axis=0)
    out_ref[0] = rows.T.astype(out_ref.dtype)


def _gate_row(gate_ref, n_heads, grp=0):
    return jnp.concatenate([gate_ref[0, 0, grp, r:r + 1, :] for r in range(n_heads)], axis=1)


def _cmp_branch(rows, q_ref, kc_ref, vc_ref, ov_ref, gate_ref, o_out, sel_out, *, tq, n_blk):
    r_heads = N_HEADS // NSA_KV_HEADS
    nq = r_heads * tq
    q0 = pl.program_id(2) * tq
    n_cmp = kc_ref.shape[2] - 1
    q4 = _stack_heads(q_ref, r_heads, HEAD_DIM)
    s = _dot(kc_ref[0, 0, 0:rows, :], q4)
    t_row = q0 + (lax.broadcasted_iota(jnp.int32, (1, nq), 1) & (tq - 1))
    lo = max(rows - CMP_MASK_ROWS, 0)
    c_col = lo + lax.broadcasted_iota(jnp.int32, (rows - lo, 1), 0)
    valid = (c_col * NSA_CMP_STRIDE + (NSA_CMP_LEN - 1) <= t_row) & (c_col < n_cmp)
    s_new = jnp.where(valid, s[lo:], NEG_INF)
    m = jnp.max(s_new, axis=0, keepdims=True)
    if lo:
        m = jnp.maximum(m, jnp.max(s[:lo], axis=0, keepdims=True))
    e = jnp.where(valid, jnp.exp2(s_new - m), 0.0).astype(BF)
    if lo:
        e = jnp.concatenate([jnp.exp2(s[:lo] - m).astype(BF), e], axis=0)
    one_row = jnp.where(lax.broadcasted_iota(jnp.int32, (V_ROWS - HEAD_DIM, rows), 0) == 0, 1.0, 0.0).astype(BF)
    n_live = min(n_blk, rows * NSA_CMP_STRIDE // NSA_SLC_LEN + 8)
    lhs = jnp.concatenate([vc_ref[0, 0, :, 0:rows], one_row, ov_ref[0:n_live, 0:rows]], axis=0)
    res = _dot(lhs, e)
    l = res[HEAD_DIM:HEAD_DIM + 1]
    inv = jnp.where(l > 0.0, 1.0 / jnp.where(l > 0.0, l, 1.0), 0.0)
    o = res[:HEAD_DIM] * (inv * _gate_row(gate_ref, r_heads))
    _unstack_store(o, o_out, r_heads, tq)
    w = res[V_ROWS:V_ROWS + n_live] * inv
    imp = w[:, 0:tq]
    for r in range(1, r_heads):
        imp = imp + w[:, r * tq:(r + 1) * tq]
    t1 = q0 + lax.broadcasted_iota(jnp.int32, (1, tq), 1)
    cur = lax.shift_right_logical(t1, int(np.log2(NSA_SLC_LEN)))
    blk = lax.broadcasted_iota(jnp.int32, (n_live, tq), 0)
    forced = (blk == 0) | (blk == cur) | (blk == cur - 1)
    imp = jnp.where(forced, BIG, jnp.where(blk > cur, NEG_INF, imp))
    for _ in range(min(NSA_TOPK, n_blk)):
        best = jnp.max(imp, axis=0, keepdims=True)
        first = jnp.min(jnp.where(imp == best, blk, n_blk), axis=0, keepdims=True)
        imp = jnp.where(blk == first, -jnp.inf, imp)
    sel_out[0, 0, 0:n_live, :] = jnp.where(imp == -jnp.inf, 0.0, SEL_OFF).astype(BF)
    if n_live < n_blk:
        sel_out[0, 0, n_live:n_blk, :] = jnp.full((n_blk - n_live, tq), SEL_OFF, BF)


def _cmp_select_kernel(q_ref, kc_ref, vc_ref, ov_ref, gate_ref, o_out, sel_out, *, tq, n_blk):
    q0 = pl.program_id(2) * tq
    n_chunk = kc_ref.shape[2]
    n_need = jnp.minimum((q0 + tq - NSA_CMP_LEN) // NSA_CMP_STRIDE + 1, n_chunk - 1)
    n_steps = n_chunk // CMP_BLOCK
    need_steps = (n_need + CMP_BLOCK - 1) // CMP_BLOCK
    for k in range(1, n_steps + 1):
        @pl.when(need_steps == k)
        def _(k=k):
            _cmp_branch(k * CMP_BLOCK, q_ref, kc_ref, vc_ref, ov_ref, gate_ref, o_out, sel_out, tq=tq, n_blk=n_blk)


def _overlap_matrix(s):
    n_chunk = s // NSA_CMP_STRIDE
    n_blk = s // NSA_SLC_LEN
    c0 = np.arange(n_chunk) * NSA_CMP_STRIDE
    c1 = c0 + NSA_CMP_LEN - 1
    b0 = np.arange(n_blk) * NSA_SLC_LEN
    ov = np.minimum(c1[None, :], b0[:, None] + NSA_SLC_LEN - 1) - np.maximum(c0[None, :], b0[:, None]) + 1
    return jnp.asarray(np.clip(ov, 0, None) / NSA_CMP_LEN, BF)


def _cmp_select(qT, k_cmp, v_cmpT, gates5, *, tq):
    b, _, s = qT.shape
    g = NSA_KV_HEADS
    r_heads = N_HEADS // g
    n_chunk = k_cmp.shape[2]
    n_blk = s // NSA_SLC_LEN
    ov = _overlap_matrix(s)
    rows = r_heads * HEAD_DIM
    return pl.pallas_call(
        functools.partial(_cmp_select_kernel, tq=tq, n_blk=n_blk),
        grid=(b, g, s // tq),
        in_specs=[
            pl.BlockSpec((1, rows, tq), lambda i, j, k: (i, j, k)),
            pl.BlockSpec((1, 1, n_chunk, HEAD_DIM), lambda i, j, k: (i, j, 0, 0)),
            pl.BlockSpec((1, 1, HEAD_DIM, n_chunk), lambda i, j, k: (i, j, 0, 0)),
            pl.BlockSpec((n_blk, n_chunk), lambda i, j, k: (0, 0)),
            pl.BlockSpec((1, 1, 1, r_heads, tq), lambda i, j, k: (i, 0, j, 0, k)),
        ],
        out_specs=[pl.BlockSpec((1, tq, rows), lambda i, j, k: (i, k, j)),
                   pl.BlockSpec((1, 1, n_blk, tq), lambda i, j, k: (i, j, 0, k))],
        out_shape=[jax.ShapeDtypeStruct((b, s, MIX_WIDTH), BF),
                   jax.ShapeDtypeStruct((b, g, n_blk, s), BF)],
        compiler_params=_params(("parallel", "parallel", "parallel")),
        name="nsa_cmp_select",
    )(qT, k_cmp, v_cmpT, ov, gates5)


def _flash_kernel(*refs, mode, n_par, n_rep, dq, kc, tq, tk, cw, window, n_blk, has_sink, has_gate):
    it = iter(refs)
    q_ref, k_ref, v_ref = next(it), next(it), next(it)
    sel_ref = next(it) if n_blk else None
    sink_ref = next(it) if has_sink else None
    gate_ref = next(it) if has_gate else None
    out_ref = next(it)
    qs_ref, m_ref, acc_ref, mt_ref = (next(it) for _ in range(4))
    s_bufs = tuple(it)

    nq = n_rep * tq
    items = [(p, c) for p in range(n_par) for c in range(nq // cw)]
    n_items = len(items)
    grp = pl.program_id(1)
    q0 = pl.program_id(2) * tq
    acc_row = lax.broadcasted_iota(jnp.int32, (V_ROWS, nq), 0)
    for p in range(n_par):
        for r in range(n_rep):
            hd = p * n_rep + r
            qs_ref[p, 0:dq, r * tq:(r + 1) * tq] = q_ref[0, hd * dq:(hd + 1) * dq, :]
        if n_blk:
            qs_ref[p, dq:dq + n_blk, :] = jnp.concatenate([sel_ref[0, 0]] * n_rep, axis=1)
            if dq + n_blk < kc:
                qs_ref[p, dq + n_blk:kc, :] = jnp.zeros((kc - dq - n_blk, nq), BF)
        if has_sink:
            m_ref[p] = jnp.concatenate(
                [jnp.full((1, tq), sink_ref[(grp * n_par + p) * n_rep + r] * LOG2E, F32) for r in range(n_rep)],
                axis=1)
            acc_ref[p] = jnp.where(acc_row == HEAD_DIM, 1.0, 0.0)
        else:
            m_ref[p] = jnp.full((1, nq), M_INIT, F32)
            acc_ref[p] = jnp.zeros((V_ROWS, nq), F32)

    t_row = q0 + (lax.broadcasted_iota(jnp.int32, (1, nq), 1) & (tq - 1))
    one_row = jnp.where(lax.broadcasted_iota(jnp.int32, (V_ROWS - HEAD_DIM, tk), 0) == 0, 1.0, 0.0).astype(BF)

    def stage_a(item, j, key0, slot, kind):
        p, c = item
        cols = slice(c * cw, (c + 1) * cw)
        s = _dot(k_ref[0, p, j][:, :kc], qs_ref[p, :, cols])
        if kind is not None:
            key = key0 + lax.broadcasted_iota(jnp.int32, (tk, 1), 0)
            ok = key <= t_row[:, cols] if kind == "causal" else key > t_row[:, cols] - window
            s = jnp.where(ok, s, NEG_INF)
        s_bufs[slot][...] = s
        mt_ref[slot] = jnp.max(s, axis=0, keepdims=True)

    def stage_b(item, j, slot):
        p, c = item
        cols = slice(c * cw, (c + 1) * cw)
        m_old = m_ref[p, :, cols]
        m_new = jnp.maximum(m_old, mt_ref[slot])
        pr = jnp.exp2(s_bufs[slot][...] - m_new).astype(BF)
        alpha = jnp.exp2(m_old - m_new)
        v = jnp.concatenate([v_ref[0, j, p * HEAD_DIM:(p + 1) * HEAD_DIM, :], one_row], axis=0)
        acc_ref[p, :, cols] = alpha * acc_ref[p, :, cols] + _dot(v, pr)
        m_ref[p, :, cols] = m_new

    if mode == "causal":
        assert n_items % 2 == 0
        n_full = q0 // tk

        def step(j, kind, next_kind, last):
            for idx, item in enumerate(items):
                slot = idx % 2
                if idx + 1 < n_items:
                    stage_a(items[idx + 1], j, j * tk, 1 - slot, kind)
                elif not last:
                    stage_a(items[0], j + 1, (j + 1) * tk, 1 - slot, next_kind)
                stage_b(item, j, slot)

        stage_a(items[0], 0, 0, 0, "causal")

        def body(j, carry):
            step(j, None, None, False)
            return carry

        def body_group(i, carry):
            for u in range(FLASH_UNROLL):
                step(FLASH_UNROLL * i + u, None, None, False)
            return carry

        n_main = jnp.maximum(n_full - 1, 0)
        n_groups = lax.shift_right_logical(n_main, int(np.log2(FLASH_UNROLL)))
        lax.fori_loop(0, n_groups, body_group, 0)
        lax.fori_loop(FLASH_UNROLL * n_groups, n_main, body, 0)

        @pl.when(n_full >= 1)
        def _():
            step(n_full - 1, None, "causal", False)

        step(n_full, "causal", None, True)
    else:
        w_tiles, q_tiles = window // tk, tq // tk
        work = []
        for i in range(w_tiles + q_tiles):
            jv = q0 // tk - w_tiles + i
            key0 = jnp.where(jv < 0, -(1 << 30), jv * tk)
            kind = "causal" if i >= w_tiles else "lower"
            work += [(item, jnp.maximum(jv, 0), key0, kind) for item in items]
        n_buf = len(s_bufs)
        ahead = WINDOW_LOOKAHEAD
        for n in range(ahead):
            stage_a(work[n][0], work[n][1], work[n][2], n % n_buf, work[n][3])
        for n, (item, j, _, _) in enumerate(work):
            if n + ahead < len(work):
                nxt = work[n + ahead]
                stage_a(nxt[0], nxt[1], nxt[2], (n + ahead) % n_buf, nxt[3])
            stage_b(item, j, n % n_buf)

    outs = []
    for p in range(n_par):
        acc = acc_ref[p]
        o = acc[:HEAD_DIM] * (1.0 / acc[HEAD_DIM:HEAD_DIM + 1])
        if has_gate:
            o = o * _gate_row(gate_ref, n_rep, p)
        outs += [o[:, r * tq:(r + 1) * tq] for r in range(n_rep)]
    out_ref[0] = jnp.concatenate(outs, axis=0).T.astype(out_ref.dtype)


def _flash(qT, k_tok, vT_tiles, *, mode, n_par, n_rep, dq, kc, tq, tk, cw, window=None,
           sel_bias=None, sinks=None, gates=None, gate_branch=0, name="flash"):
    b, _, s = qT.shape
    kh, k_lanes = k_tok.shape[1], k_tok.shape[3]
    n_t = s // tk
    k5 = k_tok.reshape(b, kh, n_t, tk, k_lanes)
    n_grp = kh // n_par
    heads = n_par * n_rep
    nq = n_rep * tq
    n_blk = sel_bias.shape[2] if sel_bias is not None else 0
    has_sink, has_gate = sinks is not None, gates is not None
    args = [qT, k5, vT_tiles]
    in_specs = [
        pl.BlockSpec((1, heads * dq, tq), lambda i, j, k: (i, j, k)),
        pl.BlockSpec((1, n_par, n_t, tk, k_lanes), lambda i, j, k: (i, j, 0, 0, 0)),
        pl.BlockSpec((1, n_t, n_par * HEAD_DIM, tk), lambda i, j, k: (i, 0, j, 0)),
    ]
    if n_blk:
        assert dq + n_blk <= kc
        args.append(sel_bias)
        in_specs.append(pl.BlockSpec((1, 1, n_blk, tq), lambda i, j, k: (i, j, 0, k)))
    if has_sink:
        args.append(sinks.astype(F32))
        in_specs.append(pl.BlockSpec(memory_space=pltpu.SMEM))
    if has_gate:
        args.append(gates)
        in_specs.append(pl.BlockSpec((1, 1, n_par, n_rep, tq),
                                     lambda i, j, k, _br=gate_branch: (i, _br, j, 0, k)))
    n_sbuf = 2 if mode == "causal" else WINDOW_SCORE_BUFS
    scratch = [pltpu.VMEM((n_par, kc, nq), BF), pltpu.VMEM((n_par, 1, nq), F32),
               pltpu.VMEM((n_par, V_ROWS, nq), F32), pltpu.VMEM((n_sbuf, 1, cw), F32)]
    scratch += [pltpu.VMEM((tk, cw), F32)] * n_sbuf
    kern = functools.partial(_flash_kernel, mode=mode, n_par=n_par, n_rep=n_rep, dq=dq, kc=kc, tq=tq, tk=tk,
                             cw=cw, window=window, n_blk=n_blk, has_sink=has_sink, has_gate=has_gate)
    return pl.pallas_call(
        kern,
        grid=(b, n_grp, s // tq),
        in_specs=in_specs,
        out_specs=pl.BlockSpec((1, tq, heads * HEAD_DIM), lambda i, j, k: (i, k, j)),
        out_shape=jax.ShapeDtypeStruct((b, s, MIX_WIDTH), BF),
        scratch_shapes=scratch,
        compiler_params=_params(("parallel", "parallel", "arbitrary")),
        name=name,
    )(*args)


def _swa_proj_kernel(h_ref, cos_ref, sin_ref, wq_ref, wk_ref, wv_ref, wz_ref, qg_ref, kg_ref,
                     q_out, k_out, v_out, z_out, y_ref, *, win_tile):
    h = h_ref[0]
    cos, sin = cos_ref[0], sin_ref[0]
    y_ref[...] = _dot_nt(wq_ref[...], h)
    for hd in range(N_HEADS):
        rows = slice(hd * HEAD_DIM, (hd + 1) * HEAD_DIM)
        q_out[0, rows, :] = (_rope_t(_head_rms_t(y_ref[rows, :], qg_ref[...]), cos, sin) * Q_SCALE).astype(BF)
    kvd = SWA_KV_HEADS * HEAD_DIM
    y_ref[0:kvd, :] = _dot_nt(wk_ref[...], h)
    for gi in range(SWA_KV_HEADS):
        rows = slice(gi * HEAD_DIM, (gi + 1) * HEAD_DIM)
        k_out[0, gi] = _to_token_major(_rope_t(_head_rms_t(y_ref[rows, :], kg_ref[...]), cos, sin)).astype(BF)
    _store_lane_tiles(v_out, _dot_nt(wv_ref[...], h).astype(BF), win_tile)
    z_out[0] = _silu(_dot(h, wz_ref[...])).astype(BF)


def _swa_proj(h, cos, sin, w_in, q_gain, k_gain, *, win_tile):
    b, s, d = h.shape
    g = SWA_KV_HEADS
    kvd = g * HEAD_DIM
    tm = min(TOK_TILE, s)
    wt = w_in.T.astype(BF)
    wq, wk, wv = wt[:MIX_WIDTH], wt[MIX_WIDTH:MIX_WIDTH + kvd], wt[MIX_WIDTH + kvd:MIX_WIDTH + 2 * kvd]
    wz = w_in[:, MIX_WIDTH + 2 * kvd:].astype(BF)
    qg, kg = q_gain.reshape(HEAD_DIM, 1), k_gain.reshape(HEAD_DIM, 1)
    wpt = tm // win_tile
    half = HEAD_DIM // 2
    return pl.pallas_call(
        functools.partial(_swa_proj_kernel, win_tile=win_tile),
        grid=(b, s // tm),
        in_specs=[pl.BlockSpec((1, tm, d), lambda i, j: (i, j, 0)),
                  pl.BlockSpec((1, half, tm), lambda i, j: (i, 0, j)),
                  pl.BlockSpec((1, half, tm), lambda i, j: (i, 0, j)),
                  _full(wq.shape), _full(wk.shape), _full(wv.shape), _full(wz.shape),
                  _full(qg.shape), _full(kg.shape)],
        out_specs=[pl.BlockSpec((1, MIX_WIDTH, tm), lambda i, j: (i, 0, j)),
                   pl.BlockSpec((1, g, tm, KEY_PAD), lambda i, j: (i, 0, j, 0)),
                   pl.BlockSpec((1, wpt, kvd, win_tile), lambda i, j: (i, j, 0, 0)),
                   pl.BlockSpec((1, tm, MIX_WIDTH), lambda i, j: (i, j, 0))],
        out_shape=[jax.ShapeDtypeStruct((b, MIX_WIDTH, s), BF),
                   jax.ShapeDtypeStruct((b, g, s, KEY_PAD), BF),
                   jax.ShapeDtypeStruct((b, s // win_tile, kvd, win_tile), BF),
                   jax.ShapeDtypeStruct((b, s, MIX_WIDTH), BF)],
        scratch_shapes=[pltpu.VMEM((MIX_WIDTH, tm), F32)],
        compiler_params=_params(("parallel", "parallel")),
        name="swa_proj",
    )(h, cos, sin, wq, wk, wv, wz, qg, kg)


def _split3(x):
    hi = x.astype(BF)
    r1 = x - hi.astype(F32)
    mid = r1.astype(BF)
    lo = (r1 - mid.astype(F32)).astype(BF)
    return hi, mid, lo


def _forget_cum_kernel(h_ref, wf_ref, bias_ref, tri_ref, cum_out, carry_ref):
    @pl.when(pl.program_id(1) == 0)
    def _():
        carry_ref[...] = jnp.zeros_like(carry_ref)

    x = _dot_nt(wf_ref[...], h_ref[0]) + bias_ref[...]
    logf = jnp.minimum(x, 0.0) - jnp.log(1.0 + jnp.exp(-jnp.abs(x)))
    tri = tri_ref[...]
    hi, mid, lo = _split3(logf)
    cum = (_dot(hi, tri) + _dot(mid, tri)) + _dot(lo, tri) + carry_ref[:, 0:1]
    cum_out[0] = cum
    carry_ref[...] = jnp.broadcast_to(cum[:, -1:], carry_ref.shape)


def _forget_cum(h, wf, bias):
    b, s, d = h.shape
    tm = min(TOK_TILE, s)
    tri = jnp.asarray(np.arange(tm)[:, None] <= np.arange(tm)[None, :], BF)
    return pl.pallas_call(
        _forget_cum_kernel,
        grid=(b, s // tm),
        in_specs=[pl.BlockSpec((1, tm, d), lambda i, j: (i, j, 0)),
                  _full(wf.shape), _full((N_HEADS, 1)), _full(tri.shape)],
        out_specs=pl.BlockSpec((1, N_HEADS, tm), lambda i, j: (i, 0, j)),
        out_shape=jax.ShapeDtypeStruct((b, N_HEADS, s), F32),
        scratch_shapes=[pltpu.VMEM((N_HEADS, LANES), F32)],
        compiler_params=_params(("parallel", "arbitrary")),
        name="fox_forget_cum",
    )(h, wf, bias.reshape(N_HEADS, 1).astype(F32), tri)


def _fox_proj_kernel(h_ref, cum_ref, wq_ref, wk_ref, wv_ref, wz_ref, qg_ref, kg_ref,
                     q_out, k_out, v_out, z_out, yq_ref, yk_ref, *, tile):
    h = h_ref[0]
    tm = h.shape[0]
    yq_ref[...] = _dot_nt(wq_ref[...], h)
    yk_ref[...] = _dot_nt(wk_ref[...], h)
    row = lax.broadcasted_iota(jnp.int32, (8, tm), 0)
    zeros = jnp.zeros((KEY_PAD - HEAD_DIM - 16, tm), F32)
    for hd in range(N_HEADS):
        rows = slice(hd * HEAD_DIM, (hd + 1) * HEAD_DIM)
        c_hi, c_mid, c_lo = (c.astype(F32) for c in _split3(cum_ref[0, hd:hd + 1, :] * LOG2E))
        c3 = jnp.where(row == 0, c_hi, jnp.where(row == 1, c_mid, jnp.where(row == 2, c_lo, 0.0)))
        one3 = jnp.where(row < 3, 1.0, 0.0)
        q = _head_rms_t(yq_ref[rows, :], qg_ref[...]) * Q_SCALE
        q_out[0, hd] = jnp.concatenate([q, one3, c3, zeros], axis=0).astype(BF)
        k = _head_rms_t(yk_ref[rows, :], kg_ref[...])
        k_out[0, hd] = jnp.concatenate([k, -c3, one3, zeros], axis=0).T.astype(BF)
    _store_lane_tiles(v_out, _dot_nt(wv_ref[...], h).astype(BF), tile)
    z_out[0] = _silu(_dot(h, wz_ref[...])).astype(BF)


def _fox_proj(h, cum, w_in, q_gain, k_gain, *, tile):
    b, s, d = h.shape
    tm = min(TOK_TILE, s)
    wt = w_in.T.astype(BF)
    wq, wk, wv = wt[:MIX_WIDTH], wt[MIX_WIDTH:2 * MIX_WIDTH], wt[2 * MIX_WIDTH:3 * MIX_WIDTH]
    wz = w_in[:, 3 * MIX_WIDTH + N_HEADS:].astype(BF)
    qg, kg = q_gain.reshape(HEAD_DIM, 1), k_gain.reshape(HEAD_DIM, 1)
    return pl.pallas_call(
        functools.partial(_fox_proj_kernel, tile=tile),
        grid=(b, s // tm),
        in_specs=[pl.BlockSpec((1, tm, d), lambda i, j: (i, j, 0)),
                  pl.BlockSpec((1, N_HEADS, tm), lambda i, j: (i, 0, j)),
                  _full(wq.shape), _full(wk.shape), _full(wv.shape), _full(wz.shape),
                  _full(qg.shape), _full(kg.shape)],
        out_specs=[pl.BlockSpec((1, N_HEADS, KEY_PAD, tm), lambda i, j: (i, 0, 0, j)),
                   pl.BlockSpec((1, N_HEADS, tm, KEY_PAD), lambda i, j: (i, 0, j, 0)),
                   pl.BlockSpec((1, tm // tile, MIX_WIDTH, tile), lambda i, j: (i, j, 0, 0)),
                   pl.BlockSpec((1, tm, MIX_WIDTH), lambda i, j: (i, j, 0))],
        out_shape=[jax.ShapeDtypeStruct((b, N_HEADS, KEY_PAD, s), BF),
                   jax.ShapeDtypeStruct((b, N_HEADS, s, KEY_PAD), BF),
                   jax.ShapeDtypeStruct((b, s // tile, MIX_WIDTH, tile), BF),
                   jax.ShapeDtypeStruct((b, s, MIX_WIDTH), BF)],
        scratch_shapes=[pltpu.VMEM((MIX_WIDTH, tm), F32), pltpu.VMEM((MIX_WIDTH, tm), F32)],
        compiler_params=_params(("parallel", "parallel")),
        name="fox_proj",
    )(h, cum, wq, wk, wv, wz, qg, kg)


def _out_proj_kernel(*refs, n_o, has_next):
    o_refs = refs[:n_o]
    z_ref, x_ref, w_ref = refs[n_o:n_o + 3]
    rest = refs[n_o + 3:]
    if has_next:
        g_ref, x_out, h_out = rest
    else:
        (x_out,) = rest
    o = o_refs[0][0].astype(F32)
    for r in o_refs[1:]:
        o = o + r[0].astype(F32)
    y = _dot((o * z_ref[0].astype(F32)).astype(BF), w_ref[...])
    x_new = x_ref[0] + y
    x_out[0] = x_new
    if has_next:
        h_out[0] = _rms_rows(x_new, g_ref[...]).astype(BF)


def _out_proj(o_list, zs, x, w_out, next_gain):
    b, s, d = x.shape
    tm = min(TOK_TILE, s)
    has_next = next_gain is not None
    blk = pl.BlockSpec((1, tm, d), lambda i, j: (i, j, 0))
    args = list(o_list) + [zs, x, w_out.astype(BF)]
    in_specs = [blk] * (len(o_list) + 2) + [_full(w_out.shape)]
    out_shape = [jax.ShapeDtypeStruct((b, s, d), F32)]
    out_specs = [blk]
    if has_next:
        args.append(next_gain.reshape(1, d))
        in_specs.append(_full((1, d)))
        out_shape.append(jax.ShapeDtypeStruct((b, s, d), BF))
        out_specs.append(blk)
    res = pl.pallas_call(
        functools.partial(_out_proj_kernel, n_o=len(o_list), has_next=has_next),
        grid=(b, s // tm), in_specs=in_specs, out_specs=out_specs, out_shape=out_shape,
        compiler_params=_params(("parallel", "parallel")),
        name="out_proj",
    )(*args)
    return (res[0], res[1]) if has_next else (res[0], None)


ITEM_LANES = 512
CMP_Q_TILE = 512
NSA_SLC_Q_TILE = 512
NSA_SLC_TILE = 512
NSA_WIN_Q_TILE = 256
NSA_WIN_TILE = 256
SWA_TILE = 128
FOX_Q_TILE = 512
FOX_K_TILE = 512
FOX_HEADS_PER_STEP = 4


def _nsa_mixer(h, cos, sin, w_in, q_gain, k_gain, cmp_pos, cmp_w1, cmp_w2):
    b, s, _ = h.shape
    g = NSA_KV_HEADS
    r = N_HEADS // g
    qT, kc_tok, vc_tok, ks, kw, vsT, vwT, gates, zs = _nsa_proj(
        h, cos, sin, w_in, q_gain, k_gain, slc_tile=NSA_SLC_TILE, win_tile=NSA_WIN_TILE)
    k_cmp, v_cmpT = _compress(kc_tok, vc_tok, cmp_pos, cmp_w1, cmp_w2, k_gain[0])
    gates5 = gates.reshape(b, 3, g, r, s)
    o_cmp, sel_bias = _cmp_select(qT, k_cmp, v_cmpT, gates5, tq=CMP_Q_TILE)
    o_slc = _flash(qT, ks, vsT, mode="causal", n_par=1, n_rep=r, dq=HEAD_DIM, kc=SLC_KEY_LANES,
                   tq=NSA_SLC_Q_TILE, tk=NSA_SLC_TILE, cw=ITEM_LANES, sel_bias=sel_bias,
                   gates=gates5, gate_branch=1, name="nsa_selected")
    o_win = _flash(qT, kw, vwT, mode="window", n_par=g, n_rep=r, dq=HEAD_DIM, kc=HEAD_DIM,
                   tq=NSA_WIN_Q_TILE, tk=NSA_WIN_TILE, cw=ITEM_LANES, window=NSA_WINDOW,
                   gates=gates5, gate_branch=2, name="nsa_window")
    return [o_cmp, o_slc, o_win], zs


def _swa_mixer(h, cos, sin, w_in, q_gain, k_gain, sinks):
    r = N_HEADS // SWA_KV_HEADS
    qT, k, vT, zs = _swa_proj(h, cos, sin, w_in, q_gain, k_gain, win_tile=SWA_TILE)
    o = _flash(qT, k, vT, mode="window", n_par=SWA_KV_HEADS, n_rep=r, dq=HEAD_DIM, kc=HEAD_DIM, tq=SWA_TILE,
               tk=SWA_TILE, cw=r * SWA_TILE, window=SWA_WINDOW, sinks=sinks, name="swa_window")
    return [o], zs


def _fox_mixer(h, w_in, forget_bias, q_gain, k_gain):
    b, s, _ = h.shape
    wf = w_in[:, 3 * MIX_WIDTH:3 * MIX_WIDTH + N_HEADS].T.astype(BF)
    cum = _forget_cum(h, wf, forget_bias)
    qT, k, vT, zs = _fox_proj(h, cum, w_in, q_gain, k_gain, tile=FOX_K_TILE)
    o = _flash(qT.reshape(b, N_HEADS * KEY_PAD, s), k, vT, mode="causal", n_par=FOX_HEADS_PER_STEP, n_rep=1, dq=KEY_PAD,
               kc=KEY_PAD, tq=FOX_Q_TILE, tk=FOX_K_TILE, cw=ITEM_LANES, name="fox_attention")
    return [o], zs


def kernel(x, positions, norm_gains, a_w_in, a_q_gain, a_k_gain, a_cmp_pos, a_cmp_w1, a_cmp_w2, a_w_out,
           b_w_in, b_q_gain, b_k_gain, b_sinks, b_w_out,
           c_w_in, c_forget_bias, c_q_gain, c_k_gain, c_w_out):
    depth = norm_gains.shape[0]
    cos, sin = _rope_tables(positions)
    h = _prenorm(x, norm_gains[0])
    for i in range(depth):
        j, mixer = divmod(i, 3)
        if mixer == 0:
            o_list, zs = _nsa_mixer(h, cos, sin, a_w_in[j], a_q_gain[j], a_k_gain[j],
                                    a_cmp_pos[j], a_cmp_w1[j], a_cmp_w2[j])
            w_out = a_w_out[j]
        elif mixer == 1:
            o_list, zs = _swa_mixer(h, cos, sin, b_w_in[j], b_q_gain[j], b_k_gain[j], b_sinks[j])
            w_out = b_w_out[j]
        else:
            o_list, zs = _fox_mixer(h, c_w_in[j], c_forget_bias[j], c_q_gain[j], c_k_gain[j])
            w_out = c_w_out[j]
        next_gain = norm_gains[i + 1] if i + 1 < depth else None
        x, h = _out_proj(o_list, zs, x, w_out, next_gain)
    return x
```

```python
import functools

import jax
import jax.numpy as jnp
import numpy as np
from jax import lax
from jax.experimental import pallas as pl
from jax.experimental.pallas import tpu as pltpu

D_MODEL = 1024
HEAD_DIM = 64
N_HEADS = 16
MIX_WIDTH = N_HEADS * HEAD_DIM
ROPE_THETA = 10000.0
EPS = 1e-6
SCALE = HEAD_DIM ** -0.5
NEG_INF = -1e30
BIG = 1e30
M_INIT = -1e29
SEL_OFF = -(2.0 ** 100)

NSA_KV_HEADS = 4
NSA_CMP_LEN = 32
NSA_CMP_STRIDE = 16
NSA_SLC_LEN = 64
NSA_TOPK = 16
NSA_WINDOW = 512
SWA_KV_HEADS = 2
SWA_WINDOW = 128

LOG2E = float(np.log2(np.e))
Q_SCALE = SCALE * LOG2E

LANES = 128
KEY_PAD = 128
SLC_KEY_LANES = 256
V_ROWS = 80
FLASH_UNROLL = 4
WINDOW_LOOKAHEAD = 2
WINDOW_SCORE_BUFS = 4
CMP_BLOCK = 128
CMP_MASK_ROWS = 2 * CMP_BLOCK
VMEM_LIMIT = 56 * 1024 * 1024

TOK_TILE = 512
NT_DIMS = (((1,), (1,)), ((), ()))

BF = jnp.bfloat16
F32 = jnp.float32


def _params(sem):
    return pltpu.CompilerParams(dimension_semantics=sem, vmem_limit_bytes=VMEM_LIMIT)


def _dot(a, b):
    return jnp.dot(a, b, preferred_element_type=F32)


def _dot_nt(a, b):
    return lax.dot_general(a, b, NT_DIMS, preferred_element_type=F32)


def _rope_tab_kernel(pos_ref, invf_ref, cos_ref, sin_ref):
    ang = invf_ref[...] * pos_ref[0].astype(F32)
    cos_ref[0] = jnp.cos(ang)
    sin_ref[0] = jnp.sin(ang)


def _rope_tables(positions):
    b, s = positions.shape
    half = HEAD_DIM // 2
    inv_freq = ROPE_THETA ** (-jnp.arange(half, dtype=F32) * 2.0 / HEAD_DIM)
    tm = min(TOK_TILE, s)
    out = jax.ShapeDtypeStruct((b, half, s), F32)
    return pl.pallas_call(
        _rope_tab_kernel,
        grid=(b, s // tm),
        in_specs=[pl.BlockSpec((1, 1, tm), lambda i, j: (i, 0, j)),
                  pl.BlockSpec((half, 1), lambda i, j: (0, 0))],
        out_specs=[pl.BlockSpec((1, half, tm), lambda i, j: (i, 0, j))] * 2,
        out_shape=[out, out],
        compiler_params=_params(("parallel", "parallel")),
        name="rope_tables",
    )(positions.reshape(b, 1, s), inv_freq.reshape(half, 1))


def _rms_rows(x, gain_row):
    y = x * lax.rsqrt(jnp.mean(x * x, axis=-1, keepdims=True) + EPS)
    return y * gain_row


def _prenorm_kernel(x_ref, g_ref, h_ref):
    h_ref[0] = _rms_rows(x_ref[0], g_ref[...]).astype(BF)


def _prenorm(x, gain):
    b, s, d = x.shape
    tm = min(TOK_TILE, s)
    return pl.pallas_call(
        _prenorm_kernel,
        grid=(b, s // tm),
        in_specs=[pl.BlockSpec((1, tm, d), lambda i, j: (i, j, 0)),
                  pl.BlockSpec((1, d), lambda i, j: (0, 0))],
        out_specs=pl.BlockSpec((1, tm, d), lambda i, j: (i, j, 0)),
        out_shape=jax.ShapeDtypeStruct((b, s, d), BF),
        compiler_params=_params(("parallel", "parallel")),
        name="prenorm",
    )(x, gain.reshape(1, d))


def _head_rms_t(y, gain_col):
    ms = jnp.mean(y * y, axis=0, keepdims=True)
    return (y * lax.rsqrt(ms + EPS)) * gain_col


def _rope_t(y, cos, sin):
    half = HEAD_DIM // 2
    x1, x2 = y[:half], y[half:]
    return jnp.concatenate([x1 * cos - x2 * sin, x2 * cos + x1 * sin], axis=0)


def _to_token_major(y):
    pad = jnp.zeros((KEY_PAD - y.shape[0], y.shape[1]), y.dtype)
    return jnp.concatenate([y, pad], axis=0).T


def _silu(z):
    return z * (1.0 / (1.0 + jnp.exp(-z)))


def _sigmoid(z):
    return 1.0 / (1.0 + jnp.exp(-z))


def _store_lane_tiles(out_ref, y, tile):
    for c in range(y.shape[1] // tile):
        out_ref[0, c] = y[:, c * tile:(c + 1) * tile]


def _nsa_proj_kernel(h_ref, cos_ref, sin_ref, wq_ref, wk_ref, wv_ref, wg_ref, wz_ref,
                     qg_ref, kg_ref,
                     q_out, kc_out, vc_out, ks_out, kw_out, vs_out, vw_out, g_out, z_out,
                     y_ref, *, slc_tile, win_tile):
    h = h_ref[0]
    cos, sin = cos_ref[0], sin_ref[0]
    g = NSA_KV_HEADS
    kvd = g * HEAD_DIM
    y_ref[...] = _dot_nt(wq_ref[...], h)
    for hd in range(N_HEADS):
        rows = slice(hd * HEAD_DIM, (hd + 1) * HEAD_DIM)
        y = _rope_t(_head_rms_t(y_ref[rows, :], qg_ref[...]), cos, sin) * Q_SCALE
        q_out[0, rows, :] = y.astype(BF)
    y_ref[0:3 * kvd, :] = _dot_nt(wk_ref[...], h)
    tm = h.shape[0]
    tok = pl.program_id(1) * tm + lax.broadcasted_iota(jnp.int32, (tm, 1), 0)
    blk_lane = HEAD_DIM + lax.shift_right_logical(tok, int(np.log2(NSA_SLC_LEN)))
    blk_hot = lax.broadcasted_iota(jnp.int32, (1, SLC_KEY_LANES), 1) == blk_lane
    for kind in range(3):
        for gi in range(g):
            r0 = (kind * g + gi) * HEAD_DIM
            y = y_ref[r0:r0 + HEAD_DIM, :]
            if kind > 0:
                y = _head_rms_t(y, kg_ref[:, kind:kind + 1])
            yt = _to_token_major(_rope_t(y, cos, sin))
            if kind == 0:
                kc_out[0, gi] = yt[:, :HEAD_DIM]
            elif kind == 1:
                wide = jnp.concatenate([yt, jnp.zeros((tm, SLC_KEY_LANES - KEY_PAD), F32)], axis=1)
                ks_out[0, gi] = jnp.where(blk_hot, 1.0, wide).astype(BF)
            else:
                kw_out[0, gi] = yt.astype(BF)
    y_ref[0:3 * kvd, :] = _dot_nt(wv_ref[...], h)
    for gi in range(g):
        r0 = gi * HEAD_DIM
        vc_out[0, gi] = _to_token_major(y_ref[r0:r0 + HEAD_DIM, :])[:, :HEAD_DIM]
    _store_lane_tiles(vs_out, y_ref[kvd:2 * kvd, :].astype(BF), slc_tile)
    _store_lane_tiles(vw_out, y_ref[2 * kvd:3 * kvd, :].astype(BF), win_tile)
    g_out[0] = _sigmoid(_dot_nt(wg_ref[...], h))
    z_out[0] = _silu(_dot(h, wz_ref[...])).astype(BF)


def _full(shape):
    nd = len(shape)
    return pl.BlockSpec(shape, lambda i, j, _n=nd: (0,) * _n)


def _nsa_proj(h, cos, sin, w_in, q_gain, k_gain, *, slc_tile, win_tile):
    b, s, d = h.shape
    g = NSA_KV_HEADS
    kvd = g * HEAD_DIM
    tm = min(TOK_TILE, s)
    sizes = [MIX_WIDTH] + [kvd] * 6 + [3 * N_HEADS]
    off = np.cumsum([0] + sizes)
    wt = w_in.T.astype(BF)
    wq = wt[off[0]:off[1]]
    wk = jnp.concatenate([wt[off[1]:off[2]], wt[off[3]:off[4]], wt[off[5]:off[6]]], axis=0)
    wv = jnp.concatenate([wt[off[2]:off[3]], wt[off[4]:off[5]], wt[off[6]:off[7]]], axis=0)
    wg = wt[off[7]:off[8]]
    wz = w_in[:, off[8]:].astype(BF)
    qg = q_gain.reshape(HEAD_DIM, 1)
    kg = k_gain.T
    n_t = s // tm
    out_shape = [
        jax.ShapeDtypeStruct((b, MIX_WIDTH, s), BF),
        jax.ShapeDtypeStruct((b, g, s, HEAD_DIM), F32),
        jax.ShapeDtypeStruct((b, g, s, HEAD_DIM), F32),
        jax.ShapeDtypeStruct((b, g, s, SLC_KEY_LANES), BF),
        jax.ShapeDtypeStruct((b, g, s, KEY_PAD), BF),
        jax.ShapeDtypeStruct((b, s // slc_tile, kvd, slc_tile), BF),
        jax.ShapeDtypeStruct((b, s // win_tile, kvd, win_tile), BF),
        jax.ShapeDtypeStruct((b, 3 * N_HEADS, s), F32),
        jax.ShapeDtypeStruct((b, s, MIX_WIDTH), BF),
    ]
    out_specs = [
        pl.BlockSpec((1, MIX_WIDTH, tm), lambda i, j: (i, 0, j)),
        pl.BlockSpec((1, g, tm, HEAD_DIM), lambda i, j: (i, 0, j, 0)),
        pl.BlockSpec((1, g, tm, HEAD_DIM), lambda i, j: (i, 0, j, 0)),
        pl.BlockSpec((1, g, tm, SLC_KEY_LANES), lambda i, j: (i, 0, j, 0)),
        pl.BlockSpec((1, g, tm, KEY_PAD), lambda i, j: (i, 0, j, 0)),
        pl.BlockSpec((1, tm // slc_tile, kvd, slc_tile), lambda i, j: (i, j, 0, 0)),
        pl.BlockSpec((1, tm // win_tile, kvd, win_tile), lambda i, j: (i, j, 0, 0)),
        pl.BlockSpec((1, 3 * N_HEADS, tm), lambda i, j: (i, 0, j)),
        pl.BlockSpec((1, tm, MIX_WIDTH), lambda i, j: (i, j, 0)),
    ]
    in_specs = [
        pl.BlockSpec((1, tm, d), lambda i, j: (i, j, 0)),
        pl.BlockSpec((1, HEAD_DIM // 2, tm), lambda i, j: (i, 0, j)),
        pl.BlockSpec((1, HEAD_DIM // 2, tm), lambda i, j: (i, 0, j)),
        _full(wq.shape), _full(wk.shape), _full(wv.shape), _full(wg.shape), _full(wz.shape),
        _full(qg.shape), _full(kg.shape),
    ]
    return pl.pallas_call(
        functools.partial(_nsa_proj_kernel, slc_tile=slc_tile, win_tile=win_tile),
        grid=(b, n_t), in_specs=in_specs, out_specs=out_specs, out_shape=out_shape,
        scratch_shapes=[pltpu.VMEM((MIX_WIDTH, tm), F32)],
        compiler_params=_params(("parallel", "parallel")),
        name="nsa_proj",
    )(h, cos, sin, wq, wk, wv, wg, wz, qg, kg)


def _gelu_tanh(x):
    c = np.float32(np.sqrt(2.0 / np.pi))
    return 0.5 * x * (1.0 + jnp.tanh(c * (x + 0.044715 * (x * x * x))))


def _compress_kernel(kc_ref, vc_ref, pos_ref, w1_ref, w2_ref, kg_ref, kcmp_out, vcmp_out):
    for which, (src, dst) in enumerate(((kc_ref, kcmp_out), (vc_ref, vcmp_out))):
        x = src[0, 0]
        n = x.shape[0]
        half = x.shape[1]
        xa = (x + pos_ref[which, 0:1, :]).astype(BF)
        xb = (x + pos_ref[which, 1:2, :]).astype(BF)
        ua = _dot(xa, w1_ref[which, :half, :])
        ub = _dot(xb, w1_ref[which, half:, :])
        row = lax.broadcasted_iota(jnp.int32, (n, 1), 0)
        ub_next = jnp.where(row == n - 1, 0.0, pltpu.roll(ub, n - 1, 0))
        hid = _gelu_tanh(ua + ub_next)
        y = _dot(hid.astype(BF), w2_ref[which])
        if which == 0:
            y = _rms_rows(y, kg_ref[...])
            dst[0, 0] = y.astype(BF)
        else:
            pad = jnp.zeros((n, KEY_PAD - HEAD_DIM), F32)
            dst[0, 0] = jnp.concatenate([y, pad], axis=1).T[:HEAD_DIM].astype(BF)


def _compress(kc_tok, vc_tok, cmp_pos, cmp_w1, cmp_w2, k_gain0):
    b, g, s, _ = kc_tok.shape
    n_chunk = s // NSA_CMP_STRIDE
    flat = NSA_CMP_STRIDE * HEAD_DIM
    kc = kc_tok.reshape(b, g, n_chunk, flat)
    vc = vc_tok.reshape(b, g, n_chunk, flat)
    pos = cmp_pos.reshape(2, 2, flat)
    w1 = cmp_w1.astype(BF)
    w2 = cmp_w2.astype(BF)
    blk = pl.BlockSpec((1, 1, n_chunk, flat), lambda i, j: (i, j, 0, 0))
    return pl.pallas_call(
        _compress_kernel,
        grid=(b, g),
        in_specs=[blk, blk, _full(pos.shape), _full(w1.shape), _full(w2.shape),
                  _full((1, HEAD_DIM))],
        out_specs=[pl.BlockSpec((1, 1, n_chunk, HEAD_DIM), lambda i, j: (i, j, 0, 0)),
                   pl.BlockSpec((1, 1, HEAD_DIM, n_chunk), lambda i, j: (i, j, 0, 0))],
        out_shape=[jax.ShapeDtypeStruct((b, g, n_chunk, HEAD_DIM), BF),
                   jax.ShapeDtypeStruct((b, g, HEAD_DIM, n_chunk), BF)],
        compiler_params=_params(("parallel", "parallel")),
        name="nsa_compress",
    )(kc, vc, pos, w1, w2, k_gain0.reshape(1, HEAD_DIM))


def _gate_row(gate_ref, n_heads, grp=0):
    return jnp.concatenate([gate_ref[0, 0, grp, r:r + 1, :] for r in range(n_heads)], axis=1)


def _cmp_branch(rows, q_ref, kc_ref, vc_ref, ov_ref, gate_ref, o_out, sel_out, *, tq, n_blk, n_grp):
    r_heads = N_HEADS // NSA_KV_HEADS
    nq = r_heads * tq
    width = r_heads * HEAD_DIM
    q0 = pl.program_id(2) * tq
    n_cmp = kc_ref.shape[2] - 1
    t_row = q0 + (lax.broadcasted_iota(jnp.int32, (1, nq), 1) & (tq - 1))
    lo = max(rows - CMP_MASK_ROWS, 0)
    c_col = lo + lax.broadcasted_iota(jnp.int32, (rows - lo, 1), 0)
    valid = (c_col * NSA_CMP_STRIDE + (NSA_CMP_LEN - 1) <= t_row) & (c_col < n_cmp)
    one_row = jnp.where(lax.broadcasted_iota(jnp.int32, (V_ROWS - HEAD_DIM, rows), 0) == 0, 1.0, 0.0).astype(BF)
    n_live = min(n_blk, rows * NSA_CMP_STRIDE // NSA_SLC_LEN + 8)
    t1 = q0 + lax.broadcasted_iota(jnp.int32, (1, tq), 1)
    cur = lax.shift_right_logical(t1, int(np.log2(NSA_SLC_LEN)))
    blk = lax.broadcasted_iota(jnp.int32, (n_live, tq), 0)
    forced = (blk == 0) | (blk == cur) | (blk == cur - 1)
    imps = []
    for gi in range(n_grp):
        q4 = jnp.concatenate([q_ref[0, (gi * r_heads + r) * HEAD_DIM:(gi * r_heads + r + 1) * HEAD_DIM, :]
                              for r in range(r_heads)], axis=1)
        s = _dot(kc_ref[0, gi, 0:rows, :], q4)
        s_new = jnp.where(valid, s[lo:], NEG_INF)
        m = jnp.max(s_new, axis=0, keepdims=True)
        if lo:
            m = jnp.maximum(m, jnp.max(s[:lo], axis=0, keepdims=True))
        e = jnp.where(valid, jnp.exp2(s_new - m), 0.0).astype(BF)
        if lo:
            e = jnp.concatenate([jnp.exp2(s[:lo] - m).astype(BF), e], axis=0)
        lhs = jnp.concatenate([vc_ref[0, gi, :, 0:rows], one_row, ov_ref[0:n_live, 0:rows]], axis=0)
        res = _dot(lhs, e)
        l = res[HEAD_DIM:HEAD_DIM + 1]
        inv = jnp.where(l > 0.0, 1.0 / jnp.where(l > 0.0, l, 1.0), 0.0)
        o = res[:HEAD_DIM] * (inv * _gate_row(gate_ref, r_heads, gi))
        o_rows = jnp.concatenate([o[:, r * tq:(r + 1) * tq] for r in range(r_heads)], axis=0)
        o_out[0, :, gi * width:(gi + 1) * width] = o_rows.T.astype(o_out.dtype)
        w = res[V_ROWS:V_ROWS + n_live] * inv
        imp = w[:, 0:tq]
        for r in range(1, r_heads):
            imp = imp + w[:, r * tq:(r + 1) * tq]
        imps.append(jnp.where(forced, BIG, jnp.where(blk > cur, NEG_INF, imp)))
    for _ in range(min(NSA_TOPK, n_blk)):
        for gi in range(n_grp):
            best = jnp.max(imps[gi], axis=0, keepdims=True)
            first = jnp.min(jnp.where(imps[gi] == best, blk, n_blk), axis=0, keepdims=True)
            imps[gi] = jnp.where(blk == first, -jnp.inf, imps[gi])
    for gi in range(n_grp):
        sel_out[0, gi, 0:n_live, :] = jnp.where(imps[gi] == -jnp.inf, 0.0, SEL_OFF).astype(BF)
        if n_live < n_blk:
            sel_out[0, gi, n_live:n_blk, :] = jnp.full((n_blk - n_live, tq), SEL_OFF, BF)


def _cmp_select_kernel(q_ref, kc_ref, vc_ref, ov_ref, gate_ref, o_out, sel_out, *, tq, n_blk, n_grp):
    q0 = pl.program_id(2) * tq
    n_chunk = kc_ref.shape[2]
    n_need = jnp.minimum((q0 + tq - NSA_CMP_LEN) // NSA_CMP_STRIDE + 1, n_chunk - 1)
    n_steps = n_chunk // CMP_BLOCK
    need_steps = (n_need + CMP_BLOCK - 1) // CMP_BLOCK
    for k in range(1, n_steps + 1):
        @pl.when(need_steps == k)
        def _(k=k):
            _cmp_branch(k * CMP_BLOCK, q_ref, kc_ref, vc_ref, ov_ref, gate_ref, o_out, sel_out,
                        tq=tq, n_blk=n_blk, n_grp=n_grp)


def _overlap_matrix(s):
    n_chunk = s // NSA_CMP_STRIDE
    n_blk = s // NSA_SLC_LEN
    c0 = np.arange(n_chunk) * NSA_CMP_STRIDE
    c1 = c0 + NSA_CMP_LEN - 1
    b0 = np.arange(n_blk) * NSA_SLC_LEN
    ov = np.minimum(c1[None, :], b0[:, None] + NSA_SLC_LEN - 1) - np.maximum(c0[None, :], b0[:, None]) + 1
    return jnp.asarray(np.clip(ov, 0, None) / NSA_CMP_LEN, BF)


def _cmp_select(qT, k_cmp, v_cmpT, gates5, *, tq):
    b, _, s = qT.shape
    g = NSA_KV_HEADS
    r_heads = N_HEADS // g
    n_chunk = k_cmp.shape[2]
    n_blk = s // NSA_SLC_LEN
    ov = _overlap_matrix(s)
    n_grp = CMP_GROUPS
    rows = n_grp * r_heads * HEAD_DIM
    return pl.pallas_call(
        functools.partial(_cmp_select_kernel, tq=tq, n_blk=n_blk, n_grp=n_grp),
        grid=(b, g // n_grp, s // tq),
        in_specs=[
            pl.BlockSpec((1, rows, tq), lambda i, j, k: (i, j, k)),
            pl.BlockSpec((1, n_grp, n_chunk, HEAD_DIM), lambda i, j, k: (i, j, 0, 0)),
            pl.BlockSpec((1, n_grp, HEAD_DIM, n_chunk), lambda i, j, k: (i, j, 0, 0)),
            pl.BlockSpec((n_blk, n_chunk), lambda i, j, k: (0, 0)),
            pl.BlockSpec((1, 1, n_grp, r_heads, tq), lambda i, j, k: (i, 0, j, 0, k)),
        ],
        out_specs=[pl.BlockSpec((1, tq, rows), lambda i, j, k: (i, k, j)),
                   pl.BlockSpec((1, n_grp, n_blk, tq), lambda i, j, k: (i, j, 0, k))],
        out_shape=[jax.ShapeDtypeStruct((b, s, MIX_WIDTH), BF),
                   jax.ShapeDtypeStruct((b, g, n_blk, s), BF)],
        compiler_params=_params(("parallel", "parallel", "parallel")),
        name="nsa_cmp_select",
    )(qT, k_cmp, v_cmpT, ov, gates5)


def _flash_kernel(*refs, mode, n_par, n_rep, dq, kc, tq, tk, cw, window, n_blk, has_sink, has_gate):
    it = iter(refs)
    q_ref, k_ref, v_ref = next(it), next(it), next(it)
    sel_ref = next(it) if n_blk else None
    sink_ref = next(it) if has_sink else None
    gate_ref = next(it) if has_gate else None
    out_ref = next(it)
    qs_ref, m_ref, acc_ref, mt_ref = (next(it) for _ in range(4))
    s_bufs = tuple(it)

    nq = n_rep * tq
    items = [(p, c) for p in range(n_par) for c in range(nq // cw)]
    n_items = len(items)
    grp = pl.program_id(1)
    q0 = pl.program_id(2) * tq
    acc_row = lax.broadcasted_iota(jnp.int32, (V_ROWS, nq), 0)
    for p in range(n_par):
        for r in range(n_rep):
            hd = p * n_rep + r
            qs_ref[p, 0:dq, r * tq:(r + 1) * tq] = q_ref[0, hd * dq:(hd + 1) * dq, :]
        if n_blk:
            qs_ref[p, dq:dq + n_blk, :] = jnp.concatenate([sel_ref[0, 0]] * n_rep, axis=1)
            if dq + n_blk < kc:
                qs_ref[p, dq + n_blk:kc, :] = jnp.zeros((kc - dq - n_blk, nq), BF)
        if has_sink:
            m_ref[p] = jnp.concatenate(
                [jnp.full((1, tq), sink_ref[(grp * n_par + p) * n_rep + r] * LOG2E, F32) for r in range(n_rep)],
                axis=1)
            acc_ref[p] = jnp.where(acc_row == HEAD_DIM, 1.0, 0.0)
        else:
            m_ref[p] = jnp.full((1, nq), M_INIT, F32)
            acc_ref[p] = jnp.zeros((V_ROWS, nq), F32)

    t_row = q0 + (lax.broadcasted_iota(jnp.int32, (1, nq), 1) & (tq - 1))
    one_row = jnp.where(lax.broadcasted_iota(jnp.int32, (V_ROWS - HEAD_DIM, tk), 0) == 0, 1.0, 0.0).astype(BF)

    def stage_a(item, j, key0, slot, kind):
        p, c = item
        cols = slice(c * cw, (c + 1) * cw)
        s = _dot(k_ref[0, p, j][:, :kc], qs_ref[p, :, cols])
        if kind is not None:
            key = key0 + lax.broadcasted_iota(jnp.int32, (tk, 1), 0)
            ok = key <= t_row[:, cols] if kind == "causal" else key > t_row[:, cols] - window
            s = jnp.where(ok, s, NEG_INF)
        s_bufs[slot][...] = s
        mt_ref[slot] = jnp.max(s, axis=0, keepdims=True)

    def stage_b(item, j, slot):
        p, c = item
        cols = slice(c * cw, (c + 1) * cw)
        m_old = m_ref[p, :, cols]
        m_new = jnp.maximum(m_old, mt_ref[slot])
        pr = jnp.exp2(s_bufs[slot][...] - m_new).astype(BF)
        alpha = jnp.exp2(m_old - m_new)
        v = jnp.concatenate([v_ref[0, j, p * HEAD_DIM:(p + 1) * HEAD_DIM, :], one_row], axis=0)
        acc_ref[p, :, cols] = alpha * acc_ref[p, :, cols] + _dot(v, pr)
        m_ref[p, :, cols] = m_new

    if mode == "causal":
        assert n_items % 2 == 0
        n_full = q0 // tk

        def step(j, kind, next_kind, last):
            for idx, item in enumerate(items):
                slot = idx % 2
                if idx + 1 < n_items:
                    stage_a(items[idx + 1], j, j * tk, 1 - slot, kind)
                elif not last:
                    stage_a(items[0], j + 1, (j + 1) * tk, 1 - slot, next_kind)
                stage_b(item, j, slot)

        stage_a(items[0], 0, 0, 0, "causal")

        def body(j, carry):
            step(j, None, None, False)
            return carry

        def body_group(i, carry):
            for u in range(FLASH_UNROLL):
                step(FLASH_UNROLL * i + u, None, None, False)
            return carry

        n_main = jnp.maximum(n_full - 1, 0)
        n_groups = lax.shift_right_logical(n_main, int(np.log2(FLASH_UNROLL)))
        lax.fori_loop(0, n_groups, body_group, 0)
        lax.fori_loop(FLASH_UNROLL * n_groups, n_main, body, 0)

        @pl.when(n_full >= 1)
        def _():
            step(n_full - 1, None, "causal", False)

        step(n_full, "causal", None, True)
    else:
        w_tiles, q_tiles = window // tk, tq // tk
        work = []
        for i in range(w_tiles + q_tiles):
            jv = q0 // tk - w_tiles + i
            key0 = jnp.where(jv < 0, -(1 << 30), jv * tk)
            kind = "causal" if i >= w_tiles else "lower"
            work += [(item, jnp.maximum(jv, 0), key0, kind) for item in items]
        n_buf = len(s_bufs)
        ahead = WINDOW_LOOKAHEAD
        for n in range(ahead):
            stage_a(work[n][0], work[n][1], work[n][2], n % n_buf, work[n][3])
        for n, (item, j, _, _) in enumerate(work):
            if n + ahead < len(work):
                nxt = work[n + ahead]
                stage_a(nxt[0], nxt[1], nxt[2], (n + ahead) % n_buf, nxt[3])
            stage_b(item, j, n % n_buf)

    outs = []
    for p in range(n_par):
        acc = acc_ref[p]
        o = acc[:HEAD_DIM] * (1.0 / acc[HEAD_DIM:HEAD_DIM + 1])
        if has_gate:
            o = o * _gate_row(gate_ref, n_rep, p)
        outs += [o[:, r * tq:(r + 1) * tq] for r in range(n_rep)]
    out_ref[0] = jnp.concatenate(outs, axis=0).T.astype(out_ref.dtype)


def _flash(qT, k_tok, vT_tiles, *, mode, n_par, n_rep, dq, kc, tq, tk, cw, window=None,
           sel_bias=None, sinks=None, gates=None, gate_branch=0, name="flash"):
    b, _, s = qT.shape
    kh, k_lanes = k_tok.shape[1], k_tok.shape[3]
    n_t = s // tk
    k5 = k_tok.reshape(b, kh, n_t, tk, k_lanes)
    n_grp = kh // n_par
    heads = n_par * n_rep
    nq = n_rep * tq
    n_blk = sel_bias.shape[2] if sel_bias is not None else 0
    has_sink, has_gate = sinks is not None, gates is not None
    args = [qT, k5, vT_tiles]
    in_specs = [
        pl.BlockSpec((1, heads * dq, tq), lambda i, j, k: (i, j, k)),
        pl.BlockSpec((1, n_par, n_t, tk, k_lanes), lambda i, j, k: (i, j, 0, 0, 0)),
        pl.BlockSpec((1, n_t, n_par * HEAD_DIM, tk), lambda i, j, k: (i, 0, j, 0)),
    ]
    if n_blk:
        assert dq + n_blk <= kc
        args.append(sel_bias)
        in_specs.append(pl.BlockSpec((1, 1, n_blk, tq), lambda i, j, k: (i, j, 0, k)))
    if has_sink:
        args.append(sinks.astype(F32))
        in_specs.append(pl.BlockSpec(memory_space=pltpu.SMEM))
    if has_gate:
        args.append(gates)
        in_specs.append(pl.BlockSpec((1, 1, n_par, n_rep, tq),
                                     lambda i, j, k, _br=gate_branch: (i, _br, j, 0, k)))
    n_sbuf = 2 if mode == "causal" else WINDOW_SCORE_BUFS
    scratch = [pltpu.VMEM((n_par, kc, nq), BF), pltpu.VMEM((n_par, 1, nq), F32),
               pltpu.VMEM((n_par, V_ROWS, nq), F32), pltpu.VMEM((n_sbuf, 1, cw), F32)]
    scratch += [pltpu.VMEM((tk, cw), F32)] * n_sbuf
    kern = functools.partial(_flash_kernel, mode=mode, n_par=n_par, n_rep=n_rep, dq=dq, kc=kc, tq=tq, tk=tk,
                             cw=cw, window=window, n_blk=n_blk, has_sink=has_sink, has_gate=has_gate)
    return pl.pallas_call(
        kern,
        grid=(b, n_grp, s // tq),
        in_specs=in_specs,
        out_specs=pl.BlockSpec((1, tq, heads * HEAD_DIM), lambda i, j, k: (i, k, j)),
        out_shape=jax.ShapeDtypeStruct((b, s, MIX_WIDTH), BF),
        scratch_shapes=scratch,
        compiler_params=_params(("parallel", "parallel", "arbitrary")),
        name=name,
    )(*args)


def _swa_proj_kernel(h_ref, cos_ref, sin_ref, wq_ref, wk_ref, wv_ref, wz_ref, qg_ref, kg_ref,
                     q_out, k_out, v_out, z_out, y_ref, *, win_tile):
    h = h_ref[0]
    cos, sin = cos_ref[0], sin_ref[0]
    y_ref[...] = _dot_nt(wq_ref[...], h)
    for hd in range(N_HEADS):
        rows = slice(hd * HEAD_DIM, (hd + 1) * HEAD_DIM)
        q_out[0, rows, :] = (_rope_t(_head_rms_t(y_ref[rows, :], qg_ref[...]), cos, sin) * Q_SCALE).astype(BF)
    kvd = SWA_KV_HEADS * HEAD_DIM
    y_ref[0:kvd, :] = _dot_nt(wk_ref[...], h)
    for gi in range(SWA_KV_HEADS):
        rows = slice(gi * HEAD_DIM, (gi + 1) * HEAD_DIM)
        k_out[0, gi] = _to_token_major(_rope_t(_head_rms_t(y_ref[rows, :], kg_ref[...]), cos, sin)).astype(BF)
    _store_lane_tiles(v_out, _dot_nt(wv_ref[...], h).astype(BF), win_tile)
    z_out[0] = _silu(_dot(h, wz_ref[...])).astype(BF)


def _swa_proj(h, cos, sin, w_in, q_gain, k_gain, *, win_tile):
    b, s, d = h.shape
    g = SWA_KV_HEADS
    kvd = g * HEAD_DIM
    tm = min(TOK_TILE, s)
    wt = w_in.T.astype(BF)
    wq, wk, wv = wt[:MIX_WIDTH], wt[MIX_WIDTH:MIX_WIDTH + kvd], wt[MIX_WIDTH + kvd:MIX_WIDTH + 2 * kvd]
    wz = w_in[:, MIX_WIDTH + 2 * kvd:].astype(BF)
    qg, kg = q_gain.reshape(HEAD_DIM, 1), k_gain.reshape(HEAD_DIM, 1)
    wpt = tm // win_tile
    half = HEAD_DIM // 2
    return pl.pallas_call(
        functools.partial(_swa_proj_kernel, win_tile=win_tile),
        grid=(b, s // tm),
        in_specs=[pl.BlockSpec((1, tm, d), lambda i, j: (i, j, 0)),
                  pl.BlockSpec((1, half, tm), lambda i, j: (i, 0, j)),
                  pl.BlockSpec((1, half, tm), lambda i, j: (i, 0, j)),
                  _full(wq.shape), _full(wk.shape), _full(wv.shape), _full(wz.shape),
                  _full(qg.shape), _full(kg.shape)],
        out_specs=[pl.BlockSpec((1, MIX_WIDTH, tm), lambda i, j: (i, 0, j)),
                   pl.BlockSpec((1, g, tm, KEY_PAD), lambda i, j: (i, 0, j, 0)),
                   pl.BlockSpec((1, wpt, kvd, win_tile), lambda i, j: (i, j, 0, 0)),
                   pl.BlockSpec((1, tm, MIX_WIDTH), lambda i, j: (i, j, 0))],
        out_shape=[jax.ShapeDtypeStruct((b, MIX_WIDTH, s), BF),
                   jax.ShapeDtypeStruct((b, g, s, KEY_PAD), BF),
                   jax.ShapeDtypeStruct((b, s // win_tile, kvd, win_tile), BF),
                   jax.ShapeDtypeStruct((b, s, MIX_WIDTH), BF)],
        scratch_shapes=[pltpu.VMEM((MIX_WIDTH, tm), F32)],
        compiler_params=_params(("parallel", "parallel")),
        name="swa_proj",
    )(h, cos, sin, wq, wk, wv, wz, qg, kg)


def _split3(x):
    hi = x.astype(BF)
    r1 = x - hi.astype(F32)
    mid = r1.astype(BF)
    lo = (r1 - mid.astype(F32)).astype(BF)
    return hi, mid, lo


def _forget_cum_kernel(h_ref, wf_ref, bias_ref, tri_ref, cum_out, carry_ref):
    @pl.when(pl.program_id(1) == 0)
    def _():
        carry_ref[...] = jnp.zeros_like(carry_ref)

    x = _dot_nt(wf_ref[...], h_ref[0]) + bias_ref[...]
    logf = jnp.minimum(x, 0.0) - jnp.log(1.0 + jnp.exp(-jnp.abs(x)))
    tri = tri_ref[...]
    hi, mid, lo = _split3(logf)
    cum = (_dot(hi, tri) + _dot(mid, tri)) + _dot(lo, tri) + carry_ref[:, 0:1]
    cum_out[0] = cum
    carry_ref[...] = jnp.broadcast_to(cum[:, -1:], carry_ref.shape)


def _forget_cum(h, wf, bias):
    b, s, d = h.shape
    tm = min(TOK_TILE, s)
    tri = jnp.asarray(np.arange(tm)[:, None] <= np.arange(tm)[None, :], BF)
    return pl.pallas_call(
        _forget_cum_kernel,
        grid=(b, s // tm),
        in_specs=[pl.BlockSpec((1, tm, d), lambda i, j: (i, j, 0)),
                  _full(wf.shape), _full((N_HEADS, 1)), _full(tri.shape)],
        out_specs=pl.BlockSpec((1, N_HEADS, tm), lambda i, j: (i, 0, j)),
        out_shape=jax.ShapeDtypeStruct((b, N_HEADS, s), F32),
        scratch_shapes=[pltpu.VMEM((N_HEADS, LANES), F32)],
        compiler_params=_params(("parallel", "arbitrary")),
        name="fox_forget_cum",
    )(h, wf, bias.reshape(N_HEADS, 1).astype(F32), tri)


def _fox_proj_kernel(h_ref, cum_ref, wq_ref, wk_ref, wv_ref, wz_ref, qg_ref, kg_ref,
                     q_out, k_out, v_out, z_out, yq_ref, yk_ref, *, tile):
    h = h_ref[0]
    tm = h.shape[0]
    yq_ref[...] = _dot_nt(wq_ref[...], h)
    yk_ref[...] = _dot_nt(wk_ref[...], h)
    row = lax.broadcasted_iota(jnp.int32, (8, tm), 0)
    zeros = jnp.zeros((KEY_PAD - HEAD_DIM - 16, tm), F32)
    for hd in range(N_HEADS):
        rows = slice(hd * HEAD_DIM, (hd + 1) * HEAD_DIM)
        c_hi, c_mid, c_lo = (c.astype(F32) for c in _split3(cum_ref[0, hd:hd + 1, :] * LOG2E))
        c3 = jnp.where(row == 0, c_hi, jnp.where(row == 1, c_mid, jnp.where(row == 2, c_lo, 0.0)))
        one3 = jnp.where(row < 3, 1.0, 0.0)
        q = _head_rms_t(yq_ref[rows, :], qg_ref[...]) * Q_SCALE
        q_out[0, hd] = jnp.concatenate([q, one3, c3, zeros], axis=0).astype(BF)
        k = _head_rms_t(yk_ref[rows, :], kg_ref[...])
        k_out[0, hd] = jnp.concatenate([k, -c3, one3, zeros], axis=0).T.astype(BF)
    _store_lane_tiles(v_out, _dot_nt(wv_ref[...], h).astype(BF), tile)
    z_out[0] = _silu(_dot(h, wz_ref[...])).astype(BF)


def _fox_proj(h, cum, w_in, q_gain, k_gain, *, tile):
    b, s, d = h.shape
    tm = min(TOK_TILE, s)
    wt = w_in.T.astype(BF)
    wq, wk, wv = wt[:MIX_WIDTH], wt[MIX_WIDTH:2 * MIX_WIDTH], wt[2 * MIX_WIDTH:3 * MIX_WIDTH]
    wz = w_in[:, 3 * MIX_WIDTH + N_HEADS:].astype(BF)
    qg, kg = q_gain.reshape(HEAD_DIM, 1), k_gain.reshape(HEAD_DIM, 1)
    return pl.pallas_call(
        functools.partial(_fox_proj_kernel, tile=tile),
        grid=(b, s // tm),
        in_specs=[pl.BlockSpec((1, tm, d), lambda i, j: (i, j, 0)),
                  pl.BlockSpec((1, N_HEADS, tm), lambda i, j: (i, 0, j)),
                  _full(wq.shape), _full(wk.shape), _full(wv.shape), _full(wz.shape),
                  _full(qg.shape), _full(kg.shape)],
        out_specs=[pl.BlockSpec((1, N_HEADS, KEY_PAD, tm), lambda i, j: (i, 0, 0, j)),
                   pl.BlockSpec((1, N_HEADS, tm, KEY_PAD), lambda i, j: (i, 0, j, 0)),
                   pl.BlockSpec((1, tm // tile, MIX_WIDTH, tile), lambda i, j: (i, j, 0, 0)),
                   pl.BlockSpec((1, tm, MIX_WIDTH), lambda i, j: (i, j, 0))],
        out_shape=[jax.ShapeDtypeStruct((b, N_HEADS, KEY_PAD, s), BF),
                   jax.ShapeDtypeStruct((b, N_HEADS, s, KEY_PAD), BF),
                   jax.ShapeDtypeStruct((b, s // tile, MIX_WIDTH, tile), BF),
                   jax.ShapeDtypeStruct((b, s, MIX_WIDTH), BF)],
        scratch_shapes=[pltpu.VMEM((MIX_WIDTH, tm), F32), pltpu.VMEM((MIX_WIDTH, tm), F32)],
        compiler_params=_params(("parallel", "parallel")),
        name="fox_proj",
    )(h, cum, wq, wk, wv, wz, qg, kg)


def _out_proj_kernel(*refs, n_o, has_next):
    o_refs = refs[:n_o]
    z_ref, x_ref, w_ref = refs[n_o:n_o + 3]
    rest = refs[n_o + 3:]
    if has_next:
        g_ref, x_out, h_out = rest
    else:
        (x_out,) = rest
    o = o_refs[0][0].astype(F32)
    for r in o_refs[1:]:
        o = o + r[0].astype(F32)
    y = _dot((o * z_ref[0].astype(F32)).astype(BF), w_ref[...])
    x_new = x_ref[0] + y
    x_out[0] = x_new
    if has_next:
        h_out[0] = _rms_rows(x_new, g_ref[...]).astype(BF)


def _out_proj(o_list, zs, x, w_out, next_gain):
    b, s, d = x.shape
    tm = min(TOK_TILE, s)
    has_next = next_gain is not None
    blk = pl.BlockSpec((1, tm, d), lambda i, j: (i, j, 0))
    args = list(o_list) + [zs, x, w_out.astype(BF)]
    in_specs = [blk] * (len(o_list) + 2) + [_full(w_out.shape)]
    out_shape = [jax.ShapeDtypeStruct((b, s, d), F32)]
    out_specs = [blk]
    if has_next:
        args.append(next_gain.reshape(1, d))
        in_specs.append(_full((1, d)))
        out_shape.append(jax.ShapeDtypeStruct((b, s, d), BF))
        out_specs.append(blk)
    res = pl.pallas_call(
        functools.partial(_out_proj_kernel, n_o=len(o_list), has_next=has_next),
        grid=(b, s // tm), in_specs=in_specs, out_specs=out_specs, out_shape=out_shape,
        compiler_params=_params(("parallel", "parallel")),
        name="out_proj",
    )(*args)
    return (res[0], res[1]) if has_next else (res[0], None)


ITEM_LANES = 512
CMP_Q_TILE = 512
CMP_GROUPS = 2
NSA_SLC_Q_TILE = 512
NSA_SLC_TILE = 512
NSA_WIN_Q_TILE = 256
NSA_WIN_TILE = 256
SWA_TILE = 128
FOX_Q_TILE = 512
FOX_K_TILE = 512
FOX_HEADS_PER_STEP = 4


def _nsa_mixer(h, cos, sin, w_in, q_gain, k_gain, cmp_pos, cmp_w1, cmp_w2):
    b, s, _ = h.shape
    g = NSA_KV_HEADS
    r = N_HEADS // g
    qT, kc_tok, vc_tok, ks, kw, vsT, vwT, gates, zs = _nsa_proj(
        h, cos, sin, w_in, q_gain, k_gain, slc_tile=NSA_SLC_TILE, win_tile=NSA_WIN_TILE)
    k_cmp, v_cmpT = _compress(kc_tok, vc_tok, cmp_pos, cmp_w1, cmp_w2, k_gain[0])
    gates5 = gates.reshape(b, 3, g, r, s)
    o_cmp, sel_bias = _cmp_select(qT, k_cmp, v_cmpT, gates5, tq=CMP_Q_TILE)
    o_slc = _flash(qT, ks, vsT, mode="causal", n_par=1, n_rep=r, dq=HEAD_DIM, kc=SLC_KEY_LANES,
                   tq=NSA_SLC_Q_TILE, tk=NSA_SLC_TILE, cw=ITEM_LANES, sel_bias=sel_bias,
                   gates=gates5, gate_branch=1, name="nsa_selected")
    o_win = _flash(qT, kw, vwT, mode="window", n_par=g, n_rep=r, dq=HEAD_DIM, kc=HEAD_DIM,
                   tq=NSA_WIN_Q_TILE, tk=NSA_WIN_TILE, cw=ITEM_LANES, window=NSA_WINDOW,
                   gates=gates5, gate_branch=2, name="nsa_window")
    return [o_cmp, o_slc, o_win], zs


def _swa_mixer(h, cos, sin, w_in, q_gain, k_gain, sinks):
    r = N_HEADS // SWA_KV_HEADS
    qT, k, vT, zs = _swa_proj(h, cos, sin, w_in, q_gain, k_gain, win_tile=SWA_TILE)
    o = _flash(qT, k, vT, mode="window", n_par=SWA_KV_HEADS, n_rep=r, dq=HEAD_DIM, kc=HEAD_DIM, tq=SWA_TILE,
               tk=SWA_TILE, cw=r * SWA_TILE, window=SWA_WINDOW, sinks=sinks, name="swa_window")
    return [o], zs


def _fox_mixer(h, w_in, forget_bias, q_gain, k_gain):
    b, s, _ = h.shape
    wf = w_in[:, 3 * MIX_WIDTH:3 * MIX_WIDTH + N_HEADS].T.astype(BF)
    cum = _forget_cum(h, wf, forget_bias)
    qT, k, vT, zs = _fox_proj(h, cum, w_in, q_gain, k_gain, tile=FOX_K_TILE)
    o = _flash(qT.reshape(b, N_HEADS * KEY_PAD, s), k, vT, mode="causal", n_par=FOX_HEADS_PER_STEP, n_rep=1, dq=KEY_PAD,
               kc=KEY_PAD, tq=FOX_Q_TILE, tk=FOX_K_TILE, cw=ITEM_LANES, name="fox_attention")
    return [o], zs


def kernel(x, positions, norm_gains, a_w_in, a_q_gain, a_k_gain, a_cmp_pos, a_cmp_w1, a_cmp_w2, a_w_out,
           b_w_in, b_q_gain, b_k_gain, b_sinks, b_w_out,
           c_w_in, c_forget_bias, c_q_gain, c_k_gain, c_w_out):
    depth = norm_gains.shape[0]
    cos, sin = _rope_tables(positions)
    h = _prenorm(x, norm_gains[0])
    for i in range(depth):
        j, mixer = divmod(i, 3)
        if mixer == 0:
            o_list, zs = _nsa_mixer(h, cos, sin, a_w_in[j], a_q_gain[j], a_k_gain[j],
                                    a_cmp_pos[j], a_cmp_w1[j], a_cmp_w2[j])
            w_out = a_w_out[j]
        elif mixer == 1:
            o_list, zs = _swa_mixer(h, cos, sin, b_w_in[j], b_q_gain[j], b_k_gain[j], b_sinks[j])
            w_out = b_w_out[j]
        else:
            o_list, zs = _fox_mixer(h, c_w_in[j], c_forget_bias[j], c_q_gain[j], c_k_gain[j])
            w_out = c_w_out[j]
        next_gain = norm_gains[i + 1] if i + 1 < depth else None
        x, h = _out_proj(o_list, zs, x, w_out, next_gain)
    return x
```

```python
import functools

import jax
import jax.numpy as jnp
import numpy as np
from jax import lax
from jax.experimental import pallas as pl
from jax.experimental.pallas import tpu as pltpu

D_MODEL = 1024
HEAD_DIM = 64
N_HEADS = 16
MIX_WIDTH = N_HEADS * HEAD_DIM
ROPE_THETA = 10000.0
EPS = 1e-6
SCALE = HEAD_DIM ** -0.5
NEG_INF = -1e30
BIG = 1e30
M_INIT = -1e29
SEL_OFF = -(2.0 ** 100)

NSA_KV_HEADS = 4
NSA_CMP_LEN = 32
NSA_CMP_STRIDE = 16
NSA_SLC_LEN = 64
NSA_TOPK = 16
NSA_WINDOW = 512
SWA_KV_HEADS = 2
SWA_WINDOW = 128

LOG2E = float(np.log2(np.e))
Q_SCALE = SCALE * LOG2E

LANES = 128
KEY_PAD = 128
SLC_KEY_LANES = 256
V_ROWS = 80
FLASH_UNROLL = 4
WINDOW_LOOKAHEAD = 2
WINDOW_SCORE_BUFS = 4
CMP_BLOCK = 128
CMP_MASK_ROWS = 2 * CMP_BLOCK
VMEM_LIMIT = 56 * 1024 * 1024

TOK_TILE = 512
NT_DIMS = (((1,), (1,)), ((), ()))

BF = jnp.bfloat16
F32 = jnp.float32


def _params(sem):
    return pltpu.CompilerParams(dimension_semantics=sem, vmem_limit_bytes=VMEM_LIMIT)


def _dot(a, b):
    return jnp.dot(a, b, preferred_element_type=F32)


def _dot_nt(a, b):
    return lax.dot_general(a, b, NT_DIMS, preferred_element_type=F32)


def _rope_tab_kernel(pos_ref, invf_ref, cos_ref, sin_ref):
    ang = invf_ref[...] * pos_ref[0].astype(F32)
    cos_ref[0] = jnp.cos(ang)
    sin_ref[0] = jnp.sin(ang)


def _rope_tables(positions):
    b, s = positions.shape
    half = HEAD_DIM // 2
    inv_freq = ROPE_THETA ** (-jnp.arange(half, dtype=F32) * 2.0 / HEAD_DIM)
    tm = min(TOK_TILE, s)
    out = jax.ShapeDtypeStruct((b, half, s), F32)
    return pl.pallas_call(
        _rope_tab_kernel,
        grid=(b, s // tm),
        in_specs=[pl.BlockSpec((1, 1, tm), lambda i, j: (i, 0, j)),
                  pl.BlockSpec((half, 1), lambda i, j: (0, 0))],
        out_specs=[pl.BlockSpec((1, half, tm), lambda i, j: (i, 0, j))] * 2,
        out_shape=[out, out],
        compiler_params=_params(("parallel", "parallel")),
        name="rope_tables",
    )(positions.reshape(b, 1, s), inv_freq.reshape(half, 1))


def _rms_rows(x, gain_row):
    y = x * lax.rsqrt(jnp.mean(x * x, axis=-1, keepdims=True) + EPS)
    return y * gain_row


def _prenorm_kernel(x_ref, g_ref, h_ref):
    h_ref[0] = _rms_rows(x_ref[0], g_ref[...]).astype(BF)


def _prenorm(x, gain):
    b, s, d = x.shape
    tm = min(TOK_TILE, s)
    return pl.pallas_call(
        _prenorm_kernel,
        grid=(b, s // tm),
        in_specs=[pl.BlockSpec((1, tm, d), lambda i, j: (i, j, 0)),
                  pl.BlockSpec((1, d), lambda i, j: (0, 0))],
        out_specs=pl.BlockSpec((1, tm, d), lambda i, j: (i, j, 0)),
        out_shape=jax.ShapeDtypeStruct((b, s, d), BF),
        compiler_params=_params(("parallel", "parallel")),
        name="prenorm",
    )(x, gain.reshape(1, d))


def _head_rms_t(y, gain_col):
    ms = jnp.mean(y * y, axis=0, keepdims=True)
    return (y * lax.rsqrt(ms + EPS)) * gain_col


def _rope_t(y, cos, sin):
    half = HEAD_DIM // 2
    x1, x2 = y[:half], y[half:]
    return jnp.concatenate([x1 * cos - x2 * sin, x2 * cos + x1 * sin], axis=0)


def _to_token_major(y):
    pad = jnp.zeros((KEY_PAD - y.shape[0], y.shape[1]), y.dtype)
    return jnp.concatenate([y, pad], axis=0).T


def _silu(z):
    return z * (1.0 / (1.0 + jnp.exp(-z)))


def _sigmoid(z):
    return 1.0 / (1.0 + jnp.exp(-z))


def _store_lane_tiles(out_ref, y, tile):
    for c in range(y.shape[1] // tile):
        out_ref[0, c] = y[:, c * tile:(c + 1) * tile]


def _nsa_proj_kernel(h_ref, cos_ref, sin_ref, wq_ref, wk_ref, wv_ref, wg_ref, wz_ref,
                     qg_ref, kg_ref,
                     q_out, kc_out, vc_out, ks_out, kw_out, vs_out, vw_out, g_out, z_out,
                     y_ref, *, slc_tile, win_tile):
    h = h_ref[0]
    cos, sin = cos_ref[0], sin_ref[0]
    g = NSA_KV_HEADS
    kvd = g * HEAD_DIM
    y_ref[...] = _dot_nt(wq_ref[...], h)
    for hd in range(N_HEADS):
        rows = slice(hd * HEAD_DIM, (hd + 1) * HEAD_DIM)
        y = _rope_t(_head_rms_t(y_ref[rows, :], qg_ref[...]), cos, sin) * Q_SCALE
        q_out[0, rows, :] = y.astype(BF)
    y_ref[0:3 * kvd, :] = _dot_nt(wk_ref[...], h)
    tm = h.shape[0]
    tok = pl.program_id(1) * tm + lax.broadcasted_iota(jnp.int32, (tm, 1), 0)
    blk_lane = HEAD_DIM + lax.shift_right_logical(tok, int(np.log2(NSA_SLC_LEN)))
    blk_hot = lax.broadcasted_iota(jnp.int32, (1, SLC_KEY_LANES), 1) == blk_lane
    for kind in range(3):
        for gi in range(g):
            r0 = (kind * g + gi) * HEAD_DIM
            y = y_ref[r0:r0 + HEAD_DIM, :]
            if kind > 0:
                y = _head_rms_t(y, kg_ref[:, kind:kind + 1])
            yt = _to_token_major(_rope_t(y, cos, sin))
            if kind == 0:
                kc_out[0, gi] = yt[:, :HEAD_DIM]
            elif kind == 1:
                wide = jnp.concatenate([yt, jnp.zeros((tm, SLC_KEY_LANES - KEY_PAD), F32)], axis=1)
                ks_out[0, gi] = jnp.where(blk_hot, 1.0, wide).astype(BF)
            else:
                kw_out[0, gi] = yt.astype(BF)
    y_ref[0:3 * kvd, :] = _dot_nt(wv_ref[...], h)
    for gi in range(g):
        r0 = gi * HEAD_DIM
        vc_out[0, gi] = _to_token_major(y_ref[r0:r0 + HEAD_DIM, :])[:, :HEAD_DIM]
    _store_lane_tiles(vs_out, y_ref[kvd:2 * kvd, :].astype(BF), slc_tile)
    _store_lane_tiles(vw_out, y_ref[2 * kvd:3 * kvd, :].astype(BF), win_tile)
    g_out[0] = _sigmoid(_dot_nt(wg_ref[...], h))
    z_out[0] = _silu(_dot(h, wz_ref[...])).astype(BF)


def _full(shape):
    nd = len(shape)
    return pl.BlockSpec(shape, lambda i, j, _n=nd: (0,) * _n)


def _nsa_proj(h, cos, sin, w_in, q_gain, k_gain, *, slc_tile, win_tile):
    b, s, d = h.shape
    g = NSA_KV_HEADS
    kvd = g * HEAD_DIM
    tm = min(TOK_TILE, s)
    sizes = [MIX_WIDTH] + [kvd] * 6 + [3 * N_HEADS]
    off = np.cumsum([0] + sizes)
    wt = w_in.T.astype(BF)
    wq = wt[off[0]:off[1]]
    wk = jnp.concatenate([wt[off[1]:off[2]], wt[off[3]:off[4]], wt[off[5]:off[6]]], axis=0)
    wv = jnp.concatenate([wt[off[2]:off[3]], wt[off[4]:off[5]], wt[off[6]:off[7]]], axis=0)
    wg = wt[off[7]:off[8]]
    wz = w_in[:, off[8]:].astype(BF)
    qg = q_gain.reshape(HEAD_DIM, 1)
    kg = k_gain.T
    n_t = s // tm
    out_shape = [
        jax.ShapeDtypeStruct((b, MIX_WIDTH, s), BF),
        jax.ShapeDtypeStruct((b, g, s, HEAD_DIM), F32),
        jax.ShapeDtypeStruct((b, g, s, HEAD_DIM), F32),
        jax.ShapeDtypeStruct((b, g, s, SLC_KEY_LANES), BF),
        jax.ShapeDtypeStruct((b, g, s, KEY_PAD), BF),
        jax.ShapeDtypeStruct((b, s // slc_tile, kvd, slc_tile), BF),
        jax.ShapeDtypeStruct((b, s // win_tile, kvd, win_tile), BF),
        jax.ShapeDtypeStruct((b, 3 * N_HEADS, s), F32),
        jax.ShapeDtypeStruct((b, s, MIX_WIDTH), BF),
    ]
    out_specs = [
        pl.BlockSpec((1, MIX_WIDTH, tm), lambda i, j: (i, 0, j)),
        pl.BlockSpec((1, g, tm, HEAD_DIM), lambda i, j: (i, 0, j, 0)),
        pl.BlockSpec((1, g, tm, HEAD_DIM), lambda i, j: (i, 0, j, 0)),
        pl.BlockSpec((1, g, tm, SLC_KEY_LANES), lambda i, j: (i, 0, j, 0)),
        pl.BlockSpec((1, g, tm, KEY_PAD), lambda i, j: (i, 0, j, 0)),
        pl.BlockSpec((1, tm // slc_tile, kvd, slc_tile), lambda i, j: (i, j, 0, 0)),
        pl.BlockSpec((1, tm // win_tile, kvd, win_tile), lambda i, j: (i, j, 0, 0)),
        pl.BlockSpec((1, 3 * N_HEADS, tm), lambda i, j: (i, 0, j)),
        pl.BlockSpec((1, tm, MIX_WIDTH), lambda i, j: (i, j, 0)),
    ]
    in_specs = [
        pl.BlockSpec((1, tm, d), lambda i, j: (i, j, 0)),
        pl.BlockSpec((1, HEAD_DIM // 2, tm), lambda i, j: (i, 0, j)),
        pl.BlockSpec((1, HEAD_DIM // 2, tm), lambda i, j: (i, 0, j)),
        _full(wq.shape), _full(wk.shape), _full(wv.shape), _full(wg.shape), _full(wz.shape),
        _full(qg.shape), _full(kg.shape),
    ]
    return pl.pallas_call(
        functools.partial(_nsa_proj_kernel, slc_tile=slc_tile, win_tile=win_tile),
        grid=(b, n_t), in_specs=in_specs, out_specs=out_specs, out_shape=out_shape,
        scratch_shapes=[pltpu.VMEM((MIX_WIDTH, tm), F32)],
        compiler_params=_params(("parallel", "parallel")),
        name="nsa_proj",
    )(h, cos, sin, wq, wk, wv, wg, wz, qg, kg)


def _gelu_tanh(x):
    c = np.float32(np.sqrt(2.0 / np.pi))
    return 0.5 * x * (1.0 + jnp.tanh(c * (x + 0.044715 * (x * x * x))))


def _compress_kernel(kc_ref, vc_ref, pos_ref, w1_ref, w2_ref, kg_ref, kcmp_out, vcmp_out):
    for which, (src, dst) in enumerate(((kc_ref, kcmp_out), (vc_ref, vcmp_out))):
        x = src[0, 0]
        n = x.shape[0]
        half = x.shape[1]
        xa = (x + pos_ref[which, 0:1, :]).astype(BF)
        xb = (x + pos_ref[which, 1:2, :]).astype(BF)
        ua = _dot(xa, w1_ref[which, :half, :])
        ub = _dot(xb, w1_ref[which, half:, :])
        row = lax.broadcasted_iota(jnp.int32, (n, 1), 0)
        ub_next = jnp.where(row == n - 1, 0.0, pltpu.roll(ub, n - 1, 0))
        hid = _gelu_tanh(ua + ub_next)
        y = _dot(hid.astype(BF), w2_ref[which])
        if which == 0:
            y = _rms_rows(y, kg_ref[...])
            dst[0, 0] = y.astype(BF)
        else:
            pad = jnp.zeros((n, KEY_PAD - HEAD_DIM), F32)
            dst[0, 0] = jnp.concatenate([y, pad], axis=1).T[:HEAD_DIM].astype(BF)


def _compress(kc_tok, vc_tok, cmp_pos, cmp_w1, cmp_w2, k_gain0):
    b, g, s, _ = kc_tok.shape
    n_chunk = s // NSA_CMP_STRIDE
    flat = NSA_CMP_STRIDE * HEAD_DIM
    kc = kc_tok.reshape(b, g, n_chunk, flat)
    vc = vc_tok.reshape(b, g, n_chunk, flat)
    pos = cmp_pos.reshape(2, 2, flat)
    w1 = cmp_w1.astype(BF)
    w2 = cmp_w2.astype(BF)
    blk = pl.BlockSpec((1, 1, n_chunk, flat), lambda i, j: (i, j, 0, 0))
    return pl.pallas_call(
        _compress_kernel,
        grid=(b, g),
        in_specs=[blk, blk, _full(pos.shape), _full(w1.shape), _full(w2.shape),
                  _full((1, HEAD_DIM))],
        out_specs=[pl.BlockSpec((1, 1, n_chunk, HEAD_DIM), lambda i, j: (i, j, 0, 0)),
                   pl.BlockSpec((1, 1, HEAD_DIM, n_chunk), lambda i, j: (i, j, 0, 0))],
        out_shape=[jax.ShapeDtypeStruct((b, g, n_chunk, HEAD_DIM), BF),
                   jax.ShapeDtypeStruct((b, g, HEAD_DIM, n_chunk), BF)],
        compiler_params=_params(("parallel", "parallel")),
        name="nsa_compress",
    )(kc, vc, pos, w1, w2, k_gain0.reshape(1, HEAD_DIM))


def _gate_row(gate_ref, n_heads, grp=0):
    return jnp.concatenate([gate_ref[0, 0, grp, r:r + 1, :] for r in range(n_heads)], axis=1)


def _cmp_branch(rows, q_ref, kc_ref, vc_ref, ov_ref, gate_ref, o_out, sel_out, *, tq, n_blk, n_grp):
    r_heads = N_HEADS // NSA_KV_HEADS
    nq = r_heads * tq
    width = r_heads * HEAD_DIM
    q0 = pl.program_id(2) * tq
    n_cmp = kc_ref.shape[2] - 1
    t_row = q0 + (lax.broadcasted_iota(jnp.int32, (1, nq), 1) & (tq - 1))
    lo = max(rows - CMP_MASK_ROWS, 0)
    c_col = lo + lax.broadcasted_iota(jnp.int32, (rows - lo, 1), 0)
    valid = (c_col * NSA_CMP_STRIDE + (NSA_CMP_LEN - 1) <= t_row) & (c_col < n_cmp)
    one_row = jnp.where(lax.broadcasted_iota(jnp.int32, (V_ROWS - HEAD_DIM, rows), 0) == 0, 1.0, 0.0).astype(BF)
    n_live = min(n_blk, rows * NSA_CMP_STRIDE // NSA_SLC_LEN + 8)
    t1 = q0 + lax.broadcasted_iota(jnp.int32, (1, tq), 1)
    cur = lax.shift_right_logical(t1, int(np.log2(NSA_SLC_LEN)))
    blk = lax.broadcasted_iota(jnp.int32, (n_live, tq), 0)
    forced = (blk == 0) | (blk == cur) | (blk == cur - 1)
    imps = []
    for gi in range(n_grp):
        q4 = jnp.concatenate([q_ref[0, (gi * r_heads + r) * HEAD_DIM:(gi * r_heads + r + 1) * HEAD_DIM, :]
                              for r in range(r_heads)], axis=1)
        s = _dot(kc_ref[0, gi, 0:rows, :], q4)
        s_new = jnp.where(valid, s[lo:], NEG_INF)
        m = jnp.max(s_new, axis=0, keepdims=True)
        if lo:
            m = jnp.maximum(m, jnp.max(s[:lo], axis=0, keepdims=True))
        e = jnp.where(valid, jnp.exp2(s_new - m), 0.0).astype(BF)
        if lo:
            e = jnp.concatenate([jnp.exp2(s[:lo] - m).astype(BF), e], axis=0)
        lhs = jnp.concatenate([vc_ref[0, gi, :, 0:rows], one_row, ov_ref[0:n_live, 0:rows]], axis=0)
        res = _dot(lhs, e)
        l = res[HEAD_DIM:HEAD_DIM + 1]
        inv = jnp.where(l > 0.0, 1.0 / jnp.where(l > 0.0, l, 1.0), 0.0)
        o = res[:HEAD_DIM] * (inv * _gate_row(gate_ref, r_heads, gi))
        o_rows = jnp.concatenate([o[:, r * tq:(r + 1) * tq] for r in range(r_heads)], axis=0)
        o_out[0, :, gi * width:(gi + 1) * width] = o_rows.T.astype(o_out.dtype)
        w = res[V_ROWS:V_ROWS + n_live] * inv
        imp = w[:, 0:tq]
        for r in range(1, r_heads):
            imp = imp + w[:, r * tq:(r + 1) * tq]
        imps.append(jnp.where(forced, BIG, jnp.where(blk > cur, NEG_INF, imp)))
    for _ in range(min(NSA_TOPK, n_blk)):
        for gi in range(n_grp):
            best = jnp.max(imps[gi], axis=0, keepdims=True)
            first = jnp.min(jnp.where(imps[gi] == best, blk, n_blk), axis=0, keepdims=True)
            imps[gi] = jnp.where(blk == first, -jnp.inf, imps[gi])
    for gi in range(n_grp):
        sel_out[0, gi, 0:n_live, :] = jnp.where(imps[gi] == -jnp.inf, 0.0, SEL_OFF).astype(BF)
        if n_live < n_blk:
            sel_out[0, gi, n_live:n_blk, :] = jnp.full((n_blk - n_live, tq), SEL_OFF, BF)


def _cmp_select_kernel(q_ref, kc_ref, vc_ref, ov_ref, gate_ref, o_out, sel_out, *, tq, n_blk, n_grp):
    q0 = pl.program_id(2) * tq
    n_chunk = kc_ref.shape[2]
    n_need = jnp.minimum((q0 + tq - NSA_CMP_LEN) // NSA_CMP_STRIDE + 1, n_chunk - 1)
    n_steps = n_chunk // CMP_BLOCK
    need_steps = (n_need + CMP_BLOCK - 1) // CMP_BLOCK
    for k in range(1, n_steps + 1):
        @pl.when(need_steps == k)
        def _(k=k):
            _cmp_branch(k * CMP_BLOCK, q_ref, kc_ref, vc_ref, ov_ref, gate_ref, o_out, sel_out,
                        tq=tq, n_blk=n_blk, n_grp=n_grp)


def _overlap_matrix(s):
    n_chunk = s // NSA_CMP_STRIDE
    n_blk = s // NSA_SLC_LEN
    c0 = np.arange(n_chunk) * NSA_CMP_STRIDE
    c1 = c0 + NSA_CMP_LEN - 1
    b0 = np.arange(n_blk) * NSA_SLC_LEN
    ov = np.minimum(c1[None, :], b0[:, None] + NSA_SLC_LEN - 1) - np.maximum(c0[None, :], b0[:, None]) + 1
    return jnp.asarray(np.clip(ov, 0, None) / NSA_CMP_LEN, BF)


def _cmp_select(qT, k_cmp, v_cmpT, gates5, *, tq):
    b, _, s = qT.shape
    g = NSA_KV_HEADS
    r_heads = N_HEADS // g
    n_chunk = k_cmp.shape[2]
    n_blk = s // NSA_SLC_LEN
    ov = _overlap_matrix(s)
    n_grp = CMP_GROUPS
    rows = n_grp * r_heads * HEAD_DIM
    return pl.pallas_call(
        functools.partial(_cmp_select_kernel, tq=tq, n_blk=n_blk, n_grp=n_grp),
        grid=(b, g // n_grp, s // tq),
        in_specs=[
            pl.BlockSpec((1, rows, tq), lambda i, j, k: (i, j, k)),
            pl.BlockSpec((1, n_grp, n_chunk, HEAD_DIM), lambda i, j, k: (i, j, 0, 0)),
            pl.BlockSpec((1, n_grp, HEAD_DIM, n_chunk), lambda i, j, k: (i, j, 0, 0)),
            pl.BlockSpec((n_blk, n_chunk), lambda i, j, k: (0, 0)),
            pl.BlockSpec((1, 1, n_grp, r_heads, tq), lambda i, j, k: (i, 0, j, 0, k)),
        ],
        out_specs=[pl.BlockSpec((1, tq, rows), lambda i, j, k: (i, k, j)),
                   pl.BlockSpec((1, n_grp, n_blk, tq), lambda i, j, k: (i, j, 0, k))],
        out_shape=[jax.ShapeDtypeStruct((b, s, MIX_WIDTH), BF),
                   jax.ShapeDtypeStruct((b, g, n_blk, s), BF)],
        compiler_params=_params(("parallel", "parallel", "parallel")),
        name="nsa_cmp_select",
    )(qT, k_cmp, v_cmpT, ov, gates5)


def _flash_kernel(*refs, mode, n_par, n_rep, dq, kc, tq, tk, cw, window, n_blk, has_sink, has_gate):
    it = iter(refs)
    q_ref, k_ref, v_ref = next(it), next(it), next(it)
    sel_ref = next(it) if n_blk else None
    sink_ref = next(it) if has_sink else None
    gate_ref = next(it) if has_gate else None
    qnext_ref = next(it) if mode == "causal" else None
    selnext_ref = next(it) if (mode == "causal" and n_blk) else None
    out_ref = next(it)
    qs_ref, m_ref, acc_ref, mt_ref = (next(it) for _ in range(4))
    qn_ref = next(it) if mode == "causal" else None
    s_bufs = tuple(it)

    nq = n_rep * tq
    items = [(p, c) for p in range(n_par) for c in range(nq // cw)]
    n_items = len(items)
    grp = pl.program_id(1)
    q0 = pl.program_id(2) * tq
    acc_row = lax.broadcasted_iota(jnp.int32, (V_ROWS, nq), 0)
    for p in range(n_par):
        for r in range(n_rep):
            hd = p * n_rep + r
            qs_ref[p, 0:dq, r * tq:(r + 1) * tq] = q_ref[0, hd * dq:(hd + 1) * dq, :]
        if n_blk:
            qs_ref[p, dq:dq + n_blk, :] = jnp.concatenate([sel_ref[0, 0]] * n_rep, axis=1)
            if dq + n_blk < kc:
                qs_ref[p, dq + n_blk:kc, :] = jnp.zeros((kc - dq - n_blk, nq), BF)
        if has_sink:
            m_ref[p] = jnp.concatenate(
                [jnp.full((1, tq), sink_ref[(grp * n_par + p) * n_rep + r] * LOG2E, F32) for r in range(n_rep)],
                axis=1)
            acc_ref[p] = jnp.where(acc_row == HEAD_DIM, 1.0, 0.0)
        else:
            m_ref[p] = jnp.full((1, nq), M_INIT, F32)
            acc_ref[p] = jnp.zeros((V_ROWS, nq), F32)

    t_row = q0 + (lax.broadcasted_iota(jnp.int32, (1, nq), 1) & (tq - 1))
    one_row = jnp.where(lax.broadcasted_iota(jnp.int32, (V_ROWS - HEAD_DIM, tk), 0) == 0, 1.0, 0.0).astype(BF)

    def stage_a(item, j, key0, slot, kind):
        p, c = item
        cols = slice(c * cw, (c + 1) * cw)
        s = _dot(k_ref[0, p, j][:, :kc], qs_ref[p, :, cols])
        if kind is not None:
            key = key0 + lax.broadcasted_iota(jnp.int32, (tk, 1), 0)
            ok = key <= t_row[:, cols] if kind == "causal" else key > t_row[:, cols] - window
            s = jnp.where(ok, s, NEG_INF)
        s_bufs[slot][...] = s
        mt_ref[slot] = jnp.max(s, axis=0, keepdims=True)

    def stage_b(item, j, slot):
        p, c = item
        cols = slice(c * cw, (c + 1) * cw)
        m_old = m_ref[p, :, cols]
        m_new = jnp.maximum(m_old, mt_ref[slot])
        pr = jnp.exp2(s_bufs[slot][...] - m_new).astype(BF)
        alpha = jnp.exp2(m_old - m_new)
        v = jnp.concatenate([v_ref[0, j, p * HEAD_DIM:(p + 1) * HEAD_DIM, :], one_row], axis=0)
        acc_ref[p, :, cols] = alpha * acc_ref[p, :, cols] + _dot(v, pr)
        m_ref[p, :, cols] = m_new

    def finalize():
        outs = []
        for p in range(n_par):
            acc = acc_ref[p]
            o = acc[:HEAD_DIM] * (1.0 / acc[HEAD_DIM:HEAD_DIM + 1])
            if has_gate:
                o = o * _gate_row(gate_ref, n_rep, p)
            outs += [o[:, r * tq:(r + 1) * tq] for r in range(n_rep)]
        out_ref[0] = jnp.concatenate(outs, axis=0).T.astype(out_ref.dtype)

    if mode == "causal":
        assert n_items % 2 == 0
        n_full = q0 // tk

        def prefetch_next_tile():
            assert cw == tq
            qn_ref[0:dq, :] = qnext_ref[0, 0:dq, :]
            if n_blk:
                qn_ref[dq:dq + n_blk, :] = selnext_ref[0, 0]
                if dq + n_blk < kc:
                    qn_ref[dq + n_blk:kc, :] = jnp.zeros((kc - dq - n_blk, tq), BF)
            s = _dot(k_ref[0, 0, 0][:, :kc], qn_ref[...])
            key = lax.broadcasted_iota(jnp.int32, (tk, 1), 0)
            s = jnp.where(key <= t_row[:, 0:cw] + tq, s, NEG_INF)
            s_bufs[0][...] = s
            mt_ref[0] = jnp.max(s, axis=0, keepdims=True)

        def step(j, kind, next_kind, last):
            for idx, item in enumerate(items):
                slot = idx % 2
                if idx + 1 < n_items:
                    stage_a(items[idx + 1], j, j * tk, 1 - slot, kind)
                elif not last:
                    stage_a(items[0], j + 1, (j + 1) * tk, 1 - slot, next_kind)
                else:
                    prefetch_next_tile()
                stage_b(item, j, slot)

        @pl.when(pl.program_id(2) == 0)
        def _():
            stage_a(items[0], 0, 0, 0, "causal")

        def body(j, carry):
            step(j, None, None, False)
            return carry

        def body_group(i, carry):
            for u in range(FLASH_UNROLL):
                step(FLASH_UNROLL * i + u, None, None, False)
            return carry

        n_main = jnp.maximum(n_full - 1, 0)
        n_groups = lax.shift_right_logical(n_main, int(np.log2(FLASH_UNROLL)))
        lax.fori_loop(0, n_groups, body_group, 0)
        lax.fori_loop(FLASH_UNROLL * n_groups, n_main, body, 0)

        @pl.when(n_full >= 1)
        def _():
            step(n_full - 1, None, "causal", False)
            step(n_full, "causal", None, True)
            finalize()

        @pl.when(n_full == 0)
        def _():
            step(0, "causal", None, True)
            finalize()
    else:
        w_tiles, q_tiles = window // tk, tq // tk
        work = []
        for i in range(w_tiles + q_tiles):
            jv = q0 // tk - w_tiles + i
            key0 = jnp.where(jv < 0, -(1 << 30), jv * tk)
            kind = "causal" if i >= w_tiles else "lower"
            work += [(item, jnp.maximum(jv, 0), key0, kind) for item in items]
        n_buf = len(s_bufs)
        ahead = WINDOW_LOOKAHEAD
        for n in range(ahead):
            stage_a(work[n][0], work[n][1], work[n][2], n % n_buf, work[n][3])
        for n, (item, j, _, _) in enumerate(work):
            if n + ahead < len(work):
                nxt = work[n + ahead]
                stage_a(nxt[0], nxt[1], nxt[2], (n + ahead) % n_buf, nxt[3])
            stage_b(item, j, n % n_buf)

        finalize()


def _flash(qT, k_tok, vT_tiles, *, mode, n_par, n_rep, dq, kc, tq, tk, cw, window=None,
           sel_bias=None, sinks=None, gates=None, gate_branch=0, name="flash"):
    b, _, s = qT.shape
    kh, k_lanes = k_tok.shape[1], k_tok.shape[3]
    n_t = s // tk
    k5 = k_tok.reshape(b, kh, n_t, tk, k_lanes)
    n_grp = kh // n_par
    heads = n_par * n_rep
    nq = n_rep * tq
    n_blk = sel_bias.shape[2] if sel_bias is not None else 0
    has_sink, has_gate = sinks is not None, gates is not None
    args = [qT, k5, vT_tiles]
    in_specs = [
        pl.BlockSpec((1, heads * dq, tq), lambda i, j, k: (i, j, k)),
        pl.BlockSpec((1, n_par, n_t, tk, k_lanes), lambda i, j, k: (i, j, 0, 0, 0)),
        pl.BlockSpec((1, n_t, n_par * HEAD_DIM, tk), lambda i, j, k: (i, 0, j, 0)),
    ]
    if n_blk:
        assert dq + n_blk <= kc
        args.append(sel_bias)
        in_specs.append(pl.BlockSpec((1, 1, n_blk, tq), lambda i, j, k: (i, j, 0, k)))
    if has_sink:
        args.append(sinks.astype(F32))
        in_specs.append(pl.BlockSpec(memory_space=pltpu.SMEM))
    if has_gate:
        args.append(gates)
        in_specs.append(pl.BlockSpec((1, 1, n_par, n_rep, tq),
                                     lambda i, j, k, _br=gate_branch: (i, _br, j, 0, k)))
    n_sbuf = 2 if mode == "causal" else WINDOW_SCORE_BUFS
    scratch = [pltpu.VMEM((n_par, kc, nq), BF), pltpu.VMEM((n_par, 1, nq), F32),
               pltpu.VMEM((n_par, V_ROWS, nq), F32), pltpu.VMEM((n_sbuf, 1, cw), F32)]
    if mode == "causal":
        last_q = s // tq - 1
        args.append(qT)
        in_specs.append(pl.BlockSpec((1, dq, tq), lambda i, j, k: (i, j * heads, jnp.minimum(k + 1, last_q))))
        if n_blk:
            args.append(sel_bias)
            in_specs.append(pl.BlockSpec((1, 1, n_blk, tq), lambda i, j, k: (i, j, 0, jnp.minimum(k + 1, last_q))))
        scratch.append(pltpu.VMEM((kc, tq), BF))
    scratch += [pltpu.VMEM((tk, cw), F32)] * n_sbuf
    kern = functools.partial(_flash_kernel, mode=mode, n_par=n_par, n_rep=n_rep, dq=dq, kc=kc, tq=tq, tk=tk,
                             cw=cw, window=window, n_blk=n_blk, has_sink=has_sink, has_gate=has_gate)
    return pl.pallas_call(
        kern,
        grid=(b, n_grp, s // tq),
        in_specs=in_specs,
        out_specs=pl.BlockSpec((1, tq, heads * HEAD_DIM), lambda i, j, k: (i, k, j)),
        out_shape=jax.ShapeDtypeStruct((b, s, MIX_WIDTH), BF),
        scratch_shapes=scratch,
        compiler_params=_params(("parallel", "parallel", "arbitrary")),
        name=name,
    )(*args)


def _swa_proj_kernel(h_ref, cos_ref, sin_ref, wq_ref, wk_ref, wv_ref, wz_ref, qg_ref, kg_ref,
                     q_out, k_out, v_out, z_out, y_ref, *, win_tile):
    h = h_ref[0]
    cos, sin = cos_ref[0], sin_ref[0]
    y_ref[...] = _dot_nt(wq_ref[...], h)
    for hd in range(N_HEADS):
        rows = slice(hd * HEAD_DIM, (hd + 1) * HEAD_DIM)
        q_out[0, rows, :] = (_rope_t(_head_rms_t(y_ref[rows, :], qg_ref[...]), cos, sin) * Q_SCALE).astype(BF)
    kvd = SWA_KV_HEADS * HEAD_DIM
    y_ref[0:kvd, :] = _dot_nt(wk_ref[...], h)
    for gi in range(SWA_KV_HEADS):
        rows = slice(gi * HEAD_DIM, (gi + 1) * HEAD_DIM)
        k_out[0, gi] = _to_token_major(_rope_t(_head_rms_t(y_ref[rows, :], kg_ref[...]), cos, sin)).astype(BF)
    _store_lane_tiles(v_out, _dot_nt(wv_ref[...], h).astype(BF), win_tile)
    z_out[0] = _silu(_dot(h, wz_ref[...])).astype(BF)


def _swa_proj(h, cos, sin, w_in, q_gain, k_gain, *, win_tile):
    b, s, d = h.shape
    g = SWA_KV_HEADS
    kvd = g * HEAD_DIM
    tm = min(TOK_TILE, s)
    wt = w_in.T.astype(BF)
    wq, wk, wv = wt[:MIX_WIDTH], wt[MIX_WIDTH:MIX_WIDTH + kvd], wt[MIX_WIDTH + kvd:MIX_WIDTH + 2 * kvd]
    wz = w_in[:, MIX_WIDTH + 2 * kvd:].astype(BF)
    qg, kg = q_gain.reshape(HEAD_DIM, 1), k_gain.reshape(HEAD_DIM, 1)
    wpt = tm // win_tile
    half = HEAD_DIM // 2
    return pl.pallas_call(
        functools.partial(_swa_proj_kernel, win_tile=win_tile),
        grid=(b, s // tm),
        in_specs=[pl.BlockSpec((1, tm, d), lambda i, j: (i, j, 0)),
                  pl.BlockSpec((1, half, tm), lambda i, j: (i, 0, j)),
                  pl.BlockSpec((1, half, tm), lambda i, j: (i, 0, j)),
                  _full(wq.shape), _full(wk.shape), _full(wv.shape), _full(wz.shape),
                  _full(qg.shape), _full(kg.shape)],
        out_specs=[pl.BlockSpec((1, MIX_WIDTH, tm), lambda i, j: (i, 0, j)),
                   pl.BlockSpec((1, g, tm, KEY_PAD), lambda i, j: (i, 0, j, 0)),
                   pl.BlockSpec((1, wpt, kvd, win_tile), lambda i, j: (i, j, 0, 0)),
                   pl.BlockSpec((1, tm, MIX_WIDTH), lambda i, j: (i, j, 0))],
        out_shape=[jax.ShapeDtypeStruct((b, MIX_WIDTH, s), BF),
                   jax.ShapeDtypeStruct((b, g, s, KEY_PAD), BF),
                   jax.ShapeDtypeStruct((b, s // win_tile, kvd, win_tile), BF),
                   jax.ShapeDtypeStruct((b, s, MIX_WIDTH), BF)],
        scratch_shapes=[pltpu.VMEM((MIX_WIDTH, tm), F32)],
        compiler_params=_params(("parallel", "parallel")),
        name="swa_proj",
    )(h, cos, sin, wq, wk, wv, wz, qg, kg)


def _split3(x):
    hi = x.astype(BF)
    r1 = x - hi.astype(F32)
    mid = r1.astype(BF)
    lo = (r1 - mid.astype(F32)).astype(BF)
    return hi, mid, lo


def _forget_cum_kernel(h_ref, wf_ref, bias_ref, tri_ref, cum_out, carry_ref):
    @pl.when(pl.program_id(1) == 0)
    def _():
        carry_ref[...] = jnp.zeros_like(carry_ref)

    x = _dot_nt(wf_ref[...], h_ref[0]) + bias_ref[...]
    logf = jnp.minimum(x, 0.0) - jnp.log(1.0 + jnp.exp(-jnp.abs(x)))
    tri = tri_ref[...]
    hi, mid, lo = _split3(logf)
    cum = (_dot(hi, tri) + _dot(mid, tri)) + _dot(lo, tri) + carry_ref[:, 0:1]
    cum_out[0] = cum
    carry_ref[...] = jnp.broadcast_to(cum[:, -1:], carry_ref.shape)


def _forget_cum(h, wf, bias):
    b, s, d = h.shape
    tm = min(TOK_TILE, s)
    tri = jnp.asarray(np.arange(tm)[:, None] <= np.arange(tm)[None, :], BF)
    return pl.pallas_call(
        _forget_cum_kernel,
        grid=(b, s // tm),
        in_specs=[pl.BlockSpec((1, tm, d), lambda i, j: (i, j, 0)),
                  _full(wf.shape), _full((N_HEADS, 1)), _full(tri.shape)],
        out_specs=pl.BlockSpec((1, N_HEADS, tm), lambda i, j: (i, 0, j)),
        out_shape=jax.ShapeDtypeStruct((b, N_HEADS, s), F32),
        scratch_shapes=[pltpu.VMEM((N_HEADS, LANES), F32)],
        compiler_params=_params(("parallel", "arbitrary")),
        name="fox_forget_cum",
    )(h, wf, bias.reshape(N_HEADS, 1).astype(F32), tri)


def _fox_proj_kernel(h_ref, cum_ref, wq_ref, wk_ref, wv_ref, wz_ref, qg_ref, kg_ref,
                     q_out, k_out, v_out, z_out, yq_ref, yk_ref, *, tile):
    h = h_ref[0]
    tm = h.shape[0]
    yq_ref[...] = _dot_nt(wq_ref[...], h)
    yk_ref[...] = _dot_nt(wk_ref[...], h)
    row = lax.broadcasted_iota(jnp.int32, (8, tm), 0)
    zeros = jnp.zeros((KEY_PAD - HEAD_DIM - 16, tm), F32)
    for hd in range(N_HEADS):
        rows = slice(hd * HEAD_DIM, (hd + 1) * HEAD_DIM)
        c_hi, c_mid, c_lo = (c.astype(F32) for c in _split3(cum_ref[0, hd:hd + 1, :] * LOG2E))
        c3 = jnp.where(row == 0, c_hi, jnp.where(row == 1, c_mid, jnp.where(row == 2, c_lo, 0.0)))
        one3 = jnp.where(row < 3, 1.0, 0.0)
        q = _head_rms_t(yq_ref[rows, :], qg_ref[...]) * Q_SCALE
        q_out[0, hd] = jnp.concatenate([q, one3, c3, zeros], axis=0).astype(BF)
        k = _head_rms_t(yk_ref[rows, :], kg_ref[...])
        k_out[0, hd] = jnp.concatenate([k, -c3, one3, zeros], axis=0).T.astype(BF)
    _store_lane_tiles(v_out, _dot_nt(wv_ref[...], h).astype(BF), tile)
    z_out[0] = _silu(_dot(h, wz_ref[...])).astype(BF)


def _fox_proj(h, cum, w_in, q_gain, k_gain, *, tile):
    b, s, d = h.shape
    tm = min(TOK_TILE, s)
    wt = w_in.T.astype(BF)
    wq, wk, wv = wt[:MIX_WIDTH], wt[MIX_WIDTH:2 * MIX_WIDTH], wt[2 * MIX_WIDTH:3 * MIX_WIDTH]
    wz = w_in[:, 3 * MIX_WIDTH + N_HEADS:].astype(BF)
    qg, kg = q_gain.reshape(HEAD_DIM, 1), k_gain.reshape(HEAD_DIM, 1)
    return pl.pallas_call(
        functools.partial(_fox_proj_kernel, tile=tile),
        grid=(b, s // tm),
        in_specs=[pl.BlockSpec((1, tm, d), lambda i, j: (i, j, 0)),
                  pl.BlockSpec((1, N_HEADS, tm), lambda i, j: (i, 0, j)),
                  _full(wq.shape), _full(wk.shape), _full(wv.shape), _full(wz.shape),
                  _full(qg.shape), _full(kg.shape)],
        out_specs=[pl.BlockSpec((1, N_HEADS, KEY_PAD, tm), lambda i, j: (i, 0, 0, j)),
                   pl.BlockSpec((1, N_HEADS, tm, KEY_PAD), lambda i, j: (i, 0, j, 0)),
                   pl.BlockSpec((1, tm // tile, MIX_WIDTH, tile), lambda i, j: (i, j, 0, 0)),
                   pl.BlockSpec((1, tm, MIX_WIDTH), lambda i, j: (i, j, 0))],
        out_shape=[jax.ShapeDtypeStruct((b, N_HEADS, KEY_PAD, s), BF),
                   jax.ShapeDtypeStruct((b, N_HEADS, s, KEY_PAD), BF),
                   jax.ShapeDtypeStruct((b, s // tile, MIX_WIDTH, tile), BF),
                   jax.ShapeDtypeStruct((b, s, MIX_WIDTH), BF)],
        scratch_shapes=[pltpu.VMEM((MIX_WIDTH, tm), F32), pltpu.VMEM((MIX_WIDTH, tm), F32)],
        compiler_params=_params(("parallel", "parallel")),
        name="fox_proj",
    )(h, cum, wq, wk, wv, wz, qg, kg)


def _out_proj_kernel(*refs, n_o, has_next):
    o_refs = refs[:n_o]
    z_ref, x_ref, w_ref = refs[n_o:n_o + 3]
    rest = refs[n_o + 3:]
    if has_next:
        g_ref, x_out, h_out = rest
    else:
        (x_out,) = rest
    o = o_refs[0][0].astype(F32)
    for r in o_refs[1:]:
        o = o + r[0].astype(F32)
    y = _dot((o * z_ref[0].astype(F32)).astype(BF), w_ref[...])
    x_new = x_ref[0] + y
    x_out[0] = x_new
    if has_next:
        h_out[0] = _rms_rows(x_new, g_ref[...]).astype(BF)


def _out_proj(o_list, zs, x, w_out, next_gain):
    b, s, d = x.shape
    tm = min(TOK_TILE, s)
    has_next = next_gain is not None
    blk = pl.BlockSpec((1, tm, d), lambda i, j: (i, j, 0))
    args = list(o_list) + [zs, x, w_out.astype(BF)]
    in_specs = [blk] * (len(o_list) + 2) + [_full(w_out.shape)]
    out_shape = [jax.ShapeDtypeStruct((b, s, d), F32)]
    out_specs = [blk]
    if has_next:
        args.append(next_gain.reshape(1, d))
        in_specs.append(_full((1, d)))
        out_shape.append(jax.ShapeDtypeStruct((b, s, d), BF))
        out_specs.append(blk)
    res = pl.pallas_call(
        functools.partial(_out_proj_kernel, n_o=len(o_list), has_next=has_next),
        grid=(b, s // tm), in_specs=in_specs, out_specs=out_specs, out_shape=out_shape,
        compiler_params=_params(("parallel", "parallel")),
        name="out_proj",
    )(*args)
    return (res[0], res[1]) if has_next else (res[0], None)


ITEM_LANES = 512
CMP_Q_TILE = 512
CMP_GROUPS = 2
NSA_SLC_Q_TILE = 512
NSA_SLC_TILE = 512
NSA_WIN_Q_TILE = 256
NSA_WIN_TILE = 256
SWA_TILE = 128
FOX_Q_TILE = 512
FOX_K_TILE = 512
FOX_HEADS_PER_STEP = 4


def _nsa_mixer(h, cos, sin, w_in, q_gain, k_gain, cmp_pos, cmp_w1, cmp_w2):
    b, s, _ = h.shape
    g = NSA_KV_HEADS
    r = N_HEADS // g
    qT, kc_tok, vc_tok, ks, kw, vsT, vwT, gates, zs = _nsa_proj(
        h, cos, sin, w_in, q_gain, k_gain, slc_tile=NSA_SLC_TILE, win_tile=NSA_WIN_TILE)
    k_cmp, v_cmpT = _compress(kc_tok, vc_tok, cmp_pos, cmp_w1, cmp_w2, k_gain[0])
    gates5 = gates.reshape(b, 3, g, r, s)
    o_cmp, sel_bias = _cmp_select(qT, k_cmp, v_cmpT, gates5, tq=CMP_Q_TILE)
    o_slc = _flash(qT, ks, vsT, mode="causal", n_par=1, n_rep=r, dq=HEAD_DIM, kc=SLC_KEY_LANES,
                   tq=NSA_SLC_Q_TILE, tk=NSA_SLC_TILE, cw=ITEM_LANES, sel_bias=sel_bias,
                   gates=gates5, gate_branch=1, name="nsa_selected")
    o_win = _flash(qT, kw, vwT, mode="window", n_par=g, n_rep=r, dq=HEAD_DIM, kc=HEAD_DIM,
                   tq=NSA_WIN_Q_TILE, tk=NSA_WIN_TILE, cw=ITEM_LANES, window=NSA_WINDOW,
                   gates=gates5, gate_branch=2, name="nsa_window")
    return [o_cmp, o_slc, o_win], zs


def _swa_mixer(h, cos, sin, w_in, q_gain, k_gain, sinks):
    r = N_HEADS // SWA_KV_HEADS
    qT, k, vT, zs = _swa_proj(h, cos, sin, w_in, q_gain, k_gain, win_tile=SWA_TILE)
    o = _flash(qT, k, vT, mode="window", n_par=SWA_KV_HEADS, n_rep=r, dq=HEAD_DIM, kc=HEAD_DIM, tq=SWA_TILE,
               tk=SWA_TILE, cw=r * SWA_TILE, window=SWA_WINDOW, sinks=sinks, name="swa_window")
    return [o], zs


def _fox_mixer(h, w_in, forget_bias, q_gain, k_gain):
    b, s, _ = h.shape
    wf = w_in[:, 3 * MIX_WIDTH:3 * MIX_WIDTH + N_HEADS].T.astype(BF)
    cum = _forget_cum(h, wf, forget_bias)
    qT, k, vT, zs = _fox_proj(h, cum, w_in, q_gain, k_gain, tile=FOX_K_TILE)
    o = _flash(qT.reshape(b, N_HEADS * KEY_PAD, s), k, vT, mode="causal", n_par=FOX_HEADS_PER_STEP, n_rep=1, dq=KEY_PAD,
               kc=KEY_PAD, tq=FOX_Q_TILE, tk=FOX_K_TILE, cw=ITEM_LANES, name="fox_attention")
    return [o], zs


def kernel(x, positions, norm_gains, a_w_in, a_q_gain, a_k_gain, a_cmp_pos, a_cmp_w1, a_cmp_w2, a_w_out,
           b_w_in, b_q_gain, b_k_gain, b_sinks, b_w_out,
           c_w_in, c_forget_bias, c_q_gain, c_k_gain, c_w_out):
    depth = norm_gains.shape[0]
    cos, sin = _rope_tables(positions)
    h = _prenorm(x, norm_gains[0])
    for i in range(depth):
        j, mixer = divmod(i, 3)
        if mixer == 0:
            o_list, zs = _nsa_mixer(h, cos, sin, a_w_in[j], a_q_gain[j], a_k_gain[j],
                                    a_cmp_pos[j], a_cmp_w1[j], a_cmp_w2[j])
            w_out = a_w_out[j]
        elif mixer == 1:
            o_list, zs = _swa_mixer(h, cos, sin, b_w_in[j], b_q_gain[j], b_k_gain[j], b_sinks[j])
            w_out = b_w_out[j]
        else:
            o_list, zs = _fox_mixer(h, c_w_in[j], c_forget_bias[j], c_q_gain[j], c_k_gain[j])
            w_out = c_w_out[j]
        next_gain = norm_gains[i + 1] if i + 1 < depth else None
        x, h = _out_proj(o_list, zs, x, w_out, next_gain)
    return x
```

```python
import functools

import jax
import jax.numpy as jnp
import numpy as np
from jax import lax
from jax.experimental import pallas as pl
from jax.experimental.pallas import tpu as pltpu

D_MODEL = 1024
HEAD_DIM = 64
N_HEADS = 16
MIX_WIDTH = N_HEADS * HEAD_DIM
ROPE_THETA = 10000.0
EPS = 1e-6
SCALE = HEAD_DIM ** -0.5
NEG_INF = -1e30
BIG = 1e30
M_INIT = -1e29
SEL_OFF = -(2.0 ** 100)

NSA_KV_HEADS = 4
NSA_CMP_LEN = 32
NSA_CMP_STRIDE = 16
NSA_SLC_LEN = 64
NSA_TOPK = 16
NSA_WINDOW = 512
SWA_KV_HEADS = 2
SWA_WINDOW = 128

LOG2E = float(np.log2(np.e))
Q_SCALE = SCALE * LOG2E

LANES = 128
KEY_PAD = 128
SLC_KEY_LANES = 256
V_ROWS = 80
FLASH_UNROLL = 4
WINDOW_LOOKAHEAD = 2
WINDOW_SCORE_BUFS = 4
CMP_BLOCK = 128
CMP_MASK_ROWS = 2 * CMP_BLOCK
VMEM_LIMIT = 56 * 1024 * 1024

TOK_TILE = 512
NT_DIMS = (((1,), (1,)), ((), ()))

BF = jnp.bfloat16
F32 = jnp.float32


def _params(sem):
    return pltpu.CompilerParams(dimension_semantics=sem, vmem_limit_bytes=VMEM_LIMIT)


def _dot(a, b):
    return jnp.dot(a, b, preferred_element_type=F32)


def _dot_nt(a, b):
    return lax.dot_general(a, b, NT_DIMS, preferred_element_type=F32)


def _rope_tab_kernel(pos_ref, invf_ref, cos_ref, sin_ref):
    ang = invf_ref[...] * pos_ref[0].astype(F32)
    cos_ref[0] = jnp.cos(ang)
    sin_ref[0] = jnp.sin(ang)


def _rope_tables(positions):
    b, s = positions.shape
    half = HEAD_DIM // 2
    inv_freq = ROPE_THETA ** (-jnp.arange(half, dtype=F32) * 2.0 / HEAD_DIM)
    tm = min(TOK_TILE, s)
    out = jax.ShapeDtypeStruct((b, half, s), F32)
    return pl.pallas_call(
        _rope_tab_kernel,
        grid=(b, s // tm),
        in_specs=[pl.BlockSpec((1, 1, tm), lambda i, j: (i, 0, j)),
                  pl.BlockSpec((half, 1), lambda i, j: (0, 0))],
        out_specs=[pl.BlockSpec((1, half, tm), lambda i, j: (i, 0, j))] * 2,
        out_shape=[out, out],
        compiler_params=_params(("parallel", "parallel")),
        name="rope_tables",
    )(positions.reshape(b, 1, s), inv_freq.reshape(half, 1))


def _rms_rows(x, gain_row):
    y = x * lax.rsqrt(jnp.mean(x * x, axis=-1, keepdims=True) + EPS)
    return y * gain_row


def _prenorm_kernel(x_ref, g_ref, h_ref):
    h_ref[0] = _rms_rows(x_ref[0], g_ref[...]).astype(BF)


def _prenorm(x, gain):
    b, s, d = x.shape
    tm = min(TOK_TILE, s)
    return pl.pallas_call(
        _prenorm_kernel,
        grid=(b, s // tm),
        in_specs=[pl.BlockSpec((1, tm, d), lambda i, j: (i, j, 0)),
                  pl.BlockSpec((1, d), lambda i, j: (0, 0))],
        out_specs=pl.BlockSpec((1, tm, d), lambda i, j: (i, j, 0)),
        out_shape=jax.ShapeDtypeStruct((b, s, d), BF),
        compiler_params=_params(("parallel", "parallel")),
        name="prenorm",
    )(x, gain.reshape(1, d))


def _head_rms_t(y, gain_col):
    ms = jnp.mean(y * y, axis=0, keepdims=True)
    return (y * lax.rsqrt(ms + EPS)) * gain_col


def _rope_t(y, cos, sin):
    half = HEAD_DIM // 2
    x1, x2 = y[:half], y[half:]
    return jnp.concatenate([x1 * cos - x2 * sin, x2 * cos + x1 * sin], axis=0)


def _to_token_major(y):
    pad = jnp.zeros((KEY_PAD - y.shape[0], y.shape[1]), y.dtype)
    return jnp.concatenate([y, pad], axis=0).T


def _silu(z):
    return z * (1.0 / (1.0 + jnp.exp(-z)))


def _sigmoid(z):
    return 1.0 / (1.0 + jnp.exp(-z))


def _store_lane_tiles(out_ref, y, tile):
    for c in range(y.shape[1] // tile):
        out_ref[0, c] = y[:, c * tile:(c + 1) * tile]


def _nsa_proj_kernel(h_ref, cos_ref, sin_ref, wq_ref, wk_ref, wv_ref, wg_ref, wz_ref,
                     qg_ref, kg_ref,
                     q_out, kc_out, vc_out, ks_out, kw_out, vs_out, vw_out, g_out, z_out,
                     y_ref, *, slc_tile, win_tile):
    h = h_ref[0]
    cos, sin = cos_ref[0], sin_ref[0]
    g = NSA_KV_HEADS
    kvd = g * HEAD_DIM
    y_ref[...] = _dot_nt(wq_ref[...], h)
    for hd in range(N_HEADS):
        rows = slice(hd * HEAD_DIM, (hd + 1) * HEAD_DIM)
        y = _rope_t(_head_rms_t(y_ref[rows, :], qg_ref[...]), cos, sin) * Q_SCALE
        q_out[0, rows, :] = y.astype(BF)
    y_ref[0:3 * kvd, :] = _dot_nt(wk_ref[...], h)
    tm = h.shape[0]
    tok = pl.program_id(1) * tm + lax.broadcasted_iota(jnp.int32, (tm, 1), 0)
    blk_lane = HEAD_DIM + lax.shift_right_logical(tok, int(np.log2(NSA_SLC_LEN)))
    blk_hot = lax.broadcasted_iota(jnp.int32, (1, SLC_KEY_LANES), 1) == blk_lane
    for kind in range(3):
        for gi in range(g):
            r0 = (kind * g + gi) * HEAD_DIM
            y = y_ref[r0:r0 + HEAD_DIM, :]
            if kind > 0:
                y = _head_rms_t(y, kg_ref[:, kind:kind + 1])
            yt = _to_token_major(_rope_t(y, cos, sin))
            if kind == 0:
                kc_out[0, gi] = yt[:, :HEAD_DIM]
            elif kind == 1:
                wide = jnp.concatenate([yt, jnp.zeros((tm, SLC_KEY_LANES - KEY_PAD), F32)], axis=1)
                ks_out[0, gi] = jnp.where(blk_hot, 1.0, wide).astype(BF)
            else:
                kw_out[0, gi] = yt.astype(BF)
    y_ref[0:3 * kvd, :] = _dot_nt(wv_ref[...], h)
    for gi in range(g):
        r0 = gi * HEAD_DIM
        vc_out[0, gi] = _to_token_major(y_ref[r0:r0 + HEAD_DIM, :])[:, :HEAD_DIM]
    _store_lane_tiles(vs_out, y_ref[kvd:2 * kvd, :].astype(BF), slc_tile)
    _store_lane_tiles(vw_out, y_ref[2 * kvd:3 * kvd, :].astype(BF), win_tile)
    g_out[0] = _sigmoid(_dot_nt(wg_ref[...], h))
    z_out[0] = _silu(_dot(h, wz_ref[...])).astype(BF)


def _full(shape):
    nd = len(shape)
    return pl.BlockSpec(shape, lambda i, j, _n=nd: (0,) * _n)


def _nsa_proj(h, cos, sin, w_in, q_gain, k_gain, *, slc_tile, win_tile):
    b, s, d = h.shape
    g = NSA_KV_HEADS
    kvd = g * HEAD_DIM
    tm = min(TOK_TILE, s)
    sizes = [MIX_WIDTH] + [kvd] * 6 + [3 * N_HEADS]
    off = np.cumsum([0] + sizes)
    wt = w_in.T.astype(BF)
    wq = wt[off[0]:off[1]]
    wk = jnp.concatenate([wt[off[1]:off[2]], wt[off[3]:off[4]], wt[off[5]:off[6]]], axis=0)
    wv = jnp.concatenate([wt[off[2]:off[3]], wt[off[4]:off[5]], wt[off[6]:off[7]]], axis=0)
    wg = wt[off[7]:off[8]]
    wz = w_in[:, off[8]:].astype(BF)
    qg = q_gain.reshape(HEAD_DIM, 1)
    kg = k_gain.T
    n_t = s // tm
    out_shape = [
        jax.ShapeDtypeStruct((b, MIX_WIDTH, s), BF),
        jax.ShapeDtypeStruct((b, g, s, HEAD_DIM), F32),
        jax.ShapeDtypeStruct((b, g, s, HEAD_DIM), F32),
        jax.ShapeDtypeStruct((b, g, s, SLC_KEY_LANES), BF),
        jax.ShapeDtypeStruct((b, g, s, KEY_PAD), BF),
        jax.ShapeDtypeStruct((b, s // slc_tile, kvd, slc_tile), BF),
        jax.ShapeDtypeStruct((b, s // win_tile, kvd, win_tile), BF),
        jax.ShapeDtypeStruct((b, 3 * N_HEADS, s), F32),
        jax.ShapeDtypeStruct((b, s, MIX_WIDTH), BF),
    ]
    out_specs = [
        pl.BlockSpec((1, MIX_WIDTH, tm), lambda i, j: (i, 0, j)),
        pl.BlockSpec((1, g, tm, HEAD_DIM), lambda i, j: (i, 0, j, 0)),
        pl.BlockSpec((1, g, tm, HEAD_DIM), lambda i, j: (i, 0, j, 0)),
        pl.BlockSpec((1, g, tm, SLC_KEY_LANES), lambda i, j: (i, 0, j, 0)),
        pl.BlockSpec((1, g, tm, KEY_PAD), lambda i, j: (i, 0, j, 0)),
        pl.BlockSpec((1, tm // slc_tile, kvd, slc_tile), lambda i, j: (i, j, 0, 0)),
        pl.BlockSpec((1, tm // win_tile, kvd, win_tile), lambda i, j: (i, j, 0, 0)),
        pl.BlockSpec((1, 3 * N_HEADS, tm), lambda i, j: (i, 0, j)),
        pl.BlockSpec((1, tm, MIX_WIDTH), lambda i, j: (i, j, 0)),
    ]
    in_specs = [
        pl.BlockSpec((1, tm, d), lambda i, j: (i, j, 0)),
        pl.BlockSpec((1, HEAD_DIM // 2, tm), lambda i, j: (i, 0, j)),
        pl.BlockSpec((1, HEAD_DIM // 2, tm), lambda i, j: (i, 0, j)),
        _full(wq.shape), _full(wk.shape), _full(wv.shape), _full(wg.shape), _full(wz.shape),
        _full(qg.shape), _full(kg.shape),
    ]
    return pl.pallas_call(
        functools.partial(_nsa_proj_kernel, slc_tile=slc_tile, win_tile=win_tile),
        grid=(b, n_t), in_specs=in_specs, out_specs=out_specs, out_shape=out_shape,
        scratch_shapes=[pltpu.VMEM((MIX_WIDTH, tm), F32)],
        compiler_params=_params(("parallel", "parallel")),
        name="nsa_proj",
    )(h, cos, sin, wq, wk, wv, wg, wz, qg, kg)


def _gelu_tanh(x):
    c = np.float32(np.sqrt(2.0 / np.pi))
    return 0.5 * x * (1.0 + jnp.tanh(c * (x + 0.044715 * (x * x * x))))


def _compress_kernel(kc_ref, vc_ref, pos_ref, w1_ref, w2_ref, kg_ref, kcmp_out, vcmp_out):
    for which, (src, dst) in enumerate(((kc_ref, kcmp_out), (vc_ref, vcmp_out))):
        x = src[0, 0]
        n = x.shape[0]
        half = x.shape[1]
        xa = (x + pos_ref[which, 0:1, :]).astype(BF)
        xb = (x + pos_ref[which, 1:2, :]).astype(BF)
        ua = _dot(xa, w1_ref[which, :half, :])
        ub = _dot(xb, w1_ref[which, half:, :])
        row = lax.broadcasted_iota(jnp.int32, (n, 1), 0)
        ub_next = jnp.where(row == n - 1, 0.0, pltpu.roll(ub, n - 1, 0))
        hid = _gelu_tanh(ua + ub_next)
        y = _dot(hid.astype(BF), w2_ref[which])
        if which == 0:
            y = _rms_rows(y, kg_ref[...])
            dst[0, 0] = y.astype(BF)
        else:
            pad = jnp.zeros((n, KEY_PAD - HEAD_DIM), F32)
            dst[0, 0] = jnp.concatenate([y, pad], axis=1).T[:HEAD_DIM].astype(BF)


def _compress(kc_tok, vc_tok, cmp_pos, cmp_w1, cmp_w2, k_gain0):
    b, g, s, _ = kc_tok.shape
    n_chunk = s // NSA_CMP_STRIDE
    flat = NSA_CMP_STRIDE * HEAD_DIM
    kc = kc_tok.reshape(b, g, n_chunk, flat)
    vc = vc_tok.reshape(b, g, n_chunk, flat)
    pos = cmp_pos.reshape(2, 2, flat)
    w1 = cmp_w1.astype(BF)
    w2 = cmp_w2.astype(BF)
    blk = pl.BlockSpec((1, 1, n_chunk, flat), lambda i, j: (i, j, 0, 0))
    return pl.pallas_call(
        _compress_kernel,
        grid=(b, g),
        in_specs=[blk, blk, _full(pos.shape), _full(w1.shape), _full(w2.shape),
                  _full((1, HEAD_DIM))],
        out_specs=[pl.BlockSpec((1, 1, n_chunk, HEAD_DIM), lambda i, j: (i, j, 0, 0)),
                   pl.BlockSpec((1, 1, HEAD_DIM, n_chunk), lambda i, j: (i, j, 0, 0))],
        out_shape=[jax.ShapeDtypeStruct((b, g, n_chunk, HEAD_DIM), BF),
                   jax.ShapeDtypeStruct((b, g, HEAD_DIM, n_chunk), BF)],
        compiler_params=_params(("parallel", "parallel")),
        name="nsa_compress",
    )(kc, vc, pos, w1, w2, k_gain0.reshape(1, HEAD_DIM))


def _gate_row(gate_ref, n_heads, grp=0):
    return jnp.concatenate([gate_ref[0, 0, grp, r:r + 1, :] for r in range(n_heads)], axis=1)


def _cmp_branch(rows, q_ref, kc_ref, vc_ref, ov_ref, gate_ref, o_out, sel_out, *, tq, n_blk, n_grp):
    r_heads = N_HEADS // NSA_KV_HEADS
    nq = r_heads * tq
    width = r_heads * HEAD_DIM
    q0 = pl.program_id(2) * tq
    n_cmp = kc_ref.shape[2] - 1
    t_row = q0 + (lax.broadcasted_iota(jnp.int32, (1, nq), 1) & (tq - 1))
    lo = max(rows - CMP_MASK_ROWS, 0)
    c_col = lo + lax.broadcasted_iota(jnp.int32, (rows - lo, 1), 0)
    valid = (c_col * NSA_CMP_STRIDE + (NSA_CMP_LEN - 1) <= t_row) & (c_col < n_cmp)
    one_row = jnp.where(lax.broadcasted_iota(jnp.int32, (V_ROWS - HEAD_DIM, rows), 0) == 0, 1.0, 0.0).astype(BF)
    n_live = min(n_blk, rows * NSA_CMP_STRIDE // NSA_SLC_LEN + 8)
    t1 = q0 + lax.broadcasted_iota(jnp.int32, (1, tq), 1)
    cur = lax.shift_right_logical(t1, int(np.log2(NSA_SLC_LEN)))
    blk = lax.broadcasted_iota(jnp.int32, (n_live, tq), 0)
    forced = (blk == 0) | (blk == cur) | (blk == cur - 1)
    imps = []
    for gi in range(n_grp):
        q4 = jnp.concatenate([q_ref[0, (gi * r_heads + r) * HEAD_DIM:(gi * r_heads + r + 1) * HEAD_DIM, :]
                              for r in range(r_heads)], axis=1)
        s = _dot(kc_ref[0, gi, 0:rows, :], q4)
        s_new = jnp.where(valid, s[lo:], NEG_INF)
        m = jnp.max(s_new, axis=0, keepdims=True)
        if lo:
            m = jnp.maximum(m, jnp.max(s[:lo], axis=0, keepdims=True))
        e = jnp.where(valid, jnp.exp2(s_new - m), 0.0).astype(BF)
        if lo:
            e = jnp.concatenate([jnp.exp2(s[:lo] - m).astype(BF), e], axis=0)
        lhs = jnp.concatenate([vc_ref[0, gi, :, 0:rows], one_row, ov_ref[0:n_live, 0:rows]], axis=0)
        res = _dot(lhs, e)
        l = res[HEAD_DIM:HEAD_DIM + 1]
        inv = jnp.where(l > 0.0, 1.0 / jnp.where(l > 0.0, l, 1.0), 0.0)
        o = res[:HEAD_DIM] * (inv * _gate_row(gate_ref, r_heads, gi))
        o_rows = jnp.concatenate([o[:, r * tq:(r + 1) * tq] for r in range(r_heads)], axis=0)
        o_out[0, :, gi * width:(gi + 1) * width] = o_rows.T.astype(o_out.dtype)
        w = res[V_ROWS:V_ROWS + n_live] * inv
        imp = w[:, 0:tq]
        for r in range(1, r_heads):
            imp = imp + w[:, r * tq:(r + 1) * tq]
        imps.append(jnp.where(forced, BIG, jnp.where(blk > cur, NEG_INF, imp)))
    for _ in range(min(NSA_TOPK, n_blk)):
        for gi in range(n_grp):
            best = jnp.max(imps[gi], axis=0, keepdims=True)
            first = jnp.min(jnp.where(imps[gi] == best, blk, n_blk), axis=0, keepdims=True)
            imps[gi] = jnp.where(blk == first, -jnp.inf, imps[gi])
    for gi in range(n_grp):
        sel_out[0, gi, 0:n_live, :] = jnp.where(imps[gi] == -jnp.inf, 0.0, SEL_OFF).astype(BF)
        if n_live < n_blk:
            sel_out[0, gi, n_live:n_blk, :] = jnp.full((n_blk - n_live, tq), SEL_OFF, BF)


def _cmp_select_kernel(q_ref, kc_ref, vc_ref, ov_ref, gate_ref, o_out, sel_out, *, tq, n_blk, n_grp):
    q0 = pl.program_id(2) * tq
    n_chunk = kc_ref.shape[2]
    n_need = jnp.minimum((q0 + tq - NSA_CMP_LEN) // NSA_CMP_STRIDE + 1, n_chunk - 1)
    n_steps = n_chunk // CMP_BLOCK
    need_steps = (n_need + CMP_BLOCK - 1) // CMP_BLOCK
    for k in range(1, n_steps + 1):
        @pl.when(need_steps == k)
        def _(k=k):
            _cmp_branch(k * CMP_BLOCK, q_ref, kc_ref, vc_ref, ov_ref, gate_ref, o_out, sel_out,
                        tq=tq, n_blk=n_blk, n_grp=n_grp)


def _overlap_matrix(s):
    n_chunk = s // NSA_CMP_STRIDE
    n_blk = s // NSA_SLC_LEN
    c0 = np.arange(n_chunk) * NSA_CMP_STRIDE
    c1 = c0 + NSA_CMP_LEN - 1
    b0 = np.arange(n_blk) * NSA_SLC_LEN
    ov = np.minimum(c1[None, :], b0[:, None] + NSA_SLC_LEN - 1) - np.maximum(c0[None, :], b0[:, None]) + 1
    return jnp.asarray(np.clip(ov, 0, None) / NSA_CMP_LEN, BF)


def _cmp_select(qT, k_cmp, v_cmpT, gates5, *, tq):
    b, _, s = qT.shape
    g = NSA_KV_HEADS
    r_heads = N_HEADS // g
    n_chunk = k_cmp.shape[2]
    n_blk = s // NSA_SLC_LEN
    ov = _overlap_matrix(s)
    n_grp = CMP_GROUPS
    rows = n_grp * r_heads * HEAD_DIM
    return pl.pallas_call(
        functools.partial(_cmp_select_kernel, tq=tq, n_blk=n_blk, n_grp=n_grp),
        grid=(b, g // n_grp, s // tq),
        in_specs=[
            pl.BlockSpec((1, rows, tq), lambda i, j, k: (i, j, k)),
            pl.BlockSpec((1, n_grp, n_chunk, HEAD_DIM), lambda i, j, k: (i, j, 0, 0)),
            pl.BlockSpec((1, n_grp, HEAD_DIM, n_chunk), lambda i, j, k: (i, j, 0, 0)),
            pl.BlockSpec((n_blk, n_chunk), lambda i, j, k: (0, 0)),
            pl.BlockSpec((1, 1, n_grp, r_heads, tq), lambda i, j, k: (i, 0, j, 0, k)),
        ],
        out_specs=[pl.BlockSpec((1, tq, rows), lambda i, j, k: (i, k, j)),
                   pl.BlockSpec((1, n_grp, n_blk, tq), lambda i, j, k: (i, j, 0, k))],
        out_shape=[jax.ShapeDtypeStruct((b, s, MIX_WIDTH), BF),
                   jax.ShapeDtypeStruct((b, g, n_blk, s), BF)],
        compiler_params=_params(("parallel", "parallel", "parallel")),
        name="nsa_cmp_select",
    )(qT, k_cmp, v_cmpT, ov, gates5)


def _flash_kernel(*refs, mode, n_par, n_rep, dq, kc, tq, tk, cw, window, n_blk, has_sink, has_gate,
                  fuse_out=None):
    it = iter(refs)
    q_ref, k_ref, v_ref = next(it), next(it), next(it)
    sel_ref = next(it) if n_blk else None
    sink_ref = next(it) if has_sink else None
    gate_ref = next(it) if has_gate else None
    qnext_ref = next(it) if mode == "causal" else None
    selnext_ref = next(it) if (mode == "causal" and n_blk) else None
    if fuse_out is None:
        out_ref = next(it)
    else:
        n_extra, has_next = fuse_out
        extra_refs = [next(it) for _ in range(n_extra)]
        z_ref, x_ref, w_ref = next(it), next(it), next(it)
        ng_ref = next(it) if has_next else None
        x_out = next(it)
        h_out = next(it) if has_next else None
    qs_ref, m_ref, acc_ref, mt_ref = (next(it) for _ in range(4))
    qn_ref = next(it) if mode == "causal" else None
    s_bufs = tuple(it)

    nq = n_rep * tq
    items = [(p, c) for p in range(n_par) for c in range(nq // cw)]
    n_items = len(items)
    grp = pl.program_id(1)
    q0 = pl.program_id(2) * tq
    acc_row = lax.broadcasted_iota(jnp.int32, (V_ROWS, nq), 0)
    for p in range(n_par):
        for r in range(n_rep):
            hd = p * n_rep + r
            qs_ref[p, 0:dq, r * tq:(r + 1) * tq] = q_ref[0, hd * dq:(hd + 1) * dq, :]
        if n_blk:
            qs_ref[p, dq:dq + n_blk, :] = jnp.concatenate([sel_ref[0, 0]] * n_rep, axis=1)
            if dq + n_blk < kc:
                qs_ref[p, dq + n_blk:kc, :] = jnp.zeros((kc - dq - n_blk, nq), BF)
        if has_sink:
            m_ref[p] = jnp.concatenate(
                [jnp.full((1, tq), sink_ref[(grp * n_par + p) * n_rep + r] * LOG2E, F32) for r in range(n_rep)],
                axis=1)
            acc_ref[p] = jnp.where(acc_row == HEAD_DIM, 1.0, 0.0)
        else:
            m_ref[p] = jnp.full((1, nq), M_INIT, F32)
            acc_ref[p] = jnp.zeros((V_ROWS, nq), F32)

    t_row = q0 + (lax.broadcasted_iota(jnp.int32, (1, nq), 1) & (tq - 1))
    one_row = jnp.where(lax.broadcasted_iota(jnp.int32, (V_ROWS - HEAD_DIM, tk), 0) == 0, 1.0, 0.0).astype(BF)

    def stage_a(item, j, key0, slot, kind):
        p, c = item
        cols = slice(c * cw, (c + 1) * cw)
        s = _dot(k_ref[0, p, j][:, :kc], qs_ref[p, :, cols])
        if kind is not None:
            key = key0 + lax.broadcasted_iota(jnp.int32, (tk, 1), 0)
            ok = key <= t_row[:, cols] if kind == "causal" else key > t_row[:, cols] - window
            s = jnp.where(ok, s, NEG_INF)
        s_bufs[slot][...] = s
        mt_ref[slot] = jnp.max(s, axis=0, keepdims=True)

    def stage_b(item, j, slot):
        p, c = item
        cols = slice(c * cw, (c + 1) * cw)
        m_old = m_ref[p, :, cols]
        m_new = jnp.maximum(m_old, mt_ref[slot])
        pr = jnp.exp2(s_bufs[slot][...] - m_new).astype(BF)
        alpha = jnp.exp2(m_old - m_new)
        v = jnp.concatenate([v_ref[0, j, p * HEAD_DIM:(p + 1) * HEAD_DIM, :], one_row], axis=0)
        acc_ref[p, :, cols] = alpha * acc_ref[p, :, cols] + _dot(v, pr)
        m_ref[p, :, cols] = m_new

    def finalize():
        outs = []
        for p in range(n_par):
            acc = acc_ref[p]
            o = acc[:HEAD_DIM] * (1.0 / acc[HEAD_DIM:HEAD_DIM + 1])
            if has_gate:
                o = o * _gate_row(gate_ref, n_rep, p)
            outs += [o[:, r * tq:(r + 1) * tq] for r in range(n_rep)]
        o_tok = jnp.concatenate(outs, axis=0).T
        if fuse_out is None:
            out_ref[0] = o_tok.astype(out_ref.dtype)
            return
        for r in extra_refs:
            o_tok = o_tok + r[0].astype(F32)
        y = _dot((o_tok * z_ref[0].astype(F32)).astype(BF), w_ref[...])
        x_new = x_ref[0] + y
        x_out[0] = x_new
        if has_next:
            h_out[0] = _rms_rows(x_new, ng_ref[...]).astype(BF)

    if mode == "causal":
        assert n_items % 2 == 0
        n_full = q0 // tk

        def prefetch_next_tile():
            assert cw == tq
            qn_ref[0:dq, :] = qnext_ref[0, 0:dq, :]
            if n_blk:
                qn_ref[dq:dq + n_blk, :] = selnext_ref[0, 0]
                if dq + n_blk < kc:
                    qn_ref[dq + n_blk:kc, :] = jnp.zeros((kc - dq - n_blk, tq), BF)
            s = _dot(k_ref[0, 0, 0][:, :kc], qn_ref[...])
            key = lax.broadcasted_iota(jnp.int32, (tk, 1), 0)
            s = jnp.where(key <= t_row[:, 0:cw] + tq, s, NEG_INF)
            s_bufs[0][...] = s
            mt_ref[0] = jnp.max(s, axis=0, keepdims=True)

        def step(j, kind, next_kind, last):
            for idx, item in enumerate(items):
                slot = idx % 2
                if idx + 1 < n_items:
                    stage_a(items[idx + 1], j, j * tk, 1 - slot, kind)
                elif not last:
                    stage_a(items[0], j + 1, (j + 1) * tk, 1 - slot, next_kind)
                else:
                    prefetch_next_tile()
                stage_b(item, j, slot)

        @pl.when(pl.program_id(2) == 0)
        def _():
            stage_a(items[0], 0, 0, 0, "causal")

        def body(j, carry):
            step(j, None, None, False)
            return carry

        def body_group(i, carry):
            for u in range(FLASH_UNROLL):
                step(FLASH_UNROLL * i + u, None, None, False)
            return carry

        n_main = jnp.maximum(n_full - 1, 0)
        n_groups = lax.shift_right_logical(n_main, int(np.log2(FLASH_UNROLL)))
        lax.fori_loop(0, n_groups, body_group, 0)
        lax.fori_loop(FLASH_UNROLL * n_groups, n_main, body, 0)

        @pl.when(n_full >= 1)
        def _():
            step(n_full - 1, None, "causal", False)
            step(n_full, "causal", None, True)
            finalize()

        @pl.when(n_full == 0)
        def _():
            step(0, "causal", None, True)
            finalize()
    else:
        w_tiles, q_tiles = window // tk, tq // tk
        work = []
        for i in range(w_tiles + q_tiles):
            jv = q0 // tk - w_tiles + i
            key0 = jnp.where(jv < 0, -(1 << 30), jv * tk)
            kind = "causal" if i >= w_tiles else "lower"
            work += [(item, jnp.maximum(jv, 0), key0, kind) for item in items]
        n_buf = len(s_bufs)
        ahead = WINDOW_LOOKAHEAD
        for n in range(ahead):
            stage_a(work[n][0], work[n][1], work[n][2], n % n_buf, work[n][3])
        for n, (item, j, _, _) in enumerate(work):
            if n + ahead < len(work):
                nxt = work[n + ahead]
                stage_a(nxt[0], nxt[1], nxt[2], (n + ahead) % n_buf, nxt[3])
            stage_b(item, j, n % n_buf)

        finalize()


def _flash(qT, k_tok, vT_tiles, *, mode, n_par, n_rep, dq, kc, tq, tk, cw, window=None,
           sel_bias=None, sinks=None, gates=None, gate_branch=0, out_proj=None, name="flash"):
    b, _, s = qT.shape
    kh, k_lanes = k_tok.shape[1], k_tok.shape[3]
    n_t = s // tk
    k5 = k_tok.reshape(b, kh, n_t, tk, k_lanes)
    n_grp = kh // n_par
    heads = n_par * n_rep
    nq = n_rep * tq
    n_blk = sel_bias.shape[2] if sel_bias is not None else 0
    has_sink, has_gate = sinks is not None, gates is not None
    args = [qT, k5, vT_tiles]
    in_specs = [
        pl.BlockSpec((1, heads * dq, tq), lambda i, j, k: (i, j, k)),
        pl.BlockSpec((1, n_par, n_t, tk, k_lanes), lambda i, j, k: (i, j, 0, 0, 0)),
        pl.BlockSpec((1, n_t, n_par * HEAD_DIM, tk), lambda i, j, k: (i, 0, j, 0)),
    ]
    if n_blk:
        assert dq + n_blk <= kc
        args.append(sel_bias)
        in_specs.append(pl.BlockSpec((1, 1, n_blk, tq), lambda i, j, k: (i, j, 0, k)))
    if has_sink:
        args.append(sinks.astype(F32))
        in_specs.append(pl.BlockSpec(memory_space=pltpu.SMEM))
    if has_gate:
        args.append(gates)
        in_specs.append(pl.BlockSpec((1, 1, n_par, n_rep, tq),
                                     lambda i, j, k, _br=gate_branch: (i, _br, j, 0, k)))
    n_sbuf = 2 if mode == "causal" else WINDOW_SCORE_BUFS
    scratch = [pltpu.VMEM((n_par, kc, nq), BF), pltpu.VMEM((n_par, 1, nq), F32),
               pltpu.VMEM((n_par, V_ROWS, nq), F32), pltpu.VMEM((n_sbuf, 1, cw), F32)]
    if mode == "causal":
        last_q = s // tq - 1
        args.append(qT)
        in_specs.append(pl.BlockSpec((1, dq, tq), lambda i, j, k: (i, j * heads, jnp.minimum(k + 1, last_q))))
        if n_blk:
            args.append(sel_bias)
            in_specs.append(pl.BlockSpec((1, 1, n_blk, tq), lambda i, j, k: (i, j, 0, jnp.minimum(k + 1, last_q))))
        scratch.append(pltpu.VMEM((kc, tq), BF))
    scratch += [pltpu.VMEM((tk, cw), F32)] * n_sbuf
    out_specs = pl.BlockSpec((1, tq, heads * HEAD_DIM), lambda i, j, k: (i, k, j))
    out_shape = jax.ShapeDtypeStruct((b, s, MIX_WIDTH), BF)
    fuse_out = None
    if out_proj is not None:
        assert mode == "window" and heads * HEAD_DIM == MIX_WIDTH and n_grp == 1
        extra, zs, x, w_out, next_gain = out_proj
        tok = pl.BlockSpec((1, tq, D_MODEL), lambda i, j, k: (i, k, 0))
        args += list(extra) + [zs, x, w_out.astype(BF)]
        in_specs += [tok] * (len(extra) + 2) + [pl.BlockSpec(w_out.shape, lambda i, j, k: (0, 0))]
        out_specs, out_shape = [tok], [jax.ShapeDtypeStruct(x.shape, F32)]
        if next_gain is not None:
            args.append(next_gain.reshape(1, D_MODEL))
            in_specs.append(pl.BlockSpec((1, D_MODEL), lambda i, j, k: (0, 0)))
            out_specs.append(tok)
            out_shape.append(jax.ShapeDtypeStruct(x.shape, BF))
        fuse_out = (len(extra), next_gain is not None)
    kern = functools.partial(_flash_kernel, mode=mode, n_par=n_par, n_rep=n_rep, dq=dq, kc=kc, tq=tq, tk=tk,
                             cw=cw, window=window, n_blk=n_blk, has_sink=has_sink, has_gate=has_gate,
                             fuse_out=fuse_out)
    res = pl.pallas_call(
        kern,
        grid=(b, n_grp, s // tq),
        in_specs=in_specs,
        out_specs=out_specs,
        out_shape=out_shape,
        scratch_shapes=scratch,
        compiler_params=_params(("parallel", "parallel", "arbitrary")),
        name=name,
    )(*args)
    if out_proj is None:
        return res
    return (res[0], res[1]) if next_gain is not None else (res[0], None)


def _swa_proj_kernel(h_ref, cos_ref, sin_ref, wq_ref, wk_ref, wv_ref, wz_ref, qg_ref, kg_ref,
                     q_out, k_out, v_out, z_out, y_ref, *, win_tile):
    h = h_ref[0]
    cos, sin = cos_ref[0], sin_ref[0]
    y_ref[...] = _dot_nt(wq_ref[...], h)
    for hd in range(N_HEADS):
        rows = slice(hd * HEAD_DIM, (hd + 1) * HEAD_DIM)
        q_out[0, rows, :] = (_rope_t(_head_rms_t(y_ref[rows, :], qg_ref[...]), cos, sin) * Q_SCALE).astype(BF)
    kvd = SWA_KV_HEADS * HEAD_DIM
    y_ref[0:kvd, :] = _dot_nt(wk_ref[...], h)
    for gi in range(SWA_KV_HEADS):
        rows = slice(gi * HEAD_DIM, (gi + 1) * HEAD_DIM)
        k_out[0, gi] = _to_token_major(_rope_t(_head_rms_t(y_ref[rows, :], kg_ref[...]), cos, sin)).astype(BF)
    _store_lane_tiles(v_out, _dot_nt(wv_ref[...], h).astype(BF), win_tile)
    z_out[0] = _silu(_dot(h, wz_ref[...])).astype(BF)


def _swa_proj(h, cos, sin, w_in, q_gain, k_gain, *, win_tile):
    b, s, d = h.shape
    g = SWA_KV_HEADS
    kvd = g * HEAD_DIM
    tm = min(TOK_TILE, s)
    wt = w_in.T.astype(BF)
    wq, wk, wv = wt[:MIX_WIDTH], wt[MIX_WIDTH:MIX_WIDTH + kvd], wt[MIX_WIDTH + kvd:MIX_WIDTH + 2 * kvd]
    wz = w_in[:, MIX_WIDTH + 2 * kvd:].astype(BF)
    qg, kg = q_gain.reshape(HEAD_DIM, 1), k_gain.reshape(HEAD_DIM, 1)
    wpt = tm // win_tile
    half = HEAD_DIM // 2
    return pl.pallas_call(
        functools.partial(_swa_proj_kernel, win_tile=win_tile),
        grid=(b, s // tm),
        in_specs=[pl.BlockSpec((1, tm, d), lambda i, j: (i, j, 0)),
                  pl.BlockSpec((1, half, tm), lambda i, j: (i, 0, j)),
                  pl.BlockSpec((1, half, tm), lambda i, j: (i, 0, j)),
                  _full(wq.shape), _full(wk.shape), _full(wv.shape), _full(wz.shape),
                  _full(qg.shape), _full(kg.shape)],
        out_specs=[pl.BlockSpec((1, MIX_WIDTH, tm), lambda i, j: (i, 0, j)),
                   pl.BlockSpec((1, g, tm, KEY_PAD), lambda i, j: (i, 0, j, 0)),
                   pl.BlockSpec((1, wpt, kvd, win_tile), lambda i, j: (i, j, 0, 0)),
                   pl.BlockSpec((1, tm, MIX_WIDTH), lambda i, j: (i, j, 0))],
        out_shape=[jax.ShapeDtypeStruct((b, MIX_WIDTH, s), BF),
                   jax.ShapeDtypeStruct((b, g, s, KEY_PAD), BF),
                   jax.ShapeDtypeStruct((b, s // win_tile, kvd, win_tile), BF),
                   jax.ShapeDtypeStruct((b, s, MIX_WIDTH), BF)],
        scratch_shapes=[pltpu.VMEM((MIX_WIDTH, tm), F32)],
        compiler_params=_params(("parallel", "parallel")),
        name="swa_proj",
    )(h, cos, sin, wq, wk, wv, wz, qg, kg)


def _split3(x):
    hi = x.astype(BF)
    r1 = x - hi.astype(F32)
    mid = r1.astype(BF)
    lo = (r1 - mid.astype(F32)).astype(BF)
    return hi, mid, lo


def _forget_cum_kernel(h_ref, wf_ref, bias_ref, tri_ref, cum_out, carry_ref):
    @pl.when(pl.program_id(1) == 0)
    def _():
        carry_ref[...] = jnp.zeros_like(carry_ref)

    x = _dot_nt(wf_ref[...], h_ref[0]) + bias_ref[...]
    logf = jnp.minimum(x, 0.0) - jnp.log(1.0 + jnp.exp(-jnp.abs(x)))
    tri = tri_ref[...]
    hi, mid, lo = _split3(logf)
    cum = (_dot(hi, tri) + _dot(mid, tri)) + _dot(lo, tri) + carry_ref[:, 0:1]
    cum_out[0] = cum
    carry_ref[...] = jnp.broadcast_to(cum[:, -1:], carry_ref.shape)


def _forget_cum(h, wf, bias):
    b, s, d = h.shape
    tm = min(TOK_TILE, s)
    tri = jnp.asarray(np.arange(tm)[:, None] <= np.arange(tm)[None, :], BF)
    return pl.pallas_call(
        _forget_cum_kernel,
        grid=(b, s // tm),
        in_specs=[pl.BlockSpec((1, tm, d), lambda i, j: (i, j, 0)),
                  _full(wf.shape), _full((N_HEADS, 1)), _full(tri.shape)],
        out_specs=pl.BlockSpec((1, N_HEADS, tm), lambda i, j: (i, 0, j)),
        out_shape=jax.ShapeDtypeStruct((b, N_HEADS, s), F32),
        scratch_shapes=[pltpu.VMEM((N_HEADS, LANES), F32)],
        compiler_params=_params(("parallel", "arbitrary")),
        name="fox_forget_cum",
    )(h, wf, bias.reshape(N_HEADS, 1).astype(F32), tri)


def _fox_proj_kernel(h_ref, cum_ref, wq_ref, wk_ref, wv_ref, wz_ref, qg_ref, kg_ref,
                     q_out, k_out, v_out, z_out, yq_ref, yk_ref, *, tile):
    h = h_ref[0]
    tm = h.shape[0]
    yq_ref[...] = _dot_nt(wq_ref[...], h)
    yk_ref[...] = _dot_nt(wk_ref[...], h)
    row = lax.broadcasted_iota(jnp.int32, (8, tm), 0)
    zeros = jnp.zeros((KEY_PAD - HEAD_DIM - 16, tm), F32)
    for hd in range(N_HEADS):
        rows = slice(hd * HEAD_DIM, (hd + 1) * HEAD_DIM)
        c_hi, c_mid, c_lo = (c.astype(F32) for c in _split3(cum_ref[0, hd:hd + 1, :] * LOG2E))
        c3 = jnp.where(row == 0, c_hi, jnp.where(row == 1, c_mid, jnp.where(row == 2, c_lo, 0.0)))
        one3 = jnp.where(row < 3, 1.0, 0.0)
        q = _head_rms_t(yq_ref[rows, :], qg_ref[...]) * Q_SCALE
        q_out[0, hd] = jnp.concatenate([q, one3, c3, zeros], axis=0).astype(BF)
        k = _head_rms_t(yk_ref[rows, :], kg_ref[...])
        k_out[0, hd] = jnp.concatenate([k, -c3, one3, zeros], axis=0).T.astype(BF)
    _store_lane_tiles(v_out, _dot_nt(wv_ref[...], h).astype(BF), tile)
    z_out[0] = _silu(_dot(h, wz_ref[...])).astype(BF)


def _fox_proj(h, cum, w_in, q_gain, k_gain, *, tile):
    b, s, d = h.shape
    tm = min(TOK_TILE, s)
    wt = w_in.T.astype(BF)
    wq, wk, wv = wt[:MIX_WIDTH], wt[MIX_WIDTH:2 * MIX_WIDTH], wt[2 * MIX_WIDTH:3 * MIX_WIDTH]
    wz = w_in[:, 3 * MIX_WIDTH + N_HEADS:].astype(BF)
    qg, kg = q_gain.reshape(HEAD_DIM, 1), k_gain.reshape(HEAD_DIM, 1)
    return pl.pallas_call(
        functools.partial(_fox_proj_kernel, tile=tile),
        grid=(b, s // tm),
        in_specs=[pl.BlockSpec((1, tm, d), lambda i, j: (i, j, 0)),
                  pl.BlockSpec((1, N_HEADS, tm), lambda i, j: (i, 0, j)),
                  _full(wq.shape), _full(wk.shape), _full(wv.shape), _full(wz.shape),
                  _full(qg.shape), _full(kg.shape)],
        out_specs=[pl.BlockSpec((1, N_HEADS, KEY_PAD, tm), lambda i, j: (i, 0, 0, j)),
                   pl.BlockSpec((1, N_HEADS, tm, KEY_PAD), lambda i, j: (i, 0, j, 0)),
                   pl.BlockSpec((1, tm // tile, MIX_WIDTH, tile), lambda i, j: (i, j, 0, 0)),
                   pl.BlockSpec((1, tm, MIX_WIDTH), lambda i, j: (i, j, 0))],
        out_shape=[jax.ShapeDtypeStruct((b, N_HEADS, KEY_PAD, s), BF),
                   jax.ShapeDtypeStruct((b, N_HEADS, s, KEY_PAD), BF),
                   jax.ShapeDtypeStruct((b, s // tile, MIX_WIDTH, tile), BF),
                   jax.ShapeDtypeStruct((b, s, MIX_WIDTH), BF)],
        scratch_shapes=[pltpu.VMEM((MIX_WIDTH, tm), F32), pltpu.VMEM((MIX_WIDTH, tm), F32)],
        compiler_params=_params(("parallel", "parallel")),
        name="fox_proj",
    )(h, cum, wq, wk, wv, wz, qg, kg)


def _out_proj_kernel(*refs, n_o, has_next):
    o_refs = refs[:n_o]
    z_ref, x_ref, w_ref = refs[n_o:n_o + 3]
    rest = refs[n_o + 3:]
    if has_next:
        g_ref, x_out, h_out = rest
    else:
        (x_out,) = rest
    o = o_refs[0][0].astype(F32)
    for r in o_refs[1:]:
        o = o + r[0].astype(F32)
    y = _dot((o * z_ref[0].astype(F32)).astype(BF), w_ref[...])
    x_new = x_ref[0] + y
    x_out[0] = x_new
    if has_next:
        h_out[0] = _rms_rows(x_new, g_ref[...]).astype(BF)


def _out_proj(o_list, zs, x, w_out, next_gain):
    b, s, d = x.shape
    tm = min(TOK_TILE, s)
    has_next = next_gain is not None
    blk = pl.BlockSpec((1, tm, d), lambda i, j: (i, j, 0))
    args = list(o_list) + [zs, x, w_out.astype(BF)]
    in_specs = [blk] * (len(o_list) + 2) + [_full(w_out.shape)]
    out_shape = [jax.ShapeDtypeStruct((b, s, d), F32)]
    out_specs = [blk]
    if has_next:
        args.append(next_gain.reshape(1, d))
        in_specs.append(_full((1, d)))
        out_shape.append(jax.ShapeDtypeStruct((b, s, d), BF))
        out_specs.append(blk)
    res = pl.pallas_call(
        functools.partial(_out_proj_kernel, n_o=len(o_list), has_next=has_next),
        grid=(b, s // tm), in_specs=in_specs, out_specs=out_specs, out_shape=out_shape,
        compiler_params=_params(("parallel", "parallel")),
        name="out_proj",
    )(*args)
    return (res[0], res[1]) if has_next else (res[0], None)


ITEM_LANES = 512
CMP_Q_TILE = 512
CMP_GROUPS = 2
NSA_SLC_Q_TILE = 512
NSA_SLC_TILE = 512
NSA_WIN_Q_TILE = 256
NSA_WIN_TILE = 256
SWA_TILE = 128
FOX_Q_TILE = 512
FOX_K_TILE = 512
FOX_HEADS_PER_STEP = 4


def _nsa_mixer(x, h, cos, sin, w_in, q_gain, k_gain, cmp_pos, cmp_w1, cmp_w2, w_out, next_gain):
    b, s, _ = h.shape
    g = NSA_KV_HEADS
    r = N_HEADS // g
    qT, kc_tok, vc_tok, ks, kw, vsT, vwT, gates, zs = _nsa_proj(
        h, cos, sin, w_in, q_gain, k_gain, slc_tile=NSA_SLC_TILE, win_tile=NSA_WIN_TILE)
    k_cmp, v_cmpT = _compress(kc_tok, vc_tok, cmp_pos, cmp_w1, cmp_w2, k_gain[0])
    gates5 = gates.reshape(b, 3, g, r, s)
    o_cmp, sel_bias = _cmp_select(qT, k_cmp, v_cmpT, gates5, tq=CMP_Q_TILE)
    o_slc = _flash(qT, ks, vsT, mode="causal", n_par=1, n_rep=r, dq=HEAD_DIM, kc=SLC_KEY_LANES,
                   tq=NSA_SLC_Q_TILE, tk=NSA_SLC_TILE, cw=ITEM_LANES, sel_bias=sel_bias,
                   gates=gates5, gate_branch=1, name="nsa_selected")
    return _flash(qT, kw, vwT, mode="window", n_par=g, n_rep=r, dq=HEAD_DIM, kc=HEAD_DIM,
                  tq=NSA_WIN_Q_TILE, tk=NSA_WIN_TILE, cw=ITEM_LANES, window=NSA_WINDOW,
                  gates=gates5, gate_branch=2, out_proj=([o_cmp, o_slc], zs, x, w_out, next_gain),
                  name="nsa_window_out")


def _swa_mixer(x, h, cos, sin, w_in, q_gain, k_gain, sinks, w_out, next_gain):
    r = N_HEADS // SWA_KV_HEADS
    qT, k, vT, zs = _swa_proj(h, cos, sin, w_in, q_gain, k_gain, win_tile=SWA_TILE)
    return _flash(qT, k, vT, mode="window", n_par=SWA_KV_HEADS, n_rep=r, dq=HEAD_DIM, kc=HEAD_DIM, tq=SWA_TILE,
                  tk=SWA_TILE, cw=r * SWA_TILE, window=SWA_WINDOW, sinks=sinks,
                  out_proj=([], zs, x, w_out, next_gain), name="swa_window_out")


def _fox_mixer(x, h, w_in, forget_bias, q_gain, k_gain, w_out, next_gain):
    b, s, _ = h.shape
    wf = w_in[:, 3 * MIX_WIDTH:3 * MIX_WIDTH + N_HEADS].T.astype(BF)
    cum = _forget_cum(h, wf, forget_bias)
    qT, k, vT, zs = _fox_proj(h, cum, w_in, q_gain, k_gain, tile=FOX_K_TILE)
    o = _flash(qT.reshape(b, N_HEADS * KEY_PAD, s), k, vT, mode="causal", n_par=FOX_HEADS_PER_STEP, n_rep=1, dq=KEY_PAD,
               kc=KEY_PAD, tq=FOX_Q_TILE, tk=FOX_K_TILE, cw=ITEM_LANES, name="fox_attention")
    return _out_proj([o], zs, x, w_out, next_gain)


def kernel(x, positions, norm_gains, a_w_in, a_q_gain, a_k_gain, a_cmp_pos, a_cmp_w1, a_cmp_w2, a_w_out,
           b_w_in, b_q_gain, b_k_gain, b_sinks, b_w_out,
           c_w_in, c_forget_bias, c_q_gain, c_k_gain, c_w_out):
    depth = norm_gains.shape[0]
    cos, sin = _rope_tables(positions)
    h = _prenorm(x, norm_gains[0])
    for i in range(depth):
        j, mixer = divmod(i, 3)
        next_gain = norm_gains[i + 1] if i + 1 < depth else None
        if mixer == 0:
            x, h = _nsa_mixer(x, h, cos, sin, a_w_in[j], a_q_gain[j], a_k_gain[j],
                              a_cmp_pos[j], a_cmp_w1[j], a_cmp_w2[j], a_w_out[j], next_gain)
        elif mixer == 1:
            x, h = _swa_mixer(x, h, cos, sin, b_w_in[j], b_q_gain[j], b_k_gain[j], b_sinks[j],
                              b_w_out[j], next_gain)
        else:
            x, h = _fox_mixer(x, h, c_w_in[j], c_forget_bias[j], c_q_gain[j], c_k_gain[j],
                              c_w_out[j], next_gain)
    return x
```

```python
import functools

import jax
import jax.numpy as jnp
import numpy as np
from jax import lax
from jax.experimental import pallas as pl
from jax.experimental.pallas import tpu as pltpu

D_MODEL = 1024
HEAD_DIM = 64
N_HEADS = 16
MIX_WIDTH = N_HEADS * HEAD_DIM
ROPE_THETA = 10000.0
EPS = 1e-6
SCALE = HEAD_DIM ** -0.5
NEG_INF = -1e30
BIG = 1e30
M_INIT = -1e29
SEL_OFF = -(2.0 ** 100)

NSA_KV_HEADS = 4
NSA_CMP_LEN = 32
NSA_CMP_STRIDE = 16
NSA_SLC_LEN = 64
NSA_TOPK = 16
NSA_WINDOW = 512
SWA_KV_HEADS = 2
SWA_WINDOW = 128

LOG2E = float(np.log2(np.e))
Q_SCALE = SCALE * LOG2E

LANES = 128
KEY_PAD = 128
SLC_KEY_LANES = 256
V_ROWS = 80
FLASH_UNROLL = 4
WINDOW_LOOKAHEAD = 2
WINDOW_SCORE_BUFS = 4
CMP_BLOCK = 128
CMP_MASK_ROWS = 2 * CMP_BLOCK
VMEM_LIMIT = 56 * 1024 * 1024

TOK_TILE = 512
NT_DIMS = (((1,), (1,)), ((), ()))

BF = jnp.bfloat16
F32 = jnp.float32


def _params(sem):
    return pltpu.CompilerParams(dimension_semantics=sem, vmem_limit_bytes=VMEM_LIMIT)


def _dot(a, b):
    return jnp.dot(a, b, preferred_element_type=F32)


def _dot_nt(a, b):
    return lax.dot_general(a, b, NT_DIMS, preferred_element_type=F32)


def _rope_tab_kernel(pos_ref, invf_ref, cos_ref, sin_ref):
    ang = invf_ref[...] * pos_ref[0].astype(F32)
    cos_ref[0] = jnp.cos(ang)
    sin_ref[0] = jnp.sin(ang)


def _rope_tables(positions):
    b, s = positions.shape
    half = HEAD_DIM // 2
    inv_freq = ROPE_THETA ** (-jnp.arange(half, dtype=F32) * 2.0 / HEAD_DIM)
    tm = min(TOK_TILE, s)
    out = jax.ShapeDtypeStruct((b, half, s), F32)
    return pl.pallas_call(
        _rope_tab_kernel,
        grid=(b, s // tm),
        in_specs=[pl.BlockSpec((1, 1, tm), lambda i, j: (i, 0, j)),
                  pl.BlockSpec((half, 1), lambda i, j: (0, 0))],
        out_specs=[pl.BlockSpec((1, half, tm), lambda i, j: (i, 0, j))] * 2,
        out_shape=[out, out],
        compiler_params=_params(("parallel", "parallel")),
        name="rope_tables",
    )(positions.reshape(b, 1, s), inv_freq.reshape(half, 1))


def _rms_rows(x, gain_row):
    y = x * lax.rsqrt(jnp.mean(x * x, axis=-1, keepdims=True) + EPS)
    return y * gain_row


def _head_rms_t(y, gain_col):
    ms = jnp.mean(y * y, axis=0, keepdims=True)
    return (y * lax.rsqrt(ms + EPS)) * gain_col


def _rope_t(y, cos, sin):
    half = HEAD_DIM // 2
    x1, x2 = y[:half], y[half:]
    return jnp.concatenate([x1 * cos - x2 * sin, x2 * cos + x1 * sin], axis=0)


def _to_token_major(y):
    pad = jnp.zeros((KEY_PAD - y.shape[0], y.shape[1]), y.dtype)
    return jnp.concatenate([y, pad], axis=0).T


def _silu(z):
    return z * (1.0 / (1.0 + jnp.exp(-z)))


def _sigmoid(z):
    return 1.0 / (1.0 + jnp.exp(-z))


def _store_lane_tiles(out_ref, y, tile):
    for c in range(y.shape[1] // tile):
        out_ref[0, c] = y[:, c * tile:(c + 1) * tile]


def _nsa_proj_kernel(*refs, slc_tile, win_tile, norm_in):
    it = iter(refs)
    h_ref = next(it)
    ng_ref = next(it) if norm_in else None
    (cos_ref, sin_ref, wq_ref, wk_ref, wv_ref, wg_ref, wz_ref, qg_ref, kg_ref,
     q_out, kc_out, vc_out, ks_out, kw_out, vs_out, vw_out, g_out, z_out, y_ref) = it
    h = _rms_rows(h_ref[0], ng_ref[...]).astype(BF) if norm_in else h_ref[0]
    cos, sin = cos_ref[0], sin_ref[0]
    g = NSA_KV_HEADS
    kvd = g * HEAD_DIM
    y_ref[...] = _dot_nt(wq_ref[...], h)
    for hd in range(N_HEADS):
        rows = slice(hd * HEAD_DIM, (hd + 1) * HEAD_DIM)
        y = _rope_t(_head_rms_t(y_ref[rows, :], qg_ref[...]), cos, sin) * Q_SCALE
        q_out[0, rows, :] = y.astype(BF)
    y_ref[0:3 * kvd, :] = _dot_nt(wk_ref[...], h)
    tm = h.shape[0]
    tok = pl.program_id(1) * tm + lax.broadcasted_iota(jnp.int32, (tm, 1), 0)
    blk_lane = HEAD_DIM + lax.shift_right_logical(tok, int(np.log2(NSA_SLC_LEN)))
    blk_hot = lax.broadcasted_iota(jnp.int32, (1, SLC_KEY_LANES), 1) == blk_lane
    for kind in range(3):
        for gi in range(g):
            r0 = (kind * g + gi) * HEAD_DIM
            y = y_ref[r0:r0 + HEAD_DIM, :]
            if kind > 0:
                y = _head_rms_t(y, kg_ref[:, kind:kind + 1])
            yt = _to_token_major(_rope_t(y, cos, sin))
            if kind == 0:
                kc_out[0, gi] = yt[:, :HEAD_DIM]
            elif kind == 1:
                wide = jnp.concatenate([yt, jnp.zeros((tm, SLC_KEY_LANES - KEY_PAD), F32)], axis=1)
                ks_out[0, gi] = jnp.where(blk_hot, 1.0, wide).astype(BF)
            else:
                kw_out[0, gi] = yt.astype(BF)
    y_ref[0:3 * kvd, :] = _dot_nt(wv_ref[...], h)
    for gi in range(g):
        r0 = gi * HEAD_DIM
        vc_out[0, gi] = _to_token_major(y_ref[r0:r0 + HEAD_DIM, :])[:, :HEAD_DIM]
    _store_lane_tiles(vs_out, y_ref[kvd:2 * kvd, :].astype(BF), slc_tile)
    _store_lane_tiles(vw_out, y_ref[2 * kvd:3 * kvd, :].astype(BF), win_tile)
    g_out[0] = _sigmoid(_dot_nt(wg_ref[...], h))
    z_out[0] = _silu(_dot(h, wz_ref[...])).astype(BF)


def _full(shape):
    nd = len(shape)
    return pl.BlockSpec(shape, lambda i, j, _n=nd: (0,) * _n)


def _nsa_proj(h, cos, sin, w_in, q_gain, k_gain, *, slc_tile, win_tile, pre_gain=None):
    b, s, d = h.shape
    g = NSA_KV_HEADS
    kvd = g * HEAD_DIM
    tm = min(TOK_TILE, s)
    sizes = [MIX_WIDTH] + [kvd] * 6 + [3 * N_HEADS]
    off = np.cumsum([0] + sizes)
    wt = w_in.T.astype(BF)
    wq = wt[off[0]:off[1]]
    wk = jnp.concatenate([wt[off[1]:off[2]], wt[off[3]:off[4]], wt[off[5]:off[6]]], axis=0)
    wv = jnp.concatenate([wt[off[2]:off[3]], wt[off[4]:off[5]], wt[off[6]:off[7]]], axis=0)
    wg = wt[off[7]:off[8]]
    wz = w_in[:, off[8]:].astype(BF)
    qg = q_gain.reshape(HEAD_DIM, 1)
    kg = k_gain.T
    n_t = s // tm
    out_shape = [
        jax.ShapeDtypeStruct((b, MIX_WIDTH, s), BF),
        jax.ShapeDtypeStruct((b, g, s, HEAD_DIM), F32),
        jax.ShapeDtypeStruct((b, g, s, HEAD_DIM), F32),
        jax.ShapeDtypeStruct((b, g, s, SLC_KEY_LANES), BF),
        jax.ShapeDtypeStruct((b, g, s, KEY_PAD), BF),
        jax.ShapeDtypeStruct((b, s // slc_tile, kvd, slc_tile), BF),
        jax.ShapeDtypeStruct((b, s // win_tile, kvd, win_tile), BF),
        jax.ShapeDtypeStruct((b, 3 * N_HEADS, s), F32),
        jax.ShapeDtypeStruct((b, s, MIX_WIDTH), BF),
    ]
    out_specs = [
        pl.BlockSpec((1, MIX_WIDTH, tm), lambda i, j: (i, 0, j)),
        pl.BlockSpec((1, g, tm, HEAD_DIM), lambda i, j: (i, 0, j, 0)),
        pl.BlockSpec((1, g, tm, HEAD_DIM), lambda i, j: (i, 0, j, 0)),
        pl.BlockSpec((1, g, tm, SLC_KEY_LANES), lambda i, j: (i, 0, j, 0)),
        pl.BlockSpec((1, g, tm, KEY_PAD), lambda i, j: (i, 0, j, 0)),
        pl.BlockSpec((1, tm // slc_tile, kvd, slc_tile), lambda i, j: (i, j, 0, 0)),
        pl.BlockSpec((1, tm // win_tile, kvd, win_tile), lambda i, j: (i, j, 0, 0)),
        pl.BlockSpec((1, 3 * N_HEADS, tm), lambda i, j: (i, 0, j)),
        pl.BlockSpec((1, tm, MIX_WIDTH), lambda i, j: (i, j, 0)),
    ]
    norm_in = pre_gain is not None
    args = [h] + ([pre_gain.reshape(1, d)] if norm_in else []) + [cos, sin, wq, wk, wv, wg, wz, qg, kg]
    in_specs = [pl.BlockSpec((1, tm, d), lambda i, j: (i, j, 0))] + ([_full((1, d))] if norm_in else []) + [
        pl.BlockSpec((1, HEAD_DIM // 2, tm), lambda i, j: (i, 0, j)),
        pl.BlockSpec((1, HEAD_DIM // 2, tm), lambda i, j: (i, 0, j)),
        _full(wq.shape), _full(wk.shape), _full(wv.shape), _full(wg.shape), _full(wz.shape),
        _full(qg.shape), _full(kg.shape),
    ]
    return pl.pallas_call(
        functools.partial(_nsa_proj_kernel, slc_tile=slc_tile, win_tile=win_tile, norm_in=norm_in),
        grid=(b, n_t), in_specs=in_specs, out_specs=out_specs, out_shape=out_shape,
        scratch_shapes=[pltpu.VMEM((MIX_WIDTH, tm), F32)],
        compiler_params=_params(("parallel", "parallel")),
        name="nsa_proj",
    )(*args)


def _gelu_tanh(x):
    c = np.float32(np.sqrt(2.0 / np.pi))
    return 0.5 * x * (1.0 + jnp.tanh(c * (x + 0.044715 * (x * x * x))))


def _compress_kernel(kc_ref, vc_ref, pos_ref, w1_ref, w2_ref, kg_ref, kcmp_out, vcmp_out):
    for which, (src, dst) in enumerate(((kc_ref, kcmp_out), (vc_ref, vcmp_out))):
        x = src[0, 0]
        n = x.shape[0]
        half = x.shape[1]
        xa = (x + pos_ref[which, 0:1, :]).astype(BF)
        xb = (x + pos_ref[which, 1:2, :]).astype(BF)
        ua = _dot(xa, w1_ref[which, :half, :])
        ub = _dot(xb, w1_ref[which, half:, :])
        row = lax.broadcasted_iota(jnp.int32, (n, 1), 0)
        ub_next = jnp.where(row == n - 1, 0.0, pltpu.roll(ub, n - 1, 0))
        hid = _gelu_tanh(ua + ub_next)
        y = _dot(hid.astype(BF), w2_ref[which])
        if which == 0:
            y = _rms_rows(y, kg_ref[...])
            dst[0, 0] = y.astype(BF)
        else:
            pad = jnp.zeros((n, KEY_PAD - HEAD_DIM), F32)
            dst[0, 0] = jnp.concatenate([y, pad], axis=1).T[:HEAD_DIM].astype(BF)


def _compress(kc_tok, vc_tok, cmp_pos, cmp_w1, cmp_w2, k_gain0):
    b, g, s, _ = kc_tok.shape
    n_chunk = s // NSA_CMP_STRIDE
    flat = NSA_CMP_STRIDE * HEAD_DIM
    kc = kc_tok.reshape(b, g, n_chunk, flat)
    vc = vc_tok.reshape(b, g, n_chunk, flat)
    pos = cmp_pos.reshape(2, 2, flat)
    w1 = cmp_w1.astype(BF)
    w2 = cmp_w2.astype(BF)
    blk = pl.BlockSpec((1, 1, n_chunk, flat), lambda i, j: (i, j, 0, 0))
    return pl.pallas_call(
        _compress_kernel,
        grid=(b, g),
        in_specs=[blk, blk, _full(pos.shape), _full(w1.shape), _full(w2.shape),
                  _full((1, HEAD_DIM))],
        out_specs=[pl.BlockSpec((1, 1, n_chunk, HEAD_DIM), lambda i, j: (i, j, 0, 0)),
                   pl.BlockSpec((1, 1, HEAD_DIM, n_chunk), lambda i, j: (i, j, 0, 0))],
        out_shape=[jax.ShapeDtypeStruct((b, g, n_chunk, HEAD_DIM), BF),
                   jax.ShapeDtypeStruct((b, g, HEAD_DIM, n_chunk), BF)],
        compiler_params=_params(("parallel", "parallel")),
        name="nsa_compress",
    )(kc, vc, pos, w1, w2, k_gain0.reshape(1, HEAD_DIM))


def _gate_row(gate_ref, n_heads, grp=0):
    return jnp.concatenate([gate_ref[0, 0, grp, r:r + 1, :] for r in range(n_heads)], axis=1)


def _cmp_branch(rows, q_ref, kc_ref, vc_ref, ov_ref, gate_ref, o_out, sel_out, *, tq, n_blk, n_grp):
    r_heads = N_HEADS // NSA_KV_HEADS
    nq = r_heads * tq
    width = r_heads * HEAD_DIM
    q0 = pl.program_id(2) * tq
    n_cmp = kc_ref.shape[2] - 1
    t_row = q0 + (lax.broadcasted_iota(jnp.int32, (1, nq), 1) & (tq - 1))
    lo = max(rows - CMP_MASK_ROWS, 0)
    c_col = lo + lax.broadcasted_iota(jnp.int32, (rows - lo, 1), 0)
    valid = (c_col * NSA_CMP_STRIDE + (NSA_CMP_LEN - 1) <= t_row) & (c_col < n_cmp)
    one_row = jnp.where(lax.broadcasted_iota(jnp.int32, (V_ROWS - HEAD_DIM, rows), 0) == 0, 1.0, 0.0).astype(BF)
    n_live = min(n_blk, rows * NSA_CMP_STRIDE // NSA_SLC_LEN + 8)
    t1 = q0 + lax.broadcasted_iota(jnp.int32, (1, tq), 1)
    cur = lax.shift_right_logical(t1, int(np.log2(NSA_SLC_LEN)))
    blk = lax.broadcasted_iota(jnp.int32, (n_live, tq), 0)
    forced = (blk == 0) | (blk == cur) | (blk == cur - 1)
    imps = []
    for gi in range(n_grp):
        q4 = jnp.concatenate([q_ref[0, (gi * r_heads + r) * HEAD_DIM:(gi * r_heads + r + 1) * HEAD_DIM, :]
                              for r in range(r_heads)], axis=1)
        s = _dot(kc_ref[0, gi, 0:rows, :], q4)
        s_new = jnp.where(valid, s[lo:], NEG_INF)
        m = jnp.max(s_new, axis=0, keepdims=True)
        if lo:
            m = jnp.maximum(m, jnp.max(s[:lo], axis=0, keepdims=True))
        e = jnp.where(valid, jnp.exp2(s_new - m), 0.0).astype(BF)
        if lo:
            e = jnp.concatenate([jnp.exp2(s[:lo] - m).astype(BF), e], axis=0)
        lhs = jnp.concatenate([vc_ref[0, gi, :, 0:rows], one_row, ov_ref[0:n_live, 0:rows]], axis=0)
        res = _dot(lhs, e)
        l = res[HEAD_DIM:HEAD_DIM + 1]
        inv = jnp.where(l > 0.0, 1.0 / jnp.where(l > 0.0, l, 1.0), 0.0)
        o = res[:HEAD_DIM] * (inv * _gate_row(gate_ref, r_heads, gi))
        o_rows = jnp.concatenate([o[:, r * tq:(r + 1) * tq] for r in range(r_heads)], axis=0)
        o_out[0, :, gi * width:(gi + 1) * width] = o_rows.T.astype(o_out.dtype)
        w = res[V_ROWS:V_ROWS + n_live] * inv
        imp = w[:, 0:tq]
        for r in range(1, r_heads):
            imp = imp + w[:, r * tq:(r + 1) * tq]
        imps.append(jnp.where(forced, BIG, jnp.where(blk > cur, NEG_INF, imp)))
    for _ in range(min(NSA_TOPK, n_blk)):
        for gi in range(n_grp):
            best = jnp.max(imps[gi], axis=0, keepdims=True)
            first = jnp.min(jnp.where(imps[gi] == best, blk, n_blk), axis=0, keepdims=True)
            imps[gi] = jnp.where(blk == first, -jnp.inf, imps[gi])
    for gi in range(n_grp):
        sel_out[0, gi, 0:n_live, :] = jnp.where(imps[gi] == -jnp.inf, 0.0, SEL_OFF).astype(BF)
        if n_live < n_blk:
            sel_out[0, gi, n_live:n_blk, :] = jnp.full((n_blk - n_live, tq), SEL_OFF, BF)


def _cmp_select_kernel(q_ref, kc_ref, vc_ref, ov_ref, gate_ref, o_out, sel_out, *, tq, n_blk, n_grp):
    q0 = pl.program_id(2) * tq
    n_chunk = kc_ref.shape[2]
    n_need = jnp.minimum((q0 + tq - NSA_CMP_LEN) // NSA_CMP_STRIDE + 1, n_chunk - 1)
    n_steps = n_chunk // CMP_BLOCK
    need_steps = (n_need + CMP_BLOCK - 1) // CMP_BLOCK
    for k in range(1, n_steps + 1):
        @pl.when(need_steps == k)
        def _(k=k):
            _cmp_branch(k * CMP_BLOCK, q_ref, kc_ref, vc_ref, ov_ref, gate_ref, o_out, sel_out,
                        tq=tq, n_blk=n_blk, n_grp=n_grp)


def _overlap_matrix(s):
    n_chunk = s // NSA_CMP_STRIDE
    n_blk = s // NSA_SLC_LEN
    c0 = np.arange(n_chunk) * NSA_CMP_STRIDE
    c1 = c0 + NSA_CMP_LEN - 1
    b0 = np.arange(n_blk) * NSA_SLC_LEN
    ov = np.minimum(c1[None, :], b0[:, None] + NSA_SLC_LEN - 1) - np.maximum(c0[None, :], b0[:, None]) + 1
    return jnp.asarray(np.clip(ov, 0, None) / NSA_CMP_LEN, BF)


def _cmp_select(qT, k_cmp, v_cmpT, gates5, *, tq):
    b, _, s = qT.shape
    g = NSA_KV_HEADS
    r_heads = N_HEADS // g
    n_chunk = k_cmp.shape[2]
    n_blk = s // NSA_SLC_LEN
    ov = _overlap_matrix(s)
    n_grp = CMP_GROUPS
    rows = n_grp * r_heads * HEAD_DIM
    return pl.pallas_call(
        functools.partial(_cmp_select_kernel, tq=tq, n_blk=n_blk, n_grp=n_grp),
        grid=(b, g // n_grp, s // tq),
        in_specs=[
            pl.BlockSpec((1, rows, tq), lambda i, j, k: (i, j, k)),
            pl.BlockSpec((1, n_grp, n_chunk, HEAD_DIM), lambda i, j, k: (i, j, 0, 0)),
            pl.BlockSpec((1, n_grp, HEAD_DIM, n_chunk), lambda i, j, k: (i, j, 0, 0)),
            pl.BlockSpec((n_blk, n_chunk), lambda i, j, k: (0, 0)),
            pl.BlockSpec((1, 1, n_grp, r_heads, tq), lambda i, j, k: (i, 0, j, 0, k)),
        ],
        out_specs=[pl.BlockSpec((1, tq, rows), lambda i, j, k: (i, k, j)),
                   pl.BlockSpec((1, n_grp, n_blk, tq), lambda i, j, k: (i, j, 0, k))],
        out_shape=[jax.ShapeDtypeStruct((b, s, MIX_WIDTH), BF),
                   jax.ShapeDtypeStruct((b, g, n_blk, s), BF)],
        compiler_params=_params(("parallel", "parallel", "parallel")),
        name="nsa_cmp_select",
    )(qT, k_cmp, v_cmpT, ov, gates5)


def _flash_kernel(*refs, mode, n_par, n_rep, dq, kc, tq, tk, cw, window, n_blk, has_sink, has_gate,
                  fuse_out=None):
    it = iter(refs)
    q_ref, k_ref, v_ref = next(it), next(it), next(it)
    sel_ref = next(it) if n_blk else None
    sink_ref = next(it) if has_sink else None
    gate_ref = next(it) if has_gate else None
    qnext_ref = next(it) if mode == "causal" else None
    selnext_ref = next(it) if (mode == "causal" and n_blk) else None
    if fuse_out is None:
        out_ref = next(it)
    else:
        n_extra, has_next = fuse_out
        extra_refs = [next(it) for _ in range(n_extra)]
        z_ref, x_ref, w_ref = next(it), next(it), next(it)
        ng_ref = next(it) if has_next else None
        x_out = next(it)
        h_out = next(it) if has_next else None
    qs_ref, m_ref, acc_ref, mt_ref = (next(it) for _ in range(4))
    qn_ref = next(it) if mode == "causal" else None
    s_bufs = tuple(it)

    nq = n_rep * tq
    items = [(p, c) for p in range(n_par) for c in range(nq // cw)]
    n_items = len(items)
    grp = pl.program_id(1)
    q0 = pl.program_id(2) * tq
    acc_row = lax.broadcasted_iota(jnp.int32, (V_ROWS, nq), 0)
    for p in range(n_par):
        for r in range(n_rep):
            hd = p * n_rep + r
            qs_ref[p, 0:dq, r * tq:(r + 1) * tq] = q_ref[0, hd * dq:(hd + 1) * dq, :]
        if n_blk:
            qs_ref[p, dq:dq + n_blk, :] = jnp.concatenate([sel_ref[0, 0]] * n_rep, axis=1)
            if dq + n_blk < kc:
                qs_ref[p, dq + n_blk:kc, :] = jnp.zeros((kc - dq - n_blk, nq), BF)
        if has_sink:
            m_ref[p] = jnp.concatenate(
                [jnp.full((1, tq), sink_ref[(grp * n_par + p) * n_rep + r] * LOG2E, F32) for r in range(n_rep)],
                axis=1)
            acc_ref[p] = jnp.where(acc_row == HEAD_DIM, 1.0, 0.0)
        else:
            m_ref[p] = jnp.full((1, nq), M_INIT, F32)
            acc_ref[p] = jnp.zeros((V_ROWS, nq), F32)

    t_row = q0 + (lax.broadcasted_iota(jnp.int32, (1, nq), 1) & (tq - 1))
    one_row = jnp.where(lax.broadcasted_iota(jnp.int32, (V_ROWS - HEAD_DIM, tk), 0) == 0, 1.0, 0.0).astype(BF)

    def stage_a(item, j, key0, slot, kind):
        p, c = item
        cols = slice(c * cw, (c + 1) * cw)
        s = _dot(k_ref[0, p, j][:, :kc], qs_ref[p, :, cols])
        if kind is not None:
            key = key0 + lax.broadcasted_iota(jnp.int32, (tk, 1), 0)
            ok = key <= t_row[:, cols] if kind == "causal" else key > t_row[:, cols] - window
            s = jnp.where(ok, s, NEG_INF)
        s_bufs[slot][...] = s
        mt_ref[slot] = jnp.max(s, axis=0, keepdims=True)

    def stage_b(item, j, slot):
        p, c = item
        cols = slice(c * cw, (c + 1) * cw)
        m_old = m_ref[p, :, cols]
        m_new = jnp.maximum(m_old, mt_ref[slot])
        pr = jnp.exp2(s_bufs[slot][...] - m_new).astype(BF)
        alpha = jnp.exp2(m_old - m_new)
        v = jnp.concatenate([v_ref[0, j, p * HEAD_DIM:(p + 1) * HEAD_DIM, :], one_row], axis=0)
        acc_ref[p, :, cols] = alpha * acc_ref[p, :, cols] + _dot(v, pr)
        m_ref[p, :, cols] = m_new

    def finalize():
        outs = []
        for p in range(n_par):
            acc = acc_ref[p]
            o = acc[:HEAD_DIM] * (1.0 / acc[HEAD_DIM:HEAD_DIM + 1])
            if has_gate:
                o = o * _gate_row(gate_ref, n_rep, p)
            outs += [o[:, r * tq:(r + 1) * tq] for r in range(n_rep)]
        o_tok = jnp.concatenate(outs, axis=0).T
        if fuse_out is None:
            out_ref[0] = o_tok.astype(out_ref.dtype)
            return
        for r in extra_refs:
            o_tok = o_tok + r[0].astype(F32)
        y = _dot((o_tok * z_ref[0].astype(F32)).astype(BF), w_ref[...])
        x_new = x_ref[0] + y
        x_out[0] = x_new
        if has_next:
            h_out[0] = _rms_rows(x_new, ng_ref[...]).astype(BF)

    if mode == "causal":
        assert n_items % 2 == 0
        n_full = q0 // tk

        def prefetch_next_tile():
            assert cw == tq
            qn_ref[0:dq, :] = qnext_ref[0, 0:dq, :]
            if n_blk:
                qn_ref[dq:dq + n_blk, :] = selnext_ref[0, 0]
                if dq + n_blk < kc:
                    qn_ref[dq + n_blk:kc, :] = jnp.zeros((kc - dq - n_blk, tq), BF)
            s = _dot(k_ref[0, 0, 0][:, :kc], qn_ref[...])
            key = lax.broadcasted_iota(jnp.int32, (tk, 1), 0)
            s = jnp.where(key <= t_row[:, 0:cw] + tq, s, NEG_INF)
            s_bufs[0][...] = s
            mt_ref[0] = jnp.max(s, axis=0, keepdims=True)

        def step(j, kind, next_kind, last):
            for idx, item in enumerate(items):
                slot = idx % 2
                if idx + 1 < n_items:
                    stage_a(items[idx + 1], j, j * tk, 1 - slot, kind)
                elif not last:
                    stage_a(items[0], j + 1, (j + 1) * tk, 1 - slot, next_kind)
                else:
                    prefetch_next_tile()
                stage_b(item, j, slot)

        @pl.when(pl.program_id(2) == 0)
        def _():
            stage_a(items[0], 0, 0, 0, "causal")

        def body(j, carry):
            step(j, None, None, False)
            return carry

        def body_group(i, carry):
            for u in range(FLASH_UNROLL):
                step(FLASH_UNROLL * i + u, None, None, False)
            return carry

        n_main = jnp.maximum(n_full - 1, 0)
        n_groups = lax.shift_right_logical(n_main, int(np.log2(FLASH_UNROLL)))
        lax.fori_loop(0, n_groups, body_group, 0)
        lax.fori_loop(FLASH_UNROLL * n_groups, n_main, body, 0)

        @pl.when(n_full >= 1)
        def _():
            step(n_full - 1, None, "causal", False)
            step(n_full, "causal", None, True)
            finalize()

        @pl.when(n_full == 0)
        def _():
            step(0, "causal", None, True)
            finalize()
    else:
        w_tiles, q_tiles = window // tk, tq // tk
        work = []
        for i in range(w_tiles + q_tiles):
            jv = q0 // tk - w_tiles + i
            key0 = jnp.where(jv < 0, -(1 << 30), jv * tk)
            kind = "causal" if i >= w_tiles else "lower"
            work += [(item, jnp.maximum(jv, 0), key0, kind) for item in items]
        n_buf = len(s_bufs)
        ahead = WINDOW_LOOKAHEAD
        for n in range(ahead):
            stage_a(work[n][0], work[n][1], work[n][2], n % n_buf, work[n][3])
        for n, (item, j, _, _) in enumerate(work):
            if n + ahead < len(work):
                nxt = work[n + ahead]
                stage_a(nxt[0], nxt[1], nxt[2], (n + ahead) % n_buf, nxt[3])
            stage_b(item, j, n % n_buf)

        finalize()


def _flash(qT, k_tok, vT_tiles, *, mode, n_par, n_rep, dq, kc, tq, tk, cw, window=None,
           sel_bias=None, sinks=None, gates=None, gate_branch=0, out_proj=None, name="flash"):
    b, _, s = qT.shape
    kh, k_lanes = k_tok.shape[1], k_tok.shape[3]
    n_t = s // tk
    k5 = k_tok.reshape(b, kh, n_t, tk, k_lanes)
    n_grp = kh // n_par
    heads = n_par * n_rep
    nq = n_rep * tq
    n_blk = sel_bias.shape[2] if sel_bias is not None else 0
    has_sink, has_gate = sinks is not None, gates is not None
    args = [qT, k5, vT_tiles]
    in_specs = [
        pl.BlockSpec((1, heads * dq, tq), lambda i, j, k: (i, j, k)),
        pl.BlockSpec((1, n_par, n_t, tk, k_lanes), lambda i, j, k: (i, j, 0, 0, 0)),
        pl.BlockSpec((1, n_t, n_par * HEAD_DIM, tk), lambda i, j, k: (i, 0, j, 0)),
    ]
    if n_blk:
        assert dq + n_blk <= kc
        args.append(sel_bias)
        in_specs.append(pl.BlockSpec((1, 1, n_blk, tq), lambda i, j, k: (i, j, 0, k)))
    if has_sink:
        args.append(sinks.astype(F32))
        in_specs.append(pl.BlockSpec(memory_space=pltpu.SMEM))
    if has_gate:
        args.append(gates)
        in_specs.append(pl.BlockSpec((1, 1, n_par, n_rep, tq),
                                     lambda i, j, k, _br=gate_branch: (i, _br, j, 0, k)))
    n_sbuf = 2 if mode == "causal" else WINDOW_SCORE_BUFS
    scratch = [pltpu.VMEM((n_par, kc, nq), BF), pltpu.VMEM((n_par, 1, nq), F32),
               pltpu.VMEM((n_par, V_ROWS, nq), F32), pltpu.VMEM((n_sbuf, 1, cw), F32)]
    if mode == "causal":
        last_q = s // tq - 1
        args.append(qT)
        in_specs.append(pl.BlockSpec((1, dq, tq), lambda i, j, k: (i, j * heads, jnp.minimum(k + 1, last_q))))
        if n_blk:
            args.append(sel_bias)
            in_specs.append(pl.BlockSpec((1, 1, n_blk, tq), lambda i, j, k: (i, j, 0, jnp.minimum(k + 1, last_q))))
        scratch.append(pltpu.VMEM((kc, tq), BF))
    scratch += [pltpu.VMEM((tk, cw), F32)] * n_sbuf
    out_specs = pl.BlockSpec((1, tq, heads * HEAD_DIM), lambda i, j, k: (i, k, j))
    out_shape = jax.ShapeDtypeStruct((b, s, MIX_WIDTH), BF)
    fuse_out = None
    if out_proj is not None:
        assert mode == "window" and heads * HEAD_DIM == MIX_WIDTH and n_grp == 1
        extra, zs, x, w_out, next_gain = out_proj
        tok = pl.BlockSpec((1, tq, D_MODEL), lambda i, j, k: (i, k, 0))
        args += list(extra) + [zs, x, w_out.astype(BF)]
        in_specs += [tok] * (len(extra) + 2) + [pl.BlockSpec(w_out.shape, lambda i, j, k: (0, 0))]
        out_specs, out_shape = [tok], [jax.ShapeDtypeStruct(x.shape, F32)]
        if next_gain is not None:
            args.append(next_gain.reshape(1, D_MODEL))
            in_specs.append(pl.BlockSpec((1, D_MODEL), lambda i, j, k: (0, 0)))
            out_specs.append(tok)
            out_shape.append(jax.ShapeDtypeStruct(x.shape, BF))
        fuse_out = (len(extra), next_gain is not None)
    kern = functools.partial(_flash_kernel, mode=mode, n_par=n_par, n_rep=n_rep, dq=dq, kc=kc, tq=tq, tk=tk,
                             cw=cw, window=window, n_blk=n_blk, has_sink=has_sink, has_gate=has_gate,
                             fuse_out=fuse_out)
    res = pl.pallas_call(
        kern,
        grid=(b, n_grp, s // tq),
        in_specs=in_specs,
        out_specs=out_specs,
        out_shape=out_shape,
        scratch_shapes=scratch,
        compiler_params=_params(("parallel", "parallel", "arbitrary")),
        name=name,
    )(*args)
    if out_proj is None:
        return res
    return (res[0], res[1]) if next_gain is not None else (res[0], None)


def _swa_proj_kernel(h_ref, cos_ref, sin_ref, wq_ref, wk_ref, wv_ref, wz_ref, qg_ref, kg_ref,
                     q_out, k_out, v_out, z_out, y_ref, *, win_tile):
    h = h_ref[0]
    cos, sin = cos_ref[0], sin_ref[0]
    y_ref[...] = _dot_nt(wq_ref[...], h)
    for hd in range(N_HEADS):
        rows = slice(hd * HEAD_DIM, (hd + 1) * HEAD_DIM)
        q_out[0, rows, :] = (_rope_t(_head_rms_t(y_ref[rows, :], qg_ref[...]), cos, sin) * Q_SCALE).astype(BF)
    kvd = SWA_KV_HEADS * HEAD_DIM
    y_ref[0:kvd, :] = _dot_nt(wk_ref[...], h)
    for gi in range(SWA_KV_HEADS):
        rows = slice(gi * HEAD_DIM, (gi + 1) * HEAD_DIM)
        k_out[0, gi] = _to_token_major(_rope_t(_head_rms_t(y_ref[rows, :], kg_ref[...]), cos, sin)).astype(BF)
    _store_lane_tiles(v_out, _dot_nt(wv_ref[...], h).astype(BF), win_tile)
    z_out[0] = _silu(_dot(h, wz_ref[...])).astype(BF)


def _swa_proj(h, cos, sin, w_in, q_gain, k_gain, *, win_tile):
    b, s, d = h.shape
    g = SWA_KV_HEADS
    kvd = g * HEAD_DIM
    tm = min(TOK_TILE, s)
    wt = w_in.T.astype(BF)
    wq, wk, wv = wt[:MIX_WIDTH], wt[MIX_WIDTH:MIX_WIDTH + kvd], wt[MIX_WIDTH + kvd:MIX_WIDTH + 2 * kvd]
    wz = w_in[:, MIX_WIDTH + 2 * kvd:].astype(BF)
    qg, kg = q_gain.reshape(HEAD_DIM, 1), k_gain.reshape(HEAD_DIM, 1)
    wpt = tm // win_tile
    half = HEAD_DIM // 2
    return pl.pallas_call(
        functools.partial(_swa_proj_kernel, win_tile=win_tile),
        grid=(b, s // tm),
        in_specs=[pl.BlockSpec((1, tm, d), lambda i, j: (i, j, 0)),
                  pl.BlockSpec((1, half, tm), lambda i, j: (i, 0, j)),
                  pl.BlockSpec((1, half, tm), lambda i, j: (i, 0, j)),
                  _full(wq.shape), _full(wk.shape), _full(wv.shape), _full(wz.shape),
                  _full(qg.shape), _full(kg.shape)],
        out_specs=[pl.BlockSpec((1, MIX_WIDTH, tm), lambda i, j: (i, 0, j)),
                   pl.BlockSpec((1, g, tm, KEY_PAD), lambda i, j: (i, 0, j, 0)),
                   pl.BlockSpec((1, wpt, kvd, win_tile), lambda i, j: (i, j, 0, 0)),
                   pl.BlockSpec((1, tm, MIX_WIDTH), lambda i, j: (i, j, 0))],
        out_shape=[jax.ShapeDtypeStruct((b, MIX_WIDTH, s), BF),
                   jax.ShapeDtypeStruct((b, g, s, KEY_PAD), BF),
                   jax.ShapeDtypeStruct((b, s // win_tile, kvd, win_tile), BF),
                   jax.ShapeDtypeStruct((b, s, MIX_WIDTH), BF)],
        scratch_shapes=[pltpu.VMEM((MIX_WIDTH, tm), F32)],
        compiler_params=_params(("parallel", "parallel")),
        name="swa_proj",
    )(h, cos, sin, wq, wk, wv, wz, qg, kg)


def _split3(x):
    hi = x.astype(BF)
    r1 = x - hi.astype(F32)
    mid = r1.astype(BF)
    lo = (r1 - mid.astype(F32)).astype(BF)
    return hi, mid, lo


def _forget_cum_kernel(h_ref, wf_ref, bias_ref, tri_ref, cum_out, carry_ref):
    @pl.when(pl.program_id(1) == 0)
    def _():
        carry_ref[...] = jnp.zeros_like(carry_ref)

    x = _dot_nt(wf_ref[...], h_ref[0]) + bias_ref[...]
    logf = jnp.minimum(x, 0.0) - jnp.log(1.0 + jnp.exp(-jnp.abs(x)))
    tri = tri_ref[...]
    hi, mid, lo = _split3(logf)
    cum = (_dot(hi, tri) + _dot(mid, tri)) + _dot(lo, tri) + carry_ref[:, 0:1]
    cum_out[0] = cum
    carry_ref[...] = jnp.broadcast_to(cum[:, -1:], carry_ref.shape)


def _forget_cum(h, wf, bias):
    b, s, d = h.shape
    tm = min(TOK_TILE, s)
    tri = jnp.asarray(np.arange(tm)[:, None] <= np.arange(tm)[None, :], BF)
    return pl.pallas_call(
        _forget_cum_kernel,
        grid=(b, s // tm),
        in_specs=[pl.BlockSpec((1, tm, d), lambda i, j: (i, j, 0)),
                  _full(wf.shape), _full((N_HEADS, 1)), _full(tri.shape)],
        out_specs=pl.BlockSpec((1, N_HEADS, tm), lambda i, j: (i, 0, j)),
        out_shape=jax.ShapeDtypeStruct((b, N_HEADS, s), F32),
        scratch_shapes=[pltpu.VMEM((N_HEADS, LANES), F32)],
        compiler_params=_params(("parallel", "arbitrary")),
        name="fox_forget_cum",
    )(h, wf, bias.reshape(N_HEADS, 1).astype(F32), tri)


def _fox_proj_kernel(h_ref, cum_ref, wq_ref, wk_ref, wv_ref, wz_ref, qg_ref, kg_ref,
                     q_out, k_out, v_out, z_out, yq_ref, yk_ref, *, tile):
    h = h_ref[0]
    tm = h.shape[0]
    yq_ref[...] = _dot_nt(wq_ref[...], h)
    yk_ref[...] = _dot_nt(wk_ref[...], h)
    row = lax.broadcasted_iota(jnp.int32, (8, tm), 0)
    zeros = jnp.zeros((KEY_PAD - HEAD_DIM - 16, tm), F32)
    for hd in range(N_HEADS):
        rows = slice(hd * HEAD_DIM, (hd + 1) * HEAD_DIM)
        c_hi, c_mid, c_lo = (c.astype(F32) for c in _split3(cum_ref[0, hd:hd + 1, :] * LOG2E))
        c3 = jnp.where(row == 0, c_hi, jnp.where(row == 1, c_mid, jnp.where(row == 2, c_lo, 0.0)))
        one3 = jnp.where(row < 3, 1.0, 0.0)
        q = _head_rms_t(yq_ref[rows, :], qg_ref[...]) * Q_SCALE
        q_out[0, hd] = jnp.concatenate([q, one3, c3, zeros], axis=0).astype(BF)
        k = _head_rms_t(yk_ref[rows, :], kg_ref[...])
        k_out[0, hd] = jnp.concatenate([k, -c3, one3, zeros], axis=0).T.astype(BF)
    _store_lane_tiles(v_out, _dot_nt(wv_ref[...], h).astype(BF), tile)
    z_out[0] = _silu(_dot(h, wz_ref[...])).astype(BF)


def _fox_proj(h, cum, w_in, q_gain, k_gain, *, tile):
    b, s, d = h.shape
    tm = min(TOK_TILE, s)
    wt = w_in.T.astype(BF)
    wq, wk, wv = wt[:MIX_WIDTH], wt[MIX_WIDTH:2 * MIX_WIDTH], wt[2 * MIX_WIDTH:3 * MIX_WIDTH]
    wz = w_in[:, 3 * MIX_WIDTH + N_HEADS:].astype(BF)
    qg, kg = q_gain.reshape(HEAD_DIM, 1), k_gain.reshape(HEAD_DIM, 1)
    return pl.pallas_call(
        functools.partial(_fox_proj_kernel, tile=tile),
        grid=(b, s // tm),
        in_specs=[pl.BlockSpec((1, tm, d), lambda i, j: (i, j, 0)),
                  pl.BlockSpec((1, N_HEADS, tm), lambda i, j: (i, 0, j)),
                  _full(wq.shape), _full(wk.shape), _full(wv.shape), _full(wz.shape),
                  _full(qg.shape), _full(kg.shape)],
        out_specs=[pl.BlockSpec((1, N_HEADS, KEY_PAD, tm), lambda i, j: (i, 0, 0, j)),
                   pl.BlockSpec((1, N_HEADS, tm, KEY_PAD), lambda i, j: (i, 0, j, 0)),
                   pl.BlockSpec((1, tm // tile, MIX_WIDTH, tile), lambda i, j: (i, j, 0, 0)),
                   pl.BlockSpec((1, tm, MIX_WIDTH), lambda i, j: (i, j, 0))],
        out_shape=[jax.ShapeDtypeStruct((b, N_HEADS, KEY_PAD, s), BF),
                   jax.ShapeDtypeStruct((b, N_HEADS, s, KEY_PAD), BF),
                   jax.ShapeDtypeStruct((b, s // tile, MIX_WIDTH, tile), BF),
                   jax.ShapeDtypeStruct((b, s, MIX_WIDTH), BF)],
        scratch_shapes=[pltpu.VMEM((MIX_WIDTH, tm), F32), pltpu.VMEM((MIX_WIDTH, tm), F32)],
        compiler_params=_params(("parallel", "parallel")),
        name="fox_proj",
    )(h, cum, wq, wk, wv, wz, qg, kg)


def _out_proj_kernel(*refs, n_o, has_next):
    o_refs = refs[:n_o]
    z_ref, x_ref, w_ref = refs[n_o:n_o + 3]
    rest = refs[n_o + 3:]
    if has_next:
        g_ref, x_out, h_out = rest
    else:
        (x_out,) = rest
    o = o_refs[0][0].astype(F32)
    for r in o_refs[1:]:
        o = o + r[0].astype(F32)
    y = _dot((o * z_ref[0].astype(F32)).astype(BF), w_ref[...])
    x_new = x_ref[0] + y
    x_out[0] = x_new
    if has_next:
        h_out[0] = _rms_rows(x_new, g_ref[...]).astype(BF)


def _out_proj(o_list, zs, x, w_out, next_gain):
    b, s, d = x.shape
    tm = min(TOK_TILE, s)
    has_next = next_gain is not None
    blk = pl.BlockSpec((1, tm, d), lambda i, j: (i, j, 0))
    args = list(o_list) + [zs, x, w_out.astype(BF)]
    in_specs = [blk] * (len(o_list) + 2) + [_full(w_out.shape)]
    out_shape = [jax.ShapeDtypeStruct((b, s, d), F32)]
    out_specs = [blk]
    if has_next:
        args.append(next_gain.reshape(1, d))
        in_specs.append(_full((1, d)))
        out_shape.append(jax.ShapeDtypeStruct((b, s, d), BF))
        out_specs.append(blk)
    res = pl.pallas_call(
        functools.partial(_out_proj_kernel, n_o=len(o_list), has_next=has_next),
        grid=(b, s // tm), in_specs=in_specs, out_specs=out_specs, out_shape=out_shape,
        compiler_params=_params(("parallel", "parallel")),
        name="out_proj",
    )(*args)
    return (res[0], res[1]) if has_next else (res[0], None)


ITEM_LANES = 512
CMP_Q_TILE = 512
CMP_GROUPS = 2
NSA_SLC_Q_TILE = 512
NSA_SLC_TILE = 512
NSA_WIN_Q_TILE = 256
NSA_WIN_TILE = 256
SWA_TILE = 128
FOX_Q_TILE = 512
FOX_K_TILE = 512
FOX_HEADS_PER_STEP = 4


def _nsa_mixer(x, h, cos, sin, w_in, q_gain, k_gain, cmp_pos, cmp_w1, cmp_w2, w_out, next_gain, pre_gain):
    b, s, _ = x.shape
    g = NSA_KV_HEADS
    r = N_HEADS // g
    qT, kc_tok, vc_tok, ks, kw, vsT, vwT, gates, zs = _nsa_proj(
        x if h is None else h, cos, sin, w_in, q_gain, k_gain, slc_tile=NSA_SLC_TILE, win_tile=NSA_WIN_TILE,
        pre_gain=pre_gain if h is None else None)
    k_cmp, v_cmpT = _compress(kc_tok, vc_tok, cmp_pos, cmp_w1, cmp_w2, k_gain[0])
    gates5 = gates.reshape(b, 3, g, r, s)
    o_cmp, sel_bias = _cmp_select(qT, k_cmp, v_cmpT, gates5, tq=CMP_Q_TILE)
    o_slc = _flash(qT, ks, vsT, mode="causal", n_par=1, n_rep=r, dq=HEAD_DIM, kc=SLC_KEY_LANES,
                   tq=NSA_SLC_Q_TILE, tk=NSA_SLC_TILE, cw=ITEM_LANES, sel_bias=sel_bias,
                   gates=gates5, gate_branch=1, name="nsa_selected")
    return _flash(qT, kw, vwT, mode="window", n_par=g, n_rep=r, dq=HEAD_DIM, kc=HEAD_DIM,
                  tq=NSA_WIN_Q_TILE, tk=NSA_WIN_TILE, cw=ITEM_LANES, window=NSA_WINDOW,
                  gates=gates5, gate_branch=2, out_proj=([o_cmp, o_slc], zs, x, w_out, next_gain),
                  name="nsa_window_out")


def _swa_mixer(x, h, cos, sin, w_in, q_gain, k_gain, sinks, w_out, next_gain):
    r = N_HEADS // SWA_KV_HEADS
    qT, k, vT, zs = _swa_proj(h, cos, sin, w_in, q_gain, k_gain, win_tile=SWA_TILE)
    return _flash(qT, k, vT, mode="window", n_par=SWA_KV_HEADS, n_rep=r, dq=HEAD_DIM, kc=HEAD_DIM, tq=SWA_TILE,
                  tk=SWA_TILE, cw=r * SWA_TILE, window=SWA_WINDOW, sinks=sinks,
                  out_proj=([], zs, x, w_out, next_gain), name="swa_window_out")


def _fox_mixer(x, h, w_in, forget_bias, q_gain, k_gain, w_out, next_gain):
    b, s, _ = h.shape
    wf = w_in[:, 3 * MIX_WIDTH:3 * MIX_WIDTH + N_HEADS].T.astype(BF)
    cum = _forget_cum(h, wf, forget_bias)
    qT, k, vT, zs = _fox_proj(h, cum, w_in, q_gain, k_gain, tile=FOX_K_TILE)
    o = _flash(qT.reshape(b, N_HEADS * KEY_PAD, s), k, vT, mode="causal", n_par=FOX_HEADS_PER_STEP, n_rep=1, dq=KEY_PAD,
               kc=KEY_PAD, tq=FOX_Q_TILE, tk=FOX_K_TILE, cw=ITEM_LANES, name="fox_attention")
    return _out_proj([o], zs, x, w_out, next_gain)


def kernel(x, positions, norm_gains, a_w_in, a_q_gain, a_k_gain, a_cmp_pos, a_cmp_w1, a_cmp_w2, a_w_out,
           b_w_in, b_q_gain, b_k_gain, b_sinks, b_w_out,
           c_w_in, c_forget_bias, c_q_gain, c_k_gain, c_w_out):
    depth = norm_gains.shape[0]
    cos, sin = _rope_tables(positions)
    h = None
    for i in range(depth):
        j, mixer = divmod(i, 3)
        next_gain = norm_gains[i + 1] if i + 1 < depth else None
        if mixer == 0:
            x, h = _nsa_mixer(x, h, cos, sin, a_w_in[j], a_q_gain[j], a_k_gain[j],
                              a_cmp_pos[j], a_cmp_w1[j], a_cmp_w2[j], a_w_out[j], next_gain, norm_gains[i])
        elif mixer == 1:
            x, h = _swa_mixer(x, h, cos, sin, b_w_in[j], b_q_gain[j], b_k_gain[j], b_sinks[j],
                              b_w_out[j], next_gain)
        else:
            x, h = _fox_mixer(x, h, c_w_in[j], c_forget_bias[j], c_q_gain[j], c_k_gain[j],
                              c_w_out[j], next_gain)
    return x
```

```python
import functools

import jax
import jax.numpy as jnp
import numpy as np
from jax import lax
from jax.experimental import pallas as pl
from jax.experimental.pallas import tpu as pltpu

D_MODEL = 1024
HEAD_DIM = 64
N_HEADS = 16
MIX_WIDTH = N_HEADS * HEAD_DIM
ROPE_THETA = 10000.0
EPS = 1e-6
SCALE = HEAD_DIM ** -0.5
NEG_INF = -1e30
BIG = 1e30
M_INIT = -1e29
SEL_OFF = -(2.0 ** 100)

NSA_KV_HEADS = 4
NSA_CMP_LEN = 32
NSA_CMP_STRIDE = 16
NSA_SLC_LEN = 64
NSA_TOPK = 16
NSA_WINDOW = 512
SWA_KV_HEADS = 2
SWA_WINDOW = 128

LOG2E = float(np.log2(np.e))
Q_SCALE = SCALE * LOG2E

LANES = 128
KEY_PAD = 128
SLC_KEY_LANES = 256
V_ROWS = 80
FLASH_UNROLL = 4
WINDOW_LOOKAHEAD = 2
WINDOW_SCORE_BUFS = 4
CMP_BLOCK = 128
CMP_MASK_ROWS = 2 * CMP_BLOCK
VMEM_LIMIT = 56 * 1024 * 1024

TOK_TILE = 512
NT_DIMS = (((1,), (1,)), ((), ()))

BF = jnp.bfloat16
F32 = jnp.float32


def _params(sem):
    return pltpu.CompilerParams(dimension_semantics=sem, vmem_limit_bytes=VMEM_LIMIT)


def _dot(a, b):
    return jnp.dot(a, b, preferred_element_type=F32)


def _dot_nt(a, b):
    return lax.dot_general(a, b, NT_DIMS, preferred_element_type=F32)


def _rope_tab_kernel(pos_ref, invf_ref, cos_ref, sin_ref):
    ang = invf_ref[...] * pos_ref[0].astype(F32)
    cos_ref[0] = jnp.cos(ang)
    sin_ref[0] = jnp.sin(ang)


def _rope_tables(positions):
    b, s = positions.shape
    half = HEAD_DIM // 2
    inv_freq = ROPE_THETA ** (-jnp.arange(half, dtype=F32) * 2.0 / HEAD_DIM)
    tm = min(TOK_TILE, s)
    out = jax.ShapeDtypeStruct((b, half, s), F32)
    return pl.pallas_call(
        _rope_tab_kernel,
        grid=(b, s // tm),
        in_specs=[pl.BlockSpec((1, 1, tm), lambda i, j: (i, 0, j)),
                  pl.BlockSpec((half, 1), lambda i, j: (0, 0))],
        out_specs=[pl.BlockSpec((1, half, tm), lambda i, j: (i, 0, j))] * 2,
        out_shape=[out, out],
        compiler_params=_params(("parallel", "parallel")),
        name="rope_tables",
    )(positions.reshape(b, 1, s), inv_freq.reshape(half, 1))


def _rms_rows(x, gain_row):
    y = x * lax.rsqrt(jnp.mean(x * x, axis=-1, keepdims=True) + EPS)
    return y * gain_row


def _head_rms_t(y, gain_col):
    ms = jnp.mean(y * y, axis=0, keepdims=True)
    return (y * lax.rsqrt(ms + EPS)) * gain_col


def _rope_t(y, cos, sin):
    half = HEAD_DIM // 2
    x1, x2 = y[:half], y[half:]
    return jnp.concatenate([x1 * cos - x2 * sin, x2 * cos + x1 * sin], axis=0)


def _to_token_major(y):
    pad = jnp.zeros((KEY_PAD - y.shape[0], y.shape[1]), y.dtype)
    return jnp.concatenate([y, pad], axis=0).T


def _silu(z):
    return z * (1.0 / (1.0 + jnp.exp(-z)))


def _sigmoid(z):
    return 1.0 / (1.0 + jnp.exp(-z))


def _store_lane_tiles(out_ref, y, tile):
    for c in range(y.shape[1] // tile):
        out_ref[0, c] = y[:, c * tile:(c + 1) * tile]


def _store_chunk_rows(out_ref, gi, yt, tok_ref):
    tok_ref[...] = yt
    n = yt.shape[0] // NSA_CMP_STRIDE
    for m in range(NSA_CMP_STRIDE // 2):
        even = tok_ref[pl.ds(2 * m, n, stride=NSA_CMP_STRIDE), :]
        odd = tok_ref[pl.ds(2 * m + 1, n, stride=NSA_CMP_STRIDE), :]
        out_ref[0, gi, :, KEY_PAD * m:KEY_PAD * (m + 1)] = even + pltpu.roll(odd, HEAD_DIM, 1)


def _nsa_proj_kernel(*refs, slc_tile, win_tile, norm_in):
    it = iter(refs)
    h_ref = next(it)
    ng_ref = next(it) if norm_in else None
    (cos_ref, sin_ref, wq_ref, wk_ref, wv_ref, wg_ref, wz_ref, qg_ref, kg_ref,
     q_out, kc_out, vc_out, ks_out, kw_out, vs_out, vw_out, g_out, z_out, y_ref, tok_ref) = it
    h = _rms_rows(h_ref[0], ng_ref[...]).astype(BF) if norm_in else h_ref[0]
    cos, sin = cos_ref[0], sin_ref[0]
    g = NSA_KV_HEADS
    kvd = g * HEAD_DIM
    y_ref[...] = _dot_nt(wq_ref[...], h)
    for hd in range(N_HEADS):
        rows = slice(hd * HEAD_DIM, (hd + 1) * HEAD_DIM)
        y = _rope_t(_head_rms_t(y_ref[rows, :], qg_ref[...]), cos, sin) * Q_SCALE
        q_out[0, rows, :] = y.astype(BF)
    y_ref[0:3 * kvd, :] = _dot_nt(wk_ref[...], h)
    tm = h.shape[0]
    tok = pl.program_id(1) * tm + lax.broadcasted_iota(jnp.int32, (tm, 1), 0)
    blk_lane = HEAD_DIM + lax.shift_right_logical(tok, int(np.log2(NSA_SLC_LEN)))
    blk_hot = lax.broadcasted_iota(jnp.int32, (1, SLC_KEY_LANES), 1) == blk_lane
    for kind in range(3):
        for gi in range(g):
            r0 = (kind * g + gi) * HEAD_DIM
            y = y_ref[r0:r0 + HEAD_DIM, :]
            if kind > 0:
                y = _head_rms_t(y, kg_ref[:, kind:kind + 1])
            yt = _to_token_major(_rope_t(y, cos, sin))
            if kind == 0:
                _store_chunk_rows(kc_out, gi, yt, tok_ref)
            elif kind == 1:
                wide = jnp.concatenate([yt, jnp.zeros((tm, SLC_KEY_LANES - KEY_PAD), F32)], axis=1)
                ks_out[0, gi] = jnp.where(blk_hot, 1.0, wide).astype(BF)
            else:
                kw_out[0, gi] = yt.astype(BF)
    y_ref[0:3 * kvd, :] = _dot_nt(wv_ref[...], h)
    for gi in range(g):
        r0 = gi * HEAD_DIM
        _store_chunk_rows(vc_out, gi, _to_token_major(y_ref[r0:r0 + HEAD_DIM, :]), tok_ref)
    _store_lane_tiles(vs_out, y_ref[kvd:2 * kvd, :].astype(BF), slc_tile)
    _store_lane_tiles(vw_out, y_ref[2 * kvd:3 * kvd, :].astype(BF), win_tile)
    g_out[0] = _sigmoid(_dot_nt(wg_ref[...], h))
    z_out[0] = _silu(_dot(h, wz_ref[...])).astype(BF)


def _full(shape):
    nd = len(shape)
    return pl.BlockSpec(shape, lambda i, j, _n=nd: (0,) * _n)


def _nsa_proj(h, cos, sin, w_in, q_gain, k_gain, *, slc_tile, win_tile, pre_gain=None):
    b, s, d = h.shape
    g = NSA_KV_HEADS
    kvd = g * HEAD_DIM
    tm = min(TOK_TILE, s)
    flat = NSA_CMP_STRIDE * HEAD_DIM
    sizes = [MIX_WIDTH] + [kvd] * 6 + [3 * N_HEADS]
    off = np.cumsum([0] + sizes)
    wt = w_in.T.astype(BF)
    wq = wt[off[0]:off[1]]
    wk = jnp.concatenate([wt[off[1]:off[2]], wt[off[3]:off[4]], wt[off[5]:off[6]]], axis=0)
    wv = jnp.concatenate([wt[off[2]:off[3]], wt[off[4]:off[5]], wt[off[6]:off[7]]], axis=0)
    wg = wt[off[7]:off[8]]
    wz = w_in[:, off[8]:].astype(BF)
    qg = q_gain.reshape(HEAD_DIM, 1)
    kg = k_gain.T
    n_t = s // tm
    out_shape = [
        jax.ShapeDtypeStruct((b, MIX_WIDTH, s), BF),
        jax.ShapeDtypeStruct((b, g, s // NSA_CMP_STRIDE, flat), F32),
        jax.ShapeDtypeStruct((b, g, s // NSA_CMP_STRIDE, flat), F32),
        jax.ShapeDtypeStruct((b, g, s, SLC_KEY_LANES), BF),
        jax.ShapeDtypeStruct((b, g, s, KEY_PAD), BF),
        jax.ShapeDtypeStruct((b, s // slc_tile, kvd, slc_tile), BF),
        jax.ShapeDtypeStruct((b, s // win_tile, kvd, win_tile), BF),
        jax.ShapeDtypeStruct((b, 3 * N_HEADS, s), F32),
        jax.ShapeDtypeStruct((b, s, MIX_WIDTH), BF),
    ]
    out_specs = [
        pl.BlockSpec((1, MIX_WIDTH, tm), lambda i, j: (i, 0, j)),
        pl.BlockSpec((1, g, tm // NSA_CMP_STRIDE, flat), lambda i, j: (i, 0, j, 0)),
        pl.BlockSpec((1, g, tm // NSA_CMP_STRIDE, flat), lambda i, j: (i, 0, j, 0)),
        pl.BlockSpec((1, g, tm, SLC_KEY_LANES), lambda i, j: (i, 0, j, 0)),
        pl.BlockSpec((1, g, tm, KEY_PAD), lambda i, j: (i, 0, j, 0)),
        pl.BlockSpec((1, tm // slc_tile, kvd, slc_tile), lambda i, j: (i, j, 0, 0)),
        pl.BlockSpec((1, tm // win_tile, kvd, win_tile), lambda i, j: (i, j, 0, 0)),
        pl.BlockSpec((1, 3 * N_HEADS, tm), lambda i, j: (i, 0, j)),
        pl.BlockSpec((1, tm, MIX_WIDTH), lambda i, j: (i, j, 0)),
    ]
    norm_in = pre_gain is not None
    args = [h] + ([pre_gain.reshape(1, d)] if norm_in else []) + [cos, sin, wq, wk, wv, wg, wz, qg, kg]
    in_specs = [pl.BlockSpec((1, tm, d), lambda i, j: (i, j, 0))] + ([_full((1, d))] if norm_in else []) + [
        pl.BlockSpec((1, HEAD_DIM // 2, tm), lambda i, j: (i, 0, j)),
        pl.BlockSpec((1, HEAD_DIM // 2, tm), lambda i, j: (i, 0, j)),
        _full(wq.shape), _full(wk.shape), _full(wv.shape), _full(wg.shape), _full(wz.shape),
        _full(qg.shape), _full(kg.shape),
    ]
    return pl.pallas_call(
        functools.partial(_nsa_proj_kernel, slc_tile=slc_tile, win_tile=win_tile, norm_in=norm_in),
        grid=(b, n_t), in_specs=in_specs, out_specs=out_specs, out_shape=out_shape,
        scratch_shapes=[pltpu.VMEM((MIX_WIDTH, tm), F32), pltpu.VMEM((tm, KEY_PAD), F32)],
        compiler_params=_params(("parallel", "parallel")),
        name="nsa_proj",
    )(*args)


def _gelu_tanh(x):
    c = np.float32(np.sqrt(2.0 / np.pi))
    return 0.5 * x * (1.0 + jnp.tanh(c * (x + 0.044715 * (x * x * x))))


def _compress_kernel(kc_ref, vc_ref, pos_ref, w1_ref, w2_ref, kg_ref, kcmp_out, vcmp_out):
    for which, (src, dst) in enumerate(((kc_ref, kcmp_out), (vc_ref, vcmp_out))):
        x = src[0, 0]
        n = x.shape[0]
        half = x.shape[1]
        xa = (x + pos_ref[which, 0:1, :]).astype(BF)
        xb = (x + pos_ref[which, 1:2, :]).astype(BF)
        ua = _dot(xa, w1_ref[which, :half, :])
        ub = _dot(xb, w1_ref[which, half:, :])
        row = lax.broadcasted_iota(jnp.int32, (n, 1), 0)
        ub_next = jnp.where(row == n - 1, 0.0, pltpu.roll(ub, n - 1, 0))
        hid = _gelu_tanh(ua + ub_next)
        y = _dot(hid.astype(BF), w2_ref[which])
        if which == 0:
            y = _rms_rows(y, kg_ref[...])
            dst[0, 0] = y.astype(BF)
        else:
            pad = jnp.zeros((n, KEY_PAD - HEAD_DIM), F32)
            dst[0, 0] = jnp.concatenate([y, pad], axis=1).T[:HEAD_DIM].astype(BF)


def _compress(kc, vc, cmp_pos, cmp_w1, cmp_w2, k_gain0):
    b, g, n_chunk, flat = kc.shape
    pos = cmp_pos.reshape(2, 2, flat)
    w1 = cmp_w1.astype(BF)
    w2 = cmp_w2.astype(BF)
    blk = pl.BlockSpec((1, 1, n_chunk, flat), lambda i, j: (i, j, 0, 0))
    return pl.pallas_call(
        _compress_kernel,
        grid=(b, g),
        in_specs=[blk, blk, _full(pos.shape), _full(w1.shape), _full(w2.shape),
                  _full((1, HEAD_DIM))],
        out_specs=[pl.BlockSpec((1, 1, n_chunk, HEAD_DIM), lambda i, j: (i, j, 0, 0)),
                   pl.BlockSpec((1, 1, HEAD_DIM, n_chunk), lambda i, j: (i, j, 0, 0))],
        out_shape=[jax.ShapeDtypeStruct((b, g, n_chunk, HEAD_DIM), BF),
                   jax.ShapeDtypeStruct((b, g, HEAD_DIM, n_chunk), BF)],
        compiler_params=_params(("parallel", "parallel")),
        name="nsa_compress",
    )(kc, vc, pos, w1, w2, k_gain0.reshape(1, HEAD_DIM))


def _gate_row(gate_ref, n_heads, grp=0):
    return jnp.concatenate([gate_ref[0, 0, grp, r:r + 1, :] for r in range(n_heads)], axis=1)


def _cmp_branch(rows, q_ref, kc_ref, vc_ref, ov_ref, gate_ref, o_out, sel_out, *, tq, n_blk, n_grp):
    r_heads = N_HEADS // NSA_KV_HEADS
    nq = r_heads * tq
    width = r_heads * HEAD_DIM
    q0 = pl.program_id(2) * tq
    n_cmp = kc_ref.shape[2] - 1
    t_row = q0 + (lax.broadcasted_iota(jnp.int32, (1, nq), 1) & (tq - 1))
    lo = max(rows - CMP_MASK_ROWS, 0)
    c_col = lo + lax.broadcasted_iota(jnp.int32, (rows - lo, 1), 0)
    valid = (c_col * NSA_CMP_STRIDE + (NSA_CMP_LEN - 1) <= t_row) & (c_col < n_cmp)
    one_row = jnp.where(lax.broadcasted_iota(jnp.int32, (V_ROWS - HEAD_DIM, rows), 0) == 0, 1.0, 0.0).astype(BF)
    n_live = min(n_blk, rows * NSA_CMP_STRIDE // NSA_SLC_LEN + 8)
    t1 = q0 + lax.broadcasted_iota(jnp.int32, (1, tq), 1)
    cur = lax.shift_right_logical(t1, int(np.log2(NSA_SLC_LEN)))
    blk = lax.broadcasted_iota(jnp.int32, (n_live, tq), 0)
    forced = (blk == 0) | (blk == cur) | (blk == cur - 1)
    imps = []
    for gi in range(n_grp):
        q4 = jnp.concatenate([q_ref[0, (gi * r_heads + r) * HEAD_DIM:(gi * r_heads + r + 1) * HEAD_DIM, :]
                              for r in range(r_heads)], axis=1)
        s = _dot(kc_ref[0, gi, 0:rows, :], q4)
        s_new = jnp.where(valid, s[lo:], NEG_INF)
        m = jnp.max(s_new, axis=0, keepdims=True)
        if lo:
            m = jnp.maximum(m, jnp.max(s[:lo], axis=0, keepdims=True))
        e = jnp.where(valid, jnp.exp2(s_new - m), 0.0).astype(BF)
        if lo:
            e = jnp.concatenate([jnp.exp2(s[:lo] - m).astype(BF), e], axis=0)
        lhs = jnp.concatenate([vc_ref[0, gi, :, 0:rows], one_row, ov_ref[0:n_live, 0:rows]], axis=0)
        res = _dot(lhs, e)
        l = res[HEAD_DIM:HEAD_DIM + 1]
        inv = jnp.where(l > 0.0, 1.0 / jnp.where(l > 0.0, l, 1.0), 0.0)
        o = res[:HEAD_DIM] * (inv * _gate_row(gate_ref, r_heads, gi))
        o_rows = jnp.concatenate([o[:, r * tq:(r + 1) * tq] for r in range(r_heads)], axis=0)
        o_out[0, :, gi * width:(gi + 1) * width] = o_rows.T.astype(o_out.dtype)
        w = res[V_ROWS:V_ROWS + n_live] * inv
        imp = w[:, 0:tq]
        for r in range(1, r_heads):
            imp = imp + w[:, r * tq:(r + 1) * tq]
        imps.append(jnp.where(forced, BIG, jnp.where(blk > cur, NEG_INF, imp)))
    for _ in range(min(NSA_TOPK, n_blk)):
        for gi in range(n_grp):
            best = jnp.max(imps[gi], axis=0, keepdims=True)
            first = jnp.min(jnp.where(imps[gi] == best, blk, n_blk), axis=0, keepdims=True)
            imps[gi] = jnp.where(blk == first, -jnp.inf, imps[gi])
    for gi in range(n_grp):
        sel_out[0, gi, 0:n_live, :] = jnp.where(imps[gi] == -jnp.inf, 0.0, SEL_OFF).astype(BF)
        if n_live < n_blk:
            sel_out[0, gi, n_live:n_blk, :] = jnp.full((n_blk - n_live, tq), SEL_OFF, BF)


def _cmp_select_kernel(q_ref, kc_ref, vc_ref, ov_ref, gate_ref, o_out, sel_out, *, tq, n_blk, n_grp):
    q0 = pl.program_id(2) * tq
    n_chunk = kc_ref.shape[2]
    n_need = jnp.minimum((q0 + tq - NSA_CMP_LEN) // NSA_CMP_STRIDE + 1, n_chunk - 1)
    n_steps = n_chunk // CMP_BLOCK
    need_steps = (n_need + CMP_BLOCK - 1) // CMP_BLOCK
    for k in range(1, n_steps + 1):
        @pl.when(need_steps == k)
        def _(k=k):
            _cmp_branch(k * CMP_BLOCK, q_ref, kc_ref, vc_ref, ov_ref, gate_ref, o_out, sel_out,
                        tq=tq, n_blk=n_blk, n_grp=n_grp)


def _overlap_matrix(s):
    n_chunk = s // NSA_CMP_STRIDE
    n_blk = s // NSA_SLC_LEN
    c0 = np.arange(n_chunk) * NSA_CMP_STRIDE
    c1 = c0 + NSA_CMP_LEN - 1
    b0 = np.arange(n_blk) * NSA_SLC_LEN
    ov = np.minimum(c1[None, :], b0[:, None] + NSA_SLC_LEN - 1) - np.maximum(c0[None, :], b0[:, None]) + 1
    return jnp.asarray(np.clip(ov, 0, None) / NSA_CMP_LEN, BF)


def _cmp_select(qT, k_cmp, v_cmpT, gates5, *, tq):
    b, _, s = qT.shape
    g = NSA_KV_HEADS
    r_heads = N_HEADS // g
    n_chunk = k_cmp.shape[2]
    n_blk = s // NSA_SLC_LEN
    ov = _overlap_matrix(s)
    n_grp = CMP_GROUPS
    rows = n_grp * r_heads * HEAD_DIM
    return pl.pallas_call(
        functools.partial(_cmp_select_kernel, tq=tq, n_blk=n_blk, n_grp=n_grp),
        grid=(b, g // n_grp, s // tq),
        in_specs=[
            pl.BlockSpec((1, rows, tq), lambda i, j, k: (i, j, k)),
            pl.BlockSpec((1, n_grp, n_chunk, HEAD_DIM), lambda i, j, k: (i, j, 0, 0)),
            pl.BlockSpec((1, n_grp, HEAD_DIM, n_chunk), lambda i, j, k: (i, j, 0, 0)),
            pl.BlockSpec((n_blk, n_chunk), lambda i, j, k: (0, 0)),
            pl.BlockSpec((1, 1, n_grp, r_heads, tq), lambda i, j, k: (i, 0, j, 0, k)),
        ],
        out_specs=[pl.BlockSpec((1, tq, rows), lambda i, j, k: (i, k, j)),
                   pl.BlockSpec((1, n_grp, n_blk, tq), lambda i, j, k: (i, j, 0, k))],
        out_shape=[jax.ShapeDtypeStruct((b, s, MIX_WIDTH), BF),
                   jax.ShapeDtypeStruct((b, g, n_blk, s), BF)],
        compiler_params=_params(("parallel", "parallel", "parallel")),
        name="nsa_cmp_select",
    )(qT, k_cmp, v_cmpT, ov, gates5)


def _flash_kernel(*refs, mode, n_par, n_rep, dq, kc, tq, tk, cw, window, n_blk, has_sink, has_gate,
                  fuse_out=None):
    it = iter(refs)
    q_ref, k_ref, v_ref = next(it), next(it), next(it)
    sel_ref = next(it) if n_blk else None
    sink_ref = next(it) if has_sink else None
    gate_ref = next(it) if has_gate else None
    qnext_ref = next(it) if mode == "causal" else None
    selnext_ref = next(it) if (mode == "causal" and n_blk) else None
    if fuse_out is None:
        out_ref = next(it)
    else:
        n_extra, has_next = fuse_out
        extra_refs = [next(it) for _ in range(n_extra)]
        z_ref, x_ref, w_ref = next(it), next(it), next(it)
        ng_ref = next(it) if has_next else None
        x_out = next(it)
        h_out = next(it) if has_next else None
    qs_ref, m_ref, acc_ref, mt_ref = (next(it) for _ in range(4))
    qn_ref = next(it) if mode == "causal" else None
    s_bufs = tuple(it)

    nq = n_rep * tq
    items = [(p, c) for p in range(n_par) for c in range(nq // cw)]
    n_items = len(items)
    grp = pl.program_id(1)
    q0 = pl.program_id(2) * tq
    acc_row = lax.broadcasted_iota(jnp.int32, (V_ROWS, nq), 0)
    for p in range(n_par):
        for r in range(n_rep):
            hd = p * n_rep + r
            qs_ref[p, 0:dq, r * tq:(r + 1) * tq] = q_ref[0, hd * dq:(hd + 1) * dq, :]
        if n_blk:
            qs_ref[p, dq:dq + n_blk, :] = jnp.concatenate([sel_ref[0, 0]] * n_rep, axis=1)
            if dq + n_blk < kc:
                qs_ref[p, dq + n_blk:kc, :] = jnp.zeros((kc - dq - n_blk, nq), BF)
        if has_sink:
            m_ref[p] = jnp.concatenate(
                [jnp.full((1, tq), sink_ref[(grp * n_par + p) * n_rep + r] * LOG2E, F32) for r in range(n_rep)],
                axis=1)
            acc_ref[p] = jnp.where(acc_row == HEAD_DIM, 1.0, 0.0)
        else:
            m_ref[p] = jnp.full((1, nq), M_INIT, F32)
            acc_ref[p] = jnp.zeros((V_ROWS, nq), F32)

    t_row = q0 + (lax.broadcasted_iota(jnp.int32, (1, nq), 1) & (tq - 1))
    one_row = jnp.where(lax.broadcasted_iota(jnp.int32, (V_ROWS - HEAD_DIM, tk), 0) == 0, 1.0, 0.0).astype(BF)

    def stage_a(item, j, key0, slot, kind):
        p, c = item
        cols = slice(c * cw, (c + 1) * cw)
        s = _dot(k_ref[0, p, j][:, :kc], qs_ref[p, :, cols])
        if kind is not None:
            key = key0 + lax.broadcasted_iota(jnp.int32, (tk, 1), 0)
            ok = key <= t_row[:, cols] if kind == "causal" else key > t_row[:, cols] - window
            s = jnp.where(ok, s, NEG_INF)
        s_bufs[slot][...] = s
        mt_ref[slot] = jnp.max(s, axis=0, keepdims=True)

    def stage_b(item, j, slot):
        p, c = item
        cols = slice(c * cw, (c + 1) * cw)
        m_old = m_ref[p, :, cols]
        m_new = jnp.maximum(m_old, mt_ref[slot])
        pr = jnp.exp2(s_bufs[slot][...] - m_new).astype(BF)
        alpha = jnp.exp2(m_old - m_new)
        v = jnp.concatenate([v_ref[0, j, p * HEAD_DIM:(p + 1) * HEAD_DIM, :], one_row], axis=0)
        acc_ref[p, :, cols] = alpha * acc_ref[p, :, cols] + _dot(v, pr)
        m_ref[p, :, cols] = m_new

    def finalize():
        outs = []
        for p in range(n_par):
            acc = acc_ref[p]
            o = acc[:HEAD_DIM] * (1.0 / acc[HEAD_DIM:HEAD_DIM + 1])
            if has_gate:
                o = o * _gate_row(gate_ref, n_rep, p)
            outs += [o[:, r * tq:(r + 1) * tq] for r in range(n_rep)]
        o_tok = jnp.concatenate(outs, axis=0).T
        if fuse_out is None:
            out_ref[0] = o_tok.astype(out_ref.dtype)
            return
        for r in extra_refs:
            o_tok = o_tok + r[0].astype(F32)
        y = _dot((o_tok * z_ref[0].astype(F32)).astype(BF), w_ref[...])
        x_new = x_ref[0] + y
        x_out[0] = x_new
        if has_next:
            h_out[0] = _rms_rows(x_new, ng_ref[...]).astype(BF)

    if mode == "causal":
        assert n_items % 2 == 0
        n_full = q0 // tk

        def prefetch_next_tile():
            assert cw == tq
            qn_ref[0:dq, :] = qnext_ref[0, 0:dq, :]
            if n_blk:
                qn_ref[dq:dq + n_blk, :] = selnext_ref[0, 0]
                if dq + n_blk < kc:
                    qn_ref[dq + n_blk:kc, :] = jnp.zeros((kc - dq - n_blk, tq), BF)
            s = _dot(k_ref[0, 0, 0][:, :kc], qn_ref[...])
            key = lax.broadcasted_iota(jnp.int32, (tk, 1), 0)
            s = jnp.where(key <= t_row[:, 0:cw] + tq, s, NEG_INF)
            s_bufs[0][...] = s
            mt_ref[0] = jnp.max(s, axis=0, keepdims=True)

        def step(j, kind, next_kind, last):
            for idx, item in enumerate(items):
                slot = idx % 2
                if idx + 1 < n_items:
                    stage_a(items[idx + 1], j, j * tk, 1 - slot, kind)
                elif not last:
                    stage_a(items[0], j + 1, (j + 1) * tk, 1 - slot, next_kind)
                else:
                    prefetch_next_tile()
                stage_b(item, j, slot)

        @pl.when(pl.program_id(2) == 0)
        def _():
            stage_a(items[0], 0, 0, 0, "causal")

        def body(j, carry):
            step(j, None, None, False)
            return carry

        def body_group(i, carry):
            for u in range(FLASH_UNROLL):
                step(FLASH_UNROLL * i + u, None, None, False)
            return carry

        n_main = jnp.maximum(n_full - 1, 0)
        n_groups = lax.shift_right_logical(n_main, int(np.log2(FLASH_UNROLL)))
        lax.fori_loop(0, n_groups, body_group, 0)
        lax.fori_loop(FLASH_UNROLL * n_groups, n_main, body, 0)

        @pl.when(n_full >= 1)
        def _():
            step(n_full - 1, None, "causal", False)
            step(n_full, "causal", None, True)
            finalize()

        @pl.when(n_full == 0)
        def _():
            step(0, "causal", None, True)
            finalize()
    else:
        w_tiles, q_tiles = window // tk, tq // tk
        work = []
        for i in range(w_tiles + q_tiles):
            jv = q0 // tk - w_tiles + i
            key0 = jnp.where(jv < 0, -(1 << 30), jv * tk)
            kind = "causal" if i >= w_tiles else "lower"
            work += [(item, jnp.maximum(jv, 0), key0, kind) for item in items]
        n_buf = len(s_bufs)
        ahead = WINDOW_LOOKAHEAD
        for n in range(ahead):
            stage_a(work[n][0], work[n][1], work[n][2], n % n_buf, work[n][3])
        for n, (item, j, _, _) in enumerate(work):
            if n + ahead < len(work):
                nxt = work[n + ahead]
                stage_a(nxt[0], nxt[1], nxt[2], (n + ahead) % n_buf, nxt[3])
            stage_b(item, j, n % n_buf)

        finalize()


def _flash(qT, k_tok, vT_tiles, *, mode, n_par, n_rep, dq, kc, tq, tk, cw, window=None,
           sel_bias=None, sinks=None, gates=None, gate_branch=0, out_proj=None, name="flash"):
    b, _, s = qT.shape
    kh, k_lanes = k_tok.shape[1], k_tok.shape[3]
    n_t = s // tk
    k5 = k_tok.reshape(b, kh, n_t, tk, k_lanes)
    n_grp = kh // n_par
    heads = n_par * n_rep
    nq = n_rep * tq
    n_blk = sel_bias.shape[2] if sel_bias is not None else 0
    has_sink, has_gate = sinks is not None, gates is not None
    args = [qT, k5, vT_tiles]
    in_specs = [
        pl.BlockSpec((1, heads * dq, tq), lambda i, j, k: (i, j, k)),
        pl.BlockSpec((1, n_par, n_t, tk, k_lanes), lambda i, j, k: (i, j, 0, 0, 0)),
        pl.BlockSpec((1, n_t, n_par * HEAD_DIM, tk), lambda i, j, k: (i, 0, j, 0)),
    ]
    if n_blk:
        assert dq + n_blk <= kc
        args.append(sel_bias)
        in_specs.append(pl.BlockSpec((1, 1, n_blk, tq), lambda i, j, k: (i, j, 0, k)))
    if has_sink:
        args.append(sinks.astype(F32))
        in_specs.append(pl.BlockSpec(memory_space=pltpu.SMEM))
    if has_gate:
        args.append(gates)
        in_specs.append(pl.BlockSpec((1, 1, n_par, n_rep, tq),
                                     lambda i, j, k, _br=gate_branch: (i, _br, j, 0, k)))
    n_sbuf = 2 if mode == "causal" else WINDOW_SCORE_BUFS
    scratch = [pltpu.VMEM((n_par, kc, nq), BF), pltpu.VMEM((n_par, 1, nq), F32),
               pltpu.VMEM((n_par, V_ROWS, nq), F32), pltpu.VMEM((n_sbuf, 1, cw), F32)]
    if mode == "causal":
        last_q = s // tq - 1
        args.append(qT)
        in_specs.append(pl.BlockSpec((1, dq, tq), lambda i, j, k: (i, j * heads, jnp.minimum(k + 1, last_q))))
        if n_blk:
            args.append(sel_bias)
            in_specs.append(pl.BlockSpec((1, 1, n_blk, tq), lambda i, j, k: (i, j, 0, jnp.minimum(k + 1, last_q))))
        scratch.append(pltpu.VMEM((kc, tq), BF))
    scratch += [pltpu.VMEM((tk, cw), F32)] * n_sbuf
    out_specs = pl.BlockSpec((1, tq, heads * HEAD_DIM), lambda i, j, k: (i, k, j))
    out_shape = jax.ShapeDtypeStruct((b, s, MIX_WIDTH), BF)
    fuse_out = None
    if out_proj is not None:
        assert mode == "window" and heads * HEAD_DIM == MIX_WIDTH and n_grp == 1
        extra, zs, x, w_out, next_gain = out_proj
        tok = pl.BlockSpec((1, tq, D_MODEL), lambda i, j, k: (i, k, 0))
        args += list(extra) + [zs, x, w_out.astype(BF)]
        in_specs += [tok] * (len(extra) + 2) + [pl.BlockSpec(w_out.shape, lambda i, j, k: (0, 0))]
        out_specs, out_shape = [tok], [jax.ShapeDtypeStruct(x.shape, F32)]
        if next_gain is not None:
            args.append(next_gain.reshape(1, D_MODEL))
            in_specs.append(pl.BlockSpec((1, D_MODEL), lambda i, j, k: (0, 0)))
            out_specs.append(tok)
            out_shape.append(jax.ShapeDtypeStruct(x.shape, BF))
        fuse_out = (len(extra), next_gain is not None)
    kern = functools.partial(_flash_kernel, mode=mode, n_par=n_par, n_rep=n_rep, dq=dq, kc=kc, tq=tq, tk=tk,
                             cw=cw, window=window, n_blk=n_blk, has_sink=has_sink, has_gate=has_gate,
                             fuse_out=fuse_out)
    res = pl.pallas_call(
        kern,
        grid=(b, n_grp, s // tq),
        in_specs=in_specs,
        out_specs=out_specs,
        out_shape=out_shape,
        scratch_shapes=scratch,
        compiler_params=_params(("parallel", "parallel", "arbitrary")),
        name=name,
    )(*args)
    if out_proj is None:
        return res
    return (res[0], res[1]) if next_gain is not None else (res[0], None)


def _swa_proj_kernel(h_ref, cos_ref, sin_ref, wq_ref, wk_ref, wv_ref, wz_ref, qg_ref, kg_ref,
                     q_out, k_out, v_out, z_out, y_ref, *, win_tile):
    h = h_ref[0]
    cos, sin = cos_ref[0], sin_ref[0]
    y_ref[...] = _dot_nt(wq_ref[...], h)
    for hd in range(N_HEADS):
        rows = slice(hd * HEAD_DIM, (hd + 1) * HEAD_DIM)
        q_out[0, rows, :] = (_rope_t(_head_rms_t(y_ref[rows, :], qg_ref[...]), cos, sin) * Q_SCALE).astype(BF)
    kvd = SWA_KV_HEADS * HEAD_DIM
    y_ref[0:kvd, :] = _dot_nt(wk_ref[...], h)
    for gi in range(SWA_KV_HEADS):
        rows = slice(gi * HEAD_DIM, (gi + 1) * HEAD_DIM)
        k_out[0, gi] = _to_token_major(_rope_t(_head_rms_t(y_ref[rows, :], kg_ref[...]), cos, sin)).astype(BF)
    _store_lane_tiles(v_out, _dot_nt(wv_ref[...], h).astype(BF), win_tile)
    z_out[0] = _silu(_dot(h, wz_ref[...])).astype(BF)


def _swa_proj(h, cos, sin, w_in, q_gain, k_gain, *, win_tile):
    b, s, d = h.shape
    g = SWA_KV_HEADS
    kvd = g * HEAD_DIM
    tm = min(TOK_TILE, s)
    wt = w_in.T.astype(BF)
    wq, wk, wv = wt[:MIX_WIDTH], wt[MIX_WIDTH:MIX_WIDTH + kvd], wt[MIX_WIDTH + kvd:MIX_WIDTH + 2 * kvd]
    wz = w_in[:, MIX_WIDTH + 2 * kvd:].astype(BF)
    qg, kg = q_gain.reshape(HEAD_DIM, 1), k_gain.reshape(HEAD_DIM, 1)
    wpt = tm // win_tile
    half = HEAD_DIM // 2
    return pl.pallas_call(
        functools.partial(_swa_proj_kernel, win_tile=win_tile),
        grid=(b, s // tm),
        in_specs=[pl.BlockSpec((1, tm, d), lambda i, j: (i, j, 0)),
                  pl.BlockSpec((1, half, tm), lambda i, j: (i, 0, j)),
                  pl.BlockSpec((1, half, tm), lambda i, j: (i, 0, j)),
                  _full(wq.shape), _full(wk.shape), _full(wv.shape), _full(wz.shape),
                  _full(qg.shape), _full(kg.shape)],
        out_specs=[pl.BlockSpec((1, MIX_WIDTH, tm), lambda i, j: (i, 0, j)),
                   pl.BlockSpec((1, g, tm, KEY_PAD), lambda i, j: (i, 0, j, 0)),
                   pl.BlockSpec((1, wpt, kvd, win_tile), lambda i, j: (i, j, 0, 0)),
                   pl.BlockSpec((1, tm, MIX_WIDTH), lambda i, j: (i, j, 0))],
        out_shape=[jax.ShapeDtypeStruct((b, MIX_WIDTH, s), BF),
                   jax.ShapeDtypeStruct((b, g, s, KEY_PAD), BF),
                   jax.ShapeDtypeStruct((b, s // win_tile, kvd, win_tile), BF),
                   jax.ShapeDtypeStruct((b, s, MIX_WIDTH), BF)],
        scratch_shapes=[pltpu.VMEM((MIX_WIDTH, tm), F32)],
        compiler_params=_params(("parallel", "parallel")),
        name="swa_proj",
    )(h, cos, sin, wq, wk, wv, wz, qg, kg)


def _split3(x):
    hi = x.astype(BF)
    r1 = x - hi.astype(F32)
    mid = r1.astype(BF)
    lo = (r1 - mid.astype(F32)).astype(BF)
    return hi, mid, lo


def _forget_cum_kernel(h_ref, wf_ref, bias_ref, tri_ref, cum_out, carry_ref):
    @pl.when(pl.program_id(1) == 0)
    def _():
        carry_ref[...] = jnp.zeros_like(carry_ref)

    x = _dot_nt(wf_ref[...], h_ref[0]) + bias_ref[...]
    logf = jnp.minimum(x, 0.0) - jnp.log(1.0 + jnp.exp(-jnp.abs(x)))
    tri = tri_ref[...]
    hi, mid, lo = _split3(logf)
    cum = (_dot(hi, tri) + _dot(mid, tri)) + _dot(lo, tri) + carry_ref[:, 0:1]
    cum_out[0] = cum
    carry_ref[...] = jnp.broadcast_to(cum[:, -1:], carry_ref.shape)


def _forget_cum(h, wf, bias):
    b, s, d = h.shape
    tm = min(TOK_TILE, s)
    tri = jnp.asarray(np.arange(tm)[:, None] <= np.arange(tm)[None, :], BF)
    return pl.pallas_call(
        _forget_cum_kernel,
        grid=(b, s // tm),
        in_specs=[pl.BlockSpec((1, tm, d), lambda i, j: (i, j, 0)),
                  _full(wf.shape), _full((N_HEADS, 1)), _full(tri.shape)],
        out_specs=pl.BlockSpec((1, N_HEADS, tm), lambda i, j: (i, 0, j)),
        out_shape=jax.ShapeDtypeStruct((b, N_HEADS, s), F32),
        scratch_shapes=[pltpu.VMEM((N_HEADS, LANES), F32)],
        compiler_params=_params(("parallel", "arbitrary")),
        name="fox_forget_cum",
    )(h, wf, bias.reshape(N_HEADS, 1).astype(F32), tri)


def _fox_proj_kernel(h_ref, cum_ref, wq_ref, wk_ref, wv_ref, wz_ref, qg_ref, kg_ref,
                     q_out, k_out, v_out, z_out, yq_ref, yk_ref, *, tile):
    h = h_ref[0]
    tm = h.shape[0]
    yq_ref[...] = _dot_nt(wq_ref[...], h)
    yk_ref[...] = _dot_nt(wk_ref[...], h)
    row = lax.broadcasted_iota(jnp.int32, (8, tm), 0)
    zeros = jnp.zeros((KEY_PAD - HEAD_DIM - 16, tm), F32)
    for hd in range(N_HEADS):
        rows = slice(hd * HEAD_DIM, (hd + 1) * HEAD_DIM)
        c_hi, c_mid, c_lo = (c.astype(F32) for c in _split3(cum_ref[0, hd:hd + 1, :] * LOG2E))
        c3 = jnp.where(row == 0, c_hi, jnp.where(row == 1, c_mid, jnp.where(row == 2, c_lo, 0.0)))
        one3 = jnp.where(row < 3, 1.0, 0.0)
        q = _head_rms_t(yq_ref[rows, :], qg_ref[...]) * Q_SCALE
        q_out[0, hd] = jnp.concatenate([q, one3, c3, zeros], axis=0).astype(BF)
        k = _head_rms_t(yk_ref[rows, :], kg_ref[...])
        k_out[0, hd] = jnp.concatenate([k, -c3, one3, zeros], axis=0).T.astype(BF)
    _store_lane_tiles(v_out, _dot_nt(wv_ref[...], h).astype(BF), tile)
    z_out[0] = _silu(_dot(h, wz_ref[...])).astype(BF)


def _fox_proj(h, cum, w_in, q_gain, k_gain, *, tile):
    b, s, d = h.shape
    tm = min(TOK_TILE, s)
    wt = w_in.T.astype(BF)
    wq, wk, wv = wt[:MIX_WIDTH], wt[MIX_WIDTH:2 * MIX_WIDTH], wt[2 * MIX_WIDTH:3 * MIX_WIDTH]
    wz = w_in[:, 3 * MIX_WIDTH + N_HEADS:].astype(BF)
    qg, kg = q_gain.reshape(HEAD_DIM, 1), k_gain.reshape(HEAD_DIM, 1)
    return pl.pallas_call(
        functools.partial(_fox_proj_kernel, tile=tile),
        grid=(b, s // tm),
        in_specs=[pl.BlockSpec((1, tm, d), lambda i, j: (i, j, 0)),
                  pl.BlockSpec((1, N_HEADS, tm), lambda i, j: (i, 0, j)),
                  _full(wq.shape), _full(wk.shape), _full(wv.shape), _full(wz.shape),
                  _full(qg.shape), _full(kg.shape)],
        out_specs=[pl.BlockSpec((1, N_HEADS, KEY_PAD, tm), lambda i, j: (i, 0, 0, j)),
                   pl.BlockSpec((1, N_HEADS, tm, KEY_PAD), lambda i, j: (i, 0, j, 0)),
                   pl.BlockSpec((1, tm // tile, MIX_WIDTH, tile), lambda i, j: (i, j, 0, 0)),
                   pl.BlockSpec((1, tm, MIX_WIDTH), lambda i, j: (i, j, 0))],
        out_shape=[jax.ShapeDtypeStruct((b, N_HEADS, KEY_PAD, s), BF),
                   jax.ShapeDtypeStruct((b, N_HEADS, s, KEY_PAD), BF),
                   jax.ShapeDtypeStruct((b, s // tile, MIX_WIDTH, tile), BF),
                   jax.ShapeDtypeStruct((b, s, MIX_WIDTH), BF)],
        scratch_shapes=[pltpu.VMEM((MIX_WIDTH, tm), F32), pltpu.VMEM((MIX_WIDTH, tm), F32)],
        compiler_params=_params(("parallel", "parallel")),
        name="fox_proj",
    )(h, cum, wq, wk, wv, wz, qg, kg)


def _out_proj_kernel(*refs, n_o, has_next):
    o_refs = refs[:n_o]
    z_ref, x_ref, w_ref = refs[n_o:n_o + 3]
    rest = refs[n_o + 3:]
    if has_next:
        g_ref, x_out, h_out = rest
    else:
        (x_out,) = rest
    o = o_refs[0][0].astype(F32)
    for r in o_refs[1:]:
        o = o + r[0].astype(F32)
    y = _dot((o * z_ref[0].astype(F32)).astype(BF), w_ref[...])
    x_new = x_ref[0] + y
    x_out[0] = x_new
    if has_next:
        h_out[0] = _rms_rows(x_new, g_ref[...]).astype(BF)


def _out_proj(o_list, zs, x, w_out, next_gain):
    b, s, d = x.shape
    tm = min(TOK_TILE, s)
    has_next = next_gain is not None
    blk = pl.BlockSpec((1, tm, d), lambda i, j: (i, j, 0))
    args = list(o_list) + [zs, x, w_out.astype(BF)]
    in_specs = [blk] * (len(o_list) + 2) + [_full(w_out.shape)]
    out_shape = [jax.ShapeDtypeStruct((b, s, d), F32)]
    out_specs = [blk]
    if has_next:
        args.append(next_gain.reshape(1, d))
        in_specs.append(_full((1, d)))
        out_shape.append(jax.ShapeDtypeStruct((b, s, d), BF))
        out_specs.append(blk)
    res = pl.pallas_call(
        functools.partial(_out_proj_kernel, n_o=len(o_list), has_next=has_next),
        grid=(b, s // tm), in_specs=in_specs, out_specs=out_specs, out_shape=out_shape,
        compiler_params=_params(("parallel", "parallel")),
        name="out_proj",
    )(*args)
    return (res[0], res[1]) if has_next else (res[0], None)


ITEM_LANES = 512
CMP_Q_TILE = 512
CMP_GROUPS = 2
NSA_SLC_Q_TILE = 512
NSA_SLC_TILE = 512
NSA_WIN_Q_TILE = 256
NSA_WIN_TILE = 256
SWA_TILE = 128
FOX_Q_TILE = 512
FOX_K_TILE = 512
FOX_HEADS_PER_STEP = 4


def _nsa_mixer(x, h, cos, sin, w_in, q_gain, k_gain, cmp_pos, cmp_w1, cmp_w2, w_out, next_gain, pre_gain):
    b, s, _ = x.shape
    g = NSA_KV_HEADS
    r = N_HEADS // g
    qT, kc_tok, vc_tok, ks, kw, vsT, vwT, gates, zs = _nsa_proj(
        x if h is None else h, cos, sin, w_in, q_gain, k_gain, slc_tile=NSA_SLC_TILE, win_tile=NSA_WIN_TILE,
        pre_gain=pre_gain if h is None else None)
    k_cmp, v_cmpT = _compress(kc_tok, vc_tok, cmp_pos, cmp_w1, cmp_w2, k_gain[0])
    gates5 = gates.reshape(b, 3, g, r, s)
    o_cmp, sel_bias = _cmp_select(qT, k_cmp, v_cmpT, gates5, tq=CMP_Q_TILE)
    o_slc = _flash(qT, ks, vsT, mode="causal", n_par=1, n_rep=r, dq=HEAD_DIM, kc=SLC_KEY_LANES,
                   tq=NSA_SLC_Q_TILE, tk=NSA_SLC_TILE, cw=ITEM_LANES, sel_bias=sel_bias,
                   gates=gates5, gate_branch=1, name="nsa_selected")
    return _flash(qT, kw, vwT, mode="window", n_par=g, n_rep=r, dq=HEAD_DIM, kc=HEAD_DIM,
                  tq=NSA_WIN_Q_TILE, tk=NSA_WIN_TILE, cw=ITEM_LANES, window=NSA_WINDOW,
                  gates=gates5, gate_branch=2, out_proj=([o_cmp, o_slc], zs, x, w_out, next_gain),
                  name="nsa_window_out")


def _swa_mixer(x, h, cos, sin, w_in, q_gain, k_gain, sinks, w_out, next_gain):
    r = N_HEADS // SWA_KV_HEADS
    qT, k, vT, zs = _swa_proj(h, cos, sin, w_in, q_gain, k_gain, win_tile=SWA_TILE)
    return _flash(qT, k, vT, mode="window", n_par=SWA_KV_HEADS, n_rep=r, dq=HEAD_DIM, kc=HEAD_DIM, tq=SWA_TILE,
                  tk=SWA_TILE, cw=r * SWA_TILE, window=SWA_WINDOW, sinks=sinks,
                  out_proj=([], zs, x, w_out, next_gain), name="swa_window_out")


def _fox_mixer(x, h, w_in, forget_bias, q_gain, k_gain, w_out, next_gain):
    b, s, _ = h.shape
    wf = w_in[:, 3 * MIX_WIDTH:3 * MIX_WIDTH + N_HEADS].T.astype(BF)
    cum = _forget_cum(h, wf, forget_bias)
    qT, k, vT, zs = _fox_proj(h, cum, w_in, q_gain, k_gain, tile=FOX_K_TILE)
    o = _flash(qT.reshape(b, N_HEADS * KEY_PAD, s), k, vT, mode="causal", n_par=FOX_HEADS_PER_STEP, n_rep=1, dq=KEY_PAD,
               kc=KEY_PAD, tq=FOX_Q_TILE, tk=FOX_K_TILE, cw=ITEM_LANES, name="fox_attention")
    return _out_proj([o], zs, x, w_out, next_gain)


def kernel(x, positions, norm_gains, a_w_in, a_q_gain, a_k_gain, a_cmp_pos, a_cmp_w1, a_cmp_w2, a_w_out,
           b_w_in, b_q_gain, b_k_gain, b_sinks, b_w_out,
           c_w_in, c_forget_bias, c_q_gain, c_k_gain, c_w_out):
    depth = norm_gains.shape[0]
    cos, sin = _rope_tables(positions)
    h = None
    for i in range(depth):
        j, mixer = divmod(i, 3)
        next_gain = norm_gains[i + 1] if i + 1 < depth else None
        if mixer == 0:
            x, h = _nsa_mixer(x, h, cos, sin, a_w_in[j], a_q_gain[j], a_k_gain[j],
                              a_cmp_pos[j], a_cmp_w1[j], a_cmp_w2[j], a_w_out[j], next_gain, norm_gains[i])
        elif mixer == 1:
            x, h = _swa_mixer(x, h, cos, sin, b_w_in[j], b_q_gain[j], b_k_gain[j], b_sinks[j],
                              b_w_out[j], next_gain)
        else:
            x, h = _fox_mixer(x, h, c_w_in[j], c_forget_bias[j], c_q_gain[j], c_k_gain[j],
                              c_w_out[j], next_gain)
    return x
```

```python
import functools

import jax
import jax.numpy as jnp
import numpy as np
from jax import lax
from jax.experimental import pallas as pl
from jax.experimental.pallas import tpu as pltpu

D_MODEL = 1024
HEAD_DIM = 64
N_HEADS = 16
MIX_WIDTH = N_HEADS * HEAD_DIM
ROPE_THETA = 10000.0
EPS = 1e-6
SCALE = HEAD_DIM ** -0.5
NEG_INF = -1e30
BIG = 1e30
M_INIT = -1e29
SEL_OFF = -(2.0 ** 100)

NSA_KV_HEADS = 4
NSA_CMP_LEN = 32
NSA_CMP_STRIDE = 16
NSA_SLC_LEN = 64
NSA_TOPK = 16
NSA_WINDOW = 512
SWA_KV_HEADS = 2
SWA_WINDOW = 128

LOG2E = float(np.log2(np.e))
Q_SCALE = SCALE * LOG2E

LANES = 128
KEY_PAD = 128
SLC_KEY_LANES = 256
V_ROWS = 80
FLASH_UNROLL = 4
WINDOW_LOOKAHEAD = 2
WINDOW_SCORE_BUFS = 4
CMP_BLOCK = 128
CMP_MASK_ROWS = 2 * CMP_BLOCK
VMEM_LIMIT = 56 * 1024 * 1024

TOK_TILE = 512
NT_DIMS = (((1,), (1,)), ((), ()))

BF = jnp.bfloat16
F32 = jnp.float32


def _params(sem):
    return pltpu.CompilerParams(dimension_semantics=sem, vmem_limit_bytes=VMEM_LIMIT)


def _dot(a, b):
    return jnp.dot(a, b, preferred_element_type=F32)


def _dot_nt(a, b):
    return lax.dot_general(a, b, NT_DIMS, preferred_element_type=F32)


def _rope_tab_kernel(pos_ref, invf_ref, cos_ref, sin_ref):
    ang = invf_ref[...] * pos_ref[0].astype(F32)
    cos_ref[0] = jnp.cos(ang)
    sin_ref[0] = jnp.sin(ang)


def _rope_tables(positions):
    b, s = positions.shape
    half = HEAD_DIM // 2
    inv_freq = ROPE_THETA ** (-jnp.arange(half, dtype=F32) * 2.0 / HEAD_DIM)
    tm = min(TOK_TILE, s)
    out = jax.ShapeDtypeStruct((b, half, s), F32)
    return pl.pallas_call(
        _rope_tab_kernel,
        grid=(b, s // tm),
        in_specs=[pl.BlockSpec((1, 1, tm), lambda i, j: (i, 0, j)),
                  pl.BlockSpec((half, 1), lambda i, j: (0, 0))],
        out_specs=[pl.BlockSpec((1, half, tm), lambda i, j: (i, 0, j))] * 2,
        out_shape=[out, out],
        compiler_params=_params(("parallel", "parallel")),
        name="rope_tables",
    )(positions.reshape(b, 1, s), inv_freq.reshape(half, 1))


def _rms_rows(x, gain_row):
    y = x * lax.rsqrt(jnp.mean(x * x, axis=-1, keepdims=True) + EPS)
    return y * gain_row


def _head_rms_t(y, gain_col):
    ms = jnp.mean(y * y, axis=0, keepdims=True)
    return (y * lax.rsqrt(ms + EPS)) * gain_col


def _rope_t(y, cos, sin):
    half = HEAD_DIM // 2
    x1, x2 = y[:half], y[half:]
    return jnp.concatenate([x1 * cos - x2 * sin, x2 * cos + x1 * sin], axis=0)


def _to_token_major(y):
    pad = jnp.zeros((KEY_PAD - y.shape[0], y.shape[1]), y.dtype)
    return jnp.concatenate([y, pad], axis=0).T


def _silu(z):
    return z * (1.0 / (1.0 + jnp.exp(-z)))


def _sigmoid(z):
    return 1.0 / (1.0 + jnp.exp(-z))


def _store_lane_tiles(out_ref, y, tile):
    for c in range(y.shape[1] // tile):
        out_ref[0, c] = y[:, c * tile:(c + 1) * tile]


def _store_chunk_rows(out_ref, gi, yt, tok_ref):
    tok_ref[...] = yt
    n = yt.shape[0] // NSA_CMP_STRIDE
    for m in range(NSA_CMP_STRIDE // 2):
        even = tok_ref[pl.ds(2 * m, n, stride=NSA_CMP_STRIDE), :]
        odd = tok_ref[pl.ds(2 * m + 1, n, stride=NSA_CMP_STRIDE), :]
        out_ref[0, gi, :, KEY_PAD * m:KEY_PAD * (m + 1)] = even + pltpu.roll(odd, HEAD_DIM, 1)


def _nsa_proj_kernel(*refs, slc_tile, win_tile, norm_in):
    it = iter(refs)
    h_ref = next(it)
    ng_ref = next(it) if norm_in else None
    (cos_ref, sin_ref, wq_ref, wk_ref, wv_ref, wg_ref, wz_ref, qg_ref, kg_ref,
     q_out, kc_out, vc_out, ks_out, kw_out, vs_out, vw_out, g_out, z_out, y_ref, tok_ref) = it
    h = _rms_rows(h_ref[0], ng_ref[...]).astype(BF) if norm_in else h_ref[0]
    cos, sin = cos_ref[0], sin_ref[0]
    g = NSA_KV_HEADS
    kvd = g * HEAD_DIM
    y_ref[...] = _dot_nt(wq_ref[...], h)
    for hd in range(N_HEADS):
        rows = slice(hd * HEAD_DIM, (hd + 1) * HEAD_DIM)
        y = _rope_t(_head_rms_t(y_ref[rows, :], qg_ref[...]), cos, sin) * Q_SCALE
        q_out[0, rows, :] = y.astype(BF)
    y_ref[0:3 * kvd, :] = _dot_nt(wk_ref[...], h)
    tm = h.shape[0]
    tok = pl.program_id(1) * tm + lax.broadcasted_iota(jnp.int32, (tm, 1), 0)
    blk_lane = HEAD_DIM + lax.shift_right_logical(tok, int(np.log2(NSA_SLC_LEN)))
    blk_hot = lax.broadcasted_iota(jnp.int32, (1, SLC_KEY_LANES), 1) == blk_lane
    for kind in range(3):
        for gi in range(g):
            r0 = (kind * g + gi) * HEAD_DIM
            y = y_ref[r0:r0 + HEAD_DIM, :]
            if kind > 0:
                y = _head_rms_t(y, kg_ref[:, kind:kind + 1])
            yt = _to_token_major(_rope_t(y, cos, sin))
            if kind == 0:
                _store_chunk_rows(kc_out, gi, yt, tok_ref)
            elif kind == 1:
                wide = jnp.concatenate([yt, jnp.zeros((tm, SLC_KEY_LANES - KEY_PAD), F32)], axis=1)
                ks_out[0, gi] = jnp.where(blk_hot, 1.0, wide).astype(BF)
            else:
                kw_out[0, gi] = yt.astype(BF)
    y_ref[0:3 * kvd, :] = _dot_nt(wv_ref[...], h)
    for gi in range(g):
        r0 = gi * HEAD_DIM
        _store_chunk_rows(vc_out, gi, _to_token_major(y_ref[r0:r0 + HEAD_DIM, :]), tok_ref)
    _store_lane_tiles(vs_out, y_ref[kvd:2 * kvd, :].astype(BF), slc_tile)
    _store_lane_tiles(vw_out, y_ref[2 * kvd:3 * kvd, :].astype(BF), win_tile)
    g_out[0] = _sigmoid(_dot_nt(wg_ref[...], h))
    z_out[0] = _silu(_dot(h, wz_ref[...])).astype(BF)


def _full(shape):
    nd = len(shape)
    return pl.BlockSpec(shape, lambda i, j, _n=nd: (0,) * _n)


def _nsa_proj(h, cos, sin, w_in, q_gain, k_gain, *, slc_tile, win_tile, pre_gain=None):
    b, s, d = h.shape
    g = NSA_KV_HEADS
    kvd = g * HEAD_DIM
    tm = min(TOK_TILE, s)
    flat = NSA_CMP_STRIDE * HEAD_DIM
    sizes = [MIX_WIDTH] + [kvd] * 6 + [3 * N_HEADS]
    off = np.cumsum([0] + sizes)
    wt = w_in.T.astype(BF)
    wq = wt[off[0]:off[1]]
    wk = jnp.concatenate([wt[off[1]:off[2]], wt[off[3]:off[4]], wt[off[5]:off[6]]], axis=0)
    wv = jnp.concatenate([wt[off[2]:off[3]], wt[off[4]:off[5]], wt[off[6]:off[7]]], axis=0)
    wg = wt[off[7]:off[8]]
    wz = w_in[:, off[8]:].astype(BF)
    qg = q_gain.reshape(HEAD_DIM, 1)
    kg = k_gain.T
    n_t = s // tm
    out_shape = [
        jax.ShapeDtypeStruct((b, MIX_WIDTH, s), BF),
        jax.ShapeDtypeStruct((b, g, s // NSA_CMP_STRIDE, flat), F32),
        jax.ShapeDtypeStruct((b, g, s // NSA_CMP_STRIDE, flat), F32),
        jax.ShapeDtypeStruct((b, g, s, SLC_KEY_LANES), BF),
        jax.ShapeDtypeStruct((b, g, s, KEY_PAD), BF),
        jax.ShapeDtypeStruct((b, s // slc_tile, kvd, slc_tile), BF),
        jax.ShapeDtypeStruct((b, s // win_tile, kvd, win_tile), BF),
        jax.ShapeDtypeStruct((b, 3 * N_HEADS, s), F32),
        jax.ShapeDtypeStruct((b, s, MIX_WIDTH), BF),
    ]
    out_specs = [
        pl.BlockSpec((1, MIX_WIDTH, tm), lambda i, j: (i, 0, j)),
        pl.BlockSpec((1, g, tm // NSA_CMP_STRIDE, flat), lambda i, j: (i, 0, j, 0)),
        pl.BlockSpec((1, g, tm // NSA_CMP_STRIDE, flat), lambda i, j: (i, 0, j, 0)),
        pl.BlockSpec((1, g, tm, SLC_KEY_LANES), lambda i, j: (i, 0, j, 0)),
        pl.BlockSpec((1, g, tm, KEY_PAD), lambda i, j: (i, 0, j, 0)),
        pl.BlockSpec((1, tm // slc_tile, kvd, slc_tile), lambda i, j: (i, j, 0, 0)),
        pl.BlockSpec((1, tm // win_tile, kvd, win_tile), lambda i, j: (i, j, 0, 0)),
        pl.BlockSpec((1, 3 * N_HEADS, tm), lambda i, j: (i, 0, j)),
        pl.BlockSpec((1, tm, MIX_WIDTH), lambda i, j: (i, j, 0)),
    ]
    norm_in = pre_gain is not None
    args = [h] + ([pre_gain.reshape(1, d)] if norm_in else []) + [cos, sin, wq, wk, wv, wg, wz, qg, kg]
    in_specs = [pl.BlockSpec((1, tm, d), lambda i, j: (i, j, 0))] + ([_full((1, d))] if norm_in else []) + [
        pl.BlockSpec((1, HEAD_DIM // 2, tm), lambda i, j: (i, 0, j)),
        pl.BlockSpec((1, HEAD_DIM // 2, tm), lambda i, j: (i, 0, j)),
        _full(wq.shape), _full(wk.shape), _full(wv.shape), _full(wg.shape), _full(wz.shape),
        _full(qg.shape), _full(kg.shape),
    ]
    return pl.pallas_call(
        functools.partial(_nsa_proj_kernel, slc_tile=slc_tile, win_tile=win_tile, norm_in=norm_in),
        grid=(b, n_t), in_specs=in_specs, out_specs=out_specs, out_shape=out_shape,
        scratch_shapes=[pltpu.VMEM((MIX_WIDTH, tm), F32), pltpu.VMEM((tm, KEY_PAD), F32)],
        compiler_params=_params(("parallel", "parallel")),
        name="nsa_proj",
    )(*args)


def _gelu_tanh(x):
    c = np.float32(np.sqrt(2.0 / np.pi))
    return 0.5 * x * (1.0 + jnp.tanh(c * (x + 0.044715 * (x * x * x))))


def _compress_kernel(kc_ref, vc_ref, pos_ref, w1_ref, w2_ref, kg_ref, kcmp_out, vcmp_out):
    for which, (src, dst) in enumerate(((kc_ref, kcmp_out), (vc_ref, vcmp_out))):
        x = src[0, 0]
        n = x.shape[0]
        half = x.shape[1]
        xa = (x + pos_ref[which, 0:1, :]).astype(BF)
        xb = (x + pos_ref[which, 1:2, :]).astype(BF)
        ua = _dot(xa, w1_ref[which, :half, :])
        ub = _dot(xb, w1_ref[which, half:, :])
        row = lax.broadcasted_iota(jnp.int32, (n, 1), 0)
        ub_next = jnp.where(row == n - 1, 0.0, pltpu.roll(ub, n - 1, 0))
        hid = _gelu_tanh(ua + ub_next)
        y = _dot(hid.astype(BF), w2_ref[which])
        if which == 0:
            y = _rms_rows(y, kg_ref[...])
            dst[0, 0] = y.astype(BF)
        else:
            pad = jnp.zeros((n, KEY_PAD - HEAD_DIM), F32)
            dst[0, 0] = jnp.concatenate([y, pad], axis=1).T[:HEAD_DIM].astype(BF)


def _compress(kc, vc, cmp_pos, cmp_w1, cmp_w2, k_gain0):
    b, g, n_chunk, flat = kc.shape
    pos = cmp_pos.reshape(2, 2, flat)
    w1 = cmp_w1.astype(BF)
    w2 = cmp_w2.astype(BF)
    blk = pl.BlockSpec((1, 1, n_chunk, flat), lambda i, j: (i, j, 0, 0))
    return pl.pallas_call(
        _compress_kernel,
        grid=(b, g),
        in_specs=[blk, blk, _full(pos.shape), _full(w1.shape), _full(w2.shape),
                  _full((1, HEAD_DIM))],
        out_specs=[pl.BlockSpec((1, 1, n_chunk, HEAD_DIM), lambda i, j: (i, j, 0, 0)),
                   pl.BlockSpec((1, 1, HEAD_DIM, n_chunk), lambda i, j: (i, j, 0, 0))],
        out_shape=[jax.ShapeDtypeStruct((b, g, n_chunk, HEAD_DIM), BF),
                   jax.ShapeDtypeStruct((b, g, HEAD_DIM, n_chunk), BF)],
        compiler_params=_params(("parallel", "parallel")),
        name="nsa_compress",
    )(kc, vc, pos, w1, w2, k_gain0.reshape(1, HEAD_DIM))


def _gate_row(gate_ref, n_heads, grp=0):
    return jnp.concatenate([gate_ref[0, 0, grp, r:r + 1, :] for r in range(n_heads)], axis=1)


def _cmp_branch(rows, q_ref, kc_ref, vc_ref, ov_ref, gate_ref, o_out, sel_out, *, tq, n_blk, n_grp):
    r_heads = N_HEADS // NSA_KV_HEADS
    nq = r_heads * tq
    width = r_heads * HEAD_DIM
    q0 = pl.program_id(2) * tq
    n_cmp = kc_ref.shape[2] - 1
    t_row = q0 + (lax.broadcasted_iota(jnp.int32, (1, nq), 1) & (tq - 1))
    lo = max(rows - CMP_MASK_ROWS, 0)
    c_col = lo + lax.broadcasted_iota(jnp.int32, (rows - lo, 1), 0)
    valid = (c_col * NSA_CMP_STRIDE + (NSA_CMP_LEN - 1) <= t_row) & (c_col < n_cmp)
    one_row = jnp.where(lax.broadcasted_iota(jnp.int32, (V_ROWS - HEAD_DIM, rows), 0) == 0, 1.0, 0.0).astype(BF)
    n_live = min(n_blk, rows * NSA_CMP_STRIDE // NSA_SLC_LEN + 8)
    t1 = q0 + lax.broadcasted_iota(jnp.int32, (1, tq), 1)
    cur = lax.shift_right_logical(t1, int(np.log2(NSA_SLC_LEN)))
    blk = lax.broadcasted_iota(jnp.int32, (n_live, tq), 0)
    forced = (blk == 0) | (blk == cur) | (blk == cur - 1)
    imps = []
    for gi in range(n_grp):
        q4 = jnp.concatenate([q_ref[0, (gi * r_heads + r) * HEAD_DIM:(gi * r_heads + r + 1) * HEAD_DIM, :]
                              for r in range(r_heads)], axis=1)
        s = _dot(kc_ref[0, gi, 0:rows, :], q4)
        s_new = jnp.where(valid, s[lo:], NEG_INF)
        m = jnp.max(s_new, axis=0, keepdims=True)
        if lo:
            m = jnp.maximum(m, jnp.max(s[:lo], axis=0, keepdims=True))
        e = jnp.where(valid, jnp.exp2(s_new - m), 0.0).astype(BF)
        if lo:
            e = jnp.concatenate([jnp.exp2(s[:lo] - m).astype(BF), e], axis=0)
        lhs = jnp.concatenate([vc_ref[0, gi, :, 0:rows], one_row, ov_ref[0:n_live, 0:rows]], axis=0)
        res = _dot(lhs, e)
        l = res[HEAD_DIM:HEAD_DIM + 1]
        inv = jnp.where(l > 0.0, 1.0 / jnp.where(l > 0.0, l, 1.0), 0.0)
        o = res[:HEAD_DIM] * (inv * _gate_row(gate_ref, r_heads, gi))
        o_rows = jnp.concatenate([o[:, r * tq:(r + 1) * tq] for r in range(r_heads)], axis=0)
        o_out[0, :, gi * width:(gi + 1) * width] = o_rows.T.astype(o_out.dtype)
        w = res[V_ROWS:V_ROWS + n_live] * inv
        imp = w[:, 0:tq]
        for r in range(1, r_heads):
            imp = imp + w[:, r * tq:(r + 1) * tq]
        imps.append(jnp.where(forced, BIG, jnp.where(blk > cur, NEG_INF, imp)))
    for _ in range(min(NSA_TOPK, n_blk)):
        for gi in range(n_grp):
            best = jnp.max(imps[gi], axis=0, keepdims=True)
            first = jnp.min(jnp.where(imps[gi] == best, blk, n_blk), axis=0, keepdims=True)
            imps[gi] = jnp.where(blk == first, -jnp.inf, imps[gi])
    for gi in range(n_grp):
        sel_out[0, gi, 0:n_live, :] = jnp.where(imps[gi] == -jnp.inf, 0.0, SEL_OFF).astype(BF)
        if n_live < n_blk:
            sel_out[0, gi, n_live:n_blk, :] = jnp.full((n_blk - n_live, tq), SEL_OFF, BF)


def _cmp_select_kernel(q_ref, kc_ref, vc_ref, ov_ref, gate_ref, o_out, sel_out, *, tq, n_blk, n_grp):
    q0 = pl.program_id(2) * tq
    n_chunk = kc_ref.shape[2]
    n_need = jnp.minimum((q0 + tq - NSA_CMP_LEN) // NSA_CMP_STRIDE + 1, n_chunk - 1)
    n_steps = n_chunk // CMP_BLOCK
    need_steps = (n_need + CMP_BLOCK - 1) // CMP_BLOCK
    for k in range(1, n_steps + 1):
        @pl.when(need_steps == k)
        def _(k=k):
            _cmp_branch(k * CMP_BLOCK, q_ref, kc_ref, vc_ref, ov_ref, gate_ref, o_out, sel_out,
                        tq=tq, n_blk=n_blk, n_grp=n_grp)


def _overlap_matrix(s):
    n_chunk = s // NSA_CMP_STRIDE
    n_blk = s // NSA_SLC_LEN
    c0 = np.arange(n_chunk) * NSA_CMP_STRIDE
    c1 = c0 + NSA_CMP_LEN - 1
    b0 = np.arange(n_blk) * NSA_SLC_LEN
    ov = np.minimum(c1[None, :], b0[:, None] + NSA_SLC_LEN - 1) - np.maximum(c0[None, :], b0[:, None]) + 1
    return jnp.asarray(np.clip(ov, 0, None) / NSA_CMP_LEN, BF)


def _cmp_select(qT, k_cmp, v_cmpT, gates5, *, tq):
    b, _, s = qT.shape
    g = NSA_KV_HEADS
    r_heads = N_HEADS // g
    n_chunk = k_cmp.shape[2]
    n_blk = s // NSA_SLC_LEN
    ov = _overlap_matrix(s)
    n_grp = CMP_GROUPS
    rows = n_grp * r_heads * HEAD_DIM
    return pl.pallas_call(
        functools.partial(_cmp_select_kernel, tq=tq, n_blk=n_blk, n_grp=n_grp),
        grid=(b, g // n_grp, s // tq),
        in_specs=[
            pl.BlockSpec((1, rows, tq), lambda i, j, k: (i, j, k)),
            pl.BlockSpec((1, n_grp, n_chunk, HEAD_DIM), lambda i, j, k: (i, j, 0, 0)),
            pl.BlockSpec((1, n_grp, HEAD_DIM, n_chunk), lambda i, j, k: (i, j, 0, 0)),
            pl.BlockSpec((n_blk, n_chunk), lambda i, j, k: (0, 0)),
            pl.BlockSpec((1, 1, n_grp, r_heads, tq), lambda i, j, k: (i, 0, j, 0, k)),
        ],
        out_specs=[pl.BlockSpec((1, tq, rows), lambda i, j, k: (i, k, j)),
                   pl.BlockSpec((1, n_grp, n_blk, tq), lambda i, j, k: (i, j, 0, k))],
        out_shape=[jax.ShapeDtypeStruct((b, s, MIX_WIDTH), BF),
                   jax.ShapeDtypeStruct((b, g, n_blk, s), BF)],
        compiler_params=_params(("parallel", "parallel", "parallel")),
        name="nsa_cmp_select",
    )(qT, k_cmp, v_cmpT, ov, gates5)


def _flash_kernel(*refs, mode, n_par, n_rep, dq, kc, tq, tk, cw, window, n_blk, has_sink, has_gate,
                  fuse_out=None):
    it = iter(refs)
    q_ref, k_ref, v_ref = next(it), next(it), next(it)
    sel_ref = next(it) if n_blk else None
    sink_ref = next(it) if has_sink else None
    gate_ref = next(it) if has_gate else None
    qnext_ref = next(it) if mode == "causal" else None
    selnext_ref = next(it) if (mode == "causal" and n_blk) else None
    if fuse_out is None:
        out_ref = next(it)
    else:
        n_extra, has_next = fuse_out
        extra_refs = [next(it) for _ in range(n_extra)]
        z_ref, x_ref, w_ref = next(it), next(it), next(it)
        ng_ref = next(it) if has_next else None
        x_out = next(it)
        h_out = next(it) if has_next else None
    qs_ref, m_ref, acc_ref, mt_ref = (next(it) for _ in range(4))
    qn_ref = next(it) if mode == "causal" else None
    s_bufs = tuple(it)

    nq = n_rep * tq
    items = [(p, c) for p in range(n_par) for c in range(nq // cw)]
    n_items = len(items)
    grp = pl.program_id(1)
    q0 = pl.program_id(2) * tq
    acc_row = lax.broadcasted_iota(jnp.int32, (V_ROWS, nq), 0)
    for p in range(n_par):
        for r in range(n_rep):
            hd = p * n_rep + r
            qs_ref[p, 0:dq, r * tq:(r + 1) * tq] = q_ref[0, hd * dq:(hd + 1) * dq, :]
        if n_blk:
            qs_ref[p, dq:dq + n_blk, :] = jnp.concatenate([sel_ref[0, 0]] * n_rep, axis=1)
            if dq + n_blk < kc:
                qs_ref[p, dq + n_blk:kc, :] = jnp.zeros((kc - dq - n_blk, nq), BF)
        if has_sink:
            m_ref[p] = jnp.concatenate(
                [jnp.full((1, tq), sink_ref[(grp * n_par + p) * n_rep + r] * LOG2E, F32) for r in range(n_rep)],
                axis=1)
            acc_ref[p] = jnp.where(acc_row == HEAD_DIM, 1.0, 0.0)
        else:
            m_ref[p] = jnp.full((1, nq), M_INIT, F32)
            acc_ref[p] = jnp.zeros((V_ROWS, nq), F32)

    t_row = q0 + (lax.broadcasted_iota(jnp.int32, (1, nq), 1) & (tq - 1))
    one_row = jnp.where(lax.broadcasted_iota(jnp.int32, (V_ROWS - HEAD_DIM, tk), 0) == 0, 1.0, 0.0).astype(BF)

    def stage_a(item, j, key0, slot, kind):
        p, c = item
        cols = slice(c * cw, (c + 1) * cw)
        s = _dot(k_ref[0, p, j][:, :kc], qs_ref[p, :, cols])
        if kind is not None:
            key = key0 + lax.broadcasted_iota(jnp.int32, (tk, 1), 0)
            ok = key <= t_row[:, cols] if kind == "causal" else key > t_row[:, cols] - window
            s = jnp.where(ok, s, NEG_INF)
        s_bufs[slot][...] = s
        mt_ref[slot] = jnp.max(s, axis=0, keepdims=True)

    def stage_b(item, j, slot):
        p, c = item
        cols = slice(c * cw, (c + 1) * cw)
        m_old = m_ref[p, :, cols]
        m_new = jnp.maximum(m_old, mt_ref[slot])
        pr = jnp.exp2(s_bufs[slot][...] - m_new).astype(BF)
        alpha = jnp.exp2(m_old - m_new)
        v = jnp.concatenate([v_ref[0, j, p * HEAD_DIM:(p + 1) * HEAD_DIM, :], one_row], axis=0)
        acc_ref[p, :, cols] = alpha * acc_ref[p, :, cols] + _dot(v, pr)
        m_ref[p, :, cols] = m_new

    def finalize():
        outs = []
        for p in range(n_par):
            acc = acc_ref[p]
            o = acc[:HEAD_DIM] * (1.0 / acc[HEAD_DIM:HEAD_DIM + 1])
            if has_gate:
                o = o * _gate_row(gate_ref, n_rep, p)
            outs += [o[:, r * tq:(r + 1) * tq] for r in range(n_rep)]
        o_tok = jnp.concatenate(outs, axis=0).T
        if fuse_out is None:
            out_ref[0] = o_tok.astype(out_ref.dtype)
            return
        for r in extra_refs:
            o_tok = o_tok + r[0].astype(F32)
        y = _dot((o_tok * z_ref[0].astype(F32)).astype(BF), w_ref[...])
        x_new = x_ref[0] + y
        x_out[0] = x_new
        if has_next:
            h_out[0] = _rms_rows(x_new, ng_ref[...]).astype(BF)

    if mode == "causal":
        assert n_items % 2 == 0
        n_full = q0 // tk

        def prefetch_next_tile():
            assert cw == tq
            qn_ref[0:dq, :] = qnext_ref[0, 0:dq, :]
            if n_blk:
                qn_ref[dq:dq + n_blk, :] = selnext_ref[0, 0]
                if dq + n_blk < kc:
                    qn_ref[dq + n_blk:kc, :] = jnp.zeros((kc - dq - n_blk, tq), BF)
            s = _dot(k_ref[0, 0, 0][:, :kc], qn_ref[...])
            key = lax.broadcasted_iota(jnp.int32, (tk, 1), 0)
            s = jnp.where(key <= t_row[:, 0:cw] + tq, s, NEG_INF)
            s_bufs[0][...] = s
            mt_ref[0] = jnp.max(s, axis=0, keepdims=True)

        def step(j, kind, next_kind, last):
            for idx, item in enumerate(items):
                slot = idx % 2
                if idx + 1 < n_items:
                    stage_a(items[idx + 1], j, j * tk, 1 - slot, kind)
                elif not last:
                    stage_a(items[0], j + 1, (j + 1) * tk, 1 - slot, next_kind)
                else:
                    prefetch_next_tile()
                stage_b(item, j, slot)

        @pl.when(pl.program_id(2) == 0)
        def _():
            stage_a(items[0], 0, 0, 0, "causal")

        def body(j, carry):
            step(j, None, None, False)
            return carry

        def body_group(i, carry):
            for u in range(FLASH_UNROLL):
                step(FLASH_UNROLL * i + u, None, None, False)
            return carry

        n_main = jnp.maximum(n_full - 1, 0)
        n_groups = lax.shift_right_logical(n_main, int(np.log2(FLASH_UNROLL)))
        lax.fori_loop(0, n_groups, body_group, 0)
        lax.fori_loop(FLASH_UNROLL * n_groups, n_main, body, 0)

        @pl.when(n_full >= 1)
        def _():
            step(n_full - 1, None, "causal", False)
            step(n_full, "causal", None, True)
            finalize()

        @pl.when(n_full == 0)
        def _():
            step(0, "causal", None, True)
            finalize()
    else:
        w_tiles, q_tiles = window // tk, tq // tk
        work = []
        for i in range(w_tiles + q_tiles):
            jv = q0 // tk - w_tiles + i
            key0 = jnp.where(jv < 0, -(1 << 30), jv * tk)
            kind = "causal" if i >= w_tiles else "lower"
            work += [(item, jnp.maximum(jv, 0), key0, kind) for item in items]
        n_buf = len(s_bufs)
        ahead = WINDOW_LOOKAHEAD
        for n in range(ahead):
            stage_a(work[n][0], work[n][1], work[n][2], n % n_buf, work[n][3])
        for n, (item, j, _, _) in enumerate(work):
            if n + ahead < len(work):
                nxt = work[n + ahead]
                stage_a(nxt[0], nxt[1], nxt[2], (n + ahead) % n_buf, nxt[3])
            stage_b(item, j, n % n_buf)

        finalize()


def _flash(qT, k_tok, vT_tiles, *, mode, n_par, n_rep, dq, kc, tq, tk, cw, window=None,
           sel_bias=None, sinks=None, gates=None, gate_branch=0, out_proj=None, name="flash"):
    b, _, s = qT.shape
    kh, k_lanes = k_tok.shape[1], k_tok.shape[3]
    n_t = s // tk
    k5 = k_tok.reshape(b, kh, n_t, tk, k_lanes)
    n_grp = kh // n_par
    heads = n_par * n_rep
    nq = n_rep * tq
    n_blk = sel_bias.shape[2] if sel_bias is not None else 0
    has_sink, has_gate = sinks is not None, gates is not None
    args = [qT, k5, vT_tiles]
    in_specs = [
        pl.BlockSpec((1, heads * dq, tq), lambda i, j, k: (i, j, k)),
        pl.BlockSpec((1, n_par, n_t, tk, k_lanes), lambda i, j, k: (i, j, 0, 0, 0)),
        pl.BlockSpec((1, n_t, n_par * HEAD_DIM, tk), lambda i, j, k: (i, 0, j, 0)),
    ]
    if n_blk:
        assert dq + n_blk <= kc
        args.append(sel_bias)
        in_specs.append(pl.BlockSpec((1, 1, n_blk, tq), lambda i, j, k: (i, j, 0, k)))
    if has_sink:
        args.append(sinks.astype(F32))
        in_specs.append(pl.BlockSpec(memory_space=pltpu.SMEM))
    if has_gate:
        args.append(gates)
        in_specs.append(pl.BlockSpec((1, 1, n_par, n_rep, tq),
                                     lambda i, j, k, _br=gate_branch: (i, _br, j, 0, k)))
    n_sbuf = 2 if mode == "causal" else WINDOW_SCORE_BUFS
    scratch = [pltpu.VMEM((n_par, kc, nq), BF), pltpu.VMEM((n_par, 1, nq), F32),
               pltpu.VMEM((n_par, V_ROWS, nq), F32), pltpu.VMEM((n_sbuf, 1, cw), F32)]
    if mode == "causal":
        last_q = s // tq - 1
        args.append(qT)
        in_specs.append(pl.BlockSpec((1, dq, tq), lambda i, j, k: (i, j * heads, jnp.minimum(k + 1, last_q))))
        if n_blk:
            args.append(sel_bias)
            in_specs.append(pl.BlockSpec((1, 1, n_blk, tq), lambda i, j, k: (i, j, 0, jnp.minimum(k + 1, last_q))))
        scratch.append(pltpu.VMEM((kc, tq), BF))
    scratch += [pltpu.VMEM((tk, cw), F32)] * n_sbuf
    out_specs = pl.BlockSpec((1, tq, heads * HEAD_DIM), lambda i, j, k: (i, k, j))
    out_shape = jax.ShapeDtypeStruct((b, s, MIX_WIDTH), BF)
    fuse_out = None
    if out_proj is not None:
        assert mode == "window" and heads * HEAD_DIM == MIX_WIDTH and n_grp == 1
        extra, zs, x, w_out, next_gain = out_proj
        tok = pl.BlockSpec((1, tq, D_MODEL), lambda i, j, k: (i, k, 0))
        args += list(extra) + [zs, x, w_out.astype(BF)]
        in_specs += [tok] * (len(extra) + 2) + [pl.BlockSpec(w_out.shape, lambda i, j, k: (0, 0))]
        out_specs, out_shape = [tok], [jax.ShapeDtypeStruct(x.shape, F32)]
        if next_gain is not None:
            args.append(next_gain.reshape(1, D_MODEL))
            in_specs.append(pl.BlockSpec((1, D_MODEL), lambda i, j, k: (0, 0)))
            out_specs.append(tok)
            out_shape.append(jax.ShapeDtypeStruct(x.shape, BF))
        fuse_out = (len(extra), next_gain is not None)
    kern = functools.partial(_flash_kernel, mode=mode, n_par=n_par, n_rep=n_rep, dq=dq, kc=kc, tq=tq, tk=tk,
                             cw=cw, window=window, n_blk=n_blk, has_sink=has_sink, has_gate=has_gate,
                             fuse_out=fuse_out)
    res = pl.pallas_call(
        kern,
        grid=(b, n_grp, s // tq),
        in_specs=in_specs,
        out_specs=out_specs,
        out_shape=out_shape,
        scratch_shapes=scratch,
        compiler_params=_params(("parallel", "parallel", "arbitrary")),
        name=name,
    )(*args)
    if out_proj is None:
        return res
    return (res[0], res[1]) if next_gain is not None else (res[0], None)


def _swa_proj_kernel(h_ref, cos_ref, sin_ref, wq_ref, wk_ref, wv_ref, wz_ref, qg_ref, kg_ref,
                     q_out, k_out, v_out, z_out, y_ref, *, win_tile):
    h = h_ref[0]
    cos, sin = cos_ref[0], sin_ref[0]
    y_ref[...] = _dot_nt(wq_ref[...], h)
    for hd in range(N_HEADS):
        rows = slice(hd * HEAD_DIM, (hd + 1) * HEAD_DIM)
        q_out[0, rows, :] = (_rope_t(_head_rms_t(y_ref[rows, :], qg_ref[...]), cos, sin) * Q_SCALE).astype(BF)
    kvd = SWA_KV_HEADS * HEAD_DIM
    y_ref[0:kvd, :] = _dot_nt(wk_ref[...], h)
    for gi in range(SWA_KV_HEADS):
        rows = slice(gi * HEAD_DIM, (gi + 1) * HEAD_DIM)
        k_out[0, gi] = _to_token_major(_rope_t(_head_rms_t(y_ref[rows, :], kg_ref[...]), cos, sin)).astype(BF)
    _store_lane_tiles(v_out, _dot_nt(wv_ref[...], h).astype(BF), win_tile)
    z_out[0] = _silu(_dot(h, wz_ref[...])).astype(BF)


def _swa_proj(h, cos, sin, w_in, q_gain, k_gain, *, win_tile):
    b, s, d = h.shape
    g = SWA_KV_HEADS
    kvd = g * HEAD_DIM
    tm = min(TOK_TILE, s)
    wt = w_in.T.astype(BF)
    wq, wk, wv = wt[:MIX_WIDTH], wt[MIX_WIDTH:MIX_WIDTH + kvd], wt[MIX_WIDTH + kvd:MIX_WIDTH + 2 * kvd]
    wz = w_in[:, MIX_WIDTH + 2 * kvd:].astype(BF)
    qg, kg = q_gain.reshape(HEAD_DIM, 1), k_gain.reshape(HEAD_DIM, 1)
    wpt = tm // win_tile
    half = HEAD_DIM // 2
    return pl.pallas_call(
        functools.partial(_swa_proj_kernel, win_tile=win_tile),
        grid=(b, s // tm),
        in_specs=[pl.BlockSpec((1, tm, d), lambda i, j: (i, j, 0)),
                  pl.BlockSpec((1, half, tm), lambda i, j: (i, 0, j)),
                  pl.BlockSpec((1, half, tm), lambda i, j: (i, 0, j)),
                  _full(wq.shape), _full(wk.shape), _full(wv.shape), _full(wz.shape),
                  _full(qg.shape), _full(kg.shape)],
        out_specs=[pl.BlockSpec((1, MIX_WIDTH, tm), lambda i, j: (i, 0, j)),
                   pl.BlockSpec((1, g, tm, KEY_PAD), lambda i, j: (i, 0, j, 0)),
                   pl.BlockSpec((1, wpt, kvd, win_tile), lambda i, j: (i, j, 0, 0)),
                   pl.BlockSpec((1, tm, MIX_WIDTH), lambda i, j: (i, j, 0))],
        out_shape=[jax.ShapeDtypeStruct((b, MIX_WIDTH, s), BF),
                   jax.ShapeDtypeStruct((b, g, s, KEY_PAD), BF),
                   jax.ShapeDtypeStruct((b, s // win_tile, kvd, win_tile), BF),
                   jax.ShapeDtypeStruct((b, s, MIX_WIDTH), BF)],
        scratch_shapes=[pltpu.VMEM((MIX_WIDTH, tm), F32)],
        compiler_params=_params(("parallel", "parallel")),
        name="swa_proj",
    )(h, cos, sin, wq, wk, wv, wz, qg, kg)


def _split3(x):
    hi = x.astype(BF)
    r1 = x - hi.astype(F32)
    mid = r1.astype(BF)
    lo = (r1 - mid.astype(F32)).astype(BF)
    return hi, mid, lo


def _fox_proj_kernel(h_ref, wf_ref, bias_ref, tri_ref, wq_ref, wk_ref, wv_ref, wz_ref, qg_ref, kg_ref,
                     q_out, k_out, v_out, z_out, yq_ref, yk_ref, cum_ref, carry_ref, *, tile):
    @pl.when(pl.program_id(1) == 0)
    def _():
        carry_ref[...] = jnp.zeros_like(carry_ref)

    h = h_ref[0]
    tm = h.shape[0]
    x = _dot_nt(wf_ref[...], h) + bias_ref[...]
    logf = jnp.minimum(x, 0.0) - jnp.log(1.0 + jnp.exp(-jnp.abs(x)))
    tri = tri_ref[...]
    hi, mid, lo = _split3(logf)
    cum = (_dot(hi, tri) + _dot(mid, tri)) + _dot(lo, tri) + carry_ref[:, 0:1]
    carry_ref[...] = jnp.broadcast_to(cum[:, -1:], carry_ref.shape)
    cum_ref[...] = cum * LOG2E

    yq_ref[...] = _dot_nt(wq_ref[...], h)
    yk_ref[...] = _dot_nt(wk_ref[...], h)
    row = lax.broadcasted_iota(jnp.int32, (8, tm), 0)
    zeros = jnp.zeros((KEY_PAD - HEAD_DIM - 16, tm), F32)
    for hd in range(N_HEADS):
        rows = slice(hd * HEAD_DIM, (hd + 1) * HEAD_DIM)
        c_hi, c_mid, c_lo = (c.astype(F32) for c in _split3(cum_ref[hd:hd + 1, :]))
        c3 = jnp.where(row == 0, c_hi, jnp.where(row == 1, c_mid, jnp.where(row == 2, c_lo, 0.0)))
        one3 = jnp.where(row < 3, 1.0, 0.0)
        q = _head_rms_t(yq_ref[rows, :], qg_ref[...]) * Q_SCALE
        q_out[0, hd] = jnp.concatenate([q, one3, c3, zeros], axis=0).astype(BF)
        k = _head_rms_t(yk_ref[rows, :], kg_ref[...])
        k_out[0, hd] = jnp.concatenate([k, -c3, one3, zeros], axis=0).T.astype(BF)
    _store_lane_tiles(v_out, _dot_nt(wv_ref[...], h).astype(BF), tile)
    z_out[0] = _silu(_dot(h, wz_ref[...])).astype(BF)


def _fox_proj(h, w_in, forget_bias, q_gain, k_gain, *, tile):
    b, s, d = h.shape
    tm = min(TOK_TILE, s)
    wt = w_in.T.astype(BF)
    wq, wk, wv = wt[:MIX_WIDTH], wt[MIX_WIDTH:2 * MIX_WIDTH], wt[2 * MIX_WIDTH:3 * MIX_WIDTH]
    wf = wt[3 * MIX_WIDTH:3 * MIX_WIDTH + N_HEADS]
    wz = w_in[:, 3 * MIX_WIDTH + N_HEADS:].astype(BF)
    qg, kg = q_gain.reshape(HEAD_DIM, 1), k_gain.reshape(HEAD_DIM, 1)
    tri = jnp.asarray(np.arange(tm)[:, None] <= np.arange(tm)[None, :], BF)
    return pl.pallas_call(
        functools.partial(_fox_proj_kernel, tile=tile),
        grid=(b, s // tm),
        in_specs=[pl.BlockSpec((1, tm, d), lambda i, j: (i, j, 0)),
                  _full(wf.shape), _full((N_HEADS, 1)), _full(tri.shape),
                  _full(wq.shape), _full(wk.shape), _full(wv.shape), _full(wz.shape),
                  _full(qg.shape), _full(kg.shape)],
        out_specs=[pl.BlockSpec((1, N_HEADS, KEY_PAD, tm), lambda i, j: (i, 0, 0, j)),
                   pl.BlockSpec((1, N_HEADS, tm, KEY_PAD), lambda i, j: (i, 0, j, 0)),
                   pl.BlockSpec((1, tm // tile, MIX_WIDTH, tile), lambda i, j: (i, j, 0, 0)),
                   pl.BlockSpec((1, tm, MIX_WIDTH), lambda i, j: (i, j, 0))],
        out_shape=[jax.ShapeDtypeStruct((b, N_HEADS, KEY_PAD, s), BF),
                   jax.ShapeDtypeStruct((b, N_HEADS, s, KEY_PAD), BF),
                   jax.ShapeDtypeStruct((b, s // tile, MIX_WIDTH, tile), BF),
                   jax.ShapeDtypeStruct((b, s, MIX_WIDTH), BF)],
        scratch_shapes=[pltpu.VMEM((MIX_WIDTH, tm), F32), pltpu.VMEM((MIX_WIDTH, tm), F32),
                        pltpu.VMEM((N_HEADS, tm), F32), pltpu.VMEM((N_HEADS, LANES), F32)],
        compiler_params=_params(("parallel", "arbitrary")),
        name="fox_proj",
    )(h, wf, forget_bias.reshape(N_HEADS, 1).astype(F32), tri, wq, wk, wv, wz, qg, kg)


def _out_proj_kernel(*refs, n_o, has_next):
    o_refs = refs[:n_o]
    z_ref, x_ref, w_ref = refs[n_o:n_o + 3]
    rest = refs[n_o + 3:]
    if has_next:
        g_ref, x_out, h_out = rest
    else:
        (x_out,) = rest
    o = o_refs[0][0].astype(F32)
    for r in o_refs[1:]:
        o = o + r[0].astype(F32)
    y = _dot((o * z_ref[0].astype(F32)).astype(BF), w_ref[...])
    x_new = x_ref[0] + y
    x_out[0] = x_new
    if has_next:
        h_out[0] = _rms_rows(x_new, g_ref[...]).astype(BF)


def _out_proj(o_list, zs, x, w_out, next_gain):
    b, s, d = x.shape
    tm = min(TOK_TILE, s)
    has_next = next_gain is not None
    blk = pl.BlockSpec((1, tm, d), lambda i, j: (i, j, 0))
    args = list(o_list) + [zs, x, w_out.astype(BF)]
    in_specs = [blk] * (len(o_list) + 2) + [_full(w_out.shape)]
    out_shape = [jax.ShapeDtypeStruct((b, s, d), F32)]
    out_specs = [blk]
    if has_next:
        args.append(next_gain.reshape(1, d))
        in_specs.append(_full((1, d)))
        out_shape.append(jax.ShapeDtypeStruct((b, s, d), BF))
        out_specs.append(blk)
    res = pl.pallas_call(
        functools.partial(_out_proj_kernel, n_o=len(o_list), has_next=has_next),
        grid=(b, s // tm), in_specs=in_specs, out_specs=out_specs, out_shape=out_shape,
        compiler_params=_params(("parallel", "parallel")),
        name="out_proj",
    )(*args)
    return (res[0], res[1]) if has_next else (res[0], None)


ITEM_LANES = 512
CMP_Q_TILE = 512
CMP_GROUPS = 2
NSA_SLC_Q_TILE = 512
NSA_SLC_TILE = 512
NSA_WIN_Q_TILE = 256
NSA_WIN_TILE = 256
SWA_TILE = 128
FOX_Q_TILE = 512
FOX_K_TILE = 512
FOX_HEADS_PER_STEP = 4


def _nsa_mixer(x, h, cos, sin, w_in, q_gain, k_gain, cmp_pos, cmp_w1, cmp_w2, w_out, next_gain, pre_gain):
    b, s, _ = x.shape
    g = NSA_KV_HEADS
    r = N_HEADS // g
    qT, kc_tok, vc_tok, ks, kw, vsT, vwT, gates, zs = _nsa_proj(
        x if h is None else h, cos, sin, w_in, q_gain, k_gain, slc_tile=NSA_SLC_TILE, win_tile=NSA_WIN_TILE,
        pre_gain=pre_gain if h is None else None)
    k_cmp, v_cmpT = _compress(kc_tok, vc_tok, cmp_pos, cmp_w1, cmp_w2, k_gain[0])
    gates5 = gates.reshape(b, 3, g, r, s)
    o_cmp, sel_bias = _cmp_select(qT, k_cmp, v_cmpT, gates5, tq=CMP_Q_TILE)
    o_slc = _flash(qT, ks, vsT, mode="causal", n_par=1, n_rep=r, dq=HEAD_DIM, kc=SLC_KEY_LANES,
                   tq=NSA_SLC_Q_TILE, tk=NSA_SLC_TILE, cw=ITEM_LANES, sel_bias=sel_bias,
                   gates=gates5, gate_branch=1, name="nsa_selected")
    return _flash(qT, kw, vwT, mode="window", n_par=g, n_rep=r, dq=HEAD_DIM, kc=HEAD_DIM,
                  tq=NSA_WIN_Q_TILE, tk=NSA_WIN_TILE, cw=ITEM_LANES, window=NSA_WINDOW,
                  gates=gates5, gate_branch=2, out_proj=([o_cmp, o_slc], zs, x, w_out, next_gain),
                  name="nsa_window_out")


def _swa_mixer(x, h, cos, sin, w_in, q_gain, k_gain, sinks, w_out, next_gain):
    r = N_HEADS // SWA_KV_HEADS
    qT, k, vT, zs = _swa_proj(h, cos, sin, w_in, q_gain, k_gain, win_tile=SWA_TILE)
    return _flash(qT, k, vT, mode="window", n_par=SWA_KV_HEADS, n_rep=r, dq=HEAD_DIM, kc=HEAD_DIM, tq=SWA_TILE,
                  tk=SWA_TILE, cw=r * SWA_TILE, window=SWA_WINDOW, sinks=sinks,
                  out_proj=([], zs, x, w_out, next_gain), name="swa_window_out")


def _fox_mixer(x, h, w_in, forget_bias, q_gain, k_gain, w_out, next_gain):
    b, s, _ = h.shape
    qT, k, vT, zs = _fox_proj(h, w_in, forget_bias, q_gain, k_gain, tile=FOX_K_TILE)
    o = _flash(qT.reshape(b, N_HEADS * KEY_PAD, s), k, vT, mode="causal", n_par=FOX_HEADS_PER_STEP, n_rep=1, dq=KEY_PAD,
               kc=KEY_PAD, tq=FOX_Q_TILE, tk=FOX_K_TILE, cw=ITEM_LANES, name="fox_attention")
    return _out_proj([o], zs, x, w_out, next_gain)


def kernel(x, positions, norm_gains, a_w_in, a_q_gain, a_k_gain, a_cmp_pos, a_cmp_w1, a_cmp_w2, a_w_out,
           b_w_in, b_q_gain, b_k_gain, b_sinks, b_w_out,
           c_w_in, c_forget_bias, c_q_gain, c_k_gain, c_w_out):
    depth = norm_gains.shape[0]
    cos, sin = _rope_tables(positions)
    h = None
    for i in range(depth):
        j, mixer = divmod(i, 3)
        next_gain = norm_gains[i + 1] if i + 1 < depth else None
        if mixer == 0:
            x, h = _nsa_mixer(x, h, cos, sin, a_w_in[j], a_q_gain[j], a_k_gain[j],
                              a_cmp_pos[j], a_cmp_w1[j], a_cmp_w2[j], a_w_out[j], next_gain, norm_gains[i])
        elif mixer == 1:
            x, h = _swa_mixer(x, h, cos, sin, b_w_in[j], b_q_gain[j], b_k_gain[j], b_sinks[j],
                              b_w_out[j], next_gain)
        else:
            x, h = _fox_mixer(x, h, c_w_in[j], c_forget_bias[j], c_q_gain[j], c_k_gain[j],
                              c_w_out[j], next_gain)
    return x
```

```python
import functools

import jax
import jax.numpy as jnp
import numpy as np
from jax import lax
from jax.experimental import pallas as pl
from jax.experimental.pallas import tpu as pltpu

D_MODEL = 1024
HEAD_DIM = 64
N_HEADS = 16
MIX_WIDTH = N_HEADS * HEAD_DIM
ROPE_THETA = 10000.0
EPS = 1e-6
SCALE = HEAD_DIM ** -0.5
NEG_INF = -1e30
BIG = 1e30
M_INIT = -1e29
SEL_OFF = -(2.0 ** 100)

NSA_KV_HEADS = 4
NSA_CMP_LEN = 32
NSA_CMP_STRIDE = 16
NSA_SLC_LEN = 64
NSA_TOPK = 16
NSA_WINDOW = 512
SWA_KV_HEADS = 2
SWA_WINDOW = 128

LOG2E = float(np.log2(np.e))
Q_SCALE = SCALE * LOG2E

LANES = 128
KEY_PAD = 128
SLC_KEY_LANES = 256
V_ROWS = 80
FLASH_UNROLL = 4
WINDOW_LOOKAHEAD = 2
WINDOW_SCORE_BUFS = 4
CMP_BLOCK = 128
CMP_MASK_ROWS = 2 * CMP_BLOCK
VMEM_LIMIT = 56 * 1024 * 1024

TOK_TILE = 512
NT_DIMS = (((1,), (1,)), ((), ()))

BF = jnp.bfloat16
F32 = jnp.float32


def _params(sem):
    return pltpu.CompilerParams(dimension_semantics=sem, vmem_limit_bytes=VMEM_LIMIT)


def _dot(a, b):
    return jnp.dot(a, b, preferred_element_type=F32)


def _dot_nt(a, b):
    return lax.dot_general(a, b, NT_DIMS, preferred_element_type=F32)


def _rope_tab_kernel(pos_ref, invf_ref, cos_ref, sin_ref):
    ang = invf_ref[...] * pos_ref[0].astype(F32)
    cos_ref[0] = jnp.cos(ang)
    sin_ref[0] = jnp.sin(ang)


def _rope_tables(positions):
    b, s = positions.shape
    half = HEAD_DIM // 2
    inv_freq = ROPE_THETA ** (-jnp.arange(half, dtype=F32) * 2.0 / HEAD_DIM)
    tm = min(TOK_TILE, s)
    out = jax.ShapeDtypeStruct((b, half, s), F32)
    return pl.pallas_call(
        _rope_tab_kernel,
        grid=(b, s // tm),
        in_specs=[pl.BlockSpec((1, 1, tm), lambda i, j: (i, 0, j)),
                  pl.BlockSpec((half, 1), lambda i, j: (0, 0))],
        out_specs=[pl.BlockSpec((1, half, tm), lambda i, j: (i, 0, j))] * 2,
        out_shape=[out, out],
        compiler_params=_params(("parallel", "parallel")),
        name="rope_tables",
    )(positions.reshape(b, 1, s), inv_freq.reshape(half, 1))


def _rms_rows(x, gain_row):
    y = x * lax.rsqrt(jnp.mean(x * x, axis=-1, keepdims=True) + EPS)
    return y * gain_row


def _head_rms_t(y, gain_col):
    ms = jnp.mean(y * y, axis=0, keepdims=True)
    return (y * lax.rsqrt(ms + EPS)) * gain_col


def _rope_t(y, cos, sin):
    half = HEAD_DIM // 2
    x1, x2 = y[:half], y[half:]
    return jnp.concatenate([x1 * cos - x2 * sin, x2 * cos + x1 * sin], axis=0)


def _to_token_major(y):
    pad = jnp.zeros((KEY_PAD - y.shape[0], y.shape[1]), y.dtype)
    return jnp.concatenate([y, pad], axis=0).T


def _silu(z):
    return z * (1.0 / (1.0 + jnp.exp(-z)))


def _sigmoid(z):
    return 1.0 / (1.0 + jnp.exp(-z))


def _store_lane_tiles(out_ref, y, tile):
    for c in range(y.shape[1] // tile):
        out_ref[0, c] = y[:, c * tile:(c + 1) * tile]


def _store_chunk_rows(out_ref, gi, yt, tok_ref):
    tok_ref[...] = yt
    n = yt.shape[0] // NSA_CMP_STRIDE
    for m in range(NSA_CMP_STRIDE // 2):
        even = tok_ref[pl.ds(2 * m, n, stride=NSA_CMP_STRIDE), :]
        odd = tok_ref[pl.ds(2 * m + 1, n, stride=NSA_CMP_STRIDE), :]
        out_ref[0, gi, :, KEY_PAD * m:KEY_PAD * (m + 1)] = even + pltpu.roll(odd, HEAD_DIM, 1)


def _nsa_proj_kernel(*refs, slc_tile, win_tile, norm_in):
    it = iter(refs)
    h_ref = next(it)
    ng_ref = next(it) if norm_in else None
    (cos_ref, sin_ref, wq_ref, wk_ref, wv_ref, wg_ref, wz_ref, qg_ref, kg_ref,
     q_out, kc_out, vc_out, ks_out, kw_out, vs_out, vw_out, g_out, z_out, y_ref, tok_ref) = it
    h = _rms_rows(h_ref[0], ng_ref[...]).astype(BF) if norm_in else h_ref[0]
    cos, sin = cos_ref[0], sin_ref[0]
    g = NSA_KV_HEADS
    kvd = g * HEAD_DIM
    y_ref[...] = _dot_nt(wq_ref[...], h)
    for hd in range(N_HEADS):
        rows = slice(hd * HEAD_DIM, (hd + 1) * HEAD_DIM)
        y = _rope_t(_head_rms_t(y_ref[rows, :], qg_ref[...]), cos, sin) * Q_SCALE
        q_out[0, rows, :] = y.astype(BF)
    y_ref[0:3 * kvd, :] = _dot_nt(wk_ref[...], h)
    tm = h.shape[0]
    tok = pl.program_id(1) * tm + lax.broadcasted_iota(jnp.int32, (tm, 1), 0)
    blk_lane = HEAD_DIM + lax.shift_right_logical(tok, int(np.log2(NSA_SLC_LEN)))
    blk_hot = lax.broadcasted_iota(jnp.int32, (1, SLC_KEY_LANES), 1) == blk_lane
    for kind in range(3):
        for gi in range(g):
            r0 = (kind * g + gi) * HEAD_DIM
            y = y_ref[r0:r0 + HEAD_DIM, :]
            if kind > 0:
                y = _head_rms_t(y, kg_ref[:, kind:kind + 1])
            yt = _to_token_major(_rope_t(y, cos, sin))
            if kind == 0:
                _store_chunk_rows(kc_out, gi, yt, tok_ref)
            elif kind == 1:
                wide = jnp.concatenate([yt, jnp.zeros((tm, SLC_KEY_LANES - KEY_PAD), F32)], axis=1)
                ks_out[0, gi] = jnp.where(blk_hot, 1.0, wide).astype(BF)
            else:
                kw_out[0, gi] = yt.astype(BF)
    y_ref[0:3 * kvd, :] = _dot_nt(wv_ref[...], h)
    for gi in range(g):
        r0 = gi * HEAD_DIM
        _store_chunk_rows(vc_out, gi, _to_token_major(y_ref[r0:r0 + HEAD_DIM, :]), tok_ref)
    _store_lane_tiles(vs_out, y_ref[kvd:2 * kvd, :].astype(BF), slc_tile)
    _store_lane_tiles(vw_out, y_ref[2 * kvd:3 * kvd, :].astype(BF), win_tile)
    g_out[0] = _sigmoid(_dot_nt(wg_ref[...], h))
    z_out[0] = _silu(_dot(h, wz_ref[...])).astype(BF)


def _full(shape):
    nd = len(shape)
    return pl.BlockSpec(shape, lambda i, j, _n=nd: (0,) * _n)


def _nsa_proj(h, cos, sin, w_in, q_gain, k_gain, *, slc_tile, win_tile, pre_gain=None):
    b, s, d = h.shape
    g = NSA_KV_HEADS
    kvd = g * HEAD_DIM
    tm = min(TOK_TILE, s)
    flat = NSA_CMP_STRIDE * HEAD_DIM
    sizes = [MIX_WIDTH] + [kvd] * 6 + [3 * N_HEADS]
    off = np.cumsum([0] + sizes)
    wt = w_in.T.astype(BF)
    wq = wt[off[0]:off[1]]
    wk = jnp.concatenate([wt[off[1]:off[2]], wt[off[3]:off[4]], wt[off[5]:off[6]]], axis=0)
    wv = jnp.concatenate([wt[off[2]:off[3]], wt[off[4]:off[5]], wt[off[6]:off[7]]], axis=0)
    wg = wt[off[7]:off[8]]
    wz = w_in[:, off[8]:].astype(BF)
    qg = q_gain.reshape(HEAD_DIM, 1)
    kg = k_gain.T
    n_t = s // tm
    out_shape = [
        jax.ShapeDtypeStruct((b, MIX_WIDTH, s), BF),
        jax.ShapeDtypeStruct((b, g, s // NSA_CMP_STRIDE, flat), F32),
        jax.ShapeDtypeStruct((b, g, s // NSA_CMP_STRIDE, flat), F32),
        jax.ShapeDtypeStruct((b, g, s, SLC_KEY_LANES), BF),
        jax.ShapeDtypeStruct((b, g, s, KEY_PAD), BF),
        jax.ShapeDtypeStruct((b, s // slc_tile, kvd, slc_tile), BF),
        jax.ShapeDtypeStruct((b, s // win_tile, kvd, win_tile), BF),
        jax.ShapeDtypeStruct((b, 3 * N_HEADS, s), F32),
        jax.ShapeDtypeStruct((b, s, MIX_WIDTH), BF),
    ]
    out_specs = [
        pl.BlockSpec((1, MIX_WIDTH, tm), lambda i, j: (i, 0, j)),
        pl.BlockSpec((1, g, tm // NSA_CMP_STRIDE, flat), lambda i, j: (i, 0, j, 0)),
        pl.BlockSpec((1, g, tm // NSA_CMP_STRIDE, flat), lambda i, j: (i, 0, j, 0)),
        pl.BlockSpec((1, g, tm, SLC_KEY_LANES), lambda i, j: (i, 0, j, 0)),
        pl.BlockSpec((1, g, tm, KEY_PAD), lambda i, j: (i, 0, j, 0)),
        pl.BlockSpec((1, tm // slc_tile, kvd, slc_tile), lambda i, j: (i, j, 0, 0)),
        pl.BlockSpec((1, tm // win_tile, kvd, win_tile), lambda i, j: (i, j, 0, 0)),
        pl.BlockSpec((1, 3 * N_HEADS, tm), lambda i, j: (i, 0, j)),
        pl.BlockSpec((1, tm, MIX_WIDTH), lambda i, j: (i, j, 0)),
    ]
    norm_in = pre_gain is not None
    args = [h] + ([pre_gain.reshape(1, d)] if norm_in else []) + [cos, sin, wq, wk, wv, wg, wz, qg, kg]
    in_specs = [pl.BlockSpec((1, tm, d), lambda i, j: (i, j, 0))] + ([_full((1, d))] if norm_in else []) + [
        pl.BlockSpec((1, HEAD_DIM // 2, tm), lambda i, j: (i, 0, j)),
        pl.BlockSpec((1, HEAD_DIM // 2, tm), lambda i, j: (i, 0, j)),
        _full(wq.shape), _full(wk.shape), _full(wv.shape), _full(wg.shape), _full(wz.shape),
        _full(qg.shape), _full(kg.shape),
    ]
    return pl.pallas_call(
        functools.partial(_nsa_proj_kernel, slc_tile=slc_tile, win_tile=win_tile, norm_in=norm_in),
        grid=(b, n_t), in_specs=in_specs, out_specs=out_specs, out_shape=out_shape,
        scratch_shapes=[pltpu.VMEM((MIX_WIDTH, tm), F32), pltpu.VMEM((tm, KEY_PAD), F32)],
        compiler_params=_params(("parallel", "parallel")),
        name="nsa_proj",
    )(*args)


def _gelu_tanh(x):
    c = np.float32(np.sqrt(2.0 / np.pi))
    return 0.5 * x * (1.0 + jnp.tanh(c * (x + 0.044715 * (x * x * x))))


def _compress_kernel(kc_ref, vc_ref, pos_ref, w1_ref, w2_ref, kg_ref, kcmp_out, vcmp_out):
    for which, (src, dst) in enumerate(((kc_ref, kcmp_out), (vc_ref, vcmp_out))):
        x = src[0, 0]
        n = x.shape[0]
        half = x.shape[1]
        xa = (x + pos_ref[which, 0:1, :]).astype(BF)
        xb = (x + pos_ref[which, 1:2, :]).astype(BF)
        ua = _dot(xa, w1_ref[which, :half, :])
        ub = _dot(xb, w1_ref[which, half:, :])
        row = lax.broadcasted_iota(jnp.int32, (n, 1), 0)
        ub_next = jnp.where(row == n - 1, 0.0, pltpu.roll(ub, n - 1, 0))
        hid = _gelu_tanh(ua + ub_next)
        y = _dot(hid.astype(BF), w2_ref[which])
        if which == 0:
            y = _rms_rows(y, kg_ref[...])
            dst[0, 0] = y.astype(BF)
        else:
            pad = jnp.zeros((n, KEY_PAD - HEAD_DIM), F32)
            dst[0, 0] = jnp.concatenate([y, pad], axis=1).T[:HEAD_DIM].astype(BF)


def _compress(kc, vc, cmp_pos, cmp_w1, cmp_w2, k_gain0):
    b, g, n_chunk, flat = kc.shape
    pos = cmp_pos.reshape(2, 2, flat)
    w1 = cmp_w1.astype(BF)
    w2 = cmp_w2.astype(BF)
    blk = pl.BlockSpec((1, 1, n_chunk, flat), lambda i, j: (i, j, 0, 0))
    return pl.pallas_call(
        _compress_kernel,
        grid=(b, g),
        in_specs=[blk, blk, _full(pos.shape), _full(w1.shape), _full(w2.shape),
                  _full((1, HEAD_DIM))],
        out_specs=[pl.BlockSpec((1, 1, n_chunk, HEAD_DIM), lambda i, j: (i, j, 0, 0)),
                   pl.BlockSpec((1, 1, HEAD_DIM, n_chunk), lambda i, j: (i, j, 0, 0))],
        out_shape=[jax.ShapeDtypeStruct((b, g, n_chunk, HEAD_DIM), BF),
                   jax.ShapeDtypeStruct((b, g, HEAD_DIM, n_chunk), BF)],
        compiler_params=_params(("parallel", "parallel")),
        name="nsa_compress",
    )(kc, vc, pos, w1, w2, k_gain0.reshape(1, HEAD_DIM))


def _gate_row(gate_ref, n_heads, grp=0):
    return jnp.concatenate([gate_ref[0, 0, grp, r:r + 1, :] for r in range(n_heads)], axis=1)


def _cmp_branch(rows, q_ref, kc_ref, vc_ref, ov_ref, gate_ref, o_out, sel_out, *, tq, n_blk, n_grp):
    r_heads = N_HEADS // NSA_KV_HEADS
    nq = r_heads * tq
    width = r_heads * HEAD_DIM
    q0 = pl.program_id(2) * tq
    n_cmp = kc_ref.shape[2] - 1
    t_row = q0 + (lax.broadcasted_iota(jnp.int32, (1, nq), 1) & (tq - 1))
    lo = max(rows - CMP_MASK_ROWS, 0)
    c_col = lo + lax.broadcasted_iota(jnp.int32, (rows - lo, 1), 0)
    valid = (c_col * NSA_CMP_STRIDE + (NSA_CMP_LEN - 1) <= t_row) & (c_col < n_cmp)
    one_row = jnp.where(lax.broadcasted_iota(jnp.int32, (V_ROWS - HEAD_DIM, rows), 0) == 0, 1.0, 0.0).astype(BF)
    n_live = min(n_blk, rows * NSA_CMP_STRIDE // NSA_SLC_LEN + 8)
    t1 = q0 + lax.broadcasted_iota(jnp.int32, (1, tq), 1)
    cur = lax.shift_right_logical(t1, int(np.log2(NSA_SLC_LEN)))
    blk = lax.broadcasted_iota(jnp.int32, (n_live, tq), 0)
    forced = (blk == 0) | (blk == cur) | (blk == cur - 1)
    imps = []
    for gi in range(n_grp):
        q4 = jnp.concatenate([q_ref[0, (gi * r_heads + r) * HEAD_DIM:(gi * r_heads + r + 1) * HEAD_DIM, :]
                              for r in range(r_heads)], axis=1)
        s = _dot(kc_ref[0, gi, 0:rows, :], q4)
        s_new = jnp.where(valid, s[lo:], NEG_INF)
        m = jnp.max(s_new, axis=0, keepdims=True)
        if lo:
            m = jnp.maximum(m, jnp.max(s[:lo], axis=0, keepdims=True))
        e = jnp.where(valid, jnp.exp2(s_new - m), 0.0).astype(BF)
        if lo:
            e = jnp.concatenate([jnp.exp2(s[:lo] - m).astype(BF), e], axis=0)
        lhs = jnp.concatenate([vc_ref[0, gi, :, 0:rows], one_row, ov_ref[0:n_live, 0:rows]], axis=0)
        res = _dot(lhs, e)
        l = res[HEAD_DIM:HEAD_DIM + 1]
        inv = jnp.where(l > 0.0, 1.0 / jnp.where(l > 0.0, l, 1.0), 0.0)
        o = res[:HEAD_DIM] * (inv * _gate_row(gate_ref, r_heads, gi))
        o_rows = jnp.concatenate([o[:, r * tq:(r + 1) * tq] for r in range(r_heads)], axis=0)
        o_out[0, :, gi * width:(gi + 1) * width] = o_rows.T.astype(o_out.dtype)
        w = res[V_ROWS:V_ROWS + n_live] * inv
        imp = w[:, 0:tq]
        for r in range(1, r_heads):
            imp = imp + w[:, r * tq:(r + 1) * tq]
        imps.append(jnp.where(forced, BIG, jnp.where(blk > cur, NEG_INF, imp)))
    for _ in range(min(NSA_TOPK, n_blk)):
        for gi in range(n_grp):
            best = jnp.max(imps[gi], axis=0, keepdims=True)
            first = jnp.min(jnp.where(imps[gi] == best, blk, n_blk), axis=0, keepdims=True)
            imps[gi] = jnp.where(blk == first, -jnp.inf, imps[gi])
    for gi in range(n_grp):
        sel_out[0, gi, 0:n_live, :] = jnp.where(imps[gi] == -jnp.inf, 0.0, SEL_OFF).astype(BF)
        if n_live < n_blk:
            sel_out[0, gi, n_live:n_blk, :] = jnp.full((n_blk - n_live, tq), SEL_OFF, BF)


def _cmp_select_kernel(q_ref, kc_ref, vc_ref, ov_ref, gate_ref, o_out, sel_out, *, tq, n_blk, n_grp):
    q0 = pl.program_id(2) * tq
    n_chunk = kc_ref.shape[2]
    n_need = jnp.minimum((q0 + tq - NSA_CMP_LEN) // NSA_CMP_STRIDE + 1, n_chunk - 1)
    n_steps = n_chunk // CMP_BLOCK
    need_steps = (n_need + CMP_BLOCK - 1) // CMP_BLOCK
    for k in range(1, n_steps + 1):
        @pl.when(need_steps == k)
        def _(k=k):
            _cmp_branch(k * CMP_BLOCK, q_ref, kc_ref, vc_ref, ov_ref, gate_ref, o_out, sel_out,
                        tq=tq, n_blk=n_blk, n_grp=n_grp)


def _overlap_matrix(s):
    n_chunk = s // NSA_CMP_STRIDE
    n_blk = s // NSA_SLC_LEN
    c0 = np.arange(n_chunk) * NSA_CMP_STRIDE
    c1 = c0 + NSA_CMP_LEN - 1
    b0 = np.arange(n_blk) * NSA_SLC_LEN
    ov = np.minimum(c1[None, :], b0[:, None] + NSA_SLC_LEN - 1) - np.maximum(c0[None, :], b0[:, None]) + 1
    return jnp.asarray(np.clip(ov, 0, None) / NSA_CMP_LEN, BF)


def _cmp_select(qT, k_cmp, v_cmpT, gates5, *, tq):
    b, _, s = qT.shape
    g = NSA_KV_HEADS
    r_heads = N_HEADS // g
    n_chunk = k_cmp.shape[2]
    n_blk = s // NSA_SLC_LEN
    ov = _overlap_matrix(s)
    n_grp = CMP_GROUPS
    rows = n_grp * r_heads * HEAD_DIM
    return pl.pallas_call(
        functools.partial(_cmp_select_kernel, tq=tq, n_blk=n_blk, n_grp=n_grp),
        grid=(b, g // n_grp, s // tq),
        in_specs=[
            pl.BlockSpec((1, rows, tq), lambda i, j, k: (i, j, k)),
            pl.BlockSpec((1, n_grp, n_chunk, HEAD_DIM), lambda i, j, k: (i, j, 0, 0)),
            pl.BlockSpec((1, n_grp, HEAD_DIM, n_chunk), lambda i, j, k: (i, j, 0, 0)),
            pl.BlockSpec((n_blk, n_chunk), lambda i, j, k: (0, 0)),
            pl.BlockSpec((1, 1, n_grp, r_heads, tq), lambda i, j, k: (i, 0, j, 0, k)),
        ],
        out_specs=[pl.BlockSpec((1, tq, rows), lambda i, j, k: (i, k, j)),
                   pl.BlockSpec((1, n_grp, n_blk, tq), lambda i, j, k: (i, j, 0, k))],
        out_shape=[jax.ShapeDtypeStruct((b, s, MIX_WIDTH), BF),
                   jax.ShapeDtypeStruct((b, g, n_blk, s), BF)],
        compiler_params=_params(("parallel", "parallel", "parallel")),
        name="nsa_cmp_select",
    )(qT, k_cmp, v_cmpT, ov, gates5)


def _flash_kernel(*refs, mode, n_par, n_rep, dq, kc, tq, tk, cw, window, n_blk, has_sink, has_gate,
                  fuse_out=None):
    it = iter(refs)
    q_ref, k_ref, v_ref = next(it), next(it), next(it)
    sel_ref = next(it) if n_blk else None
    sink_ref = next(it) if has_sink else None
    gate_ref = next(it) if has_gate else None
    qnext_ref = next(it) if mode == "causal" else None
    selnext_ref = next(it) if (mode == "causal" and n_blk) else None
    if fuse_out is None:
        out_ref = next(it)
    else:
        n_extra, has_next = fuse_out
        extra_refs = [next(it) for _ in range(n_extra)]
        z_ref, x_ref, w_ref = next(it), next(it), next(it)
        ng_ref = next(it) if has_next else None
        x_out = next(it)
        h_out = next(it) if has_next else None
    qs_ref, m_ref, acc_ref, mt_ref = (next(it) for _ in range(4))
    qn_ref = next(it) if mode == "causal" else None
    bias_ref = next(it) if mode == "window" else None
    s_bufs = tuple(it)

    nq = n_rep * tq
    items = [(p, c) for p in range(n_par) for c in range(nq // cw)]
    n_items = len(items)
    grp = pl.program_id(1)
    q0 = pl.program_id(2) * tq
    acc_row = lax.broadcasted_iota(jnp.int32, (V_ROWS, nq), 0)
    for p in range(n_par):
        for r in range(n_rep):
            hd = p * n_rep + r
            qs_ref[p, 0:dq, r * tq:(r + 1) * tq] = q_ref[0, hd * dq:(hd + 1) * dq, :]
        if n_blk:
            qs_ref[p, dq:dq + n_blk, :] = jnp.concatenate([sel_ref[0, 0]] * n_rep, axis=1)
            if dq + n_blk < kc:
                qs_ref[p, dq + n_blk:kc, :] = jnp.zeros((kc - dq - n_blk, nq), BF)
        if has_sink:
            m_ref[p] = jnp.concatenate(
                [jnp.full((1, tq), sink_ref[(grp * n_par + p) * n_rep + r] * LOG2E, F32) for r in range(n_rep)],
                axis=1)
            acc_ref[p] = jnp.where(acc_row == HEAD_DIM, 1.0, 0.0)
        else:
            m_ref[p] = jnp.full((1, nq), M_INIT, F32)
            acc_ref[p] = jnp.zeros((V_ROWS, nq), F32)

    t_row = q0 + (lax.broadcasted_iota(jnp.int32, (1, nq), 1) & (tq - 1))
    one_row = jnp.where(lax.broadcasted_iota(jnp.int32, (V_ROWS - HEAD_DIM, tk), 0) == 0, 1.0, 0.0).astype(BF)

    def stage_a(item, j, key0, slot, causal, bias=None, pen=None):
        p, c = item
        cols = slice(c * cw, (c + 1) * cw)
        s = _dot(k_ref[0, p, j][:, :kc], qs_ref[p, :, cols])
        if causal:
            key = key0 + lax.broadcasted_iota(jnp.int32, (tk, 1), 0)
            s = jnp.where(key <= t_row[:, cols], s, NEG_INF)
        if bias is not None:
            s = s + bias_ref[bias]
        s_bufs[slot][...] = s
        mt = jnp.max(s, axis=0, keepdims=True)
        mt_ref[slot] = mt if pen is None else mt - pen

    def stage_b(item, j, slot, pen=None):
        p, c = item
        cols = slice(c * cw, (c + 1) * cw)
        m_old = m_ref[p, :, cols]
        m_new = jnp.maximum(m_old, mt_ref[slot])
        pr = jnp.exp2(s_bufs[slot][...] - (m_new if pen is None else m_new + pen)).astype(BF)
        alpha = jnp.exp2(m_old - m_new)
        v = jnp.concatenate([v_ref[0, j, p * HEAD_DIM:(p + 1) * HEAD_DIM, :], one_row], axis=0)
        acc_ref[p, :, cols] = alpha * acc_ref[p, :, cols] + _dot(v, pr)
        m_ref[p, :, cols] = m_new

    def finalize():
        outs = []
        for p in range(n_par):
            acc = acc_ref[p]
            o = acc[:HEAD_DIM] * (1.0 / acc[HEAD_DIM:HEAD_DIM + 1])
            if has_gate:
                o = o * _gate_row(gate_ref, n_rep, p)
            outs += [o[:, r * tq:(r + 1) * tq] for r in range(n_rep)]
        o_tok = jnp.concatenate(outs, axis=0).T
        if fuse_out is None:
            out_ref[0] = o_tok.astype(out_ref.dtype)
            return
        for r in extra_refs:
            o_tok = o_tok + r[0].astype(F32)
        y = _dot((o_tok * z_ref[0].astype(F32)).astype(BF), w_ref[...])
        x_new = x_ref[0] + y
        x_out[0] = x_new
        if has_next:
            h_out[0] = _rms_rows(x_new, ng_ref[...]).astype(BF)

    if mode == "causal":
        assert n_items % 2 == 0
        n_full = q0 // tk

        def prefetch_next_tile():
            assert cw == tq
            qn_ref[0:dq, :] = qnext_ref[0, 0:dq, :]
            if n_blk:
                qn_ref[dq:dq + n_blk, :] = selnext_ref[0, 0]
                if dq + n_blk < kc:
                    qn_ref[dq + n_blk:kc, :] = jnp.zeros((kc - dq - n_blk, tq), BF)
            s = _dot(k_ref[0, 0, 0][:, :kc], qn_ref[...])
            key = lax.broadcasted_iota(jnp.int32, (tk, 1), 0)
            s = jnp.where(key <= t_row[:, 0:cw] + tq, s, NEG_INF)
            s_bufs[0][...] = s
            mt_ref[0] = jnp.max(s, axis=0, keepdims=True)

        def step(j, kind, next_kind, last):
            for idx, item in enumerate(items):
                slot = idx % 2
                if idx + 1 < n_items:
                    stage_a(items[idx + 1], j, j * tk, 1 - slot, kind)
                elif not last:
                    stage_a(items[0], j + 1, (j + 1) * tk, 1 - slot, next_kind)
                else:
                    prefetch_next_tile()
                stage_b(item, j, slot)

        @pl.when(pl.program_id(2) == 0)
        def _():
            stage_a(items[0], 0, 0, 0, "causal")

        def body(j, carry):
            step(j, None, None, False)
            return carry

        def body_group(i, carry):
            for u in range(FLASH_UNROLL):
                step(FLASH_UNROLL * i + u, None, None, False)
            return carry

        n_main = jnp.maximum(n_full - 1, 0)
        n_groups = lax.shift_right_logical(n_main, int(np.log2(FLASH_UNROLL)))
        lax.fori_loop(0, n_groups, body_group, 0)
        lax.fori_loop(FLASH_UNROLL * n_groups, n_main, body, 0)

        @pl.when(n_full >= 1)
        def _():
            step(n_full - 1, None, "causal", False)
            step(n_full, "causal", None, True)
            finalize()

        @pl.when(n_full == 0)
        def _():
            step(0, "causal", None, True)
            finalize()
    else:
        w_tiles, q_tiles = window // tk, tq // tk
        assert w_tiles >= q_tiles and cw % tq == 0
        t_loc = lax.broadcasted_iota(jnp.int32, (1, cw), 1) & (tq - 1)
        key_loc = lax.broadcasted_iota(jnp.int32, (tk, 1), 0)
        for i in range(q_tiles):
            bias_ref[i] = jnp.where(i * tk + key_loc > t_loc, 0.0, NEG_INF)
            bias_ref[q_tiles + i] = jnp.where(i * tk + key_loc <= t_loc, 0.0, NEG_INF)
        work = []
        for i in range(w_tiles + q_tiles):
            jv = q0 // tk - w_tiles + i
            bias = i if i < q_tiles else (q_tiles + i - w_tiles if i >= w_tiles else None)
            pen = jnp.where(jv < 0, -NEG_INF, 0.0) if i < w_tiles else None
            work += [(item, jnp.maximum(jv, 0), bias, pen) for item in items]
        n_buf = len(s_bufs)
        ahead = WINDOW_LOOKAHEAD
        for n in range(min(ahead, len(work))):
            stage_a(work[n][0], work[n][1], 0, n % n_buf, False, work[n][2], work[n][3])
        for n, (item, j, _, pen) in enumerate(work):
            if n + ahead < len(work):
                nxt = work[n + ahead]
                stage_a(nxt[0], nxt[1], 0, (n + ahead) % n_buf, False, nxt[2], nxt[3])
            stage_b(item, j, n % n_buf, pen)

        finalize()


def _flash(qT, k_tok, vT_tiles, *, mode, n_par, n_rep, dq, kc, tq, tk, cw, window=None,
           sel_bias=None, sinks=None, gates=None, gate_branch=0, out_proj=None, name="flash"):
    b, _, s = qT.shape
    kh, k_lanes = k_tok.shape[1], k_tok.shape[3]
    n_t = s // tk
    k5 = k_tok.reshape(b, kh, n_t, tk, k_lanes)
    n_grp = kh // n_par
    heads = n_par * n_rep
    nq = n_rep * tq
    n_blk = sel_bias.shape[2] if sel_bias is not None else 0
    has_sink, has_gate = sinks is not None, gates is not None
    args = [qT, k5, vT_tiles]
    in_specs = [
        pl.BlockSpec((1, heads * dq, tq), lambda i, j, k: (i, j, k)),
        pl.BlockSpec((1, n_par, n_t, tk, k_lanes), lambda i, j, k: (i, j, 0, 0, 0)),
        pl.BlockSpec((1, n_t, n_par * HEAD_DIM, tk), lambda i, j, k: (i, 0, j, 0)),
    ]
    if n_blk:
        assert dq + n_blk <= kc
        args.append(sel_bias)
        in_specs.append(pl.BlockSpec((1, 1, n_blk, tq), lambda i, j, k: (i, j, 0, k)))
    if has_sink:
        args.append(sinks.astype(F32))
        in_specs.append(pl.BlockSpec(memory_space=pltpu.SMEM))
    if has_gate:
        args.append(gates)
        in_specs.append(pl.BlockSpec((1, 1, n_par, n_rep, tq),
                                     lambda i, j, k, _br=gate_branch: (i, _br, j, 0, k)))
    n_sbuf = 2 if mode == "causal" else WINDOW_SCORE_BUFS
    scratch = [pltpu.VMEM((n_par, kc, nq), BF), pltpu.VMEM((n_par, 1, nq), F32),
               pltpu.VMEM((n_par, V_ROWS, nq), F32), pltpu.VMEM((n_sbuf, 1, cw), F32)]
    if mode == "causal":
        last_q = s // tq - 1
        args.append(qT)
        in_specs.append(pl.BlockSpec((1, dq, tq), lambda i, j, k: (i, j * heads, jnp.minimum(k + 1, last_q))))
        if n_blk:
            args.append(sel_bias)
            in_specs.append(pl.BlockSpec((1, 1, n_blk, tq), lambda i, j, k: (i, j, 0, jnp.minimum(k + 1, last_q))))
        scratch.append(pltpu.VMEM((kc, tq), BF))
    else:
        scratch.append(pltpu.VMEM((2 * (tq // tk), tk, cw), F32))
    scratch += [pltpu.VMEM((tk, cw), F32)] * n_sbuf
    out_specs = pl.BlockSpec((1, tq, heads * HEAD_DIM), lambda i, j, k: (i, k, j))
    out_shape = jax.ShapeDtypeStruct((b, s, MIX_WIDTH), BF)
    fuse_out = None
    if out_proj is not None:
        assert mode == "window" and heads * HEAD_DIM == MIX_WIDTH and n_grp == 1
        extra, zs, x, w_out, next_gain = out_proj
        tok = pl.BlockSpec((1, tq, D_MODEL), lambda i, j, k: (i, k, 0))
        args += list(extra) + [zs, x, w_out.astype(BF)]
        in_specs += [tok] * (len(extra) + 2) + [pl.BlockSpec(w_out.shape, lambda i, j, k: (0, 0))]
        out_specs, out_shape = [tok], [jax.ShapeDtypeStruct(x.shape, F32)]
        if next_gain is not None:
            args.append(next_gain.reshape(1, D_MODEL))
            in_specs.append(pl.BlockSpec((1, D_MODEL), lambda i, j, k: (0, 0)))
            out_specs.append(tok)
            out_shape.append(jax.ShapeDtypeStruct(x.shape, BF))
        fuse_out = (len(extra), next_gain is not None)
    kern = functools.partial(_flash_kernel, mode=mode, n_par=n_par, n_rep=n_rep, dq=dq, kc=kc, tq=tq, tk=tk,
                             cw=cw, window=window, n_blk=n_blk, has_sink=has_sink, has_gate=has_gate,
                             fuse_out=fuse_out)
    res = pl.pallas_call(
        kern,
        grid=(b, n_grp, s // tq),
        in_specs=in_specs,
        out_specs=out_specs,
        out_shape=out_shape,
        scratch_shapes=scratch,
        compiler_params=_params(("parallel", "parallel", "arbitrary")),
        name=name,
    )(*args)
    if out_proj is None:
        return res
    return (res[0], res[1]) if next_gain is not None else (res[0], None)


def _swa_proj_kernel(h_ref, cos_ref, sin_ref, wq_ref, wk_ref, wv_ref, wz_ref, qg_ref, kg_ref,
                     q_out, k_out, v_out, z_out, y_ref, *, win_tile):
    h = h_ref[0]
    cos, sin = cos_ref[0], sin_ref[0]
    y_ref[...] = _dot_nt(wq_ref[...], h)
    for hd in range(N_HEADS):
        rows = slice(hd * HEAD_DIM, (hd + 1) * HEAD_DIM)
        q_out[0, rows, :] = (_rope_t(_head_rms_t(y_ref[rows, :], qg_ref[...]), cos, sin) * Q_SCALE).astype(BF)
    kvd = SWA_KV_HEADS * HEAD_DIM
    y_ref[0:kvd, :] = _dot_nt(wk_ref[...], h)
    for gi in range(SWA_KV_HEADS):
        rows = slice(gi * HEAD_DIM, (gi + 1) * HEAD_DIM)
        k_out[0, gi] = _to_token_major(_rope_t(_head_rms_t(y_ref[rows, :], kg_ref[...]), cos, sin)).astype(BF)
    _store_lane_tiles(v_out, _dot_nt(wv_ref[...], h).astype(BF), win_tile)
    z_out[0] = _silu(_dot(h, wz_ref[...])).astype(BF)


def _swa_proj(h, cos, sin, w_in, q_gain, k_gain, *, win_tile):
    b, s, d = h.shape
    g = SWA_KV_HEADS
    kvd = g * HEAD_DIM
    tm = min(TOK_TILE, s)
    wt = w_in.T.astype(BF)
    wq, wk, wv = wt[:MIX_WIDTH], wt[MIX_WIDTH:MIX_WIDTH + kvd], wt[MIX_WIDTH + kvd:MIX_WIDTH + 2 * kvd]
    wz = w_in[:, MIX_WIDTH + 2 * kvd:].astype(BF)
    qg, kg = q_gain.reshape(HEAD_DIM, 1), k_gain.reshape(HEAD_DIM, 1)
    wpt = tm // win_tile
    half = HEAD_DIM // 2
    return pl.pallas_call(
        functools.partial(_swa_proj_kernel, win_tile=win_tile),
        grid=(b, s // tm),
        in_specs=[pl.BlockSpec((1, tm, d), lambda i, j: (i, j, 0)),
                  pl.BlockSpec((1, half, tm), lambda i, j: (i, 0, j)),
                  pl.BlockSpec((1, half, tm), lambda i, j: (i, 0, j)),
                  _full(wq.shape), _full(wk.shape), _full(wv.shape), _full(wz.shape),
                  _full(qg.shape), _full(kg.shape)],
        out_specs=[pl.BlockSpec((1, MIX_WIDTH, tm), lambda i, j: (i, 0, j)),
                   pl.BlockSpec((1, g, tm, KEY_PAD), lambda i, j: (i, 0, j, 0)),
                   pl.BlockSpec((1, wpt, kvd, win_tile), lambda i, j: (i, j, 0, 0)),
                   pl.BlockSpec((1, tm, MIX_WIDTH), lambda i, j: (i, j, 0))],
        out_shape=[jax.ShapeDtypeStruct((b, MIX_WIDTH, s), BF),
                   jax.ShapeDtypeStruct((b, g, s, KEY_PAD), BF),
                   jax.ShapeDtypeStruct((b, s // win_tile, kvd, win_tile), BF),
                   jax.ShapeDtypeStruct((b, s, MIX_WIDTH), BF)],
        scratch_shapes=[pltpu.VMEM((MIX_WIDTH, tm), F32)],
        compiler_params=_params(("parallel", "parallel")),
        name="swa_proj",
    )(h, cos, sin, wq, wk, wv, wz, qg, kg)


def _split3(x):
    hi = x.astype(BF)
    r1 = x - hi.astype(F32)
    mid = r1.astype(BF)
    lo = (r1 - mid.astype(F32)).astype(BF)
    return hi, mid, lo


def _fox_proj_kernel(h_ref, wf_ref, bias_ref, tri_ref, wq_ref, wk_ref, wv_ref, wz_ref, qg_ref, kg_ref,
                     q_out, k_out, v_out, z_out, yq_ref, yk_ref, cum_ref, carry_ref, *, tile):
    @pl.when(pl.program_id(1) == 0)
    def _():
        carry_ref[...] = jnp.zeros_like(carry_ref)

    h = h_ref[0]
    tm = h.shape[0]
    x = _dot_nt(wf_ref[...], h) + bias_ref[...]
    logf = jnp.minimum(x, 0.0) - jnp.log(1.0 + jnp.exp(-jnp.abs(x)))
    tri = tri_ref[...]
    hi, mid, lo = _split3(logf)
    cum = (_dot(hi, tri) + _dot(mid, tri)) + _dot(lo, tri) + carry_ref[:, 0:1]
    carry_ref[...] = jnp.broadcast_to(cum[:, -1:], carry_ref.shape)
    cum_ref[...] = cum * LOG2E

    yq_ref[...] = _dot_nt(wq_ref[...], h)
    yk_ref[...] = _dot_nt(wk_ref[...], h)
    row = lax.broadcasted_iota(jnp.int32, (8, tm), 0)
    zeros = jnp.zeros((KEY_PAD - HEAD_DIM - 16, tm), F32)
    for hd in range(N_HEADS):
        rows = slice(hd * HEAD_DIM, (hd + 1) * HEAD_DIM)
        c_hi, c_mid, c_lo = (c.astype(F32) for c in _split3(cum_ref[hd:hd + 1, :]))
        c3 = jnp.where(row == 0, c_hi, jnp.where(row == 1, c_mid, jnp.where(row == 2, c_lo, 0.0)))
        one3 = jnp.where(row < 3, 1.0, 0.0)
        q = _head_rms_t(yq_ref[rows, :], qg_ref[...]) * Q_SCALE
        q_out[0, hd] = jnp.concatenate([q, one3, c3, zeros], axis=0).astype(BF)
        k = _head_rms_t(yk_ref[rows, :], kg_ref[...])
        k_out[0, hd] = jnp.concatenate([k, -c3, one3, zeros], axis=0).T.astype(BF)
    _store_lane_tiles(v_out, _dot_nt(wv_ref[...], h).astype(BF), tile)
    z_out[0] = _silu(_dot(h, wz_ref[...])).astype(BF)


def _fox_proj(h, w_in, forget_bias, q_gain, k_gain, *, tile):
    b, s, d = h.shape
    tm = min(TOK_TILE, s)
    wt = w_in.T.astype(BF)
    wq, wk, wv = wt[:MIX_WIDTH], wt[MIX_WIDTH:2 * MIX_WIDTH], wt[2 * MIX_WIDTH:3 * MIX_WIDTH]
    wf = wt[3 * MIX_WIDTH:3 * MIX_WIDTH + N_HEADS]
    wz = w_in[:, 3 * MIX_WIDTH + N_HEADS:].astype(BF)
    qg, kg = q_gain.reshape(HEAD_DIM, 1), k_gain.reshape(HEAD_DIM, 1)
    tri = jnp.asarray(np.arange(tm)[:, None] <= np.arange(tm)[None, :], BF)
    return pl.pallas_call(
        functools.partial(_fox_proj_kernel, tile=tile),
        grid=(b, s // tm),
        in_specs=[pl.BlockSpec((1, tm, d), lambda i, j: (i, j, 0)),
                  _full(wf.shape), _full((N_HEADS, 1)), _full(tri.shape),
                  _full(wq.shape), _full(wk.shape), _full(wv.shape), _full(wz.shape),
                  _full(qg.shape), _full(kg.shape)],
        out_specs=[pl.BlockSpec((1, N_HEADS, KEY_PAD, tm), lambda i, j: (i, 0, 0, j)),
                   pl.BlockSpec((1, N_HEADS, tm, KEY_PAD), lambda i, j: (i, 0, j, 0)),
                   pl.BlockSpec((1, tm // tile, MIX_WIDTH, tile), lambda i, j: (i, j, 0, 0)),
                   pl.BlockSpec((1, tm, MIX_WIDTH), lambda i, j: (i, j, 0))],
        out_shape=[jax.ShapeDtypeStruct((b, N_HEADS, KEY_PAD, s), BF),
                   jax.ShapeDtypeStruct((b, N_HEADS, s, KEY_PAD), BF),
                   jax.ShapeDtypeStruct((b, s // tile, MIX_WIDTH, tile), BF),
                   jax.ShapeDtypeStruct((b, s, MIX_WIDTH), BF)],
        scratch_shapes=[pltpu.VMEM((MIX_WIDTH, tm), F32), pltpu.VMEM((MIX_WIDTH, tm), F32),
                        pltpu.VMEM((N_HEADS, tm), F32), pltpu.VMEM((N_HEADS, LANES), F32)],
        compiler_params=_params(("parallel", "arbitrary")),
        name="fox_proj",
    )(h, wf, forget_bias.reshape(N_HEADS, 1).astype(F32), tri, wq, wk, wv, wz, qg, kg)


def _out_proj_kernel(*refs, n_o, has_next):
    o_refs = refs[:n_o]
    z_ref, x_ref, w_ref = refs[n_o:n_o + 3]
    rest = refs[n_o + 3:]
    if has_next:
        g_ref, x_out, h_out = rest
    else:
        (x_out,) = rest
    o = o_refs[0][0].astype(F32)
    for r in o_refs[1:]:
        o = o + r[0].astype(F32)
    y = _dot((o * z_ref[0].astype(F32)).astype(BF), w_ref[...])
    x_new = x_ref[0] + y
    x_out[0] = x_new
    if has_next:
        h_out[0] = _rms_rows(x_new, g_ref[...]).astype(BF)


def _out_proj(o_list, zs, x, w_out, next_gain):
    b, s, d = x.shape
    tm = min(TOK_TILE, s)
    has_next = next_gain is not None
    blk = pl.BlockSpec((1, tm, d), lambda i, j: (i, j, 0))
    args = list(o_list) + [zs, x, w_out.astype(BF)]
    in_specs = [blk] * (len(o_list) + 2) + [_full(w_out.shape)]
    out_shape = [jax.ShapeDtypeStruct((b, s, d), F32)]
    out_specs = [blk]
    if has_next:
        args.append(next_gain.reshape(1, d))
        in_specs.append(_full((1, d)))
        out_shape.append(jax.ShapeDtypeStruct((b, s, d), BF))
        out_specs.append(blk)
    res = pl.pallas_call(
        functools.partial(_out_proj_kernel, n_o=len(o_list), has_next=has_next),
        grid=(b, s // tm), in_specs=in_specs, out_specs=out_specs, out_shape=out_shape,
        compiler_params=_params(("parallel", "parallel")),
        name="out_proj",
    )(*args)
    return (res[0], res[1]) if has_next else (res[0], None)


ITEM_LANES = 512
CMP_Q_TILE = 512
CMP_GROUPS = 2
NSA_SLC_Q_TILE = 512
NSA_SLC_TILE = 512
NSA_WIN_Q_TILE = 256
NSA_WIN_TILE = 256
SWA_TILE = 128
FOX_Q_TILE = 512
FOX_K_TILE = 512
FOX_HEADS_PER_STEP = 4


def _nsa_mixer(x, h, cos, sin, w_in, q_gain, k_gain, cmp_pos, cmp_w1, cmp_w2, w_out, next_gain, pre_gain):
    b, s, _ = x.shape
    g = NSA_KV_HEADS
    r = N_HEADS // g
    qT, kc_tok, vc_tok, ks, kw, vsT, vwT, gates, zs = _nsa_proj(
        x if h is None else h, cos, sin, w_in, q_gain, k_gain, slc_tile=NSA_SLC_TILE, win_tile=NSA_WIN_TILE,
        pre_gain=pre_gain if h is None else None)
    k_cmp, v_cmpT = _compress(kc_tok, vc_tok, cmp_pos, cmp_w1, cmp_w2, k_gain[0])
    gates5 = gates.reshape(b, 3, g, r, s)
    o_cmp, sel_bias = _cmp_select(qT, k_cmp, v_cmpT, gates5, tq=CMP_Q_TILE)
    o_slc = _flash(qT, ks, vsT, mode="causal", n_par=1, n_rep=r, dq=HEAD_DIM, kc=SLC_KEY_LANES,
                   tq=NSA_SLC_Q_TILE, tk=NSA_SLC_TILE, cw=ITEM_LANES, sel_bias=sel_bias,
                   gates=gates5, gate_branch=1, name="nsa_selected")
    return _flash(qT, kw, vwT, mode="window", n_par=g, n_rep=r, dq=HEAD_DIM, kc=HEAD_DIM,
                  tq=NSA_WIN_Q_TILE, tk=NSA_WIN_TILE, cw=ITEM_LANES, window=NSA_WINDOW,
                  gates=gates5, gate_branch=2, out_proj=([o_cmp, o_slc], zs, x, w_out, next_gain),
                  name="nsa_window_out")


def _swa_mixer(x, h, cos, sin, w_in, q_gain, k_gain, sinks, w_out, next_gain):
    r = N_HEADS // SWA_KV_HEADS
    qT, k, vT, zs = _swa_proj(h, cos, sin, w_in, q_gain, k_gain, win_tile=SWA_TILE)
    return _flash(qT, k, vT, mode="window", n_par=SWA_KV_HEADS, n_rep=r, dq=HEAD_DIM, kc=HEAD_DIM, tq=SWA_TILE,
                  tk=SWA_TILE, cw=r * SWA_TILE, window=SWA_WINDOW, sinks=sinks,
                  out_proj=([], zs, x, w_out, next_gain), name="swa_window_out")


def _fox_mixer(x, h, w_in, forget_bias, q_gain, k_gain, w_out, next_gain):
    b, s, _ = h.shape
    qT, k, vT, zs = _fox_proj(h, w_in, forget_bias, q_gain, k_gain, tile=FOX_K_TILE)
    o = _flash(qT.reshape(b, N_HEADS * KEY_PAD, s), k, vT, mode="causal", n_par=FOX_HEADS_PER_STEP, n_rep=1, dq=KEY_PAD,
               kc=KEY_PAD, tq=FOX_Q_TILE, tk=FOX_K_TILE, cw=ITEM_LANES, name="fox_attention")
    return _out_proj([o], zs, x, w_out, next_gain)


def kernel(x, positions, norm_gains, a_w_in, a_q_gain, a_k_gain, a_cmp_pos, a_cmp_w1, a_cmp_w2, a_w_out,
           b_w_in, b_q_gain, b_k_gain, b_sinks, b_w_out,
           c_w_in, c_forget_bias, c_q_gain, c_k_gain, c_w_out):
    depth = norm_gains.shape[0]
    cos, sin = _rope_tables(positions)
    h = None
    for i in range(depth):
        j, mixer = divmod(i, 3)
        next_gain = norm_gains[i + 1] if i + 1 < depth else None
        if mixer == 0:
            x, h = _nsa_mixer(x, h, cos, sin, a_w_in[j], a_q_gain[j], a_k_gain[j],
                              a_cmp_pos[j], a_cmp_w1[j], a_cmp_w2[j], a_w_out[j], next_gain, norm_gains[i])
        elif mixer == 1:
            x, h = _swa_mixer(x, h, cos, sin, b_w_in[j], b_q_gain[j], b_k_gain[j], b_sinks[j],
                              b_w_out[j], next_gain)
        else:
            x, h = _fox_mixer(x, h, c_w_in[j], c_forget_bias[j], c_q_gain[j], c_k_gain[j],
                              c_w_out[j], next_gain)
    return x
```

```python
import functools

import jax
import jax.numpy as jnp
import numpy as np
from jax import lax
from jax.experimental import pallas as pl
from jax.experimental.pallas import tpu as pltpu

D_MODEL = 1024
HEAD_DIM = 64
N_HEADS = 16
MIX_WIDTH = N_HEADS * HEAD_DIM
ROPE_THETA = 10000.0
EPS = 1e-6
SCALE = HEAD_DIM ** -0.5
NEG_INF = -1e30
M_INIT = -1e29
SEL_OFF = -(2.0 ** 100)

NSA_KV_HEADS = 4
NSA_CMP_LEN = 32
NSA_CMP_STRIDE = 16
NSA_SLC_LEN = 64
NSA_TOPK = 16
NSA_WINDOW = 512
SWA_KV_HEADS = 2
SWA_WINDOW = 128

LOG2E = float(np.log2(np.e))
Q_SCALE = SCALE * LOG2E

LANES = 128
KEY_PAD = 128
SLC_KEY_LANES = 256
V_ROWS = 80
FLASH_UNROLL = 4
WINDOW_LOOKAHEAD = 2
WINDOW_SCORE_BUFS = 4
CMP_BLOCK = 128
CMP_MASK_ROWS = 2 * CMP_BLOCK
VMEM_LIMIT = 56 * 1024 * 1024

TOK_TILE = 512
NT_DIMS = (((1,), (1,)), ((), ()))

BF = jnp.bfloat16
F32 = jnp.float32


def _params(sem):
    return pltpu.CompilerParams(dimension_semantics=sem, vmem_limit_bytes=VMEM_LIMIT)


def _dot(a, b):
    return jnp.dot(a, b, preferred_element_type=F32)


def _dot_nt(a, b):
    return lax.dot_general(a, b, NT_DIMS, preferred_element_type=F32)


def _rope_tab_kernel(pos_ref, invf_ref, cos_ref, sin_ref):
    ang = invf_ref[...] * pos_ref[0].astype(F32)
    cos_ref[0] = jnp.cos(ang)
    sin_ref[0] = jnp.sin(ang)


def _rope_tables(positions):
    b, s = positions.shape
    half = HEAD_DIM // 2
    inv_freq = ROPE_THETA ** (-jnp.arange(half, dtype=F32) * 2.0 / HEAD_DIM)
    tm = min(TOK_TILE, s)
    out = jax.ShapeDtypeStruct((b, half, s), F32)
    return pl.pallas_call(
        _rope_tab_kernel,
        grid=(b, s // tm),
        in_specs=[pl.BlockSpec((1, 1, tm), lambda i, j: (i, 0, j)),
                  pl.BlockSpec((half, 1), lambda i, j: (0, 0))],
        out_specs=[pl.BlockSpec((1, half, tm), lambda i, j: (i, 0, j))] * 2,
        out_shape=[out, out],
        compiler_params=_params(("parallel", "parallel")),
        name="rope_tables",
    )(positions.reshape(b, 1, s), inv_freq.reshape(half, 1))


def _rms_rows(x, gain_row):
    y = x * lax.rsqrt(jnp.mean(x * x, axis=-1, keepdims=True) + EPS)
    return y * gain_row


def _head_rms_t(y, gain_col):
    ms = jnp.mean(y * y, axis=0, keepdims=True)
    return (y * lax.rsqrt(ms + EPS)) * gain_col


def _rope_t(y, cos, sin):
    half = HEAD_DIM // 2
    x1, x2 = y[:half], y[half:]
    return jnp.concatenate([x1 * cos - x2 * sin, x2 * cos + x1 * sin], axis=0)


def _to_token_major(y):
    pad = jnp.zeros((KEY_PAD - y.shape[0], y.shape[1]), y.dtype)
    return jnp.concatenate([y, pad], axis=0).T


def _silu(z):
    return z * (1.0 / (1.0 + jnp.exp(-z)))


def _sigmoid(z):
    return 1.0 / (1.0 + jnp.exp(-z))


def _store_lane_tiles(out_ref, y, tile):
    for c in range(y.shape[1] // tile):
        out_ref[0, c] = y[:, c * tile:(c + 1) * tile]


def _store_chunk_rows(out_ref, gi, yt, tok_ref):
    tok_ref[...] = yt
    n = yt.shape[0] // NSA_CMP_STRIDE
    for m in range(NSA_CMP_STRIDE // 2):
        even = tok_ref[pl.ds(2 * m, n, stride=NSA_CMP_STRIDE), :]
        odd = tok_ref[pl.ds(2 * m + 1, n, stride=NSA_CMP_STRIDE), :]
        out_ref[0, gi, :, KEY_PAD * m:KEY_PAD * (m + 1)] = even + pltpu.roll(odd, HEAD_DIM, 1)


def _nsa_proj_kernel(*refs, slc_tile, win_tile, norm_in):
    it = iter(refs)
    h_ref = next(it)
    ng_ref = next(it) if norm_in else None
    (cos_ref, sin_ref, wq_ref, wk_ref, wv_ref, wg_ref, wz_ref, qg_ref, kg_ref,
     q_out, kc_out, vc_out, ks_out, kw_out, vs_out, vw_out, g_out, z_out, y_ref, tok_ref) = it
    h = _rms_rows(h_ref[0], ng_ref[...]).astype(BF) if norm_in else h_ref[0]
    cos, sin = cos_ref[0], sin_ref[0]
    g = NSA_KV_HEADS
    kvd = g * HEAD_DIM
    y_ref[...] = _dot_nt(wq_ref[...], h)
    for hd in range(N_HEADS):
        rows = slice(hd * HEAD_DIM, (hd + 1) * HEAD_DIM)
        y = _rope_t(_head_rms_t(y_ref[rows, :], qg_ref[...]), cos, sin) * Q_SCALE
        q_out[0, rows, :] = y.astype(BF)
    y_ref[0:3 * kvd, :] = _dot_nt(wk_ref[...], h)
    tm = h.shape[0]
    tok = pl.program_id(1) * tm + lax.broadcasted_iota(jnp.int32, (tm, 1), 0)
    blk_lane = HEAD_DIM + lax.shift_right_logical(tok, int(np.log2(NSA_SLC_LEN)))
    blk_hot = lax.broadcasted_iota(jnp.int32, (1, SLC_KEY_LANES), 1) == blk_lane
    for kind in range(3):
        for gi in range(g):
            r0 = (kind * g + gi) * HEAD_DIM
            y = y_ref[r0:r0 + HEAD_DIM, :]
            if kind > 0:
                y = _head_rms_t(y, kg_ref[:, kind:kind + 1])
            yt = _to_token_major(_rope_t(y, cos, sin))
            if kind == 0:
                _store_chunk_rows(kc_out, gi, yt, tok_ref)
            elif kind == 1:
                wide = jnp.concatenate([yt, jnp.zeros((tm, SLC_KEY_LANES - KEY_PAD), F32)], axis=1)
                ks_out[0, gi] = jnp.where(blk_hot, 1.0, wide).astype(BF)
            else:
                kw_out[0, gi] = yt.astype(BF)
    y_ref[0:3 * kvd, :] = _dot_nt(wv_ref[...], h)
    for gi in range(g):
        r0 = gi * HEAD_DIM
        _store_chunk_rows(vc_out, gi, _to_token_major(y_ref[r0:r0 + HEAD_DIM, :]), tok_ref)
    _store_lane_tiles(vs_out, y_ref[kvd:2 * kvd, :].astype(BF), slc_tile)
    _store_lane_tiles(vw_out, y_ref[2 * kvd:3 * kvd, :].astype(BF), win_tile)
    g_out[0] = _sigmoid(_dot_nt(wg_ref[...], h))
    z_out[0] = _silu(_dot(h, wz_ref[...])).astype(BF)


def _full(shape):
    nd = len(shape)
    return pl.BlockSpec(shape, lambda i, j, _n=nd: (0,) * _n)


def _nsa_proj(h, cos, sin, w_in, q_gain, k_gain, *, slc_tile, win_tile, pre_gain=None):
    b, s, d = h.shape
    g = NSA_KV_HEADS
    kvd = g * HEAD_DIM
    tm = min(TOK_TILE, s)
    flat = NSA_CMP_STRIDE * HEAD_DIM
    sizes = [MIX_WIDTH] + [kvd] * 6 + [3 * N_HEADS]
    off = np.cumsum([0] + sizes)
    wt = w_in.T.astype(BF)
    wq = wt[off[0]:off[1]]
    wk = jnp.concatenate([wt[off[1]:off[2]], wt[off[3]:off[4]], wt[off[5]:off[6]]], axis=0)
    wv = jnp.concatenate([wt[off[2]:off[3]], wt[off[4]:off[5]], wt[off[6]:off[7]]], axis=0)
    wg = wt[off[7]:off[8]]
    wz = w_in[:, off[8]:].astype(BF)
    qg = q_gain.reshape(HEAD_DIM, 1)
    kg = k_gain.T
    n_t = s // tm
    out_shape = [
        jax.ShapeDtypeStruct((b, MIX_WIDTH, s), BF),
        jax.ShapeDtypeStruct((b, g, s // NSA_CMP_STRIDE, flat), F32),
        jax.ShapeDtypeStruct((b, g, s // NSA_CMP_STRIDE, flat), F32),
        jax.ShapeDtypeStruct((b, g, s, SLC_KEY_LANES), BF),
        jax.ShapeDtypeStruct((b, g, s, KEY_PAD), BF),
        jax.ShapeDtypeStruct((b, s // slc_tile, kvd, slc_tile), BF),
        jax.ShapeDtypeStruct((b, s // win_tile, kvd, win_tile), BF),
        jax.ShapeDtypeStruct((b, 3 * N_HEADS, s), F32),
        jax.ShapeDtypeStruct((b, s, MIX_WIDTH), BF),
    ]
    out_specs = [
        pl.BlockSpec((1, MIX_WIDTH, tm), lambda i, j: (i, 0, j)),
        pl.BlockSpec((1, g, tm // NSA_CMP_STRIDE, flat), lambda i, j: (i, 0, j, 0)),
        pl.BlockSpec((1, g, tm // NSA_CMP_STRIDE, flat), lambda i, j: (i, 0, j, 0)),
        pl.BlockSpec((1, g, tm, SLC_KEY_LANES), lambda i, j: (i, 0, j, 0)),
        pl.BlockSpec((1, g, tm, KEY_PAD), lambda i, j: (i, 0, j, 0)),
        pl.BlockSpec((1, tm // slc_tile, kvd, slc_tile), lambda i, j: (i, j, 0, 0)),
        pl.BlockSpec((1, tm // win_tile, kvd, win_tile), lambda i, j: (i, j, 0, 0)),
        pl.BlockSpec((1, 3 * N_HEADS, tm), lambda i, j: (i, 0, j)),
        pl.BlockSpec((1, tm, MIX_WIDTH), lambda i, j: (i, j, 0)),
    ]
    norm_in = pre_gain is not None
    args = [h] + ([pre_gain.reshape(1, d)] if norm_in else []) + [cos, sin, wq, wk, wv, wg, wz, qg, kg]
    in_specs = [pl.BlockSpec((1, tm, d), lambda i, j: (i, j, 0))] + ([_full((1, d))] if norm_in else []) + [
        pl.BlockSpec((1, HEAD_DIM // 2, tm), lambda i, j: (i, 0, j)),
        pl.BlockSpec((1, HEAD_DIM // 2, tm), lambda i, j: (i, 0, j)),
        _full(wq.shape), _full(wk.shape), _full(wv.shape), _full(wg.shape), _full(wz.shape),
        _full(qg.shape), _full(kg.shape),
    ]
    return pl.pallas_call(
        functools.partial(_nsa_proj_kernel, slc_tile=slc_tile, win_tile=win_tile, norm_in=norm_in),
        grid=(b, n_t), in_specs=in_specs, out_specs=out_specs, out_shape=out_shape,
        scratch_shapes=[pltpu.VMEM((MIX_WIDTH, tm), F32), pltpu.VMEM((tm, KEY_PAD), F32)],
        compiler_params=_params(("parallel", "parallel")),
        name="nsa_proj",
    )(*args)


def _gelu_tanh(x):
    c = np.float32(np.sqrt(2.0 / np.pi))
    return 0.5 * x * (1.0 + jnp.tanh(c * (x + 0.044715 * (x * x * x))))


def _compress_kernel(kc_ref, vc_ref, pos_ref, w1_ref, w2_ref, kg_ref, kcmp_out, vcmp_out):
    for which, (src, dst) in enumerate(((kc_ref, kcmp_out), (vc_ref, vcmp_out))):
        x = src[0, 0]
        n = x.shape[0]
        half = x.shape[1]
        xa = (x + pos_ref[which, 0:1, :]).astype(BF)
        xb = (x + pos_ref[which, 1:2, :]).astype(BF)
        ua = _dot(xa, w1_ref[which, :half, :])
        ub = _dot(xb, w1_ref[which, half:, :])
        row = lax.broadcasted_iota(jnp.int32, (n, 1), 0)
        ub_next = jnp.where(row == n - 1, 0.0, pltpu.roll(ub, n - 1, 0))
        hid = _gelu_tanh(ua + ub_next)
        y = _dot(hid.astype(BF), w2_ref[which])
        if which == 0:
            y = _rms_rows(y, kg_ref[...])
            dst[0, 0] = y.astype(BF)
        else:
            pad = jnp.zeros((n, KEY_PAD - HEAD_DIM), F32)
            dst[0, 0] = jnp.concatenate([y, pad], axis=1).T[:HEAD_DIM].astype(BF)


def _compress(kc, vc, cmp_pos, cmp_w1, cmp_w2, k_gain0):
    b, g, n_chunk, flat = kc.shape
    pos = cmp_pos.reshape(2, 2, flat)
    w1 = cmp_w1.astype(BF)
    w2 = cmp_w2.astype(BF)
    blk = pl.BlockSpec((1, 1, n_chunk, flat), lambda i, j: (i, j, 0, 0))
    return pl.pallas_call(
        _compress_kernel,
        grid=(b, g),
        in_specs=[blk, blk, _full(pos.shape), _full(w1.shape), _full(w2.shape),
                  _full((1, HEAD_DIM))],
        out_specs=[pl.BlockSpec((1, 1, n_chunk, HEAD_DIM), lambda i, j: (i, j, 0, 0)),
                   pl.BlockSpec((1, 1, HEAD_DIM, n_chunk), lambda i, j: (i, j, 0, 0))],
        out_shape=[jax.ShapeDtypeStruct((b, g, n_chunk, HEAD_DIM), BF),
                   jax.ShapeDtypeStruct((b, g, HEAD_DIM, n_chunk), BF)],
        compiler_params=_params(("parallel", "parallel")),
        name="nsa_compress",
    )(kc, vc, pos, w1, w2, k_gain0.reshape(1, HEAD_DIM))


def _gate_row(gate_ref, n_heads, grp=0):
    return jnp.concatenate([gate_ref[0, 0, grp, r:r + 1, :] for r in range(n_heads)], axis=1)


def _cmp_branch(rows, q_ref, kc_ref, vc_ref, ov_ref, gate_ref, o_out, sel_out, *, tq, n_blk, n_grp):
    r_heads = N_HEADS // NSA_KV_HEADS
    nq = r_heads * tq
    width = r_heads * HEAD_DIM
    q0 = pl.program_id(2) * tq
    n_cmp = kc_ref.shape[2] - 1
    t_row = q0 + (lax.broadcasted_iota(jnp.int32, (1, nq), 1) & (tq - 1))
    lo = max(rows - CMP_MASK_ROWS, 0)
    c_col = lo + lax.broadcasted_iota(jnp.int32, (rows - lo, 1), 0)
    valid = (c_col * NSA_CMP_STRIDE + (NSA_CMP_LEN - 1) <= t_row) & (c_col < n_cmp)
    one_row = jnp.where(lax.broadcasted_iota(jnp.int32, (V_ROWS - HEAD_DIM, rows), 0) == 0, 1.0, 0.0).astype(BF)
    n_live = min(n_blk, rows * NSA_CMP_STRIDE // NSA_SLC_LEN + 8)
    t1 = q0 + lax.broadcasted_iota(jnp.int32, (1, tq), 1)
    cur = lax.shift_right_logical(t1, int(np.log2(NSA_SLC_LEN)))
    blk = lax.broadcasted_iota(jnp.int32, (n_live, tq), 0)
    forced = (blk == 0) | (blk == cur) | (blk == cur - 1)
    imps = []
    for gi in range(n_grp):
        q4 = jnp.concatenate([q_ref[0, (gi * r_heads + r) * HEAD_DIM:(gi * r_heads + r + 1) * HEAD_DIM, :]
                              for r in range(r_heads)], axis=1)
        s = _dot(kc_ref[0, gi, 0:rows, :], q4)
        s_new = jnp.where(valid, s[lo:], NEG_INF)
        m = jnp.max(s_new, axis=0, keepdims=True)
        if lo:
            m = jnp.maximum(m, jnp.max(s[:lo], axis=0, keepdims=True))
        e = jnp.where(valid, jnp.exp2(s_new - m), 0.0).astype(BF)
        if lo:
            e = jnp.concatenate([jnp.exp2(s[:lo] - m).astype(BF), e], axis=0)
        lhs = jnp.concatenate([vc_ref[0, gi, :, 0:rows], one_row, ov_ref[0:n_live, 0:rows]], axis=0)
        res = _dot(lhs, e)
        l = res[HEAD_DIM:HEAD_DIM + 1]
        inv = jnp.where(l > 0.0, 1.0 / jnp.where(l > 0.0, l, 1.0), 0.0)
        o = res[:HEAD_DIM] * (inv * _gate_row(gate_ref, r_heads, gi))
        o_rows = jnp.concatenate([o[:, r * tq:(r + 1) * tq] for r in range(r_heads)], axis=0)
        o_out[0, :, gi * width:(gi + 1) * width] = o_rows.T.astype(o_out.dtype)
        w = res[V_ROWS:V_ROWS + n_live] * inv
        imp = w[:, 0:tq]
        for r in range(1, r_heads):
            imp = imp + w[:, r * tq:(r + 1) * tq]
        imps.append(jnp.where(forced, -jnp.inf, jnp.where(blk > cur, NEG_INF, imp)))
    for _ in range(min(NSA_TOPK, n_blk) - 3):
        for gi in range(n_grp):
            best = jnp.max(imps[gi], axis=0, keepdims=True)
            first = jnp.min(jnp.where(imps[gi] == best, blk, n_blk), axis=0, keepdims=True)
            imps[gi] = jnp.where(blk == first, -jnp.inf, imps[gi])
    for gi in range(n_grp):
        sel_out[0, gi, 0:n_live, :] = jnp.where(imps[gi] == -jnp.inf, 0.0, SEL_OFF).astype(BF)
        if n_live < n_blk:
            sel_out[0, gi, n_live:n_blk, :] = jnp.full((n_blk - n_live, tq), SEL_OFF, BF)


def _cmp_select_kernel(q_ref, kc_ref, vc_ref, ov_ref, gate_ref, o_out, sel_out, *, tq, n_blk, n_grp):
    q0 = pl.program_id(2) * tq
    n_chunk = kc_ref.shape[2]
    n_need = jnp.minimum((q0 + tq - NSA_CMP_LEN) // NSA_CMP_STRIDE + 1, n_chunk - 1)
    n_steps = n_chunk // CMP_BLOCK
    need_steps = (n_need + CMP_BLOCK - 1) // CMP_BLOCK
    for k in range(1, n_steps + 1):
        @pl.when(need_steps == k)
        def _(k=k):
            _cmp_branch(k * CMP_BLOCK, q_ref, kc_ref, vc_ref, ov_ref, gate_ref, o_out, sel_out,
                        tq=tq, n_blk=n_blk, n_grp=n_grp)


def _overlap_matrix(s):
    n_chunk = s // NSA_CMP_STRIDE
    n_blk = s // NSA_SLC_LEN
    c0 = np.arange(n_chunk) * NSA_CMP_STRIDE
    c1 = c0 + NSA_CMP_LEN - 1
    b0 = np.arange(n_blk) * NSA_SLC_LEN
    ov = np.minimum(c1[None, :], b0[:, None] + NSA_SLC_LEN - 1) - np.maximum(c0[None, :], b0[:, None]) + 1
    return jnp.asarray(np.clip(ov, 0, None) / NSA_CMP_LEN, BF)


def _cmp_select(qT, k_cmp, v_cmpT, gates5, *, tq):
    b, _, s = qT.shape
    g = NSA_KV_HEADS
    r_heads = N_HEADS // g
    n_chunk = k_cmp.shape[2]
    n_blk = s // NSA_SLC_LEN
    ov = _overlap_matrix(s)
    n_grp = CMP_GROUPS
    rows = n_grp * r_heads * HEAD_DIM
    return pl.pallas_call(
        functools.partial(_cmp_select_kernel, tq=tq, n_blk=n_blk, n_grp=n_grp),
        grid=(b, g // n_grp, s // tq),
        in_specs=[
            pl.BlockSpec((1, rows, tq), lambda i, j, k: (i, j, k)),
            pl.BlockSpec((1, n_grp, n_chunk, HEAD_DIM), lambda i, j, k: (i, j, 0, 0)),
            pl.BlockSpec((1, n_grp, HEAD_DIM, n_chunk), lambda i, j, k: (i, j, 0, 0)),
            pl.BlockSpec((n_blk, n_chunk), lambda i, j, k: (0, 0)),
            pl.BlockSpec((1, 1, n_grp, r_heads, tq), lambda i, j, k: (i, 0, j, 0, k)),
        ],
        out_specs=[pl.BlockSpec((1, tq, rows), lambda i, j, k: (i, k, j)),
                   pl.BlockSpec((1, n_grp, n_blk, tq), lambda i, j, k: (i, j, 0, k))],
        out_shape=[jax.ShapeDtypeStruct((b, s, MIX_WIDTH), BF),
                   jax.ShapeDtypeStruct((b, g, n_blk, s), BF)],
        compiler_params=_params(("parallel", "parallel", "parallel")),
        name="nsa_cmp_select",
    )(qT, k_cmp, v_cmpT, ov, gates5)


def _flash_kernel(*refs, mode, n_par, n_rep, dq, kc, tq, tk, cw, window, n_blk, has_sink, has_gate,
                  fuse_out=None):
    it = iter(refs)
    q_ref, k_ref, v_ref = next(it), next(it), next(it)
    sel_ref = next(it) if n_blk else None
    sink_ref = next(it) if has_sink else None
    gate_ref = next(it) if has_gate else None
    qnext_ref = next(it) if mode == "causal" else None
    selnext_ref = next(it) if (mode == "causal" and n_blk) else None
    if fuse_out is None:
        out_ref = next(it)
    else:
        n_extra, has_next = fuse_out
        extra_refs = [next(it) for _ in range(n_extra)]
        z_ref, x_ref, w_ref = next(it), next(it), next(it)
        ng_ref = next(it) if has_next else None
        x_out = next(it)
        h_out = next(it) if has_next else None
    qs_ref, m_ref, acc_ref, mt_ref = (next(it) for _ in range(4))
    qn_ref = next(it) if mode == "causal" else None
    bias_ref = next(it) if mode == "window" else None
    s_bufs = tuple(it)

    nq = n_rep * tq
    items = [(p, c) for p in range(n_par) for c in range(nq // cw)]
    n_items = len(items)
    grp = pl.program_id(1)
    q0 = pl.program_id(2) * tq
    acc_row = lax.broadcasted_iota(jnp.int32, (V_ROWS, nq), 0)
    for p in range(n_par):
        for r in range(n_rep):
            hd = p * n_rep + r
            qs_ref[p, 0:dq, r * tq:(r + 1) * tq] = q_ref[0, hd * dq:(hd + 1) * dq, :]
        if n_blk:
            qs_ref[p, dq:dq + n_blk, :] = jnp.concatenate([sel_ref[0, 0]] * n_rep, axis=1)
            if dq + n_blk < kc:
                qs_ref[p, dq + n_blk:kc, :] = jnp.zeros((kc - dq - n_blk, nq), BF)
        if has_sink:
            m_ref[p] = jnp.concatenate(
                [jnp.full((1, tq), sink_ref[(grp * n_par + p) * n_rep + r] * LOG2E, F32) for r in range(n_rep)],
                axis=1)
            acc_ref[p] = jnp.where(acc_row == HEAD_DIM, 1.0, 0.0)
        else:
            m_ref[p] = jnp.full((1, nq), M_INIT, F32)
            acc_ref[p] = jnp.zeros((V_ROWS, nq), F32)

    t_row = q0 + (lax.broadcasted_iota(jnp.int32, (1, nq), 1) & (tq - 1))
    one_row = jnp.where(lax.broadcasted_iota(jnp.int32, (V_ROWS - HEAD_DIM, tk), 0) == 0, 1.0, 0.0).astype(BF)

    def stage_a(item, j, key0, slot, causal, bias=None, pen=None):
        p, c = item
        cols = slice(c * cw, (c + 1) * cw)
        s = _dot(k_ref[0, p, j][:, :kc], qs_ref[p, :, cols])
        if causal:
            key = key0 + lax.broadcasted_iota(jnp.int32, (tk, 1), 0)
            s = jnp.where(key <= t_row[:, cols], s, NEG_INF)
        if bias is not None:
            s = s + bias_ref[bias]
        s_bufs[slot][...] = s
        mt = jnp.max(s, axis=0, keepdims=True)
        mt_ref[slot] = mt if pen is None else mt - pen

    def stage_b(item, j, slot, pen=None):
        p, c = item
        cols = slice(c * cw, (c + 1) * cw)
        m_old = m_ref[p, :, cols]
        m_new = jnp.maximum(m_old, mt_ref[slot])
        pr = jnp.exp2(s_bufs[slot][...] - (m_new if pen is None else m_new + pen)).astype(BF)
        alpha = jnp.exp2(m_old - m_new)
        v = jnp.concatenate([v_ref[0, j, p * HEAD_DIM:(p + 1) * HEAD_DIM, :], one_row], axis=0)
        acc_ref[p, :, cols] = alpha * acc_ref[p, :, cols] + _dot(v, pr)
        m_ref[p, :, cols] = m_new

    def finalize():
        outs = []
        for p in range(n_par):
            acc = acc_ref[p]
            o = acc[:HEAD_DIM] * (1.0 / acc[HEAD_DIM:HEAD_DIM + 1])
            if has_gate:
                o = o * _gate_row(gate_ref, n_rep, p)
            outs += [o[:, r * tq:(r + 1) * tq] for r in range(n_rep)]
        o_tok = jnp.concatenate(outs, axis=0).T
        if fuse_out is None:
            out_ref[0] = o_tok.astype(out_ref.dtype)
            return
        for r in extra_refs:
            o_tok = o_tok + r[0].astype(F32)
        y = _dot((o_tok * z_ref[0].astype(F32)).astype(BF), w_ref[...])
        x_new = x_ref[0] + y
        x_out[0] = x_new
        if has_next:
            h_out[0] = _rms_rows(x_new, ng_ref[...]).astype(BF)

    if mode == "causal":
        assert n_items % 2 == 0
        n_full = q0 // tk

        def prefetch_next_tile():
            assert cw == tq
            qn_ref[0:dq, :] = qnext_ref[0, 0:dq, :]
            if n_blk:
                qn_ref[dq:dq + n_blk, :] = selnext_ref[0, 0]
                if dq + n_blk < kc:
                    qn_ref[dq + n_blk:kc, :] = jnp.zeros((kc - dq - n_blk, tq), BF)
            s = _dot(k_ref[0, 0, 0][:, :kc], qn_ref[...])
            key = lax.broadcasted_iota(jnp.int32, (tk, 1), 0)
            s = jnp.where(key <= t_row[:, 0:cw] + tq, s, NEG_INF)
            s_bufs[0][...] = s
            mt_ref[0] = jnp.max(s, axis=0, keepdims=True)

        def step(j, kind, next_kind, last):
            for idx, item in enumerate(items):
                slot = idx % 2
                if idx + 1 < n_items:
                    stage_a(items[idx + 1], j, j * tk, 1 - slot, kind)
                elif not last:
                    stage_a(items[0], j + 1, (j + 1) * tk, 1 - slot, next_kind)
                else:
                    prefetch_next_tile()
                stage_b(item, j, slot)

        @pl.when(pl.program_id(2) == 0)
        def _():
            stage_a(items[0], 0, 0, 0, "causal")

        def body(j, carry):
            step(j, None, None, False)
            return carry

        def body_group(i, carry):
            for u in range(FLASH_UNROLL):
                step(FLASH_UNROLL * i + u, None, None, False)
            return carry

        n_main = jnp.maximum(n_full - 1, 0)
        n_groups = lax.shift_right_logical(n_main, int(np.log2(FLASH_UNROLL)))
        lax.fori_loop(0, n_groups, body_group, 0)
        lax.fori_loop(FLASH_UNROLL * n_groups, n_main, body, 0)

        @pl.when(n_full >= 1)
        def _():
            step(n_full - 1, None, "causal", False)
            step(n_full, "causal", None, True)
            finalize()

        @pl.when(n_full == 0)
        def _():
            step(0, "causal", None, True)
            finalize()
    else:
        w_tiles, q_tiles = window // tk, tq // tk
        assert w_tiles >= q_tiles and cw % tq == 0
        t_loc = lax.broadcasted_iota(jnp.int32, (1, cw), 1) & (tq - 1)
        key_loc = lax.broadcasted_iota(jnp.int32, (tk, 1), 0)
        for i in range(q_tiles):
            bias_ref[i] = jnp.where(i * tk + key_loc > t_loc, 0.0, NEG_INF)
            bias_ref[q_tiles + i] = jnp.where(i * tk + key_loc <= t_loc, 0.0, NEG_INF)
        work = []
        for i in range(w_tiles + q_tiles):
            jv = q0 // tk - w_tiles + i
            bias = i if i < q_tiles else (q_tiles + i - w_tiles if i >= w_tiles else None)
            pen = jnp.where(jv < 0, -NEG_INF, 0.0) if i < w_tiles else None
            work += [(item, jnp.maximum(jv, 0), bias, pen) for item in items]
        n_buf = len(s_bufs)
        ahead = WINDOW_LOOKAHEAD
        for n in range(min(ahead, len(work))):
            stage_a(work[n][0], work[n][1], 0, n % n_buf, False, work[n][2], work[n][3])
        for n, (item, j, _, pen) in enumerate(work):
            if n + ahead < len(work):
                nxt = work[n + ahead]
                stage_a(nxt[0], nxt[1], 0, (n + ahead) % n_buf, False, nxt[2], nxt[3])
            stage_b(item, j, n % n_buf, pen)

        finalize()


def _flash(qT, k_tok, vT_tiles, *, mode, n_par, n_rep, dq, kc, tq, tk, cw, window=None,
           sel_bias=None, sinks=None, gates=None, gate_branch=0, out_proj=None, name="flash"):
    b, _, s = qT.shape
    kh, k_lanes = k_tok.shape[1], k_tok.shape[3]
    n_t = s // tk
    k5 = k_tok.reshape(b, kh, n_t, tk, k_lanes)
    n_grp = kh // n_par
    heads = n_par * n_rep
    nq = n_rep * tq
    n_blk = sel_bias.shape[2] if sel_bias is not None else 0
    has_sink, has_gate = sinks is not None, gates is not None
    args = [qT, k5, vT_tiles]
    in_specs = [
        pl.BlockSpec((1, heads * dq, tq), lambda i, j, k: (i, j, k)),
        pl.BlockSpec((1, n_par, n_t, tk, k_lanes), lambda i, j, k: (i, j, 0, 0, 0)),
        pl.BlockSpec((1, n_t, n_par * HEAD_DIM, tk), lambda i, j, k: (i, 0, j, 0)),
    ]
    if n_blk:
        assert dq + n_blk <= kc
        args.append(sel_bias)
        in_specs.append(pl.BlockSpec((1, 1, n_blk, tq), lambda i, j, k: (i, j, 0, k)))
    if has_sink:
        args.append(sinks.astype(F32))
        in_specs.append(pl.BlockSpec(memory_space=pltpu.SMEM))
    if has_gate:
        args.append(gates)
        in_specs.append(pl.BlockSpec((1, 1, n_par, n_rep, tq),
                                     lambda i, j, k, _br=gate_branch: (i, _br, j, 0, k)))
    n_sbuf = 2 if mode == "causal" else WINDOW_SCORE_BUFS
    scratch = [pltpu.VMEM((n_par, kc, nq), BF), pltpu.VMEM((n_par, 1, nq), F32),
               pltpu.VMEM((n_par, V_ROWS, nq), F32), pltpu.VMEM((n_sbuf, 1, cw), F32)]
    if mode == "causal":
        last_q = s // tq - 1
        args.append(qT)
        in_specs.append(pl.BlockSpec((1, dq, tq), lambda i, j, k: (i, j * heads, jnp.minimum(k + 1, last_q))))
        if n_blk:
            args.append(sel_bias)
            in_specs.append(pl.BlockSpec((1, 1, n_blk, tq), lambda i, j, k: (i, j, 0, jnp.minimum(k + 1, last_q))))
        scratch.append(pltpu.VMEM((kc, tq), BF))
    else:
        scratch.append(pltpu.VMEM((2 * (tq // tk), tk, cw), F32))
    scratch += [pltpu.VMEM((tk, cw), F32)] * n_sbuf
    out_specs = pl.BlockSpec((1, tq, heads * HEAD_DIM), lambda i, j, k: (i, k, j))
    out_shape = jax.ShapeDtypeStruct((b, s, MIX_WIDTH), BF)
    fuse_out = None
    if out_proj is not None:
        assert mode == "window" and heads * HEAD_DIM == MIX_WIDTH and n_grp == 1
        extra, zs, x, w_out, next_gain = out_proj
        tok = pl.BlockSpec((1, tq, D_MODEL), lambda i, j, k: (i, k, 0))
        args += list(extra) + [zs, x, w_out.astype(BF)]
        in_specs += [tok] * (len(extra) + 2) + [pl.BlockSpec(w_out.shape, lambda i, j, k: (0, 0))]
        out_specs, out_shape = [tok], [jax.ShapeDtypeStruct(x.shape, F32)]
        if next_gain is not None:
            args.append(next_gain.reshape(1, D_MODEL))
            in_specs.append(pl.BlockSpec((1, D_MODEL), lambda i, j, k: (0, 0)))
            out_specs.append(tok)
            out_shape.append(jax.ShapeDtypeStruct(x.shape, BF))
        fuse_out = (len(extra), next_gain is not None)
    kern = functools.partial(_flash_kernel, mode=mode, n_par=n_par, n_rep=n_rep, dq=dq, kc=kc, tq=tq, tk=tk,
                             cw=cw, window=window, n_blk=n_blk, has_sink=has_sink, has_gate=has_gate,
                             fuse_out=fuse_out)
    res = pl.pallas_call(
        kern,
        grid=(b, n_grp, s // tq),
        in_specs=in_specs,
        out_specs=out_specs,
        out_shape=out_shape,
        scratch_shapes=scratch,
        compiler_params=_params(("parallel", "parallel", "arbitrary")),
        name=name,
    )(*args)
    if out_proj is None:
        return res
    return (res[0], res[1]) if next_gain is not None else (res[0], None)


def _swa_proj_kernel(h_ref, cos_ref, sin_ref, wq_ref, wk_ref, wv_ref, wz_ref, qg_ref, kg_ref,
                     q_out, k_out, v_out, z_out, y_ref, *, win_tile):
    h = h_ref[0]
    cos, sin = cos_ref[0], sin_ref[0]
    y_ref[...] = _dot_nt(wq_ref[...], h)
    for hd in range(N_HEADS):
        rows = slice(hd * HEAD_DIM, (hd + 1) * HEAD_DIM)
        q_out[0, rows, :] = (_rope_t(_head_rms_t(y_ref[rows, :], qg_ref[...]), cos, sin) * Q_SCALE).astype(BF)
    kvd = SWA_KV_HEADS * HEAD_DIM
    y_ref[0:kvd, :] = _dot_nt(wk_ref[...], h)
    for gi in range(SWA_KV_HEADS):
        rows = slice(gi * HEAD_DIM, (gi + 1) * HEAD_DIM)
        k_out[0, gi] = _to_token_major(_rope_t(_head_rms_t(y_ref[rows, :], kg_ref[...]), cos, sin)).astype(BF)
    _store_lane_tiles(v_out, _dot_nt(wv_ref[...], h).astype(BF), win_tile)
    z_out[0] = _silu(_dot(h, wz_ref[...])).astype(BF)


def _swa_proj(h, cos, sin, w_in, q_gain, k_gain, *, win_tile):
    b, s, d = h.shape
    g = SWA_KV_HEADS
    kvd = g * HEAD_DIM
    tm = min(TOK_TILE, s)
    wt = w_in.T.astype(BF)
    wq, wk, wv = wt[:MIX_WIDTH], wt[MIX_WIDTH:MIX_WIDTH + kvd], wt[MIX_WIDTH + kvd:MIX_WIDTH + 2 * kvd]
    wz = w_in[:, MIX_WIDTH + 2 * kvd:].astype(BF)
    qg, kg = q_gain.reshape(HEAD_DIM, 1), k_gain.reshape(HEAD_DIM, 1)
    wpt = tm // win_tile
    half = HEAD_DIM // 2
    return pl.pallas_call(
        functools.partial(_swa_proj_kernel, win_tile=win_tile),
        grid=(b, s // tm),
        in_specs=[pl.BlockSpec((1, tm, d), lambda i, j: (i, j, 0)),
                  pl.BlockSpec((1, half, tm), lambda i, j: (i, 0, j)),
                  pl.BlockSpec((1, half, tm), lambda i, j: (i, 0, j)),
                  _full(wq.shape), _full(wk.shape), _full(wv.shape), _full(wz.shape),
                  _full(qg.shape), _full(kg.shape)],
        out_specs=[pl.BlockSpec((1, MIX_WIDTH, tm), lambda i, j: (i, 0, j)),
                   pl.BlockSpec((1, g, tm, KEY_PAD), lambda i, j: (i, 0, j, 0)),
                   pl.BlockSpec((1, wpt, kvd, win_tile), lambda i, j: (i, j, 0, 0)),
                   pl.BlockSpec((1, tm, MIX_WIDTH), lambda i, j: (i, j, 0))],
        out_shape=[jax.ShapeDtypeStruct((b, MIX_WIDTH, s), BF),
                   jax.ShapeDtypeStruct((b, g, s, KEY_PAD), BF),
                   jax.ShapeDtypeStruct((b, s // win_tile, kvd, win_tile), BF),
                   jax.ShapeDtypeStruct((b, s, MIX_WIDTH), BF)],
        scratch_shapes=[pltpu.VMEM((MIX_WIDTH, tm), F32)],
        compiler_params=_params(("parallel", "parallel")),
        name="swa_proj",
    )(h, cos, sin, wq, wk, wv, wz, qg, kg)


def _split3(x):
    hi = x.astype(BF)
    r1 = x - hi.astype(F32)
    mid = r1.astype(BF)
    lo = (r1 - mid.astype(F32)).astype(BF)
    return hi, mid, lo


def _fox_proj_kernel(h_ref, wf_ref, bias_ref, tri_ref, wq_ref, wk_ref, wv_ref, wz_ref, qg_ref, kg_ref,
                     q_out, k_out, v_out, z_out, yq_ref, yk_ref, cum_ref, carry_ref, *, tile):
    @pl.when(pl.program_id(1) == 0)
    def _():
        carry_ref[...] = jnp.zeros_like(carry_ref)

    h = h_ref[0]
    tm = h.shape[0]
    x = _dot_nt(wf_ref[...], h) + bias_ref[...]
    logf = jnp.minimum(x, 0.0) - jnp.log(1.0 + jnp.exp(-jnp.abs(x)))
    tri = tri_ref[...]
    hi, mid, lo = _split3(logf)
    cum = (_dot(hi, tri) + _dot(mid, tri)) + _dot(lo, tri) + carry_ref[:, 0:1]
    carry_ref[...] = jnp.broadcast_to(cum[:, -1:], carry_ref.shape)
    cum_ref[...] = cum * LOG2E

    yq_ref[...] = _dot_nt(wq_ref[...], h)
    yk_ref[...] = _dot_nt(wk_ref[...], h)
    row = lax.broadcasted_iota(jnp.int32, (8, tm), 0)
    zeros = jnp.zeros((KEY_PAD - HEAD_DIM - 16, tm), F32)
    for hd in range(N_HEADS):
        rows = slice(hd * HEAD_DIM, (hd + 1) * HEAD_DIM)
        c_hi, c_mid, c_lo = (c.astype(F32) for c in _split3(cum_ref[hd:hd + 1, :]))
        c3 = jnp.where(row == 0, c_hi, jnp.where(row == 1, c_mid, jnp.where(row == 2, c_lo, 0.0)))
        one3 = jnp.where(row < 3, 1.0, 0.0)
        q = _head_rms_t(yq_ref[rows, :], qg_ref[...]) * Q_SCALE
        q_out[0, hd] = jnp.concatenate([q, one3, c3, zeros], axis=0).astype(BF)
        k = _head_rms_t(yk_ref[rows, :], kg_ref[...])
        k_out[0, hd] = jnp.concatenate([k, -c3, one3, zeros], axis=0).T.astype(BF)
    _store_lane_tiles(v_out, _dot_nt(wv_ref[...], h).astype(BF), tile)
    z_out[0] = _silu(_dot(h, wz_ref[...])).astype(BF)


def _fox_proj(h, w_in, forget_bias, q_gain, k_gain, *, tile):
    b, s, d = h.shape
    tm = min(TOK_TILE, s)
    wt = w_in.T.astype(BF)
    wq, wk, wv = wt[:MIX_WIDTH], wt[MIX_WIDTH:2 * MIX_WIDTH], wt[2 * MIX_WIDTH:3 * MIX_WIDTH]
    wf = wt[3 * MIX_WIDTH:3 * MIX_WIDTH + N_HEADS]
    wz = w_in[:, 3 * MIX_WIDTH + N_HEADS:].astype(BF)
    qg, kg = q_gain.reshape(HEAD_DIM, 1), k_gain.reshape(HEAD_DIM, 1)
    tri = jnp.asarray(np.arange(tm)[:, None] <= np.arange(tm)[None, :], BF)
    return pl.pallas_call(
        functools.partial(_fox_proj_kernel, tile=tile),
        grid=(b, s // tm),
        in_specs=[pl.BlockSpec((1, tm, d), lambda i, j: (i, j, 0)),
                  _full(wf.shape), _full((N_HEADS, 1)), _full(tri.shape),
                  _full(wq.shape), _full(wk.shape), _full(wv.shape), _full(wz.shape),
                  _full(qg.shape), _full(kg.shape)],
        out_specs=[pl.BlockSpec((1, N_HEADS, KEY_PAD, tm), lambda i, j: (i, 0, 0, j)),
                   pl.BlockSpec((1, N_HEADS, tm, KEY_PAD), lambda i, j: (i, 0, j, 0)),
                   pl.BlockSpec((1, tm // tile, MIX_WIDTH, tile), lambda i, j: (i, j, 0, 0)),
                   pl.BlockSpec((1, tm, MIX_WIDTH), lambda i, j: (i, j, 0))],
        out_shape=[jax.ShapeDtypeStruct((b, N_HEADS, KEY_PAD, s), BF),
                   jax.ShapeDtypeStruct((b, N_HEADS, s, KEY_PAD), BF),
                   jax.ShapeDtypeStruct((b, s // tile, MIX_WIDTH, tile), BF),
                   jax.ShapeDtypeStruct((b, s, MIX_WIDTH), BF)],
        scratch_shapes=[pltpu.VMEM((MIX_WIDTH, tm), F32), pltpu.VMEM((MIX_WIDTH, tm), F32),
                        pltpu.VMEM((N_HEADS, tm), F32), pltpu.VMEM((N_HEADS, LANES), F32)],
        compiler_params=_params(("parallel", "arbitrary")),
        name="fox_proj",
    )(h, wf, forget_bias.reshape(N_HEADS, 1).astype(F32), tri, wq, wk, wv, wz, qg, kg)


def _out_proj_kernel(*refs, n_o, has_next):
    o_refs = refs[:n_o]
    z_ref, x_ref, w_ref = refs[n_o:n_o + 3]
    rest = refs[n_o + 3:]
    if has_next:
        g_ref, x_out, h_out = rest
    else:
        (x_out,) = rest
    o = o_refs[0][0].astype(F32)
    for r in o_refs[1:]:
        o = o + r[0].astype(F32)
    y = _dot((o * z_ref[0].astype(F32)).astype(BF), w_ref[...])
    x_new = x_ref[0] + y
    x_out[0] = x_new
    if has_next:
        h_out[0] = _rms_rows(x_new, g_ref[...]).astype(BF)


def _out_proj(o_list, zs, x, w_out, next_gain):
    b, s, d = x.shape
    tm = min(TOK_TILE, s)
    has_next = next_gain is not None
    blk = pl.BlockSpec((1, tm, d), lambda i, j: (i, j, 0))
    args = list(o_list) + [zs, x, w_out.astype(BF)]
    in_specs = [blk] * (len(o_list) + 2) + [_full(w_out.shape)]
    out_shape = [jax.ShapeDtypeStruct((b, s, d), F32)]
    out_specs = [blk]
    if has_next:
        args.append(next_gain.reshape(1, d))
        in_specs.append(_full((1, d)))
        out_shape.append(jax.ShapeDtypeStruct((b, s, d), BF))
        out_specs.append(blk)
    res = pl.pallas_call(
        functools.partial(_out_proj_kernel, n_o=len(o_list), has_next=has_next),
        grid=(b, s // tm), in_specs=in_specs, out_specs=out_specs, out_shape=out_shape,
        compiler_params=_params(("parallel", "parallel")),
        name="out_proj",
    )(*args)
    return (res[0], res[1]) if has_next else (res[0], None)


ITEM_LANES = 512
CMP_Q_TILE = 512
CMP_GROUPS = 2
NSA_SLC_Q_TILE = 512
NSA_SLC_TILE = 512
NSA_WIN_Q_TILE = 256
NSA_WIN_TILE = 256
SWA_TILE = 128
FOX_Q_TILE = 512
FOX_K_TILE = 512
FOX_HEADS_PER_STEP = 4


def _nsa_mixer(x, h, cos, sin, w_in, q_gain, k_gain, cmp_pos, cmp_w1, cmp_w2, w_out, next_gain, pre_gain):
    b, s, _ = x.shape
    g = NSA_KV_HEADS
    r = N_HEADS // g
    qT, kc_tok, vc_tok, ks, kw, vsT, vwT, gates, zs = _nsa_proj(
        x if h is None else h, cos, sin, w_in, q_gain, k_gain, slc_tile=NSA_SLC_TILE, win_tile=NSA_WIN_TILE,
        pre_gain=pre_gain if h is None else None)
    k_cmp, v_cmpT = _compress(kc_tok, vc_tok, cmp_pos, cmp_w1, cmp_w2, k_gain[0])
    gates5 = gates.reshape(b, 3, g, r, s)
    o_cmp, sel_bias = _cmp_select(qT, k_cmp, v_cmpT, gates5, tq=CMP_Q_TILE)
    o_slc = _flash(qT, ks, vsT, mode="causal", n_par=1, n_rep=r, dq=HEAD_DIM, kc=SLC_KEY_LANES,
                   tq=NSA_SLC_Q_TILE, tk=NSA_SLC_TILE, cw=ITEM_LANES, sel_bias=sel_bias,
                   gates=gates5, gate_branch=1, name="nsa_selected")
    return _flash(qT, kw, vwT, mode="window", n_par=g, n_rep=r, dq=HEAD_DIM, kc=HEAD_DIM,
                  tq=NSA_WIN_Q_TILE, tk=NSA_WIN_TILE, cw=ITEM_LANES, window=NSA_WINDOW,
                  gates=gates5, gate_branch=2, out_proj=([o_cmp, o_slc], zs, x, w_out, next_gain),
                  name="nsa_window_out")


def _swa_mixer(x, h, cos, sin, w_in, q_gain, k_gain, sinks, w_out, next_gain):
    r = N_HEADS // SWA_KV_HEADS
    qT, k, vT, zs = _swa_proj(h, cos, sin, w_in, q_gain, k_gain, win_tile=SWA_TILE)
    return _flash(qT, k, vT, mode="window", n_par=SWA_KV_HEADS, n_rep=r, dq=HEAD_DIM, kc=HEAD_DIM, tq=SWA_TILE,
                  tk=SWA_TILE, cw=r * SWA_TILE, window=SWA_WINDOW, sinks=sinks,
                  out_proj=([], zs, x, w_out, next_gain), name="swa_window_out")


def _fox_mixer(x, h, w_in, forget_bias, q_gain, k_gain, w_out, next_gain):
    b, s, _ = h.shape
    qT, k, vT, zs = _fox_proj(h, w_in, forget_bias, q_gain, k_gain, tile=FOX_K_TILE)
    o = _flash(qT.reshape(b, N_HEADS * KEY_PAD, s), k, vT, mode="causal", n_par=FOX_HEADS_PER_STEP, n_rep=1, dq=KEY_PAD,
               kc=KEY_PAD, tq=FOX_Q_TILE, tk=FOX_K_TILE, cw=ITEM_LANES, name="fox_attention")
    return _out_proj([o], zs, x, w_out, next_gain)


def kernel(x, positions, norm_gains, a_w_in, a_q_gain, a_k_gain, a_cmp_pos, a_cmp_w1, a_cmp_w2, a_w_out,
           b_w_in, b_q_gain, b_k_gain, b_sinks, b_w_out,
           c_w_in, c_forget_bias, c_q_gain, c_k_gain, c_w_out):
    depth = norm_gains.shape[0]
    cos, sin = _rope_tables(positions)
    h = None
    for i in range(depth):
        j, mixer = divmod(i, 3)
        next_gain = norm_gains[i + 1] if i + 1 < depth else None
        if mixer == 0:
            x, h = _nsa_mixer(x, h, cos, sin, a_w_in[j], a_q_gain[j], a_k_gain[j],
                              a_cmp_pos[j], a_cmp_w1[j], a_cmp_w2[j], a_w_out[j], next_gain, norm_gains[i])
        elif mixer == 1:
            x, h = _swa_mixer(x, h, cos, sin, b_w_in[j], b_q_gain[j], b_k_gain[j], b_sinks[j],
                              b_w_out[j], next_gain)
        else:
            x, h = _fox_mixer(x, h, c_w_in[j], c_forget_bias[j], c_q_gain[j], c_k_gain[j],
                              c_w_out[j], next_gain)
    return x
```

```python
import functools

import jax
import jax.numpy as jnp
import numpy as np
from jax import lax
from jax.experimental import pallas as pl
from jax.experimental.pallas import tpu as pltpu

D_MODEL = 1024
HEAD_DIM = 64
N_HEADS = 16
MIX_WIDTH = N_HEADS * HEAD_DIM
ROPE_THETA = 10000.0
EPS = 1e-6
SCALE = HEAD_DIM ** -0.5
NEG_INF = -1e30
M_INIT = -1e29
SEL_OFF = -(2.0 ** 100)

NSA_KV_HEADS = 4
NSA_CMP_LEN = 32
NSA_CMP_STRIDE = 16
NSA_SLC_LEN = 64
NSA_TOPK = 16
NSA_WINDOW = 512
SWA_KV_HEADS = 2
SWA_WINDOW = 128

LOG2E = float(np.log2(np.e))
Q_SCALE = SCALE * LOG2E

LANES = 128
KEY_PAD = 128
SLC_KEY_LANES = 256
V_ROWS = 80
FLASH_UNROLL = 4
WINDOW_LOOKAHEAD = 2
WINDOW_SCORE_BUFS = 4
CMP_BLOCK = 128
CMP_MASK_ROWS = 2 * CMP_BLOCK
VMEM_LIMIT = 56 * 1024 * 1024

TOK_TILE = 512
NT_DIMS = (((1,), (1,)), ((), ()))

BF = jnp.bfloat16
F32 = jnp.float32


def _params(sem):
    return pltpu.CompilerParams(dimension_semantics=sem, vmem_limit_bytes=VMEM_LIMIT)


def _dot(a, b):
    return jnp.dot(a, b, preferred_element_type=F32)


def _dot_nt(a, b):
    return lax.dot_general(a, b, NT_DIMS, preferred_element_type=F32)


def _rope_tab_kernel(pos_ref, invf_ref, cos_ref, sin_ref):
    ang = invf_ref[...] * pos_ref[0].astype(F32)
    cos_ref[0] = jnp.cos(ang)
    sin_ref[0] = jnp.sin(ang)


def _rope_tables(positions):
    b, s = positions.shape
    half = HEAD_DIM // 2
    inv_freq = ROPE_THETA ** (-jnp.arange(half, dtype=F32) * 2.0 / HEAD_DIM)
    tm = min(TOK_TILE, s)
    out = jax.ShapeDtypeStruct((b, half, s), F32)
    return pl.pallas_call(
        _rope_tab_kernel,
        grid=(b, s // tm),
        in_specs=[pl.BlockSpec((1, 1, tm), lambda i, j: (i, 0, j)),
                  pl.BlockSpec((half, 1), lambda i, j: (0, 0))],
        out_specs=[pl.BlockSpec((1, half, tm), lambda i, j: (i, 0, j))] * 2,
        out_shape=[out, out],
        compiler_params=_params(("parallel", "parallel")),
        name="rope_tables",
    )(positions.reshape(b, 1, s), inv_freq.reshape(half, 1))


def _rms_rows(x, gain_row):
    y = x * lax.rsqrt(jnp.mean(x * x, axis=-1, keepdims=True) + EPS)
    return y * gain_row


def _head_rms_t(y, gain_col):
    ms = jnp.mean(y * y, axis=0, keepdims=True)
    return (y * lax.rsqrt(ms + EPS)) * gain_col


def _rope_t(y, cos, sin):
    half = HEAD_DIM // 2
    x1, x2 = y[:half], y[half:]
    return jnp.concatenate([x1 * cos - x2 * sin, x2 * cos + x1 * sin], axis=0)


def _to_token_major(y):
    pad = jnp.zeros((KEY_PAD - y.shape[0], y.shape[1]), y.dtype)
    return jnp.concatenate([y, pad], axis=0).T


def _silu(z):
    return z * (1.0 / (1.0 + jnp.exp(-z)))


def _sigmoid(z):
    return 1.0 / (1.0 + jnp.exp(-z))


def _store_lane_tiles(out_ref, y, tile):
    for c in range(y.shape[1] // tile):
        out_ref[0, c] = y[:, c * tile:(c + 1) * tile]


def _store_chunk_rows(out_ref, gi, yt, tok_ref):
    tok_ref[...] = yt
    n = yt.shape[0] // NSA_CMP_STRIDE
    for m in range(NSA_CMP_STRIDE // 2):
        even = tok_ref[pl.ds(2 * m, n, stride=NSA_CMP_STRIDE), :]
        odd = tok_ref[pl.ds(2 * m + 1, n, stride=NSA_CMP_STRIDE), :]
        out_ref[0, gi, :, KEY_PAD * m:KEY_PAD * (m + 1)] = even + pltpu.roll(odd, HEAD_DIM, 1)


def _nsa_proj_kernel(*refs, slc_tile, win_tile, norm_in):
    it = iter(refs)
    h_ref = next(it)
    ng_ref = next(it) if norm_in else None
    (cos_ref, sin_ref, wq_ref, wk_ref, wv_ref, wg_ref, wz_ref, qg_ref, kg_ref,
     q_out, kc_out, vc_out, ks_out, kw_out, vs_out, vw_out, g_out, z_out, y_ref, tok_ref) = it
    h = _rms_rows(h_ref[0], ng_ref[...]).astype(BF) if norm_in else h_ref[0]
    cos, sin = cos_ref[0], sin_ref[0]
    g = NSA_KV_HEADS
    kvd = g * HEAD_DIM
    y_ref[...] = _dot_nt(wq_ref[...], h)
    for hd in range(N_HEADS):
        rows = slice(hd * HEAD_DIM, (hd + 1) * HEAD_DIM)
        y = _rope_t(_head_rms_t(y_ref[rows, :], qg_ref[...]), cos, sin) * Q_SCALE
        q_out[0, rows, :] = y.astype(BF)
    y_ref[0:3 * kvd, :] = _dot_nt(wk_ref[...], h)
    tm = h.shape[0]
    tok = pl.program_id(1) * tm + lax.broadcasted_iota(jnp.int32, (tm, 1), 0)
    blk_lane = HEAD_DIM + lax.shift_right_logical(tok, int(np.log2(NSA_SLC_LEN)))
    blk_hot = lax.broadcasted_iota(jnp.int32, (1, SLC_KEY_LANES), 1) == blk_lane
    for kind in range(3):
        for gi in range(g):
            r0 = (kind * g + gi) * HEAD_DIM
            y = y_ref[r0:r0 + HEAD_DIM, :]
            if kind > 0:
                y = _head_rms_t(y, kg_ref[:, kind:kind + 1])
            yt = _to_token_major(_rope_t(y, cos, sin))
            if kind == 0:
                _store_chunk_rows(kc_out, gi, yt, tok_ref)
            elif kind == 1:
                wide = jnp.concatenate([yt, jnp.zeros((tm, SLC_KEY_LANES - KEY_PAD), F32)], axis=1)
                ks_out[0, gi] = jnp.where(blk_hot, 1.0, wide).astype(BF)
            else:
                kw_out[0, gi] = yt.astype(BF)
    y_ref[0:3 * kvd, :] = _dot_nt(wv_ref[...], h)
    for gi in range(g):
        r0 = gi * HEAD_DIM
        _store_chunk_rows(vc_out, gi, _to_token_major(y_ref[r0:r0 + HEAD_DIM, :]), tok_ref)
    _store_lane_tiles(vs_out, y_ref[kvd:2 * kvd, :].astype(BF), slc_tile)
    _store_lane_tiles(vw_out, y_ref[2 * kvd:3 * kvd, :].astype(BF), win_tile)
    g_out[0] = _sigmoid(_dot_nt(wg_ref[...], h))
    z_out[0] = _silu(_dot(h, wz_ref[...])).astype(BF)


def _full(shape):
    nd = len(shape)
    return pl.BlockSpec(shape, lambda i, j, _n=nd: (0,) * _n)


def _nsa_proj(h, cos, sin, w_in, q_gain, k_gain, *, slc_tile, win_tile, pre_gain=None):
    b, s, d = h.shape
    g = NSA_KV_HEADS
    kvd = g * HEAD_DIM
    tm = min(TOK_TILE, s)
    flat = NSA_CMP_STRIDE * HEAD_DIM
    sizes = [MIX_WIDTH] + [kvd] * 6 + [3 * N_HEADS]
    off = np.cumsum([0] + sizes)
    wt = w_in.T.astype(BF)
    wq = wt[off[0]:off[1]]
    wk = jnp.concatenate([wt[off[1]:off[2]], wt[off[3]:off[4]], wt[off[5]:off[6]]], axis=0)
    wv = jnp.concatenate([wt[off[2]:off[3]], wt[off[4]:off[5]], wt[off[6]:off[7]]], axis=0)
    wg = wt[off[7]:off[8]]
    wz = w_in[:, off[8]:].astype(BF)
    qg = q_gain.reshape(HEAD_DIM, 1)
    kg = k_gain.T
    n_t = s // tm
    out_shape = [
        jax.ShapeDtypeStruct((b, MIX_WIDTH, s), BF),
        jax.ShapeDtypeStruct((b, g, s // NSA_CMP_STRIDE, flat), F32),
        jax.ShapeDtypeStruct((b, g, s // NSA_CMP_STRIDE, flat), F32),
        jax.ShapeDtypeStruct((b, g, s, SLC_KEY_LANES), BF),
        jax.ShapeDtypeStruct((b, g, s, KEY_PAD), BF),
        jax.ShapeDtypeStruct((b, s // slc_tile, kvd, slc_tile), BF),
        jax.ShapeDtypeStruct((b, s // win_tile, kvd, win_tile), BF),
        jax.ShapeDtypeStruct((b, 3 * N_HEADS, s), F32),
        jax.ShapeDtypeStruct((b, s, MIX_WIDTH), BF),
    ]
    out_specs = [
        pl.BlockSpec((1, MIX_WIDTH, tm), lambda i, j: (i, 0, j)),
        pl.BlockSpec((1, g, tm // NSA_CMP_STRIDE, flat), lambda i, j: (i, 0, j, 0)),
        pl.BlockSpec((1, g, tm // NSA_CMP_STRIDE, flat), lambda i, j: (i, 0, j, 0)),
        pl.BlockSpec((1, g, tm, SLC_KEY_LANES), lambda i, j: (i, 0, j, 0)),
        pl.BlockSpec((1, g, tm, KEY_PAD), lambda i, j: (i, 0, j, 0)),
        pl.BlockSpec((1, tm // slc_tile, kvd, slc_tile), lambda i, j: (i, j, 0, 0)),
        pl.BlockSpec((1, tm // win_tile, kvd, win_tile), lambda i, j: (i, j, 0, 0)),
        pl.BlockSpec((1, 3 * N_HEADS, tm), lambda i, j: (i, 0, j)),
        pl.BlockSpec((1, tm, MIX_WIDTH), lambda i, j: (i, j, 0)),
    ]
    norm_in = pre_gain is not None
    args = [h] + ([pre_gain.reshape(1, d)] if norm_in else []) + [cos, sin, wq, wk, wv, wg, wz, qg, kg]
    in_specs = [pl.BlockSpec((1, tm, d), lambda i, j: (i, j, 0))] + ([_full((1, d))] if norm_in else []) + [
        pl.BlockSpec((1, HEAD_DIM // 2, tm), lambda i, j: (i, 0, j)),
        pl.BlockSpec((1, HEAD_DIM // 2, tm), lambda i, j: (i, 0, j)),
        _full(wq.shape), _full(wk.shape), _full(wv.shape), _full(wg.shape), _full(wz.shape),
        _full(qg.shape), _full(kg.shape),
    ]
    return pl.pallas_call(
        functools.partial(_nsa_proj_kernel, slc_tile=slc_tile, win_tile=win_tile, norm_in=norm_in),
        grid=(b, n_t), in_specs=in_specs, out_specs=out_specs, out_shape=out_shape,
        scratch_shapes=[pltpu.VMEM((MIX_WIDTH, tm), F32), pltpu.VMEM((tm, KEY_PAD), F32)],
        compiler_params=_params(("parallel", "parallel")),
        name="nsa_proj",
    )(*args)


def _gelu_tanh(x):
    c = np.float32(np.sqrt(2.0 / np.pi))
    return 0.5 * x * (1.0 + jnp.tanh(c * (x + 0.044715 * (x * x * x))))


def _compress_kernel(kc_ref, vc_ref, pos_ref, w1_ref, w2_ref, kg_ref, kcmp_out, vcmp_out):
    for which, (src, dst) in enumerate(((kc_ref, kcmp_out), (vc_ref, vcmp_out))):
        x = src[0, 0]
        n = x.shape[0]
        half = x.shape[1]
        xa = (x + pos_ref[which, 0:1, :]).astype(BF)
        xb = (x + pos_ref[which, 1:2, :]).astype(BF)
        ua = _dot(xa, w1_ref[which, :half, :])
        ub = _dot(xb, w1_ref[which, half:, :])
        row = lax.broadcasted_iota(jnp.int32, (n, 1), 0)
        ub_next = jnp.where(row == n - 1, 0.0, pltpu.roll(ub, n - 1, 0))
        hid = _gelu_tanh(ua + ub_next)
        y = _dot(hid.astype(BF), w2_ref[which])
        if which == 0:
            y = _rms_rows(y, kg_ref[...])
            dst[0, 0] = y.astype(BF)
        else:
            pad = jnp.zeros((n, KEY_PAD - HEAD_DIM), F32)
            dst[0, 0] = jnp.concatenate([y, pad], axis=1).T[:HEAD_DIM].astype(BF)


def _compress(kc, vc, cmp_pos, cmp_w1, cmp_w2, k_gain0):
    b, g, n_chunk, flat = kc.shape
    pos = cmp_pos.reshape(2, 2, flat)
    w1 = cmp_w1.astype(BF)
    w2 = cmp_w2.astype(BF)
    blk = pl.BlockSpec((1, 1, n_chunk, flat), lambda i, j: (i, j, 0, 0))
    return pl.pallas_call(
        _compress_kernel,
        grid=(b, g),
        in_specs=[blk, blk, _full(pos.shape), _full(w1.shape), _full(w2.shape),
                  _full((1, HEAD_DIM))],
        out_specs=[pl.BlockSpec((1, 1, n_chunk, HEAD_DIM), lambda i, j: (i, j, 0, 0)),
                   pl.BlockSpec((1, 1, HEAD_DIM, n_chunk), lambda i, j: (i, j, 0, 0))],
        out_shape=[jax.ShapeDtypeStruct((b, g, n_chunk, HEAD_DIM), BF),
                   jax.ShapeDtypeStruct((b, g, HEAD_DIM, n_chunk), BF)],
        compiler_params=_params(("parallel", "parallel")),
        name="nsa_compress",
    )(kc, vc, pos, w1, w2, k_gain0.reshape(1, HEAD_DIM))


def _gate_row(gate_ref, n_heads, grp=0):
    return jnp.concatenate([gate_ref[0, 0, grp, r:r + 1, :] for r in range(n_heads)], axis=1)


def _cmp_branch(rows, q_ref, kc_ref, vc_ref, ov_ref, gate_ref, o_out, sel_out, *, tq, n_blk, n_grp):
    r_heads = N_HEADS // NSA_KV_HEADS
    nq = r_heads * tq
    width = r_heads * HEAD_DIM
    q0 = pl.program_id(2) * tq
    n_cmp = kc_ref.shape[2] - 1
    t_row = q0 + (lax.broadcasted_iota(jnp.int32, (1, nq), 1) & (tq - 1))
    lo = max(rows - CMP_MASK_ROWS, 0)
    c_col = lo + lax.broadcasted_iota(jnp.int32, (rows - lo, 1), 0)
    valid = (c_col * NSA_CMP_STRIDE + (NSA_CMP_LEN - 1) <= t_row) & (c_col < n_cmp)
    one_row = jnp.where(lax.broadcasted_iota(jnp.int32, (V_ROWS - HEAD_DIM, rows), 0) == 0, 1.0, 0.0).astype(BF)
    n_live = min(n_blk, rows * NSA_CMP_STRIDE // NSA_SLC_LEN + 8)
    t1 = q0 + lax.broadcasted_iota(jnp.int32, (1, tq), 1)
    cur = lax.shift_right_logical(t1, int(np.log2(NSA_SLC_LEN)))
    blk = lax.broadcasted_iota(jnp.int32, (n_live, tq), 0)
    forced = (blk == 0) | (blk == cur) | (blk == cur - 1)
    imps = []
    for gi in range(n_grp):
        q4 = jnp.concatenate([q_ref[0, (gi * r_heads + r) * HEAD_DIM:(gi * r_heads + r + 1) * HEAD_DIM, :]
                              for r in range(r_heads)], axis=1)
        s = _dot(kc_ref[0, gi, 0:rows, :], q4)
        s_new = jnp.where(valid, s[lo:], NEG_INF)
        m = jnp.max(s_new, axis=0, keepdims=True)
        if lo:
            m = jnp.maximum(m, jnp.max(s[:lo], axis=0, keepdims=True))
        e = jnp.where(valid, jnp.exp2(s_new - m), 0.0).astype(BF)
        if lo:
            e = jnp.concatenate([jnp.exp2(s[:lo] - m).astype(BF), e], axis=0)
        lhs = jnp.concatenate([vc_ref[0, gi, :, 0:rows], one_row, ov_ref[0:n_live, 0:rows]], axis=0)
        res = _dot(lhs, e)
        l = res[HEAD_DIM:HEAD_DIM + 1]
        inv = jnp.where(l > 0.0, 1.0 / jnp.where(l > 0.0, l, 1.0), 0.0)
        o = res[:HEAD_DIM] * (inv * _gate_row(gate_ref, r_heads, gi))
        o_rows = jnp.concatenate([o[:, r * tq:(r + 1) * tq] for r in range(r_heads)], axis=0)
        o_out[0, :, gi * width:(gi + 1) * width] = o_rows.T.astype(o_out.dtype)
        w = res[V_ROWS:V_ROWS + n_live] * inv
        imp = w[:, 0:tq]
        for r in range(1, r_heads):
            imp = imp + w[:, r * tq:(r + 1) * tq]
        imps.append(jnp.where(forced, -jnp.inf, jnp.where(blk > cur, NEG_INF, imp)))
    for _ in range(min(NSA_TOPK, n_blk) - 3):
        for gi in range(n_grp):
            best = jnp.max(imps[gi], axis=0, keepdims=True)
            first = jnp.min(jnp.where(imps[gi] == best, blk, n_blk), axis=0, keepdims=True)
            imps[gi] = jnp.where(blk == first, -jnp.inf, imps[gi])
    for gi in range(n_grp):
        sel_out[0, gi, 0:n_live, :] = jnp.where(imps[gi] == -jnp.inf, 0.0, SEL_OFF).astype(BF)
        if n_live < n_blk:
            sel_out[0, gi, n_live:n_blk, :] = jnp.full((n_blk - n_live, tq), SEL_OFF, BF)


def _cmp_select_kernel(q_ref, kc_ref, vc_ref, ov_ref, gate_ref, o_out, sel_out, *, tq, n_blk, n_grp):
    q0 = pl.program_id(2) * tq
    n_chunk = kc_ref.shape[2]
    n_need = jnp.minimum((q0 + tq - NSA_CMP_LEN) // NSA_CMP_STRIDE + 1, n_chunk - 1)
    n_steps = n_chunk // CMP_BLOCK
    need_steps = (n_need + CMP_BLOCK - 1) // CMP_BLOCK
    for k in range(1, n_steps + 1):
        @pl.when(need_steps == k)
        def _(k=k):
            _cmp_branch(k * CMP_BLOCK, q_ref, kc_ref, vc_ref, ov_ref, gate_ref, o_out, sel_out,
                        tq=tq, n_blk=n_blk, n_grp=n_grp)


def _overlap_matrix(s):
    n_chunk = s // NSA_CMP_STRIDE
    n_blk = s // NSA_SLC_LEN
    c0 = np.arange(n_chunk) * NSA_CMP_STRIDE
    c1 = c0 + NSA_CMP_LEN - 1
    b0 = np.arange(n_blk) * NSA_SLC_LEN
    ov = np.minimum(c1[None, :], b0[:, None] + NSA_SLC_LEN - 1) - np.maximum(c0[None, :], b0[:, None]) + 1
    return jnp.asarray(np.clip(ov, 0, None) / NSA_CMP_LEN, BF)


def _cmp_select(qT, k_cmp, v_cmpT, gates5, *, tq):
    b, _, s = qT.shape
    g = NSA_KV_HEADS
    r_heads = N_HEADS // g
    n_chunk = k_cmp.shape[2]
    n_blk = s // NSA_SLC_LEN
    ov = _overlap_matrix(s)
    n_grp = CMP_GROUPS
    rows = n_grp * r_heads * HEAD_DIM
    return pl.pallas_call(
        functools.partial(_cmp_select_kernel, tq=tq, n_blk=n_blk, n_grp=n_grp),
        grid=(b, g // n_grp, s // tq),
        in_specs=[
            pl.BlockSpec((1, rows, tq), lambda i, j, k: (i, j, k)),
            pl.BlockSpec((1, n_grp, n_chunk, HEAD_DIM), lambda i, j, k: (i, j, 0, 0)),
            pl.BlockSpec((1, n_grp, HEAD_DIM, n_chunk), lambda i, j, k: (i, j, 0, 0)),
            pl.BlockSpec((n_blk, n_chunk), lambda i, j, k: (0, 0)),
            pl.BlockSpec((1, 1, n_grp, r_heads, tq), lambda i, j, k: (i, 0, j, 0, k)),
        ],
        out_specs=[pl.BlockSpec((1, tq, rows), lambda i, j, k: (i, k, j)),
                   pl.BlockSpec((1, n_grp, n_blk, tq), lambda i, j, k: (i, j, 0, k))],
        out_shape=[jax.ShapeDtypeStruct((b, s, MIX_WIDTH), BF),
                   jax.ShapeDtypeStruct((b, g, n_blk, s), BF)],
        compiler_params=_params(("parallel", "parallel", "parallel")),
        name="nsa_cmp_select",
    )(qT, k_cmp, v_cmpT, ov, gates5)


def _flash_kernel(*refs, mode, n_par, n_rep, dq, kc, tq, tk, cw, window, n_blk, has_sink, has_gate,
                  fuse_out=None):
    it = iter(refs)
    q_ref, k_ref, v_ref = next(it), next(it), next(it)
    sel_ref = next(it) if n_blk else None
    sink_ref = next(it) if has_sink else None
    gate_ref = next(it) if has_gate else None
    qnext_ref = next(it) if mode == "causal" else None
    selnext_ref = next(it) if (mode == "causal" and n_blk) else None
    if fuse_out is None:
        out_ref = next(it)
    else:
        n_extra, has_next = fuse_out
        extra_refs = [next(it) for _ in range(n_extra)]
        z_ref, x_ref, w_ref = next(it), next(it), next(it)
        ng_ref = next(it) if has_next else None
        x_out = next(it)
        h_out = next(it) if has_next else None
    qs_ref, m_ref, acc_ref, mt_ref = (next(it) for _ in range(4))
    qn_ref = next(it) if mode == "causal" else None
    bias_ref = next(it) if mode == "window" else None
    s_bufs = tuple(it)

    nq = n_rep * tq
    items = [(p, c) for p in range(n_par) for c in range(nq // cw)]
    n_items = len(items)
    grp = pl.program_id(1)
    q0 = pl.program_id(2) * tq
    acc_row = lax.broadcasted_iota(jnp.int32, (V_ROWS, nq), 0)
    for p in range(n_par):
        for r in range(n_rep):
            hd = p * n_rep + r
            qs_ref[p, 0:dq, r * tq:(r + 1) * tq] = q_ref[0, hd * dq:(hd + 1) * dq, :]
        if n_blk:
            qs_ref[p, dq:dq + n_blk, :] = jnp.concatenate([sel_ref[0, p]] * n_rep, axis=1)
            if dq + n_blk < kc:
                qs_ref[p, dq + n_blk:kc, :] = jnp.zeros((kc - dq - n_blk, nq), BF)
        if has_sink:
            m_ref[p] = jnp.concatenate(
                [jnp.full((1, tq), sink_ref[(grp * n_par + p) * n_rep + r] * LOG2E, F32) for r in range(n_rep)],
                axis=1)
            acc_ref[p] = jnp.where(acc_row == HEAD_DIM, 1.0, 0.0)
        else:
            m_ref[p] = jnp.full((1, nq), M_INIT, F32)
            acc_ref[p] = jnp.zeros((V_ROWS, nq), F32)

    t_row = q0 + (lax.broadcasted_iota(jnp.int32, (1, nq), 1) & (tq - 1))
    one_row = jnp.where(lax.broadcasted_iota(jnp.int32, (V_ROWS - HEAD_DIM, tk), 0) == 0, 1.0, 0.0).astype(BF)

    def stage_a(item, j, key0, slot, causal, bias=None, pen=None):
        p, c = item
        cols = slice(c * cw, (c + 1) * cw)
        s = _dot(k_ref[0, p, j][:, :kc], qs_ref[p, :, cols])
        if causal:
            key = key0 + lax.broadcasted_iota(jnp.int32, (tk, 1), 0)
            s = jnp.where(key <= t_row[:, cols], s, NEG_INF)
        if bias is not None:
            s = s + bias_ref[bias]
        s_bufs[slot][...] = s
        mt = jnp.max(s, axis=0, keepdims=True)
        mt_ref[slot] = mt if pen is None else mt - pen

    def stage_b(item, j, slot, pen=None):
        p, c = item
        cols = slice(c * cw, (c + 1) * cw)
        m_old = m_ref[p, :, cols]
        m_new = jnp.maximum(m_old, mt_ref[slot])
        pr = jnp.exp2(s_bufs[slot][...] - (m_new if pen is None else m_new + pen)).astype(BF)
        alpha = jnp.exp2(m_old - m_new)
        v = jnp.concatenate([v_ref[0, j, p * HEAD_DIM:(p + 1) * HEAD_DIM, :], one_row], axis=0)
        acc_ref[p, :, cols] = alpha * acc_ref[p, :, cols] + _dot(v, pr)
        m_ref[p, :, cols] = m_new

    def finalize():
        outs = []
        for p in range(n_par):
            acc = acc_ref[p]
            o = acc[:HEAD_DIM] * (1.0 / acc[HEAD_DIM:HEAD_DIM + 1])
            if has_gate:
                o = o * _gate_row(gate_ref, n_rep, p)
            outs += [o[:, r * tq:(r + 1) * tq] for r in range(n_rep)]
        o_tok = jnp.concatenate(outs, axis=0).T
        if fuse_out is None:
            out_ref[0] = o_tok.astype(out_ref.dtype)
            return
        for r in extra_refs:
            o_tok = o_tok + r[0].astype(F32)
        y = _dot((o_tok * z_ref[0].astype(F32)).astype(BF), w_ref[...])
        x_new = x_ref[0] + y
        x_out[0] = x_new
        if has_next:
            h_out[0] = _rms_rows(x_new, ng_ref[...]).astype(BF)

    if mode == "causal":
        assert n_items % 2 == 0
        n_full = q0 // tk

        def prefetch_next_tile():
            assert cw == tq
            qn_ref[0:dq, :] = qnext_ref[0, 0:dq, :]
            if n_blk:
                qn_ref[dq:dq + n_blk, :] = selnext_ref[0, 0]
                if dq + n_blk < kc:
                    qn_ref[dq + n_blk:kc, :] = jnp.zeros((kc - dq - n_blk, tq), BF)
            s = _dot(k_ref[0, 0, 0][:, :kc], qn_ref[...])
            key = lax.broadcasted_iota(jnp.int32, (tk, 1), 0)
            s = jnp.where(key <= t_row[:, 0:cw] + tq, s, NEG_INF)
            s_bufs[0][...] = s
            mt_ref[0] = jnp.max(s, axis=0, keepdims=True)

        def step(j, kind, next_kind, last):
            for idx, item in enumerate(items):
                slot = idx % 2
                if idx + 1 < n_items:
                    stage_a(items[idx + 1], j, j * tk, 1 - slot, kind)
                elif not last:
                    stage_a(items[0], j + 1, (j + 1) * tk, 1 - slot, next_kind)
                else:
                    prefetch_next_tile()
                stage_b(item, j, slot)

        @pl.when(pl.program_id(2) == 0)
        def _():
            stage_a(items[0], 0, 0, 0, "causal")

        def body(j, carry):
            step(j, None, None, False)
            return carry

        def body_group(i, carry):
            for u in range(FLASH_UNROLL):
                step(FLASH_UNROLL * i + u, None, None, False)
            return carry

        n_main = jnp.maximum(n_full - 1, 0)
        n_groups = lax.shift_right_logical(n_main, int(np.log2(FLASH_UNROLL)))
        lax.fori_loop(0, n_groups, body_group, 0)
        lax.fori_loop(FLASH_UNROLL * n_groups, n_main, body, 0)

        @pl.when(n_full >= 1)
        def _():
            step(n_full - 1, None, "causal", False)
            step(n_full, "causal", None, True)
            finalize()

        @pl.when(n_full == 0)
        def _():
            step(0, "causal", None, True)
            finalize()
    else:
        w_tiles, q_tiles = window // tk, tq // tk
        assert w_tiles >= q_tiles and cw % tq == 0
        t_loc = lax.broadcasted_iota(jnp.int32, (1, cw), 1) & (tq - 1)
        key_loc = lax.broadcasted_iota(jnp.int32, (tk, 1), 0)
        for i in range(q_tiles):
            bias_ref[i] = jnp.where(i * tk + key_loc > t_loc, 0.0, NEG_INF)
            bias_ref[q_tiles + i] = jnp.where(i * tk + key_loc <= t_loc, 0.0, NEG_INF)
        work = []
        for i in range(w_tiles + q_tiles):
            jv = q0 // tk - w_tiles + i
            bias = i if i < q_tiles else (q_tiles + i - w_tiles if i >= w_tiles else None)
            pen = jnp.where(jv < 0, -NEG_INF, 0.0) if i < w_tiles else None
            work += [(item, jnp.maximum(jv, 0), bias, pen) for item in items]
        n_buf = len(s_bufs)
        ahead = WINDOW_LOOKAHEAD
        for n in range(min(ahead, len(work))):
            stage_a(work[n][0], work[n][1], 0, n % n_buf, False, work[n][2], work[n][3])
        for n, (item, j, _, pen) in enumerate(work):
            if n + ahead < len(work):
                nxt = work[n + ahead]
                stage_a(nxt[0], nxt[1], 0, (n + ahead) % n_buf, False, nxt[2], nxt[3])
            stage_b(item, j, n % n_buf, pen)

        finalize()


def _flash(qT, k_tok, vT_tiles, *, mode, n_par, n_rep, dq, kc, tq, tk, cw, window=None,
           sel_bias=None, sinks=None, gates=None, gate_branch=0, out_proj=None, name="flash"):
    b, _, s = qT.shape
    kh, k_lanes = k_tok.shape[1], k_tok.shape[3]
    n_t = s // tk
    k5 = k_tok.reshape(b, kh, n_t, tk, k_lanes)
    n_grp = kh // n_par
    heads = n_par * n_rep
    nq = n_rep * tq
    n_blk = sel_bias.shape[2] if sel_bias is not None else 0
    has_sink, has_gate = sinks is not None, gates is not None
    args = [qT, k5, vT_tiles]
    in_specs = [
        pl.BlockSpec((1, heads * dq, tq), lambda i, j, k: (i, j, k)),
        pl.BlockSpec((1, n_par, n_t, tk, k_lanes), lambda i, j, k: (i, j, 0, 0, 0)),
        pl.BlockSpec((1, n_t, n_par * HEAD_DIM, tk), lambda i, j, k: (i, 0, j, 0)),
    ]
    if n_blk:
        assert dq + n_blk <= kc
        args.append(sel_bias)
        in_specs.append(pl.BlockSpec((1, n_par, n_blk, tq), lambda i, j, k: (i, j, 0, k)))
    if has_sink:
        args.append(sinks.astype(F32))
        in_specs.append(pl.BlockSpec(memory_space=pltpu.SMEM))
    if has_gate:
        args.append(gates)
        in_specs.append(pl.BlockSpec((1, 1, n_par, n_rep, tq),
                                     lambda i, j, k, _br=gate_branch: (i, _br, j, 0, k)))
    n_sbuf = 2 if mode == "causal" else WINDOW_SCORE_BUFS
    scratch = [pltpu.VMEM((n_par, kc, nq), BF), pltpu.VMEM((n_par, 1, nq), F32),
               pltpu.VMEM((n_par, V_ROWS, nq), F32), pltpu.VMEM((n_sbuf, 1, cw), F32)]
    if mode == "causal":
        last_q = s // tq - 1
        args.append(qT)
        in_specs.append(pl.BlockSpec((1, dq, tq), lambda i, j, k: (i, j * heads, jnp.minimum(k + 1, last_q))))
        if n_blk:
            args.append(sel_bias)
            in_specs.append(pl.BlockSpec((1, 1, n_blk, tq),
                                         lambda i, j, k: (i, j * n_par, 0, jnp.minimum(k + 1, last_q))))
        scratch.append(pltpu.VMEM((kc, tq), BF))
    else:
        scratch.append(pltpu.VMEM((2 * (tq // tk), tk, cw), F32))
    scratch += [pltpu.VMEM((tk, cw), F32)] * n_sbuf
    out_specs = pl.BlockSpec((1, tq, heads * HEAD_DIM), lambda i, j, k: (i, k, j))
    out_shape = jax.ShapeDtypeStruct((b, s, MIX_WIDTH), BF)
    fuse_out = None
    if out_proj is not None:
        assert mode == "window" and heads * HEAD_DIM == MIX_WIDTH and n_grp == 1
        extra, zs, x, w_out, next_gain = out_proj
        tok = pl.BlockSpec((1, tq, D_MODEL), lambda i, j, k: (i, k, 0))
        args += list(extra) + [zs, x, w_out.astype(BF)]
        in_specs += [tok] * (len(extra) + 2) + [pl.BlockSpec(w_out.shape, lambda i, j, k: (0, 0))]
        out_specs, out_shape = [tok], [jax.ShapeDtypeStruct(x.shape, F32)]
        if next_gain is not None:
            args.append(next_gain.reshape(1, D_MODEL))
            in_specs.append(pl.BlockSpec((1, D_MODEL), lambda i, j, k: (0, 0)))
            out_specs.append(tok)
            out_shape.append(jax.ShapeDtypeStruct(x.shape, BF))
        fuse_out = (len(extra), next_gain is not None)
    kern = functools.partial(_flash_kernel, mode=mode, n_par=n_par, n_rep=n_rep, dq=dq, kc=kc, tq=tq, tk=tk,
                             cw=cw, window=window, n_blk=n_blk, has_sink=has_sink, has_gate=has_gate,
                             fuse_out=fuse_out)
    res = pl.pallas_call(
        kern,
        grid=(b, n_grp, s // tq),
        in_specs=in_specs,
        out_specs=out_specs,
        out_shape=out_shape,
        scratch_shapes=scratch,
        compiler_params=_params(("parallel", "parallel", "arbitrary")),
        name=name,
    )(*args)
    if out_proj is None:
        return res
    return (res[0], res[1]) if next_gain is not None else (res[0], None)


def _swa_proj_kernel(h_ref, cos_ref, sin_ref, wq_ref, wk_ref, wv_ref, wz_ref, qg_ref, kg_ref,
                     q_out, k_out, v_out, z_out, y_ref, *, win_tile):
    h = h_ref[0]
    cos, sin = cos_ref[0], sin_ref[0]
    y_ref[...] = _dot_nt(wq_ref[...], h)
    for hd in range(N_HEADS):
        rows = slice(hd * HEAD_DIM, (hd + 1) * HEAD_DIM)
        q_out[0, rows, :] = (_rope_t(_head_rms_t(y_ref[rows, :], qg_ref[...]), cos, sin) * Q_SCALE).astype(BF)
    kvd = SWA_KV_HEADS * HEAD_DIM
    y_ref[0:kvd, :] = _dot_nt(wk_ref[...], h)
    for gi in range(SWA_KV_HEADS):
        rows = slice(gi * HEAD_DIM, (gi + 1) * HEAD_DIM)
        k_out[0, gi] = _to_token_major(_rope_t(_head_rms_t(y_ref[rows, :], kg_ref[...]), cos, sin)).astype(BF)
    _store_lane_tiles(v_out, _dot_nt(wv_ref[...], h).astype(BF), win_tile)
    z_out[0] = _silu(_dot(h, wz_ref[...])).astype(BF)


def _swa_proj(h, cos, sin, w_in, q_gain, k_gain, *, win_tile):
    b, s, d = h.shape
    g = SWA_KV_HEADS
    kvd = g * HEAD_DIM
    tm = min(TOK_TILE, s)
    wt = w_in.T.astype(BF)
    wq, wk, wv = wt[:MIX_WIDTH], wt[MIX_WIDTH:MIX_WIDTH + kvd], wt[MIX_WIDTH + kvd:MIX_WIDTH + 2 * kvd]
    wz = w_in[:, MIX_WIDTH + 2 * kvd:].astype(BF)
    qg, kg = q_gain.reshape(HEAD_DIM, 1), k_gain.reshape(HEAD_DIM, 1)
    wpt = tm // win_tile
    half = HEAD_DIM // 2
    return pl.pallas_call(
        functools.partial(_swa_proj_kernel, win_tile=win_tile),
        grid=(b, s // tm),
        in_specs=[pl.BlockSpec((1, tm, d), lambda i, j: (i, j, 0)),
                  pl.BlockSpec((1, half, tm), lambda i, j: (i, 0, j)),
                  pl.BlockSpec((1, half, tm), lambda i, j: (i, 0, j)),
                  _full(wq.shape), _full(wk.shape), _full(wv.shape), _full(wz.shape),
                  _full(qg.shape), _full(kg.shape)],
        out_specs=[pl.BlockSpec((1, MIX_WIDTH, tm), lambda i, j: (i, 0, j)),
                   pl.BlockSpec((1, g, tm, KEY_PAD), lambda i, j: (i, 0, j, 0)),
                   pl.BlockSpec((1, wpt, kvd, win_tile), lambda i, j: (i, j, 0, 0)),
                   pl.BlockSpec((1, tm, MIX_WIDTH), lambda i, j: (i, j, 0))],
        out_shape=[jax.ShapeDtypeStruct((b, MIX_WIDTH, s), BF),
                   jax.ShapeDtypeStruct((b, g, s, KEY_PAD), BF),
                   jax.ShapeDtypeStruct((b, s // win_tile, kvd, win_tile), BF),
                   jax.ShapeDtypeStruct((b, s, MIX_WIDTH), BF)],
        scratch_shapes=[pltpu.VMEM((MIX_WIDTH, tm), F32)],
        compiler_params=_params(("parallel", "parallel")),
        name="swa_proj",
    )(h, cos, sin, wq, wk, wv, wz, qg, kg)


def _split3(x):
    hi = x.astype(BF)
    r1 = x - hi.astype(F32)
    mid = r1.astype(BF)
    lo = (r1 - mid.astype(F32)).astype(BF)
    return hi, mid, lo


def _fox_proj_kernel(h_ref, wf_ref, bias_ref, tri_ref, wq_ref, wk_ref, wv_ref, wz_ref, qg_ref, kg_ref,
                     q_out, k_out, v_out, z_out, yq_ref, yk_ref, cum_ref, carry_ref, *, tile):
    @pl.when(pl.program_id(1) == 0)
    def _():
        carry_ref[...] = jnp.zeros_like(carry_ref)

    h = h_ref[0]
    tm = h.shape[0]
    x = _dot_nt(wf_ref[...], h) + bias_ref[...]
    logf = jnp.minimum(x, 0.0) - jnp.log(1.0 + jnp.exp(-jnp.abs(x)))
    tri = tri_ref[...]
    hi, mid, lo = _split3(logf)
    cum = (_dot(hi, tri) + _dot(mid, tri)) + _dot(lo, tri) + carry_ref[:, 0:1]
    carry_ref[...] = jnp.broadcast_to(cum[:, -1:], carry_ref.shape)
    cum_ref[...] = cum * LOG2E

    yq_ref[...] = _dot_nt(wq_ref[...], h)
    yk_ref[...] = _dot_nt(wk_ref[...], h)
    row = lax.broadcasted_iota(jnp.int32, (8, tm), 0)
    zeros = jnp.zeros((KEY_PAD - HEAD_DIM - 16, tm), F32)
    for hd in range(N_HEADS):
        rows = slice(hd * HEAD_DIM, (hd + 1) * HEAD_DIM)
        c_hi, c_mid, c_lo = (c.astype(F32) for c in _split3(cum_ref[hd:hd + 1, :]))
        c3 = jnp.where(row == 0, c_hi, jnp.where(row == 1, c_mid, jnp.where(row == 2, c_lo, 0.0)))
        one3 = jnp.where(row < 3, 1.0, 0.0)
        q = _head_rms_t(yq_ref[rows, :], qg_ref[...]) * Q_SCALE
        q_out[0, hd] = jnp.concatenate([q, one3, c3, zeros], axis=0).astype(BF)
        k = _head_rms_t(yk_ref[rows, :], kg_ref[...])
        k_out[0, hd] = jnp.concatenate([k, -c3, one3, zeros], axis=0).T.astype(BF)
    _store_lane_tiles(v_out, _dot_nt(wv_ref[...], h).astype(BF), tile)
    z_out[0] = _silu(_dot(h, wz_ref[...])).astype(BF)


def _fox_proj(h, w_in, forget_bias, q_gain, k_gain, *, tile):
    b, s, d = h.shape
    tm = min(TOK_TILE, s)
    wt = w_in.T.astype(BF)
    wq, wk, wv = wt[:MIX_WIDTH], wt[MIX_WIDTH:2 * MIX_WIDTH], wt[2 * MIX_WIDTH:3 * MIX_WIDTH]
    wf = wt[3 * MIX_WIDTH:3 * MIX_WIDTH + N_HEADS]
    wz = w_in[:, 3 * MIX_WIDTH + N_HEADS:].astype(BF)
    qg, kg = q_gain.reshape(HEAD_DIM, 1), k_gain.reshape(HEAD_DIM, 1)
    tri = jnp.asarray(np.arange(tm)[:, None] <= np.arange(tm)[None, :], BF)
    return pl.pallas_call(
        functools.partial(_fox_proj_kernel, tile=tile),
        grid=(b, s // tm),
        in_specs=[pl.BlockSpec((1, tm, d), lambda i, j: (i, j, 0)),
                  _full(wf.shape), _full((N_HEADS, 1)), _full(tri.shape),
                  _full(wq.shape), _full(wk.shape), _full(wv.shape), _full(wz.shape),
                  _full(qg.shape), _full(kg.shape)],
        out_specs=[pl.BlockSpec((1, N_HEADS, KEY_PAD, tm), lambda i, j: (i, 0, 0, j)),
                   pl.BlockSpec((1, N_HEADS, tm, KEY_PAD), lambda i, j: (i, 0, j, 0)),
                   pl.BlockSpec((1, tm // tile, MIX_WIDTH, tile), lambda i, j: (i, j, 0, 0)),
                   pl.BlockSpec((1, tm, MIX_WIDTH), lambda i, j: (i, j, 0))],
        out_shape=[jax.ShapeDtypeStruct((b, N_HEADS, KEY_PAD, s), BF),
                   jax.ShapeDtypeStruct((b, N_HEADS, s, KEY_PAD), BF),
                   jax.ShapeDtypeStruct((b, s // tile, MIX_WIDTH, tile), BF),
                   jax.ShapeDtypeStruct((b, s, MIX_WIDTH), BF)],
        scratch_shapes=[pltpu.VMEM((MIX_WIDTH, tm), F32), pltpu.VMEM((MIX_WIDTH, tm), F32),
                        pltpu.VMEM((N_HEADS, tm), F32), pltpu.VMEM((N_HEADS, LANES), F32)],
        compiler_params=_params(("parallel", "arbitrary")),
        name="fox_proj",
    )(h, wf, forget_bias.reshape(N_HEADS, 1).astype(F32), tri, wq, wk, wv, wz, qg, kg)


def _out_proj_kernel(*refs, n_o, has_next):
    o_refs = refs[:n_o]
    z_ref, x_ref, w_ref = refs[n_o:n_o + 3]
    rest = refs[n_o + 3:]
    if has_next:
        g_ref, x_out, h_out = rest
    else:
        (x_out,) = rest
    o = o_refs[0][0].astype(F32)
    for r in o_refs[1:]:
        o = o + r[0].astype(F32)
    y = _dot((o * z_ref[0].astype(F32)).astype(BF), w_ref[...])
    x_new = x_ref[0] + y
    x_out[0] = x_new
    if has_next:
        h_out[0] = _rms_rows(x_new, g_ref[...]).astype(BF)


def _out_proj(o_list, zs, x, w_out, next_gain):
    b, s, d = x.shape
    tm = min(TOK_TILE, s)
    has_next = next_gain is not None
    blk = pl.BlockSpec((1, tm, d), lambda i, j: (i, j, 0))
    args = list(o_list) + [zs, x, w_out.astype(BF)]
    in_specs = [blk] * (len(o_list) + 2) + [_full(w_out.shape)]
    out_shape = [jax.ShapeDtypeStruct((b, s, d), F32)]
    out_specs = [blk]
    if has_next:
        args.append(next_gain.reshape(1, d))
        in_specs.append(_full((1, d)))
        out_shape.append(jax.ShapeDtypeStruct((b, s, d), BF))
        out_specs.append(blk)
    res = pl.pallas_call(
        functools.partial(_out_proj_kernel, n_o=len(o_list), has_next=has_next),
        grid=(b, s // tm), in_specs=in_specs, out_specs=out_specs, out_shape=out_shape,
        compiler_params=_params(("parallel", "parallel")),
        name="out_proj",
    )(*args)
    return (res[0], res[1]) if has_next else (res[0], None)


ITEM_LANES = 512
CMP_Q_TILE = 512
CMP_GROUPS = 2
NSA_SLC_Q_TILE = 512
NSA_SLC_TILE = 512
NSA_SLC_GROUPS = 2
NSA_WIN_Q_TILE = 256
NSA_WIN_TILE = 256
SWA_TILE = 128
FOX_Q_TILE = 512
FOX_K_TILE = 512
FOX_HEADS_PER_STEP = 4


def _nsa_mixer(x, h, cos, sin, w_in, q_gain, k_gain, cmp_pos, cmp_w1, cmp_w2, w_out, next_gain, pre_gain):
    b, s, _ = x.shape
    g = NSA_KV_HEADS
    r = N_HEADS // g
    qT, kc_tok, vc_tok, ks, kw, vsT, vwT, gates, zs = _nsa_proj(
        x if h is None else h, cos, sin, w_in, q_gain, k_gain, slc_tile=NSA_SLC_TILE, win_tile=NSA_WIN_TILE,
        pre_gain=pre_gain if h is None else None)
    k_cmp, v_cmpT = _compress(kc_tok, vc_tok, cmp_pos, cmp_w1, cmp_w2, k_gain[0])
    gates5 = gates.reshape(b, 3, g, r, s)
    o_cmp, sel_bias = _cmp_select(qT, k_cmp, v_cmpT, gates5, tq=CMP_Q_TILE)
    o_slc = _flash(qT, ks, vsT, mode="causal", n_par=NSA_SLC_GROUPS, n_rep=r, dq=HEAD_DIM, kc=SLC_KEY_LANES,
                   tq=NSA_SLC_Q_TILE, tk=NSA_SLC_TILE, cw=ITEM_LANES, sel_bias=sel_bias,
                   gates=gates5, gate_branch=1, name="nsa_selected")
    return _flash(qT, kw, vwT, mode="window", n_par=g, n_rep=r, dq=HEAD_DIM, kc=HEAD_DIM,
                  tq=NSA_WIN_Q_TILE, tk=NSA_WIN_TILE, cw=ITEM_LANES, window=NSA_WINDOW,
                  gates=gates5, gate_branch=2, out_proj=([o_cmp, o_slc], zs, x, w_out, next_gain),
                  name="nsa_window_out")


def _swa_mixer(x, h, cos, sin, w_in, q_gain, k_gain, sinks, w_out, next_gain):
    r = N_HEADS // SWA_KV_HEADS
    qT, k, vT, zs = _swa_proj(h, cos, sin, w_in, q_gain, k_gain, win_tile=SWA_TILE)
    return _flash(qT, k, vT, mode="window", n_par=SWA_KV_HEADS, n_rep=r, dq=HEAD_DIM, kc=HEAD_DIM, tq=SWA_TILE,
                  tk=SWA_TILE, cw=r * SWA_TILE, window=SWA_WINDOW, sinks=sinks,
                  out_proj=([], zs, x, w_out, next_gain), name="swa_window_out")


def _fox_mixer(x, h, w_in, forget_bias, q_gain, k_gain, w_out, next_gain):
    b, s, _ = h.shape
    qT, k, vT, zs = _fox_proj(h, w_in, forget_bias, q_gain, k_gain, tile=FOX_K_TILE)
    o = _flash(qT.reshape(b, N_HEADS * KEY_PAD, s), k, vT, mode="causal", n_par=FOX_HEADS_PER_STEP, n_rep=1, dq=KEY_PAD,
               kc=KEY_PAD, tq=FOX_Q_TILE, tk=FOX_K_TILE, cw=ITEM_LANES, name="fox_attention")
    return _out_proj([o], zs, x, w_out, next_gain)


def kernel(x, positions, norm_gains, a_w_in, a_q_gain, a_k_gain, a_cmp_pos, a_cmp_w1, a_cmp_w2, a_w_out,
           b_w_in, b_q_gain, b_k_gain, b_sinks, b_w_out,
           c_w_in, c_forget_bias, c_q_gain, c_k_gain, c_w_out):
    depth = norm_gains.shape[0]
    cos, sin = _rope_tables(positions)
    h = None
    for i in range(depth):
        j, mixer = divmod(i, 3)
        next_gain = norm_gains[i + 1] if i + 1 < depth else None
        if mixer == 0:
            x, h = _nsa_mixer(x, h, cos, sin, a_w_in[j], a_q_gain[j], a_k_gain[j],
                              a_cmp_pos[j], a_cmp_w1[j], a_cmp_w2[j], a_w_out[j], next_gain, norm_gains[i])
        elif mixer == 1:
            x, h = _swa_mixer(x, h, cos, sin, b_w_in[j], b_q_gain[j], b_k_gain[j], b_sinks[j],
                              b_w_out[j], next_gain)
        else:
            x, h = _fox_mixer(x, h, c_w_in[j], c_forget_bias[j], c_q_gain[j], c_k_gain[j],
                              c_w_out[j], next_gain)
    return x
```

```python
import functools

import jax
import jax.numpy as jnp
import numpy as np
from jax import lax
from jax.experimental import pallas as pl
from jax.experimental.pallas import tpu as pltpu

D_MODEL = 1024
HEAD_DIM = 64
N_HEADS = 16
MIX_WIDTH = N_HEADS * HEAD_DIM
ROPE_THETA = 10000.0
EPS = 1e-6
SCALE = HEAD_DIM ** -0.5
NEG_INF = -1e30
M_INIT = -1e29
SEL_OFF = -(2.0 ** 100)

NSA_KV_HEADS = 4
NSA_CMP_LEN = 32
NSA_CMP_STRIDE = 16
NSA_SLC_LEN = 64
NSA_TOPK = 16
NSA_WINDOW = 512
SWA_KV_HEADS = 2
SWA_WINDOW = 128

LOG2E = float(np.log2(np.e))
Q_SCALE = SCALE * LOG2E

LANES = 128
KEY_PAD = 128
SLC_KEY_LANES = 256
V_ROWS = 80
FLASH_UNROLL = 4
WINDOW_LOOKAHEAD = 2
WINDOW_SCORE_BUFS = 4
CMP_BLOCK = 128
CMP_MASK_ROWS = 2 * CMP_BLOCK
VMEM_LIMIT = 56 * 1024 * 1024

TOK_TILE = 512
NT_DIMS = (((1,), (1,)), ((), ()))

BF = jnp.bfloat16
F32 = jnp.float32


def _params(sem):
    return pltpu.CompilerParams(dimension_semantics=sem, vmem_limit_bytes=VMEM_LIMIT)


def _dot(a, b):
    return jnp.dot(a, b, preferred_element_type=F32)


def _dot_nt(a, b):
    return lax.dot_general(a, b, NT_DIMS, preferred_element_type=F32)


def _rope_tab_kernel(pos_ref, invf_ref, cos_ref, sin_ref):
    ang = invf_ref[...] * pos_ref[0].astype(F32)
    cos_ref[0] = jnp.cos(ang)
    sin_ref[0] = jnp.sin(ang)


def _rope_tables(positions):
    b, s = positions.shape
    half = HEAD_DIM // 2
    inv_freq = ROPE_THETA ** (-jnp.arange(half, dtype=F32) * 2.0 / HEAD_DIM)
    tm = min(TOK_TILE, s)
    out = jax.ShapeDtypeStruct((b, half, s), F32)
    return pl.pallas_call(
        _rope_tab_kernel,
        grid=(b, s // tm),
        in_specs=[pl.BlockSpec((1, 1, tm), lambda i, j: (i, 0, j)),
                  pl.BlockSpec((half, 1), lambda i, j: (0, 0))],
        out_specs=[pl.BlockSpec((1, half, tm), lambda i, j: (i, 0, j))] * 2,
        out_shape=[out, out],
        compiler_params=_params(("parallel", "parallel")),
        name="rope_tables",
    )(positions.reshape(b, 1, s), inv_freq.reshape(half, 1))


def _rms_rows(x, gain_row):
    y = x * lax.rsqrt(jnp.mean(x * x, axis=-1, keepdims=True) + EPS)
    return y * gain_row


def _head_rms_t(y, gain_col):
    ms = jnp.mean(y * y, axis=0, keepdims=True)
    return (y * lax.rsqrt(ms + EPS)) * gain_col


def _rope_t(y, cos, sin):
    half = HEAD_DIM // 2
    x1, x2 = y[:half], y[half:]
    return jnp.concatenate([x1 * cos - x2 * sin, x2 * cos + x1 * sin], axis=0)


def _to_token_major(y):
    pad = jnp.zeros((KEY_PAD - y.shape[0], y.shape[1]), y.dtype)
    return jnp.concatenate([y, pad], axis=0).T


def _silu(z):
    return z * (1.0 / (1.0 + jnp.exp(-z)))


def _sigmoid(z):
    return 1.0 / (1.0 + jnp.exp(-z))


def _store_lane_tiles(out_ref, y, tile):
    for c in range(y.shape[1] // tile):
        out_ref[0, c] = y[:, c * tile:(c + 1) * tile]


def _store_chunk_rows(out_ref, gi, yt, tok_ref):
    tok_ref[...] = yt
    n = yt.shape[0] // NSA_CMP_STRIDE
    for m in range(NSA_CMP_STRIDE // 2):
        even = tok_ref[pl.ds(2 * m, n, stride=NSA_CMP_STRIDE), :]
        odd = tok_ref[pl.ds(2 * m + 1, n, stride=NSA_CMP_STRIDE), :]
        out_ref[0, gi, :, KEY_PAD * m:KEY_PAD * (m + 1)] = even + pltpu.roll(odd, HEAD_DIM, 1)


def _nsa_proj_kernel(*refs, slc_tile, win_tile, norm_in):
    it = iter(refs)
    h_ref = next(it)
    ng_ref = next(it) if norm_in else None
    (cos_ref, sin_ref, wq_ref, wk_ref, wv_ref, wg_ref, wz_ref, qg_ref, kg_ref,
     q_out, kc_out, vc_out, ks_out, kw_out, vs_out, vw_out, g_out, z_out, y_ref, tok_ref) = it
    h = _rms_rows(h_ref[0], ng_ref[...]).astype(BF) if norm_in else h_ref[0]
    cos, sin = cos_ref[0], sin_ref[0]
    g = NSA_KV_HEADS
    kvd = g * HEAD_DIM
    y_ref[...] = _dot_nt(wq_ref[...], h)
    for hd in range(N_HEADS):
        rows = slice(hd * HEAD_DIM, (hd + 1) * HEAD_DIM)
        y = _rope_t(_head_rms_t(y_ref[rows, :], qg_ref[...]), cos, sin) * Q_SCALE
        q_out[0, rows, :] = y.astype(BF)
    y_ref[0:3 * kvd, :] = _dot_nt(wk_ref[...], h)
    tm = h.shape[0]
    tok = pl.program_id(1) * tm + lax.broadcasted_iota(jnp.int32, (tm, 1), 0)
    blk_lane = HEAD_DIM + lax.shift_right_logical(tok, int(np.log2(NSA_SLC_LEN)))
    blk_hot = lax.broadcasted_iota(jnp.int32, (1, SLC_KEY_LANES), 1) == blk_lane
    for kind in range(3):
        for gi in range(g):
            r0 = (kind * g + gi) * HEAD_DIM
            y = y_ref[r0:r0 + HEAD_DIM, :]
            if kind > 0:
                y = _head_rms_t(y, kg_ref[:, kind:kind + 1])
            yt = _to_token_major(_rope_t(y, cos, sin))
            if kind == 0:
                _store_chunk_rows(kc_out, gi, yt, tok_ref)
            elif kind == 1:
                wide = jnp.concatenate([yt, jnp.zeros((tm, SLC_KEY_LANES - KEY_PAD), F32)], axis=1)
                ks_out[0, gi] = jnp.where(blk_hot, 1.0, wide).astype(BF)
            else:
                kw_out[0, gi] = yt.astype(BF)
    y_ref[0:3 * kvd, :] = _dot_nt(wv_ref[...], h)
    for gi in range(g):
        r0 = gi * HEAD_DIM
        _store_chunk_rows(vc_out, gi, _to_token_major(y_ref[r0:r0 + HEAD_DIM, :]), tok_ref)
    _store_lane_tiles(vs_out, y_ref[kvd:2 * kvd, :].astype(BF), slc_tile)
    _store_lane_tiles(vw_out, y_ref[2 * kvd:3 * kvd, :].astype(BF), win_tile)
    g_out[0] = _sigmoid(_dot_nt(wg_ref[...], h))
    z_out[0] = _silu(_dot(h, wz_ref[...])).astype(BF)


def _full(shape):
    nd = len(shape)
    return pl.BlockSpec(shape, lambda i, j, _n=nd: (0,) * _n)


def _nsa_proj(h, cos, sin, w_in, q_gain, k_gain, *, slc_tile, win_tile, pre_gain=None):
    b, s, d = h.shape
    g = NSA_KV_HEADS
    kvd = g * HEAD_DIM
    tm = min(TOK_TILE, s)
    flat = NSA_CMP_STRIDE * HEAD_DIM
    sizes = [MIX_WIDTH] + [kvd] * 6 + [3 * N_HEADS]
    off = np.cumsum([0] + sizes)
    wt = w_in.T.astype(BF)
    wq = wt[off[0]:off[1]]
    wk = jnp.concatenate([wt[off[1]:off[2]], wt[off[3]:off[4]], wt[off[5]:off[6]]], axis=0)
    wv = jnp.concatenate([wt[off[2]:off[3]], wt[off[4]:off[5]], wt[off[6]:off[7]]], axis=0)
    wg = wt[off[7]:off[8]]
    wz = w_in[:, off[8]:].astype(BF)
    qg = q_gain.reshape(HEAD_DIM, 1)
    kg = k_gain.T
    n_t = s // tm
    out_shape = [
        jax.ShapeDtypeStruct((b, MIX_WIDTH, s), BF),
        jax.ShapeDtypeStruct((b, g, s // NSA_CMP_STRIDE, flat), F32),
        jax.ShapeDtypeStruct((b, g, s // NSA_CMP_STRIDE, flat), F32),
        jax.ShapeDtypeStruct((b, g, s, SLC_KEY_LANES), BF),
        jax.ShapeDtypeStruct((b, g, s, KEY_PAD), BF),
        jax.ShapeDtypeStruct((b, s // slc_tile, kvd, slc_tile), BF),
        jax.ShapeDtypeStruct((b, s // win_tile, kvd, win_tile), BF),
        jax.ShapeDtypeStruct((b, 3 * N_HEADS, s), F32),
        jax.ShapeDtypeStruct((b, s, MIX_WIDTH), BF),
    ]
    out_specs = [
        pl.BlockSpec((1, MIX_WIDTH, tm), lambda i, j: (i, 0, j)),
        pl.BlockSpec((1, g, tm // NSA_CMP_STRIDE, flat), lambda i, j: (i, 0, j, 0)),
        pl.BlockSpec((1, g, tm // NSA_CMP_STRIDE, flat), lambda i, j: (i, 0, j, 0)),
        pl.BlockSpec((1, g, tm, SLC_KEY_LANES), lambda i, j: (i, 0, j, 0)),
        pl.BlockSpec((1, g, tm, KEY_PAD), lambda i, j: (i, 0, j, 0)),
        pl.BlockSpec((1, tm // slc_tile, kvd, slc_tile), lambda i, j: (i, j, 0, 0)),
        pl.BlockSpec((1, tm // win_tile, kvd, win_tile), lambda i, j: (i, j, 0, 0)),
        pl.BlockSpec((1, 3 * N_HEADS, tm), lambda i, j: (i, 0, j)),
        pl.BlockSpec((1, tm, MIX_WIDTH), lambda i, j: (i, j, 0)),
    ]
    norm_in = pre_gain is not None
    args = [h] + ([pre_gain.reshape(1, d)] if norm_in else []) + [cos, sin, wq, wk, wv, wg, wz, qg, kg]
    in_specs = [pl.BlockSpec((1, tm, d), lambda i, j: (i, j, 0))] + ([_full((1, d))] if norm_in else []) + [
        pl.BlockSpec((1, HEAD_DIM // 2, tm), lambda i, j: (i, 0, j)),
        pl.BlockSpec((1, HEAD_DIM // 2, tm), lambda i, j: (i, 0, j)),
        _full(wq.shape), _full(wk.shape), _full(wv.shape), _full(wg.shape), _full(wz.shape),
        _full(qg.shape), _full(kg.shape),
    ]
    return pl.pallas_call(
        functools.partial(_nsa_proj_kernel, slc_tile=slc_tile, win_tile=win_tile, norm_in=norm_in),
        grid=(b, n_t), in_specs=in_specs, out_specs=out_specs, out_shape=out_shape,
        scratch_shapes=[pltpu.VMEM((MIX_WIDTH, tm), F32), pltpu.VMEM((tm, KEY_PAD), F32)],
        compiler_params=_params(("parallel", "parallel")),
        name="nsa_proj",
    )(*args)


def _gelu_tanh(x):
    c = np.float32(np.sqrt(2.0 / np.pi))
    return 0.5 * x * (1.0 + jnp.tanh(c * (x + 0.044715 * (x * x * x))))


def _compress_kernel(kc_ref, vc_ref, pos_ref, w1_ref, w2_ref, kg_ref, kcmp_out, vcmp_out):
    for which, (src, dst) in enumerate(((kc_ref, kcmp_out), (vc_ref, vcmp_out))):
        x = src[0, 0]
        n = x.shape[0]
        half = x.shape[1]
        xa = (x + pos_ref[which, 0:1, :]).astype(BF)
        xb = (x + pos_ref[which, 1:2, :]).astype(BF)
        ua = _dot(xa, w1_ref[which, :half, :])
        ub = _dot(xb, w1_ref[which, half:, :])
        row = lax.broadcasted_iota(jnp.int32, (n, 1), 0)
        ub_next = jnp.where(row == n - 1, 0.0, pltpu.roll(ub, n - 1, 0))
        hid = _gelu_tanh(ua + ub_next)
        y = _dot(hid.astype(BF), w2_ref[which])
        if which == 0:
            y = _rms_rows(y, kg_ref[...])
            dst[0, 0] = y.astype(BF)
        else:
            pad = jnp.zeros((n, KEY_PAD - HEAD_DIM), F32)
            dst[0, 0] = jnp.concatenate([y, pad], axis=1).T[:HEAD_DIM].astype(BF)


def _compress(kc, vc, cmp_pos, cmp_w1, cmp_w2, k_gain0):
    b, g, n_chunk, flat = kc.shape
    pos = cmp_pos.reshape(2, 2, flat)
    w1 = cmp_w1.astype(BF)
    w2 = cmp_w2.astype(BF)
    blk = pl.BlockSpec((1, 1, n_chunk, flat), lambda i, j: (i, j, 0, 0))
    return pl.pallas_call(
        _compress_kernel,
        grid=(b, g),
        in_specs=[blk, blk, _full(pos.shape), _full(w1.shape), _full(w2.shape),
                  _full((1, HEAD_DIM))],
        out_specs=[pl.BlockSpec((1, 1, n_chunk, HEAD_DIM), lambda i, j: (i, j, 0, 0)),
                   pl.BlockSpec((1, 1, HEAD_DIM, n_chunk), lambda i, j: (i, j, 0, 0))],
        out_shape=[jax.ShapeDtypeStruct((b, g, n_chunk, HEAD_DIM), BF),
                   jax.ShapeDtypeStruct((b, g, HEAD_DIM, n_chunk), BF)],
        compiler_params=_params(("parallel", "parallel")),
        name="nsa_compress",
    )(kc, vc, pos, w1, w2, k_gain0.reshape(1, HEAD_DIM))


def _gate_row(gate_ref, n_heads, grp=0):
    return jnp.concatenate([gate_ref[0, 0, grp, r:r + 1, :] for r in range(n_heads)], axis=1)


def _cmp_branch(rows, q_ref, kc_ref, vc_ref, ov_ref, gate_ref, o_out, sel_out, *, tq, n_blk, n_grp):
    r_heads = N_HEADS // NSA_KV_HEADS
    nq = r_heads * tq
    width = r_heads * HEAD_DIM
    q0 = pl.program_id(2) * tq
    n_cmp = kc_ref.shape[2] - 1
    t_row = q0 + (lax.broadcasted_iota(jnp.int32, (1, nq), 1) & (tq - 1))
    lo = max(rows - CMP_MASK_ROWS, 0)
    c_col = lo + lax.broadcasted_iota(jnp.int32, (rows - lo, 1), 0)
    valid = (c_col * NSA_CMP_STRIDE + (NSA_CMP_LEN - 1) <= t_row) & (c_col < n_cmp)
    one_row = jnp.where(lax.broadcasted_iota(jnp.int32, (V_ROWS - HEAD_DIM, rows), 0) == 0, 1.0, 0.0).astype(BF)
    n_live = min(n_blk, rows * NSA_CMP_STRIDE // NSA_SLC_LEN + 8)
    t1 = q0 + lax.broadcasted_iota(jnp.int32, (1, tq), 1)
    cur = lax.shift_right_logical(t1, int(np.log2(NSA_SLC_LEN)))
    blk = lax.broadcasted_iota(jnp.int32, (n_live, tq), 0)
    forced = (blk == 0) | (blk == cur) | (blk == cur - 1)
    imps = []
    for gi in range(n_grp):
        q4 = jnp.concatenate([q_ref[0, (gi * r_heads + r) * HEAD_DIM:(gi * r_heads + r + 1) * HEAD_DIM, :]
                              for r in range(r_heads)], axis=1)
        s = _dot(kc_ref[0, gi, 0:rows, :], q4)
        s_new = jnp.where(valid, s[lo:], NEG_INF)
        m = jnp.max(s_new, axis=0, keepdims=True)
        if lo:
            m = jnp.maximum(m, jnp.max(s[:lo], axis=0, keepdims=True))
        e = jnp.where(valid, jnp.exp2(s_new - m), 0.0).astype(BF)
        if lo:
            e = jnp.concatenate([jnp.exp2(s[:lo] - m).astype(BF), e], axis=0)
        lhs = jnp.concatenate([vc_ref[0, gi, :, 0:rows], one_row, ov_ref[0:n_live, 0:rows]], axis=0)
        res = _dot(lhs, e)
        l = res[HEAD_DIM:HEAD_DIM + 1]
        inv = jnp.where(l > 0.0, 1.0 / jnp.where(l > 0.0, l, 1.0), 0.0)
        o = res[:HEAD_DIM] * (inv * _gate_row(gate_ref, r_heads, gi))
        o_rows = jnp.concatenate([o[:, r * tq:(r + 1) * tq] for r in range(r_heads)], axis=0)
        o_out[0, :, gi * width:(gi + 1) * width] = o_rows.T.astype(o_out.dtype)
        w = res[V_ROWS:V_ROWS + n_live] * inv
        imp = w[:, 0:tq]
        for r in range(1, r_heads):
            imp = imp + w[:, r * tq:(r + 1) * tq]
        imps.append(jnp.where(forced, -jnp.inf, jnp.where(blk > cur, NEG_INF, imp)))
    for _ in range(min(NSA_TOPK, n_blk) - 3):
        for gi in range(n_grp):
            best = jnp.max(imps[gi], axis=0, keepdims=True)
            first = jnp.min(jnp.where(imps[gi] == best, blk, n_blk), axis=0, keepdims=True)
            imps[gi] = jnp.where(blk == first, -jnp.inf, imps[gi])
    for gi in range(n_grp):
        sel_out[0, gi, 0:n_live, :] = jnp.where(imps[gi] == -jnp.inf, 0.0, SEL_OFF).astype(BF)
        if n_live < n_blk:
            sel_out[0, gi, n_live:n_blk, :] = jnp.full((n_blk - n_live, tq), SEL_OFF, BF)


def _cmp_select_kernel(q_ref, kc_ref, vc_ref, ov_ref, gate_ref, o_out, sel_out, *, tq, n_blk, n_grp):
    q0 = pl.program_id(2) * tq
    n_chunk = kc_ref.shape[2]
    n_need = jnp.minimum((q0 + tq - NSA_CMP_LEN) // NSA_CMP_STRIDE + 1, n_chunk - 1)
    n_steps = n_chunk // CMP_BLOCK
    need_steps = (n_need + CMP_BLOCK - 1) // CMP_BLOCK
    for k in range(1, n_steps + 1):
        @pl.when(need_steps == k)
        def _(k=k):
            _cmp_branch(k * CMP_BLOCK, q_ref, kc_ref, vc_ref, ov_ref, gate_ref, o_out, sel_out,
                        tq=tq, n_blk=n_blk, n_grp=n_grp)


def _overlap_matrix(s):
    n_chunk = s // NSA_CMP_STRIDE
    n_blk = s // NSA_SLC_LEN
    c0 = np.arange(n_chunk) * NSA_CMP_STRIDE
    c1 = c0 + NSA_CMP_LEN - 1
    b0 = np.arange(n_blk) * NSA_SLC_LEN
    ov = np.minimum(c1[None, :], b0[:, None] + NSA_SLC_LEN - 1) - np.maximum(c0[None, :], b0[:, None]) + 1
    return jnp.asarray(np.clip(ov, 0, None) / NSA_CMP_LEN, BF)


def _cmp_select(qT, k_cmp, v_cmpT, gates5, *, tq):
    b, _, s = qT.shape
    g = NSA_KV_HEADS
    r_heads = N_HEADS // g
    n_chunk = k_cmp.shape[2]
    n_blk = s // NSA_SLC_LEN
    ov = _overlap_matrix(s)
    n_grp = CMP_GROUPS
    rows = n_grp * r_heads * HEAD_DIM
    return pl.pallas_call(
        functools.partial(_cmp_select_kernel, tq=tq, n_blk=n_blk, n_grp=n_grp),
        grid=(b, g // n_grp, s // tq),
        in_specs=[
            pl.BlockSpec((1, rows, tq), lambda i, j, k: (i, j, k)),
            pl.BlockSpec((1, n_grp, n_chunk, HEAD_DIM), lambda i, j, k: (i, j, 0, 0)),
            pl.BlockSpec((1, n_grp, HEAD_DIM, n_chunk), lambda i, j, k: (i, j, 0, 0)),
            pl.BlockSpec((n_blk, n_chunk), lambda i, j, k: (0, 0)),
            pl.BlockSpec((1, 1, n_grp, r_heads, tq), lambda i, j, k: (i, 0, j, 0, k)),
        ],
        out_specs=[pl.BlockSpec((1, tq, rows), lambda i, j, k: (i, k, j)),
                   pl.BlockSpec((1, n_grp, n_blk, tq), lambda i, j, k: (i, j, 0, k))],
        out_shape=[jax.ShapeDtypeStruct((b, s, MIX_WIDTH), BF),
                   jax.ShapeDtypeStruct((b, g, n_blk, s), BF)],
        compiler_params=_params(("parallel", "parallel", "parallel")),
        name="nsa_cmp_select",
    )(qT, k_cmp, v_cmpT, ov, gates5)


def _flash_kernel(*refs, mode, n_par, n_rep, dq, kc, tq, tk, cw, window, n_blk, has_sink, has_gate,
                  fuse_out=None):
    it = iter(refs)
    q_ref, k_ref, v_ref = next(it), next(it), next(it)
    sel_ref = next(it) if n_blk else None
    sink_ref = next(it) if has_sink else None
    gate_ref = next(it) if has_gate else None
    qnext_ref = next(it) if mode == "causal" else None
    selnext_ref = next(it) if (mode == "causal" and n_blk) else None
    if fuse_out is None:
        out_ref = next(it)
    else:
        n_extra, has_next = fuse_out
        extra_refs = [next(it) for _ in range(n_extra)]
        z_ref, x_ref, w_ref = next(it), next(it), next(it)
        ng_ref = next(it) if has_next else None
        x_out = next(it)
        h_out = next(it) if has_next else None
    qs_ref, m_ref, acc_ref, mt_ref = (next(it) for _ in range(4))
    qn_ref = next(it) if mode == "causal" else None
    bias_ref = next(it) if mode == "window" else None
    s_bufs = tuple(it)

    nq = n_rep * tq
    items = [(p, c) for p in range(n_par) for c in range(nq // cw)]
    n_items = len(items)
    grp = pl.program_id(1)
    q0 = pl.program_id(2) * tq
    acc_row = lax.broadcasted_iota(jnp.int32, (V_ROWS, nq), 0)
    for p in range(n_par):
        for r in range(n_rep):
            hd = p * n_rep + r
            qs_ref[p, 0:dq, r * tq:(r + 1) * tq] = q_ref[0, hd * dq:(hd + 1) * dq, :]
        if n_blk:
            qs_ref[p, dq:dq + n_blk, :] = jnp.concatenate([sel_ref[0, p]] * n_rep, axis=1)
            if dq + n_blk < kc:
                qs_ref[p, dq + n_blk:kc, :] = jnp.zeros((kc - dq - n_blk, nq), BF)
        if has_sink:
            m_ref[p] = jnp.concatenate(
                [jnp.full((1, tq), sink_ref[(grp * n_par + p) * n_rep + r] * LOG2E, F32) for r in range(n_rep)],
                axis=1)
            acc_ref[p] = jnp.where(acc_row == HEAD_DIM, 1.0, 0.0)
        else:
            m_ref[p] = jnp.full((1, nq), M_INIT, F32)
            acc_ref[p] = jnp.zeros((V_ROWS, nq), F32)

    t_row = q0 + (lax.broadcasted_iota(jnp.int32, (1, nq), 1) & (tq - 1))
    one_row = jnp.where(lax.broadcasted_iota(jnp.int32, (V_ROWS - HEAD_DIM, tk), 0) == 0, 1.0, 0.0).astype(BF)

    def stage_a(item, j, key0, slot, causal, bias=None, pen=None):
        p, c = item
        cols = slice(c * cw, (c + 1) * cw)
        s = _dot(k_ref[0, p, j][:, :kc], qs_ref[p, :, cols])
        if causal:
            key = key0 + lax.broadcasted_iota(jnp.int32, (tk, 1), 0)
            s = jnp.where(key <= t_row[:, cols], s, NEG_INF)
        if bias is not None:
            s = s + bias_ref[bias]
        s_bufs[slot][...] = s
        mt = jnp.max(s, axis=0, keepdims=True)
        mt_ref[slot] = mt if pen is None else mt - pen

    def stage_b(item, j, slot, pen=None):
        p, c = item
        cols = slice(c * cw, (c + 1) * cw)
        m_old = m_ref[p, :, cols]
        m_new = jnp.maximum(m_old, mt_ref[slot])
        pr = jnp.exp2(s_bufs[slot][...] - (m_new if pen is None else m_new + pen)).astype(BF)
        alpha = jnp.exp2(m_old - m_new)
        v = jnp.concatenate([v_ref[0, j, p * HEAD_DIM:(p + 1) * HEAD_DIM, :], one_row], axis=0)
        acc_ref[p, :, cols] = alpha * acc_ref[p, :, cols] + _dot(v, pr)
        m_ref[p, :, cols] = m_new

    def finalize():
        outs = []
        for p in range(n_par):
            acc = acc_ref[p]
            o = acc[:HEAD_DIM] * (1.0 / acc[HEAD_DIM:HEAD_DIM + 1])
            if has_gate:
                o = o * _gate_row(gate_ref, n_rep, p)
            outs += [o[:, r * tq:(r + 1) * tq] for r in range(n_rep)]
        o_tok = jnp.concatenate(outs, axis=0).T
        if fuse_out is None:
            out_ref[0] = o_tok.astype(out_ref.dtype)
            return
        for r in extra_refs:
            o_tok = o_tok + r[0].astype(F32)
        y = _dot((o_tok * z_ref[0].astype(F32)).astype(BF), w_ref[...])
        x_new = x_ref[0] + y
        x_out[0] = x_new
        if has_next:
            h_out[0] = _rms_rows(x_new, ng_ref[...]).astype(BF)

    if mode == "causal":
        assert n_items % 2 == 0
        n_full = q0 // tk

        def prefetch_next_tile():
            assert cw == tq
            qn_ref[0:dq, :] = qnext_ref[0, 0:dq, :]
            if n_blk:
                qn_ref[dq:dq + n_blk, :] = selnext_ref[0, 0]
                if dq + n_blk < kc:
                    qn_ref[dq + n_blk:kc, :] = jnp.zeros((kc - dq - n_blk, tq), BF)
            s = _dot(k_ref[0, 0, 0][:, :kc], qn_ref[...])
            key = lax.broadcasted_iota(jnp.int32, (tk, 1), 0)
            s = jnp.where(key <= t_row[:, 0:cw] + tq, s, NEG_INF)
            s_bufs[0][...] = s
            mt_ref[0] = jnp.max(s, axis=0, keepdims=True)

        def step(j, kind, next_kind, last):
            for idx, item in enumerate(items):
                slot = idx % 2
                if idx + 1 < n_items:
                    stage_a(items[idx + 1], j, j * tk, 1 - slot, kind)
                elif not last:
                    stage_a(items[0], j + 1, (j + 1) * tk, 1 - slot, next_kind)
                else:
                    prefetch_next_tile()
                stage_b(item, j, slot)

        @pl.when(pl.program_id(2) == 0)
        def _():
            stage_a(items[0], 0, 0, 0, "causal")

        def body(j, carry):
            step(j, None, None, False)
            return carry

        def body_group(i, carry):
            for u in range(FLASH_UNROLL):
                step(FLASH_UNROLL * i + u, None, None, False)
            return carry

        n_main = jnp.maximum(n_full - 1, 0)
        n_groups = lax.shift_right_logical(n_main, int(np.log2(FLASH_UNROLL)))
        lax.fori_loop(0, n_groups, body_group, 0)
        lax.fori_loop(FLASH_UNROLL * n_groups, n_main, body, 0)

        @pl.when(n_full >= 1)
        def _():
            step(n_full - 1, None, "causal", False)
            step(n_full, "causal", None, True)
            finalize()

        @pl.when(n_full == 0)
        def _():
            step(0, "causal", None, True)
            finalize()
    else:
        w_tiles, q_tiles = window // tk, tq // tk
        assert w_tiles >= q_tiles and cw % tq == 0
        t_loc = lax.broadcasted_iota(jnp.int32, (1, cw), 1) & (tq - 1)
        key_loc = lax.broadcasted_iota(jnp.int32, (tk, 1), 0)
        for i in range(q_tiles):
            bias_ref[i] = jnp.where(i * tk + key_loc > t_loc, 0.0, NEG_INF)
            bias_ref[q_tiles + i] = jnp.where(i * tk + key_loc <= t_loc, 0.0, NEG_INF)
        work = []
        for i in range(w_tiles + q_tiles):
            jv = q0 // tk - w_tiles + i
            bias = i if i < q_tiles else (q_tiles + i - w_tiles if i >= w_tiles else None)
            pen = jnp.where(jv < 0, -NEG_INF, 0.0) if i < w_tiles else None
            work += [(item, jnp.maximum(jv, 0), bias, pen) for item in items]
        n_buf = len(s_bufs)
        ahead = WINDOW_LOOKAHEAD
        for n in range(min(ahead, len(work))):
            stage_a(work[n][0], work[n][1], 0, n % n_buf, False, work[n][2], work[n][3])
        for n, (item, j, _, pen) in enumerate(work):
            if n + ahead < len(work):
                nxt = work[n + ahead]
                stage_a(nxt[0], nxt[1], 0, (n + ahead) % n_buf, False, nxt[2], nxt[3])
            stage_b(item, j, n % n_buf, pen)

        finalize()


def _flash(qT, k_tok, vT_tiles, *, mode, n_par, n_rep, dq, kc, tq, tk, cw, window=None,
           sel_bias=None, sinks=None, gates=None, gate_branch=0, out_proj=None, single_buffer_v=False,
           name="flash"):
    b, _, s = qT.shape
    kh, k_lanes = k_tok.shape[1], k_tok.shape[3]
    n_t = s // tk
    k5 = k_tok.reshape(b, kh, n_t, tk, k_lanes)
    n_grp = kh // n_par
    heads = n_par * n_rep
    nq = n_rep * tq
    n_blk = sel_bias.shape[2] if sel_bias is not None else 0
    has_sink, has_gate = sinks is not None, gates is not None
    args = [qT, k5, vT_tiles]
    in_specs = [
        pl.BlockSpec((1, heads * dq, tq), lambda i, j, k: (i, j, k)),
        pl.BlockSpec((1, n_par, n_t, tk, k_lanes), lambda i, j, k: (i, j, 0, 0, 0)),
        pl.BlockSpec((1, n_t, n_par * HEAD_DIM, tk), lambda i, j, k: (i, 0, j, 0),
                     **({"pipeline_mode": pl.Buffered(1)} if single_buffer_v else {})),
    ]
    if n_blk:
        assert dq + n_blk <= kc
        args.append(sel_bias)
        in_specs.append(pl.BlockSpec((1, n_par, n_blk, tq), lambda i, j, k: (i, j, 0, k)))
    if has_sink:
        args.append(sinks.astype(F32))
        in_specs.append(pl.BlockSpec(memory_space=pltpu.SMEM))
    if has_gate:
        args.append(gates)
        in_specs.append(pl.BlockSpec((1, 1, n_par, n_rep, tq),
                                     lambda i, j, k, _br=gate_branch: (i, _br, j, 0, k)))
    n_sbuf = 2 if mode == "causal" else WINDOW_SCORE_BUFS
    scratch = [pltpu.VMEM((n_par, kc, nq), BF), pltpu.VMEM((n_par, 1, nq), F32),
               pltpu.VMEM((n_par, V_ROWS, nq), F32), pltpu.VMEM((n_sbuf, 1, cw), F32)]
    if mode == "causal":
        last_q = s // tq - 1
        args.append(qT)
        in_specs.append(pl.BlockSpec((1, dq, tq), lambda i, j, k: (i, j * heads, jnp.minimum(k + 1, last_q))))
        if n_blk:
            args.append(sel_bias)
            in_specs.append(pl.BlockSpec((1, 1, n_blk, tq),
                                         lambda i, j, k: (i, j * n_par, 0, jnp.minimum(k + 1, last_q))))
        scratch.append(pltpu.VMEM((kc, tq), BF))
    else:
        scratch.append(pltpu.VMEM((2 * (tq // tk), tk, cw), F32))
    scratch += [pltpu.VMEM((tk, cw), F32)] * n_sbuf
    out_specs = pl.BlockSpec((1, tq, heads * HEAD_DIM), lambda i, j, k: (i, k, j))
    out_shape = jax.ShapeDtypeStruct((b, s, MIX_WIDTH), BF)
    fuse_out = None
    if out_proj is not None:
        assert mode == "window" and heads * HEAD_DIM == MIX_WIDTH and n_grp == 1
        extra, zs, x, w_out, next_gain = out_proj
        tok = pl.BlockSpec((1, tq, D_MODEL), lambda i, j, k: (i, k, 0))
        args += list(extra) + [zs, x, w_out.astype(BF)]
        in_specs += [tok] * (len(extra) + 2) + [pl.BlockSpec(w_out.shape, lambda i, j, k: (0, 0))]
        out_specs, out_shape = [tok], [jax.ShapeDtypeStruct(x.shape, F32)]
        if next_gain is not None:
            args.append(next_gain.reshape(1, D_MODEL))
            in_specs.append(pl.BlockSpec((1, D_MODEL), lambda i, j, k: (0, 0)))
            out_specs.append(tok)
            out_shape.append(jax.ShapeDtypeStruct(x.shape, BF))
        fuse_out = (len(extra), next_gain is not None)
    kern = functools.partial(_flash_kernel, mode=mode, n_par=n_par, n_rep=n_rep, dq=dq, kc=kc, tq=tq, tk=tk,
                             cw=cw, window=window, n_blk=n_blk, has_sink=has_sink, has_gate=has_gate,
                             fuse_out=fuse_out)
    res = pl.pallas_call(
        kern,
        grid=(b, n_grp, s // tq),
        in_specs=in_specs,
        out_specs=out_specs,
        out_shape=out_shape,
        scratch_shapes=scratch,
        compiler_params=_params(("parallel", "parallel", "arbitrary")),
        name=name,
    )(*args)
    if out_proj is None:
        return res
    return (res[0], res[1]) if next_gain is not None else (res[0], None)


def _swa_proj_kernel(h_ref, cos_ref, sin_ref, wq_ref, wk_ref, wv_ref, wz_ref, qg_ref, kg_ref,
                     q_out, k_out, v_out, z_out, y_ref, *, win_tile):
    h = h_ref[0]
    cos, sin = cos_ref[0], sin_ref[0]
    y_ref[...] = _dot_nt(wq_ref[...], h)
    for hd in range(N_HEADS):
        rows = slice(hd * HEAD_DIM, (hd + 1) * HEAD_DIM)
        q_out[0, rows, :] = (_rope_t(_head_rms_t(y_ref[rows, :], qg_ref[...]), cos, sin) * Q_SCALE).astype(BF)
    kvd = SWA_KV_HEADS * HEAD_DIM
    y_ref[0:kvd, :] = _dot_nt(wk_ref[...], h)
    for gi in range(SWA_KV_HEADS):
        rows = slice(gi * HEAD_DIM, (gi + 1) * HEAD_DIM)
        k_out[0, gi] = _to_token_major(_rope_t(_head_rms_t(y_ref[rows, :], kg_ref[...]), cos, sin)).astype(BF)
    _store_lane_tiles(v_out, _dot_nt(wv_ref[...], h).astype(BF), win_tile)
    z_out[0] = _silu(_dot(h, wz_ref[...])).astype(BF)


def _swa_proj(h, cos, sin, w_in, q_gain, k_gain, *, win_tile):
    b, s, d = h.shape
    g = SWA_KV_HEADS
    kvd = g * HEAD_DIM
    tm = min(TOK_TILE, s)
    wt = w_in.T.astype(BF)
    wq, wk, wv = wt[:MIX_WIDTH], wt[MIX_WIDTH:MIX_WIDTH + kvd], wt[MIX_WIDTH + kvd:MIX_WIDTH + 2 * kvd]
    wz = w_in[:, MIX_WIDTH + 2 * kvd:].astype(BF)
    qg, kg = q_gain.reshape(HEAD_DIM, 1), k_gain.reshape(HEAD_DIM, 1)
    wpt = tm // win_tile
    half = HEAD_DIM // 2
    return pl.pallas_call(
        functools.partial(_swa_proj_kernel, win_tile=win_tile),
        grid=(b, s // tm),
        in_specs=[pl.BlockSpec((1, tm, d), lambda i, j: (i, j, 0)),
                  pl.BlockSpec((1, half, tm), lambda i, j: (i, 0, j)),
                  pl.BlockSpec((1, half, tm), lambda i, j: (i, 0, j)),
                  _full(wq.shape), _full(wk.shape), _full(wv.shape), _full(wz.shape),
                  _full(qg.shape), _full(kg.shape)],
        out_specs=[pl.BlockSpec((1, MIX_WIDTH, tm), lambda i, j: (i, 0, j)),
                   pl.BlockSpec((1, g, tm, KEY_PAD), lambda i, j: (i, 0, j, 0)),
                   pl.BlockSpec((1, wpt, kvd, win_tile), lambda i, j: (i, j, 0, 0)),
                   pl.BlockSpec((1, tm, MIX_WIDTH), lambda i, j: (i, j, 0))],
        out_shape=[jax.ShapeDtypeStruct((b, MIX_WIDTH, s), BF),
                   jax.ShapeDtypeStruct((b, g, s, KEY_PAD), BF),
                   jax.ShapeDtypeStruct((b, s // win_tile, kvd, win_tile), BF),
                   jax.ShapeDtypeStruct((b, s, MIX_WIDTH), BF)],
        scratch_shapes=[pltpu.VMEM((MIX_WIDTH, tm), F32)],
        compiler_params=_params(("parallel", "parallel")),
        name="swa_proj",
    )(h, cos, sin, wq, wk, wv, wz, qg, kg)


def _split3(x):
    hi = x.astype(BF)
    r1 = x - hi.astype(F32)
    mid = r1.astype(BF)
    lo = (r1 - mid.astype(F32)).astype(BF)
    return hi, mid, lo


def _fox_proj_kernel(h_ref, wf_ref, bias_ref, tri_ref, wq_ref, wk_ref, wv_ref, wz_ref, qg_ref, kg_ref,
                     q_out, k_out, v_out, z_out, yq_ref, yk_ref, cum_ref, carry_ref, *, tile):
    @pl.when(pl.program_id(1) == 0)
    def _():
        carry_ref[...] = jnp.zeros_like(carry_ref)

    h = h_ref[0]
    tm = h.shape[0]
    x = _dot_nt(wf_ref[...], h) + bias_ref[...]
    logf = jnp.minimum(x, 0.0) - jnp.log(1.0 + jnp.exp(-jnp.abs(x)))
    tri = tri_ref[...]
    hi, mid, lo = _split3(logf)
    cum = (_dot(hi, tri) + _dot(mid, tri)) + _dot(lo, tri) + carry_ref[:, 0:1]
    carry_ref[...] = jnp.broadcast_to(cum[:, -1:], carry_ref.shape)
    cum_ref[...] = cum * LOG2E

    yq_ref[...] = _dot_nt(wq_ref[...], h)
    yk_ref[...] = _dot_nt(wk_ref[...], h)
    row = lax.broadcasted_iota(jnp.int32, (8, tm), 0)
    zeros = jnp.zeros((KEY_PAD - HEAD_DIM - 16, tm), F32)
    for hd in range(N_HEADS):
        rows = slice(hd * HEAD_DIM, (hd + 1) * HEAD_DIM)
        c_hi, c_mid, c_lo = (c.astype(F32) for c in _split3(cum_ref[hd:hd + 1, :]))
        c3 = jnp.where(row == 0, c_hi, jnp.where(row == 1, c_mid, jnp.where(row == 2, c_lo, 0.0)))
        one3 = jnp.where(row < 3, 1.0, 0.0)
        q = _head_rms_t(yq_ref[rows, :], qg_ref[...]) * Q_SCALE
        q_out[0, hd] = jnp.concatenate([q, one3, c3, zeros], axis=0).astype(BF)
        k = _head_rms_t(yk_ref[rows, :], kg_ref[...])
        k_out[0, hd] = jnp.concatenate([k, -c3, one3, zeros], axis=0).T.astype(BF)
    _store_lane_tiles(v_out, _dot_nt(wv_ref[...], h).astype(BF), tile)
    z_out[0] = _silu(_dot(h, wz_ref[...])).astype(BF)


def _fox_proj(h, w_in, forget_bias, q_gain, k_gain, *, tile):
    b, s, d = h.shape
    tm = min(TOK_TILE, s)
    wt = w_in.T.astype(BF)
    wq, wk, wv = wt[:MIX_WIDTH], wt[MIX_WIDTH:2 * MIX_WIDTH], wt[2 * MIX_WIDTH:3 * MIX_WIDTH]
    wf = wt[3 * MIX_WIDTH:3 * MIX_WIDTH + N_HEADS]
    wz = w_in[:, 3 * MIX_WIDTH + N_HEADS:].astype(BF)
    qg, kg = q_gain.reshape(HEAD_DIM, 1), k_gain.reshape(HEAD_DIM, 1)
    tri = jnp.asarray(np.arange(tm)[:, None] <= np.arange(tm)[None, :], BF)
    return pl.pallas_call(
        functools.partial(_fox_proj_kernel, tile=tile),
        grid=(b, s // tm),
        in_specs=[pl.BlockSpec((1, tm, d), lambda i, j: (i, j, 0)),
                  _full(wf.shape), _full((N_HEADS, 1)), _full(tri.shape),
                  _full(wq.shape), _full(wk.shape), _full(wv.shape), _full(wz.shape),
                  _full(qg.shape), _full(kg.shape)],
        out_specs=[pl.BlockSpec((1, N_HEADS, KEY_PAD, tm), lambda i, j: (i, 0, 0, j)),
                   pl.BlockSpec((1, N_HEADS, tm, KEY_PAD), lambda i, j: (i, 0, j, 0)),
                   pl.BlockSpec((1, tm // tile, MIX_WIDTH, tile), lambda i, j: (i, j, 0, 0)),
                   pl.BlockSpec((1, tm, MIX_WIDTH), lambda i, j: (i, j, 0))],
        out_shape=[jax.ShapeDtypeStruct((b, N_HEADS, KEY_PAD, s), BF),
                   jax.ShapeDtypeStruct((b, N_HEADS, s, KEY_PAD), BF),
                   jax.ShapeDtypeStruct((b, s // tile, MIX_WIDTH, tile), BF),
                   jax.ShapeDtypeStruct((b, s, MIX_WIDTH), BF)],
        scratch_shapes=[pltpu.VMEM((MIX_WIDTH, tm), F32), pltpu.VMEM((MIX_WIDTH, tm), F32),
                        pltpu.VMEM((N_HEADS, tm), F32), pltpu.VMEM((N_HEADS, LANES), F32)],
        compiler_params=_params(("parallel", "arbitrary")),
        name="fox_proj",
    )(h, wf, forget_bias.reshape(N_HEADS, 1).astype(F32), tri, wq, wk, wv, wz, qg, kg)


def _out_proj_kernel(*refs, n_o, has_next):
    o_refs = refs[:n_o]
    z_ref, x_ref, w_ref = refs[n_o:n_o + 3]
    rest = refs[n_o + 3:]
    if has_next:
        g_ref, x_out, h_out = rest
    else:
        (x_out,) = rest
    o = o_refs[0][0].astype(F32)
    for r in o_refs[1:]:
        o = o + r[0].astype(F32)
    y = _dot((o * z_ref[0].astype(F32)).astype(BF), w_ref[...])
    x_new = x_ref[0] + y
    x_out[0] = x_new
    if has_next:
        h_out[0] = _rms_rows(x_new, g_ref[...]).astype(BF)


def _out_proj(o_list, zs, x, w_out, next_gain):
    b, s, d = x.shape
    tm = min(TOK_TILE, s)
    has_next = next_gain is not None
    blk = pl.BlockSpec((1, tm, d), lambda i, j: (i, j, 0))
    args = list(o_list) + [zs, x, w_out.astype(BF)]
    in_specs = [blk] * (len(o_list) + 2) + [_full(w_out.shape)]
    out_shape = [jax.ShapeDtypeStruct((b, s, d), F32)]
    out_specs = [blk]
    if has_next:
        args.append(next_gain.reshape(1, d))
        in_specs.append(_full((1, d)))
        out_shape.append(jax.ShapeDtypeStruct((b, s, d), BF))
        out_specs.append(blk)
    res = pl.pallas_call(
        functools.partial(_out_proj_kernel, n_o=len(o_list), has_next=has_next),
        grid=(b, s // tm), in_specs=in_specs, out_specs=out_specs, out_shape=out_shape,
        compiler_params=_params(("parallel", "parallel")),
        name="out_proj",
    )(*args)
    return (res[0], res[1]) if has_next else (res[0], None)


ITEM_LANES = 512
CMP_Q_TILE = 512
CMP_GROUPS = 2
NSA_SLC_Q_TILE = 512
NSA_SLC_TILE = 512
NSA_SLC_GROUPS = 2
NSA_WIN_Q_TILE = 256
NSA_WIN_TILE = 256
SWA_TILE = 128
FOX_Q_TILE = 512
FOX_K_TILE = 512
FOX_HEADS_PER_STEP = 8


def _nsa_mixer(x, h, cos, sin, w_in, q_gain, k_gain, cmp_pos, cmp_w1, cmp_w2, w_out, next_gain, pre_gain):
    b, s, _ = x.shape
    g = NSA_KV_HEADS
    r = N_HEADS // g
    qT, kc_tok, vc_tok, ks, kw, vsT, vwT, gates, zs = _nsa_proj(
        x if h is None else h, cos, sin, w_in, q_gain, k_gain, slc_tile=NSA_SLC_TILE, win_tile=NSA_WIN_TILE,
        pre_gain=pre_gain if h is None else None)
    k_cmp, v_cmpT = _compress(kc_tok, vc_tok, cmp_pos, cmp_w1, cmp_w2, k_gain[0])
    gates5 = gates.reshape(b, 3, g, r, s)
    o_cmp, sel_bias = _cmp_select(qT, k_cmp, v_cmpT, gates5, tq=CMP_Q_TILE)
    o_slc = _flash(qT, ks, vsT, mode="causal", n_par=NSA_SLC_GROUPS, n_rep=r, dq=HEAD_DIM, kc=SLC_KEY_LANES,
                   tq=NSA_SLC_Q_TILE, tk=NSA_SLC_TILE, cw=ITEM_LANES, sel_bias=sel_bias,
                   gates=gates5, gate_branch=1, name="nsa_selected")
    return _flash(qT, kw, vwT, mode="window", n_par=g, n_rep=r, dq=HEAD_DIM, kc=HEAD_DIM,
                  tq=NSA_WIN_Q_TILE, tk=NSA_WIN_TILE, cw=ITEM_LANES, window=NSA_WINDOW,
                  gates=gates5, gate_branch=2, out_proj=([o_cmp, o_slc], zs, x, w_out, next_gain),
                  name="nsa_window_out")


def _swa_mixer(x, h, cos, sin, w_in, q_gain, k_gain, sinks, w_out, next_gain):
    r = N_HEADS // SWA_KV_HEADS
    qT, k, vT, zs = _swa_proj(h, cos, sin, w_in, q_gain, k_gain, win_tile=SWA_TILE)
    return _flash(qT, k, vT, mode="window", n_par=SWA_KV_HEADS, n_rep=r, dq=HEAD_DIM, kc=HEAD_DIM, tq=SWA_TILE,
                  tk=SWA_TILE, cw=r * SWA_TILE, window=SWA_WINDOW, sinks=sinks,
                  out_proj=([], zs, x, w_out, next_gain), name="swa_window_out")


def _fox_mixer(x, h, w_in, forget_bias, q_gain, k_gain, w_out, next_gain):
    b, s, _ = h.shape
    qT, k, vT, zs = _fox_proj(h, w_in, forget_bias, q_gain, k_gain, tile=FOX_K_TILE)
    o = _flash(qT.reshape(b, N_HEADS * KEY_PAD, s), k, vT, mode="causal", n_par=FOX_HEADS_PER_STEP, n_rep=1, dq=KEY_PAD,
               kc=KEY_PAD, tq=FOX_Q_TILE, tk=FOX_K_TILE, cw=ITEM_LANES, single_buffer_v=True,
               name="fox_attention")
    return _out_proj([o], zs, x, w_out, next_gain)


def kernel(x, positions, norm_gains, a_w_in, a_q_gain, a_k_gain, a_cmp_pos, a_cmp_w1, a_cmp_w2, a_w_out,
           b_w_in, b_q_gain, b_k_gain, b_sinks, b_w_out,
           c_w_in, c_forget_bias, c_q_gain, c_k_gain, c_w_out):
    depth = norm_gains.shape[0]
    cos, sin = _rope_tables(positions)
    h = None
    for i in range(depth):
        j, mixer = divmod(i, 3)
        next_gain = norm_gains[i + 1] if i + 1 < depth else None
        if mixer == 0:
            x, h = _nsa_mixer(x, h, cos, sin, a_w_in[j], a_q_gain[j], a_k_gain[j],
                              a_cmp_pos[j], a_cmp_w1[j], a_cmp_w2[j], a_w_out[j], next_gain, norm_gains[i])
        elif mixer == 1:
            x, h = _swa_mixer(x, h, cos, sin, b_w_in[j], b_q_gain[j], b_k_gain[j], b_sinks[j],
                              b_w_out[j], next_gain)
        else:
            x, h = _fox_mixer(x, h, c_w_in[j], c_forget_bias[j], c_q_gain[j], c_k_gain[j],
                              c_w_out[j], next_gain)
    return x
```

```python
import functools

import jax
import jax.numpy as jnp
import numpy as np
from jax import lax
from jax.experimental import pallas as pl
from jax.experimental.pallas import tpu as pltpu

D_MODEL = 1024
HEAD_DIM = 64
N_HEADS = 16
MIX_WIDTH = N_HEADS * HEAD_DIM
ROPE_THETA = 10000.0
EPS = 1e-6
SCALE = HEAD_DIM ** -0.5
NEG_INF = -1e30
M_INIT = -1e29
SEL_OFF = -(2.0 ** 100)

NSA_KV_HEADS = 4
NSA_CMP_LEN = 32
NSA_CMP_STRIDE = 16
NSA_SLC_LEN = 64
NSA_TOPK = 16
NSA_WINDOW = 512
SWA_KV_HEADS = 2
SWA_WINDOW = 128

LOG2E = float(np.log2(np.e))
Q_SCALE = SCALE * LOG2E

LANES = 128
KEY_PAD = 128
SLC_KEY_LANES = 256
V_ROWS = 80
FLASH_UNROLL = 4
WINDOW_LOOKAHEAD = 2
WINDOW_SCORE_BUFS = 4
CMP_BLOCK = 128
CMP_MASK_ROWS = 2 * CMP_BLOCK
VMEM_LIMIT = 56 * 1024 * 1024

TOK_TILE = 512
NT_DIMS = (((1,), (1,)), ((), ()))

BF = jnp.bfloat16
F32 = jnp.float32


def _params(sem):
    return pltpu.CompilerParams(dimension_semantics=sem, vmem_limit_bytes=VMEM_LIMIT)


def _dot(a, b):
    return jnp.dot(a, b, preferred_element_type=F32)


def _dot_nt(a, b):
    return lax.dot_general(a, b, NT_DIMS, preferred_element_type=F32)


def _rope_tab_kernel(pos_ref, invf_ref, cos_ref, sin_ref):
    ang = invf_ref[...] * pos_ref[0].astype(F32)
    cos_ref[0] = jnp.cos(ang)
    sin_ref[0] = jnp.sin(ang)


def _rope_tables(positions):
    b, s = positions.shape
    half = HEAD_DIM // 2
    inv_freq = ROPE_THETA ** (-jnp.arange(half, dtype=F32) * 2.0 / HEAD_DIM)
    tm = min(TOK_TILE, s)
    out = jax.ShapeDtypeStruct((b, half, s), F32)
    return pl.pallas_call(
        _rope_tab_kernel,
        grid=(b, s // tm),
        in_specs=[pl.BlockSpec((1, 1, tm), lambda i, j: (i, 0, j)),
                  pl.BlockSpec((half, 1), lambda i, j: (0, 0))],
        out_specs=[pl.BlockSpec((1, half, tm), lambda i, j: (i, 0, j))] * 2,
        out_shape=[out, out],
        compiler_params=_params(("parallel", "parallel")),
        name="rope_tables",
    )(positions.reshape(b, 1, s), inv_freq.reshape(half, 1))


def _rms_rows(x, gain_row):
    y = x * lax.rsqrt(jnp.mean(x * x, axis=-1, keepdims=True) + EPS)
    return y * gain_row


def _head_rms_t(y, gain_col):
    ms = jnp.mean(y * y, axis=0, keepdims=True)
    return (y * lax.rsqrt(ms + EPS)) * gain_col


def _rope_t(y, cos, sin):
    half = HEAD_DIM // 2
    x1, x2 = y[:half], y[half:]
    return jnp.concatenate([x1 * cos - x2 * sin, x2 * cos + x1 * sin], axis=0)


def _to_token_major(y):
    pad = jnp.zeros((KEY_PAD - y.shape[0], y.shape[1]), y.dtype)
    return jnp.concatenate([y, pad], axis=0).T


def _silu(z):
    return z * (1.0 / (1.0 + jnp.exp(-z)))


def _sigmoid(z):
    return 1.0 / (1.0 + jnp.exp(-z))


def _store_lane_tiles(out_ref, y, tile):
    for c in range(y.shape[1] // tile):
        out_ref[0, c] = y[:, c * tile:(c + 1) * tile]


def _store_chunk_rows(out_ref, gi, yt, tok_ref):
    tok_ref[...] = yt
    n = yt.shape[0] // NSA_CMP_STRIDE
    for m in range(NSA_CMP_STRIDE // 2):
        even = tok_ref[pl.ds(2 * m, n, stride=NSA_CMP_STRIDE), :]
        odd = tok_ref[pl.ds(2 * m + 1, n, stride=NSA_CMP_STRIDE), :]
        out_ref[0, gi, :, KEY_PAD * m:KEY_PAD * (m + 1)] = even + pltpu.roll(odd, HEAD_DIM, 1)


def _nsa_proj_kernel(*refs, slc_tile, win_tile, norm_in):
    it = iter(refs)
    h_ref = next(it)
    ng_ref = next(it) if norm_in else None
    (cos_ref, sin_ref, wq_ref, wk_ref, wv_ref, wg_ref, wz_ref, qg_ref, kg_ref,
     q_out, kc_out, vc_out, ks_out, kw_out, vs_out, vw_out, g_out, z_out, y_ref, tok_ref) = it
    h = _rms_rows(h_ref[0], ng_ref[...]).astype(BF) if norm_in else h_ref[0]
    cos, sin = cos_ref[0], sin_ref[0]
    g = NSA_KV_HEADS
    kvd = g * HEAD_DIM
    y_ref[...] = _dot_nt(wq_ref[...], h)
    for hd in range(N_HEADS):
        rows = slice(hd * HEAD_DIM, (hd + 1) * HEAD_DIM)
        y = _rope_t(_head_rms_t(y_ref[rows, :], qg_ref[...]), cos, sin) * Q_SCALE
        q_out[0, rows, :] = y.astype(BF)
    y_ref[0:3 * kvd, :] = _dot_nt(wk_ref[...], h)
    tm = h.shape[0]
    tok = pl.program_id(1) * tm + lax.broadcasted_iota(jnp.int32, (tm, 1), 0)
    blk_lane = HEAD_DIM + lax.shift_right_logical(tok, int(np.log2(NSA_SLC_LEN)))
    blk_hot = lax.broadcasted_iota(jnp.int32, (1, SLC_KEY_LANES), 1) == blk_lane
    for kind in range(3):
        for gi in range(g):
            r0 = (kind * g + gi) * HEAD_DIM
            y = y_ref[r0:r0 + HEAD_DIM, :]
            if kind > 0:
                y = _head_rms_t(y, kg_ref[:, kind:kind + 1])
            yt = _to_token_major(_rope_t(y, cos, sin))
            if kind == 0:
                _store_chunk_rows(kc_out, gi, yt, tok_ref)
            elif kind == 1:
                wide = jnp.concatenate([yt, jnp.zeros((tm, SLC_KEY_LANES - KEY_PAD), F32)], axis=1)
                ks_out[0, gi] = jnp.where(blk_hot, 1.0, wide).astype(BF)
            else:
                kw_out[0, gi] = yt.astype(BF)
    y_ref[0:3 * kvd, :] = _dot_nt(wv_ref[...], h)
    for gi in range(g):
        r0 = gi * HEAD_DIM
        _store_chunk_rows(vc_out, gi, _to_token_major(y_ref[r0:r0 + HEAD_DIM, :]), tok_ref)
    _store_lane_tiles(vs_out, y_ref[kvd:2 * kvd, :].astype(BF), slc_tile)
    _store_lane_tiles(vw_out, y_ref[2 * kvd:3 * kvd, :].astype(BF), win_tile)
    g_out[0] = _sigmoid(_dot_nt(wg_ref[...], h))
    z_out[0] = _silu(_dot(h, wz_ref[...])).astype(BF)


def _full(shape):
    nd = len(shape)
    return pl.BlockSpec(shape, lambda i, j, _n=nd: (0,) * _n)


def _nsa_proj(h, cos, sin, w_in, q_gain, k_gain, *, slc_tile, win_tile, pre_gain=None):
    b, s, d = h.shape
    g = NSA_KV_HEADS
    kvd = g * HEAD_DIM
    tm = min(TOK_TILE, s)
    flat = NSA_CMP_STRIDE * HEAD_DIM
    sizes = [MIX_WIDTH] + [kvd] * 6 + [3 * N_HEADS]
    off = np.cumsum([0] + sizes)
    wt = w_in.T.astype(BF)
    wq = wt[off[0]:off[1]]
    wk = jnp.concatenate([wt[off[1]:off[2]], wt[off[3]:off[4]], wt[off[5]:off[6]]], axis=0)
    wv = jnp.concatenate([wt[off[2]:off[3]], wt[off[4]:off[5]], wt[off[6]:off[7]]], axis=0)
    wg = wt[off[7]:off[8]]
    wz = w_in[:, off[8]:].astype(BF)
    qg = q_gain.reshape(HEAD_DIM, 1)
    kg = k_gain.T
    n_t = s // tm
    out_shape = [
        jax.ShapeDtypeStruct((b, MIX_WIDTH, s), BF),
        jax.ShapeDtypeStruct((b, g, s // NSA_CMP_STRIDE, flat), F32),
        jax.ShapeDtypeStruct((b, g, s // NSA_CMP_STRIDE, flat), F32),
        jax.ShapeDtypeStruct((b, g, s, SLC_KEY_LANES), BF),
        jax.ShapeDtypeStruct((b, g, s, KEY_PAD), BF),
        jax.ShapeDtypeStruct((b, s // slc_tile, kvd, slc_tile), BF),
        jax.ShapeDtypeStruct((b, s // win_tile, kvd, win_tile), BF),
        jax.ShapeDtypeStruct((b, 3 * N_HEADS, s), F32),
        jax.ShapeDtypeStruct((b, s, MIX_WIDTH), BF),
    ]
    out_specs = [
        pl.BlockSpec((1, MIX_WIDTH, tm), lambda i, j: (i, 0, j)),
        pl.BlockSpec((1, g, tm // NSA_CMP_STRIDE, flat), lambda i, j: (i, 0, j, 0)),
        pl.BlockSpec((1, g, tm // NSA_CMP_STRIDE, flat), lambda i, j: (i, 0, j, 0)),
        pl.BlockSpec((1, g, tm, SLC_KEY_LANES), lambda i, j: (i, 0, j, 0)),
        pl.BlockSpec((1, g, tm, KEY_PAD), lambda i, j: (i, 0, j, 0)),
        pl.BlockSpec((1, tm // slc_tile, kvd, slc_tile), lambda i, j: (i, j, 0, 0)),
        pl.BlockSpec((1, tm // win_tile, kvd, win_tile), lambda i, j: (i, j, 0, 0)),
        pl.BlockSpec((1, 3 * N_HEADS, tm), lambda i, j: (i, 0, j)),
        pl.BlockSpec((1, tm, MIX_WIDTH), lambda i, j: (i, j, 0)),
    ]
    norm_in = pre_gain is not None
    args = [h] + ([pre_gain.reshape(1, d)] if norm_in else []) + [cos, sin, wq, wk, wv, wg, wz, qg, kg]
    in_specs = [pl.BlockSpec((1, tm, d), lambda i, j: (i, j, 0))] + ([_full((1, d))] if norm_in else []) + [
        pl.BlockSpec((1, HEAD_DIM // 2, tm), lambda i, j: (i, 0, j)),
        pl.BlockSpec((1, HEAD_DIM // 2, tm), lambda i, j: (i, 0, j)),
        _full(wq.shape), _full(wk.shape), _full(wv.shape), _full(wg.shape), _full(wz.shape),
        _full(qg.shape), _full(kg.shape),
    ]
    return pl.pallas_call(
        functools.partial(_nsa_proj_kernel, slc_tile=slc_tile, win_tile=win_tile, norm_in=norm_in),
        grid=(b, n_t), in_specs=in_specs, out_specs=out_specs, out_shape=out_shape,
        scratch_shapes=[pltpu.VMEM((MIX_WIDTH, tm), F32), pltpu.VMEM((tm, KEY_PAD), F32)],
        compiler_params=_params(("parallel", "parallel")),
        name="nsa_proj",
    )(*args)


def _gelu_tanh(x):
    c = np.float32(np.sqrt(2.0 / np.pi))
    return 0.5 * x * (1.0 + jnp.tanh(c * (x + 0.044715 * (x * x * x))))


def _compress_kernel(kc_ref, vc_ref, pos_ref, w1_ref, w2_ref, kg_ref, kcmp_out, vcmp_out):
    for which, (src, dst) in enumerate(((kc_ref, kcmp_out), (vc_ref, vcmp_out))):
        x = src[0, 0]
        n = x.shape[0]
        half = x.shape[1]
        xa = (x + pos_ref[which, 0:1, :]).astype(BF)
        xb = (x + pos_ref[which, 1:2, :]).astype(BF)
        ua = _dot(xa, w1_ref[which, :half, :])
        ub = _dot(xb, w1_ref[which, half:, :])
        row = lax.broadcasted_iota(jnp.int32, (n, 1), 0)
        ub_next = jnp.where(row == n - 1, 0.0, pltpu.roll(ub, n - 1, 0))
        hid = _gelu_tanh(ua + ub_next)
        y = _dot(hid.astype(BF), w2_ref[which])
        if which == 0:
            y = _rms_rows(y, kg_ref[...])
            dst[0, 0] = y.astype(BF)
        else:
            pad = jnp.zeros((n, KEY_PAD - HEAD_DIM), F32)
            dst[0, 0] = jnp.concatenate([y, pad], axis=1).T[:HEAD_DIM].astype(BF)


def _compress(kc, vc, cmp_pos, cmp_w1, cmp_w2, k_gain0):
    b, g, n_chunk, flat = kc.shape
    pos = cmp_pos.reshape(2, 2, flat)
    w1 = cmp_w1.astype(BF)
    w2 = cmp_w2.astype(BF)
    blk = pl.BlockSpec((1, 1, n_chunk, flat), lambda i, j: (i, j, 0, 0))
    return pl.pallas_call(
        _compress_kernel,
        grid=(b, g),
        in_specs=[blk, blk, _full(pos.shape), _full(w1.shape), _full(w2.shape),
                  _full((1, HEAD_DIM))],
        out_specs=[pl.BlockSpec((1, 1, n_chunk, HEAD_DIM), lambda i, j: (i, j, 0, 0)),
                   pl.BlockSpec((1, 1, HEAD_DIM, n_chunk), lambda i, j: (i, j, 0, 0))],
        out_shape=[jax.ShapeDtypeStruct((b, g, n_chunk, HEAD_DIM), BF),
                   jax.ShapeDtypeStruct((b, g, HEAD_DIM, n_chunk), BF)],
        compiler_params=_params(("parallel", "parallel")),
        name="nsa_compress",
    )(kc, vc, pos, w1, w2, k_gain0.reshape(1, HEAD_DIM))


def _gate_row(gate_ref, n_heads, grp=0):
    return jnp.concatenate([gate_ref[0, 0, grp, r:r + 1, :] for r in range(n_heads)], axis=1)


def _cmp_branch(rows, q_ref, kc_ref, vc_ref, ov_ref, gate_ref, o_out, sel_out, *, tq, n_blk, n_grp):
    r_heads = N_HEADS // NSA_KV_HEADS
    nq = r_heads * tq
    width = r_heads * HEAD_DIM
    q0 = pl.program_id(2) * tq
    n_cmp = kc_ref.shape[2] - 1
    t_row = q0 + (lax.broadcasted_iota(jnp.int32, (1, nq), 1) & (tq - 1))
    lo = max(rows - CMP_MASK_ROWS, 0)
    c_col = lo + lax.broadcasted_iota(jnp.int32, (rows - lo, 1), 0)
    valid = (c_col * NSA_CMP_STRIDE + (NSA_CMP_LEN - 1) <= t_row) & (c_col < n_cmp)
    one_row = jnp.where(lax.broadcasted_iota(jnp.int32, (V_ROWS - HEAD_DIM, rows), 0) == 0, 1.0, 0.0).astype(BF)
    n_live = min(n_blk, rows * NSA_CMP_STRIDE // NSA_SLC_LEN + 8)
    t1 = q0 + lax.broadcasted_iota(jnp.int32, (1, tq), 1)
    cur = lax.shift_right_logical(t1, int(np.log2(NSA_SLC_LEN)))
    blk = lax.broadcasted_iota(jnp.int32, (n_live, tq), 0)
    forced = (blk == 0) | (blk == cur) | (blk == cur - 1)
    imps = []
    for gi in range(n_grp):
        q4 = jnp.concatenate([q_ref[0, (gi * r_heads + r) * HEAD_DIM:(gi * r_heads + r + 1) * HEAD_DIM, :]
                              for r in range(r_heads)], axis=1)
        s = _dot(kc_ref[0, gi, 0:rows, :], q4)
        s_new = jnp.where(valid, s[lo:], NEG_INF)
        m = jnp.max(s_new, axis=0, keepdims=True)
        if lo:
            m = jnp.maximum(m, jnp.max(s[:lo], axis=0, keepdims=True))
        e = jnp.where(valid, jnp.exp2(s_new - m), 0.0).astype(BF)
        if lo:
            e = jnp.concatenate([jnp.exp2(s[:lo] - m).astype(BF), e], axis=0)
        lhs = jnp.concatenate([vc_ref[0, gi, :, 0:rows], one_row, ov_ref[0:n_live, 0:rows]], axis=0)
        res = _dot(lhs, e)
        l = res[HEAD_DIM:HEAD_DIM + 1]
        inv = jnp.where(l > 0.0, 1.0 / jnp.where(l > 0.0, l, 1.0), 0.0)
        o = res[:HEAD_DIM] * (inv * _gate_row(gate_ref, r_heads, gi))
        o_rows = jnp.concatenate([o[:, r * tq:(r + 1) * tq] for r in range(r_heads)], axis=0)
        o_out[0, :, gi * width:(gi + 1) * width] = o_rows.T.astype(o_out.dtype)
        w = res[V_ROWS:V_ROWS + n_live] * inv
        imp = w[:, 0:tq]
        for r in range(1, r_heads):
            imp = imp + w[:, r * tq:(r + 1) * tq]
        imps.append(jnp.where(forced, -jnp.inf, jnp.where(blk > cur, NEG_INF, imp)))
    for _ in range(min(NSA_TOPK, n_blk) - 3):
        for gi in range(n_grp):
            best = jnp.max(imps[gi], axis=0, keepdims=True)
            first = jnp.min(jnp.where(imps[gi] == best, blk, n_blk), axis=0, keepdims=True)
            imps[gi] = jnp.where(blk == first, -jnp.inf, imps[gi])
    for gi in range(n_grp):
        sel_out[0, gi, 0:n_live, :] = jnp.where(imps[gi] == -jnp.inf, 0.0, SEL_OFF).astype(BF)
        if n_live < n_blk:
            sel_out[0, gi, n_live:n_blk, :] = jnp.full((n_blk - n_live, tq), SEL_OFF, BF)


def _cmp_select_kernel(q_ref, kc_ref, vc_ref, ov_ref, gate_ref, o_out, sel_out, *, tq, n_blk, n_grp):
    q0 = pl.program_id(2) * tq
    n_chunk = kc_ref.shape[2]
    n_need = jnp.minimum((q0 + tq - NSA_CMP_LEN) // NSA_CMP_STRIDE + 1, n_chunk - 1)
    n_steps = n_chunk // CMP_BLOCK
    need_steps = (n_need + CMP_BLOCK - 1) // CMP_BLOCK
    for k in range(1, n_steps + 1):
        @pl.when(need_steps == k)
        def _(k=k):
            _cmp_branch(k * CMP_BLOCK, q_ref, kc_ref, vc_ref, ov_ref, gate_ref, o_out, sel_out,
                        tq=tq, n_blk=n_blk, n_grp=n_grp)


def _overlap_matrix(s):
    n_chunk = s // NSA_CMP_STRIDE
    n_blk = s // NSA_SLC_LEN
    c0 = np.arange(n_chunk) * NSA_CMP_STRIDE
    c1 = c0 + NSA_CMP_LEN - 1
    b0 = np.arange(n_blk) * NSA_SLC_LEN
    ov = np.minimum(c1[None, :], b0[:, None] + NSA_SLC_LEN - 1) - np.maximum(c0[None, :], b0[:, None]) + 1
    return jnp.asarray(np.clip(ov, 0, None) / NSA_CMP_LEN, BF)


def _cmp_select(qT, k_cmp, v_cmpT, gates5, *, tq):
    b, _, s = qT.shape
    g = NSA_KV_HEADS
    r_heads = N_HEADS // g
    n_chunk = k_cmp.shape[2]
    n_blk = s // NSA_SLC_LEN
    ov = _overlap_matrix(s)
    n_grp = CMP_GROUPS
    rows = n_grp * r_heads * HEAD_DIM
    return pl.pallas_call(
        functools.partial(_cmp_select_kernel, tq=tq, n_blk=n_blk, n_grp=n_grp),
        grid=(b, g // n_grp, s // tq),
        in_specs=[
            pl.BlockSpec((1, rows, tq), lambda i, j, k: (i, j, k)),
            pl.BlockSpec((1, n_grp, n_chunk, HEAD_DIM), lambda i, j, k: (i, j, 0, 0)),
            pl.BlockSpec((1, n_grp, HEAD_DIM, n_chunk), lambda i, j, k: (i, j, 0, 0)),
            pl.BlockSpec((n_blk, n_chunk), lambda i, j, k: (0, 0)),
            pl.BlockSpec((1, 1, n_grp, r_heads, tq), lambda i, j, k: (i, 0, j, 0, k)),
        ],
        out_specs=[pl.BlockSpec((1, tq, rows), lambda i, j, k: (i, k, j)),
                   pl.BlockSpec((1, n_grp, n_blk, tq), lambda i, j, k: (i, j, 0, k))],
        out_shape=[jax.ShapeDtypeStruct((b, s, MIX_WIDTH), BF),
                   jax.ShapeDtypeStruct((b, g, n_blk, s), BF)],
        compiler_params=_params(("parallel", "parallel", "parallel")),
        name="nsa_cmp_select",
    )(qT, k_cmp, v_cmpT, ov, gates5)


def _flash_kernel(*refs, mode, n_par, n_rep, dq, kc, tq, tk, cw, window, n_blk, has_sink, has_gate,
                  fuse_out=None):
    it = iter(refs)
    q_ref, k_ref, v_ref = next(it), next(it), next(it)
    sel_ref = next(it) if n_blk else None
    sink_ref = next(it) if has_sink else None
    gate_ref = next(it) if has_gate else None
    qnext_ref = next(it) if mode == "causal" else None
    selnext_ref = next(it) if (mode == "causal" and n_blk) else None
    if fuse_out is None:
        out_ref = next(it)
    else:
        n_extra, has_next = fuse_out
        extra_refs = [next(it) for _ in range(n_extra)]
        z_ref, x_ref, w_ref = next(it), next(it), next(it)
        ng_ref = next(it) if has_next else None
        x_out = next(it)
        h_out = next(it) if has_next else None
    qs_ref, m_ref, acc_ref, mt_ref = (next(it) for _ in range(4))
    qn_ref = next(it) if mode == "causal" else None
    bias_ref = next(it) if mode == "window" else None
    s_bufs = tuple(it)

    nq = n_rep * tq
    items = [(p, c) for p in range(n_par) for c in range(nq // cw)]
    n_items = len(items)
    grp = pl.program_id(1)
    q0 = pl.program_id(2) * tq
    acc_row = lax.broadcasted_iota(jnp.int32, (V_ROWS, nq), 0)
    for p in range(n_par):
        for r in range(n_rep):
            hd = p * n_rep + r
            qs_ref[p, 0:dq, r * tq:(r + 1) * tq] = q_ref[0, hd * dq:(hd + 1) * dq, :]
        if n_blk:
            qs_ref[p, dq:dq + n_blk, :] = jnp.concatenate([sel_ref[0, p]] * n_rep, axis=1)
            if dq + n_blk < kc:
                qs_ref[p, dq + n_blk:kc, :] = jnp.zeros((kc - dq - n_blk, nq), BF)
        if has_sink:
            m_ref[p] = jnp.concatenate(
                [jnp.full((1, tq), sink_ref[(grp * n_par + p) * n_rep + r] * LOG2E, F32) for r in range(n_rep)],
                axis=1)
            acc_ref[p] = jnp.where(acc_row == HEAD_DIM, 1.0, 0.0)
        else:
            m_ref[p] = jnp.full((1, nq), M_INIT, F32)
            acc_ref[p] = jnp.zeros((V_ROWS, nq), F32)

    t_row = q0 + (lax.broadcasted_iota(jnp.int32, (1, nq), 1) & (tq - 1))
    one_row = jnp.where(lax.broadcasted_iota(jnp.int32, (V_ROWS - HEAD_DIM, tk), 0) == 0, 1.0, 0.0).astype(BF)

    def stage_a(item, j, key0, slot, causal, bias=None, pen=None):
        p, c = item
        cols = slice(c * cw, (c + 1) * cw)
        s = _dot(k_ref[0, p, j][:, :kc], qs_ref[p, :, cols])
        if causal:
            key = key0 + lax.broadcasted_iota(jnp.int32, (tk, 1), 0)
            s = jnp.where(key <= t_row[:, cols], s, NEG_INF)
        if bias is not None:
            s = s + bias_ref[bias]
        s_bufs[slot][...] = s
        mt = jnp.max(s, axis=0, keepdims=True)
        mt_ref[slot] = mt if pen is None else mt - pen

    def stage_b(item, j, slot, pen=None):
        p, c = item
        cols = slice(c * cw, (c + 1) * cw)
        m_old = m_ref[p, :, cols]
        m_new = jnp.maximum(m_old, mt_ref[slot])
        pr = jnp.exp2(s_bufs[slot][...] - (m_new if pen is None else m_new + pen)).astype(BF)
        alpha = jnp.exp2(m_old - m_new)
        v = jnp.concatenate([v_ref[0, j, p * HEAD_DIM:(p + 1) * HEAD_DIM, :], one_row], axis=0)
        acc_ref[p, :, cols] = alpha * acc_ref[p, :, cols] + _dot(v, pr)
        m_ref[p, :, cols] = m_new

    def finalize():
        outs = []
        for p in range(n_par):
            acc = acc_ref[p]
            o = acc[:HEAD_DIM] * (1.0 / acc[HEAD_DIM:HEAD_DIM + 1])
            if has_gate:
                o = o * _gate_row(gate_ref, n_rep, p)
            outs += [o[:, r * tq:(r + 1) * tq] for r in range(n_rep)]
        o_tok = jnp.concatenate(outs, axis=0).T
        if fuse_out is None:
            out_ref[0] = o_tok.astype(out_ref.dtype)
            return
        for r in extra_refs:
            o_tok = o_tok + r[0].astype(F32)
        y = _dot((o_tok * z_ref[0].astype(F32)).astype(BF), w_ref[...])
        x_new = x_ref[0] + y
        x_out[0] = x_new
        if has_next:
            h_out[0] = _rms_rows(x_new, ng_ref[...]).astype(BF)

    if mode == "causal":
        assert n_items % 2 == 0
        n_full = q0 // tk

        def prefetch_next_tile():
            assert cw == tq
            qn_ref[0:dq, :] = qnext_ref[0, 0:dq, :]
            if n_blk:
                qn_ref[dq:dq + n_blk, :] = selnext_ref[0, 0]
                if dq + n_blk < kc:
                    qn_ref[dq + n_blk:kc, :] = jnp.zeros((kc - dq - n_blk, tq), BF)
            s = _dot(k_ref[0, 0, 0][:, :kc], qn_ref[...])
            key = lax.broadcasted_iota(jnp.int32, (tk, 1), 0)
            s = jnp.where(key <= t_row[:, 0:cw] + tq, s, NEG_INF)
            s_bufs[0][...] = s
            mt_ref[0] = jnp.max(s, axis=0, keepdims=True)

        def step(j, kind, next_kind, last):
            for idx, item in enumerate(items):
                slot = idx % 2
                if idx + 1 < n_items:
                    stage_a(items[idx + 1], j, j * tk, 1 - slot, kind)
                elif not last:
                    stage_a(items[0], j + 1, (j + 1) * tk, 1 - slot, next_kind)
                else:
                    prefetch_next_tile()
                stage_b(item, j, slot)

        @pl.when(pl.program_id(2) == 0)
        def _():
            stage_a(items[0], 0, 0, 0, "causal")

        def body(j, carry):
            step(j, None, None, False)
            return carry

        def body_group(i, carry):
            for u in range(FLASH_UNROLL):
                step(FLASH_UNROLL * i + u, None, None, False)
            return carry

        n_main = jnp.maximum(n_full - 1, 0)
        n_groups = lax.shift_right_logical(n_main, int(np.log2(FLASH_UNROLL)))
        lax.fori_loop(0, n_groups, body_group, 0)
        lax.fori_loop(FLASH_UNROLL * n_groups, n_main, body, 0)

        @pl.when(n_full >= 1)
        def _():
            step(n_full - 1, None, "causal", False)
            step(n_full, "causal", None, True)
            finalize()

        @pl.when(n_full == 0)
        def _():
            step(0, "causal", None, True)
            finalize()
    else:
        w_tiles, q_tiles = window // tk, tq // tk
        assert w_tiles >= q_tiles and cw % tq == 0
        t_loc = lax.broadcasted_iota(jnp.int32, (1, cw), 1) & (tq - 1)
        key_loc = lax.broadcasted_iota(jnp.int32, (tk, 1), 0)
        for i in range(q_tiles):
            bias_ref[i] = jnp.where(i * tk + key_loc > t_loc, 0.0, NEG_INF)
            bias_ref[q_tiles + i] = jnp.where(i * tk + key_loc <= t_loc, 0.0, NEG_INF)
        work = []
        for i in range(w_tiles + q_tiles):
            jv = q0 // tk - w_tiles + i
            bias = i if i < q_tiles else (q_tiles + i - w_tiles if i >= w_tiles else None)
            pen = jnp.where(jv < 0, -NEG_INF, 0.0) if i < w_tiles else None
            work += [(item, jnp.maximum(jv, 0), bias, pen) for item in items]
        n_buf = len(s_bufs)
        ahead = WINDOW_LOOKAHEAD
        for n in range(min(ahead, len(work))):
            stage_a(work[n][0], work[n][1], 0, n % n_buf, False, work[n][2], work[n][3])
        for n, (item, j, _, pen) in enumerate(work):
            if n + ahead < len(work):
                nxt = work[n + ahead]
                stage_a(nxt[0], nxt[1], 0, (n + ahead) % n_buf, False, nxt[2], nxt[3])
            stage_b(item, j, n % n_buf, pen)

        finalize()


def _flash(qT, k_tok, vT_tiles, *, mode, n_par, n_rep, dq, kc, tq, tk, cw, window=None,
           sel_bias=None, sinks=None, gates=None, gate_branch=0, out_proj=None, single_buffer_v=False,
           name="flash"):
    b, _, s = qT.shape
    kh, k_lanes = k_tok.shape[1], k_tok.shape[3]
    n_t = s // tk
    k5 = k_tok.reshape(b, kh, n_t, tk, k_lanes)
    n_grp = kh // n_par
    heads = n_par * n_rep
    nq = n_rep * tq
    n_blk = sel_bias.shape[2] if sel_bias is not None else 0
    has_sink, has_gate = sinks is not None, gates is not None
    args = [qT, k5, vT_tiles]
    in_specs = [
        pl.BlockSpec((1, heads * dq, tq), lambda i, j, k: (i, j, k)),
        pl.BlockSpec((1, n_par, n_t, tk, k_lanes), lambda i, j, k: (i, j, 0, 0, 0)),
        pl.BlockSpec((1, n_t, n_par * HEAD_DIM, tk), lambda i, j, k: (i, 0, j, 0),
                     **({"pipeline_mode": pl.Buffered(1)} if single_buffer_v else {})),
    ]
    if n_blk:
        assert dq + n_blk <= kc
        args.append(sel_bias)
        in_specs.append(pl.BlockSpec((1, n_par, n_blk, tq), lambda i, j, k: (i, j, 0, k)))
    if has_sink:
        args.append(sinks.astype(F32))
        in_specs.append(pl.BlockSpec(memory_space=pltpu.SMEM))
    if has_gate:
        args.append(gates)
        in_specs.append(pl.BlockSpec((1, 1, n_par, n_rep, tq),
                                     lambda i, j, k, _br=gate_branch: (i, _br, j, 0, k)))
    n_sbuf = 2 if mode == "causal" else WINDOW_SCORE_BUFS
    scratch = [pltpu.VMEM((n_par, kc, nq), BF), pltpu.VMEM((n_par, 1, nq), F32),
               pltpu.VMEM((n_par, V_ROWS, nq), F32), pltpu.VMEM((n_sbuf, 1, cw), F32)]
    if mode == "causal":
        last_q = s // tq - 1
        args.append(qT)
        in_specs.append(pl.BlockSpec((1, dq, tq), lambda i, j, k: (i, j * heads, jnp.minimum(k + 1, last_q))))
        if n_blk:
            args.append(sel_bias)
            in_specs.append(pl.BlockSpec((1, 1, n_blk, tq),
                                         lambda i, j, k: (i, j * n_par, 0, jnp.minimum(k + 1, last_q))))
        scratch.append(pltpu.VMEM((kc, tq), BF))
    else:
        scratch.append(pltpu.VMEM((2 * (tq // tk), tk, cw), F32))
    scratch += [pltpu.VMEM((tk, cw), F32)] * n_sbuf
    out_specs = pl.BlockSpec((1, tq, heads * HEAD_DIM), lambda i, j, k: (i, k, j))
    out_shape = jax.ShapeDtypeStruct((b, s, MIX_WIDTH), BF)
    fuse_out = None
    if out_proj is not None:
        assert mode == "window" and heads * HEAD_DIM == MIX_WIDTH and n_grp == 1
        extra, zs, x, w_out, next_gain = out_proj
        tok = pl.BlockSpec((1, tq, D_MODEL), lambda i, j, k: (i, k, 0))
        args += list(extra) + [zs, x, w_out.astype(BF)]
        in_specs += [tok] * (len(extra) + 2) + [pl.BlockSpec(w_out.shape, lambda i, j, k: (0, 0))]
        out_specs, out_shape = [tok], [jax.ShapeDtypeStruct(x.shape, F32)]
        if next_gain is not None:
            args.append(next_gain.reshape(1, D_MODEL))
            in_specs.append(pl.BlockSpec((1, D_MODEL), lambda i, j, k: (0, 0)))
            out_specs.append(tok)
            out_shape.append(jax.ShapeDtypeStruct(x.shape, BF))
        fuse_out = (len(extra), next_gain is not None)
    kern = functools.partial(_flash_kernel, mode=mode, n_par=n_par, n_rep=n_rep, dq=dq, kc=kc, tq=tq, tk=tk,
                             cw=cw, window=window, n_blk=n_blk, has_sink=has_sink, has_gate=has_gate,
                             fuse_out=fuse_out)
    res = pl.pallas_call(
        kern,
        grid=(b, n_grp, s // tq),
        in_specs=in_specs,
        out_specs=out_specs,
        out_shape=out_shape,
        scratch_shapes=scratch,
        compiler_params=_params(("parallel", "parallel", "arbitrary")),
        name=name,
    )(*args)
    if out_proj is None:
        return res
    return (res[0], res[1]) if next_gain is not None else (res[0], None)


def _swa_proj_kernel(h_ref, cos_ref, sin_ref, wq_ref, wk_ref, wv_ref, wz_ref, qg_ref, kg_ref,
                     q_out, k_out, v_out, z_out, y_ref, *, win_tile):
    h = h_ref[0]
    cos, sin = cos_ref[0], sin_ref[0]
    y_ref[...] = _dot_nt(wq_ref[...], h)
    for hd in range(N_HEADS):
        rows = slice(hd * HEAD_DIM, (hd + 1) * HEAD_DIM)
        q_out[0, rows, :] = (_rope_t(_head_rms_t(y_ref[rows, :], qg_ref[...]), cos, sin) * Q_SCALE).astype(BF)
    kvd = SWA_KV_HEADS * HEAD_DIM
    y_ref[0:kvd, :] = _dot_nt(wk_ref[...], h)
    for gi in range(SWA_KV_HEADS):
        rows = slice(gi * HEAD_DIM, (gi + 1) * HEAD_DIM)
        k_out[0, gi] = _to_token_major(_rope_t(_head_rms_t(y_ref[rows, :], kg_ref[...]), cos, sin)).astype(BF)
    _store_lane_tiles(v_out, _dot_nt(wv_ref[...], h).astype(BF), win_tile)
    z_out[0] = _silu(_dot(h, wz_ref[...])).astype(BF)


def _swa_proj(h, cos, sin, w_in, q_gain, k_gain, *, win_tile):
    b, s, d = h.shape
    g = SWA_KV_HEADS
    kvd = g * HEAD_DIM
    tm = min(TOK_TILE, s)
    wt = w_in.T.astype(BF)
    wq, wk, wv = wt[:MIX_WIDTH], wt[MIX_WIDTH:MIX_WIDTH + kvd], wt[MIX_WIDTH + kvd:MIX_WIDTH + 2 * kvd]
    wz = w_in[:, MIX_WIDTH + 2 * kvd:].astype(BF)
    qg, kg = q_gain.reshape(HEAD_DIM, 1), k_gain.reshape(HEAD_DIM, 1)
    wpt = tm // win_tile
    half = HEAD_DIM // 2
    return pl.pallas_call(
        functools.partial(_swa_proj_kernel, win_tile=win_tile),
        grid=(b, s // tm),
        in_specs=[pl.BlockSpec((1, tm, d), lambda i, j: (i, j, 0)),
                  pl.BlockSpec((1, half, tm), lambda i, j: (i, 0, j)),
                  pl.BlockSpec((1, half, tm), lambda i, j: (i, 0, j)),
                  _full(wq.shape), _full(wk.shape), _full(wv.shape), _full(wz.shape),
                  _full(qg.shape), _full(kg.shape)],
        out_specs=[pl.BlockSpec((1, MIX_WIDTH, tm), lambda i, j: (i, 0, j)),
                   pl.BlockSpec((1, g, tm, KEY_PAD), lambda i, j: (i, 0, j, 0)),
                   pl.BlockSpec((1, wpt, kvd, win_tile), lambda i, j: (i, j, 0, 0)),
                   pl.BlockSpec((1, tm, MIX_WIDTH), lambda i, j: (i, j, 0))],
        out_shape=[jax.ShapeDtypeStruct((b, MIX_WIDTH, s), BF),
                   jax.ShapeDtypeStruct((b, g, s, KEY_PAD), BF),
                   jax.ShapeDtypeStruct((b, s // win_tile, kvd, win_tile), BF),
                   jax.ShapeDtypeStruct((b, s, MIX_WIDTH), BF)],
        scratch_shapes=[pltpu.VMEM((MIX_WIDTH, tm), F32)],
        compiler_params=_params(("parallel", "parallel")),
        name="swa_proj",
    )(h, cos, sin, wq, wk, wv, wz, qg, kg)


def _split3(x):
    hi = x.astype(BF)
    r1 = x - hi.astype(F32)
    mid = r1.astype(BF)
    lo = (r1 - mid.astype(F32)).astype(BF)
    return hi, mid, lo


def _fox_proj_kernel(h_ref, wf_ref, bias_ref, tri_ref, wq_ref, wk_ref, wv_ref, wz_ref, qg_ref, kg_ref,
                     q_out, k_out, v_out, z_out, yq_ref, yk_ref, cum_ref, carry_ref, *, tile):
    @pl.when(pl.program_id(1) == 0)
    def _():
        carry_ref[...] = jnp.zeros_like(carry_ref)

    h = h_ref[0]
    tm = h.shape[0]
    x = _dot_nt(wf_ref[...], h) + bias_ref[...]
    logf = jnp.minimum(x, 0.0) - jnp.log(1.0 + jnp.exp(-jnp.abs(x)))
    tri = tri_ref[...]
    hi, mid, lo = _split3(logf)
    cum = (_dot(hi, tri) + _dot(mid, tri)) + _dot(lo, tri) + carry_ref[:, 0:1]
    carry_ref[...] = jnp.broadcast_to(cum[:, -1:], carry_ref.shape)
    cum_ref[...] = cum * LOG2E

    yq_ref[...] = _dot_nt(wq_ref[...], h)
    yk_ref[...] = _dot_nt(wk_ref[...], h)
    row = lax.broadcasted_iota(jnp.int32, (8, tm), 0)
    zeros = jnp.zeros((KEY_PAD - HEAD_DIM - 16, tm), F32)
    for hd in range(N_HEADS):
        rows = slice(hd * HEAD_DIM, (hd + 1) * HEAD_DIM)
        c_hi, c_mid, c_lo = (c.astype(F32) for c in _split3(cum_ref[hd:hd + 1, :]))
        c3 = jnp.where(row == 0, c_hi, jnp.where(row == 1, c_mid, jnp.where(row == 2, c_lo, 0.0)))
        one3 = jnp.where(row < 3, 1.0, 0.0)
        q = _head_rms_t(yq_ref[rows, :], qg_ref[...]) * Q_SCALE
        q_out[0, hd] = jnp.concatenate([q, one3, c3, zeros], axis=0).astype(BF)
        k = _head_rms_t(yk_ref[rows, :], kg_ref[...])
        k_out[0, hd] = jnp.concatenate([k, -c3, one3, zeros], axis=0).T.astype(BF)
    _store_lane_tiles(v_out, _dot_nt(wv_ref[...], h).astype(BF), tile)
    z_out[0] = _silu(_dot(h, wz_ref[...])).astype(BF)


def _fox_proj(h, w_in, forget_bias, q_gain, k_gain, *, tile):
    b, s, d = h.shape
    tm = min(TOK_TILE, s)
    wt = w_in.T.astype(BF)
    wq, wk, wv = wt[:MIX_WIDTH], wt[MIX_WIDTH:2 * MIX_WIDTH], wt[2 * MIX_WIDTH:3 * MIX_WIDTH]
    wf = wt[3 * MIX_WIDTH:3 * MIX_WIDTH + N_HEADS]
    wz = w_in[:, 3 * MIX_WIDTH + N_HEADS:].astype(BF)
    qg, kg = q_gain.reshape(HEAD_DIM, 1), k_gain.reshape(HEAD_DIM, 1)
    tri = jnp.asarray(np.arange(tm)[:, None] <= np.arange(tm)[None, :], BF)
    return pl.pallas_call(
        functools.partial(_fox_proj_kernel, tile=tile),
        grid=(b, s // tm),
        in_specs=[pl.BlockSpec((1, tm, d), lambda i, j: (i, j, 0)),
                  _full(wf.shape), _full((N_HEADS, 1)), _full(tri.shape),
                  _full(wq.shape), _full(wk.shape), _full(wv.shape), _full(wz.shape),
                  _full(qg.shape), _full(kg.shape)],
        out_specs=[pl.BlockSpec((1, N_HEADS, KEY_PAD, tm), lambda i, j: (i, 0, 0, j)),
                   pl.BlockSpec((1, N_HEADS, tm, KEY_PAD), lambda i, j: (i, 0, j, 0)),
                   pl.BlockSpec((1, tm // tile, MIX_WIDTH, tile), lambda i, j: (i, j, 0, 0)),
                   pl.BlockSpec((1, tm, MIX_WIDTH), lambda i, j: (i, j, 0))],
        out_shape=[jax.ShapeDtypeStruct((b, N_HEADS, KEY_PAD, s), BF),
                   jax.ShapeDtypeStruct((b, N_HEADS, s, KEY_PAD), BF),
                   jax.ShapeDtypeStruct((b, s // tile, MIX_WIDTH, tile), BF),
                   jax.ShapeDtypeStruct((b, s, MIX_WIDTH), BF)],
        scratch_shapes=[pltpu.VMEM((MIX_WIDTH, tm), F32), pltpu.VMEM((MIX_WIDTH, tm), F32),
                        pltpu.VMEM((N_HEADS, tm), F32), pltpu.VMEM((N_HEADS, LANES), F32)],
        compiler_params=_params(("parallel", "arbitrary")),
        name="fox_proj",
    )(h, wf, forget_bias.reshape(N_HEADS, 1).astype(F32), tri, wq, wk, wv, wz, qg, kg)


def _out_proj_kernel(*refs, n_o, has_next):
    o_refs = refs[:n_o]
    z_ref, x_ref, w_ref = refs[n_o:n_o + 3]
    rest = refs[n_o + 3:]
    if has_next:
        g_ref, x_out, h_out = rest
    else:
        (x_out,) = rest
    o = o_refs[0][0].astype(F32)
    for r in o_refs[1:]:
        o = o + r[0].astype(F32)
    y = _dot((o * z_ref[0].astype(F32)).astype(BF), w_ref[...])
    x_new = x_ref[0] + y
    x_out[0] = x_new
    if has_next:
        h_out[0] = _rms_rows(x_new, g_ref[...]).astype(BF)


def _out_proj(o_list, zs, x, w_out, next_gain):
    b, s, d = x.shape
    tm = min(TOK_TILE, s)
    has_next = next_gain is not None
    blk = pl.BlockSpec((1, tm, d), lambda i, j: (i, j, 0))
    args = list(o_list) + [zs, x, w_out.astype(BF)]
    in_specs = [blk] * (len(o_list) + 2) + [_full(w_out.shape)]
    out_shape = [jax.ShapeDtypeStruct((b, s, d), F32)]
    out_specs = [blk]
    if has_next:
        args.append(next_gain.reshape(1, d))
        in_specs.append(_full((1, d)))
        out_shape.append(jax.ShapeDtypeStruct((b, s, d), BF))
        out_specs.append(blk)
    res = pl.pallas_call(
        functools.partial(_out_proj_kernel, n_o=len(o_list), has_next=has_next),
        grid=(b, s // tm), in_specs=in_specs, out_specs=out_specs, out_shape=out_shape,
        compiler_params=_params(("parallel", "parallel")),
        name="out_proj",
    )(*args)
    return (res[0], res[1]) if has_next else (res[0], None)


ITEM_LANES = 512
CMP_Q_TILE = 512
CMP_GROUPS = 2
NSA_SLC_Q_TILE = 512
NSA_SLC_TILE = 512
NSA_SLC_GROUPS = 4
NSA_WIN_Q_TILE = 256
NSA_WIN_TILE = 256
SWA_TILE = 128
FOX_Q_TILE = 512
FOX_K_TILE = 512
FOX_HEADS_PER_STEP = 8


def _nsa_mixer(x, h, cos, sin, w_in, q_gain, k_gain, cmp_pos, cmp_w1, cmp_w2, w_out, next_gain, pre_gain):
    b, s, _ = x.shape
    g = NSA_KV_HEADS
    r = N_HEADS // g
    qT, kc_tok, vc_tok, ks, kw, vsT, vwT, gates, zs = _nsa_proj(
        x if h is None else h, cos, sin, w_in, q_gain, k_gain, slc_tile=NSA_SLC_TILE, win_tile=NSA_WIN_TILE,
        pre_gain=pre_gain if h is None else None)
    k_cmp, v_cmpT = _compress(kc_tok, vc_tok, cmp_pos, cmp_w1, cmp_w2, k_gain[0])
    gates5 = gates.reshape(b, 3, g, r, s)
    o_cmp, sel_bias = _cmp_select(qT, k_cmp, v_cmpT, gates5, tq=CMP_Q_TILE)
    o_slc = _flash(qT, ks, vsT, mode="causal", n_par=NSA_SLC_GROUPS, n_rep=r, dq=HEAD_DIM, kc=SLC_KEY_LANES,
                   tq=NSA_SLC_Q_TILE, tk=NSA_SLC_TILE, cw=ITEM_LANES, sel_bias=sel_bias,
                   gates=gates5, gate_branch=1, single_buffer_v=True, name="nsa_selected")
    return _flash(qT, kw, vwT, mode="window", n_par=g, n_rep=r, dq=HEAD_DIM, kc=HEAD_DIM,
                  tq=NSA_WIN_Q_TILE, tk=NSA_WIN_TILE, cw=ITEM_LANES, window=NSA_WINDOW,
                  gates=gates5, gate_branch=2, out_proj=([o_cmp, o_slc], zs, x, w_out, next_gain),
                  name="nsa_window_out")


def _swa_mixer(x, h, cos, sin, w_in, q_gain, k_gain, sinks, w_out, next_gain):
    r = N_HEADS // SWA_KV_HEADS
    qT, k, vT, zs = _swa_proj(h, cos, sin, w_in, q_gain, k_gain, win_tile=SWA_TILE)
    return _flash(qT, k, vT, mode="window", n_par=SWA_KV_HEADS, n_rep=r, dq=HEAD_DIM, kc=HEAD_DIM, tq=SWA_TILE,
                  tk=SWA_TILE, cw=r * SWA_TILE, window=SWA_WINDOW, sinks=sinks,
                  out_proj=([], zs, x, w_out, next_gain), name="swa_window_out")


def _fox_mixer(x, h, w_in, forget_bias, q_gain, k_gain, w_out, next_gain):
    b, s, _ = h.shape
    qT, k, vT, zs = _fox_proj(h, w_in, forget_bias, q_gain, k_gain, tile=FOX_K_TILE)
    o = _flash(qT.reshape(b, N_HEADS * KEY_PAD, s), k, vT, mode="causal", n_par=FOX_HEADS_PER_STEP, n_rep=1, dq=KEY_PAD,
               kc=KEY_PAD, tq=FOX_Q_TILE, tk=FOX_K_TILE, cw=ITEM_LANES, single_buffer_v=True,
               name="fox_attention")
    return _out_proj([o], zs, x, w_out, next_gain)


def kernel(x, positions, norm_gains, a_w_in, a_q_gain, a_k_gain, a_cmp_pos, a_cmp_w1, a_cmp_w2, a_w_out,
           b_w_in, b_q_gain, b_k_gain, b_sinks, b_w_out,
           c_w_in, c_forget_bias, c_q_gain, c_k_gain, c_w_out):
    depth = norm_gains.shape[0]
    cos, sin = _rope_tables(positions)
    h = None
    for i in range(depth):
        j, mixer = divmod(i, 3)
        next_gain = norm_gains[i + 1] if i + 1 < depth else None
        if mixer == 0:
            x, h = _nsa_mixer(x, h, cos, sin, a_w_in[j], a_q_gain[j], a_k_gain[j],
                              a_cmp_pos[j], a_cmp_w1[j], a_cmp_w2[j], a_w_out[j], next_gain, norm_gains[i])
        elif mixer == 1:
            x, h = _swa_mixer(x, h, cos, sin, b_w_in[j], b_q_gain[j], b_k_gain[j], b_sinks[j],
                              b_w_out[j], next_gain)
        else:
            x, h = _fox_mixer(x, h, c_w_in[j], c_forget_bias[j], c_q_gain[j], c_k_gain[j],
                              c_w_out[j], next_gain)
    return x
```

```python
import functools

import jax
import jax.numpy as jnp
import numpy as np
from jax import lax
from jax.experimental import pallas as pl
from jax.experimental.pallas import tpu as pltpu

D_MODEL = 1024
HEAD_DIM = 64
N_HEADS = 16
MIX_WIDTH = N_HEADS * HEAD_DIM
ROPE_THETA = 10000.0
EPS = 1e-6
SCALE = HEAD_DIM ** -0.5
NEG_INF = -1e30
M_INIT = -1e29
SEL_OFF = -(2.0 ** 100)

NSA_KV_HEADS = 4
NSA_CMP_LEN = 32
NSA_CMP_STRIDE = 16
NSA_SLC_LEN = 64
NSA_TOPK = 16
NSA_WINDOW = 512
SWA_KV_HEADS = 2
SWA_WINDOW = 128

LOG2E = float(np.log2(np.e))
Q_SCALE = SCALE * LOG2E

LANES = 128
KEY_PAD = 128
SLC_KEY_LANES = 256
V_ROWS = 80
FLASH_UNROLL = 4
WINDOW_LOOKAHEAD = 2
WINDOW_SCORE_BUFS = 4
CMP_BLOCK = 128
CMP_MASK_ROWS = 2 * CMP_BLOCK
VMEM_LIMIT = 56 * 1024 * 1024

TOK_TILE = 512
NT_DIMS = (((1,), (1,)), ((), ()))

BF = jnp.bfloat16
F32 = jnp.float32


def _params(sem):
    return pltpu.CompilerParams(dimension_semantics=sem, vmem_limit_bytes=VMEM_LIMIT)


def _dot(a, b):
    return jnp.dot(a, b, preferred_element_type=F32)


def _dot_nt(a, b):
    return lax.dot_general(a, b, NT_DIMS, preferred_element_type=F32)


def _rope_tab_kernel(pos_ref, invf_ref, cos_ref, sin_ref):
    ang = invf_ref[...] * pos_ref[0].astype(F32)
    cos_ref[0] = jnp.cos(ang)
    sin_ref[0] = jnp.sin(ang)


def _rope_tables(positions):
    b, s = positions.shape
    half = HEAD_DIM // 2
    inv_freq = ROPE_THETA ** (-jnp.arange(half, dtype=F32) * 2.0 / HEAD_DIM)
    tm = min(TOK_TILE, s)
    out = jax.ShapeDtypeStruct((b, half, s), F32)
    return pl.pallas_call(
        _rope_tab_kernel,
        grid=(b, s // tm),
        in_specs=[pl.BlockSpec((1, 1, tm), lambda i, j: (i, 0, j)),
                  pl.BlockSpec((half, 1), lambda i, j: (0, 0))],
        out_specs=[pl.BlockSpec((1, half, tm), lambda i, j: (i, 0, j))] * 2,
        out_shape=[out, out],
        compiler_params=_params(("parallel", "parallel")),
        name="rope_tables",
    )(positions.reshape(b, 1, s), inv_freq.reshape(half, 1))


def _rms_rows(x, gain_row):
    y = x * lax.rsqrt(jnp.mean(x * x, axis=-1, keepdims=True) + EPS)
    return y * gain_row


def _head_rms_t(y, gain_col):
    ms = jnp.mean(y * y, axis=0, keepdims=True)
    return (y * lax.rsqrt(ms + EPS)) * gain_col


def _rope_t(y, cos, sin):
    half = HEAD_DIM // 2
    x1, x2 = y[:half], y[half:]
    return jnp.concatenate([x1 * cos - x2 * sin, x2 * cos + x1 * sin], axis=0)


def _to_token_major(y):
    pad = jnp.zeros((KEY_PAD - y.shape[0], y.shape[1]), y.dtype)
    return jnp.concatenate([y, pad], axis=0).T


def _silu(z):
    return z * (1.0 / (1.0 + jnp.exp(-z)))


def _sigmoid(z):
    return 1.0 / (1.0 + jnp.exp(-z))


def _store_lane_tiles(out_ref, y, tile):
    for c in range(y.shape[1] // tile):
        out_ref[0, c] = y[:, c * tile:(c + 1) * tile]


def _store_chunk_rows(out_ref, gi, yt, tok_ref):
    tok_ref[...] = yt
    n = yt.shape[0] // NSA_CMP_STRIDE
    for m in range(NSA_CMP_STRIDE // 2):
        even = tok_ref[pl.ds(2 * m, n, stride=NSA_CMP_STRIDE), :]
        odd = tok_ref[pl.ds(2 * m + 1, n, stride=NSA_CMP_STRIDE), :]
        out_ref[0, gi, :, KEY_PAD * m:KEY_PAD * (m + 1)] = even + pltpu.roll(odd, HEAD_DIM, 1)


def _nsa_proj_kernel(*refs, slc_tile, win_tile, norm_in):
    it = iter(refs)
    h_ref = next(it)
    ng_ref = next(it) if norm_in else None
    (cos_ref, sin_ref, wq_ref, wk_ref, wv_ref, wg_ref, wz_ref, qg_ref, kg_ref,
     q_out, kc_out, vc_out, ks_out, kw_out, vs_out, vw_out, g_out, z_out, y_ref, tok_ref) = it
    h = _rms_rows(h_ref[0], ng_ref[...]).astype(BF) if norm_in else h_ref[0]
    cos, sin = cos_ref[0], sin_ref[0]
    g = NSA_KV_HEADS
    kvd = g * HEAD_DIM
    y_ref[...] = _dot_nt(wq_ref[...], h)
    for hd in range(N_HEADS):
        rows = slice(hd * HEAD_DIM, (hd + 1) * HEAD_DIM)
        y = _rope_t(_head_rms_t(y_ref[rows, :], qg_ref[...]), cos, sin) * Q_SCALE
        q_out[0, rows, :] = y.astype(BF)
    y_ref[0:3 * kvd, :] = _dot_nt(wk_ref[...], h)
    tm = h.shape[0]
    tok = pl.program_id(1) * tm + lax.broadcasted_iota(jnp.int32, (tm, 1), 0)
    blk_lane = HEAD_DIM + lax.shift_right_logical(tok, int(np.log2(NSA_SLC_LEN)))
    blk_hot = lax.broadcasted_iota(jnp.int32, (1, SLC_KEY_LANES), 1) == blk_lane
    for kind in range(3):
        for gi in range(g):
            r0 = (kind * g + gi) * HEAD_DIM
            y = y_ref[r0:r0 + HEAD_DIM, :]
            if kind > 0:
                y = _head_rms_t(y, kg_ref[:, kind:kind + 1])
            yt = _to_token_major(_rope_t(y, cos, sin))
            if kind == 0:
                _store_chunk_rows(kc_out, gi, yt, tok_ref)
            elif kind == 1:
                wide = jnp.concatenate([yt, jnp.zeros((tm, SLC_KEY_LANES - KEY_PAD), F32)], axis=1)
                ks_out[0, gi] = jnp.where(blk_hot, 1.0, wide).astype(BF)
            else:
                kw_out[0, gi] = yt.astype(BF)
    y_ref[0:3 * kvd, :] = _dot_nt(wv_ref[...], h)
    for gi in range(g):
        r0 = gi * HEAD_DIM
        _store_chunk_rows(vc_out, gi, _to_token_major(y_ref[r0:r0 + HEAD_DIM, :]), tok_ref)
    _store_lane_tiles(vs_out, y_ref[kvd:2 * kvd, :].astype(BF), slc_tile)
    _store_lane_tiles(vw_out, y_ref[2 * kvd:3 * kvd, :].astype(BF), win_tile)
    g_out[0] = _sigmoid(_dot_nt(wg_ref[...], h))
    z_out[0] = _silu(_dot(h, wz_ref[...])).astype(BF)


def _full(shape):
    nd = len(shape)
    return pl.BlockSpec(shape, lambda i, j, _n=nd: (0,) * _n)


def _nsa_proj(h, cos, sin, w_in, q_gain, k_gain, *, slc_tile, win_tile, pre_gain=None):
    b, s, d = h.shape
    g = NSA_KV_HEADS
    kvd = g * HEAD_DIM
    tm = min(TOK_TILE, s)
    flat = NSA_CMP_STRIDE * HEAD_DIM
    sizes = [MIX_WIDTH] + [kvd] * 6 + [3 * N_HEADS]
    off = np.cumsum([0] + sizes)
    wt = w_in.T.astype(BF)
    wq = wt[off[0]:off[1]]
    wk = jnp.concatenate([wt[off[1]:off[2]], wt[off[3]:off[4]], wt[off[5]:off[6]]], axis=0)
    wv = jnp.concatenate([wt[off[2]:off[3]], wt[off[4]:off[5]], wt[off[6]:off[7]]], axis=0)
    wg = wt[off[7]:off[8]]
    wz = w_in[:, off[8]:].astype(BF)
    qg = q_gain.reshape(HEAD_DIM, 1)
    kg = k_gain.T
    n_t = s // tm
    out_shape = [
        jax.ShapeDtypeStruct((b, MIX_WIDTH, s), BF),
        jax.ShapeDtypeStruct((b, g, s // NSA_CMP_STRIDE, flat), F32),
        jax.ShapeDtypeStruct((b, g, s // NSA_CMP_STRIDE, flat), F32),
        jax.ShapeDtypeStruct((b, g, s, SLC_KEY_LANES), BF),
        jax.ShapeDtypeStruct((b, g, s, KEY_PAD), BF),
        jax.ShapeDtypeStruct((b, s // slc_tile, kvd, slc_tile), BF),
        jax.ShapeDtypeStruct((b, s // win_tile, kvd, win_tile), BF),
        jax.ShapeDtypeStruct((b, 3 * N_HEADS, s), F32),
        jax.ShapeDtypeStruct((b, s, MIX_WIDTH), BF),
    ]
    out_specs = [
        pl.BlockSpec((1, MIX_WIDTH, tm), lambda i, j: (i, 0, j)),
        pl.BlockSpec((1, g, tm // NSA_CMP_STRIDE, flat), lambda i, j: (i, 0, j, 0)),
        pl.BlockSpec((1, g, tm // NSA_CMP_STRIDE, flat), lambda i, j: (i, 0, j, 0)),
        pl.BlockSpec((1, g, tm, SLC_KEY_LANES), lambda i, j: (i, 0, j, 0)),
        pl.BlockSpec((1, g, tm, KEY_PAD), lambda i, j: (i, 0, j, 0)),
        pl.BlockSpec((1, tm // slc_tile, kvd, slc_tile), lambda i, j: (i, j, 0, 0)),
        pl.BlockSpec((1, tm // win_tile, kvd, win_tile), lambda i, j: (i, j, 0, 0)),
        pl.BlockSpec((1, 3 * N_HEADS, tm), lambda i, j: (i, 0, j)),
        pl.BlockSpec((1, tm, MIX_WIDTH), lambda i, j: (i, j, 0)),
    ]
    norm_in = pre_gain is not None
    args = [h] + ([pre_gain.reshape(1, d)] if norm_in else []) + [cos, sin, wq, wk, wv, wg, wz, qg, kg]
    in_specs = [pl.BlockSpec((1, tm, d), lambda i, j: (i, j, 0))] + ([_full((1, d))] if norm_in else []) + [
        pl.BlockSpec((1, HEAD_DIM // 2, tm), lambda i, j: (i, 0, j)),
        pl.BlockSpec((1, HEAD_DIM // 2, tm), lambda i, j: (i, 0, j)),
        _full(wq.shape), _full(wk.shape), _full(wv.shape), _full(wg.shape), _full(wz.shape),
        _full(qg.shape), _full(kg.shape),
    ]
    return pl.pallas_call(
        functools.partial(_nsa_proj_kernel, slc_tile=slc_tile, win_tile=win_tile, norm_in=norm_in),
        grid=(b, n_t), in_specs=in_specs, out_specs=out_specs, out_shape=out_shape,
        scratch_shapes=[pltpu.VMEM((MIX_WIDTH, tm), F32), pltpu.VMEM((tm, KEY_PAD), F32)],
        compiler_params=_params(("parallel", "parallel")),
        name="nsa_proj",
    )(*args)


def _gelu_tanh(x):
    c = np.float32(np.sqrt(2.0 / np.pi))
    return 0.5 * x * (1.0 + jnp.tanh(c * (x + 0.044715 * (x * x * x))))


def _compress_kernel(kc_ref, vc_ref, pos_ref, w1_ref, w2_ref, kg_ref, kcmp_out, vcmp_out):
    for which, (src, dst) in enumerate(((kc_ref, kcmp_out), (vc_ref, vcmp_out))):
        x = src[0, 0]
        n = x.shape[0]
        half = x.shape[1]
        xa = (x + pos_ref[which, 0:1, :]).astype(BF)
        xb = (x + pos_ref[which, 1:2, :]).astype(BF)
        ua = _dot(xa, w1_ref[which, :half, :])
        ub = _dot(xb, w1_ref[which, half:, :])
        row = lax.broadcasted_iota(jnp.int32, (n, 1), 0)
        ub_next = jnp.where(row == n - 1, 0.0, pltpu.roll(ub, n - 1, 0))
        hid = _gelu_tanh(ua + ub_next)
        y = _dot(hid.astype(BF), w2_ref[which])
        if which == 0:
            y = _rms_rows(y, kg_ref[...])
            dst[0, 0] = y.astype(BF)
        else:
            pad = jnp.zeros((n, KEY_PAD - HEAD_DIM), F32)
            dst[0, 0] = jnp.concatenate([y, pad], axis=1).T[:HEAD_DIM].astype(BF)


def _compress(kc, vc, cmp_pos, cmp_w1, cmp_w2, k_gain0):
    b, g, n_chunk, flat = kc.shape
    pos = cmp_pos.reshape(2, 2, flat)
    w1 = cmp_w1.astype(BF)
    w2 = cmp_w2.astype(BF)
    blk = pl.BlockSpec((1, 1, n_chunk, flat), lambda i, j: (i, j, 0, 0))
    return pl.pallas_call(
        _compress_kernel,
        grid=(b, g),
        in_specs=[blk, blk, _full(pos.shape), _full(w1.shape), _full(w2.shape),
                  _full((1, HEAD_DIM))],
        out_specs=[pl.BlockSpec((1, 1, n_chunk, HEAD_DIM), lambda i, j: (i, j, 0, 0)),
                   pl.BlockSpec((1, 1, HEAD_DIM, n_chunk), lambda i, j: (i, j, 0, 0))],
        out_shape=[jax.ShapeDtypeStruct((b, g, n_chunk, HEAD_DIM), BF),
                   jax.ShapeDtypeStruct((b, g, HEAD_DIM, n_chunk), BF)],
        compiler_params=_params(("parallel", "parallel")),
        name="nsa_compress",
    )(kc, vc, pos, w1, w2, k_gain0.reshape(1, HEAD_DIM))


def _gate_row(gate_ref, n_heads, grp=0):
    return jnp.concatenate([gate_ref[0, 0, grp, r:r + 1, :] for r in range(n_heads)], axis=1)


def _cmp_branch(rows, q_ref, kc_ref, vc_ref, ov_ref, gate_ref, o_out, sel_out, *, tq, n_blk, n_grp):
    r_heads = N_HEADS // NSA_KV_HEADS
    nq = r_heads * tq
    width = r_heads * HEAD_DIM
    q0 = pl.program_id(2) * tq
    n_cmp = kc_ref.shape[2] - 1
    t_row = q0 + (lax.broadcasted_iota(jnp.int32, (1, nq), 1) & (tq - 1))
    lo = max(rows - CMP_MASK_ROWS, 0)
    c_col = lo + lax.broadcasted_iota(jnp.int32, (rows - lo, 1), 0)
    valid = (c_col * NSA_CMP_STRIDE + (NSA_CMP_LEN - 1) <= t_row) & (c_col < n_cmp)
    one_row = jnp.where(lax.broadcasted_iota(jnp.int32, (V_ROWS - HEAD_DIM, rows), 0) == 0, 1.0, 0.0).astype(BF)
    n_live = min(n_blk, rows * NSA_CMP_STRIDE // NSA_SLC_LEN + 8)
    t1 = q0 + lax.broadcasted_iota(jnp.int32, (1, tq), 1)
    cur = lax.shift_right_logical(t1, int(np.log2(NSA_SLC_LEN)))
    blk = lax.broadcasted_iota(jnp.int32, (n_live, tq), 0)
    forced = (blk == 0) | (blk == cur) | (blk == cur - 1)
    imps = []
    for gi in range(n_grp):
        q4 = jnp.concatenate([q_ref[0, (gi * r_heads + r) * HEAD_DIM:(gi * r_heads + r + 1) * HEAD_DIM, :]
                              for r in range(r_heads)], axis=1)
        s = _dot(kc_ref[0, gi, 0:rows, :], q4)
        s_new = jnp.where(valid, s[lo:], NEG_INF)
        m = jnp.max(s_new, axis=0, keepdims=True)
        if lo:
            m = jnp.maximum(m, jnp.max(s[:lo], axis=0, keepdims=True))
        e = jnp.where(valid, jnp.exp2(s_new - m), 0.0).astype(BF)
        if lo:
            e = jnp.concatenate([jnp.exp2(s[:lo] - m).astype(BF), e], axis=0)
        lhs = jnp.concatenate([vc_ref[0, gi, :, 0:rows], one_row, ov_ref[0:n_live, 0:rows]], axis=0)
        res = _dot(lhs, e)
        l = res[HEAD_DIM:HEAD_DIM + 1]
        inv = jnp.where(l > 0.0, 1.0 / jnp.where(l > 0.0, l, 1.0), 0.0)
        o = res[:HEAD_DIM] * (inv * _gate_row(gate_ref, r_heads, gi))
        o_rows = jnp.concatenate([o[:, r * tq:(r + 1) * tq] for r in range(r_heads)], axis=0)
        o_out[0, :, gi * width:(gi + 1) * width] = o_rows.T.astype(o_out.dtype)
        w = res[V_ROWS:V_ROWS + n_live] * inv
        imp = w[:, 0:tq]
        for r in range(1, r_heads):
            imp = imp + w[:, r * tq:(r + 1) * tq]
        imps.append(jnp.where(forced, -jnp.inf, jnp.where(blk > cur, NEG_INF, imp)))
    for _ in range(min(NSA_TOPK, n_blk) - 3):
        for gi in range(n_grp):
            best = jnp.max(imps[gi], axis=0, keepdims=True)
            first = jnp.min(jnp.where(imps[gi] == best, blk, n_blk), axis=0, keepdims=True)
            imps[gi] = jnp.where(blk == first, -jnp.inf, imps[gi])
    for gi in range(n_grp):
        sel_out[0, gi, 0:n_live, :] = jnp.where(imps[gi] == -jnp.inf, 0.0, SEL_OFF).astype(BF)
        if n_live < n_blk:
            sel_out[0, gi, n_live:n_blk, :] = jnp.full((n_blk - n_live, tq), SEL_OFF, BF)


def _cmp_select_kernel(q_ref, kc_ref, vc_ref, ov_ref, gate_ref, o_out, sel_out, *, tq, n_blk, n_grp):
    q0 = pl.program_id(2) * tq
    n_chunk = kc_ref.shape[2]
    n_need = jnp.minimum((q0 + tq - NSA_CMP_LEN) // NSA_CMP_STRIDE + 1, n_chunk - 1)
    n_steps = n_chunk // CMP_BLOCK
    need_steps = (n_need + CMP_BLOCK - 1) // CMP_BLOCK
    for k in range(1, n_steps + 1):
        @pl.when(need_steps == k)
        def _(k=k):
            _cmp_branch(k * CMP_BLOCK, q_ref, kc_ref, vc_ref, ov_ref, gate_ref, o_out, sel_out,
                        tq=tq, n_blk=n_blk, n_grp=n_grp)


def _overlap_matrix(s):
    n_chunk = s // NSA_CMP_STRIDE
    n_blk = s // NSA_SLC_LEN
    c0 = np.arange(n_chunk) * NSA_CMP_STRIDE
    c1 = c0 + NSA_CMP_LEN - 1
    b0 = np.arange(n_blk) * NSA_SLC_LEN
    ov = np.minimum(c1[None, :], b0[:, None] + NSA_SLC_LEN - 1) - np.maximum(c0[None, :], b0[:, None]) + 1
    return jnp.asarray(np.clip(ov, 0, None) / NSA_CMP_LEN, BF)


def _cmp_select(qT, k_cmp, v_cmpT, gates5, *, tq):
    b, _, s = qT.shape
    g = NSA_KV_HEADS
    r_heads = N_HEADS // g
    n_chunk = k_cmp.shape[2]
    n_blk = s // NSA_SLC_LEN
    ov = _overlap_matrix(s)
    n_grp = CMP_GROUPS
    rows = n_grp * r_heads * HEAD_DIM
    return pl.pallas_call(
        functools.partial(_cmp_select_kernel, tq=tq, n_blk=n_blk, n_grp=n_grp),
        grid=(b, g // n_grp, s // tq),
        in_specs=[
            pl.BlockSpec((1, rows, tq), lambda i, j, k: (i, j, k)),
            pl.BlockSpec((1, n_grp, n_chunk, HEAD_DIM), lambda i, j, k: (i, j, 0, 0)),
            pl.BlockSpec((1, n_grp, HEAD_DIM, n_chunk), lambda i, j, k: (i, j, 0, 0)),
            pl.BlockSpec((n_blk, n_chunk), lambda i, j, k: (0, 0)),
            pl.BlockSpec((1, 1, n_grp, r_heads, tq), lambda i, j, k: (i, 0, j, 0, k)),
        ],
        out_specs=[pl.BlockSpec((1, tq, rows), lambda i, j, k: (i, k, j)),
                   pl.BlockSpec((1, n_grp, n_blk, tq), lambda i, j, k: (i, j, 0, k))],
        out_shape=[jax.ShapeDtypeStruct((b, s, MIX_WIDTH), BF),
                   jax.ShapeDtypeStruct((b, g, n_blk, s), BF)],
        compiler_params=_params(("parallel", "parallel", "parallel")),
        name="nsa_cmp_select",
    )(qT, k_cmp, v_cmpT, ov, gates5)


def _flash_kernel(*refs, mode, n_par, n_rep, dq, kc, tq, tk, cw, window, n_blk, has_sink, has_gate,
                  fuse_out=None):
    it = iter(refs)
    q_ref, k_ref, v_ref = next(it), next(it), next(it)
    sel_ref = next(it) if n_blk else None
    sink_ref = next(it) if has_sink else None
    gate_ref = next(it) if has_gate else None
    qnext_ref = next(it) if mode == "causal" else None
    selnext_ref = next(it) if (mode == "causal" and n_blk) else None
    if fuse_out is None:
        out_ref = next(it)
    else:
        n_extra, has_next = fuse_out
        extra_refs = [next(it) for _ in range(n_extra)]
        z_ref, x_ref, w_ref = next(it), next(it), next(it)
        ng_ref = next(it) if has_next else None
        x_out = next(it)
        h_out = next(it) if has_next else None
    qs_ref, m_ref, acc_ref, mt_ref = (next(it) for _ in range(4))
    qn_ref = next(it) if mode == "causal" else None
    bias_ref = next(it)
    s_bufs = tuple(it)

    nq = n_rep * tq
    items = [(p, c) for p in range(n_par) for c in range(nq // cw)]
    n_items = len(items)
    grp = pl.program_id(1)
    q0 = pl.program_id(2) * tq
    acc_row = lax.broadcasted_iota(jnp.int32, (V_ROWS, nq), 0)
    for p in range(n_par):
        for r in range(n_rep):
            hd = p * n_rep + r
            qs_ref[p, 0:dq, r * tq:(r + 1) * tq] = q_ref[0, hd * dq:(hd + 1) * dq, :]
        if n_blk:
            qs_ref[p, dq:dq + n_blk, :] = jnp.concatenate([sel_ref[0, p]] * n_rep, axis=1)
            if dq + n_blk < kc:
                qs_ref[p, dq + n_blk:kc, :] = jnp.zeros((kc - dq - n_blk, nq), BF)
        if has_sink:
            m_ref[p] = jnp.concatenate(
                [jnp.full((1, tq), sink_ref[(grp * n_par + p) * n_rep + r] * LOG2E, F32) for r in range(n_rep)],
                axis=1)
            acc_ref[p] = jnp.where(acc_row == HEAD_DIM, 1.0, 0.0)
        else:
            m_ref[p] = jnp.full((1, nq), M_INIT, F32)
            acc_ref[p] = jnp.zeros((V_ROWS, nq), F32)

    one_row = jnp.where(lax.broadcasted_iota(jnp.int32, (V_ROWS - HEAD_DIM, tk), 0) == 0, 1.0, 0.0).astype(BF)

    def stage_a(item, j, slot, causal, bias=None, pen=None):
        p, c = item
        cols = slice(c * cw, (c + 1) * cw)
        s = _dot(k_ref[0, p, j][:, :kc], qs_ref[p, :, cols])
        if causal:
            s = s + bias_ref[0]
        if bias is not None:
            s = s + bias_ref[bias]
        s_bufs[slot][...] = s
        mt = jnp.max(s, axis=0, keepdims=True)
        mt_ref[slot] = mt if pen is None else mt - pen

    def stage_b(item, j, slot, pen=None):
        p, c = item
        cols = slice(c * cw, (c + 1) * cw)
        m_old = m_ref[p, :, cols]
        m_new = jnp.maximum(m_old, mt_ref[slot])
        pr = jnp.exp2(s_bufs[slot][...] - (m_new if pen is None else m_new + pen)).astype(BF)
        alpha = jnp.exp2(m_old - m_new)
        v = jnp.concatenate([v_ref[0, j, p * HEAD_DIM:(p + 1) * HEAD_DIM, :], one_row], axis=0)
        acc_ref[p, :, cols] = alpha * acc_ref[p, :, cols] + _dot(v, pr)
        m_ref[p, :, cols] = m_new

    def finalize():
        outs = []
        for p in range(n_par):
            acc = acc_ref[p]
            o = acc[:HEAD_DIM] * (1.0 / acc[HEAD_DIM:HEAD_DIM + 1])
            if has_gate:
                o = o * _gate_row(gate_ref, n_rep, p)
            outs += [o[:, r * tq:(r + 1) * tq] for r in range(n_rep)]
        o_tok = jnp.concatenate(outs, axis=0).T
        if fuse_out is None:
            out_ref[0] = o_tok.astype(out_ref.dtype)
            return
        for r in extra_refs:
            o_tok = o_tok + r[0].astype(F32)
        y = _dot((o_tok * z_ref[0].astype(F32)).astype(BF), w_ref[...])
        x_new = x_ref[0] + y
        x_out[0] = x_new
        if has_next:
            h_out[0] = _rms_rows(x_new, ng_ref[...]).astype(BF)

    if mode == "causal":
        assert n_items % 2 == 0 and tq == tk == cw
        n_full = q0 // tk
        bias_ref[0] = jnp.where(lax.broadcasted_iota(jnp.int32, (tk, 1), 0)
                                <= lax.broadcasted_iota(jnp.int32, (1, cw), 1), 0.0, NEG_INF)

        def prefetch_next_tile():
            assert cw == tq
            qn_ref[0:dq, :] = qnext_ref[0, 0:dq, :]
            if n_blk:
                qn_ref[dq:dq + n_blk, :] = selnext_ref[0, 0]
                if dq + n_blk < kc:
                    qn_ref[dq + n_blk:kc, :] = jnp.zeros((kc - dq - n_blk, tq), BF)
            s = _dot(k_ref[0, 0, 0][:, :kc], qn_ref[...])
            s_bufs[0][...] = s
            mt_ref[0] = jnp.max(s, axis=0, keepdims=True)

        def step(j, kind, next_kind, last):
            for idx, item in enumerate(items):
                slot = idx % 2
                if idx + 1 < n_items:
                    stage_a(items[idx + 1], j, 1 - slot, kind)
                elif not last:
                    stage_a(items[0], j + 1, 1 - slot, next_kind)
                else:
                    prefetch_next_tile()
                stage_b(item, j, slot)

        @pl.when(pl.program_id(2) == 0)
        def _():
            stage_a(items[0], 0, 0, "causal")

        def body(j, carry):
            step(j, None, None, False)
            return carry

        def body_group(i, carry):
            for u in range(FLASH_UNROLL):
                step(FLASH_UNROLL * i + u, None, None, False)
            return carry

        n_main = jnp.maximum(n_full - 1, 0)
        n_groups = lax.shift_right_logical(n_main, int(np.log2(FLASH_UNROLL)))
        lax.fori_loop(0, n_groups, body_group, 0)
        lax.fori_loop(FLASH_UNROLL * n_groups, n_main, body, 0)

        @pl.when(n_full >= 1)
        def _():
            step(n_full - 1, None, "causal", False)
            step(n_full, "causal", None, True)
            finalize()

        @pl.when(n_full == 0)
        def _():
            step(0, "causal", None, True)
            finalize()
    else:
        w_tiles, q_tiles = window // tk, tq // tk
        assert w_tiles >= q_tiles and cw % tq == 0
        t_loc = lax.broadcasted_iota(jnp.int32, (1, cw), 1) & (tq - 1)
        key_loc = lax.broadcasted_iota(jnp.int32, (tk, 1), 0)
        for i in range(q_tiles):
            bias_ref[i] = jnp.where(i * tk + key_loc > t_loc, 0.0, NEG_INF)
            bias_ref[q_tiles + i] = jnp.where(i * tk + key_loc <= t_loc, 0.0, NEG_INF)
        work = []
        for i in range(w_tiles + q_tiles):
            jv = q0 // tk - w_tiles + i
            bias = i if i < q_tiles else (q_tiles + i - w_tiles if i >= w_tiles else None)
            pen = jnp.where(jv < 0, -NEG_INF, 0.0) if i < w_tiles else None
            work += [(item, jnp.maximum(jv, 0), bias, pen) for item in items]
        n_buf = len(s_bufs)
        ahead = WINDOW_LOOKAHEAD
        for n in range(min(ahead, len(work))):
            stage_a(work[n][0], work[n][1], n % n_buf, False, work[n][2], work[n][3])
        for n, (item, j, _, pen) in enumerate(work):
            if n + ahead < len(work):
                nxt = work[n + ahead]
                stage_a(nxt[0], nxt[1], (n + ahead) % n_buf, False, nxt[2], nxt[3])
            stage_b(item, j, n % n_buf, pen)

        finalize()


def _flash(qT, k_tok, vT_tiles, *, mode, n_par, n_rep, dq, kc, tq, tk, cw, window=None,
           sel_bias=None, sinks=None, gates=None, gate_branch=0, out_proj=None, single_buffer_v=False,
           name="flash"):
    b, _, s = qT.shape
    kh, k_lanes = k_tok.shape[1], k_tok.shape[3]
    n_t = s // tk
    k5 = k_tok.reshape(b, kh, n_t, tk, k_lanes)
    n_grp = kh // n_par
    heads = n_par * n_rep
    nq = n_rep * tq
    n_blk = sel_bias.shape[2] if sel_bias is not None else 0
    has_sink, has_gate = sinks is not None, gates is not None
    args = [qT, k5, vT_tiles]
    in_specs = [
        pl.BlockSpec((1, heads * dq, tq), lambda i, j, k: (i, j, k)),
        pl.BlockSpec((1, n_par, n_t, tk, k_lanes), lambda i, j, k: (i, j, 0, 0, 0)),
        pl.BlockSpec((1, n_t, n_par * HEAD_DIM, tk), lambda i, j, k: (i, 0, j, 0),
                     **({"pipeline_mode": pl.Buffered(1)} if single_buffer_v else {})),
    ]
    if n_blk:
        assert dq + n_blk <= kc
        args.append(sel_bias)
        in_specs.append(pl.BlockSpec((1, n_par, n_blk, tq), lambda i, j, k: (i, j, 0, k)))
    if has_sink:
        args.append(sinks.astype(F32))
        in_specs.append(pl.BlockSpec(memory_space=pltpu.SMEM))
    if has_gate:
        args.append(gates)
        in_specs.append(pl.BlockSpec((1, 1, n_par, n_rep, tq),
                                     lambda i, j, k, _br=gate_branch: (i, _br, j, 0, k)))
    n_sbuf = 2 if mode == "causal" else WINDOW_SCORE_BUFS
    scratch = [pltpu.VMEM((n_par, kc, nq), BF), pltpu.VMEM((n_par, 1, nq), F32),
               pltpu.VMEM((n_par, V_ROWS, nq), F32), pltpu.VMEM((n_sbuf, 1, cw), F32)]
    if mode == "causal":
        last_q = s // tq - 1
        args.append(qT)
        in_specs.append(pl.BlockSpec((1, dq, tq), lambda i, j, k: (i, j * heads, jnp.minimum(k + 1, last_q))))
        if n_blk:
            args.append(sel_bias)
            in_specs.append(pl.BlockSpec((1, 1, n_blk, tq),
                                         lambda i, j, k: (i, j * n_par, 0, jnp.minimum(k + 1, last_q))))
        scratch += [pltpu.VMEM((kc, tq), BF), pltpu.VMEM((1, tk, cw), F32)]
    else:
        scratch.append(pltpu.VMEM((2 * (tq // tk), tk, cw), F32))
    scratch += [pltpu.VMEM((tk, cw), F32)] * n_sbuf
    out_specs = pl.BlockSpec((1, tq, heads * HEAD_DIM), lambda i, j, k: (i, k, j))
    out_shape = jax.ShapeDtypeStruct((b, s, MIX_WIDTH), BF)
    fuse_out = None
    if out_proj is not None:
        assert mode == "window" and heads * HEAD_DIM == MIX_WIDTH and n_grp == 1
        extra, zs, x, w_out, next_gain = out_proj
        tok = pl.BlockSpec((1, tq, D_MODEL), lambda i, j, k: (i, k, 0))
        args += list(extra) + [zs, x, w_out.astype(BF)]
        in_specs += [tok] * (len(extra) + 2) + [pl.BlockSpec(w_out.shape, lambda i, j, k: (0, 0))]
        out_specs, out_shape = [tok], [jax.ShapeDtypeStruct(x.shape, F32)]
        if next_gain is not None:
            args.append(next_gain.reshape(1, D_MODEL))
            in_specs.append(pl.BlockSpec((1, D_MODEL), lambda i, j, k: (0, 0)))
            out_specs.append(tok)
            out_shape.append(jax.ShapeDtypeStruct(x.shape, BF))
        fuse_out = (len(extra), next_gain is not None)
    kern = functools.partial(_flash_kernel, mode=mode, n_par=n_par, n_rep=n_rep, dq=dq, kc=kc, tq=tq, tk=tk,
                             cw=cw, window=window, n_blk=n_blk, has_sink=has_sink, has_gate=has_gate,
                             fuse_out=fuse_out)
    res = pl.pallas_call(
        kern,
        grid=(b, n_grp, s // tq),
        in_specs=in_specs,
        out_specs=out_specs,
        out_shape=out_shape,
        scratch_shapes=scratch,
        compiler_params=_params(("parallel", "parallel", "arbitrary")),
        name=name,
    )(*args)
    if out_proj is None:
        return res
    return (res[0], res[1]) if next_gain is not None else (res[0], None)


def _swa_proj_kernel(h_ref, cos_ref, sin_ref, wq_ref, wk_ref, wv_ref, wz_ref, qg_ref, kg_ref,
                     q_out, k_out, v_out, z_out, y_ref, *, win_tile):
    h = h_ref[0]
    cos, sin = cos_ref[0], sin_ref[0]
    y_ref[...] = _dot_nt(wq_ref[...], h)
    for hd in range(N_HEADS):
        rows = slice(hd * HEAD_DIM, (hd + 1) * HEAD_DIM)
        q_out[0, rows, :] = (_rope_t(_head_rms_t(y_ref[rows, :], qg_ref[...]), cos, sin) * Q_SCALE).astype(BF)
    kvd = SWA_KV_HEADS * HEAD_DIM
    y_ref[0:kvd, :] = _dot_nt(wk_ref[...], h)
    for gi in range(SWA_KV_HEADS):
        rows = slice(gi * HEAD_DIM, (gi + 1) * HEAD_DIM)
        k_out[0, gi] = _to_token_major(_rope_t(_head_rms_t(y_ref[rows, :], kg_ref[...]), cos, sin)).astype(BF)
    _store_lane_tiles(v_out, _dot_nt(wv_ref[...], h).astype(BF), win_tile)
    z_out[0] = _silu(_dot(h, wz_ref[...])).astype(BF)


def _swa_proj(h, cos, sin, w_in, q_gain, k_gain, *, win_tile):
    b, s, d = h.shape
    g = SWA_KV_HEADS
    kvd = g * HEAD_DIM
    tm = min(TOK_TILE, s)
    wt = w_in.T.astype(BF)
    wq, wk, wv = wt[:MIX_WIDTH], wt[MIX_WIDTH:MIX_WIDTH + kvd], wt[MIX_WIDTH + kvd:MIX_WIDTH + 2 * kvd]
    wz = w_in[:, MIX_WIDTH + 2 * kvd:].astype(BF)
    qg, kg = q_gain.reshape(HEAD_DIM, 1), k_gain.reshape(HEAD_DIM, 1)
    wpt = tm // win_tile
    half = HEAD_DIM // 2
    return pl.pallas_call(
        functools.partial(_swa_proj_kernel, win_tile=win_tile),
        grid=(b, s // tm),
        in_specs=[pl.BlockSpec((1, tm, d), lambda i, j: (i, j, 0)),
                  pl.BlockSpec((1, half, tm), lambda i, j: (i, 0, j)),
                  pl.BlockSpec((1, half, tm), lambda i, j: (i, 0, j)),
                  _full(wq.shape), _full(wk.shape), _full(wv.shape), _full(wz.shape),
                  _full(qg.shape), _full(kg.shape)],
        out_specs=[pl.BlockSpec((1, MIX_WIDTH, tm), lambda i, j: (i, 0, j)),
                   pl.BlockSpec((1, g, tm, KEY_PAD), lambda i, j: (i, 0, j, 0)),
                   pl.BlockSpec((1, wpt, kvd, win_tile), lambda i, j: (i, j, 0, 0)),
                   pl.BlockSpec((1, tm, MIX_WIDTH), lambda i, j: (i, j, 0))],
        out_shape=[jax.ShapeDtypeStruct((b, MIX_WIDTH, s), BF),
                   jax.ShapeDtypeStruct((b, g, s, KEY_PAD), BF),
                   jax.ShapeDtypeStruct((b, s // win_tile, kvd, win_tile), BF),
                   jax.ShapeDtypeStruct((b, s, MIX_WIDTH), BF)],
        scratch_shapes=[pltpu.VMEM((MIX_WIDTH, tm), F32)],
        compiler_params=_params(("parallel", "parallel")),
        name="swa_proj",
    )(h, cos, sin, wq, wk, wv, wz, qg, kg)


def _split3(x):
    hi = x.astype(BF)
    r1 = x - hi.astype(F32)
    mid = r1.astype(BF)
    lo = (r1 - mid.astype(F32)).astype(BF)
    return hi, mid, lo


def _fox_proj_kernel(h_ref, wf_ref, bias_ref, tri_ref, wq_ref, wk_ref, wv_ref, wz_ref, qg_ref, kg_ref,
                     q_out, k_out, v_out, z_out, yq_ref, yk_ref, cum_ref, carry_ref, *, tile):
    @pl.when(pl.program_id(1) == 0)
    def _():
        carry_ref[...] = jnp.zeros_like(carry_ref)

    h = h_ref[0]
    tm = h.shape[0]
    x = _dot_nt(wf_ref[...], h) + bias_ref[...]
    logf = jnp.minimum(x, 0.0) - jnp.log(1.0 + jnp.exp(-jnp.abs(x)))
    tri = tri_ref[...]
    hi, mid, lo = _split3(logf)
    cum = (_dot(hi, tri) + _dot(mid, tri)) + _dot(lo, tri) + carry_ref[:, 0:1]
    carry_ref[...] = jnp.broadcast_to(cum[:, -1:], carry_ref.shape)
    cum_ref[...] = cum * LOG2E

    yq_ref[...] = _dot_nt(wq_ref[...], h)
    yk_ref[...] = _dot_nt(wk_ref[...], h)
    row = lax.broadcasted_iota(jnp.int32, (8, tm), 0)
    zeros = jnp.zeros((KEY_PAD - HEAD_DIM - 16, tm), F32)
    for hd in range(N_HEADS):
        rows = slice(hd * HEAD_DIM, (hd + 1) * HEAD_DIM)
        c_hi, c_mid, c_lo = (c.astype(F32) for c in _split3(cum_ref[hd:hd + 1, :]))
        c3 = jnp.where(row == 0, c_hi, jnp.where(row == 1, c_mid, jnp.where(row == 2, c_lo, 0.0)))
        one3 = jnp.where(row < 3, 1.0, 0.0)
        q = _head_rms_t(yq_ref[rows, :], qg_ref[...]) * Q_SCALE
        q_out[0, hd] = jnp.concatenate([q, one3, c3, zeros], axis=0).astype(BF)
        k = _head_rms_t(yk_ref[rows, :], kg_ref[...])
        k_out[0, hd] = jnp.concatenate([k, -c3, one3, zeros], axis=0).T.astype(BF)
    _store_lane_tiles(v_out, _dot_nt(wv_ref[...], h).astype(BF), tile)
    z_out[0] = _silu(_dot(h, wz_ref[...])).astype(BF)


def _fox_proj(h, w_in, forget_bias, q_gain, k_gain, *, tile):
    b, s, d = h.shape
    tm = min(TOK_TILE, s)
    wt = w_in.T.astype(BF)
    wq, wk, wv = wt[:MIX_WIDTH], wt[MIX_WIDTH:2 * MIX_WIDTH], wt[2 * MIX_WIDTH:3 * MIX_WIDTH]
    wf = wt[3 * MIX_WIDTH:3 * MIX_WIDTH + N_HEADS]
    wz = w_in[:, 3 * MIX_WIDTH + N_HEADS:].astype(BF)
    qg, kg = q_gain.reshape(HEAD_DIM, 1), k_gain.reshape(HEAD_DIM, 1)
    tri = jnp.asarray(np.arange(tm)[:, None] <= np.arange(tm)[None, :], BF)
    return pl.pallas_call(
        functools.partial(_fox_proj_kernel, tile=tile),
        grid=(b, s // tm),
        in_specs=[pl.BlockSpec((1, tm, d), lambda i, j: (i, j, 0)),
                  _full(wf.shape), _full((N_HEADS, 1)), _full(tri.shape),
                  _full(wq.shape), _full(wk.shape), _full(wv.shape), _full(wz.shape),
                  _full(qg.shape), _full(kg.shape)],
        out_specs=[pl.BlockSpec((1, N_HEADS, KEY_PAD, tm), lambda i, j: (i, 0, 0, j)),
                   pl.BlockSpec((1, N_HEADS, tm, KEY_PAD), lambda i, j: (i, 0, j, 0)),
                   pl.BlockSpec((1, tm // tile, MIX_WIDTH, tile), lambda i, j: (i, j, 0, 0)),
                   pl.BlockSpec((1, tm, MIX_WIDTH), lambda i, j: (i, j, 0))],
        out_shape=[jax.ShapeDtypeStruct((b, N_HEADS, KEY_PAD, s), BF),
                   jax.ShapeDtypeStruct((b, N_HEADS, s, KEY_PAD), BF),
                   jax.ShapeDtypeStruct((b, s // tile, MIX_WIDTH, tile), BF),
                   jax.ShapeDtypeStruct((b, s, MIX_WIDTH), BF)],
        scratch_shapes=[pltpu.VMEM((MIX_WIDTH, tm), F32), pltpu.VMEM((MIX_WIDTH, tm), F32),
                        pltpu.VMEM((N_HEADS, tm), F32), pltpu.VMEM((N_HEADS, LANES), F32)],
        compiler_params=_params(("parallel", "arbitrary")),
        name="fox_proj",
    )(h, wf, forget_bias.reshape(N_HEADS, 1).astype(F32), tri, wq, wk, wv, wz, qg, kg)


def _out_proj_kernel(*refs, n_o, has_next):
    o_refs = refs[:n_o]
    z_ref, x_ref, w_ref = refs[n_o:n_o + 3]
    rest = refs[n_o + 3:]
    if has_next:
        g_ref, x_out, h_out = rest
    else:
        (x_out,) = rest
    o = o_refs[0][0].astype(F32)
    for r in o_refs[1:]:
        o = o + r[0].astype(F32)
    y = _dot((o * z_ref[0].astype(F32)).astype(BF), w_ref[...])
    x_new = x_ref[0] + y
    x_out[0] = x_new
    if has_next:
        h_out[0] = _rms_rows(x_new, g_ref[...]).astype(BF)


def _out_proj(o_list, zs, x, w_out, next_gain):
    b, s, d = x.shape
    tm = min(TOK_TILE, s)
    has_next = next_gain is not None
    blk = pl.BlockSpec((1, tm, d), lambda i, j: (i, j, 0))
    args = list(o_list) + [zs, x, w_out.astype(BF)]
    in_specs = [blk] * (len(o_list) + 2) + [_full(w_out.shape)]
    out_shape = [jax.ShapeDtypeStruct((b, s, d), F32)]
    out_specs = [blk]
    if has_next:
        args.append(next_gain.reshape(1, d))
        in_specs.append(_full((1, d)))
        out_shape.append(jax.ShapeDtypeStruct((b, s, d), BF))
        out_specs.append(blk)
    res = pl.pallas_call(
        functools.partial(_out_proj_kernel, n_o=len(o_list), has_next=has_next),
        grid=(b, s // tm), in_specs=in_specs, out_specs=out_specs, out_shape=out_shape,
        compiler_params=_params(("parallel", "parallel")),
        name="out_proj",
    )(*args)
    return (res[0], res[1]) if has_next else (res[0], None)


ITEM_LANES = 512
CMP_Q_TILE = 512
CMP_GROUPS = 2
NSA_SLC_Q_TILE = 512
NSA_SLC_TILE = 512
NSA_SLC_GROUPS = 4
NSA_WIN_Q_TILE = 256
NSA_WIN_TILE = 256
SWA_TILE = 128
FOX_Q_TILE = 512
FOX_K_TILE = 512
FOX_HEADS_PER_STEP = 8


def _nsa_mixer(x, h, cos, sin, w_in, q_gain, k_gain, cmp_pos, cmp_w1, cmp_w2, w_out, next_gain, pre_gain):
    b, s, _ = x.shape
    g = NSA_KV_HEADS
    r = N_HEADS // g
    qT, kc_tok, vc_tok, ks, kw, vsT, vwT, gates, zs = _nsa_proj(
        x if h is None else h, cos, sin, w_in, q_gain, k_gain, slc_tile=NSA_SLC_TILE, win_tile=NSA_WIN_TILE,
        pre_gain=pre_gain if h is None else None)
    k_cmp, v_cmpT = _compress(kc_tok, vc_tok, cmp_pos, cmp_w1, cmp_w2, k_gain[0])
    gates5 = gates.reshape(b, 3, g, r, s)
    o_cmp, sel_bias = _cmp_select(qT, k_cmp, v_cmpT, gates5, tq=CMP_Q_TILE)
    o_slc = _flash(qT, ks, vsT, mode="causal", n_par=NSA_SLC_GROUPS, n_rep=r, dq=HEAD_DIM, kc=SLC_KEY_LANES,
                   tq=NSA_SLC_Q_TILE, tk=NSA_SLC_TILE, cw=ITEM_LANES, sel_bias=sel_bias,
                   gates=gates5, gate_branch=1, single_buffer_v=True, name="nsa_selected")
    return _flash(qT, kw, vwT, mode="window", n_par=g, n_rep=r, dq=HEAD_DIM, kc=HEAD_DIM,
                  tq=NSA_WIN_Q_TILE, tk=NSA_WIN_TILE, cw=ITEM_LANES, window=NSA_WINDOW,
                  gates=gates5, gate_branch=2, out_proj=([o_cmp, o_slc], zs, x, w_out, next_gain),
                  name="nsa_window_out")


def _swa_mixer(x, h, cos, sin, w_in, q_gain, k_gain, sinks, w_out, next_gain):
    r = N_HEADS // SWA_KV_HEADS
    qT, k, vT, zs = _swa_proj(h, cos, sin, w_in, q_gain, k_gain, win_tile=SWA_TILE)
    return _flash(qT, k, vT, mode="window", n_par=SWA_KV_HEADS, n_rep=r, dq=HEAD_DIM, kc=HEAD_DIM, tq=SWA_TILE,
                  tk=SWA_TILE, cw=r * SWA_TILE, window=SWA_WINDOW, sinks=sinks,
                  out_proj=([], zs, x, w_out, next_gain), name="swa_window_out")


def _fox_mixer(x, h, w_in, forget_bias, q_gain, k_gain, w_out, next_gain):
    b, s, _ = h.shape
    qT, k, vT, zs = _fox_proj(h, w_in, forget_bias, q_gain, k_gain, tile=FOX_K_TILE)
    o = _flash(qT.reshape(b, N_HEADS * KEY_PAD, s), k, vT, mode="causal", n_par=FOX_HEADS_PER_STEP, n_rep=1, dq=KEY_PAD,
               kc=KEY_PAD, tq=FOX_Q_TILE, tk=FOX_K_TILE, cw=ITEM_LANES, single_buffer_v=True,
               name="fox_attention")
    return _out_proj([o], zs, x, w_out, next_gain)


def kernel(x, positions, norm_gains, a_w_in, a_q_gain, a_k_gain, a_cmp_pos, a_cmp_w1, a_cmp_w2, a_w_out,
           b_w_in, b_q_gain, b_k_gain, b_sinks, b_w_out,
           c_w_in, c_forget_bias, c_q_gain, c_k_gain, c_w_out):
    depth = norm_gains.shape[0]
    cos, sin = _rope_tables(positions)
    h = None
    for i in range(depth):
        j, mixer = divmod(i, 3)
        next_gain = norm_gains[i + 1] if i + 1 < depth else None
        if mixer == 0:
            x, h = _nsa_mixer(x, h, cos, sin, a_w_in[j], a_q_gain[j], a_k_gain[j],
                              a_cmp_pos[j], a_cmp_w1[j], a_cmp_w2[j], a_w_out[j], next_gain, norm_gains[i])
        elif mixer == 1:
            x, h = _swa_mixer(x, h, cos, sin, b_w_in[j], b_q_gain[j], b_k_gain[j], b_sinks[j],
                              b_w_out[j], next_gain)
        else:
            x, h = _fox_mixer(x, h, c_w_in[j], c_forget_bias[j], c_q_gain[j], c_k_gain[j],
                              c_w_out[j], next_gain)
    return x
```

```python
import functools

import jax
import jax.numpy as jnp
import numpy as np
from jax import lax
from jax.experimental import pallas as pl
from jax.experimental.pallas import tpu as pltpu

D_MODEL = 1024
HEAD_DIM = 64
N_HEADS = 16
MIX_WIDTH = N_HEADS * HEAD_DIM
ROPE_THETA = 10000.0
EPS = 1e-6
SCALE = HEAD_DIM ** -0.5
NEG_INF = -1e30
M_INIT = -1e29
SEL_OFF = -(2.0 ** 100)

NSA_KV_HEADS = 4
NSA_CMP_LEN = 32
NSA_CMP_STRIDE = 16
NSA_SLC_LEN = 64
NSA_TOPK = 16
NSA_WINDOW = 512
SWA_KV_HEADS = 2
SWA_WINDOW = 128

LOG2E = float(np.log2(np.e))
Q_SCALE = SCALE * LOG2E

LANES = 128
KEY_PAD = 128
SLC_KEY_LANES = 256
V_ROWS = 80
FLASH_UNROLL = 4
WINDOW_LOOKAHEAD = 2
WINDOW_SCORE_BUFS = 4
CMP_BLOCK = 128
CMP_MASK_ROWS = 2 * CMP_BLOCK
VMEM_LIMIT = 56 * 1024 * 1024

TOK_TILE = 512
NT_DIMS = (((1,), (1,)), ((), ()))

BF = jnp.bfloat16
F32 = jnp.float32


def _params(sem):
    return pltpu.CompilerParams(dimension_semantics=sem, vmem_limit_bytes=VMEM_LIMIT)


def _dot(a, b):
    return jnp.dot(a, b, preferred_element_type=F32)


def _dot_nt(a, b):
    return lax.dot_general(a, b, NT_DIMS, preferred_element_type=F32)


def _rope_tab_kernel(pos_ref, invf_ref, cos_ref, sin_ref):
    ang = invf_ref[...] * pos_ref[0].astype(F32)
    cos_ref[0] = jnp.cos(ang)
    sin_ref[0] = jnp.sin(ang)


def _rope_tables(positions):
    b, s = positions.shape
    half = HEAD_DIM // 2
    inv_freq = ROPE_THETA ** (-jnp.arange(half, dtype=F32) * 2.0 / HEAD_DIM)
    tm = min(TOK_TILE, s)
    out = jax.ShapeDtypeStruct((b, half, s), F32)
    return pl.pallas_call(
        _rope_tab_kernel,
        grid=(b, s // tm),
        in_specs=[pl.BlockSpec((1, 1, tm), lambda i, j: (i, 0, j)),
                  pl.BlockSpec((half, 1), lambda i, j: (0, 0))],
        out_specs=[pl.BlockSpec((1, half, tm), lambda i, j: (i, 0, j))] * 2,
        out_shape=[out, out],
        compiler_params=_params(("parallel", "parallel")),
        name="rope_tables",
    )(positions.reshape(b, 1, s), inv_freq.reshape(half, 1))


def _rms_rows(x, gain_row):
    y = x * lax.rsqrt(jnp.mean(x * x, axis=-1, keepdims=True) + EPS)
    return y * gain_row


def _head_rms_t(y, gain_col):
    ms = jnp.mean(y * y, axis=0, keepdims=True)
    return (y * lax.rsqrt(ms + EPS)) * gain_col


def _rope_t(y, cos, sin):
    half = HEAD_DIM // 2
    x1, x2 = y[:half], y[half:]
    return jnp.concatenate([x1 * cos - x2 * sin, x2 * cos + x1 * sin], axis=0)


def _to_token_major(y):
    pad = jnp.zeros((KEY_PAD - y.shape[0], y.shape[1]), y.dtype)
    return jnp.concatenate([y, pad], axis=0).T


def _silu(z):
    return z * (1.0 / (1.0 + jnp.exp(-z)))


def _sigmoid(z):
    return 1.0 / (1.0 + jnp.exp(-z))


def _store_lane_tiles(out_ref, y, tile):
    for c in range(y.shape[1] // tile):
        out_ref[0, c] = y[:, c * tile:(c + 1) * tile]


def _store_chunk_rows(out_ref, gi, yt, tok_ref):
    tok_ref[...] = yt
    n = yt.shape[0] // NSA_CMP_STRIDE
    for m in range(NSA_CMP_STRIDE // 2):
        even = tok_ref[pl.ds(2 * m, n, stride=NSA_CMP_STRIDE), :]
        odd = tok_ref[pl.ds(2 * m + 1, n, stride=NSA_CMP_STRIDE), :]
        out_ref[0, gi, :, KEY_PAD * m:KEY_PAD * (m + 1)] = even + pltpu.roll(odd, HEAD_DIM, 1)


def _nsa_proj_kernel(*refs, slc_tile, win_tile, norm_in):
    it = iter(refs)
    h_ref = next(it)
    ng_ref = next(it) if norm_in else None
    (cos_ref, sin_ref, wq_ref, wk_ref, wv_ref, wg_ref, wz_ref, qg_ref, kg_ref,
     q_out, kc_out, vc_out, ks_out, kw_out, vs_out, vw_out, g_out, z_out, y_ref, tok_ref) = it
    h = _rms_rows(h_ref[0], ng_ref[...]).astype(BF) if norm_in else h_ref[0]
    cos, sin = cos_ref[0], sin_ref[0]
    g = NSA_KV_HEADS
    kvd = g * HEAD_DIM
    y_ref[...] = _dot_nt(wq_ref[...], h)
    for hd in range(N_HEADS):
        rows = slice(hd * HEAD_DIM, (hd + 1) * HEAD_DIM)
        y = _rope_t(_head_rms_t(y_ref[rows, :], qg_ref[...]), cos, sin) * Q_SCALE
        q_out[0, rows, :] = y.astype(BF)
    y_ref[0:3 * kvd, :] = _dot_nt(wk_ref[...], h)
    tm = h.shape[0]
    tok = pl.program_id(1) * tm + lax.broadcasted_iota(jnp.int32, (tm, 1), 0)
    blk_lane = HEAD_DIM + lax.shift_right_logical(tok, int(np.log2(NSA_SLC_LEN)))
    blk_hot = lax.broadcasted_iota(jnp.int32, (1, SLC_KEY_LANES), 1) == blk_lane
    for kind in range(3):
        for gi in range(g):
            r0 = (kind * g + gi) * HEAD_DIM
            y = y_ref[r0:r0 + HEAD_DIM, :]
            if kind > 0:
                y = _head_rms_t(y, kg_ref[:, kind:kind + 1])
            yt = _to_token_major(_rope_t(y, cos, sin))
            if kind == 0:
                _store_chunk_rows(kc_out, gi, yt, tok_ref)
            elif kind == 1:
                wide = jnp.concatenate([yt, jnp.zeros((tm, SLC_KEY_LANES - KEY_PAD), F32)], axis=1)
                ks_out[0, gi] = jnp.where(blk_hot, 1.0, wide).astype(BF)
            else:
                kw_out[0, gi] = yt.astype(BF)
    y_ref[0:3 * kvd, :] = _dot_nt(wv_ref[...], h)
    for gi in range(g):
        r0 = gi * HEAD_DIM
        _store_chunk_rows(vc_out, gi, _to_token_major(y_ref[r0:r0 + HEAD_DIM, :]), tok_ref)
    _store_lane_tiles(vs_out, y_ref[kvd:2 * kvd, :].astype(BF), slc_tile)
    _store_lane_tiles(vw_out, y_ref[2 * kvd:3 * kvd, :].astype(BF), win_tile)
    g_out[0] = _sigmoid(_dot_nt(wg_ref[...], h))
    z_out[0] = _silu(_dot(h, wz_ref[...])).astype(BF)


def _full(shape):
    nd = len(shape)
    return pl.BlockSpec(shape, lambda i, j, _n=nd: (0,) * _n)


def _nsa_proj(h, cos, sin, w_in, q_gain, k_gain, *, slc_tile, win_tile, pre_gain=None):
    b, s, d = h.shape
    g = NSA_KV_HEADS
    kvd = g * HEAD_DIM
    tm = min(TOK_TILE, s)
    flat = NSA_CMP_STRIDE * HEAD_DIM
    sizes = [MIX_WIDTH] + [kvd] * 6 + [3 * N_HEADS]
    off = np.cumsum([0] + sizes)
    wt = w_in.T.astype(BF)
    wq = wt[off[0]:off[1]]
    wk = jnp.concatenate([wt[off[1]:off[2]], wt[off[3]:off[4]], wt[off[5]:off[6]]], axis=0)
    wv = jnp.concatenate([wt[off[2]:off[3]], wt[off[4]:off[5]], wt[off[6]:off[7]]], axis=0)
    wg = wt[off[7]:off[8]]
    wz = w_in[:, off[8]:].astype(BF)
    qg = q_gain.reshape(HEAD_DIM, 1)
    kg = k_gain.T
    n_t = s // tm
    out_shape = [
        jax.ShapeDtypeStruct((b, MIX_WIDTH, s), BF),
        jax.ShapeDtypeStruct((b, g, s // NSA_CMP_STRIDE, flat), F32),
        jax.ShapeDtypeStruct((b, g, s // NSA_CMP_STRIDE, flat), F32),
        jax.ShapeDtypeStruct((b, g, s, SLC_KEY_LANES), BF),
        jax.ShapeDtypeStruct((b, g, s, KEY_PAD), BF),
        jax.ShapeDtypeStruct((b, s // slc_tile, kvd, slc_tile), BF),
        jax.ShapeDtypeStruct((b, s // win_tile, kvd, win_tile), BF),
        jax.ShapeDtypeStruct((b, 3 * N_HEADS, s), F32),
        jax.ShapeDtypeStruct((b, s, MIX_WIDTH), BF),
    ]
    out_specs = [
        pl.BlockSpec((1, MIX_WIDTH, tm), lambda i, j: (i, 0, j)),
        pl.BlockSpec((1, g, tm // NSA_CMP_STRIDE, flat), lambda i, j: (i, 0, j, 0)),
        pl.BlockSpec((1, g, tm // NSA_CMP_STRIDE, flat), lambda i, j: (i, 0, j, 0)),
        pl.BlockSpec((1, g, tm, SLC_KEY_LANES), lambda i, j: (i, 0, j, 0)),
        pl.BlockSpec((1, g, tm, KEY_PAD), lambda i, j: (i, 0, j, 0)),
        pl.BlockSpec((1, tm // slc_tile, kvd, slc_tile), lambda i, j: (i, j, 0, 0)),
        pl.BlockSpec((1, tm // win_tile, kvd, win_tile), lambda i, j: (i, j, 0, 0)),
        pl.BlockSpec((1, 3 * N_HEADS, tm), lambda i, j: (i, 0, j)),
        pl.BlockSpec((1, tm, MIX_WIDTH), lambda i, j: (i, j, 0)),
    ]
    norm_in = pre_gain is not None
    args = [h] + ([pre_gain.reshape(1, d)] if norm_in else []) + [cos, sin, wq, wk, wv, wg, wz, qg, kg]
    in_specs = [pl.BlockSpec((1, tm, d), lambda i, j: (i, j, 0))] + ([_full((1, d))] if norm_in else []) + [
        pl.BlockSpec((1, HEAD_DIM // 2, tm), lambda i, j: (i, 0, j)),
        pl.BlockSpec((1, HEAD_DIM // 2, tm), lambda i, j: (i, 0, j)),
        _full(wq.shape), _full(wk.shape), _full(wv.shape), _full(wg.shape), _full(wz.shape),
        _full(qg.shape), _full(kg.shape),
    ]
    return pl.pallas_call(
        functools.partial(_nsa_proj_kernel, slc_tile=slc_tile, win_tile=win_tile, norm_in=norm_in),
        grid=(b, n_t), in_specs=in_specs, out_specs=out_specs, out_shape=out_shape,
        scratch_shapes=[pltpu.VMEM((MIX_WIDTH, tm), F32), pltpu.VMEM((tm, KEY_PAD), F32)],
        compiler_params=_params(("parallel", "parallel")),
        name="nsa_proj",
    )(*args)


def _gelu_tanh(x):
    c = np.float32(np.sqrt(2.0 / np.pi))
    return 0.5 * x * (1.0 + jnp.tanh(c * (x + 0.044715 * (x * x * x))))


def _compress_kernel(kc_ref, vc_ref, pos_ref, w1_ref, w2_ref, kg_ref, kcmp_out, vcmp_out):
    for which, (src, dst) in enumerate(((kc_ref, kcmp_out), (vc_ref, vcmp_out))):
        x = src[0, 0]
        n = x.shape[0]
        half = x.shape[1]
        xa = (x + pos_ref[which, 0:1, :]).astype(BF)
        xb = (x + pos_ref[which, 1:2, :]).astype(BF)
        ua = _dot(xa, w1_ref[which, :half, :])
        ub = _dot(xb, w1_ref[which, half:, :])
        row = lax.broadcasted_iota(jnp.int32, (n, 1), 0)
        ub_next = jnp.where(row == n - 1, 0.0, pltpu.roll(ub, n - 1, 0))
        hid = _gelu_tanh(ua + ub_next)
        y = _dot(hid.astype(BF), w2_ref[which])
        if which == 0:
            y = _rms_rows(y, kg_ref[...])
            dst[0, 0] = y.astype(BF)
        else:
            pad = jnp.zeros((n, KEY_PAD - HEAD_DIM), F32)
            dst[0, 0] = jnp.concatenate([y, pad], axis=1).T[:HEAD_DIM].astype(BF)


def _compress(kc, vc, cmp_pos, cmp_w1, cmp_w2, k_gain0):
    b, g, n_chunk, flat = kc.shape
    pos = cmp_pos.reshape(2, 2, flat)
    w1 = cmp_w1.astype(BF)
    w2 = cmp_w2.astype(BF)
    blk = pl.BlockSpec((1, 1, n_chunk, flat), lambda i, j: (i, j, 0, 0))
    return pl.pallas_call(
        _compress_kernel,
        grid=(b, g),
        in_specs=[blk, blk, _full(pos.shape), _full(w1.shape), _full(w2.shape),
                  _full((1, HEAD_DIM))],
        out_specs=[pl.BlockSpec((1, 1, n_chunk, HEAD_DIM), lambda i, j: (i, j, 0, 0)),
                   pl.BlockSpec((1, 1, HEAD_DIM, n_chunk), lambda i, j: (i, j, 0, 0))],
        out_shape=[jax.ShapeDtypeStruct((b, g, n_chunk, HEAD_DIM), BF),
                   jax.ShapeDtypeStruct((b, g, HEAD_DIM, n_chunk), BF)],
        compiler_params=_params(("parallel", "parallel")),
        name="nsa_compress",
    )(kc, vc, pos, w1, w2, k_gain0.reshape(1, HEAD_DIM))


def _gate_row(gate_ref, n_heads, grp=0):
    return jnp.concatenate([gate_ref[0, 0, grp, r:r + 1, :] for r in range(n_heads)], axis=1)


def _cmp_branch(rows, q_ref, kc_ref, vc_ref, ov_ref, gate_ref, o_out, sel_out, *, tq, n_blk, n_grp):
    r_heads = N_HEADS // NSA_KV_HEADS
    nq = r_heads * tq
    width = r_heads * HEAD_DIM
    q0 = pl.program_id(2) * tq
    n_cmp = kc_ref.shape[2] - 1
    t_row = q0 + (lax.broadcasted_iota(jnp.int32, (1, nq), 1) & (tq - 1))
    lo = max(rows - CMP_MASK_ROWS, 0)
    c_col = lo + lax.broadcasted_iota(jnp.int32, (rows - lo, 1), 0)
    valid = (c_col * NSA_CMP_STRIDE + (NSA_CMP_LEN - 1) <= t_row) & (c_col < n_cmp)
    one_row = jnp.where(lax.broadcasted_iota(jnp.int32, (V_ROWS - HEAD_DIM, rows), 0) == 0, 1.0, 0.0).astype(BF)
    n_live = min(n_blk, rows * NSA_CMP_STRIDE // NSA_SLC_LEN + 8)
    t1 = q0 + lax.broadcasted_iota(jnp.int32, (1, tq), 1)
    cur = lax.shift_right_logical(t1, int(np.log2(NSA_SLC_LEN)))
    blk = lax.broadcasted_iota(jnp.int32, (n_live, tq), 0)
    forced = (blk == 0) | (blk == cur) | (blk == cur - 1)
    imps = []
    for gi in range(n_grp):
        q4 = jnp.concatenate([q_ref[0, (gi * r_heads + r) * HEAD_DIM:(gi * r_heads + r + 1) * HEAD_DIM, :]
                              for r in range(r_heads)], axis=1)
        s = _dot(kc_ref[0, gi, 0:rows, :], q4)
        s_new = jnp.where(valid, s[lo:], NEG_INF)
        m = jnp.max(s_new, axis=0, keepdims=True)
        if lo:
            m = jnp.maximum(m, jnp.max(s[:lo], axis=0, keepdims=True))
        e = jnp.where(valid, jnp.exp2(s_new - m), 0.0).astype(BF)
        if lo:
            e = jnp.concatenate([jnp.exp2(s[:lo] - m).astype(BF), e], axis=0)
        lhs = jnp.concatenate([vc_ref[0, gi, :, 0:rows], one_row, ov_ref[0:n_live, 0:rows]], axis=0)
        res = _dot(lhs, e)
        l = res[HEAD_DIM:HEAD_DIM + 1]
        inv = jnp.where(l > 0.0, 1.0 / jnp.where(l > 0.0, l, 1.0), 0.0)
        o = res[:HEAD_DIM] * (inv * _gate_row(gate_ref, r_heads, gi))
        o_rows = jnp.concatenate([o[:, r * tq:(r + 1) * tq] for r in range(r_heads)], axis=0)
        o_out[0, :, gi * width:(gi + 1) * width] = o_rows.T.astype(o_out.dtype)
        w = res[V_ROWS:V_ROWS + n_live] * inv
        imp = w[:, 0:tq]
        for r in range(1, r_heads):
            imp = imp + w[:, r * tq:(r + 1) * tq]
        imps.append(jnp.where(forced, -jnp.inf, jnp.where(blk > cur, NEG_INF, imp)))
    for _ in range(min(NSA_TOPK, n_blk) - 3):
        for gi in range(n_grp):
            best = jnp.max(imps[gi], axis=0, keepdims=True)
            first = jnp.min(jnp.where(imps[gi] == best, blk, n_blk), axis=0, keepdims=True)
            imps[gi] = jnp.where(blk == first, -jnp.inf, imps[gi])
    for gi in range(n_grp):
        sel_out[0, gi, 0:n_live, :] = jnp.where(imps[gi] == -jnp.inf, 0.0, SEL_OFF).astype(BF)
        if n_live < n_blk:
            sel_out[0, gi, n_live:n_blk, :] = jnp.full((n_blk - n_live, tq), SEL_OFF, BF)


def _cmp_select_kernel(q_ref, kc_ref, vc_ref, ov_ref, gate_ref, o_out, sel_out, *, tq, n_blk, n_grp):
    q0 = pl.program_id(2) * tq
    n_chunk = kc_ref.shape[2]
    n_need = jnp.minimum((q0 + tq - NSA_CMP_LEN) // NSA_CMP_STRIDE + 1, n_chunk - 1)
    n_steps = n_chunk // CMP_BLOCK
    need_steps = (n_need + CMP_BLOCK - 1) // CMP_BLOCK
    for k in range(1, n_steps + 1):
        @pl.when(need_steps == k)
        def _(k=k):
            _cmp_branch(k * CMP_BLOCK, q_ref, kc_ref, vc_ref, ov_ref, gate_ref, o_out, sel_out,
                        tq=tq, n_blk=n_blk, n_grp=n_grp)


def _overlap_matrix(s):
    n_chunk = s // NSA_CMP_STRIDE
    n_blk = s // NSA_SLC_LEN
    c0 = np.arange(n_chunk) * NSA_CMP_STRIDE
    c1 = c0 + NSA_CMP_LEN - 1
    b0 = np.arange(n_blk) * NSA_SLC_LEN
    ov = np.minimum(c1[None, :], b0[:, None] + NSA_SLC_LEN - 1) - np.maximum(c0[None, :], b0[:, None]) + 1
    return jnp.asarray(np.clip(ov, 0, None) / NSA_CMP_LEN, BF)


def _cmp_select(qT, k_cmp, v_cmpT, gates5, *, tq):
    b, _, s = qT.shape
    g = NSA_KV_HEADS
    r_heads = N_HEADS // g
    n_chunk = k_cmp.shape[2]
    n_blk = s // NSA_SLC_LEN
    ov = _overlap_matrix(s)
    n_grp = CMP_GROUPS
    rows = n_grp * r_heads * HEAD_DIM
    return pl.pallas_call(
        functools.partial(_cmp_select_kernel, tq=tq, n_blk=n_blk, n_grp=n_grp),
        grid=(b, g // n_grp, s // tq),
        in_specs=[
            pl.BlockSpec((1, rows, tq), lambda i, j, k: (i, j, k)),
            pl.BlockSpec((1, n_grp, n_chunk, HEAD_DIM), lambda i, j, k: (i, j, 0, 0)),
            pl.BlockSpec((1, n_grp, HEAD_DIM, n_chunk), lambda i, j, k: (i, j, 0, 0)),
            pl.BlockSpec((n_blk, n_chunk), lambda i, j, k: (0, 0)),
            pl.BlockSpec((1, 1, n_grp, r_heads, tq), lambda i, j, k: (i, 0, j, 0, k)),
        ],
        out_specs=[pl.BlockSpec((1, tq, rows), lambda i, j, k: (i, k, j)),
                   pl.BlockSpec((1, n_grp, n_blk, tq), lambda i, j, k: (i, j, 0, k))],
        out_shape=[jax.ShapeDtypeStruct((b, s, MIX_WIDTH), BF),
                   jax.ShapeDtypeStruct((b, g, n_blk, s), BF)],
        compiler_params=_params(("parallel", "parallel", "parallel")),
        name="nsa_cmp_select",
    )(qT, k_cmp, v_cmpT, ov, gates5)


def _flash_kernel(*refs, mode, n_par, n_rep, dq, kc, tq, tk, cw, window, n_blk, has_sink, has_gate,
                  fuse_out=None):
    it = iter(refs)
    q_ref, k_ref, v_ref = next(it), next(it), next(it)
    sel_ref = next(it) if n_blk else None
    sink_ref = next(it) if has_sink else None
    gate_ref = next(it) if has_gate else None
    qnext_ref = next(it) if mode == "causal" else None
    selnext_ref = next(it) if (mode == "causal" and n_blk) else None
    if fuse_out is None:
        out_ref = next(it)
    else:
        n_extra, has_next = fuse_out
        extra_refs = [next(it) for _ in range(n_extra)]
        z_ref, x_ref, w_ref = next(it), next(it), next(it)
        ng_ref = next(it) if has_next else None
        x_out = next(it)
        h_out = next(it) if has_next else None
    qs_ref, m_ref, acc_ref, mt_ref = (next(it) for _ in range(4))
    qn_ref = next(it) if mode == "causal" else None
    bias_ref = next(it) if mode == "window" else None
    s_bufs = tuple(it)

    nq = n_rep * tq
    items = [(p, c) for p in range(n_par) for c in range(nq // cw)]
    n_items = len(items)
    grp = pl.program_id(1)
    q0 = pl.program_id(2) * tq
    acc_row = lax.broadcasted_iota(jnp.int32, (V_ROWS, nq), 0)
    for p in range(n_par):
        for r in range(n_rep):
            hd = p * n_rep + r
            qs_ref[p, 0:dq, r * tq:(r + 1) * tq] = q_ref[0, hd * dq:(hd + 1) * dq, :]
        if n_blk:
            qs_ref[p, dq:dq + n_blk, :] = jnp.concatenate([sel_ref[0, p]] * n_rep, axis=1)
            if dq + n_blk < kc:
                qs_ref[p, dq + n_blk:kc, :] = jnp.zeros((kc - dq - n_blk, nq), BF)
        if has_sink:
            m_ref[p] = jnp.concatenate(
                [jnp.full((1, tq), sink_ref[(grp * n_par + p) * n_rep + r] * LOG2E, F32) for r in range(n_rep)],
                axis=1)
            acc_ref[p] = jnp.where(acc_row == HEAD_DIM, 1.0, 0.0)
        else:
            m_ref[p] = jnp.full((1, nq), M_INIT, F32)
            acc_ref[p] = jnp.zeros((V_ROWS, nq), F32)

    t_row = q0 + (lax.broadcasted_iota(jnp.int32, (1, nq), 1) & (tq - 1))
    one_row = jnp.where(lax.broadcasted_iota(jnp.int32, (V_ROWS - HEAD_DIM, tk), 0) == 0, 1.0, 0.0).astype(BF)

    def stage_a(item, j, key0, slot, causal, bias=None, pen=None):
        p, c = item
        cols = slice(c * cw, (c + 1) * cw)
        s = _dot(k_ref[0, p, j][:, :kc], qs_ref[p, :, cols])
        if causal:
            key = key0 + lax.broadcasted_iota(jnp.int32, (tk, 1), 0)
            s = jnp.where(key <= t_row[:, cols], s, NEG_INF)
        if bias is not None:
            s = s + bias_ref[bias]
        s_bufs[slot][...] = s
        mt = jnp.max(s, axis=0, keepdims=True)
        mt_ref[slot] = mt if pen is None else mt - pen

    def stage_b(item, j, slot, pen=None):
        p, c = item
        cols = slice(c * cw, (c + 1) * cw)
        m_old = m_ref[p, :, cols]
        m_new = jnp.maximum(m_old, mt_ref[slot])
        pr = jnp.exp2(s_bufs[slot][...] - (m_new if pen is None else m_new + pen)).astype(BF)
        alpha = jnp.exp2(m_old - m_new)
        v = jnp.concatenate([v_ref[0, j, p * HEAD_DIM:(p + 1) * HEAD_DIM, :], one_row], axis=0)
        acc_ref[p, :, cols] = alpha * acc_ref[p, :, cols] + _dot(v, pr)
        m_ref[p, :, cols] = m_new

    def finalize():
        outs = []
        for p in range(n_par):
            acc = acc_ref[p]
            o = acc[:HEAD_DIM] * (1.0 / acc[HEAD_DIM:HEAD_DIM + 1])
            if has_gate:
                o = o * _gate_row(gate_ref, n_rep, p)
            outs += [o[:, r * tq:(r + 1) * tq] for r in range(n_rep)]
        o_tok = jnp.concatenate(outs, axis=0).T
        if fuse_out is None:
            out_ref[0] = o_tok.astype(out_ref.dtype)
            return
        for r in extra_refs:
            o_tok = o_tok + r[0].astype(F32)
        y = _dot((o_tok * z_ref[0].astype(F32)).astype(BF), w_ref[...])
        x_new = x_ref[0] + y
        x_out[0] = x_new
        if has_next:
            h_out[0] = _rms_rows(x_new, ng_ref[...]).astype(BF)

    if mode == "causal":
        assert n_items % 2 == 0
        n_full = q0 // tk

        def prefetch_next_tile():
            assert cw == tq
            qn_ref[0:dq, :] = qnext_ref[0, 0:dq, :]
            if n_blk:
                qn_ref[dq:dq + n_blk, :] = selnext_ref[0, 0]
                if dq + n_blk < kc:
                    qn_ref[dq + n_blk:kc, :] = jnp.zeros((kc - dq - n_blk, tq), BF)
            s = _dot(k_ref[0, 0, 0][:, :kc], qn_ref[...])
            key = lax.broadcasted_iota(jnp.int32, (tk, 1), 0)
            s = jnp.where(key <= t_row[:, 0:cw] + tq, s, NEG_INF)
            s_bufs[0][...] = s
            mt_ref[0] = jnp.max(s, axis=0, keepdims=True)

        def step(j, kind, next_kind, last):
            for idx, item in enumerate(items):
                slot = idx % 2
                if idx + 1 < n_items:
                    stage_a(items[idx + 1], j, j * tk, 1 - slot, kind)
                elif not last:
                    stage_a(items[0], j + 1, (j + 1) * tk, 1 - slot, next_kind)
                else:
                    prefetch_next_tile()
                stage_b(item, j, slot)

        @pl.when(pl.program_id(2) == 0)
        def _():
            stage_a(items[0], 0, 0, 0, "causal")

        def body(j, carry):
            step(j, None, None, False)
            return carry

        def body_group(i, carry):
            for u in range(FLASH_UNROLL):
                step(FLASH_UNROLL * i + u, None, None, False)
            return carry

        n_main = jnp.maximum(n_full - 1, 0)
        n_groups = lax.shift_right_logical(n_main, int(np.log2(FLASH_UNROLL)))
        lax.fori_loop(0, n_groups, body_group, 0)
        lax.fori_loop(FLASH_UNROLL * n_groups, n_main, body, 0)

        @pl.when(n_full >= 1)
        def _():
            step(n_full - 1, None, "causal", False)
            step(n_full, "causal", None, True)
            finalize()

        @pl.when(n_full == 0)
        def _():
            step(0, "causal", None, True)
            finalize()
    else:
        w_tiles, q_tiles = window // tk, tq // tk
        assert cw % tq == 0
        t_loc = lax.broadcasted_iota(jnp.int32, (1, cw), 1) & (tq - 1)
        key_loc = lax.broadcasted_iota(jnp.int32, (tk, 1), 0)
        masked = [i for i in range(w_tiles + q_tiles) if i < q_tiles or i >= w_tiles]
        for n, i in enumerate(masked):
            ok = (i * tk + key_loc > t_loc) & (i * tk + key_loc - window <= t_loc)
            bias_ref[n] = jnp.where(ok, 0.0, NEG_INF)
        work = []
        for i in range(w_tiles + q_tiles):
            jv = q0 // tk - w_tiles + i
            bias = masked.index(i) if i in masked else None
            pen = jnp.where(jv < 0, -NEG_INF, 0.0) if i < w_tiles else None
            work += [(item, jnp.maximum(jv, 0), bias, pen) for item in items]
        n_buf = len(s_bufs)
        ahead = WINDOW_LOOKAHEAD
        for n in range(min(ahead, len(work))):
            stage_a(work[n][0], work[n][1], 0, n % n_buf, False, work[n][2], work[n][3])
        for n, (item, j, _, pen) in enumerate(work):
            if n + ahead < len(work):
                nxt = work[n + ahead]
                stage_a(nxt[0], nxt[1], 0, (n + ahead) % n_buf, False, nxt[2], nxt[3])
            stage_b(item, j, n % n_buf, pen)

        finalize()


def _flash(qT, k_tok, vT_tiles, *, mode, n_par, n_rep, dq, kc, tq, tk, cw, window=None,
           sel_bias=None, sinks=None, gates=None, gate_branch=0, out_proj=None, single_buffer_v=False,
           name="flash"):
    b, _, s = qT.shape
    kh, k_lanes = k_tok.shape[1], k_tok.shape[3]
    n_t = s // tk
    k5 = k_tok.reshape(b, kh, n_t, tk, k_lanes)
    n_grp = kh // n_par
    heads = n_par * n_rep
    nq = n_rep * tq
    n_blk = sel_bias.shape[2] if sel_bias is not None else 0
    has_sink, has_gate = sinks is not None, gates is not None
    args = [qT, k5, vT_tiles]
    in_specs = [
        pl.BlockSpec((1, heads * dq, tq), lambda i, j, k: (i, j, k)),
        pl.BlockSpec((1, n_par, n_t, tk, k_lanes), lambda i, j, k: (i, j, 0, 0, 0)),
        pl.BlockSpec((1, n_t, n_par * HEAD_DIM, tk), lambda i, j, k: (i, 0, j, 0),
                     **({"pipeline_mode": pl.Buffered(1)} if single_buffer_v else {})),
    ]
    if n_blk:
        assert dq + n_blk <= kc
        args.append(sel_bias)
        in_specs.append(pl.BlockSpec((1, n_par, n_blk, tq), lambda i, j, k: (i, j, 0, k)))
    if has_sink:
        args.append(sinks.astype(F32))
        in_specs.append(pl.BlockSpec(memory_space=pltpu.SMEM))
    if has_gate:
        args.append(gates)
        in_specs.append(pl.BlockSpec((1, 1, n_par, n_rep, tq),
                                     lambda i, j, k, _br=gate_branch: (i, _br, j, 0, k)))
    n_sbuf = 2 if mode == "causal" else WINDOW_SCORE_BUFS
    scratch = [pltpu.VMEM((n_par, kc, nq), BF), pltpu.VMEM((n_par, 1, nq), F32),
               pltpu.VMEM((n_par, V_ROWS, nq), F32), pltpu.VMEM((n_sbuf, 1, cw), F32)]
    if mode == "causal":
        last_q = s // tq - 1
        args.append(qT)
        in_specs.append(pl.BlockSpec((1, dq, tq), lambda i, j, k: (i, j * heads, jnp.minimum(k + 1, last_q))))
        if n_blk:
            args.append(sel_bias)
            in_specs.append(pl.BlockSpec((1, 1, n_blk, tq),
                                         lambda i, j, k: (i, j * n_par, 0, jnp.minimum(k + 1, last_q))))
        scratch.append(pltpu.VMEM((kc, tq), BF))
    else:
        w_t, q_t = window // tk, tq // tk
        scratch.append(pltpu.VMEM((min(2 * q_t, w_t + q_t), tk, cw), F32))
    scratch += [pltpu.VMEM((tk, cw), F32)] * n_sbuf
    out_specs = pl.BlockSpec((1, tq, heads * HEAD_DIM), lambda i, j, k: (i, k, j))
    out_shape = jax.ShapeDtypeStruct((b, s, MIX_WIDTH), BF)
    fuse_out = None
    if out_proj is not None:
        assert mode == "window" and heads * HEAD_DIM == MIX_WIDTH and n_grp == 1
        extra, zs, x, w_out, next_gain = out_proj
        tok = pl.BlockSpec((1, tq, D_MODEL), lambda i, j, k: (i, k, 0))
        args += list(extra) + [zs, x, w_out.astype(BF)]
        in_specs += [tok] * (len(extra) + 2) + [pl.BlockSpec(w_out.shape, lambda i, j, k: (0, 0))]
        out_specs, out_shape = [tok], [jax.ShapeDtypeStruct(x.shape, F32)]
        if next_gain is not None:
            args.append(next_gain.reshape(1, D_MODEL))
            in_specs.append(pl.BlockSpec((1, D_MODEL), lambda i, j, k: (0, 0)))
            out_specs.append(tok)
            out_shape.append(jax.ShapeDtypeStruct(x.shape, BF))
        fuse_out = (len(extra), next_gain is not None)
    kern = functools.partial(_flash_kernel, mode=mode, n_par=n_par, n_rep=n_rep, dq=dq, kc=kc, tq=tq, tk=tk,
                             cw=cw, window=window, n_blk=n_blk, has_sink=has_sink, has_gate=has_gate,
                             fuse_out=fuse_out)
    res = pl.pallas_call(
        kern,
        grid=(b, n_grp, s // tq),
        in_specs=in_specs,
        out_specs=out_specs,
        out_shape=out_shape,
        scratch_shapes=scratch,
        compiler_params=_params(("parallel", "parallel", "arbitrary")),
        name=name,
    )(*args)
    if out_proj is None:
        return res
    return (res[0], res[1]) if next_gain is not None else (res[0], None)


def _swa_proj_kernel(h_ref, cos_ref, sin_ref, wq_ref, wk_ref, wv_ref, wz_ref, qg_ref, kg_ref,
                     q_out, k_out, v_out, z_out, y_ref, *, win_tile):
    h = h_ref[0]
    cos, sin = cos_ref[0], sin_ref[0]
    y_ref[...] = _dot_nt(wq_ref[...], h)
    for hd in range(N_HEADS):
        rows = slice(hd * HEAD_DIM, (hd + 1) * HEAD_DIM)
        q_out[0, rows, :] = (_rope_t(_head_rms_t(y_ref[rows, :], qg_ref[...]), cos, sin) * Q_SCALE).astype(BF)
    kvd = SWA_KV_HEADS * HEAD_DIM
    y_ref[0:kvd, :] = _dot_nt(wk_ref[...], h)
    for gi in range(SWA_KV_HEADS):
        rows = slice(gi * HEAD_DIM, (gi + 1) * HEAD_DIM)
        k_out[0, gi] = _to_token_major(_rope_t(_head_rms_t(y_ref[rows, :], kg_ref[...]), cos, sin)).astype(BF)
    _store_lane_tiles(v_out, _dot_nt(wv_ref[...], h).astype(BF), win_tile)
    z_out[0] = _silu(_dot(h, wz_ref[...])).astype(BF)


def _swa_proj(h, cos, sin, w_in, q_gain, k_gain, *, win_tile):
    b, s, d = h.shape
    g = SWA_KV_HEADS
    kvd = g * HEAD_DIM
    tm = min(TOK_TILE, s)
    wt = w_in.T.astype(BF)
    wq, wk, wv = wt[:MIX_WIDTH], wt[MIX_WIDTH:MIX_WIDTH + kvd], wt[MIX_WIDTH + kvd:MIX_WIDTH + 2 * kvd]
    wz = w_in[:, MIX_WIDTH + 2 * kvd:].astype(BF)
    qg, kg = q_gain.reshape(HEAD_DIM, 1), k_gain.reshape(HEAD_DIM, 1)
    wpt = tm // win_tile
    half = HEAD_DIM // 2
    return pl.pallas_call(
        functools.partial(_swa_proj_kernel, win_tile=win_tile),
        grid=(b, s // tm),
        in_specs=[pl.BlockSpec((1, tm, d), lambda i, j: (i, j, 0)),
                  pl.BlockSpec((1, half, tm), lambda i, j: (i, 0, j)),
                  pl.BlockSpec((1, half, tm), lambda i, j: (i, 0, j)),
                  _full(wq.shape), _full(wk.shape), _full(wv.shape), _full(wz.shape),
                  _full(qg.shape), _full(kg.shape)],
        out_specs=[pl.BlockSpec((1, MIX_WIDTH, tm), lambda i, j: (i, 0, j)),
                   pl.BlockSpec((1, g, tm, KEY_PAD), lambda i, j: (i, 0, j, 0)),
                   pl.BlockSpec((1, wpt, kvd, win_tile), lambda i, j: (i, j, 0, 0)),
                   pl.BlockSpec((1, tm, MIX_WIDTH), lambda i, j: (i, j, 0))],
        out_shape=[jax.ShapeDtypeStruct((b, MIX_WIDTH, s), BF),
                   jax.ShapeDtypeStruct((b, g, s, KEY_PAD), BF),
                   jax.ShapeDtypeStruct((b, s // win_tile, kvd, win_tile), BF),
                   jax.ShapeDtypeStruct((b, s, MIX_WIDTH), BF)],
        scratch_shapes=[pltpu.VMEM((MIX_WIDTH, tm), F32)],
        compiler_params=_params(("parallel", "parallel")),
        name="swa_proj",
    )(h, cos, sin, wq, wk, wv, wz, qg, kg)


def _split3(x):
    hi = x.astype(BF)
    r1 = x - hi.astype(F32)
    mid = r1.astype(BF)
    lo = (r1 - mid.astype(F32)).astype(BF)
    return hi, mid, lo


def _fox_proj_kernel(h_ref, wf_ref, bias_ref, tri_ref, wq_ref, wk_ref, wv_ref, wz_ref, qg_ref, kg_ref,
                     q_out, k_out, v_out, z_out, yq_ref, yk_ref, cum_ref, carry_ref, *, tile):
    @pl.when(pl.program_id(1) == 0)
    def _():
        carry_ref[...] = jnp.zeros_like(carry_ref)

    h = h_ref[0]
    tm = h.shape[0]
    x = _dot_nt(wf_ref[...], h) + bias_ref[...]
    logf = jnp.minimum(x, 0.0) - jnp.log(1.0 + jnp.exp(-jnp.abs(x)))
    tri = tri_ref[...]
    hi, mid, lo = _split3(logf)
    cum = (_dot(hi, tri) + _dot(mid, tri)) + _dot(lo, tri) + carry_ref[:, 0:1]
    carry_ref[...] = jnp.broadcast_to(cum[:, -1:], carry_ref.shape)
    cum_ref[...] = cum * LOG2E

    yq_ref[...] = _dot_nt(wq_ref[...], h)
    yk_ref[...] = _dot_nt(wk_ref[...], h)
    row = lax.broadcasted_iota(jnp.int32, (8, tm), 0)
    zeros = jnp.zeros((KEY_PAD - HEAD_DIM - 16, tm), F32)
    for hd in range(N_HEADS):
        rows = slice(hd * HEAD_DIM, (hd + 1) * HEAD_DIM)
        c_hi, c_mid, c_lo = (c.astype(F32) for c in _split3(cum_ref[hd:hd + 1, :]))
        c3 = jnp.where(row == 0, c_hi, jnp.where(row == 1, c_mid, jnp.where(row == 2, c_lo, 0.0)))
        one3 = jnp.where(row < 3, 1.0, 0.0)
        q = _head_rms_t(yq_ref[rows, :], qg_ref[...]) * Q_SCALE
        q_out[0, hd] = jnp.concatenate([q, one3, c3, zeros], axis=0).astype(BF)
        k = _head_rms_t(yk_ref[rows, :], kg_ref[...])
        k_out[0, hd] = jnp.concatenate([k, -c3, one3, zeros], axis=0).T.astype(BF)
    _store_lane_tiles(v_out, _dot_nt(wv_ref[...], h).astype(BF), tile)
    z_out[0] = _silu(_dot(h, wz_ref[...])).astype(BF)


def _fox_proj(h, w_in, forget_bias, q_gain, k_gain, *, tile):
    b, s, d = h.shape
    tm = min(TOK_TILE, s)
    wt = w_in.T.astype(BF)
    wq, wk, wv = wt[:MIX_WIDTH], wt[MIX_WIDTH:2 * MIX_WIDTH], wt[2 * MIX_WIDTH:3 * MIX_WIDTH]
    wf = wt[3 * MIX_WIDTH:3 * MIX_WIDTH + N_HEADS]
    wz = w_in[:, 3 * MIX_WIDTH + N_HEADS:].astype(BF)
    qg, kg = q_gain.reshape(HEAD_DIM, 1), k_gain.reshape(HEAD_DIM, 1)
    tri = jnp.asarray(np.arange(tm)[:, None] <= np.arange(tm)[None, :], BF)
    return pl.pallas_call(
        functools.partial(_fox_proj_kernel, tile=tile),
        grid=(b, s // tm),
        in_specs=[pl.BlockSpec((1, tm, d), lambda i, j: (i, j, 0)),
                  _full(wf.shape), _full((N_HEADS, 1)), _full(tri.shape),
                  _full(wq.shape), _full(wk.shape), _full(wv.shape), _full(wz.shape),
                  _full(qg.shape), _full(kg.shape)],
        out_specs=[pl.BlockSpec((1, N_HEADS, KEY_PAD, tm), lambda i, j: (i, 0, 0, j)),
                   pl.BlockSpec((1, N_HEADS, tm, KEY_PAD), lambda i, j: (i, 0, j, 0)),
                   pl.BlockSpec((1, tm // tile, MIX_WIDTH, tile), lambda i, j: (i, j, 0, 0)),
                   pl.BlockSpec((1, tm, MIX_WIDTH), lambda i, j: (i, j, 0))],
        out_shape=[jax.ShapeDtypeStruct((b, N_HEADS, KEY_PAD, s), BF),
                   jax.ShapeDtypeStruct((b, N_HEADS, s, KEY_PAD), BF),
                   jax.ShapeDtypeStruct((b, s // tile, MIX_WIDTH, tile), BF),
                   jax.ShapeDtypeStruct((b, s, MIX_WIDTH), BF)],
        scratch_shapes=[pltpu.VMEM((MIX_WIDTH, tm), F32), pltpu.VMEM((MIX_WIDTH, tm), F32),
                        pltpu.VMEM((N_HEADS, tm), F32), pltpu.VMEM((N_HEADS, LANES), F32)],
        compiler_params=_params(("parallel", "arbitrary")),
        name="fox_proj",
    )(h, wf, forget_bias.reshape(N_HEADS, 1).astype(F32), tri, wq, wk, wv, wz, qg, kg)


def _out_proj_kernel(*refs, n_o, has_next):
    o_refs = refs[:n_o]
    z_ref, x_ref, w_ref = refs[n_o:n_o + 3]
    rest = refs[n_o + 3:]
    if has_next:
        g_ref, x_out, h_out = rest
    else:
        (x_out,) = rest
    o = o_refs[0][0].astype(F32)
    for r in o_refs[1:]:
        o = o + r[0].astype(F32)
    y = _dot((o * z_ref[0].astype(F32)).astype(BF), w_ref[...])
    x_new = x_ref[0] + y
    x_out[0] = x_new
    if has_next:
        h_out[0] = _rms_rows(x_new, g_ref[...]).astype(BF)


def _out_proj(o_list, zs, x, w_out, next_gain):
    b, s, d = x.shape
    tm = min(TOK_TILE, s)
    has_next = next_gain is not None
    blk = pl.BlockSpec((1, tm, d), lambda i, j: (i, j, 0))
    args = list(o_list) + [zs, x, w_out.astype(BF)]
    in_specs = [blk] * (len(o_list) + 2) + [_full(w_out.shape)]
    out_shape = [jax.ShapeDtypeStruct((b, s, d), F32)]
    out_specs = [blk]
    if has_next:
        args.append(next_gain.reshape(1, d))
        in_specs.append(_full((1, d)))
        out_shape.append(jax.ShapeDtypeStruct((b, s, d), BF))
        out_specs.append(blk)
    res = pl.pallas_call(
        functools.partial(_out_proj_kernel, n_o=len(o_list), has_next=has_next),
        grid=(b, s // tm), in_specs=in_specs, out_specs=out_specs, out_shape=out_shape,
        compiler_params=_params(("parallel", "parallel")),
        name="out_proj",
    )(*args)
    return (res[0], res[1]) if has_next else (res[0], None)


ITEM_LANES = 512
CMP_Q_TILE = 512
CMP_GROUPS = 2
NSA_SLC_Q_TILE = 512
NSA_SLC_TILE = 512
NSA_SLC_GROUPS = 4
NSA_WIN_Q_TILE = 256
NSA_WIN_TILE = 256
SWA_TILE = 128
SWA_Q_TILE = 256
FOX_Q_TILE = 512
FOX_K_TILE = 512
FOX_HEADS_PER_STEP = 8


def _nsa_mixer(x, h, cos, sin, w_in, q_gain, k_gain, cmp_pos, cmp_w1, cmp_w2, w_out, next_gain, pre_gain):
    b, s, _ = x.shape
    g = NSA_KV_HEADS
    r = N_HEADS // g
    qT, kc_tok, vc_tok, ks, kw, vsT, vwT, gates, zs = _nsa_proj(
        x if h is None else h, cos, sin, w_in, q_gain, k_gain, slc_tile=NSA_SLC_TILE, win_tile=NSA_WIN_TILE,
        pre_gain=pre_gain if h is None else None)
    k_cmp, v_cmpT = _compress(kc_tok, vc_tok, cmp_pos, cmp_w1, cmp_w2, k_gain[0])
    gates5 = gates.reshape(b, 3, g, r, s)
    o_cmp, sel_bias = _cmp_select(qT, k_cmp, v_cmpT, gates5, tq=CMP_Q_TILE)
    o_slc = _flash(qT, ks, vsT, mode="causal", n_par=NSA_SLC_GROUPS, n_rep=r, dq=HEAD_DIM, kc=SLC_KEY_LANES,
                   tq=NSA_SLC_Q_TILE, tk=NSA_SLC_TILE, cw=ITEM_LANES, sel_bias=sel_bias,
                   gates=gates5, gate_branch=1, single_buffer_v=True, name="nsa_selected")
    return _flash(qT, kw, vwT, mode="window", n_par=g, n_rep=r, dq=HEAD_DIM, kc=HEAD_DIM,
                  tq=NSA_WIN_Q_TILE, tk=NSA_WIN_TILE, cw=ITEM_LANES, window=NSA_WINDOW,
                  gates=gates5, gate_branch=2, out_proj=([o_cmp, o_slc], zs, x, w_out, next_gain),
                  name="nsa_window_out")


def _swa_mixer(x, h, cos, sin, w_in, q_gain, k_gain, sinks, w_out, next_gain):
    r = N_HEADS // SWA_KV_HEADS
    qT, k, vT, zs = _swa_proj(h, cos, sin, w_in, q_gain, k_gain, win_tile=SWA_TILE)
    return _flash(qT, k, vT, mode="window", n_par=SWA_KV_HEADS, n_rep=r, dq=HEAD_DIM, kc=HEAD_DIM, tq=SWA_Q_TILE,
                  tk=SWA_TILE, cw=r * SWA_TILE, window=SWA_WINDOW, sinks=sinks,
                  out_proj=([], zs, x, w_out, next_gain), name="swa_window_out")


def _fox_mixer(x, h, w_in, forget_bias, q_gain, k_gain, w_out, next_gain):
    b, s, _ = h.shape
    qT, k, vT, zs = _fox_proj(h, w_in, forget_bias, q_gain, k_gain, tile=FOX_K_TILE)
    o = _flash(qT.reshape(b, N_HEADS * KEY_PAD, s), k, vT, mode="causal", n_par=FOX_HEADS_PER_STEP, n_rep=1, dq=KEY_PAD,
               kc=KEY_PAD, tq=FOX_Q_TILE, tk=FOX_K_TILE, cw=ITEM_LANES, single_buffer_v=True,
               name="fox_attention")
    return _out_proj([o], zs, x, w_out, next_gain)


def kernel(x, positions, norm_gains, a_w_in, a_q_gain, a_k_gain, a_cmp_pos, a_cmp_w1, a_cmp_w2, a_w_out,
           b_w_in, b_q_gain, b_k_gain, b_sinks, b_w_out,
           c_w_in, c_forget_bias, c_q_gain, c_k_gain, c_w_out):
    depth = norm_gains.shape[0]
    cos, sin = _rope_tables(positions)
    h = None
    for i in range(depth):
        j, mixer = divmod(i, 3)
        next_gain = norm_gains[i + 1] if i + 1 < depth else None
        if mixer == 0:
            x, h = _nsa_mixer(x, h, cos, sin, a_w_in[j], a_q_gain[j], a_k_gain[j],
                              a_cmp_pos[j], a_cmp_w1[j], a_cmp_w2[j], a_w_out[j], next_gain, norm_gains[i])
        elif mixer == 1:
            x, h = _swa_mixer(x, h, cos, sin, b_w_in[j], b_q_gain[j], b_k_gain[j], b_sinks[j],
                              b_w_out[j], next_gain)
        else:
            x, h = _fox_mixer(x, h, c_w_in[j], c_forget_bias[j], c_q_gain[j], c_k_gain[j],
                              c_w_out[j], next_gain)
    return x
```
